```python
import jax, jax.numpy as jnp
from jax import lax
import numpy as np

D_MODEL = 1024
BATCH = 8
SEQ = 8192
DEPTH = 1

N_MEM = 256
EPS = 1e-6
MIX_WIDTH = D_MODEL
ATTN_WIDTH = MIX_WIDTH // 2
ATTN_HEAD_DIM = 64
ATTN_Q_HEADS = ATTN_WIDTH // ATTN_HEAD_DIM
ATTN_KV_HEADS = ATTN_Q_HEADS // 4
ATTN_KV_WIDTH = ATTN_KV_HEADS * ATTN_HEAD_DIM
WINDOW = 128
BLOCK = 128
HGRN_WIDTH = MIX_WIDTH - ATTN_WIDTH
HGRN_VAL_DIM = 128
HGRN_HEADS = HGRN_WIDTH // HGRN_VAL_DIM
HGRN_KEY_DIM = 128
HGRN_FDIM = HGRN_HEADS * HGRN_KEY_DIM
CHUNK = 64
IN_SPLITS = (ATTN_WIDTH, ATTN_KV_WIDTH, ATTN_KV_WIDTH, HGRN_FDIM, HGRN_FDIM, HGRN_WIDTH, HGRN_WIDTH)
IN_PROJ_WIDTH = sum(IN_SPLITS)
CA_HEADS = 4
CA_HEAD_DIM = D_MODEL // CA_HEADS
CA_WIDTH = CA_HEADS * CA_HEAD_DIM
D_FF = 2816
CONV_WIDTH = 3

kernel_name = "hybrid_swa_sink_hgrn2_memxattn_convffn"


def rms_norm(x, w):
    xf = x.astype(jnp.float32)
    y = xf * lax.rsqrt(jnp.mean(xf * xf, axis=-1, keepdims=True) + EPS)
    return (y * w.astype(jnp.float32)).astype(x.dtype)


def sliding_window_sink_attention(q, k, v, sinks):
    B, T, Hq, D = q.shape
    Hkv = k.shape[2]
    G = Hq // Hkv
    nb = T // BLOCK
    qb = q.reshape(B, nb, BLOCK, Hkv, G, D)

    def with_prev(t):
        tb = t.reshape(B, nb, BLOCK, Hkv, D)
        prev = jnp.pad(tb, ((0, 0), (1, 0), (0, 0), (0, 0), (0, 0)))[:, :-1]
        return jnp.concatenate([prev, tb], axis=2)

    kw, vw = with_prev(k), with_prev(v)
    s = jnp.einsum('bnqhgd,bnkhd->bnhgqk', qb, kw).astype(jnp.float32) * (D ** -0.5)
    qi = jnp.arange(BLOCK)[:, None]
    kj = jnp.arange(2 * BLOCK)[None, :]
    diff = qi + BLOCK - kj
    key_pos = jnp.arange(nb)[:, None, None] * BLOCK - BLOCK + kj[None]
    allowed = (diff >= 0) & (diff < WINDOW) & (key_pos >= 0)
    s = jnp.where(allowed[None, :, None, None], s, -jnp.inf)
    sink = sinks.astype(jnp.float32).reshape(Hkv, G)[None, None, :, :, None, None]
    sink = jnp.broadcast_to(sink, s.shape[:-1] + (1,))
    p = jax.nn.softmax(jnp.concatenate([s, sink], axis=-1), axis=-1)[..., :-1]
    o = jnp.einsum('bnhgqk,bnkhd->bnqhgd', p.astype(v.dtype), vw)
    return o.reshape(B, T, Hq * D)


def hgrn2_chunkwise(q, k, v, log_f):
    B, T, H, K = q.shape
    V = v.shape[-1]
    n = T // CHUNK

    def to_chunks(t):
        return t.reshape(B, n, CHUNK, H, t.shape[-1]).transpose(1, 0, 3, 2, 4)

    causal = jnp.tril(jnp.ones((CHUNK, CHUNK), dtype=bool))

    def step(S, xs):
        qc, kc, vc, gc = xs
        bc = jnp.cumsum(gc, axis=2)
        rel = bc[:, :, :, None, :] - bc[:, :, None, :, :]
        decay = jnp.exp(jnp.where(causal[:, :, None], rel, -jnp.inf))
        A = jnp.einsum('bhtk,bhsk,bhtsk->bhts', qc, kc, decay)
        o = jnp.einsum('bhts,bhsv->bhtv', A, vc) + jnp.einsum('bhtk,bhkv->bhtv', qc * jnp.exp(bc), S)
        b_last = bc[:, :, -1:, :]
        S = S * jnp.exp(b_last[:, :, 0, :])[..., None] + jnp.einsum(
            'bhsk,bhsv->bhkv', kc * jnp.exp(b_last - bc), vc)
        return S, o

    S0 = jnp.zeros((B, H, K, V), jnp.float32)
    _, o = lax.scan(step, S0, (to_chunks(q), to_chunks(k), to_chunks(v), to_chunks(log_f)))
    return o.transpose(1, 0, 3, 2, 4).reshape(B, T, H, V)


def hgrn2_group(q_raw, f_raw, i_raw, g_raw, lb, out_norm_w):
    B, T, _ = q_raw.shape
    f32 = jnp.float32
    q = jax.nn.silu(q_raw.astype(f32)).reshape(B, T, HGRN_HEADS, HGRN_KEY_DIM) * (HGRN_KEY_DIM ** -0.5)
    fr = f_raw.astype(f32)
    lb = lb.astype(f32)
    f = lb + (1.0 - lb) * jax.nn.sigmoid(fr)
    k = (1.0 - lb) * jax.nn.sigmoid(-fr)
    log_f = jnp.log(f)
    k = k.reshape(B, T, HGRN_HEADS, HGRN_KEY_DIM)
    log_f = log_f.reshape(B, T, HGRN_HEADS, HGRN_KEY_DIM)
    v = i_raw.astype(f32).reshape(B, T, HGRN_HEADS, HGRN_VAL_DIM)
    o = hgrn2_chunkwise(q, k, v, log_f)
    o = rms_norm(o, out_norm_w).reshape(B, T, HGRN_WIDTH)
    return (o * jax.nn.silu(g_raw.astype(f32))).astype(q_raw.dtype)


def memory_cross_attention(h, mem_n, wq, wk, wv, wo):
    B, T, _ = h.shape
    M = mem_n.shape[1]
    q = (h @ wq).reshape(B, T, CA_HEADS, CA_HEAD_DIM)
    k = (mem_n @ wk).reshape(B, M, CA_HEADS, CA_HEAD_DIM)
    v = (mem_n @ wv).reshape(B, M, CA_HEADS, CA_HEAD_DIM)
    s = jnp.einsum('bthd,bmhd->bhtm', q, k).astype(jnp.float32) * (CA_HEAD_DIM ** -0.5)
    p = jax.nn.softmax(s, axis=-1).astype(v.dtype)
    o = jnp.einsum('bhtm,bmhd->bthd', p, v).reshape(B, T, CA_WIDTH)
    return o @ wo


def conv_ffn(h, w_up, conv_w, conv_b, w_down):
    u = h @ w_up
    C = u.shape[-1]
    u = lax.conv_general_dilated(
        u, conv_w.reshape(CONV_WIDTH, 1, C).astype(u.dtype), window_strides=(1,),
        padding=[(CONV_WIDTH - 1, 0)], dimension_numbers=('NWC', 'WIO', 'NWC'),
        feature_group_count=C) + conv_b
    gate, val = jnp.split(u, 2, axis=-1)
    return (jax.nn.gelu(gate, approximate=True) * val) @ w_down


def _fwd_setup_inputs(seed: int = 0) -> dict:
    key = jax.random.key(seed)
    ks = jax.random.split(key, 24)
    f32 = jnp.float32

    def w(k, shape, fan_in):
        return jax.random.normal(k, shape, f32) * (fan_in ** -0.5)

    def gain(k, shape):
        return 1.0 + 0.01 * jax.random.normal(k, shape, f32)

    return {
        "x": jax.random.normal(ks[0], (BATCH, SEQ, D_MODEL), f32),
        "mem": jax.random.normal(ks[1], (BATCH, N_MEM, D_MODEL), f32),
        "mix_pre_norm": gain(ks[2], (DEPTH, D_MODEL)),
        "w_in": w(ks[3], (DEPTH, D_MODEL, IN_PROJ_WIDTH), D_MODEL),
        "attn_sinks": 0.5 * jax.random.normal(ks[4], (DEPTH, ATTN_Q_HEADS), f32),
        "hgrn_lb_logits": 0.1 * jax.random.normal(ks[5], (DEPTH + 1, HGRN_FDIM), f32),
        "hgrn_out_norm": gain(ks[6], (DEPTH, HGRN_VAL_DIM)),
        "w_out": w(ks[7], (DEPTH, MIX_WIDTH, D_MODEL), MIX_WIDTH),
        "mix_post_norm": gain(ks[8], (DEPTH, D_MODEL)),
        "ca_pre_norm": gain(ks[9], (DEPTH, D_MODEL)),
        "mem_norm": gain(ks[10], (DEPTH, D_MODEL)),
        "ca_wq": w(ks[11], (DEPTH, D_MODEL, CA_WIDTH), D_MODEL),
        "ca_wk": w(ks[12], (DEPTH, D_MODEL, CA_WIDTH), D_MODEL),
        "ca_wv": w(ks[13], (DEPTH, D_MODEL, CA_WIDTH), D_MODEL),
        "ca_wo": w(ks[14], (DEPTH, CA_WIDTH, D_MODEL), CA_WIDTH),
        "ca_post_norm": gain(ks[15], (DEPTH, D_MODEL)),
        "ffn_pre_norm": gain(ks[16], (DEPTH, D_MODEL)),
        "ffn_w_up": w(ks[17], (DEPTH, D_MODEL, 2 * D_FF), D_MODEL),
        "ffn_conv_w": w(ks[18], (DEPTH, CONV_WIDTH, 2 * D_FF), CONV_WIDTH),
        "ffn_conv_b": 0.01 * jax.random.normal(ks[19], (DEPTH, 2 * D_FF), f32),
        "ffn_w_down": w(ks[20], (DEPTH, D_FF, D_MODEL), D_FF),
        "ffn_post_norm": gain(ks[21], (DEPTH, D_MODEL)),
    }


def _fwd_reference(x, mem, mix_pre_norm, w_in, attn_sinks, hgrn_lb_logits, hgrn_out_norm, w_out,
              mix_post_norm, ca_pre_norm, mem_norm, ca_wq, ca_wk, ca_wv, ca_wo, ca_post_norm,
              ffn_pre_norm, ffn_w_up, ffn_conv_w, ffn_conv_b, ffn_w_down, ffn_post_norm):
    B, T, _ = x.shape
    lower_bounds = jnp.cumsum(jax.nn.softmax(hgrn_lb_logits.astype(jnp.float32), axis=0), axis=0)
    split_points = list(np.cumsum(IN_SPLITS)[:-1])
    for l in range(DEPTH):
        h = rms_norm(x, mix_pre_norm[l])
        z = h @ w_in[l]
        q_a, k_a, v_a, q_h, f_h, i_h, g_h = jnp.split(z, split_points, axis=-1)
        attn = sliding_window_sink_attention(
            q_a.reshape(B, T, ATTN_Q_HEADS, ATTN_HEAD_DIM),
            k_a.reshape(B, T, ATTN_KV_HEADS, ATTN_HEAD_DIM),
            v_a.reshape(B, T, ATTN_KV_HEADS, ATTN_HEAD_DIM),
            attn_sinks[l])
        rec = hgrn2_group(q_h, f_h, i_h, g_h, lower_bounds[l], hgrn_out_norm[l])
        m = jnp.concatenate([attn.astype(x.dtype), rec.astype(x.dtype)], axis=-1) @ w_out[l]
        x = x + rms_norm(m, mix_post_norm[l])
        h = rms_norm(x, ca_pre_norm[l])
        mem_n = rms_norm(mem, mem_norm[l])
        c = memory_cross_attention(h, mem_n, ca_wq[l], ca_wk[l], ca_wv[l], ca_wo[l])
        x = x + rms_norm(c, ca_post_norm[l])
        h = rms_norm(x, ffn_pre_norm[l])
        y = conv_ffn(h, ffn_w_up[l], ffn_conv_w[l], ffn_conv_b[l], ffn_w_down[l])
        x = x + rms_norm(y, ffn_post_norm[l])
    return x


import jax as _jax
import jax.numpy as _jnp

TWIN_FORMAT = 'train_step'
FWD_PARAMS = ['x', 'mem', 'mix_pre_norm', 'w_in', 'attn_sinks', 'hgrn_lb_logits', 'hgrn_out_norm', 'w_out', 'mix_post_norm', 'ca_pre_norm', 'mem_norm', 'ca_wq', 'ca_wk', 'ca_wv', 'ca_wo', 'ca_post_norm', 'ffn_pre_norm', 'ffn_w_up', 'ffn_conv_w', 'ffn_conv_b', 'ffn_w_down', 'ffn_post_norm']
TWIN_WEIGHTS = ['mix_pre_norm', 'w_in', 'attn_sinks', 'hgrn_lb_logits', 'hgrn_out_norm', 'w_out', 'mix_post_norm', 'ca_pre_norm', 'mem_norm', 'ca_wq', 'ca_wk', 'ca_wv', 'ca_wo', 'ca_post_norm', 'ffn_pre_norm', 'ffn_w_up', 'ffn_conv_w', 'ffn_conv_b', 'ffn_w_down', 'ffn_post_norm']
TWIN_DIFF_INPUT = 'x'
TWIN_INPUTS = ['x', 'mem', 'mix_pre_norm', 'w_in', 'attn_sinks', 'hgrn_lb_logits', 'hgrn_out_norm', 'w_out', 'mix_post_norm', 'ca_pre_norm', 'mem_norm', 'ca_wq', 'ca_wk', 'ca_wv', 'ca_wo', 'ca_post_norm', 'ffn_pre_norm', 'ffn_w_up', 'ffn_conv_w', 'ffn_conv_b', 'ffn_w_down', 'ffn_post_norm', 'loss_target', 'm_mix_pre_norm', 'm_w_in', 'm_attn_sinks', 'm_hgrn_lb_logits', 'm_hgrn_out_norm', 'm_w_out', 'm_mix_post_norm', 'm_ca_pre_norm', 'm_mem_norm', 'm_ca_wq', 'm_ca_wk', 'm_ca_wv', 'm_ca_wo', 'm_ca_post_norm', 'm_ffn_pre_norm', 'm_ffn_w_up', 'm_ffn_conv_w', 'm_ffn_conv_b', 'm_ffn_w_down', 'm_ffn_post_norm', 'v_mix_pre_norm', 'v_w_in', 'v_attn_sinks', 'v_hgrn_lb_logits', 'v_hgrn_out_norm', 'v_w_out', 'v_mix_post_norm', 'v_ca_pre_norm', 'v_mem_norm', 'v_ca_wq', 'v_ca_wk', 'v_ca_wv', 'v_ca_wo', 'v_ca_post_norm', 'v_ffn_pre_norm', 'v_ffn_w_up', 'v_ffn_conv_w', 'v_ffn_conv_b', 'v_ffn_w_down', 'v_ffn_post_norm']
TWIN_OUTPUTS = ['loss', 'grad_x', 'grad_mix_pre_norm', 'grad_w_in', 'grad_attn_sinks', 'grad_hgrn_lb_logits', 'grad_hgrn_out_norm', 'grad_w_out', 'grad_mix_post_norm', 'grad_ca_pre_norm', 'grad_mem_norm', 'grad_ca_wq', 'grad_ca_wk', 'grad_ca_wv', 'grad_ca_wo', 'grad_ca_post_norm', 'grad_ffn_pre_norm', 'grad_ffn_w_up', 'grad_ffn_conv_w', 'grad_ffn_conv_b', 'grad_ffn_w_down', 'grad_ffn_post_norm', 'delta_mix_pre_norm', 'delta_w_in', 'delta_attn_sinks', 'delta_hgrn_lb_logits', 'delta_hgrn_out_norm', 'delta_w_out', 'delta_mix_post_norm', 'delta_ca_pre_norm', 'delta_mem_norm', 'delta_ca_wq', 'delta_ca_wk', 'delta_ca_wv', 'delta_ca_wo', 'delta_ca_post_norm', 'delta_ffn_pre_norm', 'delta_ffn_w_up', 'delta_ffn_conv_w', 'delta_ffn_conv_b', 'delta_ffn_w_down', 'delta_ffn_post_norm', 'new_m_mix_pre_norm', 'new_m_w_in', 'new_m_attn_sinks', 'new_m_hgrn_lb_logits', 'new_m_hgrn_out_norm', 'new_m_w_out', 'new_m_mix_post_norm', 'new_m_ca_pre_norm', 'new_m_mem_norm', 'new_m_ca_wq', 'new_m_ca_wk', 'new_m_ca_wv', 'new_m_ca_wo', 'new_m_ca_post_norm', 'new_m_ffn_pre_norm', 'new_m_ffn_w_up', 'new_m_ffn_conv_w', 'new_m_ffn_conv_b', 'new_m_ffn_w_down', 'new_m_ffn_post_norm', 'new_v_mix_pre_norm', 'new_v_w_in', 'new_v_attn_sinks', 'new_v_hgrn_lb_logits', 'new_v_hgrn_out_norm', 'new_v_w_out', 'new_v_mix_post_norm', 'new_v_ca_pre_norm', 'new_v_mem_norm', 'new_v_ca_wq', 'new_v_ca_wk', 'new_v_ca_wv', 'new_v_ca_wo', 'new_v_ca_post_norm', 'new_v_ffn_pre_norm', 'new_v_ffn_w_up', 'new_v_ffn_conv_w', 'new_v_ffn_conv_b', 'new_v_ffn_w_down', 'new_v_ffn_post_norm']
TWIN_LEAF_KINDS = {'loss': 'loss', 'grad_x': 'grad_x', 'grad_mix_pre_norm': 'grad_w', 'grad_w_in': 'grad_w', 'grad_attn_sinks': 'grad_w', 'grad_hgrn_lb_logits': 'grad_w', 'grad_hgrn_out_norm': 'grad_w', 'grad_w_out': 'grad_w', 'grad_mix_post_norm': 'grad_w', 'grad_ca_pre_norm': 'grad_w', 'grad_mem_norm': 'grad_w', 'grad_ca_wq': 'grad_w', 'grad_ca_wk': 'grad_w', 'grad_ca_wv': 'grad_w', 'grad_ca_wo': 'grad_w', 'grad_ca_post_norm': 'grad_w', 'grad_ffn_pre_norm': 'grad_w', 'grad_ffn_w_up': 'grad_w', 'grad_ffn_conv_w': 'grad_w', 'grad_ffn_conv_b': 'grad_w', 'grad_ffn_w_down': 'grad_w', 'grad_ffn_post_norm': 'grad_w', 'delta_mix_pre_norm': 'delta_w', 'delta_w_in': 'delta_w', 'delta_attn_sinks': 'delta_w', 'delta_hgrn_lb_logits': 'delta_w', 'delta_hgrn_out_norm': 'delta_w', 'delta_w_out': 'delta_w', 'delta_mix_post_norm': 'delta_w', 'delta_ca_pre_norm': 'delta_w', 'delta_mem_norm': 'delta_w', 'delta_ca_wq': 'delta_w', 'delta_ca_wk': 'delta_w', 'delta_ca_wv': 'delta_w', 'delta_ca_wo': 'delta_w', 'delta_ca_post_norm': 'delta_w', 'delta_ffn_pre_norm': 'delta_w', 'delta_ffn_w_up': 'delta_w', 'delta_ffn_conv_w': 'delta_w', 'delta_ffn_conv_b': 'delta_w', 'delta_ffn_w_down': 'delta_w', 'delta_ffn_post_norm': 'delta_w', 'new_m_mix_pre_norm': 'new_m', 'new_m_w_in': 'new_m', 'new_m_attn_sinks': 'new_m', 'new_m_hgrn_lb_logits': 'new_m', 'new_m_hgrn_out_norm': 'new_m', 'new_m_w_out': 'new_m', 'new_m_mix_post_norm': 'new_m', 'new_m_ca_pre_norm': 'new_m', 'new_m_mem_norm': 'new_m', 'new_m_ca_wq': 'new_m', 'new_m_ca_wk': 'new_m', 'new_m_ca_wv': 'new_m', 'new_m_ca_wo': 'new_m', 'new_m_ca_post_norm': 'new_m', 'new_m_ffn_pre_norm': 'new_m', 'new_m_ffn_w_up': 'new_m', 'new_m_ffn_conv_w': 'new_m', 'new_m_ffn_conv_b': 'new_m', 'new_m_ffn_w_down': 'new_m', 'new_m_ffn_post_norm': 'new_m', 'new_v_mix_pre_norm': 'new_v', 'new_v_w_in': 'new_v', 'new_v_attn_sinks': 'new_v', 'new_v_hgrn_lb_logits': 'new_v', 'new_v_hgrn_out_norm': 'new_v', 'new_v_w_out': 'new_v', 'new_v_mix_post_norm': 'new_v', 'new_v_ca_pre_norm': 'new_v', 'new_v_mem_norm': 'new_v', 'new_v_ca_wq': 'new_v', 'new_v_ca_wk': 'new_v', 'new_v_ca_wv': 'new_v', 'new_v_ca_wo': 'new_v', 'new_v_ca_post_norm': 'new_v', 'new_v_ffn_pre_norm': 'new_v', 'new_v_ffn_w_up': 'new_v', 'new_v_ffn_conv_w': 'new_v', 'new_v_ffn_conv_b': 'new_v', 'new_v_ffn_w_down': 'new_v', 'new_v_ffn_post_norm': 'new_v'}


def _forward(args):
    return _fwd_reference(*[args[k] for k in FWD_PARAMS])


def _output_shape():
    def fwd():
        inp = _fwd_setup_inputs(0)
        return _fwd_reference(*[inp[k] for k in FWD_PARAMS])
    out = _jax.eval_shape(fwd)
    return out.shape, out.dtype

N_MICROBATCH = 1
ADAM_LR = 0.001
ADAM_B1 = 0.9
ADAM_B2 = 0.999
ADAM_EPS = 1e-08
ADAM_WD = 0.01
ADAM_STEP = 10
PER_EXAMPLE_BATCH_AXIS = {'x': 0, 'mem': 0, 'loss_target': 0}
SHARED_INPUTS = []
_WEIGHT_DTYPES = {'mix_pre_norm': _jnp.float32, 'w_in': _jnp.float32, 'attn_sinks': _jnp.float32, 'hgrn_lb_logits': _jnp.float32, 'hgrn_out_norm': _jnp.float32, 'w_out': _jnp.float32, 'mix_post_norm': _jnp.float32, 'ca_pre_norm': _jnp.float32, 'mem_norm': _jnp.float32, 'ca_wq': _jnp.float32, 'ca_wk': _jnp.float32, 'ca_wv': _jnp.float32, 'ca_wo': _jnp.float32, 'ca_post_norm': _jnp.float32, 'ffn_pre_norm': _jnp.float32, 'ffn_w_up': _jnp.float32, 'ffn_conv_w': _jnp.float32, 'ffn_conv_b': _jnp.float32, 'ffn_w_down': _jnp.float32, 'ffn_post_norm': _jnp.float32}
MOMENT_SCALE = {'mix_pre_norm': 1.275991e+00, 'w_in': 7.489866e-01, 'attn_sinks': 2.435983e-01, 'hgrn_lb_logits': 1.044082e-01, 'hgrn_out_norm': 3.195374e+00, 'w_out': 1.104879e+00, 'mix_post_norm': 6.376322e+01, 'ca_pre_norm': 9.420391e-01, 'mem_norm': 2.844562e+00, 'ca_wq': 9.735606e-01, 'ca_wk': 9.861279e-01, 'ca_wv': 2.960993e+00, 'ca_wo': 3.070158e+00, 'ca_post_norm': 6.472667e+01, 'ffn_pre_norm': 2.352215e+00, 'ffn_w_up': 9.235499e-01, 'ffn_conv_w': 1.169769e+00, 'ffn_conv_b': 3.142256e+00, 'ffn_w_down': 2.100749e+00, 'ffn_post_norm': 6.407699e+01}


def _to_microbatches(a, axis):
    t = _jnp.moveaxis(a, axis, 0)
    t = t.reshape((N_MICROBATCH, t.shape[0] // N_MICROBATCH) + t.shape[1:])
    return _jnp.moveaxis(t, 1, axis + 1)


def setup_inputs(seed: int = 0) -> dict:
    inp = _fwd_setup_inputs(seed)
    key = _jax.random.fold_in(_jax.random.key(seed), 7919)
    shape, _ = _output_shape()
    out = dict(inp)
    out["loss_target"] = _jax.random.normal(_jax.random.fold_in(key, 0), shape, _jnp.float32)
    for i, name in enumerate(TWIN_WEIGHTS):
        w = inp[name].astype(_jnp.float32)
        if MOMENT_SCALE is None:
            s = _jnp.sqrt(_jnp.mean(_jnp.square(w)) + 1e-30)
        else:
            s = MOMENT_SCALE[name]
        km, kv = _jax.random.split(_jax.random.fold_in(key, i + 1))
        out[name] = w
        out["m_" + name] = s * _jax.random.normal(km, w.shape, _jnp.float32)
        out["v_" + name] = (s * s) * _jax.random.uniform(kv, w.shape, _jnp.float32, 0.5, 1.5)
    if N_MICROBATCH > 1:
        for name, axis in PER_EXAMPLE_BATCH_AXIS.items():
            out[name] = _to_microbatches(out[name], axis)
    return {'x': out['x'], 'mem': out['mem'], 'mix_pre_norm': out['mix_pre_norm'], 'w_in': out['w_in'], 'attn_sinks': out['attn_sinks'], 'hgrn_lb_logits': out['hgrn_lb_logits'], 'hgrn_out_norm': out['hgrn_out_norm'], 'w_out': out['w_out'], 'mix_post_norm': out['mix_post_norm'], 'ca_pre_norm': out['ca_pre_norm'], 'mem_norm': out['mem_norm'], 'ca_wq': out['ca_wq'], 'ca_wk': out['ca_wk'], 'ca_wv': out['ca_wv'], 'ca_wo': out['ca_wo'], 'ca_post_norm': out['ca_post_norm'], 'ffn_pre_norm': out['ffn_pre_norm'], 'ffn_w_up': out['ffn_w_up'], 'ffn_conv_w': out['ffn_conv_w'], 'ffn_conv_b': out['ffn_conv_b'], 'ffn_w_down': out['ffn_w_down'], 'ffn_post_norm': out['ffn_post_norm'], 'loss_target': out['loss_target'], 'm_mix_pre_norm': out['m_mix_pre_norm'], 'm_w_in': out['m_w_in'], 'm_attn_sinks': out['m_attn_sinks'], 'm_hgrn_lb_logits': out['m_hgrn_lb_logits'], 'm_hgrn_out_norm': out['m_hgrn_out_norm'], 'm_w_out': out['m_w_out'], 'm_mix_post_norm': out['m_mix_post_norm'], 'm_ca_pre_norm': out['m_ca_pre_norm'], 'm_mem_norm': out['m_mem_norm'], 'm_ca_wq': out['m_ca_wq'], 'm_ca_wk': out['m_ca_wk'], 'm_ca_wv': out['m_ca_wv'], 'm_ca_wo': out['m_ca_wo'], 'm_ca_post_norm': out['m_ca_post_norm'], 'm_ffn_pre_norm': out['m_ffn_pre_norm'], 'm_ffn_w_up': out['m_ffn_w_up'], 'm_ffn_conv_w': out['m_ffn_conv_w'], 'm_ffn_conv_b': out['m_ffn_conv_b'], 'm_ffn_w_down': out['m_ffn_w_down'], 'm_ffn_post_norm': out['m_ffn_post_norm'], 'v_mix_pre_norm': out['v_mix_pre_norm'], 'v_w_in': out['v_w_in'], 'v_attn_sinks': out['v_attn_sinks'], 'v_hgrn_lb_logits': out['v_hgrn_lb_logits'], 'v_hgrn_out_norm': out['v_hgrn_out_norm'], 'v_w_out': out['v_w_out'], 'v_mix_post_norm': out['v_mix_post_norm'], 'v_ca_pre_norm': out['v_ca_pre_norm'], 'v_mem_norm': out['v_mem_norm'], 'v_ca_wq': out['v_ca_wq'], 'v_ca_wk': out['v_ca_wk'], 'v_ca_wv': out['v_ca_wv'], 'v_ca_wo': out['v_ca_wo'], 'v_ca_post_norm': out['v_ca_post_norm'], 'v_ffn_pre_norm': out['v_ffn_pre_norm'], 'v_ffn_w_up': out['v_ffn_w_up'], 'v_ffn_conv_w': out['v_ffn_conv_w'], 'v_ffn_conv_b': out['v_ffn_conv_b'], 'v_ffn_w_down': out['v_ffn_w_down'], 'v_ffn_post_norm': out['v_ffn_post_norm']}


def _loss(weights, diff, rest, loss_target):
    with _jax.named_scope("forward"):
        args = {**rest, TWIN_DIFF_INPUT: diff, **{k: w.astype(_WEIGHT_DTYPES[k]) for k, w in weights.items()}}
        y = _forward(args)
    with _jax.named_scope("loss_head"):
        err = _jnp.square(y.astype(_jnp.float32) - loss_target)
        return 0.5 * _jnp.sum(_jnp.mean(err, axis=-1)) if err.ndim else 0.5 * err


def _adamw(w, g, m, v):
    m = ADAM_B1 * m + (1.0 - ADAM_B1) * g
    v = ADAM_B2 * v + (1.0 - ADAM_B2) * _jnp.square(g)
    m_hat = m / (1.0 - ADAM_B1 ** ADAM_STEP)
    v_hat = v / (1.0 - ADAM_B2 ** ADAM_STEP)
    delta = -ADAM_LR * (m_hat / (_jnp.sqrt(v_hat) + ADAM_EPS) + ADAM_WD * w)
    return delta, m, v


def reference(x, mem, mix_pre_norm, w_in, attn_sinks, hgrn_lb_logits, hgrn_out_norm, w_out, mix_post_norm, ca_pre_norm, mem_norm, ca_wq, ca_wk, ca_wv, ca_wo, ca_post_norm, ffn_pre_norm, ffn_w_up, ffn_conv_w, ffn_conv_b, ffn_w_down, ffn_post_norm, loss_target, m_mix_pre_norm, m_w_in, m_attn_sinks, m_hgrn_lb_logits, m_hgrn_out_norm, m_w_out, m_mix_post_norm, m_ca_pre_norm, m_mem_norm, m_ca_wq, m_ca_wk, m_ca_wv, m_ca_wo, m_ca_post_norm, m_ffn_pre_norm, m_ffn_w_up, m_ffn_conv_w, m_ffn_conv_b, m_ffn_w_down, m_ffn_post_norm, v_mix_pre_norm, v_w_in, v_attn_sinks, v_hgrn_lb_logits, v_hgrn_out_norm, v_w_out, v_mix_post_norm, v_ca_pre_norm, v_mem_norm, v_ca_wq, v_ca_wk, v_ca_wv, v_ca_wo, v_ca_post_norm, v_ffn_pre_norm, v_ffn_w_up, v_ffn_conv_w, v_ffn_conv_b, v_ffn_w_down, v_ffn_post_norm):
    given = dict(x=x, mem=mem, mix_pre_norm=mix_pre_norm, w_in=w_in, attn_sinks=attn_sinks, hgrn_lb_logits=hgrn_lb_logits, hgrn_out_norm=hgrn_out_norm, w_out=w_out, mix_post_norm=mix_post_norm, ca_pre_norm=ca_pre_norm, mem_norm=mem_norm, ca_wq=ca_wq, ca_wk=ca_wk, ca_wv=ca_wv, ca_wo=ca_wo, ca_post_norm=ca_post_norm, ffn_pre_norm=ffn_pre_norm, ffn_w_up=ffn_w_up, ffn_conv_w=ffn_conv_w, ffn_conv_b=ffn_conv_b, ffn_w_down=ffn_w_down, ffn_post_norm=ffn_post_norm, loss_target=loss_target, m_mix_pre_norm=m_mix_pre_norm, m_w_in=m_w_in, m_attn_sinks=m_attn_sinks, m_hgrn_lb_logits=m_hgrn_lb_logits, m_hgrn_out_norm=m_hgrn_out_norm, m_w_out=m_w_out, m_mix_post_norm=m_mix_post_norm, m_ca_pre_norm=m_ca_pre_norm, m_mem_norm=m_mem_norm, m_ca_wq=m_ca_wq, m_ca_wk=m_ca_wk, m_ca_wv=m_ca_wv, m_ca_wo=m_ca_wo, m_ca_post_norm=m_ca_post_norm, m_ffn_pre_norm=m_ffn_pre_norm, m_ffn_w_up=m_ffn_w_up, m_ffn_conv_w=m_ffn_conv_w, m_ffn_conv_b=m_ffn_conv_b, m_ffn_w_down=m_ffn_w_down, m_ffn_post_norm=m_ffn_post_norm, v_mix_pre_norm=v_mix_pre_norm, v_w_in=v_w_in, v_attn_sinks=v_attn_sinks, v_hgrn_lb_logits=v_hgrn_lb_logits, v_hgrn_out_norm=v_hgrn_out_norm, v_w_out=v_w_out, v_mix_post_norm=v_mix_post_norm, v_ca_pre_norm=v_ca_pre_norm, v_mem_norm=v_mem_norm, v_ca_wq=v_ca_wq, v_ca_wk=v_ca_wk, v_ca_wv=v_ca_wv, v_ca_wo=v_ca_wo, v_ca_post_norm=v_ca_post_norm, v_ffn_pre_norm=v_ffn_pre_norm, v_ffn_w_up=v_ffn_w_up, v_ffn_conv_w=v_ffn_conv_w, v_ffn_conv_b=v_ffn_conv_b, v_ffn_w_down=v_ffn_w_down, v_ffn_post_norm=v_ffn_post_norm)
    weights = {n: given[n] for n in TWIN_WEIGHTS}
    shared = {n: given[n] for n in SHARED_INPUTS}
    per_example = {n: given[n] for n in ['x', 'mem']}
    grad_fn = _jax.value_and_grad(_loss, argnums=(0, 1))

    def one_microbatch(ex, loss_target):
        ex = dict(ex)
        diff = ex.pop(TWIN_DIFF_INPUT)
        return grad_fn(weights, diff, {**shared, **ex}, loss_target)

    if N_MICROBATCH == 1:
        loss, (grad_w, grad_x) = one_microbatch(per_example, given["loss_target"])
    else:
        def body(carry, xs):
            loss_sum, grad_sum = carry
            l_k, (gw_k, gx_k) = one_microbatch(xs[0], xs[1])
            with _jax.named_scope("update"):
                return (loss_sum + l_k, _jax.tree.map(_jnp.add, grad_sum, gw_k)), gx_k

        init = (_jnp.zeros((), _jnp.float32), _jax.tree.map(_jnp.zeros_like, weights))
        (loss, grad_w), grad_x = _jax.lax.scan(body, init, (per_example, given["loss_target"]))
    with _jax.named_scope("update"):
        delta_w, new_m, new_v = {}, {}, {}
        for n in TWIN_WEIGHTS:
            delta_w[n], new_m[n], new_v[n] = _adamw(weights[n], grad_w[n], given["m_" + n], given["v_" + n])
    return (loss, grad_x, *[grad_w[n] for n in TWIN_WEIGHTS], *[delta_w[n] for n in TWIN_WEIGHTS],
            *[new_m[n] for n in TWIN_WEIGHTS], *[new_v[n] for n in TWIN_WEIGHTS])
```

```python
import jax
import jax.numpy as jnp
from jax import lax
from jax.experimental import pallas as pl
from jax.experimental.pallas import tpu as pltpu

F32 = jnp.float32
BF16 = jnp.bfloat16
EPS = 1e-6
N_DEV = 8
MESH_AXES = ("x", "y", "c")

ATTN_HEAD_DIM = 64
ATTN_Q_HEADS = 8
ATTN_KV_HEADS = 2
ATTN_BLOCK = 128
HGRN_HEADS = 4
HGRN_DIM = 128
HGRN_CHUNK = 64
HGRN_LEVELS = (32, 16, 8, 4, 2, 1)
CA_HEADS = 4
CA_HEAD_DIM = 256
D_FF = 2816

ADAM_LR = 0.001
ADAM_B1 = 0.9
ADAM_B2 = 0.999
ADAM_EPS = 1e-08
ADAM_WD = 0.01
ADAM_STEP = 10

VMEM_LIMIT = 56 << 20
LANE = 128

NT = (((1,), (1,)), ((), ()))
TN = (((0,), (0,)), ((), ()))


def _params(*sem):
    return pltpu.CompilerParams(dimension_semantics=sem, vmem_limit_bytes=VMEM_LIMIT)


def _tile(n, cap):
    if n <= cap:
        return n
    best = 0
    for t in range(LANE, cap + 1, LANE):
        if n % t == 0:
            best = t
    assert best, (n, cap)
    return best


def _dot(a, b, dims=None):
    if dims is None:
        return jnp.dot(a, b, preferred_element_type=F32)
    return lax.dot_general(a, b, dims, preferred_element_type=F32)


def _bf(x):
    return x.astype(BF16)


def _sigmoid(x):
    return 1.0 / (1.0 + jnp.exp(-x))


def _rms(x):
    r = lax.rsqrt(jnp.mean(x * x, axis=-1, keepdims=True) + EPS)
    return x * r, r


def _rms_bwd(dxh, xh, r):
    return r * (dxh - xh * jnp.mean(dxh * xh, axis=-1, keepdims=True))


def _mm(a, b, *, mode, out_dtype, name, tm=512, tn=1024, tk=1024, split_a=False, split_b=False, split_out=False):
    def dims(arr, split):
        if split:
            return arr.shape[1], 2 * arr.shape[2]
        return arr.shape

    ar, ac = dims(a, split_a)
    br, bc = dims(b, split_b)
    if mode == "nn":
        M, K, N = ar, ac, bc
        assert br == K
    elif mode == "nt":
        M, K, N = ar, ac, br
        assert bc == K
    else:
        K, M, N = ar, ac, bc
        assert br == K
    a_cols_half = ac // 2 if split_a else None
    b_cols_half = bc // 2 if split_b else None
    tm = _tile(M, tm)
    tn = _tile((N // 2) if (split_out or (split_b and mode != "nt")) else N, tn)
    tk = _tile((K // 2) if ((split_a and mode != "tn") or (split_b and mode == "nt")) else K, tk)
    if split_a and mode == "tn":
        tm = _tile(M // 2, tm)
    gm, gn, gk = M // tm, N // tn, K // tk

    def spec(split, half, blk, rc):
        if not split:
            return pl.BlockSpec(blk, rc)
        per_half = half // blk[1]

        def imap(i, j, k):
            r, c = rc(i, j, k)
            return (c // per_half, r, c % per_half)

        return pl.BlockSpec((None,) + blk, imap)

    if mode == "nn":
        a_spec = spec(split_a, a_cols_half, (tm, tk), lambda i, j, k: (i, k))
        b_spec = spec(split_b, b_cols_half, (tk, tn), lambda i, j, k: (k, j))
        dn = None
    elif mode == "nt":
        a_spec = spec(split_a, a_cols_half, (tm, tk), lambda i, j, k: (i, k))
        b_spec = spec(split_b, b_cols_half, (tn, tk), lambda i, j, k: (j, k))
        dn = NT
    else:
        a_spec = spec(split_a, a_cols_half, (tk, tm), lambda i, j, k: (k, i))
        b_spec = spec(split_b, b_cols_half, (tk, tn), lambda i, j, k: (k, j))
        dn = TN
    o_spec = spec(split_out, N // 2 if split_out else None, (tm, tn), lambda i, j, k: (i, j))
    out_shape = (2, M, N // 2) if split_out else (M, N)

    if gk == 1:
        def body(a_ref, b_ref, o_ref):
            o_ref[...] = _dot(_bf(a_ref[...]), _bf(b_ref[...]), dn).astype(o_ref.dtype)
        scratch = []
    else:
        def body(a_ref, b_ref, o_ref, acc_ref):
            k = pl.program_id(2)

            @pl.when(k == 0)
            def _():
                acc_ref[...] = jnp.zeros_like(acc_ref)

            acc_ref[...] += _dot(_bf(a_ref[...]), _bf(b_ref[...]), dn)

            @pl.when(k == gk - 1)
            def _():
                o_ref[...] = acc_ref[...].astype(o_ref.dtype)
        scratch = [pltpu.VMEM((tm, tn), F32)]

    return pl.pallas_call(
        body, name=name, grid=(gm, gn, gk), in_specs=[a_spec, b_spec], out_specs=o_spec,
        out_shape=jax.ShapeDtypeStruct(out_shape, out_dtype), scratch_shapes=scratch,
        compiler_params=_params("parallel", "parallel", "arbitrary"),
    )(a, b)


ROWS = 256


def _row_spec(tr, cols):
    return pl.BlockSpec((tr, cols), lambda i: (i, 0))


def _vec_spec(cols):
    return pl.BlockSpec((1, cols), lambda i: (0, 0))


def _norm_fwd(x, g, name):
    T, Dm = x.shape
    tr = min(ROWS, T)

    def body(x_ref, g_ref, h_ref):
        xh, _ = _rms(x_ref[...])
        h_ref[...] = (xh * g_ref[...]).astype(h_ref.dtype)

    return pl.pallas_call(
        body, name=name, grid=(T // tr,), in_specs=[_row_spec(tr, Dm), _vec_spec(Dm)], out_specs=_row_spec(tr, Dm),
        out_shape=jax.ShapeDtypeStruct((T, Dm), BF16), compiler_params=_params("parallel"),
    )(x, g)


def _post_pre(x, m, g_post, g_pre, name):
    T, Dm = x.shape
    tr = min(ROWS, T)

    def body(x_ref, m_ref, gp_ref, gn_ref, xo_ref, h_ref):
        mh, _ = _rms(m_ref[...])
        xn = x_ref[...] + mh * gp_ref[...]
        xo_ref[...] = xn
        xh, _ = _rms(xn)
        h_ref[...] = (xh * gn_ref[...]).astype(h_ref.dtype)

    return pl.pallas_call(
        body, name=name, grid=(T // tr,),
        in_specs=[_row_spec(tr, Dm), _row_spec(tr, Dm), _vec_spec(Dm), _vec_spec(Dm)],
        out_specs=[_row_spec(tr, Dm), _row_spec(tr, Dm)],
        out_shape=[jax.ShapeDtypeStruct((T, Dm), F32), jax.ShapeDtypeStruct((T, Dm), BF16)],
        compiler_params=_params("parallel"),
    )(x, m, g_post, g_pre)


def _final(x2, y, g_post, target, name):
    T, Dm = x2.shape
    tr = min(ROWS, T)

    def body(x_ref, y_ref, g_ref, t_ref, loss_ref, dx_ref, dy_ref, dg_ref):
        @pl.when(pl.program_id(0) == 0)
        def _():
            loss_ref[...] = jnp.zeros_like(loss_ref)
            dg_ref[...] = jnp.zeros_like(dg_ref)

        g = g_ref[...]
        yh, r = _rms(y_ref[...])
        d = x_ref[...] + yh * g - t_ref[...]
        loss_ref[...] += jnp.zeros((1, LANE), F32) + 0.5 * jnp.sum(jnp.mean(d * d, axis=-1, keepdims=True))
        dx = d * (1.0 / Dm)
        dx_ref[...] = dx
        dy_ref[...] = _rms_bwd(dx * g, yh, r).astype(dy_ref.dtype)
        dg_ref[...] += jnp.sum(dx * yh, axis=0, keepdims=True)

    return pl.pallas_call(
        body, name=name, grid=(T // tr,),
        in_specs=[_row_spec(tr, Dm), _row_spec(tr, Dm), _vec_spec(Dm), _row_spec(tr, Dm)],
        out_specs=[_vec_spec(LANE), _row_spec(tr, Dm), _row_spec(tr, Dm), _vec_spec(Dm)],
        out_shape=[jax.ShapeDtypeStruct((1, LANE), F32), jax.ShapeDtypeStruct((T, Dm), F32),
                   jax.ShapeDtypeStruct((T, Dm), BF16), jax.ShapeDtypeStruct((1, Dm), F32)],
        compiler_params=_params("arbitrary"),
    )(x2, y, g_post, target)


def _norm_bwd2(dx_cur, dh, x_prev, g_pre, m_prev, g_post, name):
    T, Dm = x_prev.shape
    tr = min(ROWS, T)

    def body(dx_ref, dh_ref, x_ref, gn_ref, m_ref, gp_ref, dxo_ref, dm_ref, dgn_ref, dgp_ref):
        @pl.when(pl.program_id(0) == 0)
        def _():
            dgn_ref[...] = jnp.zeros_like(dgn_ref)
            dgp_ref[...] = jnp.zeros_like(dgp_ref)

        dh = dh_ref[...].astype(F32)
        xh, r = _rms(x_ref[...])
        dx = dx_ref[...] + _rms_bwd(dh * gn_ref[...], xh, r)
        dxo_ref[...] = dx
        dgn_ref[...] += jnp.sum(dh * xh, axis=0, keepdims=True)
        mh, rm = _rms(m_ref[...])
        dm_ref[...] = _rms_bwd(dx * gp_ref[...], mh, rm).astype(dm_ref.dtype)
        dgp_ref[...] += jnp.sum(dx * mh, axis=0, keepdims=True)

    return pl.pallas_call(
        body, name=name, grid=(T // tr,),
        in_specs=[_row_spec(tr, Dm), _row_spec(tr, Dm), _row_spec(tr, Dm), _vec_spec(Dm), _row_spec(tr, Dm), _vec_spec(Dm)],
        out_specs=[_row_spec(tr, Dm), _row_spec(tr, Dm), _vec_spec(Dm), _vec_spec(Dm)],
        out_shape=[jax.ShapeDtypeStruct((T, Dm), F32), jax.ShapeDtypeStruct((T, Dm), BF16),
                   jax.ShapeDtypeStruct((1, Dm), F32), jax.ShapeDtypeStruct((1, Dm), F32)],
        compiler_params=_params("arbitrary"),
    )(dx_cur, dh, x_prev, g_pre, m_prev, g_post)


def _norm_bwd1(dx_cur, dh, x_prev, g_pre, name):
    T, Dm = x_prev.shape
    tr = min(ROWS, T)

    def body(dx_ref, dh_ref, x_ref, gn_ref, dxo_ref, dgn_ref):
        @pl.when(pl.program_id(0) == 0)
        def _():
            dgn_ref[...] = jnp.zeros_like(dgn_ref)

        dh = dh_ref[...].astype(F32)
        xh, r = _rms(x_ref[...])
        dxo_ref[...] = dx_ref[...] + _rms_bwd(dh * gn_ref[...], xh, r)
        dgn_ref[...] += jnp.sum(dh * xh, axis=0, keepdims=True)

    return pl.pallas_call(
        body, name=name, grid=(T // tr,),
        in_specs=[_row_spec(tr, Dm), _row_spec(tr, Dm), _row_spec(tr, Dm), _vec_spec(Dm)],
        out_specs=[_row_spec(tr, Dm), _vec_spec(Dm)],
        out_shape=[jax.ShapeDtypeStruct((T, Dm), F32), jax.ShapeDtypeStruct((1, Dm), F32)],
        compiler_params=_params("arbitrary"),
    )(dx_cur, dh, x_prev, g_pre)


def _gain_bwd(x, dh_a, dh_b, name):
    T, Dm = x.shape

    def body(x_ref, a_ref, b_ref, dg_ref):
        xh, _ = _rms(x_ref[...])
        dg_ref[...] = jnp.sum((a_ref[...] + b_ref[...]) * xh, axis=0, keepdims=True)

    return pl.pallas_call(
        body, name=name, grid=(1,), in_specs=[_row_spec(T, Dm)] * 3, out_specs=_vec_spec(Dm),
        out_shape=jax.ShapeDtypeStruct((1, Dm), F32), compiler_params=_params("arbitrary"),
    )(x, dh_a, dh_b)


def _swa_mask(n):
    row = lax.broadcasted_iota(jnp.int32, (ATTN_BLOCK, 2 * ATTN_BLOCK), 0)
    col = lax.broadcasted_iota(jnp.int32, (ATTN_BLOCK, 2 * ATTN_BLOCK), 1)
    diff = row + ATTN_BLOCK - col
    return (diff >= 0) & (diff < ATTN_BLOCK) & ((col >= ATTN_BLOCK) | (n > 0))


def _swa_specs():
    blk = ATTN_BLOCK
    prev = lambda n: jnp.maximum(n - 1, 0)
    return [
        pl.BlockSpec(memory_space=pltpu.SMEM),
        pl.BlockSpec((blk, 512), lambda n: (n, 0)),
        pl.BlockSpec((blk, 128), lambda n: (prev(n), 4)),
        pl.BlockSpec((blk, 128), lambda n: (n, 4)),
        pl.BlockSpec((blk, 128), lambda n: (prev(n), 5)),
        pl.BlockSpec((blk, 128), lambda n: (n, 5)),
    ]


def _swa_fwd(z, sinks, name):
    T = z.shape[0]
    blk, hd = ATTN_BLOCK, ATTN_HEAD_DIM
    scale = hd ** -0.5

    def body(sink_ref, q_ref, kp_ref, kc_ref, vp_ref, vc_ref, o_ref, lse_ref):
        allowed = _swa_mask(pl.program_id(0))
        for hk in range(ATTN_KV_HEADS):
            ks = slice(hd * hk, hd * hk + hd)
            k = _bf(jnp.concatenate([kp_ref[:, ks], kc_ref[:, ks]], axis=0))
            v = _bf(jnp.concatenate([vp_ref[:, ks], vc_ref[:, ks]], axis=0))
            for g in range(ATTN_Q_HEADS // ATTN_KV_HEADS):
                h = hk * (ATTN_Q_HEADS // ATTN_KV_HEADS) + g
                hs = slice(hd * h, hd * h + hd)
                s = _dot(_bf(q_ref[:, hs]), k, NT) * scale
                s = jnp.where(allowed, s, -1e30)
                sink = sink_ref[0, h]
                m = jnp.maximum(jnp.max(s, axis=-1, keepdims=True), sink)
                p = jnp.exp(s - m)
                l = jnp.sum(p, axis=-1, keepdims=True) + jnp.exp(sink - m)
                o_ref[:, hs] = _dot(_bf(p / l), v).astype(o_ref.dtype)
                lse_ref[:, h:h + 1] = m + jnp.log(l)

    return pl.pallas_call(
        body, name=name, grid=(T // blk,), in_specs=_swa_specs(),
        out_specs=[pl.BlockSpec((blk, 512), lambda n: (n, 0)), pl.BlockSpec((blk, ATTN_Q_HEADS), lambda n: (n, 0))],
        out_shape=[jax.ShapeDtypeStruct((T, 512), BF16), jax.ShapeDtypeStruct((T, ATTN_Q_HEADS), F32)],
        compiler_params=_params("parallel"),
    )(sinks, z, z, z, z, z)


def _swa_bwd(z, sinks, dcat, lse, name):
    T = z.shape[0]
    blk, hd = ATTN_BLOCK, ATTN_HEAD_DIM
    scale = hd ** -0.5
    group = ATTN_Q_HEADS // ATTN_KV_HEADS

    def body(sink_ref, q_ref, kp_ref, kc_ref, vp_ref, vc_ref, do_ref, lse_ref,
             dq_ref, dka_ref, dkb_ref, dva_ref, dvb_ref, dsink_ref):
        @pl.when(pl.program_id(0) == 0)
        def _():
            dsink_ref[...] = jnp.zeros_like(dsink_ref)

        allowed = _swa_mask(pl.program_id(0))
        lane = lax.broadcasted_iota(jnp.int32, (1, ATTN_Q_HEADS), 1)
        dsink = jnp.zeros((1, ATTN_Q_HEADS), F32)
        for hk in range(ATTN_KV_HEADS):
            ks = slice(hd * hk, hd * hk + hd)
            k = _bf(jnp.concatenate([kp_ref[:, ks], kc_ref[:, ks]], axis=0))
            v = _bf(jnp.concatenate([vp_ref[:, ks], vc_ref[:, ks]], axis=0))
            dk = jnp.zeros((2 * blk, hd), F32)
            dv = jnp.zeros((2 * blk, hd), F32)
            for g in range(group):
                h = hk * group + g
                hs = slice(hd * h, hd * h + hd)
                qh = _bf(q_ref[:, hs])
                doh = _bf(do_ref[:, hs])
                lse_h = lse_ref[:, h:h + 1]
                s = _dot(qh, k, NT) * scale
                p = jnp.where(allowed, jnp.exp(jnp.where(allowed, s, -1e30) - lse_h), 0.0)
                dp = _dot(doh, v, NT)
                delta = jnp.sum(p * dp, axis=-1, keepdims=True)
                ds = _bf(p * (dp - delta) * scale)
                dq_ref[:, hs] = _dot(ds, k).astype(dq_ref.dtype)
                dk = dk + _dot(ds, qh, TN)
                dv = dv + _dot(_bf(p), doh, TN)
                p_sink = jnp.exp(sink_ref[0, h] - lse_h)
                dsink = dsink + jnp.where(lane == h, -jnp.sum(p_sink * delta), 0.0)
            dkb_ref[:, ks] = dk[:blk]
            dka_ref[:, ks] = dk[blk:]
            dvb_ref[:, ks] = dv[:blk]
            dva_ref[:, ks] = dv[blk:]
        dsink_ref[...] += dsink

    kv_out = pl.BlockSpec((blk, 128), lambda n: (n, 0))
    return pl.pallas_call(
        body, name=name, grid=(T // blk,),
        in_specs=_swa_specs() + [pl.BlockSpec((blk, 512), lambda n: (n, 0)),
                                 pl.BlockSpec((blk, ATTN_Q_HEADS), lambda n: (n, 0))],
        out_specs=[pl.BlockSpec((blk, 512), lambda n: (n, 0)), kv_out, kv_out, kv_out, kv_out,
                   pl.BlockSpec((1, ATTN_Q_HEADS), lambda n: (0, 0))],
        out_shape=[jax.ShapeDtypeStruct((T, 512), BF16)] + [jax.ShapeDtypeStruct((T, 128), F32)] * 4
        + [jax.ShapeDtypeStruct((1, ATTN_Q_HEADS), F32)],
        compiler_params=_params("arbitrary"),
    )(sinks, z, z, z, z, z, dcat, lse)


def _assemble_dz(dq_a, dka, dkb, dva, dvb, dqr, dfr, dir_, dgr, name):
    T = dq_a.shape[0]
    blk = ATTN_BLOCK
    nb = T // blk

    def body(dq_ref, dka_ref, dkb_ref, dva_ref, dvb_ref, dqr_ref, dfr_ref, dir_ref, dgr_ref, o_ref):
        has_next = pl.program_id(0) < nb - 1
        o_ref[:, 0:512] = dq_ref[...]
        o_ref[:, 512:640] = (dka_ref[...] + jnp.where(has_next, dkb_ref[...], 0.0)).astype(o_ref.dtype)
        o_ref[:, 640:768] = (dva_ref[...] + jnp.where(has_next, dvb_ref[...], 0.0)).astype(o_ref.dtype)
        o_ref[:, 768:1280] = dqr_ref[...]
        o_ref[:, 1280:1792] = dfr_ref[...]
        o_ref[:, 1792:2304] = dir_ref[...]
        o_ref[:, 2304:2816] = dgr_ref[...]

    cur = lambda w: pl.BlockSpec((blk, w), lambda n: (n, 0))
    nxt = pl.BlockSpec((blk, 128), lambda n: (jnp.minimum(n + 1, nb - 1), 0))
    return pl.pallas_call(
        body, name=name, grid=(nb,),
        in_specs=[cur(512), cur(128), nxt, cur(128), nxt, cur(512), cur(512), cur(512), cur(512)],
        out_specs=pl.BlockSpec((blk, 2816), lambda n: (n, 0)),
        out_shape=jax.ShapeDtypeStruct((T, 2816), BF16), compiler_params=_params("parallel"),
    )(dq_a, dka, dkb, dva, dvb, dqr, dfr, dir_, dgr)


HGRN_ROWS = 512


def _hgrn_consts():
    c = HGRN_CHUNK
    r = lax.broadcasted_iota(jnp.int32, (c, c), 0)
    s = lax.broadcasted_iota(jnp.int32, (c, c), 1)
    rcol = lax.broadcasted_iota(jnp.int32, (c, 1), 0)
    stack = [s <= r]
    same_block, upper = [], []
    for m in HGRN_LEVELS:
        ref = (r & ~(2 * m - 1)) + (m - 1)
        stack.append(s <= ref)
        same_block.append((r & ~(2 * m - 1)) == (s & ~(2 * m - 1)))
        upper.append((rcol & (2 * m - 1)) >= m)
    cum_mat = jnp.concatenate([jnp.where(t, 1.0, 0.0).astype(BF16) for t in stack], axis=0)
    rev_mat = jnp.where(s >= r, 1.0, 0.0).astype(BF16)
    return cum_mat, rev_mat, r == s, same_block, upper


def _split3(x):
    hi = _bf(x)
    r1 = x - hi.astype(F32)
    mid = _bf(r1)
    lo = _bf(r1 - mid.astype(F32))
    return jnp.concatenate([hi, mid, lo], axis=1)


def _dot_hilo(a, b):
    r, c = a.shape[0], b.shape[1]
    a_hi, b_hi = _bf(a), _bf(b)
    a2 = jnp.concatenate([a_hi, _bf(a - a_hi.astype(F32))], axis=0)
    b2 = jnp.concatenate([b_hi, _bf(b - b_hi.astype(F32))], axis=1)
    y = _dot(a2, b2)
    return y[:r, :c] + y[:r, c:] + y[r:, :c]


def _fold3(y):
    w = y.shape[1] // 3
    return y[:, :w] + y[:, w:2 * w] + y[:, 2 * w:]


def _hgrn_gates(qr, fr, lb):
    sq = _sigmoid(qr)
    q = qr * sq * (HGRN_DIM ** -0.5)
    sf = _sigmoid(fr)
    f = lb + (1.0 - lb) * sf
    k = (1.0 - lb) * _sigmoid(-fr)
    return q, sq, sf, f, k, jnp.log(f)


def _hgrn_intra(q, k, cums, consts):
    _, _, eye, same_block, upper = consts
    c = HGRN_CHUNK
    b = cums[:c]
    a = jnp.where(eye, _dot(_bf(q), _bf(k), NT), 0.0)
    saved = []
    for i in range(len(HGRN_LEVELS)):
        bref = cums[c * (i + 1):c * (i + 2)]
        up = upper[i]
        eq = jnp.where(up, jnp.exp(jnp.where(up, b - bref, 0.0)), 0.0)
        ek = jnp.where(up, 0.0, jnp.exp(jnp.where(up, 0.0, bref - b)))
        qt = q * eq
        kt = k * ek
        a = a + jnp.where(same_block[i], _dot(_bf(qt), _bf(kt), NT), 0.0)
        saved.append((eq, ek, qt, kt))
    return a, saved


def _hgrn_specs(tb, nb, rev):
    tmap = (lambda t: nb - 1 - t) if rev else (lambda t: t)
    zcol = lambda base: pl.BlockSpec((tb, HGRN_DIM), lambda h, t: (tmap(t), base + h))
    return zcol, [zcol(6), zcol(10), zcol(14), zcol(18),
                  pl.BlockSpec((1, HGRN_DIM), lambda h, t: (0, h)),
                  pl.BlockSpec((1, HGRN_DIM), lambda h, t: (0, 0))]


def _hgrn_fwd(z, lb, onw, name):
    T = z.shape[0]
    tb = min(HGRN_ROWS, T)
    nb, c, nc = T // tb, HGRN_CHUNK, min(HGRN_ROWS, T) // HGRN_CHUNK

    def body(qr_ref, fr_ref, ir_ref, gr_ref, lb_ref, onw_ref, rec_ref, o_ref, st_ref, state):
        @pl.when(pl.program_id(1) == 0)
        def _():
            state[...] = jnp.zeros_like(state)

        consts = _hgrn_consts()
        lbv = lb_ref[...]
        onwv = onw_ref[...]

        def chunk(ci, carry):
            sl = pl.ds(pl.multiple_of(ci * c, c), c)
            q, _, _, _, k, g = _hgrn_gates(qr_ref[sl, :], fr_ref[sl, :], lbv)
            v = _bf(ir_ref[sl, :])
            cums = _fold3(_dot(consts[0], _split3(g)))
            b = cums[:c]
            a, _ = _hgrn_intra(q, k, cums, consts)
            st = state[...]
            st_ref[ci] = st
            o = _dot(_bf(a), v) + _dot(_bf(q * jnp.exp(b)), _bf(st), NT)
            bl = b[c - 1:c, :]
            state[...] = st * jnp.exp(bl) + _dot(v, _bf(k * jnp.exp(bl - b)), TN)
            o_ref[sl, :] = o
            oh, _ = _rms(o)
            gr = gr_ref[sl, :]
            rec_ref[sl, :] = (oh * onwv * (gr * _sigmoid(gr))).astype(rec_ref.dtype)
            return carry

        lax.fori_loop(0, nc, chunk, 0)

    _, in_specs = _hgrn_specs(tb, nb, False)
    out_blk = pl.BlockSpec((tb, HGRN_DIM), lambda h, t: (t, h))
    return pl.pallas_call(
        body, name=name, grid=(HGRN_HEADS, nb), in_specs=in_specs,
        out_specs=[out_blk, out_blk, pl.BlockSpec((None, nc, HGRN_DIM, HGRN_DIM), lambda h, t: (h, t, 0, 0))],
        out_shape=[jax.ShapeDtypeStruct((T, 512), BF16), jax.ShapeDtypeStruct((T, 512), F32),
                   jax.ShapeDtypeStruct((HGRN_HEADS, T // c, HGRN_DIM, HGRN_DIM), F32)],
        scratch_shapes=[pltpu.VMEM((HGRN_DIM, HGRN_DIM), F32)],
        compiler_params=_params("parallel", "arbitrary"),
    )(z, z, z, z, lb, onw)


def _hgrn_bwd(z, lb, onw, o, states, dcat, name):
    T = z.shape[0]
    tb = min(HGRN_ROWS, T)
    nb, c, nc = T // tb, HGRN_CHUNK, min(HGRN_ROWS, T) // HGRN_CHUNK

    def body(qr_ref, fr_ref, ir_ref, gr_ref, lb_ref, onw_ref, o_ref, st_ref, drec_ref,
             dqr_ref, dfr_ref, dir_ref, dgr_ref, dlb_ref, donw_ref, dstate):
        @pl.when(pl.program_id(1) == 0)
        def _():
            dstate[...] = jnp.zeros_like(dstate)
            dlb_ref[...] = jnp.zeros_like(dlb_ref)

        @pl.when((pl.program_id(0) == 0) & (pl.program_id(1) == 0))
        def _():
            donw_ref[...] = jnp.zeros_like(donw_ref)

        consts = _hgrn_consts()
        rev_mat, eye, same_block = consts[1], consts[2], consts[3]
        lbv = lb_ref[...]
        onwv = onw_ref[...]
        last = lax.broadcasted_iota(jnp.int32, (c, 1), 0) == c - 1

        def chunk(i, carry):
            ci = nc - 1 - i
            sl = pl.ds(pl.multiple_of(ci * c, c), c)
            qr, fr = qr_ref[sl, :], fr_ref[sl, :]
            q, sq, sf, f, k, g = _hgrn_gates(qr, fr, lbv)
            vf = ir_ref[sl, :]
            v = _bf(vf)
            cums = _fold3(_dot(consts[0], _split3(g)))
            b = cums[:c]
            a, saved = _hgrn_intra(q, k, cums, consts)
            st = st_ref[ci]
            dst = dstate[...]

            gr = gr_ref[sl, :]
            sg = _sigmoid(gr)
            ov = o_ref[sl, :]
            oh, r = _rms(ov)
            drec = drec_ref[sl, :].astype(F32)
            dgr_ref[sl, :] = (drec * oh * onwv * (sg * (1.0 + gr * (1.0 - sg)))).astype(dgr_ref.dtype)
            don = drec * (gr * sg)
            donw_ref[...] += jnp.sum(don * oh, axis=0, keepdims=True)
            do = _bf(_rms_bwd(don * onwv, oh, r))

            eb = jnp.exp(b)
            bl = b[c - 1:c, :]
            ebl = jnp.exp(bl)
            ekb = jnp.exp(bl - b)
            qe = q * eb
            ke = k * ekb
            da = _dot(do, v, NT)
            dv = _dot(_bf(a), do, TN) + _dot(_bf(ke), _bf(dst), NT)
            dqe = _dot(do, _bf(st))
            dke = _dot(v, _bf(dst))
            dstate[...] = dst * ebl + _dot(do, _bf(qe), TN)
            dq = dqe * eb
            dk = dke * ekb
            db_last = jnp.sum(dke * ke, axis=0, keepdims=True) + jnp.sum(dst * st, axis=0, keepdims=True) * ebl
            dat = _dot(v, do, NT)
            dad = jnp.sum(jnp.where(eye, da, 0.0), axis=1, keepdims=True)
            dq = dq + dad * k
            dk = dk + dad * q
            for lvl in range(len(HGRN_LEVELS)):
                eq, ek, qt, kt = saved[lvl]
                dq = dq + _dot_hilo(jnp.where(same_block[lvl], da, 0.0), kt) * eq
                dk = dk + _dot_hilo(jnp.where(same_block[lvl], dat, 0.0), qt) * ek
            db = q * dq - k * dk + jnp.where(last, db_last, 0.0)
            dg = _fold3(_dot(rev_mat, _split3(db)))

            dqr_ref[sl, :] = (dq * (HGRN_DIM ** -0.5) * (sq * (1.0 + qr * (1.0 - sq)))).astype(dqr_ref.dtype)
            dfk = dg / f - dk
            dfr_ref[sl, :] = ((1.0 - lbv) * sf * (1.0 - sf) * dfk).astype(dfr_ref.dtype)
            dlb_ref[...] += jnp.sum((1.0 - sf) * dfk, axis=0, keepdims=True)
            dir_ref[sl, :] = dv.astype(dir_ref.dtype)
            return carry

        lax.fori_loop(0, nc, chunk, 0)

    zcol, in_specs = _hgrn_specs(tb, nb, True)
    rblk = pl.BlockSpec((tb, HGRN_DIM), lambda h, t: (nb - 1 - t, h))
    in_specs = in_specs + [
        rblk,
        pl.BlockSpec((None, nc, HGRN_DIM, HGRN_DIM), lambda h, t: (h, nb - 1 - t, 0, 0)),
        pl.BlockSpec((tb, HGRN_DIM), lambda h, t: (nb - 1 - t, 4 + h)),
    ]
    return pl.pallas_call(
        body, name=name, grid=(HGRN_HEADS, nb), in_specs=in_specs,
        out_specs=[rblk, rblk, rblk, rblk, pl.BlockSpec((1, HGRN_DIM), lambda h, t: (0, h)),
                   pl.BlockSpec((1, HGRN_DIM), lambda h, t: (0, 0))],
        out_shape=[jax.ShapeDtypeStruct((T, 512), BF16)] * 4
        + [jax.ShapeDtypeStruct((1, 512), F32), jax.ShapeDtypeStruct((1, HGRN_DIM), F32)],
        scratch_shapes=[pltpu.VMEM((HGRN_DIM, HGRN_DIM), F32)],
        compiler_params=_params("arbitrary", "arbitrary"),
    )(z, z, z, z, lb, onw, o, states, dcat)


def _lower_bound(logits, name):
    def body(l_ref, lb_ref):
        l0, l1 = l_ref[0:1, :], l_ref[1:2, :]
        m = jnp.maximum(l0, l1)
        e0, e1 = jnp.exp(l0 - m), jnp.exp(l1 - m)
        lb_ref[...] = e0 / (e0 + e1)

    return pl.pallas_call(
        body, name=name, out_shape=jax.ShapeDtypeStruct((1, logits.shape[1]), F32),
    )(logits)


def _lower_bound_bwd(lb, dlb, name):
    def body(lb_ref, dlb_ref, dl_ref):
        p = lb_ref[...]
        d0 = dlb_ref[...] * p * (1.0 - p)
        dl_ref[0:1, :] = d0
        dl_ref[1:2, :] = -d0

    return pl.pallas_call(
        body, name=name, out_shape=jax.ShapeDtypeStruct((2, lb.shape[1]), F32),
    )(lb, dlb)


CA_ROWS = 512


def _ca_fwd(q, k, v, name):
    T, W = q.shape
    M = k.shape[0]
    tq = min(CA_ROWS, T)
    scale = CA_HEAD_DIM ** -0.5

    def body(q_ref, k_ref, v_ref, o_ref):
        for h in range(CA_HEADS):
            hs = slice(CA_HEAD_DIM * h, CA_HEAD_DIM * (h + 1))
            s = _dot(q_ref[:, hs], k_ref[:, hs], NT) * scale
            p = jnp.exp(s - jnp.max(s, axis=-1, keepdims=True))
            p = p / jnp.sum(p, axis=-1, keepdims=True)
            o_ref[:, hs] = _dot(_bf(p), v_ref[:, hs]).astype(o_ref.dtype)

    full = pl.BlockSpec((M, W), lambda i: (0, 0))
    return pl.pallas_call(
        body, name=name, grid=(T // tq,), in_specs=[_row_spec(tq, W), full, full], out_specs=_row_spec(tq, W),
        out_shape=jax.ShapeDtypeStruct((T, W), BF16), compiler_params=_params("parallel"),
    )(q, k, v)


def _ca_bwd(q, k, v, do, name):
    T, W = q.shape
    M = k.shape[0]
    tq = min(CA_ROWS, T)
    scale = CA_HEAD_DIM ** -0.5

    def body(q_ref, k_ref, v_ref, do_ref, dq_ref, dk_ref, dv_ref):
        @pl.when(pl.program_id(0) == 0)
        def _():
            dk_ref[...] = jnp.zeros_like(dk_ref)
            dv_ref[...] = jnp.zeros_like(dv_ref)

        for h in range(CA_HEADS):
            hs = slice(CA_HEAD_DIM * h, CA_HEAD_DIM * (h + 1))
            qh, kh, vh, doh = q_ref[:, hs], k_ref[:, hs], v_ref[:, hs], do_ref[:, hs]
            s = _dot(qh, kh, NT) * scale
            p = jnp.exp(s - jnp.max(s, axis=-1, keepdims=True))
            p = p / jnp.sum(p, axis=-1, keepdims=True)
            dp = _dot(doh, vh, NT)
            ds = _bf(p * (dp - jnp.sum(p * dp, axis=-1, keepdims=True)) * scale)
            dq_ref[:, hs] = _dot(ds, kh).astype(dq_ref.dtype)
            dk_ref[:, hs] += _dot(ds, qh, TN)
            dv_ref[:, hs] += _dot(_bf(p), doh, TN)

    full = pl.BlockSpec((M, W), lambda i: (0, 0))
    return pl.pallas_call(
        body, name=name, grid=(T // tq,), in_specs=[_row_spec(tq, W), full, full, _row_spec(tq, W)],
        out_specs=[_row_spec(tq, W), full, full],
        out_shape=[jax.ShapeDtypeStruct((T, W), BF16), jax.ShapeDtypeStruct((M, W), F32), jax.ShapeDtypeStruct((M, W), F32)],
        compiler_params=_params("arbitrary"),
    )(q, k, v, do)


FFN_ROWS = 256
FFN_COLS = 1408
GELU_C0 = 0.7978845608028654
GELU_C1 = 0.044715


def _gelu(x):
    t = jnp.tanh(GELU_C0 * (x + GELU_C1 * x * x * x))
    return 0.5 * x * (1.0 + t), t


def _gelu_grad(x, t):
    return 0.5 * (1.0 + t) + 0.5 * x * (1.0 - t * t) * GELU_C0 * (1.0 + 3.0 * GELU_C1 * x * x)


def _shift_down(cur, halo, first, tb):
    row = lax.broadcasted_iota(jnp.int32, (tb, 1), 0)
    h6 = jnp.where(first, 0.0, halo[6:7])
    h7 = jnp.where(first, 0.0, halo[7:8])
    u1 = jnp.where(row == 0, h7, pltpu.roll(cur, 1, 0))
    u2 = jnp.where(row == 0, h6, jnp.where(row == 1, h7, pltpu.roll(cur, 2, 0)))
    return u1, u2


def _conv(u_ref, halo_ref, w_ref, b_ref, half, first, tb):
    cur = u_ref[half]
    u1, u2 = _shift_down(cur, halo_ref[half], first, tb)
    w = w_ref[...]
    return w[0:1] * u2 + w[1:2] * u1 + w[2:3] * cur + b_ref[...], cur, u1, u2


def _ffn_specs(tb, tc, rows_first):
    nj = D_FF // tc
    rc = (lambda a, b: (a, b)) if rows_first else (lambda a, b: (b, a))
    def at(f):
        return lambda a, b: f(*rc(a, b))
    blk = pl.BlockSpec((2, tb, tc), at(lambda t, j: (0, t, j)))
    halo = pl.BlockSpec((2, 8, tc), at(lambda t, j: (0, jnp.maximum(t * (tb // 8) - 1, 0), j)))
    wg = pl.BlockSpec((3, tc), at(lambda t, j: (0, j)))
    wv = pl.BlockSpec((3, tc), at(lambda t, j: (0, j + nj)))
    bg = pl.BlockSpec((1, tc), at(lambda t, j: (0, j)))
    bv = pl.BlockSpec((1, tc), at(lambda t, j: (0, j + nj)))
    flat = pl.BlockSpec((tb, tc), at(lambda t, j: (t, j)))
    return blk, halo, wg, wv, bg, bv, flat


def _glu_fwd(u, cw, cb, name):
    T = u.shape[1]
    tb, tc = min(FFN_ROWS, T), FFN_COLS

    def body(u_ref, halo_ref, wg_ref, wv_ref, bg_ref, bv_ref, a_ref):
        first = pl.program_id(0) == 0
        cg = _conv(u_ref, halo_ref, wg_ref, bg_ref, 0, first, tb)[0]
        cv = _conv(u_ref, halo_ref, wv_ref, bv_ref, 1, first, tb)[0]
        a_ref[...] = (_gelu(cg)[0] * cv).astype(a_ref.dtype)

    blk, halo, wg, wv, bg, bv, flat = _ffn_specs(tb, tc, True)
    return pl.pallas_call(
        body, name=name, grid=(T // tb, D_FF // tc), in_specs=[blk, halo, wg, wv, bg, bv], out_specs=flat,
        out_shape=jax.ShapeDtypeStruct((T, D_FF), BF16), compiler_params=_params("parallel", "parallel"),
    )(u, u, cw, cw, cb, cb)


def _glu_bwd(u, cw, cb, da, name):
    T = u.shape[1]
    tb, tc = min(FFN_ROWS, T), FFN_COLS

    def body(u_ref, halo_ref, wg_ref, wv_ref, bg_ref, bv_ref, da_ref, dc_ref, db_ref, dw_ref):
        first = pl.program_id(1) == 0

        @pl.when(first)
        def _():
            db_ref[...] = jnp.zeros_like(db_ref)
            dw_ref[...] = jnp.zeros_like(dw_ref)

        cg, ug, ug1, ug2 = _conv(u_ref, halo_ref, wg_ref, bg_ref, 0, first, tb)
        cv, uv, uv1, uv2 = _conv(u_ref, halo_ref, wv_ref, bv_ref, 1, first, tb)
        da = da_ref[...]
        gl, t = _gelu(cg)
        dcg = da * cv * _gelu_grad(cg, t)
        dcv = da * gl
        dc_ref[0] = dcg
        dc_ref[1] = dcv
        for half, dc, taps in ((0, dcg, (ug2, ug1, ug)), (1, dcv, (uv2, uv1, uv))):
            db_ref[half] += jnp.sum(dc, axis=0, keepdims=True)
            for tap in range(3):
                dw_ref[half, tap:tap + 1, :] += jnp.sum(dc * taps[tap], axis=0, keepdims=True)

    blk, halo, wg, wv, bg, bv, flat = _ffn_specs(tb, tc, False)
    return pl.pallas_call(
        body, name=name, grid=(D_FF // tc, T // tb), in_specs=[blk, halo, wg, wv, bg, bv, flat],
        out_specs=[blk, pl.BlockSpec((2, 1, tc), lambda j, t: (0, 0, j)), pl.BlockSpec((2, 3, tc), lambda j, t: (0, 0, j))],
        out_shape=[jax.ShapeDtypeStruct((2, T, D_FF), F32), jax.ShapeDtypeStruct((2, 1, D_FF), F32),
                   jax.ShapeDtypeStruct((2, 3, D_FF), F32)],
        compiler_params=_params("parallel", "arbitrary"),
    )(u, u, cw, cw, cb, cb, da)


def _conv_bwd(dc, cw, name):
    T = dc.shape[1]
    tb, tc = min(FFN_ROWS, T), FFN_COLS
    nt, nj = T // tb, D_FF // tc

    def body(dc_ref, halo_ref, wg_ref, wv_ref, du_ref):
        last = pl.program_id(0) == nt - 1
        row = lax.broadcasted_iota(jnp.int32, (tb, 1), 0)
        for half, w_ref in ((0, wg_ref), (1, wv_ref)):
            cur = dc_ref[half]
            halo = halo_ref[half]
            h0 = jnp.where(last, 0.0, halo[0:1])
            h1 = jnp.where(last, 0.0, halo[1:2])
            d1 = jnp.where(row == tb - 1, h0, pltpu.roll(cur, tb - 1, 0))
            d2 = jnp.where(row == tb - 1, h1, jnp.where(row == tb - 2, h0, pltpu.roll(cur, tb - 2, 0)))
            w = w_ref[...]
            du_ref[half] = (w[2:3] * cur + w[1:2] * d1 + w[0:1] * d2).astype(du_ref.dtype)

    blk = pl.BlockSpec((2, tb, tc), lambda t, j: (0, t, j))
    halo = pl.BlockSpec((2, 8, tc), lambda t, j: (0, jnp.minimum((t + 1) * (tb // 8), T // 8 - 1), j))
    wg = pl.BlockSpec((3, tc), lambda t, j: (0, j))
    wv = pl.BlockSpec((3, tc), lambda t, j: (0, j + nj))
    return pl.pallas_call(
        body, name=name, grid=(nt, nj), in_specs=[blk, halo, wg, wv], out_specs=blk,
        out_shape=jax.ShapeDtypeStruct((2, T, D_FF), BF16), compiler_params=_params("parallel", "parallel"),
    )(dc, dc, cw, cw)


def _mesh_pos():
    return lax.axis_index("x"), lax.axis_index("y"), lax.axis_index("c")


def _peer(pos, k):
    return (pos[0] ^ ((k >> 2) & 1), pos[1] ^ ((k >> 1) & 1), pos[2] ^ (k & 1))


def _index(pos):
    return 4 * pos[0] + 2 * pos[1] + pos[2]


def _all_gather(x, name):
    def body(x_ref, out_ref, send_sems, recv_sems, local_sem):
        pos = _mesh_pos()
        me = _index(pos)
        local = pltpu.make_async_copy(x_ref, out_ref.at[me], local_sem)
        local.start()
        sends = []
        for k in range(1, N_DEV):
            cp = pltpu.make_async_remote_copy(
                src_ref=x_ref, dst_ref=out_ref.at[me], send_sem=send_sems.at[k - 1], recv_sem=recv_sems.at[k - 1],
                device_id=_peer(pos, k), device_id_type=pl.DeviceIdType.MESH)
            cp.start()
            sends.append(cp)
        for k in range(1, N_DEV):
            peer = _peer(pos, k)
            pltpu.make_async_remote_copy(
                src_ref=x_ref, dst_ref=out_ref.at[_index(peer)], send_sem=send_sems.at[k - 1],
                recv_sem=recv_sems.at[k - 1], device_id=peer, device_id_type=pl.DeviceIdType.MESH).wait_recv()
        for cp in sends:
            cp.wait_send()
        local.wait()

    return pl.pallas_call(
        body, name=name, out_shape=jax.ShapeDtypeStruct((N_DEV,) + x.shape, x.dtype),
        in_specs=[pl.BlockSpec(memory_space=pl.ANY)], out_specs=pl.BlockSpec(memory_space=pl.ANY),
        scratch_shapes=[pltpu.SemaphoreType.DMA((N_DEV - 1,)), pltpu.SemaphoreType.DMA((N_DEV - 1,)),
                        pltpu.SemaphoreType.DMA],
    )(x)


def _exchange(x, name):
    def body(x_ref, out_ref, send_sems, recv_sems, local_sem):
        pos = _mesh_pos()
        me = _index(pos)
        local = pltpu.make_async_copy(x_ref.at[me], out_ref.at[me], local_sem)
        local.start()
        sends = []
        for k in range(1, N_DEV):
            peer = _peer(pos, k)
            cp = pltpu.make_async_remote_copy(
                src_ref=x_ref.at[_index(peer)], dst_ref=out_ref.at[me], send_sem=send_sems.at[k - 1],
                recv_sem=recv_sems.at[k - 1], device_id=peer, device_id_type=pl.DeviceIdType.MESH)
            cp.start()
            sends.append(cp)
        for k in range(1, N_DEV):
            peer = _peer(pos, k)
            pltpu.make_async_remote_copy(
                src_ref=x_ref.at[me], dst_ref=out_ref.at[_index(peer)], send_sem=send_sems.at[k - 1],
                recv_sem=recv_sems.at[k - 1], device_id=peer, device_id_type=pl.DeviceIdType.MESH).wait_recv()
        for cp in sends:
            cp.wait_send()
        local.wait()

    return pl.pallas_call(
        body, name=name, out_shape=jax.ShapeDtypeStruct(x.shape, x.dtype),
        in_specs=[pl.BlockSpec(memory_space=pl.ANY)], out_specs=pl.BlockSpec(memory_space=pl.ANY),
        scratch_shapes=[pltpu.SemaphoreType.DMA((N_DEV - 1,)), pltpu.SemaphoreType.DMA((N_DEV - 1,)),
                        pltpu.SemaphoreType.DMA],
    )(x)


def _adamw(w, g, m, v):
    m = ADAM_B1 * m + (1.0 - ADAM_B1) * g
    v = ADAM_B2 * v + (1.0 - ADAM_B2) * (g * g)
    m_hat = m / (1.0 - ADAM_B1 ** ADAM_STEP)
    v_hat = v / (1.0 - ADAM_B2 ** ADAM_STEP)
    delta = -ADAM_LR * (m_hat / (jnp.sqrt(v_hat) + ADAM_EPS) + ADAM_WD * w)
    return delta, m, v


def _sum_adamw(parts, w, m, v, name):
    R, C = w.shape
    tr = min(ROWS, R)

    def body(p_ref, w_ref, m_ref, v_ref, g_ref, d_ref, mo_ref, vo_ref):
        g = p_ref[0].astype(F32)
        for i in range(1, N_DEV):
            g = g + p_ref[i].astype(F32)
        g_ref[...] = g
        d_ref[...], mo_ref[...], vo_ref[...] = _adamw(w_ref[...], g, m_ref[...], v_ref[...])

    row = _row_spec(tr, C)
    return pl.pallas_call(
        body, name=name, grid=(R // tr,),
        in_specs=[pl.BlockSpec((N_DEV, tr, C), lambda i: (0, i, 0)), row, row, row], out_specs=[row] * 4,
        out_shape=[jax.ShapeDtypeStruct((R, C), F32)] * 4, compiler_params=_params("parallel"),
    )(parts, w, m, v)


def _sum_parts(parts, name):
    _, R, C = parts.shape

    def body(p_ref, g_ref):
        g = p_ref[0]
        for i in range(1, N_DEV):
            g = g + p_ref[i]
        g_ref[...] = g

    return pl.pallas_call(body, name=name, out_shape=jax.ShapeDtypeStruct((R, C), F32))(parts)


def _adamw_call(w, g, m, v, name):
    def body(w_ref, g_ref, m_ref, v_ref, d_ref, mo_ref, vo_ref):
        d_ref[...], mo_ref[...], vo_ref[...] = _adamw(w_ref[...], g_ref[...], m_ref[...], v_ref[...])

    return pl.pallas_call(body, name=name, out_shape=[jax.ShapeDtypeStruct(w.shape, F32)] * 3)(w, g, m, v)


BIG = ("w_in", "w_out", "ca_wq", "ca_wk", "ca_wv", "ca_wo", "ffn_w_up", "ffn_w_down")
BIG_FULL = {"w_in": (1024, 2816), "w_out": (1024, 1024), "ca_wq": (1024, 1024), "ca_wk": (1024, 1024),
            "ca_wv": (1024, 1024), "ca_wo": (1024, 1024), "ffn_w_up": (1024, 5632), "ffn_w_down": (2816, 1024)}
COL_SHARDED = ("w_in", "ffn_w_up")
PACK_COLS = 1024
NORMS = ("mix_pre_norm", "mix_post_norm", "ca_pre_norm", "mem_norm", "ca_post_norm", "ffn_pre_norm", "ffn_post_norm")
SMALL_ROWS = 32


def _big_rows(name):
    r, c = BIG_FULL[name]
    return r * c // N_DEV // PACK_COLS


def _pack_shards(shards):
    return jnp.concatenate([shards[n].reshape(_big_rows(n), PACK_COLS) for n in BIG], axis=0)


def _unpack_shards(pack, shapes):
    out, r0 = {}, 0
    for n in BIG:
        out[n] = pack[r0:r0 + _big_rows(n)].reshape(shapes[n])
        r0 += _big_rows(n)
    return out


def _unpack_gathered(gathered):
    out, r0 = {}, 0
    for n in BIG:
        rows = _big_rows(n)
        blk = gathered[:, r0:r0 + rows]
        r, c = BIG_FULL[n]
        if n in COL_SHARDED:
            out[n] = blk.reshape(N_DEV, r, c // N_DEV).transpose(1, 0, 2).reshape(r, c)
        else:
            out[n] = blk.reshape(r, c)
        r0 += rows
    return out


def _pack_full_grads(grads):
    parts = []
    for n in BIG:
        r, c = BIG_FULL[n]
        g = grads[n]
        if n in COL_SHARDED:
            g = g.reshape(r, N_DEV, c // N_DEV).transpose(1, 0, 2)
        parts.append(g.reshape(N_DEV, _big_rows(n), PACK_COLS))
    return jnp.concatenate(parts, axis=1).astype(BF16)


def _pad_row(vec):
    vec = vec.reshape(-1)
    n = -(-vec.shape[0] // PACK_COLS) * PACK_COLS
    return jnp.pad(vec, (0, n - vec.shape[0])).reshape(-1, PACK_COLS)


def _pack_small(norms, logits, out_norm, sinks, loss, conv_b, conv_w):
    rows = [_pad_row(norms[n]) for n in NORMS]
    rows.append(_pad_row(logits))
    rows.append(_pad_row(jnp.concatenate([out_norm.reshape(-1), sinks.reshape(-1), loss.reshape(-1)])))
    rows.append(_pad_row(conv_b))
    rows.append(_pad_row(conv_w))
    pack = jnp.concatenate(rows, axis=0)
    return jnp.pad(pack, ((0, SMALL_ROWS - pack.shape[0]), (0, 0)))


def _unpack_small(pack):
    norms = {n: pack[i:i + 1] for i, n in enumerate(NORMS)}
    logits = pack[7].reshape(2, 512)
    out_norm = pack[8:9, 0:128]
    sinks = pack[8:9, 128:136]
    loss = pack[8, 136]
    conv_b = pack[9:15].reshape(-1)[:2 * D_FF].reshape(1, 2 * D_FF)
    conv_w = pack[15:32].reshape(-1)[:6 * D_FF].reshape(3, 2 * D_FF)
    return norms, logits, out_norm, sinks, loss, conv_b, conv_w


def kernel(x, mem, mix_pre_norm, w_in, attn_sinks, hgrn_lb_logits, hgrn_out_norm, w_out, mix_post_norm, ca_pre_norm, mem_norm, ca_wq, ca_wk, ca_wv, ca_wo, ca_post_norm, ffn_pre_norm, ffn_w_up, ffn_conv_w, ffn_conv_b, ffn_w_down, ffn_post_norm, loss_target, m_mix_pre_norm, m_w_in, m_attn_sinks, m_hgrn_lb_logits, m_hgrn_out_norm, m_w_out, m_mix_post_norm, m_ca_pre_norm, m_mem_norm, m_ca_wq, m_ca_wk, m_ca_wv, m_ca_wo, m_ca_post_norm, m_ffn_pre_norm, m_ffn_w_up, m_ffn_conv_w, m_ffn_conv_b, m_ffn_w_down, m_ffn_post_norm, v_mix_pre_norm, v_w_in, v_attn_sinks, v_hgrn_lb_logits, v_hgrn_out_norm, v_w_out, v_mix_post_norm, v_ca_pre_norm, v_mem_norm, v_ca_wq, v_ca_wk, v_ca_wv, v_ca_wo, v_ca_post_norm, v_ffn_pre_norm, v_ffn_w_up, v_ffn_conv_w, v_ffn_conv_b, v_ffn_w_down, v_ffn_post_norm):
    names = ["mix_pre_norm", "w_in", "attn_sinks", "hgrn_lb_logits", "hgrn_out_norm", "w_out", "mix_post_norm",
             "ca_pre_norm", "mem_norm", "ca_wq", "ca_wk", "ca_wv", "ca_wo", "ca_post_norm", "ffn_pre_norm",
             "ffn_w_up", "ffn_conv_w", "ffn_conv_b", "ffn_w_down", "ffn_post_norm"]
    w_all = dict(zip(names, [mix_pre_norm, w_in, attn_sinks, hgrn_lb_logits, hgrn_out_norm, w_out, mix_post_norm,
                             ca_pre_norm, mem_norm, ca_wq, ca_wk, ca_wv, ca_wo, ca_post_norm, ffn_pre_norm,
                             ffn_w_up, ffn_conv_w, ffn_conv_b, ffn_w_down, ffn_post_norm]))
    m_all = dict(zip(names, [m_mix_pre_norm, m_w_in, m_attn_sinks, m_hgrn_lb_logits, m_hgrn_out_norm, m_w_out,
                             m_mix_post_norm, m_ca_pre_norm, m_mem_norm, m_ca_wq, m_ca_wk, m_ca_wv, m_ca_wo,
                             m_ca_post_norm, m_ffn_pre_norm, m_ffn_w_up, m_ffn_conv_w, m_ffn_conv_b, m_ffn_w_down,
                             m_ffn_post_norm]))
    v_all = dict(zip(names, [v_mix_pre_norm, v_w_in, v_attn_sinks, v_hgrn_lb_logits, v_hgrn_out_norm, v_w_out,
                             v_mix_post_norm, v_ca_pre_norm, v_mem_norm, v_ca_wq, v_ca_wk, v_ca_wv, v_ca_wo,
                             v_ca_post_norm, v_ffn_pre_norm, v_ffn_w_up, v_ffn_conv_w, v_ffn_conv_b, v_ffn_w_down,
                             v_ffn_post_norm]))
    dev = _index(_mesh_pos())

    shards = {n: w_all[n][0] for n in BIG}
    w_pack = _pack_shards(shards)
    gathered = _all_gather(w_pack.astype(BF16), "gather_weights")
    wf = _unpack_gathered(gathered)
    conv_w_rows = _all_gather(_pad_row(ffn_conv_w[0]), "gather_conv_w")
    conv_w_full = conv_w_rows.reshape(N_DEV, -1)[:, :3 * 704].reshape(N_DEV, 3, 704).transpose(1, 0, 2).reshape(3, 2 * D_FF)

    grads_big, grads_small, loss_local, grad_x = _local_step(
        x[0], mem[0], loss_target[0], wf, conv_w_full,
        {n: w_all[n] for n in NORMS}, attn_sinks, hgrn_lb_logits, hgrn_out_norm, ffn_conv_b)

    received = _exchange(_pack_full_grads(grads_big), "scatter_grads")
    small_pack = _pack_small(grads_small["norms"], grads_small["logits"], grads_small["out_norm"], grads_small["sinks"],
                             loss_local, grads_small["conv_b"], grads_small["conv_w"])
    small_sum = _sum_parts(_all_gather(small_pack, "gather_small"), "sum_small")
    g_norms, g_logits, g_out_norm, g_sinks, loss, g_conv_b, g_conv_w_full = _unpack_small(small_sum)
    g_conv_w = lax.dynamic_slice_in_dim(g_conv_w_full, dev * 704, 704, axis=1)

    g_pack, d_pack, m_pack, v_pack = _sum_adamw(
        received, w_pack, _pack_shards({n: m_all[n][0] for n in BIG}), _pack_shards({n: v_all[n][0] for n in BIG}),
        "adamw_big")
    shard_shapes = {n: w_all[n].shape for n in BIG}
    out_g, out_d, out_m, out_v = (_unpack_shards(p, shard_shapes) for p in (g_pack, d_pack, m_pack, v_pack))

    small_g = dict(g_norms)
    small_g.update(attn_sinks=g_sinks, hgrn_lb_logits=g_logits, hgrn_out_norm=g_out_norm,
                   ffn_conv_b=g_conv_b, ffn_conv_w=g_conv_w[None])
    small_names = [n for n in names if n not in BIG]

    def small_pack_of(tree):
        return jnp.concatenate([_pad_row(tree[n]) for n in small_names], axis=0)

    ds, ms, vs = _adamw_call(small_pack_of(w_all), small_pack_of(small_g), small_pack_of(m_all), small_pack_of(v_all),
                             "adamw_small")

    def small_unpack(pack):
        out, r0 = {}, 0
        for n in small_names:
            size = 1
            for s in w_all[n].shape:
                size *= s
            rows = -(-size // PACK_COLS)
            out[n] = pack[r0:r0 + rows].reshape(-1)[:size].reshape(w_all[n].shape)
            r0 += rows
        return out

    sd, sm, sv = small_unpack(ds), small_unpack(ms), small_unpack(vs)
    for n in small_names:
        out_g[n] = small_g[n].reshape(w_all[n].shape)
        out_d[n], out_m[n], out_v[n] = sd[n], sm[n], sv[n]

    return (loss, grad_x[None], *[out_g[n] for n in names], *[out_d[n] for n in names],
            *[out_m[n] for n in names], *[out_v[n] for n in names])


def _local_step(x, mem, target, wf, conv_w, norms, sinks, lb_logits, out_norm, conv_b):
    g1, g2, g3 = norms["mix_pre_norm"], norms["mix_post_norm"], norms["ca_pre_norm"]
    g4, g5, g6, g7 = norms["mem_norm"], norms["ca_post_norm"], norms["ffn_pre_norm"], norms["ffn_post_norm"]
    w_in, w_out, wq, wk, wv, wo = wf["w_in"], wf["w_out"], wf["ca_wq"], wf["ca_wk"], wf["ca_wv"], wf["ca_wo"]
    w_up, w_down = wf["ffn_w_up"], wf["ffn_w_down"]

    h1 = _norm_fwd(x, g1, "mix_norm")
    z = _mm(h1, w_in, mode="nn", out_dtype=F32, name="in_proj", tn=1408)
    attn, lse = _swa_fwd(z, sinks, "swa_fwd")
    lb = _lower_bound(lb_logits, "lower_bound")
    rec, o_rec, states = _hgrn_fwd(z, lb, out_norm, "hgrn_fwd")
    cat = jnp.concatenate([attn, rec], axis=1)
    mix = _mm(cat, w_out, mode="nn", out_dtype=F32, name="out_proj")
    x1, h2 = _post_pre(x, mix, g2, g3, "mix_post")
    mem_n = _norm_fwd(mem, g4, "mem_norm")
    q = _mm(h2, wq, mode="nn", out_dtype=BF16, name="ca_q")
    k = _mm(mem_n, wk, mode="nn", out_dtype=BF16, name="ca_k")
    v = _mm(mem_n, wv, mode="nn", out_dtype=BF16, name="ca_v")
    oc = _ca_fwd(q, k, v, "ca_fwd")
    c = _mm(oc, wo, mode="nn", out_dtype=F32, name="ca_o")
    x2, h3 = _post_pre(x1, c, g5, g6, "ca_post")
    u = _mm(h3, w_up, mode="nn", out_dtype=F32, name="ffn_up", tn=1408, split_out=True)
    a = _glu_fwd(u, conv_w, conv_b, "glu_fwd")
    y = _mm(a, w_down, mode="nn", out_dtype=F32, name="ffn_down", tk=1408)
    loss, dx3, dy, dg7 = _final(x2, y, g7, target, "loss_head")

    da = _mm(dy, w_down, mode="nt", out_dtype=F32, name="ffn_down_dx", tn=1408)
    d_w_down = _mm(a, dy, mode="tn", out_dtype=F32, name="ffn_down_dw", tm=1408, tk=512)
    dc, d_cb, d_cw = _glu_bwd(u, conv_w, conv_b, da, "glu_bwd")
    du = _conv_bwd(dc, conv_w, "conv_bwd")
    dh3 = _mm(du, w_up, mode="nt", out_dtype=F32, name="ffn_up_dx", tk=1408, split_a=True)
    d_w_up = _mm(h3, du, mode="tn", out_dtype=F32, name="ffn_up_dw", tm=1024, tn=1408, tk=512, split_b=True)
    dx2, dcv, dg6, dg5 = _norm_bwd2(dx3, dh3, x2, g6, c, g5, "ca_post_bwd")
    doc = _mm(dcv, wo, mode="nt", out_dtype=BF16, name="ca_o_dx")
    d_wo = _mm(oc, dcv, mode="tn", out_dtype=F32, name="ca_o_dw", tm=1024, tk=512)
    dq, dk, dv = _ca_bwd(q, k, v, doc, "ca_bwd")
    d_wq = _mm(h2, dq, mode="tn", out_dtype=F32, name="ca_q_dw", tm=1024, tk=512)
    dh2 = _mm(dq, wq, mode="nt", out_dtype=F32, name="ca_q_dx")
    d_wk = _mm(mem_n, dk, mode="tn", out_dtype=F32, name="ca_k_dw", tm=1024)
    d_wv = _mm(mem_n, dv, mode="tn", out_dtype=F32, name="ca_v_dw", tm=1024)
    dmem_k = _mm(dk, wk, mode="nt", out_dtype=F32, name="ca_k_dx")
    dmem_v = _mm(dv, wv, mode="nt", out_dtype=F32, name="ca_v_dx")
    dg4 = _gain_bwd(mem, dmem_k, dmem_v, "mem_norm_bwd")
    dx1, dmix, dg3, dg2 = _norm_bwd2(dx2, dh2, x1, g3, mix, g2, "mix_post_bwd")
    dcat = _mm(dmix, w_out, mode="nt", out_dtype=F32, name="out_proj_dx")
    d_w_out = _mm(cat, dmix, mode="tn", out_dtype=F32, name="out_proj_dw", tm=1024, tk=512)
    dqr, dfr, dir_, dgr, dlb, donw = _hgrn_bwd(z, lb, out_norm, o_rec, states, dcat, "hgrn_bwd")
    dq_a, dka, dkb, dva, dvb, dsinks = _swa_bwd(z, sinks, dcat, lse, "swa_bwd")
    dz = _assemble_dz(dq_a, dka, dkb, dva, dvb, dqr, dfr, dir_, dgr, "assemble_dz")
    d_w_in = _mm(h1, dz, mode="tn", out_dtype=F32, name="in_proj_dw", tm=1024, tn=1408, tk=512)
    dh1 = _mm(dz, w_in, mode="nt", out_dtype=F32, name="in_proj_dx", tk=1408)
    dx, dg1 = _norm_bwd1(dx1, dh1, x, g1, "mix_norm_bwd")

    big = {"w_in": d_w_in, "w_out": d_w_out, "ca_wq": d_wq, "ca_wk": d_wk, "ca_wv": d_wv, "ca_wo": d_wo,
           "ffn_w_up": d_w_up, "ffn_w_down": d_w_down}
    small = {
        "norms": {"mix_pre_norm": dg1, "mix_post_norm": dg2, "ca_pre_norm": dg3, "mem_norm": dg4,
                  "ca_post_norm": dg5, "ffn_pre_norm": dg6, "ffn_post_norm": dg7},
        "logits": _lower_bound_bwd(lb, dlb, "lower_bound_bwd"),
        "out_norm": donw,
        "sinks": dsinks,
        "conv_b": jnp.concatenate([d_cb[0], d_cb[1]], axis=1),
        "conv_w": jnp.concatenate([d_cw[0], d_cw[1]], axis=1),
    }
    return big, small, loss[0, 0:1], dx
```

```python
import jax
import jax.numpy as jnp
from jax import lax
from jax.experimental import pallas as pl
from jax.experimental.pallas import tpu as pltpu

F32 = jnp.float32
BF16 = jnp.bfloat16
EPS = 1e-6
N_DEV = 8
MESH_AXES = ("x", "y", "c")

ATTN_HEAD_DIM = 64
ATTN_Q_HEADS = 8
ATTN_KV_HEADS = 2
ATTN_BLOCK = 128
HGRN_HEADS = 4
HGRN_DIM = 128
HGRN_CHUNK = 64
HGRN_LEVELS = (32, 16, 8, 4, 2, 1)
CA_HEADS = 4
CA_HEAD_DIM = 256
D_FF = 2816

ADAM_LR = 0.001
ADAM_B1 = 0.9
ADAM_B2 = 0.999
ADAM_EPS = 1e-08
ADAM_WD = 0.01
ADAM_STEP = 10

VMEM_LIMIT = 56 << 20
LANE = 128

NT = (((1,), (1,)), ((), ()))
TN = (((0,), (0,)), ((), ()))


def _params(*sem):
    return pltpu.CompilerParams(dimension_semantics=sem, vmem_limit_bytes=VMEM_LIMIT)


def _tile(n, cap):
    if n <= cap:
        return n
    best = 0
    for t in range(LANE, cap + 1, LANE):
        if n % t == 0:
            best = t
    assert best, (n, cap)
    return best


def _dot(a, b, dims=None):
    if dims is None:
        return jnp.dot(a, b, preferred_element_type=F32)
    return lax.dot_general(a, b, dims, preferred_element_type=F32)


def _bf(x):
    return x.astype(BF16)


def _sigmoid(x):
    return 1.0 / (1.0 + jnp.exp(-x))


def _rms(x):
    r = lax.rsqrt(jnp.mean(x * x, axis=-1, keepdims=True) + EPS)
    return x * r, r


def _rms_bwd(dxh, xh, r):
    return r * (dxh - xh * jnp.mean(dxh * xh, axis=-1, keepdims=True))


def _mm(a, b, *, mode, out_dtype, name, tm=512, tn=1024, tk=1024, split_a=False, split_b=False, split_out=False,
        exchange=None):
    def dims(arr, split):
        if split:
            return arr.shape[1], 2 * arr.shape[2]
        return arr.shape

    ar, ac = dims(a, split_a)
    br, bc = dims(b, split_b)
    if mode == "nn":
        M, K, N = ar, ac, bc
        assert br == K
    elif mode == "nt":
        M, K, N = ar, ac, br
        assert bc == K
    else:
        K, M, N = ar, ac, bc
        assert br == K
    a_cols_half = ac // 2 if split_a else None
    b_cols_half = bc // 2 if split_b else None
    tm = _tile(M, tm)
    tn = _tile((N // 2) if (split_out or (split_b and mode != "nt")) else N, tn)
    tk = _tile((K // 2) if ((split_a and mode != "tn") or (split_b and mode == "nt")) else K, tk)
    if split_a and mode == "tn":
        tm = _tile(M // 2, tm)
    gm, gn, gk = M // tm, N // tn, K // tk

    def spec(split, half, blk, rc):
        if not split:
            return pl.BlockSpec(blk, rc)
        per_half = half // blk[1]

        def imap(i, j, k):
            r, c = rc(i, j, k)
            return (c // per_half, r, c % per_half)

        return pl.BlockSpec((None,) + blk, imap)

    if mode == "nn":
        a_spec = spec(split_a, a_cols_half, (tm, tk), lambda i, j, k: (i, k))
        b_spec = spec(split_b, b_cols_half, (tk, tn), lambda i, j, k: (k, j))
        dn = None
    elif mode == "nt":
        a_spec = spec(split_a, a_cols_half, (tm, tk), lambda i, j, k: (i, k))
        b_spec = spec(split_b, b_cols_half, (tn, tk), lambda i, j, k: (j, k))
        dn = NT
    else:
        a_spec = spec(split_a, a_cols_half, (tk, tm), lambda i, j, k: (k, i))
        b_spec = spec(split_b, b_cols_half, (tk, tn), lambda i, j, k: (k, j))
        dn = TN
    o_spec = spec(split_out, N // 2 if split_out else None, (tm, tn), lambda i, j, k: (i, j))
    out_shape = (2, M, N // 2) if split_out else (M, N)

    if gk == 1:
        def body(a_ref, b_ref, o_ref):
            o_ref[...] = _dot(_bf(a_ref[...]), _bf(b_ref[...]), dn).astype(o_ref.dtype)
        scratch = []
    else:
        def body(a_ref, b_ref, o_ref, acc_ref):
            k = pl.program_id(2)

            @pl.when(k == 0)
            def _():
                acc_ref[...] = jnp.zeros_like(acc_ref)

            acc_ref[...] += _dot(_bf(a_ref[...]), _bf(b_ref[...]), dn)

            @pl.when(k == gk - 1)
            def _():
                o_ref[...] = acc_ref[...].astype(o_ref.dtype)
        scratch = [pltpu.VMEM((tm, tn), F32)]

    out = _hosted_call(
        body, name=name, grid=(gm, gn, gk), in_specs=[a_spec, b_spec], out_specs=[o_spec],
        out_shape=[jax.ShapeDtypeStruct(out_shape, out_dtype)], scratch=scratch, args=(a, b),
        semantics=("parallel", "parallel", "arbitrary"), exchange=exchange)
    return out[0] if exchange is None else out


ROWS = 256


def _row_spec(tr, cols):
    return pl.BlockSpec((tr, cols), lambda i: (i, 0))


def _vec_spec(cols):
    return pl.BlockSpec((1, cols), lambda i: (0, 0))


def _norm_fwd(x, g, name, exchange=None):
    T, Dm = x.shape
    tr = min(ROWS, T)

    def body(x_ref, g_ref, h_ref):
        xh, _ = _rms(x_ref[...])
        h_ref[...] = (xh * g_ref[...]).astype(h_ref.dtype)

    out = _hosted_call(
        body, name=name, grid=(T // tr,), in_specs=[_row_spec(tr, Dm), _vec_spec(Dm)], out_specs=[_row_spec(tr, Dm)],
        out_shape=[jax.ShapeDtypeStruct((T, Dm), BF16)], scratch=[], args=(x, g), semantics=("parallel",),
        exchange=exchange)
    return out[0] if exchange is None else out


def _post_pre(x, m, g_post, g_pre, name):
    T, Dm = x.shape
    tr = min(ROWS, T)

    def body(x_ref, m_ref, gp_ref, gn_ref, xo_ref, h_ref):
        mh, _ = _rms(m_ref[...])
        xn = x_ref[...] + mh * gp_ref[...]
        xo_ref[...] = xn
        xh, _ = _rms(xn)
        h_ref[...] = (xh * gn_ref[...]).astype(h_ref.dtype)

    return pl.pallas_call(
        body, name=name, grid=(T // tr,),
        in_specs=[_row_spec(tr, Dm), _row_spec(tr, Dm), _vec_spec(Dm), _vec_spec(Dm)],
        out_specs=[_row_spec(tr, Dm), _row_spec(tr, Dm)],
        out_shape=[jax.ShapeDtypeStruct((T, Dm), F32), jax.ShapeDtypeStruct((T, Dm), BF16)],
        compiler_params=_params("parallel"),
    )(x, m, g_post, g_pre)


def _final(x2, y, g_post, target, name):
    T, Dm = x2.shape
    tr = min(ROWS, T)

    def body(x_ref, y_ref, g_ref, t_ref, loss_ref, dx_ref, dy_ref, dg_ref):
        @pl.when(pl.program_id(0) == 0)
        def _():
            loss_ref[...] = jnp.zeros_like(loss_ref)
            dg_ref[...] = jnp.zeros_like(dg_ref)

        g = g_ref[...]
        yh, r = _rms(y_ref[...])
        d = x_ref[...] + yh * g - t_ref[...]
        loss_ref[...] += jnp.zeros((1, LANE), F32) + 0.5 * jnp.sum(jnp.mean(d * d, axis=-1, keepdims=True))
        dx = d * (1.0 / Dm)
        dx_ref[...] = dx
        dy_ref[...] = _rms_bwd(dx * g, yh, r).astype(dy_ref.dtype)
        dg_ref[...] += jnp.sum(dx * yh, axis=0, keepdims=True)

    return pl.pallas_call(
        body, name=name, grid=(T // tr,),
        in_specs=[_row_spec(tr, Dm), _row_spec(tr, Dm), _vec_spec(Dm), _row_spec(tr, Dm)],
        out_specs=[_vec_spec(LANE), _row_spec(tr, Dm), _row_spec(tr, Dm), _vec_spec(Dm)],
        out_shape=[jax.ShapeDtypeStruct((1, LANE), F32), jax.ShapeDtypeStruct((T, Dm), F32),
                   jax.ShapeDtypeStruct((T, Dm), BF16), jax.ShapeDtypeStruct((1, Dm), F32)],
        compiler_params=_params("arbitrary"),
    )(x2, y, g_post, target)


def _norm_bwd2(dx_cur, dh, x_prev, g_pre, m_prev, g_post, name):
    T, Dm = x_prev.shape
    tr = min(ROWS, T)

    def body(dx_ref, dh_ref, x_ref, gn_ref, m_ref, gp_ref, dxo_ref, dm_ref, dgn_ref, dgp_ref):
        @pl.when(pl.program_id(0) == 0)
        def _():
            dgn_ref[...] = jnp.zeros_like(dgn_ref)
            dgp_ref[...] = jnp.zeros_like(dgp_ref)

        dh = dh_ref[...].astype(F32)
        xh, r = _rms(x_ref[...])
        dx = dx_ref[...] + _rms_bwd(dh * gn_ref[...], xh, r)
        dxo_ref[...] = dx
        dgn_ref[...] += jnp.sum(dh * xh, axis=0, keepdims=True)
        mh, rm = _rms(m_ref[...])
        dm_ref[...] = _rms_bwd(dx * gp_ref[...], mh, rm).astype(dm_ref.dtype)
        dgp_ref[...] += jnp.sum(dx * mh, axis=0, keepdims=True)

    return pl.pallas_call(
        body, name=name, grid=(T // tr,),
        in_specs=[_row_spec(tr, Dm), _row_spec(tr, Dm), _row_spec(tr, Dm), _vec_spec(Dm), _row_spec(tr, Dm), _vec_spec(Dm)],
        out_specs=[_row_spec(tr, Dm), _row_spec(tr, Dm), _vec_spec(Dm), _vec_spec(Dm)],
        out_shape=[jax.ShapeDtypeStruct((T, Dm), F32), jax.ShapeDtypeStruct((T, Dm), BF16),
                   jax.ShapeDtypeStruct((1, Dm), F32), jax.ShapeDtypeStruct((1, Dm), F32)],
        compiler_params=_params("arbitrary"),
    )(dx_cur, dh, x_prev, g_pre, m_prev, g_post)


def _norm_bwd1(dx_cur, dh, x_prev, g_pre, name):
    T, Dm = x_prev.shape
    tr = min(ROWS, T)

    def body(dx_ref, dh_ref, x_ref, gn_ref, dxo_ref, dgn_ref):
        @pl.when(pl.program_id(0) == 0)
        def _():
            dgn_ref[...] = jnp.zeros_like(dgn_ref)

        dh = dh_ref[...].astype(F32)
        xh, r = _rms(x_ref[...])
        dxo_ref[...] = dx_ref[...] + _rms_bwd(dh * gn_ref[...], xh, r)
        dgn_ref[...] += jnp.sum(dh * xh, axis=0, keepdims=True)

    return pl.pallas_call(
        body, name=name, grid=(T // tr,),
        in_specs=[_row_spec(tr, Dm), _row_spec(tr, Dm), _row_spec(tr, Dm), _vec_spec(Dm)],
        out_specs=[_row_spec(tr, Dm), _vec_spec(Dm)],
        out_shape=[jax.ShapeDtypeStruct((T, Dm), F32), jax.ShapeDtypeStruct((1, Dm), F32)],
        compiler_params=_params("arbitrary"),
    )(dx_cur, dh, x_prev, g_pre)


def _gain_bwd(x, dh_a, dh_b, name):
    T, Dm = x.shape

    def body(x_ref, a_ref, b_ref, dg_ref):
        xh, _ = _rms(x_ref[...])
        dg_ref[...] = jnp.sum((a_ref[...] + b_ref[...]) * xh, axis=0, keepdims=True)

    return pl.pallas_call(
        body, name=name, grid=(1,), in_specs=[_row_spec(T, Dm)] * 3, out_specs=_vec_spec(Dm),
        out_shape=jax.ShapeDtypeStruct((1, Dm), F32), compiler_params=_params("arbitrary"),
    )(x, dh_a, dh_b)


def _swa_mask(n):
    row = lax.broadcasted_iota(jnp.int32, (ATTN_BLOCK, 2 * ATTN_BLOCK), 0)
    col = lax.broadcasted_iota(jnp.int32, (ATTN_BLOCK, 2 * ATTN_BLOCK), 1)
    diff = row + ATTN_BLOCK - col
    return (diff >= 0) & (diff < ATTN_BLOCK) & ((col >= ATTN_BLOCK) | (n > 0))


def _swa_specs():
    blk = ATTN_BLOCK
    prev = lambda n: jnp.maximum(n - 1, 0)
    return [
        pl.BlockSpec(memory_space=pltpu.SMEM),
        pl.BlockSpec((blk, 512), lambda n: (n, 0)),
        pl.BlockSpec((blk, 128), lambda n: (prev(n), 4)),
        pl.BlockSpec((blk, 128), lambda n: (n, 4)),
        pl.BlockSpec((blk, 128), lambda n: (prev(n), 5)),
        pl.BlockSpec((blk, 128), lambda n: (n, 5)),
    ]


def _swa_fwd(z, sinks, name):
    T = z.shape[0]
    blk, hd = ATTN_BLOCK, ATTN_HEAD_DIM
    scale = hd ** -0.5

    def body(sink_ref, q_ref, kp_ref, kc_ref, vp_ref, vc_ref, o_ref, lse_ref):
        allowed = _swa_mask(pl.program_id(0))
        for hk in range(ATTN_KV_HEADS):
            ks = slice(hd * hk, hd * hk + hd)
            k = _bf(jnp.concatenate([kp_ref[:, ks], kc_ref[:, ks]], axis=0))
            v = _bf(jnp.concatenate([vp_ref[:, ks], vc_ref[:, ks]], axis=0))
            for g in range(ATTN_Q_HEADS // ATTN_KV_HEADS):
                h = hk * (ATTN_Q_HEADS // ATTN_KV_HEADS) + g
                hs = slice(hd * h, hd * h + hd)
                s = _dot(_bf(q_ref[:, hs]), k, NT) * scale
                s = jnp.where(allowed, s, -1e30)
                sink = sink_ref[0, h]
                m = jnp.maximum(jnp.max(s, axis=-1, keepdims=True), sink)
                p = jnp.exp(s - m)
                l = jnp.sum(p, axis=-1, keepdims=True) + jnp.exp(sink - m)
                o_ref[:, hs] = _dot(_bf(p / l), v).astype(o_ref.dtype)
                lse_ref[:, h:h + 1] = m + jnp.log(l)

    return pl.pallas_call(
        body, name=name, grid=(T // blk,), in_specs=_swa_specs(),
        out_specs=[pl.BlockSpec((blk, 512), lambda n: (n, 0)), pl.BlockSpec((blk, ATTN_Q_HEADS), lambda n: (n, 0))],
        out_shape=[jax.ShapeDtypeStruct((T, 512), BF16), jax.ShapeDtypeStruct((T, ATTN_Q_HEADS), F32)],
        compiler_params=_params("parallel"),
    )(sinks, z, z, z, z, z)


def _swa_bwd(z, sinks, dcat, lse, name):
    T = z.shape[0]
    blk, hd = ATTN_BLOCK, ATTN_HEAD_DIM
    scale = hd ** -0.5
    group = ATTN_Q_HEADS // ATTN_KV_HEADS

    def body(sink_ref, q_ref, kp_ref, kc_ref, vp_ref, vc_ref, do_ref, lse_ref,
             dq_ref, dka_ref, dkb_ref, dva_ref, dvb_ref, dsink_ref):
        @pl.when(pl.program_id(0) == 0)
        def _():
            dsink_ref[...] = jnp.zeros_like(dsink_ref)

        allowed = _swa_mask(pl.program_id(0))
        lane = lax.broadcasted_iota(jnp.int32, (1, ATTN_Q_HEADS), 1)
        dsink = jnp.zeros((1, ATTN_Q_HEADS), F32)
        for hk in range(ATTN_KV_HEADS):
            ks = slice(hd * hk, hd * hk + hd)
            k = _bf(jnp.concatenate([kp_ref[:, ks], kc_ref[:, ks]], axis=0))
            v = _bf(jnp.concatenate([vp_ref[:, ks], vc_ref[:, ks]], axis=0))
            dk = jnp.zeros((2 * blk, hd), F32)
            dv = jnp.zeros((2 * blk, hd), F32)
            for g in range(group):
                h = hk * group + g
                hs = slice(hd * h, hd * h + hd)
                qh = _bf(q_ref[:, hs])
                doh = _bf(do_ref[:, hs])
                lse_h = lse_ref[:, h:h + 1]
                s = _dot(qh, k, NT) * scale
                p = jnp.where(allowed, jnp.exp(jnp.where(allowed, s, -1e30) - lse_h), 0.0)
                dp = _dot(doh, v, NT)
                delta = jnp.sum(p * dp, axis=-1, keepdims=True)
                ds = _bf(p * (dp - delta) * scale)
                dq_ref[:, hs] = _dot(ds, k).astype(dq_ref.dtype)
                dk = dk + _dot(ds, qh, TN)
                dv = dv + _dot(_bf(p), doh, TN)
                p_sink = jnp.exp(sink_ref[0, h] - lse_h)
                dsink = dsink + jnp.where(lane == h, -jnp.sum(p_sink * delta), 0.0)
            dkb_ref[:, ks] = dk[:blk]
            dka_ref[:, ks] = dk[blk:]
            dvb_ref[:, ks] = dv[:blk]
            dva_ref[:, ks] = dv[blk:]
        dsink_ref[...] += dsink

    kv_out = pl.BlockSpec((blk, 128), lambda n: (n, 0))
    return pl.pallas_call(
        body, name=name, grid=(T // blk,),
        in_specs=_swa_specs() + [pl.BlockSpec((blk, 512), lambda n: (n, 0)),
                                 pl.BlockSpec((blk, ATTN_Q_HEADS), lambda n: (n, 0))],
        out_specs=[pl.BlockSpec((blk, 512), lambda n: (n, 0)), kv_out, kv_out, kv_out, kv_out,
                   pl.BlockSpec((1, ATTN_Q_HEADS), lambda n: (0, 0))],
        out_shape=[jax.ShapeDtypeStruct((T, 512), BF16)] + [jax.ShapeDtypeStruct((T, 128), F32)] * 4
        + [jax.ShapeDtypeStruct((1, ATTN_Q_HEADS), F32)],
        compiler_params=_params("arbitrary"),
    )(sinks, z, z, z, z, z, dcat, lse)


def _assemble_dz(dq_a, dka, dkb, dva, dvb, dqr, dfr, dir_, dgr, name):
    T = dq_a.shape[0]
    blk = ATTN_BLOCK
    nb = T // blk

    def body(dq_ref, dka_ref, dkb_ref, dva_ref, dvb_ref, dqr_ref, dfr_ref, dir_ref, dgr_ref, o_ref):
        has_next = pl.program_id(0) < nb - 1
        o_ref[:, 0:512] = dq_ref[...]
        o_ref[:, 512:640] = (dka_ref[...] + jnp.where(has_next, dkb_ref[...], 0.0)).astype(o_ref.dtype)
        o_ref[:, 640:768] = (dva_ref[...] + jnp.where(has_next, dvb_ref[...], 0.0)).astype(o_ref.dtype)
        o_ref[:, 768:1280] = dqr_ref[...]
        o_ref[:, 1280:1792] = dfr_ref[...]
        o_ref[:, 1792:2304] = dir_ref[...]
        o_ref[:, 2304:2816] = dgr_ref[...]

    cur = lambda w: pl.BlockSpec((blk, w), lambda n: (n, 0))
    nxt = pl.BlockSpec((blk, 128), lambda n: (jnp.minimum(n + 1, nb - 1), 0))
    return pl.pallas_call(
        body, name=name, grid=(nb,),
        in_specs=[cur(512), cur(128), nxt, cur(128), nxt, cur(512), cur(512), cur(512), cur(512)],
        out_specs=pl.BlockSpec((blk, 2816), lambda n: (n, 0)),
        out_shape=jax.ShapeDtypeStruct((T, 2816), BF16), compiler_params=_params("parallel"),
    )(dq_a, dka, dkb, dva, dvb, dqr, dfr, dir_, dgr)


HGRN_ROWS = 512


def _hgrn_consts():
    c = HGRN_CHUNK
    r = lax.broadcasted_iota(jnp.int32, (c, c), 0)
    s = lax.broadcasted_iota(jnp.int32, (c, c), 1)
    rcol = lax.broadcasted_iota(jnp.int32, (c, 1), 0)
    stack = [s <= r]
    same_block, upper = [], []
    for m in HGRN_LEVELS:
        ref = (r & ~(2 * m - 1)) + (m - 1)
        stack.append(s <= ref)
        same_block.append((r & ~(2 * m - 1)) == (s & ~(2 * m - 1)))
        upper.append((rcol & (2 * m - 1)) >= m)
    cum_mat = jnp.concatenate([jnp.where(t, 1.0, 0.0).astype(BF16) for t in stack], axis=0)
    rev_mat = jnp.where(s >= r, 1.0, 0.0).astype(BF16)
    return cum_mat, rev_mat, r == s, same_block, upper


def _split3(x):
    hi = _bf(x)
    r1 = x - hi.astype(F32)
    mid = _bf(r1)
    lo = _bf(r1 - mid.astype(F32))
    return jnp.concatenate([hi, mid, lo], axis=1)


def _dot_hilo(a, b):
    r, c = a.shape[0], b.shape[1]
    a_hi, b_hi = _bf(a), _bf(b)
    a2 = jnp.concatenate([a_hi, _bf(a - a_hi.astype(F32))], axis=0)
    b2 = jnp.concatenate([b_hi, _bf(b - b_hi.astype(F32))], axis=1)
    y = _dot(a2, b2)
    return y[:r, :c] + y[:r, c:] + y[r:, :c]


def _fold3(y):
    w = y.shape[1] // 3
    return y[:, :w] + y[:, w:2 * w] + y[:, 2 * w:]


def _hgrn_gates(qr, fr, lb):
    sq = _sigmoid(qr)
    q = qr * sq * (HGRN_DIM ** -0.5)
    sf = _sigmoid(fr)
    f = lb + (1.0 - lb) * sf
    k = (1.0 - lb) * _sigmoid(-fr)
    return q, sq, sf, f, k, jnp.log(f)


def _hgrn_intra(q, k, cums, consts):
    _, _, eye, same_block, upper = consts
    c = HGRN_CHUNK
    b = cums[:c]
    a = jnp.where(eye, _dot(_bf(q), _bf(k), NT), 0.0)
    saved = []
    for i in range(len(HGRN_LEVELS)):
        bref = cums[c * (i + 1):c * (i + 2)]
        up = upper[i]
        eq = jnp.where(up, jnp.exp(jnp.where(up, b - bref, 0.0)), 0.0)
        ek = jnp.where(up, 0.0, jnp.exp(jnp.where(up, 0.0, bref - b)))
        qt = q * eq
        kt = k * ek
        a = a + jnp.where(same_block[i], _dot(_bf(qt), _bf(kt), NT), 0.0)
        saved.append((eq, ek, qt, kt))
    return a, saved


def _hgrn_specs(tb, nb, rev):
    tmap = (lambda t: nb - 1 - t) if rev else (lambda t: t)
    zcol = lambda base: pl.BlockSpec((tb, HGRN_DIM), lambda h, t: (tmap(t), base + h))
    return zcol, [zcol(6), zcol(10), zcol(14), zcol(18),
                  pl.BlockSpec((1, HGRN_DIM), lambda h, t: (0, h)),
                  pl.BlockSpec((1, HGRN_DIM), lambda h, t: (0, 0))]


def _hgrn_fwd(z, lb, onw, name, exchange=None):
    T = z.shape[0]
    tb = min(HGRN_ROWS, T)
    nb, c, nc = T // tb, HGRN_CHUNK, min(HGRN_ROWS, T) // HGRN_CHUNK

    def body(qr_ref, fr_ref, ir_ref, gr_ref, lb_ref, onw_ref, rec_ref, o_ref, st_ref, state):
        @pl.when(pl.program_id(1) == 0)
        def _():
            state[...] = jnp.zeros_like(state)

        consts = _hgrn_consts()
        lbv = lb_ref[...]
        onwv = onw_ref[...]

        def chunk(ci, carry):
            sl = pl.ds(pl.multiple_of(ci * c, c), c)
            q, _, _, _, k, g = _hgrn_gates(qr_ref[sl, :], fr_ref[sl, :], lbv)
            v = _bf(ir_ref[sl, :])
            cums = _fold3(_dot(consts[0], _split3(g)))
            b = cums[:c]
            a, _ = _hgrn_intra(q, k, cums, consts)
            st = state[...]
            st_ref[ci] = st
            o = _dot(_bf(a), v) + _dot(_bf(q * jnp.exp(b)), _bf(st), NT)
            bl = b[c - 1:c, :]
            state[...] = st * jnp.exp(bl) + _dot(v, _bf(k * jnp.exp(bl - b)), TN)
            o_ref[sl, :] = o
            oh, _ = _rms(o)
            gr = gr_ref[sl, :]
            rec_ref[sl, :] = (oh * onwv * (gr * _sigmoid(gr))).astype(rec_ref.dtype)
            return carry

        lax.fori_loop(0, nc, chunk, 0)

    _, in_specs = _hgrn_specs(tb, nb, False)
    out_blk = pl.BlockSpec((tb, HGRN_DIM), lambda h, t: (t, h))
    return _hosted_call(
        body, name=name, grid=(HGRN_HEADS, nb), in_specs=in_specs,
        out_specs=[out_blk, out_blk, pl.BlockSpec((None, nc, HGRN_DIM, HGRN_DIM), lambda h, t: (h, t, 0, 0))],
        out_shape=[jax.ShapeDtypeStruct((T, 512), BF16), jax.ShapeDtypeStruct((T, 512), F32),
                   jax.ShapeDtypeStruct((HGRN_HEADS, T // c, HGRN_DIM, HGRN_DIM), F32)],
        scratch=[pltpu.VMEM((HGRN_DIM, HGRN_DIM), F32)], args=(z, z, z, z, lb, onw),
        semantics=("parallel", "arbitrary"), exchange=exchange)


def _hgrn_bwd(z, lb, onw, o, states, dcat, name, exchange=None):
    T = z.shape[0]
    tb = min(HGRN_ROWS, T)
    nb, c, nc = T // tb, HGRN_CHUNK, min(HGRN_ROWS, T) // HGRN_CHUNK

    def body(qr_ref, fr_ref, ir_ref, gr_ref, lb_ref, onw_ref, o_ref, st_ref, drec_ref,
             dqr_ref, dfr_ref, dir_ref, dgr_ref, dlb_ref, donw_ref, dstate):
        @pl.when(pl.program_id(1) == 0)
        def _():
            dstate[...] = jnp.zeros_like(dstate)
            dlb_ref[...] = jnp.zeros_like(dlb_ref)

        @pl.when((pl.program_id(0) == 0) & (pl.program_id(1) == 0))
        def _():
            donw_ref[...] = jnp.zeros_like(donw_ref)

        consts = _hgrn_consts()
        rev_mat, eye, same_block = consts[1], consts[2], consts[3]
        lbv = lb_ref[...]
        onwv = onw_ref[...]
        last = lax.broadcasted_iota(jnp.int32, (c, 1), 0) == c - 1

        def chunk(i, carry):
            ci = nc - 1 - i
            sl = pl.ds(pl.multiple_of(ci * c, c), c)
            qr, fr = qr_ref[sl, :], fr_ref[sl, :]
            q, sq, sf, f, k, g = _hgrn_gates(qr, fr, lbv)
            vf = ir_ref[sl, :]
            v = _bf(vf)
            cums = _fold3(_dot(consts[0], _split3(g)))
            b = cums[:c]
            a, saved = _hgrn_intra(q, k, cums, consts)
            st = st_ref[ci]
            dst = dstate[...]

            gr = gr_ref[sl, :]
            sg = _sigmoid(gr)
            ov = o_ref[sl, :]
            oh, r = _rms(ov)
            drec = drec_ref[sl, :].astype(F32)
            dgr_ref[sl, :] = (drec * oh * onwv * (sg * (1.0 + gr * (1.0 - sg)))).astype(dgr_ref.dtype)
            don = drec * (gr * sg)
            donw_ref[...] += jnp.sum(don * oh, axis=0, keepdims=True)
            do = _bf(_rms_bwd(don * onwv, oh, r))

            eb = jnp.exp(b)
            bl = b[c - 1:c, :]
            ebl = jnp.exp(bl)
            ekb = jnp.exp(bl - b)
            qe = q * eb
            ke = k * ekb
            da = _dot(do, v, NT)
            dv = _dot(_bf(a), do, TN) + _dot(_bf(ke), _bf(dst), NT)
            dqe = _dot(do, _bf(st))
            dke = _dot(v, _bf(dst))
            dstate[...] = dst * ebl + _dot(do, _bf(qe), TN)
            dq = dqe * eb
            dk = dke * ekb
            db_last = jnp.sum(dke * ke, axis=0, keepdims=True) + jnp.sum(dst * st, axis=0, keepdims=True) * ebl
            dat = _dot(v, do, NT)
            dad = jnp.sum(jnp.where(eye, da, 0.0), axis=1, keepdims=True)
            dq = dq + dad * k
            dk = dk + dad * q
            for lvl in range(len(HGRN_LEVELS)):
                eq, ek, qt, kt = saved[lvl]
                dq = dq + _dot_hilo(jnp.where(same_block[lvl], da, 0.0), kt) * eq
                dk = dk + _dot_hilo(jnp.where(same_block[lvl], dat, 0.0), qt) * ek
            db = q * dq - k * dk + jnp.where(last, db_last, 0.0)
            dg = _fold3(_dot(rev_mat, _split3(db)))

            dqr_ref[sl, :] = (dq * (HGRN_DIM ** -0.5) * (sq * (1.0 + qr * (1.0 - sq)))).astype(dqr_ref.dtype)
            dfk = dg / f - dk
            dfr_ref[sl, :] = ((1.0 - lbv) * sf * (1.0 - sf) * dfk).astype(dfr_ref.dtype)
            dlb_ref[...] += jnp.sum((1.0 - sf) * dfk, axis=0, keepdims=True)
            dir_ref[sl, :] = dv.astype(dir_ref.dtype)
            return carry

        lax.fori_loop(0, nc, chunk, 0)

    zcol, in_specs = _hgrn_specs(tb, nb, True)
    rblk = pl.BlockSpec((tb, HGRN_DIM), lambda h, t: (nb - 1 - t, h))
    in_specs = in_specs + [
        rblk,
        pl.BlockSpec((None, nc, HGRN_DIM, HGRN_DIM), lambda h, t: (h, nb - 1 - t, 0, 0)),
        pl.BlockSpec((tb, HGRN_DIM), lambda h, t: (nb - 1 - t, 4 + h)),
    ]
    return _hosted_call(
        body, name=name, grid=(HGRN_HEADS, nb), in_specs=in_specs,
        out_specs=[rblk, rblk, rblk, rblk, pl.BlockSpec((1, HGRN_DIM), lambda h, t: (0, h)),
                   pl.BlockSpec((1, HGRN_DIM), lambda h, t: (0, 0))],
        out_shape=[jax.ShapeDtypeStruct((T, 512), BF16)] * 4
        + [jax.ShapeDtypeStruct((1, 512), F32), jax.ShapeDtypeStruct((1, HGRN_DIM), F32)],
        scratch=[pltpu.VMEM((HGRN_DIM, HGRN_DIM), F32)], args=(z, z, z, z, lb, onw, o, states, dcat),
        semantics=("arbitrary", "arbitrary"), exchange=exchange)


def _lower_bound(logits, name):
    def body(l_ref, lb_ref):
        l0, l1 = l_ref[0:1, :], l_ref[1:2, :]
        m = jnp.maximum(l0, l1)
        e0, e1 = jnp.exp(l0 - m), jnp.exp(l1 - m)
        lb_ref[...] = e0 / (e0 + e1)

    return pl.pallas_call(
        body, name=name, out_shape=jax.ShapeDtypeStruct((1, logits.shape[1]), F32),
    )(logits)


def _lower_bound_bwd(lb, dlb, name):
    def body(lb_ref, dlb_ref, dl_ref):
        p = lb_ref[...]
        d0 = dlb_ref[...] * p * (1.0 - p)
        dl_ref[0:1, :] = d0
        dl_ref[1:2, :] = -d0

    return pl.pallas_call(
        body, name=name, out_shape=jax.ShapeDtypeStruct((2, lb.shape[1]), F32),
    )(lb, dlb)


CA_ROWS = 512


def _ca_fwd(q, k, v, name):
    T, W = q.shape
    M = k.shape[0]
    tq = min(CA_ROWS, T)
    scale = CA_HEAD_DIM ** -0.5

    def body(q_ref, k_ref, v_ref, o_ref):
        for h in range(CA_HEADS):
            hs = slice(CA_HEAD_DIM * h, CA_HEAD_DIM * (h + 1))
            s = _dot(q_ref[:, hs], k_ref[:, hs], NT) * scale
            p = jnp.exp(s - jnp.max(s, axis=-1, keepdims=True))
            p = p / jnp.sum(p, axis=-1, keepdims=True)
            o_ref[:, hs] = _dot(_bf(p), v_ref[:, hs]).astype(o_ref.dtype)

    full = pl.BlockSpec((M, W), lambda i: (0, 0))
    return pl.pallas_call(
        body, name=name, grid=(T // tq,), in_specs=[_row_spec(tq, W), full, full], out_specs=_row_spec(tq, W),
        out_shape=jax.ShapeDtypeStruct((T, W), BF16), compiler_params=_params("parallel"),
    )(q, k, v)


def _ca_bwd(q, k, v, do, name):
    T, W = q.shape
    M = k.shape[0]
    tq = min(CA_ROWS, T)
    scale = CA_HEAD_DIM ** -0.5

    def body(q_ref, k_ref, v_ref, do_ref, dq_ref, dk_ref, dv_ref):
        @pl.when(pl.program_id(0) == 0)
        def _():
            dk_ref[...] = jnp.zeros_like(dk_ref)
            dv_ref[...] = jnp.zeros_like(dv_ref)

        for h in range(CA_HEADS):
            hs = slice(CA_HEAD_DIM * h, CA_HEAD_DIM * (h + 1))
            qh, kh, vh, doh = q_ref[:, hs], k_ref[:, hs], v_ref[:, hs], do_ref[:, hs]
            s = _dot(qh, kh, NT) * scale
            p = jnp.exp(s - jnp.max(s, axis=-1, keepdims=True))
            p = p / jnp.sum(p, axis=-1, keepdims=True)
            dp = _dot(doh, vh, NT)
            ds = _bf(p * (dp - jnp.sum(p * dp, axis=-1, keepdims=True)) * scale)
            dq_ref[:, hs] = _dot(ds, kh).astype(dq_ref.dtype)
            dk_ref[:, hs] += _dot(ds, qh, TN)
            dv_ref[:, hs] += _dot(_bf(p), doh, TN)

    full = pl.BlockSpec((M, W), lambda i: (0, 0))
    return pl.pallas_call(
        body, name=name, grid=(T // tq,), in_specs=[_row_spec(tq, W), full, full, _row_spec(tq, W)],
        out_specs=[_row_spec(tq, W), full, full],
        out_shape=[jax.ShapeDtypeStruct((T, W), BF16), jax.ShapeDtypeStruct((M, W), F32), jax.ShapeDtypeStruct((M, W), F32)],
        compiler_params=_params("arbitrary"),
    )(q, k, v, do)


FFN_ROWS = 256
FFN_COLS = 1408
GELU_C0 = 0.7978845608028654
GELU_C1 = 0.044715


def _gelu(x):
    t = jnp.tanh(GELU_C0 * (x + GELU_C1 * x * x * x))
    return 0.5 * x * (1.0 + t), t


def _gelu_grad(x, t):
    return 0.5 * (1.0 + t) + 0.5 * x * (1.0 - t * t) * GELU_C0 * (1.0 + 3.0 * GELU_C1 * x * x)


def _shift_down(cur, halo, first, tb):
    row = lax.broadcasted_iota(jnp.int32, (tb, 1), 0)
    h6 = jnp.where(first, 0.0, halo[6:7])
    h7 = jnp.where(first, 0.0, halo[7:8])
    u1 = jnp.where(row == 0, h7, pltpu.roll(cur, 1, 0))
    u2 = jnp.where(row == 0, h6, jnp.where(row == 1, h7, pltpu.roll(cur, 2, 0)))
    return u1, u2


def _conv(u_ref, halo_ref, w_ref, b_ref, half, first, tb):
    cur = u_ref[half]
    u1, u2 = _shift_down(cur, halo_ref[half], first, tb)
    w = w_ref[...]
    return w[0:1] * u2 + w[1:2] * u1 + w[2:3] * cur + b_ref[...], cur, u1, u2


def _ffn_specs(tb, tc, rows_first):
    nj = D_FF // tc
    rc = (lambda a, b: (a, b)) if rows_first else (lambda a, b: (b, a))
    def at(f):
        return lambda a, b: f(*rc(a, b))
    blk = pl.BlockSpec((2, tb, tc), at(lambda t, j: (0, t, j)))
    halo = pl.BlockSpec((2, 8, tc), at(lambda t, j: (0, jnp.maximum(t * (tb // 8) - 1, 0), j)))
    wg = pl.BlockSpec((3, tc), at(lambda t, j: (0, j)))
    wv = pl.BlockSpec((3, tc), at(lambda t, j: (0, j + nj)))
    bg = pl.BlockSpec((1, tc), at(lambda t, j: (0, j)))
    bv = pl.BlockSpec((1, tc), at(lambda t, j: (0, j + nj)))
    flat = pl.BlockSpec((tb, tc), at(lambda t, j: (t, j)))
    return blk, halo, wg, wv, bg, bv, flat


def _glu_fwd(u, cw, cb, name):
    T = u.shape[1]
    tb, tc = min(FFN_ROWS, T), FFN_COLS

    def body(u_ref, halo_ref, wg_ref, wv_ref, bg_ref, bv_ref, a_ref):
        first = pl.program_id(0) == 0
        cg = _conv(u_ref, halo_ref, wg_ref, bg_ref, 0, first, tb)[0]
        cv = _conv(u_ref, halo_ref, wv_ref, bv_ref, 1, first, tb)[0]
        a_ref[...] = (_gelu(cg)[0] * cv).astype(a_ref.dtype)

    blk, halo, wg, wv, bg, bv, flat = _ffn_specs(tb, tc, True)
    return pl.pallas_call(
        body, name=name, grid=(T // tb, D_FF // tc), in_specs=[blk, halo, wg, wv, bg, bv], out_specs=flat,
        out_shape=jax.ShapeDtypeStruct((T, D_FF), BF16), compiler_params=_params("parallel", "parallel"),
    )(u, u, cw, cw, cb, cb)


def _glu_bwd(u, cw, cb, da, name):
    T = u.shape[1]
    tb, tc = min(FFN_ROWS, T), FFN_COLS

    def body(u_ref, halo_ref, wg_ref, wv_ref, bg_ref, bv_ref, da_ref, dc_ref, db_ref, dw_ref):
        first = pl.program_id(1) == 0

        @pl.when(first)
        def _():
            db_ref[...] = jnp.zeros_like(db_ref)
            dw_ref[...] = jnp.zeros_like(dw_ref)

        cg, ug, ug1, ug2 = _conv(u_ref, halo_ref, wg_ref, bg_ref, 0, first, tb)
        cv, uv, uv1, uv2 = _conv(u_ref, halo_ref, wv_ref, bv_ref, 1, first, tb)
        da = da_ref[...]
        gl, t = _gelu(cg)
        dcg = da * cv * _gelu_grad(cg, t)
        dcv = da * gl
        dc_ref[0] = dcg
        dc_ref[1] = dcv
        for half, dc, taps in ((0, dcg, (ug2, ug1, ug)), (1, dcv, (uv2, uv1, uv))):
            db_ref[half] += jnp.sum(dc, axis=0, keepdims=True)
            for tap in range(3):
                dw_ref[half, tap:tap + 1, :] += jnp.sum(dc * taps[tap], axis=0, keepdims=True)

    blk, halo, wg, wv, bg, bv, flat = _ffn_specs(tb, tc, False)
    return pl.pallas_call(
        body, name=name, grid=(D_FF // tc, T // tb), in_specs=[blk, halo, wg, wv, bg, bv, flat],
        out_specs=[blk, pl.BlockSpec((2, 1, tc), lambda j, t: (0, 0, j)), pl.BlockSpec((2, 3, tc), lambda j, t: (0, 0, j))],
        out_shape=[jax.ShapeDtypeStruct((2, T, D_FF), F32), jax.ShapeDtypeStruct((2, 1, D_FF), F32),
                   jax.ShapeDtypeStruct((2, 3, D_FF), F32)],
        compiler_params=_params("parallel", "arbitrary"),
    )(u, u, cw, cw, cb, cb, da)


def _conv_bwd(dc, cw, name):
    T = dc.shape[1]
    tb, tc = min(FFN_ROWS, T), FFN_COLS
    nt, nj = T // tb, D_FF // tc

    def body(dc_ref, halo_ref, wg_ref, wv_ref, du_ref):
        last = pl.program_id(0) == nt - 1
        row = lax.broadcasted_iota(jnp.int32, (tb, 1), 0)
        for half, w_ref in ((0, wg_ref), (1, wv_ref)):
            cur = dc_ref[half]
            halo = halo_ref[half]
            h0 = jnp.where(last, 0.0, halo[0:1])
            h1 = jnp.where(last, 0.0, halo[1:2])
            d1 = jnp.where(row == tb - 1, h0, pltpu.roll(cur, tb - 1, 0))
            d2 = jnp.where(row == tb - 1, h1, jnp.where(row == tb - 2, h0, pltpu.roll(cur, tb - 2, 0)))
            w = w_ref[...]
            du_ref[half] = (w[2:3] * cur + w[1:2] * d1 + w[0:1] * d2).astype(du_ref.dtype)

    blk = pl.BlockSpec((2, tb, tc), lambda t, j: (0, t, j))
    halo = pl.BlockSpec((2, 8, tc), lambda t, j: (0, jnp.minimum((t + 1) * (tb // 8), T // 8 - 1), j))
    wg = pl.BlockSpec((3, tc), lambda t, j: (0, j))
    wv = pl.BlockSpec((3, tc), lambda t, j: (0, j + nj))
    return pl.pallas_call(
        body, name=name, grid=(nt, nj), in_specs=[blk, halo, wg, wv], out_specs=blk,
        out_shape=jax.ShapeDtypeStruct((2, T, D_FF), BF16), compiler_params=_params("parallel", "parallel"),
    )(dc, dc, cw, cw)


def _mesh_pos():
    return lax.axis_index("x"), lax.axis_index("y"), lax.axis_index("c")


def _peer(pos, k):
    return (pos[0] ^ ((k >> 2) & 1), pos[1] ^ ((k >> 1) & 1), pos[2] ^ (k & 1))


def _index(pos):
    return 4 * pos[0] + 2 * pos[1] + pos[2]


class _Exchange:
    def __init__(self, kind, buf):
        assert kind in ("gather", "scatter")
        self.kind, self.buf = kind, buf
        self.out_shape = jax.ShapeDtypeStruct(((N_DEV,) + buf.shape) if kind == "gather" else buf.shape, buf.dtype)
        self.spec = pl.BlockSpec(memory_space=pl.ANY)
        self.scratch = [pltpu.SemaphoreType.DMA((N_DEV - 1,)), pltpu.SemaphoreType.DMA((N_DEV - 1,)),
                        pltpu.SemaphoreType.DMA]

    def _src(self, x_ref, dest):
        return x_ref if self.kind == "gather" else x_ref.at[dest]

    def _copies(self, x_ref, out_ref, send_sems, recv_sems, local_sem):
        pos = _mesh_pos()
        me = _index(pos)
        local = pltpu.make_async_copy(self._src(x_ref, me), out_ref.at[me], local_sem)
        sends, recvs = [], []
        for k in range(1, N_DEV):
            peer = _peer(pos, k)
            sends.append(pltpu.make_async_remote_copy(
                src_ref=self._src(x_ref, _index(peer)), dst_ref=out_ref.at[me], send_sem=send_sems.at[k - 1],
                recv_sem=recv_sems.at[k - 1], device_id=peer, device_id_type=pl.DeviceIdType.MESH))
            recvs.append(pltpu.make_async_remote_copy(
                src_ref=self._src(x_ref, me), dst_ref=out_ref.at[_index(peer)], send_sem=send_sems.at[k - 1],
                recv_sem=recv_sems.at[k - 1], device_id=peer, device_id_type=pl.DeviceIdType.MESH))
        return local, sends, recvs

    def start(self, *refs):
        local, sends, _ = self._copies(*refs)
        local.start()
        for cp in sends:
            cp.start()

    def finish(self, *refs):
        local, sends, recvs = self._copies(*refs)
        for cp in recvs:
            cp.wait_recv()
        for cp in sends:
            cp.wait_send()
        local.wait()


def _hosted_call(body, *, name, grid, in_specs, out_specs, out_shape, scratch, args, semantics, exchange=None):
    if exchange is None:
        return pl.pallas_call(
            body, name=name, grid=grid, in_specs=in_specs, out_specs=out_specs, out_shape=out_shape,
            scratch_shapes=scratch, compiler_params=_params(*semantics))(*args)
    n_in, n_out, n_scr = len(in_specs), len(out_specs), len(scratch)

    def hosted(*refs):
        ins, x_ref = refs[:n_in], refs[n_in]
        outs, land_ref = refs[n_in + 1:n_in + 1 + n_out], refs[n_in + 1 + n_out]
        rest = refs[n_in + n_out + 2:]
        sems = rest[n_scr:]
        ids = [pl.program_id(a) for a in range(len(grid))]
        first, last = ids[0] == 0, ids[0] == grid[0] - 1
        for a in range(1, len(grid)):
            first, last = first & (ids[a] == 0), last & (ids[a] == grid[a] - 1)

        @pl.when(first)
        def _():
            exchange.start(x_ref, land_ref, *sems)

        body(*ins, *outs, *rest[:n_scr])

        @pl.when(last)
        def _():
            exchange.finish(x_ref, land_ref, *sems)

    return pl.pallas_call(
        hosted, name=name, grid=grid, in_specs=list(in_specs) + [exchange.spec],
        out_specs=list(out_specs) + [exchange.spec], out_shape=list(out_shape) + [exchange.out_shape],
        scratch_shapes=list(scratch) + exchange.scratch, compiler_params=_params(*(["arbitrary"] * len(grid))),
    )(*args, exchange.buf)


def _exchange_alone(exchange, name):
    def body(x_ref, out_ref, send_sems, recv_sems, local_sem):
        exchange.start(x_ref, out_ref, send_sems, recv_sems, local_sem)
        exchange.finish(x_ref, out_ref, send_sems, recv_sems, local_sem)

    return pl.pallas_call(
        body, name=name, out_shape=exchange.out_shape, in_specs=[exchange.spec], out_specs=exchange.spec,
        scratch_shapes=exchange.scratch)(exchange.buf)


def _adamw(w, g, m, v):
    m = ADAM_B1 * m + (1.0 - ADAM_B1) * g
    v = ADAM_B2 * v + (1.0 - ADAM_B2) * (g * g)
    m_hat = m / (1.0 - ADAM_B1 ** ADAM_STEP)
    v_hat = v / (1.0 - ADAM_B2 ** ADAM_STEP)
    delta = -ADAM_LR * (m_hat / (jnp.sqrt(v_hat) + ADAM_EPS) + ADAM_WD * w)
    return delta, m, v


def _sum_adamw(parts, w, m, v, name):
    R, C = w.shape
    tr = max(t for t in range(16, ROWS + 1, 16) if R % t == 0)

    def body(p_ref, w_ref, m_ref, v_ref, g_ref, d_ref, mo_ref, vo_ref):
        g = p_ref[0].astype(F32)
        for i in range(1, N_DEV):
            g = g + p_ref[i].astype(F32)
        g_ref[...] = g
        d_ref[...], mo_ref[...], vo_ref[...] = _adamw(w_ref[...], g, m_ref[...], v_ref[...])

    row = _row_spec(tr, C)
    return pl.pallas_call(
        body, name=name, grid=(R // tr,),
        in_specs=[pl.BlockSpec((N_DEV, tr, C), lambda i: (0, i, 0)), row, row, row], out_specs=[row] * 4,
        out_shape=[jax.ShapeDtypeStruct((R, C), F32)] * 4, compiler_params=_params("parallel"),
    )(parts, w, m, v)


def _sum_parts(parts, name):
    _, R, C = parts.shape

    def body(p_ref, g_ref):
        g = p_ref[0]
        for i in range(1, N_DEV):
            g = g + p_ref[i]
        g_ref[...] = g

    return pl.pallas_call(body, name=name, out_shape=jax.ShapeDtypeStruct((R, C), F32))(parts)


def _adamw_call(w, g, m, v, name):
    def body(w_ref, g_ref, m_ref, v_ref, d_ref, mo_ref, vo_ref):
        d_ref[...], mo_ref[...], vo_ref[...] = _adamw(w_ref[...], g_ref[...], m_ref[...], v_ref[...])

    return pl.pallas_call(body, name=name, out_shape=[jax.ShapeDtypeStruct(w.shape, F32)] * 3)(w, g, m, v)


BIG = ("w_in", "w_out", "ca_wq", "ca_wk", "ca_wv", "ca_wo", "ffn_w_up", "ffn_w_down")
BIG_FULL = {"w_in": (1024, 2816), "w_out": (1024, 1024), "ca_wq": (1024, 1024), "ca_wk": (1024, 1024),
            "ca_wv": (1024, 1024), "ca_wo": (1024, 1024), "ffn_w_up": (1024, 5632), "ffn_w_down": (2816, 1024)}
FIRST = ("w_in",)
LATER = BIG[1:]
COL_SHARDED = ("w_in", "ffn_w_up")
PACK_COLS = 1024
NORMS = ("mix_pre_norm", "mix_post_norm", "ca_pre_norm", "mem_norm", "ca_post_norm", "ffn_pre_norm", "ffn_post_norm")
SMALL_ROWS = 32


def _big_rows(name):
    r, c = BIG_FULL[name]
    return r * c // N_DEV // PACK_COLS


def _pack_shards(shards, names):
    return jnp.concatenate([shards[n].reshape(_big_rows(n), PACK_COLS) for n in names], axis=0)


def _unpack_shards(pack, shapes, names):
    out, r0 = {}, 0
    for n in names:
        out[n] = pack[r0:r0 + _big_rows(n)].reshape(shapes[n])
        r0 += _big_rows(n)
    return out


def _unpack_gathered(gathered, names):
    out, r0 = {}, 0
    for n in names:
        rows = _big_rows(n)
        blk = gathered[:, r0:r0 + rows]
        r, c = BIG_FULL[n]
        if n in COL_SHARDED:
            out[n] = blk.reshape(N_DEV, r, c // N_DEV).transpose(1, 0, 2).reshape(r, c)
        else:
            out[n] = blk.reshape(r, c)
        r0 += rows
    return out


def _pack_full_grads(grads, names):
    parts = []
    for n in names:
        r, c = BIG_FULL[n]
        g = grads[n]
        if n in COL_SHARDED:
            g = g.reshape(r, N_DEV, c // N_DEV).transpose(1, 0, 2)
        parts.append(g.reshape(N_DEV, _big_rows(n), PACK_COLS))
    return jnp.concatenate(parts, axis=1).astype(BF16)


def _pad_row(vec):
    vec = vec.reshape(-1)
    n = -(-vec.shape[0] // PACK_COLS) * PACK_COLS
    return jnp.pad(vec, (0, n - vec.shape[0])).reshape(-1, PACK_COLS)


def _pack_small(norms, logits, out_norm, sinks, loss, conv_b, conv_w):
    rows = [_pad_row(norms[n]) for n in NORMS]
    rows.append(_pad_row(logits))
    rows.append(_pad_row(jnp.concatenate([out_norm.reshape(-1), sinks.reshape(-1), loss.reshape(-1)])))
    rows.append(_pad_row(conv_b))
    rows.append(_pad_row(conv_w))
    pack = jnp.concatenate(rows, axis=0)
    return jnp.pad(pack, ((0, SMALL_ROWS - pack.shape[0]), (0, 0)))


def _unpack_small(pack):
    norms = {n: pack[i:i + 1] for i, n in enumerate(NORMS)}
    logits = pack[7].reshape(2, 512)
    out_norm = pack[8:9, 0:128]
    sinks = pack[8:9, 128:136]
    loss = pack[8, 136]
    conv_b = pack[9:15].reshape(-1)[:2 * D_FF].reshape(1, 2 * D_FF)
    conv_w = pack[15:32].reshape(-1)[:6 * D_FF].reshape(3, 2 * D_FF)
    return norms, logits, out_norm, sinks, loss, conv_b, conv_w


def kernel(x, mem, mix_pre_norm, w_in, attn_sinks, hgrn_lb_logits, hgrn_out_norm, w_out, mix_post_norm, ca_pre_norm, mem_norm, ca_wq, ca_wk, ca_wv, ca_wo, ca_post_norm, ffn_pre_norm, ffn_w_up, ffn_conv_w, ffn_conv_b, ffn_w_down, ffn_post_norm, loss_target, m_mix_pre_norm, m_w_in, m_attn_sinks, m_hgrn_lb_logits, m_hgrn_out_norm, m_w_out, m_mix_post_norm, m_ca_pre_norm, m_mem_norm, m_ca_wq, m_ca_wk, m_ca_wv, m_ca_wo, m_ca_post_norm, m_ffn_pre_norm, m_ffn_w_up, m_ffn_conv_w, m_ffn_conv_b, m_ffn_w_down, m_ffn_post_norm, v_mix_pre_norm, v_w_in, v_attn_sinks, v_hgrn_lb_logits, v_hgrn_out_norm, v_w_out, v_mix_post_norm, v_ca_pre_norm, v_mem_norm, v_ca_wq, v_ca_wk, v_ca_wv, v_ca_wo, v_ca_post_norm, v_ffn_pre_norm, v_ffn_w_up, v_ffn_conv_w, v_ffn_conv_b, v_ffn_w_down, v_ffn_post_norm):
    names = ["mix_pre_norm", "w_in", "attn_sinks", "hgrn_lb_logits", "hgrn_out_norm", "w_out", "mix_post_norm",
             "ca_pre_norm", "mem_norm", "ca_wq", "ca_wk", "ca_wv", "ca_wo", "ca_post_norm", "ffn_pre_norm",
             "ffn_w_up", "ffn_conv_w", "ffn_conv_b", "ffn_w_down", "ffn_post_norm"]
    w_all = dict(zip(names, [mix_pre_norm, w_in, attn_sinks, hgrn_lb_logits, hgrn_out_norm, w_out, mix_post_norm,
                             ca_pre_norm, mem_norm, ca_wq, ca_wk, ca_wv, ca_wo, ca_post_norm, ffn_pre_norm,
                             ffn_w_up, ffn_conv_w, ffn_conv_b, ffn_w_down, ffn_post_norm]))
    m_all = dict(zip(names, [m_mix_pre_norm, m_w_in, m_attn_sinks, m_hgrn_lb_logits, m_hgrn_out_norm, m_w_out,
                             m_mix_post_norm, m_ca_pre_norm, m_mem_norm, m_ca_wq, m_ca_wk, m_ca_wv, m_ca_wo,
                             m_ca_post_norm, m_ffn_pre_norm, m_ffn_w_up, m_ffn_conv_w, m_ffn_conv_b, m_ffn_w_down,
                             m_ffn_post_norm]))
    v_all = dict(zip(names, [v_mix_pre_norm, v_w_in, v_attn_sinks, v_hgrn_lb_logits, v_hgrn_out_norm, v_w_out,
                             v_mix_post_norm, v_ca_pre_norm, v_mem_norm, v_ca_wq, v_ca_wk, v_ca_wv, v_ca_wo,
                             v_ca_post_norm, v_ffn_pre_norm, v_ffn_w_up, v_ffn_conv_w, v_ffn_conv_b, v_ffn_w_down,
                             v_ffn_post_norm]))
    dev = _index(_mesh_pos())

    shards = {n: w_all[n][0] for n in BIG}
    w_packs = {grp: _pack_shards(shards, grp) for grp in (FIRST, LATER)}
    conv_w_rows = _exchange_alone(_Exchange("gather", _pad_row(ffn_conv_w[0])), "gather_conv_w")
    conv_w_full = conv_w_rows.reshape(N_DEV, -1)[:, :3 * 704].reshape(N_DEV, 3, 704).transpose(1, 0, 2).reshape(3, 2 * D_FF)

    received, grads_small, loss_local, grad_x = _local_step(
        x[0], mem[0], loss_target[0], {grp: w_packs[grp].astype(BF16) for grp in w_packs}, conv_w_full,
        {n: w_all[n] for n in NORMS}, attn_sinks, hgrn_lb_logits, hgrn_out_norm, ffn_conv_b)

    small_pack = _pack_small(grads_small["norms"], grads_small["logits"], grads_small["out_norm"], grads_small["sinks"],
                             loss_local, grads_small["conv_b"], grads_small["conv_w"])
    small_sum = _sum_parts(_exchange_alone(_Exchange("gather", small_pack), "gather_small"), "sum_small")
    g_norms, g_logits, g_out_norm, g_sinks, loss, g_conv_b, g_conv_w_full = _unpack_small(small_sum)
    g_conv_w = lax.dynamic_slice_in_dim(g_conv_w_full, dev * 704, 704, axis=1)

    shard_shapes = {n: w_all[n].shape for n in BIG}
    out_g, out_d, out_m, out_v = {}, {}, {}, {}
    for grp, tag in ((FIRST, "first"), (LATER, "later")):
        packs = _sum_adamw(
            received[grp], w_packs[grp], _pack_shards({n: m_all[n][0] for n in grp}, grp),
            _pack_shards({n: v_all[n][0] for n in grp}, grp), "adamw_" + tag)
        for tree, pack in zip((out_g, out_d, out_m, out_v), packs):
            tree.update(_unpack_shards(pack, shard_shapes, grp))

    small_g = dict(g_norms)
    small_g.update(attn_sinks=g_sinks, hgrn_lb_logits=g_logits, hgrn_out_norm=g_out_norm,
                   ffn_conv_b=g_conv_b, ffn_conv_w=g_conv_w[None])
    small_names = [n for n in names if n not in BIG]

    def small_pack_of(tree):
        return jnp.concatenate([_pad_row(tree[n]) for n in small_names], axis=0)

    ds, ms, vs = _adamw_call(small_pack_of(w_all), small_pack_of(small_g), small_pack_of(m_all), small_pack_of(v_all),
                             "adamw_small")

    def small_unpack(pack):
        out, r0 = {}, 0
        for n in small_names:
            size = 1
            for s in w_all[n].shape:
                size *= s
            rows = -(-size // PACK_COLS)
            out[n] = pack[r0:r0 + rows].reshape(-1)[:size].reshape(w_all[n].shape)
            r0 += rows
        return out

    sd, sm, sv = small_unpack(ds), small_unpack(ms), small_unpack(vs)
    for n in small_names:
        out_g[n] = small_g[n].reshape(w_all[n].shape)
        out_d[n], out_m[n], out_v[n] = sd[n], sm[n], sv[n]

    return (loss, grad_x[None], *[out_g[n] for n in names], *[out_d[n] for n in names],
            *[out_m[n] for n in names], *[out_v[n] for n in names])


def _local_step(x, mem, target, w_packs, conv_w, norms, sinks, lb_logits, out_norm, conv_b):
    g1, g2, g3 = norms["mix_pre_norm"], norms["mix_post_norm"], norms["ca_pre_norm"]
    g4, g5, g6, g7 = norms["mem_norm"], norms["ca_post_norm"], norms["ffn_pre_norm"], norms["ffn_post_norm"]

    h1, gathered = _norm_fwd(x, g1, "mix_norm", exchange=_Exchange("gather", w_packs[FIRST]))
    w_in = _unpack_gathered(gathered, FIRST)["w_in"]
    z = _mm(h1, w_in, mode="nn", out_dtype=F32, name="in_proj", tn=1408)
    attn, lse = _swa_fwd(z, sinks, "swa_fwd")
    lb = _lower_bound(lb_logits, "lower_bound")
    rec, o_rec, states, gathered = _hgrn_fwd(z, lb, out_norm, "hgrn_fwd", exchange=_Exchange("gather", w_packs[LATER]))
    wf = _unpack_gathered(gathered, LATER)
    w_out, wq, wk, wv, wo = wf["w_out"], wf["ca_wq"], wf["ca_wk"], wf["ca_wv"], wf["ca_wo"]
    w_up, w_down = wf["ffn_w_up"], wf["ffn_w_down"]
    cat = jnp.concatenate([attn, rec], axis=1)
    mix = _mm(cat, w_out, mode="nn", out_dtype=F32, name="out_proj")
    x1, h2 = _post_pre(x, mix, g2, g3, "mix_post")
    mem_n = _norm_fwd(mem, g4, "mem_norm")
    q = _mm(h2, wq, mode="nn", out_dtype=BF16, name="ca_q")
    k = _mm(mem_n, wk, mode="nn", out_dtype=BF16, name="ca_k")
    v = _mm(mem_n, wv, mode="nn", out_dtype=BF16, name="ca_v")
    oc = _ca_fwd(q, k, v, "ca_fwd")
    c = _mm(oc, wo, mode="nn", out_dtype=F32, name="ca_o")
    x2, h3 = _post_pre(x1, c, g5, g6, "ca_post")
    u = _mm(h3, w_up, mode="nn", out_dtype=F32, name="ffn_up", tn=1408, split_out=True)
    a = _glu_fwd(u, conv_w, conv_b, "glu_fwd")
    y = _mm(a, w_down, mode="nn", out_dtype=F32, name="ffn_down", tk=1408)
    loss, dx3, dy, dg7 = _final(x2, y, g7, target, "loss_head")

    da = _mm(dy, w_down, mode="nt", out_dtype=F32, name="ffn_down_dx", tn=1408)
    d_w_down = _mm(a, dy, mode="tn", out_dtype=F32, name="ffn_down_dw", tm=1408, tk=512)
    dc, d_cb, d_cw = _glu_bwd(u, conv_w, conv_b, da, "glu_bwd")
    du = _conv_bwd(dc, conv_w, "conv_bwd")
    dh3 = _mm(du, w_up, mode="nt", out_dtype=F32, name="ffn_up_dx", tk=1408, split_a=True)
    d_w_up = _mm(h3, du, mode="tn", out_dtype=F32, name="ffn_up_dw", tm=1024, tn=1408, tk=512, split_b=True)
    dx2, dcv, dg6, dg5 = _norm_bwd2(dx3, dh3, x2, g6, c, g5, "ca_post_bwd")
    doc = _mm(dcv, wo, mode="nt", out_dtype=BF16, name="ca_o_dx")
    d_wo = _mm(oc, dcv, mode="tn", out_dtype=F32, name="ca_o_dw", tm=1024, tk=512)
    dq, dk, dv = _ca_bwd(q, k, v, doc, "ca_bwd")
    d_wq = _mm(h2, dq, mode="tn", out_dtype=F32, name="ca_q_dw", tm=1024, tk=512)
    dh2 = _mm(dq, wq, mode="nt", out_dtype=F32, name="ca_q_dx")
    d_wk = _mm(mem_n, dk, mode="tn", out_dtype=F32, name="ca_k_dw", tm=1024)
    d_wv = _mm(mem_n, dv, mode="tn", out_dtype=F32, name="ca_v_dw", tm=1024)
    dmem_k = _mm(dk, wk, mode="nt", out_dtype=F32, name="ca_k_dx")
    dmem_v = _mm(dv, wv, mode="nt", out_dtype=F32, name="ca_v_dx")
    dg4 = _gain_bwd(mem, dmem_k, dmem_v, "mem_norm_bwd")
    dx1, dmix, dg3, dg2 = _norm_bwd2(dx2, dh2, x1, g3, mix, g2, "mix_post_bwd")
    dcat = _mm(dmix, w_out, mode="nt", out_dtype=F32, name="out_proj_dx")
    d_w_out = _mm(cat, dmix, mode="tn", out_dtype=F32, name="out_proj_dw", tm=1024, tk=512)
    later = {"w_out": d_w_out, "ca_wq": d_wq, "ca_wk": d_wk, "ca_wv": d_wv, "ca_wo": d_wo,
             "ffn_w_up": d_w_up, "ffn_w_down": d_w_down}
    dqr, dfr, dir_, dgr, dlb, donw, got_later = _hgrn_bwd(
        z, lb, out_norm, o_rec, states, dcat, "hgrn_bwd", exchange=_Exchange("scatter", _pack_full_grads(later, LATER)))
    dq_a, dka, dkb, dva, dvb, dsinks = _swa_bwd(z, sinks, dcat, lse, "swa_bwd")
    dz = _assemble_dz(dq_a, dka, dkb, dva, dvb, dqr, dfr, dir_, dgr, "assemble_dz")
    d_w_in = _mm(h1, dz, mode="tn", out_dtype=F32, name="in_proj_dw", tm=1024, tn=1408, tk=512)
    dh1, got_first = _mm(dz, w_in, mode="nt", out_dtype=F32, name="in_proj_dx", tk=1408,
                         exchange=_Exchange("scatter", _pack_full_grads({"w_in": d_w_in}, FIRST)))
    dx, dg1 = _norm_bwd1(dx1, dh1, x, g1, "mix_norm_bwd")

    small = {
        "norms": {"mix_pre_norm": dg1, "mix_post_norm": dg2, "ca_pre_norm": dg3, "mem_norm": dg4,
                  "ca_post_norm": dg5, "ffn_pre_norm": dg6, "ffn_post_norm": dg7},
        "logits": _lower_bound_bwd(lb, dlb, "lower_bound_bwd"),
        "out_norm": donw,
        "sinks": dsinks,
        "conv_b": jnp.concatenate([d_cb[0], d_cb[1]], axis=1),
        "conv_w": jnp.concatenate([d_cw[0], d_cw[1]], axis=1),
    }
    return {FIRST: got_first, LATER: got_later}, small, loss[0, 0:1], dx
```

```python
import jax
import jax.numpy as jnp
from jax import lax
from jax.experimental import pallas as pl
from jax.experimental.pallas import tpu as pltpu

F32 = jnp.float32
BF16 = jnp.bfloat16
EPS = 1e-6
N_DEV = 8
MESH_AXES = ("x", "y", "c")

ATTN_HEAD_DIM = 64
ATTN_Q_HEADS = 8
ATTN_KV_HEADS = 2
ATTN_BLOCK = 128
HGRN_HEADS = 4
HGRN_DIM = 128
HGRN_CHUNK = 64
HGRN_PAIR = 2
HGRN_LEVELS = (32, 16, 8, 4, 2, 1)
CA_HEADS = 4
CA_HEAD_DIM = 256
D_FF = 2816

ADAM_LR = 0.001
ADAM_B1 = 0.9
ADAM_B2 = 0.999
ADAM_EPS = 1e-08
ADAM_WD = 0.01
ADAM_STEP = 10

VMEM_LIMIT = 56 << 20
LANE = 128

NT = (((1,), (1,)), ((), ()))
TN = (((0,), (0,)), ((), ()))


def _params(*sem):
    return pltpu.CompilerParams(dimension_semantics=sem, vmem_limit_bytes=VMEM_LIMIT)


def _tile(n, cap):
    if n <= cap:
        return n
    best = 0
    for t in range(LANE, cap + 1, LANE):
        if n % t == 0:
            best = t
    assert best, (n, cap)
    return best


def _dot(a, b, dims=None):
    if dims is None:
        return jnp.dot(a, b, preferred_element_type=F32)
    return lax.dot_general(a, b, dims, preferred_element_type=F32)


def _bf(x):
    return x.astype(BF16)


def _sigmoid(x):
    return 1.0 / (1.0 + jnp.exp(-x))


def _rms(x):
    r = lax.rsqrt(jnp.mean(x * x, axis=-1, keepdims=True) + EPS)
    return x * r, r


def _rms_bwd(dxh, xh, r):
    return r * (dxh - xh * jnp.mean(dxh * xh, axis=-1, keepdims=True))


def _mm(a, b, *, mode, out_dtype, name, tm=512, tn=1024, tk=1024, split_a=False, split_b=False, split_out=False,
        exchange=None):
    def dims(arr, split):
        if split:
            return arr.shape[1], 2 * arr.shape[2]
        return arr.shape

    ar, ac = dims(a, split_a)
    br, bc = dims(b, split_b)
    if mode == "nn":
        M, K, N = ar, ac, bc
        assert br == K
    elif mode == "nt":
        M, K, N = ar, ac, br
        assert bc == K
    else:
        K, M, N = ar, ac, bc
        assert br == K
    a_cols_half = ac // 2 if split_a else None
    b_cols_half = bc // 2 if split_b else None
    tm = _tile(M, tm)
    tn = _tile((N // 2) if (split_out or (split_b and mode != "nt")) else N, tn)
    tk = _tile((K // 2) if ((split_a and mode != "tn") or (split_b and mode == "nt")) else K, tk)
    if split_a and mode == "tn":
        tm = _tile(M // 2, tm)
    gm, gn, gk = M // tm, N // tn, K // tk
    a_bytes, b_bytes = a.size * a.dtype.itemsize, b.size * b.dtype.itemsize
    rows_outer = gk > 1 or a_bytes + gm * b_bytes <= gn * a_bytes + b_bytes
    grid = (gm, gn, gk) if rows_outer else (gn, gm, gk)

    def spec(split, half, blk, rc):
        def imap(p, q, k):
            r, c = rc(*((p, q) if rows_outer else (q, p)), k)
            if not split:
                return (r, c)
            per_half = half // blk[1]
            return (c // per_half, r, c % per_half)

        return pl.BlockSpec(((None,) + blk) if split else blk, imap)

    if mode == "nn":
        a_spec = spec(split_a, a_cols_half, (tm, tk), lambda i, j, k: (i, k))
        b_spec = spec(split_b, b_cols_half, (tk, tn), lambda i, j, k: (k, j))
        dn = None
    elif mode == "nt":
        a_spec = spec(split_a, a_cols_half, (tm, tk), lambda i, j, k: (i, k))
        b_spec = spec(split_b, b_cols_half, (tn, tk), lambda i, j, k: (j, k))
        dn = NT
    else:
        a_spec = spec(split_a, a_cols_half, (tk, tm), lambda i, j, k: (k, i))
        b_spec = spec(split_b, b_cols_half, (tk, tn), lambda i, j, k: (k, j))
        dn = TN
    o_spec = spec(split_out, N // 2 if split_out else None, (tm, tn), lambda i, j, k: (i, j))
    out_shape = (2, M, N // 2) if split_out else (M, N)

    if gk == 1:
        def body(a_ref, b_ref, o_ref):
            o_ref[...] = _dot(_bf(a_ref[...]), _bf(b_ref[...]), dn).astype(o_ref.dtype)
        scratch = []
    else:
        def body(a_ref, b_ref, o_ref, acc_ref):
            k = pl.program_id(2)

            @pl.when(k == 0)
            def _():
                acc_ref[...] = jnp.zeros_like(acc_ref)

            acc_ref[...] += _dot(_bf(a_ref[...]), _bf(b_ref[...]), dn)

            @pl.when(k == gk - 1)
            def _():
                o_ref[...] = acc_ref[...].astype(o_ref.dtype)
        scratch = [pltpu.VMEM((tm, tn), F32)]

    out = _hosted_call(
        body, name=name, grid=grid, in_specs=[a_spec, b_spec], out_specs=[o_spec],
        out_shape=[jax.ShapeDtypeStruct(out_shape, out_dtype)], scratch=scratch, args=(a, b),
        semantics=("parallel", "parallel", "arbitrary"), exchange=exchange)
    return out[0] if exchange is None else out


ROWS = 256


def _row_spec(tr, cols):
    return pl.BlockSpec((tr, cols), lambda i: (i, 0))


def _vec_spec(cols):
    return pl.BlockSpec((1, cols), lambda i: (0, 0))


def _norm_fwd(x, g, name, exchange=None):
    T, Dm = x.shape
    tr = min(ROWS, T)

    def body(x_ref, g_ref, h_ref):
        xh, _ = _rms(x_ref[...])
        h_ref[...] = (xh * g_ref[...]).astype(h_ref.dtype)

    out = _hosted_call(
        body, name=name, grid=(T // tr,), in_specs=[_row_spec(tr, Dm), _vec_spec(Dm)], out_specs=[_row_spec(tr, Dm)],
        out_shape=[jax.ShapeDtypeStruct((T, Dm), BF16)], scratch=[], args=(x, g), semantics=("parallel",),
        exchange=exchange)
    return out[0] if exchange is None else out


def _post_pre(x, m, g_post, g_pre, name):
    T, Dm = x.shape
    tr = min(ROWS, T)

    def body(x_ref, m_ref, gp_ref, gn_ref, xo_ref, h_ref):
        mh, _ = _rms(m_ref[...])
        xn = x_ref[...] + mh * gp_ref[...]
        xo_ref[...] = xn
        xh, _ = _rms(xn)
        h_ref[...] = (xh * gn_ref[...]).astype(h_ref.dtype)

    return pl.pallas_call(
        body, name=name, grid=(T // tr,),
        in_specs=[_row_spec(tr, Dm), _row_spec(tr, Dm), _vec_spec(Dm), _vec_spec(Dm)],
        out_specs=[_row_spec(tr, Dm), _row_spec(tr, Dm)],
        out_shape=[jax.ShapeDtypeStruct((T, Dm), F32), jax.ShapeDtypeStruct((T, Dm), BF16)],
        compiler_params=_params("parallel"),
    )(x, m, g_post, g_pre)


def _final(x2, y, g_post, target, name):
    T, Dm = x2.shape
    tr = min(ROWS, T)

    def body(x_ref, y_ref, g_ref, t_ref, loss_ref, dx_ref, dy_ref, dg_ref):
        @pl.when(pl.program_id(0) == 0)
        def _():
            loss_ref[...] = jnp.zeros_like(loss_ref)
            dg_ref[...] = jnp.zeros_like(dg_ref)

        g = g_ref[...]
        yh, r = _rms(y_ref[...])
        d = x_ref[...] + yh * g - t_ref[...]
        loss_ref[...] += jnp.zeros((1, LANE), F32) + 0.5 * jnp.sum(jnp.mean(d * d, axis=-1, keepdims=True))
        dx = d * (1.0 / Dm)
        dx_ref[...] = dx
        dy_ref[...] = _rms_bwd(dx * g, yh, r).astype(dy_ref.dtype)
        dg_ref[...] += jnp.sum(dx * yh, axis=0, keepdims=True)

    return pl.pallas_call(
        body, name=name, grid=(T // tr,),
        in_specs=[_row_spec(tr, Dm), _row_spec(tr, Dm), _vec_spec(Dm), _row_spec(tr, Dm)],
        out_specs=[_vec_spec(LANE), _row_spec(tr, Dm), _row_spec(tr, Dm), _vec_spec(Dm)],
        out_shape=[jax.ShapeDtypeStruct((1, LANE), F32), jax.ShapeDtypeStruct((T, Dm), F32),
                   jax.ShapeDtypeStruct((T, Dm), BF16), jax.ShapeDtypeStruct((1, Dm), F32)],
        compiler_params=_params("arbitrary"),
    )(x2, y, g_post, target)


def _norm_bwd2(dx_cur, dh, x_prev, g_pre, m_prev, g_post, name):
    T, Dm = x_prev.shape
    tr = min(ROWS, T)

    def body(dx_ref, dh_ref, x_ref, gn_ref, m_ref, gp_ref, dxo_ref, dm_ref, dgn_ref, dgp_ref):
        @pl.when(pl.program_id(0) == 0)
        def _():
            dgn_ref[...] = jnp.zeros_like(dgn_ref)
            dgp_ref[...] = jnp.zeros_like(dgp_ref)

        dh = dh_ref[...].astype(F32)
        xh, r = _rms(x_ref[...])
        dx = dx_ref[...] + _rms_bwd(dh * gn_ref[...], xh, r)
        dxo_ref[...] = dx
        dgn_ref[...] += jnp.sum(dh * xh, axis=0, keepdims=True)
        mh, rm = _rms(m_ref[...])
        dm_ref[...] = _rms_bwd(dx * gp_ref[...], mh, rm).astype(dm_ref.dtype)
        dgp_ref[...] += jnp.sum(dx * mh, axis=0, keepdims=True)

    return pl.pallas_call(
        body, name=name, grid=(T // tr,),
        in_specs=[_row_spec(tr, Dm), _row_spec(tr, Dm), _row_spec(tr, Dm), _vec_spec(Dm), _row_spec(tr, Dm), _vec_spec(Dm)],
        out_specs=[_row_spec(tr, Dm), _row_spec(tr, Dm), _vec_spec(Dm), _vec_spec(Dm)],
        out_shape=[jax.ShapeDtypeStruct((T, Dm), F32), jax.ShapeDtypeStruct((T, Dm), BF16),
                   jax.ShapeDtypeStruct((1, Dm), F32), jax.ShapeDtypeStruct((1, Dm), F32)],
        compiler_params=_params("arbitrary"),
    )(dx_cur, dh, x_prev, g_pre, m_prev, g_post)


def _norm_bwd1(dx_cur, dh, x_prev, g_pre, name):
    T, Dm = x_prev.shape
    tr = min(ROWS, T)

    def body(dx_ref, dh_ref, x_ref, gn_ref, dxo_ref, dgn_ref):
        @pl.when(pl.program_id(0) == 0)
        def _():
            dgn_ref[...] = jnp.zeros_like(dgn_ref)

        dh = dh_ref[...].astype(F32)
        xh, r = _rms(x_ref[...])
        dxo_ref[...] = dx_ref[...] + _rms_bwd(dh * gn_ref[...], xh, r)
        dgn_ref[...] += jnp.sum(dh * xh, axis=0, keepdims=True)

    return pl.pallas_call(
        body, name=name, grid=(T // tr,),
        in_specs=[_row_spec(tr, Dm), _row_spec(tr, Dm), _row_spec(tr, Dm), _vec_spec(Dm)],
        out_specs=[_row_spec(tr, Dm), _vec_spec(Dm)],
        out_shape=[jax.ShapeDtypeStruct((T, Dm), F32), jax.ShapeDtypeStruct((1, Dm), F32)],
        compiler_params=_params("arbitrary"),
    )(dx_cur, dh, x_prev, g_pre)


def _gain_bwd(x, dh_a, dh_b, name):
    T, Dm = x.shape

    def body(x_ref, a_ref, b_ref, dg_ref):
        xh, _ = _rms(x_ref[...])
        dg_ref[...] = jnp.sum((a_ref[...] + b_ref[...]) * xh, axis=0, keepdims=True)

    return pl.pallas_call(
        body, name=name, grid=(1,), in_specs=[_row_spec(T, Dm)] * 3, out_specs=_vec_spec(Dm),
        out_shape=jax.ShapeDtypeStruct((1, Dm), F32), compiler_params=_params("arbitrary"),
    )(x, dh_a, dh_b)


def _swa_mask(n):
    row = lax.broadcasted_iota(jnp.int32, (ATTN_BLOCK, 2 * ATTN_BLOCK), 0)
    col = lax.broadcasted_iota(jnp.int32, (ATTN_BLOCK, 2 * ATTN_BLOCK), 1)
    diff = row + ATTN_BLOCK - col
    return (diff >= 0) & (diff < ATTN_BLOCK) & ((col >= ATTN_BLOCK) | (n > 0))


def _swa_specs():
    blk = ATTN_BLOCK
    prev = lambda n: jnp.maximum(n - 1, 0)
    return [
        pl.BlockSpec(memory_space=pltpu.SMEM),
        pl.BlockSpec((blk, 512), lambda n: (n, 0)),
        pl.BlockSpec((blk, 128), lambda n: (prev(n), 4)),
        pl.BlockSpec((blk, 128), lambda n: (n, 4)),
        pl.BlockSpec((blk, 128), lambda n: (prev(n), 5)),
        pl.BlockSpec((blk, 128), lambda n: (n, 5)),
    ]


def _swa_fwd(z, sinks, name):
    T = z.shape[0]
    blk, hd = ATTN_BLOCK, ATTN_HEAD_DIM
    scale = hd ** -0.5

    def body(sink_ref, q_ref, kp_ref, kc_ref, vp_ref, vc_ref, o_ref, lse_ref):
        allowed = _swa_mask(pl.program_id(0))
        for hk in range(ATTN_KV_HEADS):
            ks = slice(hd * hk, hd * hk + hd)
            k = _bf(jnp.concatenate([kp_ref[:, ks], kc_ref[:, ks]], axis=0))
            v = _bf(jnp.concatenate([vp_ref[:, ks], vc_ref[:, ks]], axis=0))
            for g in range(ATTN_Q_HEADS // ATTN_KV_HEADS):
                h = hk * (ATTN_Q_HEADS // ATTN_KV_HEADS) + g
                hs = slice(hd * h, hd * h + hd)
                s = _dot(_bf(q_ref[:, hs]), k, NT) * scale
                s = jnp.where(allowed, s, -1e30)
                sink = sink_ref[0, h]
                m = jnp.maximum(jnp.max(s, axis=-1, keepdims=True), sink)
                p = jnp.exp(s - m)
                l = jnp.sum(p, axis=-1, keepdims=True) + jnp.exp(sink - m)
                o_ref[:, hs] = _dot(_bf(p / l), v).astype(o_ref.dtype)
                lse_ref[:, h:h + 1] = m + jnp.log(l)

    return pl.pallas_call(
        body, name=name, grid=(T // blk,), in_specs=_swa_specs(),
        out_specs=[pl.BlockSpec((blk, 512), lambda n: (n, 0)), pl.BlockSpec((blk, ATTN_Q_HEADS), lambda n: (n, 0))],
        out_shape=[jax.ShapeDtypeStruct((T, 512), BF16), jax.ShapeDtypeStruct((T, ATTN_Q_HEADS), F32)],
        compiler_params=_params("parallel"),
    )(sinks, z, z, z, z, z)


def _swa_bwd(z, sinks, dcat, lse, name):
    T = z.shape[0]
    blk, hd = ATTN_BLOCK, ATTN_HEAD_DIM
    scale = hd ** -0.5
    group = ATTN_Q_HEADS // ATTN_KV_HEADS

    def body(sink_ref, q_ref, kp_ref, kc_ref, vp_ref, vc_ref, do_ref, lse_ref,
             dq_ref, dka_ref, dkb_ref, dva_ref, dvb_ref, dsink_ref):
        @pl.when(pl.program_id(0) == 0)
        def _():
            dsink_ref[...] = jnp.zeros_like(dsink_ref)

        allowed = _swa_mask(pl.program_id(0))
        lane = lax.broadcasted_iota(jnp.int32, (1, ATTN_Q_HEADS), 1)
        dsink = jnp.zeros((1, ATTN_Q_HEADS), F32)
        for hk in range(ATTN_KV_HEADS):
            ks = slice(hd * hk, hd * hk + hd)
            k = _bf(jnp.concatenate([kp_ref[:, ks], kc_ref[:, ks]], axis=0))
            v = _bf(jnp.concatenate([vp_ref[:, ks], vc_ref[:, ks]], axis=0))
            dk = jnp.zeros((2 * blk, hd), F32)
            dv = jnp.zeros((2 * blk, hd), F32)
            for g in range(group):
                h = hk * group + g
                hs = slice(hd * h, hd * h + hd)
                qh = _bf(q_ref[:, hs])
                doh = _bf(do_ref[:, hs])
                lse_h = lse_ref[:, h:h + 1]
                s = _dot(qh, k, NT) * scale
                p = jnp.where(allowed, jnp.exp(jnp.where(allowed, s, -1e30) - lse_h), 0.0)
                dp = _dot(doh, v, NT)
                delta = jnp.sum(p * dp, axis=-1, keepdims=True)
                ds = _bf(p * (dp - delta) * scale)
                dq_ref[:, hs] = _dot(ds, k).astype(dq_ref.dtype)
                dk = dk + _dot(ds, qh, TN)
                dv = dv + _dot(_bf(p), doh, TN)
                p_sink = jnp.exp(sink_ref[0, h] - lse_h)
                dsink = dsink + jnp.where(lane == h, -jnp.sum(p_sink * delta), 0.0)
            dkb_ref[:, ks] = dk[:blk]
            dka_ref[:, ks] = dk[blk:]
            dvb_ref[:, ks] = dv[:blk]
            dva_ref[:, ks] = dv[blk:]
        dsink_ref[...] += dsink

    kv_out = pl.BlockSpec((blk, 128), lambda n: (n, 0))
    return pl.pallas_call(
        body, name=name, grid=(T // blk,),
        in_specs=_swa_specs() + [pl.BlockSpec((blk, 512), lambda n: (n, 0)),
                                 pl.BlockSpec((blk, ATTN_Q_HEADS), lambda n: (n, 0))],
        out_specs=[pl.BlockSpec((blk, 512), lambda n: (n, 0)), kv_out, kv_out, kv_out, kv_out,
                   pl.BlockSpec((1, ATTN_Q_HEADS), lambda n: (0, 0))],
        out_shape=[jax.ShapeDtypeStruct((T, 512), BF16)] + [jax.ShapeDtypeStruct((T, 128), F32)] * 4
        + [jax.ShapeDtypeStruct((1, ATTN_Q_HEADS), F32)],
        compiler_params=_params("arbitrary"),
    )(sinks, z, z, z, z, z, dcat, lse)


def _assemble_dz(dq_a, dka, dkb, dva, dvb, dqr, dfr, dir_, dgr, name):
    T = dq_a.shape[0]
    blk = ATTN_BLOCK
    nb = T // blk

    def body(dq_ref, dka_ref, dkb_ref, dva_ref, dvb_ref, dqr_ref, dfr_ref, dir_ref, dgr_ref, o_ref):
        has_next = pl.program_id(0) < nb - 1
        o_ref[:, 0:512] = dq_ref[...]
        o_ref[:, 512:640] = (dka_ref[...] + jnp.where(has_next, dkb_ref[...], 0.0)).astype(o_ref.dtype)
        o_ref[:, 640:768] = (dva_ref[...] + jnp.where(has_next, dvb_ref[...], 0.0)).astype(o_ref.dtype)
        o_ref[:, 768:1280] = dqr_ref[...]
        o_ref[:, 1280:1792] = dfr_ref[...]
        o_ref[:, 1792:2304] = dir_ref[...]
        o_ref[:, 2304:2816] = dgr_ref[...]

    cur = lambda w: pl.BlockSpec((blk, w), lambda n: (n, 0))
    nxt = pl.BlockSpec((blk, 128), lambda n: (jnp.minimum(n + 1, nb - 1), 0))
    return pl.pallas_call(
        body, name=name, grid=(nb,),
        in_specs=[cur(512), cur(128), nxt, cur(128), nxt, cur(512), cur(512), cur(512), cur(512)],
        out_specs=pl.BlockSpec((blk, 2816), lambda n: (n, 0)),
        out_shape=jax.ShapeDtypeStruct((T, 2816), BF16), compiler_params=_params("parallel"),
    )(dq_a, dka, dkb, dva, dvb, dqr, dfr, dir_, dgr)


HGRN_ROWS = 512


def _hgrn_consts():
    c = HGRN_CHUNK
    r = lax.broadcasted_iota(jnp.int32, (c, c), 0)
    s = lax.broadcasted_iota(jnp.int32, (c, c), 1)
    rcol = lax.broadcasted_iota(jnp.int32, (c, 1), 0)
    same_block, upper = [], []
    for m in HGRN_LEVELS:
        same_block.append((r & ~(2 * m - 1)) == (s & ~(2 * m - 1)))
        upper.append((rcol & (2 * m - 1)) >= m)
    cum_mat = jnp.where(s <= r, 1.0, 0.0).astype(BF16)
    rev_mat = jnp.where(s >= r, 1.0, 0.0).astype(BF16)
    return cum_mat, rev_mat, r == s, same_block, upper, rcol & 3


def _hgrn_level_decay(g, b, m, pos4):
    c = HGRN_CHUNK
    if m == 1:
        return jnp.exp(jnp.where((pos4 & 1) == 1, g, 0.0))
    if m == 2:
        after, before = pltpu.roll(g, c - 1, 0), pltpu.roll(g, 1, 0)
        return jnp.exp(jnp.where(pos4 == 0, after, jnp.where(pos4 == 1, 0.0, jnp.where(pos4 == 2, g, g + before))))
    b3 = b.reshape(c // (2 * m), 2 * m, HGRN_DIM)
    bref = jnp.broadcast_to(b3[:, m - 1:m, :], b3.shape).reshape(c, HGRN_DIM)
    return jnp.exp(-jnp.abs(b - bref))


def _split3(x):
    hi = _bf(x)
    r1 = x - hi.astype(F32)
    mid = _bf(r1)
    lo = _bf(r1 - mid.astype(F32))
    return jnp.concatenate([hi, mid, lo], axis=1)


def _dot_hilo(a, b):
    r, c = a.shape[0], b.shape[1]
    a_hi, b_hi = _bf(a), _bf(b)
    a2 = jnp.concatenate([a_hi, _bf(a - a_hi.astype(F32))], axis=0)
    b2 = jnp.concatenate([b_hi, _bf(b - b_hi.astype(F32))], axis=1)
    y = _dot(a2, b2)
    return y[:r, :c] + y[:r, c:] + y[r:, :c]


def _fold3(y):
    w = y.shape[1] // 3
    return y[:, :w] + y[:, w:2 * w] + y[:, 2 * w:]


def _hgrn_gates(qr, fr, lb):
    sq = _sigmoid(qr)
    q = qr * sq * (HGRN_DIM ** -0.5)
    sf = _sigmoid(fr)
    f = lb + (1.0 - lb) * sf
    k = (1.0 - lb) * _sigmoid(-fr)
    return q, sq, sf, f, k, jnp.log(f)


def _hgrn_intra(q, k, g, b, consts):
    _, _, eye, same_block, upper, pos4 = consts
    a = jnp.where(eye, _dot(_bf(q), _bf(k), NT), 0.0)
    saved = []
    for i, m in enumerate(HGRN_LEVELS):
        up = upper[i]
        e = _hgrn_level_decay(g, b, m, pos4)
        qt = jnp.where(up, q * e, 0.0)
        kt = jnp.where(up, 0.0, k * e)
        a = a + jnp.where(same_block[i], _dot(_bf(qt), _bf(kt), NT), 0.0)
        saved.append((e, qt, kt))
    return a, saved


def _hgrn_specs(tb, nb, rev):
    tmap = (lambda t: nb - 1 - t) if rev else (lambda t: t)
    w = HGRN_PAIR * HGRN_DIM
    zcol = lambda base: pl.BlockSpec((tb, w), lambda h, t: (tmap(t), base // HGRN_PAIR + h))
    return zcol, [zcol(6), zcol(10), zcol(14), zcol(18),
                  pl.BlockSpec((1, w), lambda h, t: (0, h)),
                  pl.BlockSpec((1, HGRN_DIM), lambda h, t: (0, 0))]


def _hgrn_fwd(z, lb, onw, name, exchange=None):
    T = z.shape[0]
    tb = min(HGRN_ROWS, T)
    nb, c, nc = T // tb, HGRN_CHUNK, min(HGRN_ROWS, T) // HGRN_CHUNK

    def body(qr_ref, fr_ref, ir_ref, gr_ref, lb_ref, onw_ref, rec_ref, o_ref, st_ref, state):
        @pl.when(pl.program_id(1) == 0)
        def _():
            state[...] = jnp.zeros_like(state)

        consts = _hgrn_consts()
        lbv = lb_ref[...]
        onwv = onw_ref[...]

        def chunk(ci, carry):
            sl = pl.ds(pl.multiple_of(ci * c, c), c)
            for hh in range(HGRN_PAIR):
                ls = slice(HGRN_DIM * hh, HGRN_DIM * (hh + 1))
                q, _, _, _, k, g = _hgrn_gates(qr_ref[sl, ls], fr_ref[sl, ls], lbv[:, ls])
                v = _bf(ir_ref[sl, ls])
                b = _fold3(_dot(consts[0], _split3(g)))
                a, _ = _hgrn_intra(q, k, g, b, consts)
                st = state[hh]
                st_ref[hh, ci] = st
                o = _dot(_bf(a), v) + _dot(_bf(q * jnp.exp(b)), _bf(st), NT)
                bl = b[c - 1:c, :]
                state[hh] = st * jnp.exp(bl) + _dot(v, _bf(k * jnp.exp(bl - b)), TN)
                o_ref[sl, ls] = o
                oh, _ = _rms(o)
                gr = gr_ref[sl, ls]
                rec_ref[sl, ls] = (oh * onwv * (gr * _sigmoid(gr))).astype(rec_ref.dtype)
            return carry

        lax.fori_loop(0, nc, chunk, 0)

    _, in_specs = _hgrn_specs(tb, nb, False)
    out_blk = pl.BlockSpec((tb, HGRN_PAIR * HGRN_DIM), lambda h, t: (t, h))
    return _hosted_call(
        body, name=name, grid=(HGRN_HEADS // HGRN_PAIR, nb), in_specs=in_specs,
        out_specs=[out_blk, out_blk, pl.BlockSpec((HGRN_PAIR, nc, HGRN_DIM, HGRN_DIM), lambda h, t: (h, t, 0, 0))],
        out_shape=[jax.ShapeDtypeStruct((T, 512), BF16), jax.ShapeDtypeStruct((T, 512), F32),
                   jax.ShapeDtypeStruct((HGRN_HEADS, T // c, HGRN_DIM, HGRN_DIM), F32)],
        scratch=[pltpu.VMEM((HGRN_PAIR, HGRN_DIM, HGRN_DIM), F32)], args=(z, z, z, z, lb, onw),
        semantics=("parallel", "arbitrary"), exchange=exchange)


def _hgrn_bwd(z, lb, onw, o, states, dcat, name, exchange=None):
    T = z.shape[0]
    tb = min(HGRN_ROWS, T)
    nb, c, nc = T // tb, HGRN_CHUNK, min(HGRN_ROWS, T) // HGRN_CHUNK

    def body(qr_ref, fr_ref, ir_ref, gr_ref, lb_ref, onw_ref, o_ref, st_ref, drec_ref,
             dqr_ref, dfr_ref, dir_ref, dgr_ref, dlb_ref, donw_ref, dstate):
        @pl.when(pl.program_id(1) == 0)
        def _():
            dstate[...] = jnp.zeros_like(dstate)
            dlb_ref[...] = jnp.zeros_like(dlb_ref)

        @pl.when((pl.program_id(0) == 0) & (pl.program_id(1) == 0))
        def _():
            donw_ref[...] = jnp.zeros_like(donw_ref)

        consts = _hgrn_consts()
        rev_mat, eye, same_block, upper = consts[1:5]
        lbv = lb_ref[...]
        onwv = onw_ref[...]
        last = lax.broadcasted_iota(jnp.int32, (c, 1), 0) == c - 1

        def one_head(sl, ls, hh, ci):
            qr, fr = qr_ref[sl, ls], fr_ref[sl, ls]
            q, sq, sf, f, k, g = _hgrn_gates(qr, fr, lbv[:, ls])
            v = _bf(ir_ref[sl, ls])
            b = _fold3(_dot(consts[0], _split3(g)))
            a, saved = _hgrn_intra(q, k, g, b, consts)
            st = st_ref[hh, ci]
            dst = dstate[hh]

            gr = gr_ref[sl, ls]
            sg = _sigmoid(gr)
            ov = o_ref[sl, ls]
            oh, r = _rms(ov)
            drec = drec_ref[sl, ls].astype(F32)
            dgr_ref[sl, ls] = (drec * oh * onwv * (sg * (1.0 + gr * (1.0 - sg)))).astype(dgr_ref.dtype)
            don = drec * (gr * sg)
            donw_ref[...] += jnp.sum(don * oh, axis=0, keepdims=True)
            do = _bf(_rms_bwd(don * onwv, oh, r))

            eb = jnp.exp(b)
            bl = b[c - 1:c, :]
            ebl = jnp.exp(bl)
            ekb = jnp.exp(bl - b)
            qe = q * eb
            ke = k * ekb
            da = _dot(do, v, NT)
            dv = _dot(_bf(a), do, TN) + _dot(_bf(ke), _bf(dst), NT)
            dqe = _dot(do, _bf(st))
            dke = _dot(v, _bf(dst))
            dstate[hh] = dst * ebl + _dot(do, _bf(qe), TN)
            dq = dqe * eb
            dk = dke * ekb
            db_last = jnp.sum(dke * ke, axis=0, keepdims=True) + jnp.sum(dst * st, axis=0, keepdims=True) * ebl
            dat = _dot(v, do, NT)
            dad = jnp.sum(jnp.where(eye, da, 0.0), axis=1, keepdims=True)
            dq = dq + dad * k
            dk = dk + dad * q
            for lvl in range(len(HGRN_LEVELS)):
                e, qt, kt = saved[lvl]
                dq = dq + jnp.where(upper[lvl], _dot_hilo(jnp.where(same_block[lvl], da, 0.0), kt) * e, 0.0)
                dk = dk + jnp.where(upper[lvl], 0.0, _dot_hilo(jnp.where(same_block[lvl], dat, 0.0), qt) * e)
            db = q * dq - k * dk + jnp.where(last, db_last, 0.0)
            dg = _fold3(_dot(rev_mat, _split3(db)))

            dqr_ref[sl, ls] = (dq * (HGRN_DIM ** -0.5) * (sq * (1.0 + qr * (1.0 - sq)))).astype(dqr_ref.dtype)
            dfk = dg / f - dk
            dfr_ref[sl, ls] = ((1.0 - lbv[:, ls]) * sf * (1.0 - sf) * dfk).astype(dfr_ref.dtype)
            dlb_ref[:, ls] += jnp.sum((1.0 - sf) * dfk, axis=0, keepdims=True)
            dir_ref[sl, ls] = dv.astype(dir_ref.dtype)

        def chunk(i, carry):
            ci = nc - 1 - i
            sl = pl.ds(pl.multiple_of(ci * c, c), c)
            for hh in range(HGRN_PAIR):
                one_head(sl, slice(HGRN_DIM * hh, HGRN_DIM * (hh + 1)), hh, ci)
            return carry

        lax.fori_loop(0, nc, chunk, 0)

    zcol, in_specs = _hgrn_specs(tb, nb, True)
    rblk = pl.BlockSpec((tb, HGRN_PAIR * HGRN_DIM), lambda h, t: (nb - 1 - t, h))
    in_specs = in_specs + [
        rblk,
        pl.BlockSpec((HGRN_PAIR, nc, HGRN_DIM, HGRN_DIM), lambda h, t: (h, nb - 1 - t, 0, 0)),
        pl.BlockSpec((tb, HGRN_PAIR * HGRN_DIM), lambda h, t: (nb - 1 - t, 4 // HGRN_PAIR + h)),
    ]
    return _hosted_call(
        body, name=name, grid=(HGRN_HEADS // HGRN_PAIR, nb), in_specs=in_specs,
        out_specs=[rblk, rblk, rblk, rblk, pl.BlockSpec((1, HGRN_PAIR * HGRN_DIM), lambda h, t: (0, h)),
                   pl.BlockSpec((1, HGRN_DIM), lambda h, t: (0, 0))],
        out_shape=[jax.ShapeDtypeStruct((T, 512), BF16)] * 4
        + [jax.ShapeDtypeStruct((1, 512), F32), jax.ShapeDtypeStruct((1, HGRN_DIM), F32)],
        scratch=[pltpu.VMEM((HGRN_PAIR, HGRN_DIM, HGRN_DIM), F32)], args=(z, z, z, z, lb, onw, o, states, dcat),
        semantics=("arbitrary", "arbitrary"), exchange=exchange)


def _lower_bound(logits, name):
    def body(l_ref, lb_ref):
        l0, l1 = l_ref[0:1, :], l_ref[1:2, :]
        m = jnp.maximum(l0, l1)
        e0, e1 = jnp.exp(l0 - m), jnp.exp(l1 - m)
        lb_ref[...] = e0 / (e0 + e1)

    return pl.pallas_call(
        body, name=name, out_shape=jax.ShapeDtypeStruct((1, logits.shape[1]), F32),
    )(logits)


def _lower_bound_bwd(lb, dlb, name):
    def body(lb_ref, dlb_ref, dl_ref):
        p = lb_ref[...]
        d0 = dlb_ref[...] * p * (1.0 - p)
        dl_ref[0:1, :] = d0
        dl_ref[1:2, :] = -d0

    return pl.pallas_call(
        body, name=name, out_shape=jax.ShapeDtypeStruct((2, lb.shape[1]), F32),
    )(lb, dlb)


CA_ROWS = 512


def _ca_fwd(q, k, v, name):
    T, W = q.shape
    M = k.shape[0]
    tq = min(CA_ROWS, T)
    scale = CA_HEAD_DIM ** -0.5

    def body(q_ref, k_ref, v_ref, o_ref):
        for h in range(CA_HEADS):
            hs = slice(CA_HEAD_DIM * h, CA_HEAD_DIM * (h + 1))
            s = _dot(q_ref[:, hs], k_ref[:, hs], NT) * scale
            p = jnp.exp(s - jnp.max(s, axis=-1, keepdims=True))
            p = p / jnp.sum(p, axis=-1, keepdims=True)
            o_ref[:, hs] = _dot(_bf(p), v_ref[:, hs]).astype(o_ref.dtype)

    full = pl.BlockSpec((M, W), lambda i: (0, 0))
    return pl.pallas_call(
        body, name=name, grid=(T // tq,), in_specs=[_row_spec(tq, W), full, full], out_specs=_row_spec(tq, W),
        out_shape=jax.ShapeDtypeStruct((T, W), BF16), compiler_params=_params("parallel"),
    )(q, k, v)


def _ca_bwd(q, k, v, do, name):
    T, W = q.shape
    M = k.shape[0]
    tq = min(CA_ROWS, T)
    scale = CA_HEAD_DIM ** -0.5

    def body(q_ref, k_ref, v_ref, do_ref, dq_ref, dk_ref, dv_ref):
        @pl.when(pl.program_id(0) == 0)
        def _():
            dk_ref[...] = jnp.zeros_like(dk_ref)
            dv_ref[...] = jnp.zeros_like(dv_ref)

        for h in range(CA_HEADS):
            hs = slice(CA_HEAD_DIM * h, CA_HEAD_DIM * (h + 1))
            qh, kh, vh, doh = q_ref[:, hs], k_ref[:, hs], v_ref[:, hs], do_ref[:, hs]
            s = _dot(qh, kh, NT) * scale
            p = jnp.exp(s - jnp.max(s, axis=-1, keepdims=True))
            p = p / jnp.sum(p, axis=-1, keepdims=True)
            dp = _dot(doh, vh, NT)
            ds = _bf(p * (dp - jnp.sum(p * dp, axis=-1, keepdims=True)) * scale)
            dq_ref[:, hs] = _dot(ds, kh).astype(dq_ref.dtype)
            dk_ref[:, hs] += _dot(ds, qh, TN)
            dv_ref[:, hs] += _dot(_bf(p), doh, TN)

    full = pl.BlockSpec((M, W), lambda i: (0, 0))
    return pl.pallas_call(
        body, name=name, grid=(T // tq,), in_specs=[_row_spec(tq, W), full, full, _row_spec(tq, W)],
        out_specs=[_row_spec(tq, W), full, full],
        out_shape=[jax.ShapeDtypeStruct((T, W), BF16), jax.ShapeDtypeStruct((M, W), F32), jax.ShapeDtypeStruct((M, W), F32)],
        compiler_params=_params("arbitrary"),
    )(q, k, v, do)


FFN_ROWS = 256
FFN_COLS = 1408
GELU_C0 = 0.7978845608028654
GELU_C1 = 0.044715


def _gelu(x):
    t = jnp.tanh(GELU_C0 * (x + GELU_C1 * x * x * x))
    return 0.5 * x * (1.0 + t), t


def _gelu_grad(x, t):
    return 0.5 * (1.0 + t) + 0.5 * x * (1.0 - t * t) * GELU_C0 * (1.0 + 3.0 * GELU_C1 * x * x)


def _shift_down(cur, halo, first, tb):
    row = lax.broadcasted_iota(jnp.int32, (tb, 1), 0)
    h6 = jnp.where(first, 0.0, halo[6:7])
    h7 = jnp.where(first, 0.0, halo[7:8])
    u1 = jnp.where(row == 0, h7, pltpu.roll(cur, 1, 0))
    u2 = jnp.where(row == 0, h6, jnp.where(row == 1, h7, pltpu.roll(cur, 2, 0)))
    return u1, u2


def _conv(u_ref, halo_ref, w_ref, b_ref, half, first, tb):
    cur = u_ref[half]
    u1, u2 = _shift_down(cur, halo_ref[half], first, tb)
    w = w_ref[...]
    return w[0:1] * u2 + w[1:2] * u1 + w[2:3] * cur + b_ref[...], cur, u1, u2


def _ffn_specs(tb, tc, rows_first):
    nj = D_FF // tc
    rc = (lambda a, b: (a, b)) if rows_first else (lambda a, b: (b, a))
    def at(f):
        return lambda a, b: f(*rc(a, b))
    blk = pl.BlockSpec((2, tb, tc), at(lambda t, j: (0, t, j)))
    halo = pl.BlockSpec((2, 8, tc), at(lambda t, j: (0, jnp.maximum(t * (tb // 8) - 1, 0), j)))
    wg = pl.BlockSpec((3, tc), at(lambda t, j: (0, j)))
    wv = pl.BlockSpec((3, tc), at(lambda t, j: (0, j + nj)))
    bg = pl.BlockSpec((1, tc), at(lambda t, j: (0, j)))
    bv = pl.BlockSpec((1, tc), at(lambda t, j: (0, j + nj)))
    flat = pl.BlockSpec((tb, tc), at(lambda t, j: (t, j)))
    return blk, halo, wg, wv, bg, bv, flat


def _glu_fwd(u, cw, cb, name):
    T = u.shape[1]
    tb, tc = min(FFN_ROWS, T), FFN_COLS

    def body(u_ref, halo_ref, wg_ref, wv_ref, bg_ref, bv_ref, a_ref):
        first = pl.program_id(0) == 0
        cg = _conv(u_ref, halo_ref, wg_ref, bg_ref, 0, first, tb)[0]
        cv = _conv(u_ref, halo_ref, wv_ref, bv_ref, 1, first, tb)[0]
        a_ref[...] = (_gelu(cg)[0] * cv).astype(a_ref.dtype)

    blk, halo, wg, wv, bg, bv, flat = _ffn_specs(tb, tc, True)
    return pl.pallas_call(
        body, name=name, grid=(T // tb, D_FF // tc), in_specs=[blk, halo, wg, wv, bg, bv], out_specs=flat,
        out_shape=jax.ShapeDtypeStruct((T, D_FF), BF16), compiler_params=_params("parallel", "parallel"),
    )(u, u, cw, cw, cb, cb)


def _glu_bwd(u, cw, cb, da, name):
    T = u.shape[1]
    tb, tc = min(FFN_ROWS, T), FFN_COLS

    def body(u_ref, halo_ref, wg_ref, wv_ref, bg_ref, bv_ref, da_ref, dc_ref, db_ref, dw_ref):
        first = pl.program_id(1) == 0

        @pl.when(first)
        def _():
            db_ref[...] = jnp.zeros_like(db_ref)
            dw_ref[...] = jnp.zeros_like(dw_ref)

        cg, ug, ug1, ug2 = _conv(u_ref, halo_ref, wg_ref, bg_ref, 0, first, tb)
        cv, uv, uv1, uv2 = _conv(u_ref, halo_ref, wv_ref, bv_ref, 1, first, tb)
        da = da_ref[...]
        gl, t = _gelu(cg)
        dcg = da * cv * _gelu_grad(cg, t)
        dcv = da * gl
        dc_ref[0] = dcg
        dc_ref[1] = dcv
        for half, dc, taps in ((0, dcg, (ug2, ug1, ug)), (1, dcv, (uv2, uv1, uv))):
            db_ref[half] += jnp.sum(dc, axis=0, keepdims=True)
            for tap in range(3):
                dw_ref[half, tap:tap + 1, :] += jnp.sum(dc * taps[tap], axis=0, keepdims=True)

    blk, halo, wg, wv, bg, bv, flat = _ffn_specs(tb, tc, False)
    return pl.pallas_call(
        body, name=name, grid=(D_FF // tc, T // tb), in_specs=[blk, halo, wg, wv, bg, bv, flat],
        out_specs=[blk, pl.BlockSpec((2, 1, tc), lambda j, t: (0, 0, j)), pl.BlockSpec((2, 3, tc), lambda j, t: (0, 0, j))],
        out_shape=[jax.ShapeDtypeStruct((2, T, D_FF), F32), jax.ShapeDtypeStruct((2, 1, D_FF), F32),
                   jax.ShapeDtypeStruct((2, 3, D_FF), F32)],
        compiler_params=_params("parallel", "arbitrary"),
    )(u, u, cw, cw, cb, cb, da)


def _conv_bwd(dc, cw, name):
    T = dc.shape[1]
    tb, tc = min(FFN_ROWS, T), FFN_COLS
    nt, nj = T // tb, D_FF // tc

    def body(dc_ref, halo_ref, wg_ref, wv_ref, du_ref):
        last = pl.program_id(0) == nt - 1
        row = lax.broadcasted_iota(jnp.int32, (tb, 1), 0)
        for half, w_ref in ((0, wg_ref), (1, wv_ref)):
            cur = dc_ref[half]
            halo = halo_ref[half]
            h0 = jnp.where(last, 0.0, halo[0:1])
            h1 = jnp.where(last, 0.0, halo[1:2])
            d1 = jnp.where(row == tb - 1, h0, pltpu.roll(cur, tb - 1, 0))
            d2 = jnp.where(row == tb - 1, h1, jnp.where(row == tb - 2, h0, pltpu.roll(cur, tb - 2, 0)))
            w = w_ref[...]
            du_ref[half] = (w[2:3] * cur + w[1:2] * d1 + w[0:1] * d2).astype(du_ref.dtype)

    blk = pl.BlockSpec((2, tb, tc), lambda t, j: (0, t, j))
    halo = pl.BlockSpec((2, 8, tc), lambda t, j: (0, jnp.minimum((t + 1) * (tb // 8), T // 8 - 1), j))
    wg = pl.BlockSpec((3, tc), lambda t, j: (0, j))
    wv = pl.BlockSpec((3, tc), lambda t, j: (0, j + nj))
    return pl.pallas_call(
        body, name=name, grid=(nt, nj), in_specs=[blk, halo, wg, wv], out_specs=blk,
        out_shape=jax.ShapeDtypeStruct((2, T, D_FF), BF16), compiler_params=_params("parallel", "parallel"),
    )(dc, dc, cw, cw)


def _mesh_pos():
    return lax.axis_index("x"), lax.axis_index("y"), lax.axis_index("c")


def _peer(pos, k):
    return (pos[0] ^ ((k >> 2) & 1), pos[1] ^ ((k >> 1) & 1), pos[2] ^ (k & 1))


def _index(pos):
    return 4 * pos[0] + 2 * pos[1] + pos[2]


class _Exchange:
    def __init__(self, kind, buf):
        assert kind in ("gather", "scatter")
        self.kind, self.buf = kind, buf
        self.out_shape = jax.ShapeDtypeStruct(((N_DEV,) + buf.shape) if kind == "gather" else buf.shape, buf.dtype)
        self.spec = pl.BlockSpec(memory_space=pl.ANY)
        self.scratch = [pltpu.SemaphoreType.DMA((N_DEV - 1,)), pltpu.SemaphoreType.DMA((N_DEV - 1,)),
                        pltpu.SemaphoreType.DMA]

    def _src(self, x_ref, dest):
        return x_ref if self.kind == "gather" else x_ref.at[dest]

    def _copies(self, x_ref, out_ref, send_sems, recv_sems, local_sem):
        pos = _mesh_pos()
        me = _index(pos)
        local = pltpu.make_async_copy(self._src(x_ref, me), out_ref.at[me], local_sem)
        sends, recvs = [], []
        for k in range(1, N_DEV):
            peer = _peer(pos, k)
            sends.append(pltpu.make_async_remote_copy(
                src_ref=self._src(x_ref, _index(peer)), dst_ref=out_ref.at[me], send_sem=send_sems.at[k - 1],
                recv_sem=recv_sems.at[k - 1], device_id=peer, device_id_type=pl.DeviceIdType.MESH))
            recvs.append(pltpu.make_async_remote_copy(
                src_ref=self._src(x_ref, me), dst_ref=out_ref.at[_index(peer)], send_sem=send_sems.at[k - 1],
                recv_sem=recv_sems.at[k - 1], device_id=peer, device_id_type=pl.DeviceIdType.MESH))
        return local, sends, recvs

    def start(self, *refs):
        local, sends, _ = self._copies(*refs)
        local.start()
        for cp in sends:
            cp.start()

    def finish(self, *refs):
        local, sends, recvs = self._copies(*refs)
        for cp in recvs:
            cp.wait_recv()
        for cp in sends:
            cp.wait_send()
        local.wait()


def _hosted_call(body, *, name, grid, in_specs, out_specs, out_shape, scratch, args, semantics, exchange=None):
    if exchange is None:
        return pl.pallas_call(
            body, name=name, grid=grid, in_specs=in_specs, out_specs=out_specs, out_shape=out_shape,
            scratch_shapes=scratch, compiler_params=_params(*semantics))(*args)
    n_in, n_out, n_scr = len(in_specs), len(out_specs), len(scratch)

    def hosted(*refs):
        ins, x_ref = refs[:n_in], refs[n_in]
        outs, land_ref = refs[n_in + 1:n_in + 1 + n_out], refs[n_in + 1 + n_out]
        rest = refs[n_in + n_out + 2:]
        sems = rest[n_scr:]
        ids = [pl.program_id(a) for a in range(len(grid))]
        first, last = ids[0] == 0, ids[0] == grid[0] - 1
        for a in range(1, len(grid)):
            first, last = first & (ids[a] == 0), last & (ids[a] == grid[a] - 1)

        @pl.when(first)
        def _():
            exchange.start(x_ref, land_ref, *sems)

        body(*ins, *outs, *rest[:n_scr])

        @pl.when(last)
        def _():
            exchange.finish(x_ref, land_ref, *sems)

    return pl.pallas_call(
        hosted, name=name, grid=grid, in_specs=list(in_specs) + [exchange.spec],
        out_specs=list(out_specs) + [exchange.spec], out_shape=list(out_shape) + [exchange.out_shape],
        scratch_shapes=list(scratch) + exchange.scratch, compiler_params=_params(*(["arbitrary"] * len(grid))),
    )(*args, exchange.buf)


def _exchange_alone(exchange, name):
    def body(x_ref, out_ref, send_sems, recv_sems, local_sem):
        exchange.start(x_ref, out_ref, send_sems, recv_sems, local_sem)
        exchange.finish(x_ref, out_ref, send_sems, recv_sems, local_sem)

    return pl.pallas_call(
        body, name=name, out_shape=exchange.out_shape, in_specs=[exchange.spec], out_specs=exchange.spec,
        scratch_shapes=exchange.scratch)(exchange.buf)


def _adamw(w, g, m, v):
    m = ADAM_B1 * m + (1.0 - ADAM_B1) * g
    v = ADAM_B2 * v + (1.0 - ADAM_B2) * (g * g)
    m_hat = m / (1.0 - ADAM_B1 ** ADAM_STEP)
    v_hat = v / (1.0 - ADAM_B2 ** ADAM_STEP)
    delta = -ADAM_LR * (m_hat / (jnp.sqrt(v_hat) + ADAM_EPS) + ADAM_WD * w)
    return delta, m, v


def _sum_adamw(parts, w, m, v, name):
    R, C = w.shape
    tr = max(t for t in range(16, ROWS + 1, 16) if R % t == 0)

    def body(p_ref, w_ref, m_ref, v_ref, g_ref, d_ref, mo_ref, vo_ref):
        g = p_ref[0].astype(F32)
        for i in range(1, N_DEV):
            g = g + p_ref[i].astype(F32)
        g_ref[...] = g
        d_ref[...], mo_ref[...], vo_ref[...] = _adamw(w_ref[...], g, m_ref[...], v_ref[...])

    row = _row_spec(tr, C)
    return pl.pallas_call(
        body, name=name, grid=(R // tr,),
        in_specs=[pl.BlockSpec((N_DEV, tr, C), lambda i: (0, i, 0)), row, row, row], out_specs=[row] * 4,
        out_shape=[jax.ShapeDtypeStruct((R, C), F32)] * 4, compiler_params=_params("parallel"),
    )(parts, w, m, v)


def _sum_parts(parts, name):
    _, R, C = parts.shape

    def body(p_ref, g_ref):
        g = p_ref[0]
        for i in range(1, N_DEV):
            g = g + p_ref[i]
        g_ref[...] = g

    return pl.pallas_call(body, name=name, out_shape=jax.ShapeDtypeStruct((R, C), F32))(parts)


def _adamw_call(w, g, m, v, name):
    def body(w_ref, g_ref, m_ref, v_ref, d_ref, mo_ref, vo_ref):
        d_ref[...], mo_ref[...], vo_ref[...] = _adamw(w_ref[...], g_ref[...], m_ref[...], v_ref[...])

    return pl.pallas_call(body, name=name, out_shape=[jax.ShapeDtypeStruct(w.shape, F32)] * 3)(w, g, m, v)


BIG = ("w_in", "w_out", "ca_wq", "ca_wk", "ca_wv", "ca_wo", "ffn_w_up", "ffn_w_down")
BIG_FULL = {"w_in": (1024, 2816), "w_out": (1024, 1024), "ca_wq": (1024, 1024), "ca_wk": (1024, 1024),
            "ca_wv": (1024, 1024), "ca_wo": (1024, 1024), "ffn_w_up": (1024, 5632), "ffn_w_down": (2816, 1024)}
FIRST = ("w_in",)
LATER = BIG[1:]
COL_SHARDED = ("w_in", "ffn_w_up")
PACK_COLS = 1024
NORMS = ("mix_pre_norm", "mix_post_norm", "ca_pre_norm", "mem_norm", "ca_post_norm", "ffn_pre_norm", "ffn_post_norm")
SMALL_ROWS = 32


def _big_rows(name):
    r, c = BIG_FULL[name]
    return r * c // N_DEV // PACK_COLS


def _pack_shards(shards, names):
    return jnp.concatenate([shards[n].reshape(_big_rows(n), PACK_COLS) for n in names], axis=0)


def _unpack_shards(pack, shapes, names):
    out, r0 = {}, 0
    for n in names:
        out[n] = pack[r0:r0 + _big_rows(n)].reshape(shapes[n])
        r0 += _big_rows(n)
    return out


def _unpack_gathered(gathered, names):
    out, r0 = {}, 0
    for n in names:
        rows = _big_rows(n)
        blk = gathered[:, r0:r0 + rows]
        r, c = BIG_FULL[n]
        if n in COL_SHARDED:
            out[n] = blk.reshape(N_DEV, r, c // N_DEV).transpose(1, 0, 2).reshape(r, c)
        else:
            out[n] = blk.reshape(r, c)
        r0 += rows
    return out


def _pack_full_grads(grads, names):
    parts = []
    for n in names:
        r, c = BIG_FULL[n]
        g = grads[n]
        if n in COL_SHARDED:
            g = g.reshape(r, N_DEV, c // N_DEV).transpose(1, 0, 2)
        parts.append(g.reshape(N_DEV, _big_rows(n), PACK_COLS))
    return jnp.concatenate(parts, axis=1).astype(BF16)


def _pad_row(vec):
    vec = vec.reshape(-1)
    n = -(-vec.shape[0] // PACK_COLS) * PACK_COLS
    return jnp.pad(vec, (0, n - vec.shape[0])).reshape(-1, PACK_COLS)


def _pack_small(norms, logits, out_norm, sinks, loss, conv_b, conv_w):
    rows = [_pad_row(norms[n]) for n in NORMS]
    rows.append(_pad_row(logits))
    rows.append(_pad_row(jnp.concatenate([out_norm.reshape(-1), sinks.reshape(-1), loss.reshape(-1)])))
    rows.append(_pad_row(conv_b))
    rows.append(_pad_row(conv_w))
    pack = jnp.concatenate(rows, axis=0)
    return jnp.pad(pack, ((0, SMALL_ROWS - pack.shape[0]), (0, 0)))


def _unpack_small(pack):
    norms = {n: pack[i:i + 1] for i, n in enumerate(NORMS)}
    logits = pack[7].reshape(2, 512)
    out_norm = pack[8:9, 0:128]
    sinks = pack[8:9, 128:136]
    loss = pack[8, 136]
    conv_b = pack[9:15].reshape(-1)[:2 * D_FF].reshape(1, 2 * D_FF)
    conv_w = pack[15:32].reshape(-1)[:6 * D_FF].reshape(3, 2 * D_FF)
    return norms, logits, out_norm, sinks, loss, conv_b, conv_w


def kernel(x, mem, mix_pre_norm, w_in, attn_sinks, hgrn_lb_logits, hgrn_out_norm, w_out, mix_post_norm, ca_pre_norm, mem_norm, ca_wq, ca_wk, ca_wv, ca_wo, ca_post_norm, ffn_pre_norm, ffn_w_up, ffn_conv_w, ffn_conv_b, ffn_w_down, ffn_post_norm, loss_target, m_mix_pre_norm, m_w_in, m_attn_sinks, m_hgrn_lb_logits, m_hgrn_out_norm, m_w_out, m_mix_post_norm, m_ca_pre_norm, m_mem_norm, m_ca_wq, m_ca_wk, m_ca_wv, m_ca_wo, m_ca_post_norm, m_ffn_pre_norm, m_ffn_w_up, m_ffn_conv_w, m_ffn_conv_b, m_ffn_w_down, m_ffn_post_norm, v_mix_pre_norm, v_w_in, v_attn_sinks, v_hgrn_lb_logits, v_hgrn_out_norm, v_w_out, v_mix_post_norm, v_ca_pre_norm, v_mem_norm, v_ca_wq, v_ca_wk, v_ca_wv, v_ca_wo, v_ca_post_norm, v_ffn_pre_norm, v_ffn_w_up, v_ffn_conv_w, v_ffn_conv_b, v_ffn_w_down, v_ffn_post_norm):
    names = ["mix_pre_norm", "w_in", "attn_sinks", "hgrn_lb_logits", "hgrn_out_norm", "w_out", "mix_post_norm",
             "ca_pre_norm", "mem_norm", "ca_wq", "ca_wk", "ca_wv", "ca_wo", "ca_post_norm", "ffn_pre_norm",
             "ffn_w_up", "ffn_conv_w", "ffn_conv_b", "ffn_w_down", "ffn_post_norm"]
    w_all = dict(zip(names, [mix_pre_norm, w_in, attn_sinks, hgrn_lb_logits, hgrn_out_norm, w_out, mix_post_norm,
                             ca_pre_norm, mem_norm, ca_wq, ca_wk, ca_wv, ca_wo, ca_post_norm, ffn_pre_norm,
                             ffn_w_up, ffn_conv_w, ffn_conv_b, ffn_w_down, ffn_post_norm]))
    m_all = dict(zip(names, [m_mix_pre_norm, m_w_in, m_attn_sinks, m_hgrn_lb_logits, m_hgrn_out_norm, m_w_out,
                             m_mix_post_norm, m_ca_pre_norm, m_mem_norm, m_ca_wq, m_ca_wk, m_ca_wv, m_ca_wo,
                             m_ca_post_norm, m_ffn_pre_norm, m_ffn_w_up, m_ffn_conv_w, m_ffn_conv_b, m_ffn_w_down,
                             m_ffn_post_norm]))
    v_all = dict(zip(names, [v_mix_pre_norm, v_w_in, v_attn_sinks, v_hgrn_lb_logits, v_hgrn_out_norm, v_w_out,
                             v_mix_post_norm, v_ca_pre_norm, v_mem_norm, v_ca_wq, v_ca_wk, v_ca_wv, v_ca_wo,
                             v_ca_post_norm, v_ffn_pre_norm, v_ffn_w_up, v_ffn_conv_w, v_ffn_conv_b, v_ffn_w_down,
                             v_ffn_post_norm]))
    dev = _index(_mesh_pos())

    shards = {n: w_all[n][0] for n in BIG}
    w_packs = {grp: _pack_shards(shards, grp) for grp in (FIRST, LATER)}
    conv_w_rows = _exchange_alone(_Exchange("gather", _pad_row(ffn_conv_w[0])), "gather_conv_w")
    conv_w_full = conv_w_rows.reshape(N_DEV, -1)[:, :3 * 704].reshape(N_DEV, 3, 704).transpose(1, 0, 2).reshape(3, 2 * D_FF)

    received, grads_small, loss_local, grad_x = _local_step(
        x[0], mem[0], loss_target[0], {grp: w_packs[grp].astype(BF16) for grp in w_packs}, conv_w_full,
        {n: w_all[n] for n in NORMS}, attn_sinks, hgrn_lb_logits, hgrn_out_norm, ffn_conv_b)

    small_pack = _pack_small(grads_small["norms"], grads_small["logits"], grads_small["out_norm"], grads_small["sinks"],
                             loss_local, grads_small["conv_b"], grads_small["conv_w"])
    small_sum = _sum_parts(_exchange_alone(_Exchange("gather", small_pack), "gather_small"), "sum_small")
    g_norms, g_logits, g_out_norm, g_sinks, loss, g_conv_b, g_conv_w_full = _unpack_small(small_sum)
    g_conv_w = lax.dynamic_slice_in_dim(g_conv_w_full, dev * 704, 704, axis=1)

    shard_shapes = {n: w_all[n].shape for n in BIG}
    out_g, out_d, out_m, out_v = {}, {}, {}, {}
    for grp, tag in ((FIRST, "first"), (LATER, "later")):
        packs = _sum_adamw(
            received[grp], w_packs[grp], _pack_shards({n: m_all[n][0] for n in grp}, grp),
            _pack_shards({n: v_all[n][0] for n in grp}, grp), "adamw_" + tag)
        for tree, pack in zip((out_g, out_d, out_m, out_v), packs):
            tree.update(_unpack_shards(pack, shard_shapes, grp))

    small_g = dict(g_norms)
    small_g.update(attn_sinks=g_sinks, hgrn_lb_logits=g_logits, hgrn_out_norm=g_out_norm,
                   ffn_conv_b=g_conv_b, ffn_conv_w=g_conv_w[None])
    small_names = [n for n in names if n not in BIG]

    def small_pack_of(tree):
        return jnp.concatenate([_pad_row(tree[n]) for n in small_names], axis=0)

    ds, ms, vs = _adamw_call(small_pack_of(w_all), small_pack_of(small_g), small_pack_of(m_all), small_pack_of(v_all),
                             "adamw_small")

    def small_unpack(pack):
        out, r0 = {}, 0
        for n in small_names:
            size = 1
            for s in w_all[n].shape:
                size *= s
            rows = -(-size // PACK_COLS)
            out[n] = pack[r0:r0 + rows].reshape(-1)[:size].reshape(w_all[n].shape)
            r0 += rows
        return out

    sd, sm, sv = small_unpack(ds), small_unpack(ms), small_unpack(vs)
    for n in small_names:
        out_g[n] = small_g[n].reshape(w_all[n].shape)
        out_d[n], out_m[n], out_v[n] = sd[n], sm[n], sv[n]

    return (loss, grad_x[None], *[out_g[n] for n in names], *[out_d[n] for n in names],
            *[out_m[n] for n in names], *[out_v[n] for n in names])


def _local_step(x, mem, target, w_packs, conv_w, norms, sinks, lb_logits, out_norm, conv_b):
    g1, g2, g3 = norms["mix_pre_norm"], norms["mix_post_norm"], norms["ca_pre_norm"]
    g4, g5, g6, g7 = norms["mem_norm"], norms["ca_post_norm"], norms["ffn_pre_norm"], norms["ffn_post_norm"]

    h1, gathered = _norm_fwd(x, g1, "mix_norm", exchange=_Exchange("gather", w_packs[FIRST]))
    w_in = _unpack_gathered(gathered, FIRST)["w_in"]
    z = _mm(h1, w_in, mode="nn", out_dtype=F32, name="in_proj", tn=1408)
    attn, lse = _swa_fwd(z, sinks, "swa_fwd")
    lb = _lower_bound(lb_logits, "lower_bound")
    rec, o_rec, states, gathered = _hgrn_fwd(z, lb, out_norm, "hgrn_fwd", exchange=_Exchange("gather", w_packs[LATER]))
    wf = _unpack_gathered(gathered, LATER)
    w_out, wq, wk, wv, wo = wf["w_out"], wf["ca_wq"], wf["ca_wk"], wf["ca_wv"], wf["ca_wo"]
    w_up, w_down = wf["ffn_w_up"], wf["ffn_w_down"]
    cat = jnp.concatenate([attn, rec], axis=1)
    mix = _mm(cat, w_out, mode="nn", out_dtype=F32, name="out_proj")
    x1, h2 = _post_pre(x, mix, g2, g3, "mix_post")
    mem_n = _norm_fwd(mem, g4, "mem_norm")
    q = _mm(h2, wq, mode="nn", out_dtype=BF16, name="ca_q")
    k = _mm(mem_n, wk, mode="nn", out_dtype=BF16, name="ca_k")
    v = _mm(mem_n, wv, mode="nn", out_dtype=BF16, name="ca_v")
    oc = _ca_fwd(q, k, v, "ca_fwd")
    c = _mm(oc, wo, mode="nn", out_dtype=F32, name="ca_o")
    x2, h3 = _post_pre(x1, c, g5, g6, "ca_post")
    u = _mm(h3, w_up, mode="nn", out_dtype=F32, name="ffn_up", tn=1408, split_out=True)
    a = _glu_fwd(u, conv_w, conv_b, "glu_fwd")
    y = _mm(a, w_down, mode="nn", out_dtype=F32, name="ffn_down", tk=2816)
    loss, dx3, dy, dg7 = _final(x2, y, g7, target, "loss_head")

    da = _mm(dy, w_down, mode="nt", out_dtype=F32, name="ffn_down_dx", tn=1408)
    d_w_down = _mm(a, dy, mode="tn", out_dtype=F32, name="ffn_down_dw", tm=1408, tk=512)
    dc, d_cb, d_cw = _glu_bwd(u, conv_w, conv_b, da, "glu_bwd")
    du = _conv_bwd(dc, conv_w, "conv_bwd")
    dh3 = _mm(du, w_up, mode="nt", out_dtype=F32, name="ffn_up_dx", tm=2048, tk=1408, split_a=True)
    d_w_up = _mm(h3, du, mode="tn", out_dtype=F32, name="ffn_up_dw", tm=1024, tn=1408, tk=512, split_b=True)
    dx2, dcv, dg6, dg5 = _norm_bwd2(dx3, dh3, x2, g6, c, g5, "ca_post_bwd")
    doc = _mm(dcv, wo, mode="nt", out_dtype=BF16, name="ca_o_dx")
    d_wo = _mm(oc, dcv, mode="tn", out_dtype=F32, name="ca_o_dw", tm=1024, tk=512)
    dq, dk, dv = _ca_bwd(q, k, v, doc, "ca_bwd")
    d_wq = _mm(h2, dq, mode="tn", out_dtype=F32, name="ca_q_dw", tm=1024, tk=512)
    dh2 = _mm(dq, wq, mode="nt", out_dtype=F32, name="ca_q_dx")
    d_wk = _mm(mem_n, dk, mode="tn", out_dtype=F32, name="ca_k_dw", tm=1024)
    d_wv = _mm(mem_n, dv, mode="tn", out_dtype=F32, name="ca_v_dw", tm=1024)
    dmem_k = _mm(dk, wk, mode="nt", out_dtype=F32, name="ca_k_dx")
    dmem_v = _mm(dv, wv, mode="nt", out_dtype=F32, name="ca_v_dx")
    dg4 = _gain_bwd(mem, dmem_k, dmem_v, "mem_norm_bwd")
    dx1, dmix, dg3, dg2 = _norm_bwd2(dx2, dh2, x1, g3, mix, g2, "mix_post_bwd")
    dcat = _mm(dmix, w_out, mode="nt", out_dtype=F32, name="out_proj_dx")
    d_w_out = _mm(cat, dmix, mode="tn", out_dtype=F32, name="out_proj_dw", tm=1024, tk=512)
    later = {"w_out": d_w_out, "ca_wq": d_wq, "ca_wk": d_wk, "ca_wv": d_wv, "ca_wo": d_wo,
             "ffn_w_up": d_w_up, "ffn_w_down": d_w_down}
    dqr, dfr, dir_, dgr, dlb, donw, got_later = _hgrn_bwd(
        z, lb, out_norm, o_rec, states, dcat, "hgrn_bwd", exchange=_Exchange("scatter", _pack_full_grads(later, LATER)))
    dq_a, dka, dkb, dva, dvb, dsinks = _swa_bwd(z, sinks, dcat, lse, "swa_bwd")
    dz = _assemble_dz(dq_a, dka, dkb, dva, dvb, dqr, dfr, dir_, dgr, "assemble_dz")
    d_w_in = _mm(h1, dz, mode="tn", out_dtype=F32, name="in_proj_dw", tm=1024, tn=1408, tk=512)
    dh1, got_first = _mm(dz, w_in, mode="nt", out_dtype=F32, name="in_proj_dx", tk=2816,
                         exchange=_Exchange("scatter", _pack_full_grads({"w_in": d_w_in}, FIRST)))
    dx, dg1 = _norm_bwd1(dx1, dh1, x, g1, "mix_norm_bwd")

    small = {
        "norms": {"mix_pre_norm": dg1, "mix_post_norm": dg2, "ca_pre_norm": dg3, "mem_norm": dg4,
                  "ca_post_norm": dg5, "ffn_pre_norm": dg6, "ffn_post_norm": dg7},
        "logits": _lower_bound_bwd(lb, dlb, "lower_bound_bwd"),
        "out_norm": donw,
        "sinks": dsinks,
        "conv_b": jnp.concatenate([d_cb[0], d_cb[1]], axis=1),
        "conv_w": jnp.concatenate([d_cw[0], d_cw[1]], axis=1),
    }
    return {FIRST: got_first, LATER: got_later}, small, loss[0, 0:1], dx
```

```python
import jax
import jax.numpy as jnp
from jax import lax
from jax.experimental import pallas as pl
from jax.experimental.pallas import tpu as pltpu

F32 = jnp.float32
BF16 = jnp.bfloat16
EPS = 1e-6
N_DEV = 8
MESH_AXES = ("x", "y", "c")

ATTN_HEAD_DIM = 64
ATTN_Q_HEADS = 8
ATTN_KV_HEADS = 2
ATTN_BLOCK = 128
HGRN_HEADS = 4
HGRN_DIM = 128
HGRN_CHUNK = 64
HGRN_PAIR = 2
HGRN_LEVELS = (32, 16, 8, 4, 2, 1)
CA_HEADS = 4
CA_HEAD_DIM = 256
D_FF = 2816

ADAM_LR = 0.001
ADAM_B1 = 0.9
ADAM_B2 = 0.999
ADAM_EPS = 1e-08
ADAM_WD = 0.01
ADAM_STEP = 10

VMEM_LIMIT = 56 << 20
LANE = 128

NT = (((1,), (1,)), ((), ()))
TN = (((0,), (0,)), ((), ()))


def _params(*sem):
    return pltpu.CompilerParams(dimension_semantics=sem, vmem_limit_bytes=VMEM_LIMIT)


def _tile(n, cap):
    if n <= cap:
        return n
    best = 0
    for t in range(LANE, cap + 1, LANE):
        if n % t == 0:
            best = t
    assert best, (n, cap)
    return best


def _dot(a, b, dims=None):
    if dims is None:
        return jnp.dot(a, b, preferred_element_type=F32)
    return lax.dot_general(a, b, dims, preferred_element_type=F32)


def _bf(x):
    return x.astype(BF16)


def _sigmoid(x):
    return 1.0 / (1.0 + jnp.exp(-x))


def _rms(x):
    r = lax.rsqrt(jnp.mean(x * x, axis=-1, keepdims=True) + EPS)
    return x * r, r


def _rms_bwd(dxh, xh, r):
    return r * (dxh - xh * jnp.mean(dxh * xh, axis=-1, keepdims=True))


def _mm(a, b, *, mode, out_dtype, name, tm=512, tn=1024, tk=1024, split_a=False, split_b=False, split_out=False,
        exchange=None):
    def dims(arr, split):
        if split:
            return arr.shape[1], 2 * arr.shape[2]
        return arr.shape

    ar, ac = dims(a, split_a)
    br, bc = dims(b, split_b)
    if mode == "nn":
        M, K, N = ar, ac, bc
        assert br == K
    elif mode == "nt":
        M, K, N = ar, ac, br
        assert bc == K
    else:
        K, M, N = ar, ac, bc
        assert br == K
    a_cols_half = ac // 2 if split_a else None
    b_cols_half = bc // 2 if split_b else None
    tm = _tile(M, tm)
    tn = _tile((N // 2) if (split_out or (split_b and mode != "nt")) else N, tn)
    tk = _tile((K // 2) if ((split_a and mode != "tn") or (split_b and mode == "nt")) else K, tk)
    if split_a and mode == "tn":
        tm = _tile(M // 2, tm)
    gm, gn, gk = M // tm, N // tn, K // tk
    a_bytes, b_bytes = a.size * a.dtype.itemsize, b.size * b.dtype.itemsize
    rows_outer = gk > 1 or a_bytes + gm * b_bytes <= gn * a_bytes + b_bytes
    grid = (gm, gn, gk) if rows_outer else (gn, gm, gk)

    def spec(split, half, blk, rc):
        def imap(p, q, k):
            r, c = rc(*((p, q) if rows_outer else (q, p)), k)
            if not split:
                return (r, c)
            per_half = half // blk[1]
            return (c // per_half, r, c % per_half)

        return pl.BlockSpec(((None,) + blk) if split else blk, imap)

    if mode == "nn":
        a_spec = spec(split_a, a_cols_half, (tm, tk), lambda i, j, k: (i, k))
        b_spec = spec(split_b, b_cols_half, (tk, tn), lambda i, j, k: (k, j))
        dn = None
    elif mode == "nt":
        a_spec = spec(split_a, a_cols_half, (tm, tk), lambda i, j, k: (i, k))
        b_spec = spec(split_b, b_cols_half, (tn, tk), lambda i, j, k: (j, k))
        dn = NT
    else:
        a_spec = spec(split_a, a_cols_half, (tk, tm), lambda i, j, k: (k, i))
        b_spec = spec(split_b, b_cols_half, (tk, tn), lambda i, j, k: (k, j))
        dn = TN
    o_spec = spec(split_out, N // 2 if split_out else None, (tm, tn), lambda i, j, k: (i, j))
    out_shape = (2, M, N // 2) if split_out else (M, N)

    if gk == 1:
        def body(a_ref, b_ref, o_ref):
            o_ref[...] = _dot(_bf(a_ref[...]), _bf(b_ref[...]), dn).astype(o_ref.dtype)
        scratch = []
    else:
        def body(a_ref, b_ref, o_ref, acc_ref):
            k = pl.program_id(2)

            @pl.when(k == 0)
            def _():
                acc_ref[...] = jnp.zeros_like(acc_ref)

            acc_ref[...] += _dot(_bf(a_ref[...]), _bf(b_ref[...]), dn)

            @pl.when(k == gk - 1)
            def _():
                o_ref[...] = acc_ref[...].astype(o_ref.dtype)
        scratch = [pltpu.VMEM((tm, tn), F32)]

    out = _hosted_call(
        body, name=name, grid=grid, in_specs=[a_spec, b_spec], out_specs=[o_spec],
        out_shape=[jax.ShapeDtypeStruct(out_shape, out_dtype)], scratch=scratch, args=(a, b),
        semantics=("parallel", "parallel", "arbitrary"), exchange=exchange)
    return out[0] if exchange is None else out


ROWS = 256


def _row_spec(tr, cols):
    return pl.BlockSpec((tr, cols), lambda i: (i, 0))


def _vec_spec(cols):
    return pl.BlockSpec((1, cols), lambda i: (0, 0))


def _norm_fwd(x, g, name, exchange=None):
    T, Dm = x.shape
    tr = min(ROWS, T)

    def body(x_ref, g_ref, h_ref):
        xh, _ = _rms(x_ref[...])
        h_ref[...] = (xh * g_ref[...]).astype(h_ref.dtype)

    out = _hosted_call(
        body, name=name, grid=(T // tr,), in_specs=[_row_spec(tr, Dm), _vec_spec(Dm)], out_specs=[_row_spec(tr, Dm)],
        out_shape=[jax.ShapeDtypeStruct((T, Dm), BF16)], scratch=[], args=(x, g), semantics=("parallel",),
        exchange=exchange)
    return out[0] if exchange is None else out


def _post_pre(x, m, g_post, g_pre, name):
    T, Dm = x.shape
    tr = min(ROWS, T)

    def body(x_ref, m_ref, gp_ref, gn_ref, xo_ref, h_ref):
        mh, _ = _rms(m_ref[...])
        xn = x_ref[...] + mh * gp_ref[...]
        xo_ref[...] = xn
        xh, _ = _rms(xn)
        h_ref[...] = (xh * gn_ref[...]).astype(h_ref.dtype)

    return pl.pallas_call(
        body, name=name, grid=(T // tr,),
        in_specs=[_row_spec(tr, Dm), _row_spec(tr, Dm), _vec_spec(Dm), _vec_spec(Dm)],
        out_specs=[_row_spec(tr, Dm), _row_spec(tr, Dm)],
        out_shape=[jax.ShapeDtypeStruct((T, Dm), F32), jax.ShapeDtypeStruct((T, Dm), BF16)],
        compiler_params=_params("parallel"),
    )(x, m, g_post, g_pre)


def _final(x2, y, g_post, target, name):
    T, Dm = x2.shape
    tr = min(ROWS, T)

    def body(x_ref, y_ref, g_ref, t_ref, loss_ref, dx_ref, dy_ref, dg_ref):
        @pl.when(pl.program_id(0) == 0)
        def _():
            loss_ref[...] = jnp.zeros_like(loss_ref)
            dg_ref[...] = jnp.zeros_like(dg_ref)

        g = g_ref[...]
        yh, r = _rms(y_ref[...])
        d = x_ref[...] + yh * g - t_ref[...]
        loss_ref[...] += jnp.zeros((1, LANE), F32) + 0.5 * jnp.sum(jnp.mean(d * d, axis=-1, keepdims=True))
        dx = d * (1.0 / Dm)
        dx_ref[...] = dx
        dy_ref[...] = _rms_bwd(dx * g, yh, r).astype(dy_ref.dtype)
        dg_ref[...] += jnp.sum(dx * yh, axis=0, keepdims=True)

    return pl.pallas_call(
        body, name=name, grid=(T // tr,),
        in_specs=[_row_spec(tr, Dm), _row_spec(tr, Dm), _vec_spec(Dm), _row_spec(tr, Dm)],
        out_specs=[_vec_spec(LANE), _row_spec(tr, Dm), _row_spec(tr, Dm), _vec_spec(Dm)],
        out_shape=[jax.ShapeDtypeStruct((1, LANE), F32), jax.ShapeDtypeStruct((T, Dm), F32),
                   jax.ShapeDtypeStruct((T, Dm), BF16), jax.ShapeDtypeStruct((1, Dm), F32)],
        compiler_params=_params("arbitrary"),
    )(x2, y, g_post, target)


def _norm_bwd2(dx_cur, dh, x_prev, g_pre, m_prev, g_post, name):
    T, Dm = x_prev.shape
    tr = min(ROWS, T)

    def body(dx_ref, dh_ref, x_ref, gn_ref, m_ref, gp_ref, dxo_ref, dm_ref, dgn_ref, dgp_ref):
        @pl.when(pl.program_id(0) == 0)
        def _():
            dgn_ref[...] = jnp.zeros_like(dgn_ref)
            dgp_ref[...] = jnp.zeros_like(dgp_ref)

        dh = dh_ref[...].astype(F32)
        xh, r = _rms(x_ref[...])
        dx = dx_ref[...] + _rms_bwd(dh * gn_ref[...], xh, r)
        dxo_ref[...] = dx
        dgn_ref[...] += jnp.sum(dh * xh, axis=0, keepdims=True)
        mh, rm = _rms(m_ref[...])
        dm_ref[...] = _rms_bwd(dx * gp_ref[...], mh, rm).astype(dm_ref.dtype)
        dgp_ref[...] += jnp.sum(dx * mh, axis=0, keepdims=True)

    return pl.pallas_call(
        body, name=name, grid=(T // tr,),
        in_specs=[_row_spec(tr, Dm), _row_spec(tr, Dm), _row_spec(tr, Dm), _vec_spec(Dm), _row_spec(tr, Dm), _vec_spec(Dm)],
        out_specs=[_row_spec(tr, Dm), _row_spec(tr, Dm), _vec_spec(Dm), _vec_spec(Dm)],
        out_shape=[jax.ShapeDtypeStruct((T, Dm), F32), jax.ShapeDtypeStruct((T, Dm), BF16),
                   jax.ShapeDtypeStruct((1, Dm), F32), jax.ShapeDtypeStruct((1, Dm), F32)],
        compiler_params=_params("arbitrary"),
    )(dx_cur, dh, x_prev, g_pre, m_prev, g_post)


def _norm_bwd1(dx_cur, dh, x_prev, g_pre, name):
    T, Dm = x_prev.shape
    tr = min(ROWS, T)

    def body(dx_ref, dh_ref, x_ref, gn_ref, dxo_ref, dgn_ref):
        @pl.when(pl.program_id(0) == 0)
        def _():
            dgn_ref[...] = jnp.zeros_like(dgn_ref)

        dh = dh_ref[...].astype(F32)
        xh, r = _rms(x_ref[...])
        dxo_ref[...] = dx_ref[...] + _rms_bwd(dh * gn_ref[...], xh, r)
        dgn_ref[...] += jnp.sum(dh * xh, axis=0, keepdims=True)

    return pl.pallas_call(
        body, name=name, grid=(T // tr,),
        in_specs=[_row_spec(tr, Dm), _row_spec(tr, Dm), _row_spec(tr, Dm), _vec_spec(Dm)],
        out_specs=[_row_spec(tr, Dm), _vec_spec(Dm)],
        out_shape=[jax.ShapeDtypeStruct((T, Dm), F32), jax.ShapeDtypeStruct((1, Dm), F32)],
        compiler_params=_params("arbitrary"),
    )(dx_cur, dh, x_prev, g_pre)


def _gain_bwd(x, dh_a, dh_b, name):
    T, Dm = x.shape

    def body(x_ref, a_ref, b_ref, dg_ref):
        xh, _ = _rms(x_ref[...])
        dg_ref[...] = jnp.sum((a_ref[...] + b_ref[...]) * xh, axis=0, keepdims=True)

    return pl.pallas_call(
        body, name=name, grid=(1,), in_specs=[_row_spec(T, Dm)] * 3, out_specs=_vec_spec(Dm),
        out_shape=jax.ShapeDtypeStruct((1, Dm), F32), compiler_params=_params("arbitrary"),
    )(x, dh_a, dh_b)


ATTN_GROUP = ATTN_Q_HEADS // ATTN_KV_HEADS


def _swa_mask(n):
    rows = ATTN_GROUP * ATTN_BLOCK
    row = lax.broadcasted_iota(jnp.int32, (rows, 2 * ATTN_BLOCK), 0) & (ATTN_BLOCK - 1)
    col = lax.broadcasted_iota(jnp.int32, (rows, 2 * ATTN_BLOCK), 1)
    diff = row + ATTN_BLOCK - col
    return (diff >= 0) & (diff < ATTN_BLOCK) & ((col >= ATTN_BLOCK) | (n > 0))


def _swa_rows(ref, hk, dtype):
    hd = ATTN_HEAD_DIM
    return jnp.concatenate(
        [ref[:, hd * (hk * ATTN_GROUP + g):hd * (hk * ATTN_GROUP + g + 1)].astype(dtype) for g in range(ATTN_GROUP)],
        axis=0)


def _swa_per_row(vals):
    seg = lax.broadcasted_iota(jnp.int32, (ATTN_GROUP * ATTN_BLOCK, 1), 0) // ATTN_BLOCK
    col = jnp.zeros((ATTN_GROUP * ATTN_BLOCK, 1), F32)
    for g, val in enumerate(vals):
        col = jnp.where(seg == g, val, col)
    return col


def _swa_specs():
    blk = ATTN_BLOCK
    prev = lambda n: jnp.maximum(n - 1, 0)
    return [
        pl.BlockSpec(memory_space=pltpu.SMEM),
        pl.BlockSpec((blk, 512), lambda n: (n, 0)),
        pl.BlockSpec((blk, 128), lambda n: (prev(n), 4)),
        pl.BlockSpec((blk, 128), lambda n: (n, 4)),
        pl.BlockSpec((blk, 128), lambda n: (prev(n), 5)),
        pl.BlockSpec((blk, 128), lambda n: (n, 5)),
    ]


def _swa_fwd(z, sinks, name):
    T = z.shape[0]
    blk, hd = ATTN_BLOCK, ATTN_HEAD_DIM
    scale = hd ** -0.5

    def body(sink_ref, q_ref, kp_ref, kc_ref, vp_ref, vc_ref, o_ref, lse_ref):
        allowed = _swa_mask(pl.program_id(0))
        hks = range(ATTN_KV_HEADS)
        kss = [slice(hd * hk, hd * hk + hd) for hk in hks]
        k = [_bf(jnp.concatenate([kp_ref[:, ks], kc_ref[:, ks]], axis=0)) for ks in kss]
        v = [_bf(jnp.concatenate([vp_ref[:, ks], vc_ref[:, ks]], axis=0)) for ks in kss]
        s = [jnp.where(allowed, _dot(_swa_rows(q_ref, hk, BF16), k[hk], NT) * scale, -1e30) for hk in hks]
        sink = [_swa_per_row([sink_ref[0, hk * ATTN_GROUP + g] for g in range(ATTN_GROUP)]) for hk in hks]
        m = [jnp.maximum(jnp.max(s[hk], axis=-1, keepdims=True), sink[hk]) for hk in hks]
        p = [jnp.exp(s[hk] - m[hk]) for hk in hks]
        l = [jnp.sum(p[hk], axis=-1, keepdims=True) + jnp.exp(sink[hk] - m[hk]) for hk in hks]
        o = [_dot(_bf(p[hk] / l[hk]), v[hk]).astype(o_ref.dtype) for hk in hks]
        for hk in hks:
            lse = m[hk] + jnp.log(l[hk])
            for g in range(ATTN_GROUP):
                h = hk * ATTN_GROUP + g
                o_ref[:, hd * h:hd * (h + 1)] = o[hk][blk * g:blk * (g + 1)]
                lse_ref[:, h:h + 1] = lse[blk * g:blk * (g + 1)]

    return pl.pallas_call(
        body, name=name, grid=(T // blk,), in_specs=_swa_specs(),
        out_specs=[pl.BlockSpec((blk, 512), lambda n: (n, 0)), pl.BlockSpec((blk, ATTN_Q_HEADS), lambda n: (n, 0))],
        out_shape=[jax.ShapeDtypeStruct((T, 512), BF16), jax.ShapeDtypeStruct((T, ATTN_Q_HEADS), F32)],
        compiler_params=_params("parallel"),
    )(sinks, z, z, z, z, z)


def _swa_bwd(z, sinks, dcat, lse, name):
    T = z.shape[0]
    blk, hd = ATTN_BLOCK, ATTN_HEAD_DIM
    scale = hd ** -0.5
    group = ATTN_Q_HEADS // ATTN_KV_HEADS

    def body(sink_ref, q_ref, kp_ref, kc_ref, vp_ref, vc_ref, do_ref, lse_ref,
             dq_ref, dka_ref, dkb_ref, dva_ref, dvb_ref, dsink_ref):
        @pl.when(pl.program_id(0) == 0)
        def _():
            dsink_ref[...] = jnp.zeros_like(dsink_ref)

        allowed = _swa_mask(pl.program_id(0))
        lane = lax.broadcasted_iota(jnp.int32, (1, ATTN_Q_HEADS), 1)
        dsink = jnp.zeros((1, ATTN_Q_HEADS), F32)
        hks = range(ATTN_KV_HEADS)
        kss = [slice(hd * hk, hd * hk + hd) for hk in hks]
        k = [_bf(jnp.concatenate([kp_ref[:, ks], kc_ref[:, ks]], axis=0)) for ks in kss]
        v = [_bf(jnp.concatenate([vp_ref[:, ks], vc_ref[:, ks]], axis=0)) for ks in kss]
        qs = [_swa_rows(q_ref, hk, BF16) for hk in hks]
        dos = [_swa_rows(do_ref, hk, BF16) for hk in hks]
        lse = [jnp.concatenate([lse_ref[:, hk * group + g:hk * group + g + 1] for g in range(group)], axis=0)
               for hk in hks]
        s = [_dot(qs[hk], k[hk], NT) * scale for hk in hks]
        dp = [_dot(dos[hk], v[hk], NT) for hk in hks]
        p = [jnp.where(allowed, jnp.exp(jnp.where(allowed, s[hk], -1e30) - lse[hk]), 0.0) for hk in hks]
        delta = [jnp.sum(p[hk] * dp[hk], axis=-1, keepdims=True) for hk in hks]
        ds = [_bf(p[hk] * (dp[hk] - delta[hk]) * scale) for hk in hks]
        dq = [_dot(ds[hk], k[hk]).astype(dq_ref.dtype) for hk in hks]
        dk = [_dot(ds[hk], qs[hk], TN) for hk in hks]
        dv = [_dot(_bf(p[hk]), dos[hk], TN) for hk in hks]
        for hk in hks:
            sink = _swa_per_row([sink_ref[0, hk * group + g] for g in range(group)])
            sink_part = jnp.exp(sink - lse[hk]) * delta[hk]
            for g in range(group):
                h = hk * group + g
                dq_ref[:, hd * h:hd * (h + 1)] = dq[hk][blk * g:blk * (g + 1)]
                dsink = dsink + jnp.where(lane == h, -jnp.sum(sink_part[blk * g:blk * (g + 1)]), 0.0)
            dkb_ref[:, kss[hk]] = dk[hk][:blk]
            dka_ref[:, kss[hk]] = dk[hk][blk:]
            dvb_ref[:, kss[hk]] = dv[hk][:blk]
            dva_ref[:, kss[hk]] = dv[hk][blk:]
        dsink_ref[...] += dsink

    kv_out = pl.BlockSpec((blk, 128), lambda n: (n, 0))
    return pl.pallas_call(
        body, name=name, grid=(T // blk,),
        in_specs=_swa_specs() + [pl.BlockSpec((blk, 512), lambda n: (n, 0)),
                                 pl.BlockSpec((blk, ATTN_Q_HEADS), lambda n: (n, 0))],
        out_specs=[pl.BlockSpec((blk, 512), lambda n: (n, 0)), kv_out, kv_out, kv_out, kv_out,
                   pl.BlockSpec((1, ATTN_Q_HEADS), lambda n: (0, 0))],
        out_shape=[jax.ShapeDtypeStruct((T, 512), BF16)] + [jax.ShapeDtypeStruct((T, 128), F32)] * 4
        + [jax.ShapeDtypeStruct((1, ATTN_Q_HEADS), F32)],
        compiler_params=_params("arbitrary"),
    )(sinks, z, z, z, z, z, dcat, lse)


def _assemble_dz(dq_a, dka, dkb, dva, dvb, dqr, dfr, dir_, dgr, name):
    T = dq_a.shape[0]
    blk = ATTN_BLOCK
    nb = T // blk

    def body(dq_ref, dka_ref, dkb_ref, dva_ref, dvb_ref, dqr_ref, dfr_ref, dir_ref, dgr_ref, o_ref):
        has_next = pl.program_id(0) < nb - 1
        o_ref[:, 0:512] = dq_ref[...]
        o_ref[:, 512:640] = (dka_ref[...] + jnp.where(has_next, dkb_ref[...], 0.0)).astype(o_ref.dtype)
        o_ref[:, 640:768] = (dva_ref[...] + jnp.where(has_next, dvb_ref[...], 0.0)).astype(o_ref.dtype)
        o_ref[:, 768:1280] = dqr_ref[...]
        o_ref[:, 1280:1792] = dfr_ref[...]
        o_ref[:, 1792:2304] = dir_ref[...]
        o_ref[:, 2304:2816] = dgr_ref[...]

    cur = lambda w: pl.BlockSpec((blk, w), lambda n: (n, 0))
    nxt = pl.BlockSpec((blk, 128), lambda n: (jnp.minimum(n + 1, nb - 1), 0))
    return pl.pallas_call(
        body, name=name, grid=(nb,),
        in_specs=[cur(512), cur(128), nxt, cur(128), nxt, cur(512), cur(512), cur(512), cur(512)],
        out_specs=pl.BlockSpec((blk, 2816), lambda n: (n, 0)),
        out_shape=jax.ShapeDtypeStruct((T, 2816), BF16), compiler_params=_params("parallel"),
    )(dq_a, dka, dkb, dva, dvb, dqr, dfr, dir_, dgr)


HGRN_ROWS = 512


def _hgrn_consts():
    c = HGRN_CHUNK
    r = lax.broadcasted_iota(jnp.int32, (c, c), 0)
    s = lax.broadcasted_iota(jnp.int32, (c, c), 1)
    rcol = lax.broadcasted_iota(jnp.int32, (c, 1), 0)
    same_block, upper = [], []
    for m in HGRN_LEVELS:
        same_block.append((r & ~(2 * m - 1)) == (s & ~(2 * m - 1)))
        upper.append((rcol & (2 * m - 1)) >= m)
    cum_mat = jnp.where(s <= r, 1.0, 0.0).astype(BF16)
    rev_mat = jnp.where(s >= r, 1.0, 0.0).astype(BF16)
    return cum_mat, rev_mat, r == s, same_block, upper, rcol & 3


def _hgrn_level_decay(g, b, m, pos4):
    c = HGRN_CHUNK
    if m == 1:
        return jnp.exp(jnp.where((pos4 & 1) == 1, g, 0.0))
    if m == 2:
        after, before = pltpu.roll(g, c - 1, 0), pltpu.roll(g, 1, 0)
        return jnp.exp(jnp.where(pos4 == 0, after, jnp.where(pos4 == 1, 0.0, jnp.where(pos4 == 2, g, g + before))))
    b3 = b.reshape(c // (2 * m), 2 * m, HGRN_DIM)
    bref = jnp.broadcast_to(b3[:, m - 1:m, :], b3.shape).reshape(c, HGRN_DIM)
    return jnp.exp(-jnp.abs(b - bref))


def _split3(x):
    hi = _bf(x)
    r1 = x - hi.astype(F32)
    mid = _bf(r1)
    lo = _bf(r1 - mid.astype(F32))
    return jnp.concatenate([hi, mid, lo], axis=1)


def _dot_hilo(a, b):
    r, c = a.shape[0], b.shape[1]
    a_hi, b_hi = _bf(a), _bf(b)
    a2 = jnp.concatenate([a_hi, _bf(a - a_hi.astype(F32))], axis=0)
    b2 = jnp.concatenate([b_hi, _bf(b - b_hi.astype(F32))], axis=1)
    y = _dot(a2, b2)
    return y[:r, :c] + y[:r, c:] + y[r:, :c]


def _fold3(y):
    w = y.shape[1] // 3
    return y[:, :w] + y[:, w:2 * w] + y[:, 2 * w:]


def _hgrn_gates(qr, fr, lb):
    sq = _sigmoid(qr)
    q = qr * sq * (HGRN_DIM ** -0.5)
    sf = _sigmoid(fr)
    f = lb + (1.0 - lb) * sf
    k = (1.0 - lb) * _sigmoid(-fr)
    return q, sq, sf, f, k, jnp.log(f)


def _hgrn_intra(q, k, g, b, consts):
    _, _, eye, same_block, upper, pos4 = consts
    heads = range(len(q))
    a = [jnp.where(eye, _dot(_bf(q[hh]), _bf(k[hh]), NT), 0.0) for hh in heads]
    saved = [[] for _ in heads]
    for i, m in enumerate(HGRN_LEVELS):
        up = upper[i]
        e = [_hgrn_level_decay(g[hh], b[hh], m, pos4) for hh in heads]
        qt = [jnp.where(up, q[hh] * e[hh], 0.0) for hh in heads]
        kt = [jnp.where(up, 0.0, k[hh] * e[hh]) for hh in heads]
        p = [_dot(_bf(qt[hh]), _bf(kt[hh]), NT) for hh in heads]
        for hh in heads:
            a[hh] = a[hh] + jnp.where(same_block[i], p[hh], 0.0)
            saved[hh].append((e[hh], qt[hh], kt[hh]))
    return a, saved


def _hgrn_specs(tb, nb, rev):
    tmap = (lambda t: nb - 1 - t) if rev else (lambda t: t)
    w = HGRN_PAIR * HGRN_DIM
    zcol = lambda base: pl.BlockSpec((tb, w), lambda h, t: (tmap(t), base // HGRN_PAIR + h))
    return zcol, [zcol(6), zcol(10), zcol(14), zcol(18),
                  pl.BlockSpec((1, w), lambda h, t: (0, h)),
                  pl.BlockSpec((1, HGRN_DIM), lambda h, t: (0, 0))]


def _hgrn_fwd(z, lb, onw, name, exchange=None):
    T = z.shape[0]
    tb = min(HGRN_ROWS, T)
    nb, c, nc = T // tb, HGRN_CHUNK, min(HGRN_ROWS, T) // HGRN_CHUNK

    def body(qr_ref, fr_ref, ir_ref, gr_ref, lb_ref, onw_ref, rec_ref, o_ref, st_ref, state):
        @pl.when(pl.program_id(1) == 0)
        def _():
            state[...] = jnp.zeros_like(state)

        consts = _hgrn_consts()
        lbv = lb_ref[...]
        onwv = onw_ref[...]

        def chunk(ci, carry):
            sl = pl.ds(pl.multiple_of(ci * c, c), c)
            heads = range(HGRN_PAIR)
            lss = [slice(HGRN_DIM * hh, HGRN_DIM * (hh + 1)) for hh in heads]
            gates = [_hgrn_gates(qr_ref[sl, ls], fr_ref[sl, ls], lbv[:, ls]) for ls in lss]
            q, k, g = [t[0] for t in gates], [t[4] for t in gates], [t[5] for t in gates]
            v = [_bf(ir_ref[sl, ls]) for ls in lss]
            b = [_fold3(_dot(consts[0], _split3(g[hh]))) for hh in heads]
            a, _ = _hgrn_intra(q, k, g, b, consts)
            st = [state[hh] for hh in heads]
            for hh in heads:
                st_ref[hh, ci] = st[hh]
            bl = [b[hh][c - 1:c, :] for hh in heads]
            o_state = [_dot(_bf(q[hh] * jnp.exp(b[hh])), _bf(st[hh]), NT) for hh in heads]
            kv = [_dot(v[hh], _bf(k[hh] * jnp.exp(bl[hh] - b[hh])), TN) for hh in heads]
            o = [_dot(_bf(a[hh]), v[hh]) + o_state[hh] for hh in heads]
            for hh in heads:
                state[hh] = st[hh] * jnp.exp(bl[hh]) + kv[hh]
                o_ref[sl, lss[hh]] = o[hh]
                oh, _ = _rms(o[hh])
                gr = gr_ref[sl, lss[hh]]
                rec_ref[sl, lss[hh]] = (oh * onwv * (gr * _sigmoid(gr))).astype(rec_ref.dtype)
            return carry

        lax.fori_loop(0, nc, chunk, 0)

    _, in_specs = _hgrn_specs(tb, nb, False)
    out_blk = pl.BlockSpec((tb, HGRN_PAIR * HGRN_DIM), lambda h, t: (t, h))
    return _hosted_call(
        body, name=name, grid=(HGRN_HEADS // HGRN_PAIR, nb), in_specs=in_specs,
        out_specs=[out_blk, out_blk, pl.BlockSpec((HGRN_PAIR, nc, HGRN_DIM, HGRN_DIM), lambda h, t: (h, t, 0, 0))],
        out_shape=[jax.ShapeDtypeStruct((T, 512), BF16), jax.ShapeDtypeStruct((T, 512), F32),
                   jax.ShapeDtypeStruct((HGRN_HEADS, T // c, HGRN_DIM, HGRN_DIM), F32)],
        scratch=[pltpu.VMEM((HGRN_PAIR, HGRN_DIM, HGRN_DIM), F32)], args=(z, z, z, z, lb, onw),
        semantics=("parallel", "arbitrary"), exchange=exchange)


def _hgrn_bwd(z, lb, onw, o, states, dcat, name, exchange=None):
    T = z.shape[0]
    tb = min(HGRN_ROWS, T)
    nb, c, nc = T // tb, HGRN_CHUNK, min(HGRN_ROWS, T) // HGRN_CHUNK

    def body(qr_ref, fr_ref, ir_ref, gr_ref, lb_ref, onw_ref, o_ref, st_ref, drec_ref,
             dqr_ref, dfr_ref, dir_ref, dgr_ref, dlb_ref, donw_ref, dstate):
        @pl.when(pl.program_id(1) == 0)
        def _():
            dstate[...] = jnp.zeros_like(dstate)
            dlb_ref[...] = jnp.zeros_like(dlb_ref)

        @pl.when((pl.program_id(0) == 0) & (pl.program_id(1) == 0))
        def _():
            donw_ref[...] = jnp.zeros_like(donw_ref)

        consts = _hgrn_consts()
        rev_mat, eye, same_block, upper = consts[1:5]
        lbv = lb_ref[...]
        onwv = onw_ref[...]
        last = lax.broadcasted_iota(jnp.int32, (c, 1), 0) == c - 1

        def chunk(i, carry):
            ci = nc - 1 - i
            sl = pl.ds(pl.multiple_of(ci * c, c), c)
            hs = range(HGRN_PAIR)
            lss = [slice(HGRN_DIM * hh, HGRN_DIM * (hh + 1)) for hh in hs]
            qr = [qr_ref[sl, ls] for ls in lss]
            gates = [_hgrn_gates(qr[hh], fr_ref[sl, lss[hh]], lbv[:, lss[hh]]) for hh in hs]
            q, sq, sf, f, k, g = ([t[j] for t in gates] for j in range(6))
            v = [_bf(ir_ref[sl, ls]) for ls in lss]
            b = [_fold3(_dot(consts[0], _split3(g[hh]))) for hh in hs]
            a, saved = _hgrn_intra(q, k, g, b, consts)
            st = [st_ref[hh, ci] for hh in hs]
            dst = [dstate[hh] for hh in hs]

            gr = [gr_ref[sl, ls] for ls in lss]
            sg = [_sigmoid(gr[hh]) for hh in hs]
            norm = [_rms(o_ref[sl, ls]) for ls in lss]
            oh, r = [t[0] for t in norm], [t[1] for t in norm]
            drec = [drec_ref[sl, ls].astype(F32) for ls in lss]
            don = [drec[hh] * (gr[hh] * sg[hh]) for hh in hs]
            do = [_bf(_rms_bwd(don[hh] * onwv, oh[hh], r[hh])) for hh in hs]
            donw = jnp.sum(don[0] * oh[0], axis=0, keepdims=True)
            for hh in hs:
                dgr_ref[sl, lss[hh]] = (drec[hh] * oh[hh] * onwv
                                        * (sg[hh] * (1.0 + gr[hh] * (1.0 - sg[hh])))).astype(dgr_ref.dtype)
                if hh:
                    donw = donw + jnp.sum(don[hh] * oh[hh], axis=0, keepdims=True)
            donw_ref[...] += donw

            eb = [jnp.exp(b[hh]) for hh in hs]
            bl = [b[hh][c - 1:c, :] for hh in hs]
            ebl = [jnp.exp(bl[hh]) for hh in hs]
            ekb = [jnp.exp(bl[hh] - b[hh]) for hh in hs]
            qe = [q[hh] * eb[hh] for hh in hs]
            ke = [k[hh] * ekb[hh] for hh in hs]
            da = [_dot(do[hh], v[hh], NT) for hh in hs]
            dat = [_dot(v[hh], do[hh], NT) for hh in hs]
            dqe = [_dot(do[hh], _bf(st[hh])) for hh in hs]
            dke = [_dot(v[hh], _bf(dst[hh])) for hh in hs]
            dv_a = [_dot(_bf(a[hh]), do[hh], TN) for hh in hs]
            dv_s = [_dot(_bf(ke[hh]), _bf(dst[hh]), NT) for hh in hs]
            dst_in = [_dot(do[hh], _bf(qe[hh]), TN) for hh in hs]
            dad = [jnp.sum(jnp.where(eye, da[hh], 0.0), axis=1, keepdims=True) for hh in hs]
            dq = [dqe[hh] * eb[hh] + dad[hh] * k[hh] for hh in hs]
            dk = [dke[hh] * ekb[hh] + dad[hh] * q[hh] for hh in hs]
            db_last = [jnp.sum(dke[hh] * ke[hh], axis=0, keepdims=True)
                       + jnp.sum(dst[hh] * st[hh], axis=0, keepdims=True) * ebl[hh] for hh in hs]
            for hh in hs:
                dstate[hh] = dst[hh] * ebl[hh] + dst_in[hh]
                dir_ref[sl, lss[hh]] = (dv_a[hh] + dv_s[hh]).astype(dir_ref.dtype)
            for lvl in range(len(HGRN_LEVELS)):
                xq = [_dot_hilo(jnp.where(same_block[lvl], da[hh], 0.0), saved[hh][lvl][2]) for hh in hs]
                xk = [_dot_hilo(jnp.where(same_block[lvl], dat[hh], 0.0), saved[hh][lvl][1]) for hh in hs]
                for hh in hs:
                    e = saved[hh][lvl][0]
                    dq[hh] = dq[hh] + jnp.where(upper[lvl], xq[hh] * e, 0.0)
                    dk[hh] = dk[hh] + jnp.where(upper[lvl], 0.0, xk[hh] * e)
            db = [q[hh] * dq[hh] - k[hh] * dk[hh] + jnp.where(last, db_last[hh], 0.0) for hh in hs]
            dg = [_fold3(_dot(rev_mat, _split3(db[hh]))) for hh in hs]

            for hh in hs:
                ls = lss[hh]
                dqr_ref[sl, ls] = (dq[hh] * (HGRN_DIM ** -0.5)
                                   * (sq[hh] * (1.0 + qr[hh] * (1.0 - sq[hh])))).astype(dqr_ref.dtype)
                dfk = dg[hh] / f[hh] - dk[hh]
                dfr_ref[sl, ls] = ((1.0 - lbv[:, ls]) * sf[hh] * (1.0 - sf[hh]) * dfk).astype(dfr_ref.dtype)
                dlb_ref[:, ls] += jnp.sum((1.0 - sf[hh]) * dfk, axis=0, keepdims=True)
            return carry

        lax.fori_loop(0, nc, chunk, 0)

    zcol, in_specs = _hgrn_specs(tb, nb, True)
    rblk = pl.BlockSpec((tb, HGRN_PAIR * HGRN_DIM), lambda h, t: (nb - 1 - t, h))
    in_specs = in_specs + [
        rblk,
        pl.BlockSpec((HGRN_PAIR, nc, HGRN_DIM, HGRN_DIM), lambda h, t: (h, nb - 1 - t, 0, 0)),
        pl.BlockSpec((tb, HGRN_PAIR * HGRN_DIM), lambda h, t: (nb - 1 - t, 4 // HGRN_PAIR + h)),
    ]
    return _hosted_call(
        body, name=name, grid=(HGRN_HEADS // HGRN_PAIR, nb), in_specs=in_specs,
        out_specs=[rblk, rblk, rblk, rblk, pl.BlockSpec((1, HGRN_PAIR * HGRN_DIM), lambda h, t: (0, h)),
                   pl.BlockSpec((1, HGRN_DIM), lambda h, t: (0, 0))],
        out_shape=[jax.ShapeDtypeStruct((T, 512), BF16)] * 4
        + [jax.ShapeDtypeStruct((1, 512), F32), jax.ShapeDtypeStruct((1, HGRN_DIM), F32)],
        scratch=[pltpu.VMEM((HGRN_PAIR, HGRN_DIM, HGRN_DIM), F32)], args=(z, z, z, z, lb, onw, o, states, dcat),
        semantics=("arbitrary", "arbitrary"), exchange=exchange)


def _lower_bound(logits, name):
    def body(l_ref, lb_ref):
        l0, l1 = l_ref[0:1, :], l_ref[1:2, :]
        m = jnp.maximum(l0, l1)
        e0, e1 = jnp.exp(l0 - m), jnp.exp(l1 - m)
        lb_ref[...] = e0 / (e0 + e1)

    return pl.pallas_call(
        body, name=name, out_shape=jax.ShapeDtypeStruct((1, logits.shape[1]), F32),
    )(logits)


def _lower_bound_bwd(lb, dlb, name):
    def body(lb_ref, dlb_ref, dl_ref):
        p = lb_ref[...]
        d0 = dlb_ref[...] * p * (1.0 - p)
        dl_ref[0:1, :] = d0
        dl_ref[1:2, :] = -d0

    return pl.pallas_call(
        body, name=name, out_shape=jax.ShapeDtypeStruct((2, lb.shape[1]), F32),
    )(lb, dlb)


CA_ROWS = 512


def _ca_fwd(q, k, v, name):
    T, W = q.shape
    M = k.shape[0]
    tq = min(CA_ROWS, T)
    scale = CA_HEAD_DIM ** -0.5

    def body(q_ref, k_ref, v_ref, o_ref):
        for h in range(CA_HEADS):
            hs = slice(CA_HEAD_DIM * h, CA_HEAD_DIM * (h + 1))
            s = _dot(q_ref[:, hs], k_ref[:, hs], NT) * scale
            p = jnp.exp(s - jnp.max(s, axis=-1, keepdims=True))
            p = p / jnp.sum(p, axis=-1, keepdims=True)
            o_ref[:, hs] = _dot(_bf(p), v_ref[:, hs]).astype(o_ref.dtype)

    full = pl.BlockSpec((M, W), lambda i: (0, 0))
    return pl.pallas_call(
        body, name=name, grid=(T // tq,), in_specs=[_row_spec(tq, W), full, full], out_specs=_row_spec(tq, W),
        out_shape=jax.ShapeDtypeStruct((T, W), BF16), compiler_params=_params("parallel"),
    )(q, k, v)


def _ca_bwd(q, k, v, do, name):
    T, W = q.shape
    M = k.shape[0]
    tq = min(CA_ROWS, T)
    scale = CA_HEAD_DIM ** -0.5

    def body(q_ref, k_ref, v_ref, do_ref, dq_ref, dk_ref, dv_ref):
        @pl.when(pl.program_id(0) == 0)
        def _():
            dk_ref[...] = jnp.zeros_like(dk_ref)
            dv_ref[...] = jnp.zeros_like(dv_ref)

        for h in range(CA_HEADS):
            hs = slice(CA_HEAD_DIM * h, CA_HEAD_DIM * (h + 1))
            qh, kh, vh, doh = q_ref[:, hs], k_ref[:, hs], v_ref[:, hs], do_ref[:, hs]
            s = _dot(qh, kh, NT) * scale
            p = jnp.exp(s - jnp.max(s, axis=-1, keepdims=True))
            p = p / jnp.sum(p, axis=-1, keepdims=True)
            dp = _dot(doh, vh, NT)
            ds = _bf(p * (dp - jnp.sum(p * dp, axis=-1, keepdims=True)) * scale)
            dq_ref[:, hs] = _dot(ds, kh).astype(dq_ref.dtype)
            dk_ref[:, hs] += _dot(ds, qh, TN)
            dv_ref[:, hs] += _dot(_bf(p), doh, TN)

    full = pl.BlockSpec((M, W), lambda i: (0, 0))
    return pl.pallas_call(
        body, name=name, grid=(T // tq,), in_specs=[_row_spec(tq, W), full, full, _row_spec(tq, W)],
        out_specs=[_row_spec(tq, W), full, full],
        out_shape=[jax.ShapeDtypeStruct((T, W), BF16), jax.ShapeDtypeStruct((M, W), F32), jax.ShapeDtypeStruct((M, W), F32)],
        compiler_params=_params("arbitrary"),
    )(q, k, v, do)


FFN_ROWS = 256
FFN_COLS = 1408
GELU_C0 = 0.7978845608028654
GELU_C1 = 0.044715


def _gelu(x):
    t = jnp.tanh(GELU_C0 * (x + GELU_C1 * x * x * x))
    return 0.5 * x * (1.0 + t), t


def _gelu_grad(x, t):
    return 0.5 * (1.0 + t) + 0.5 * x * (1.0 - t * t) * GELU_C0 * (1.0 + 3.0 * GELU_C1 * x * x)


def _shift_down(cur, halo, first, tb):
    row = lax.broadcasted_iota(jnp.int32, (tb, 1), 0)
    h6 = jnp.where(first, 0.0, halo[6:7])
    h7 = jnp.where(first, 0.0, halo[7:8])
    u1 = jnp.where(row == 0, h7, pltpu.roll(cur, 1, 0))
    u2 = jnp.where(row == 0, h6, jnp.where(row == 1, h7, pltpu.roll(cur, 2, 0)))
    return u1, u2


def _conv(u_ref, halo_ref, w_ref, b_ref, half, first, tb):
    cur = u_ref[half]
    u1, u2 = _shift_down(cur, halo_ref[half], first, tb)
    w = w_ref[...]
    return w[0:1] * u2 + w[1:2] * u1 + w[2:3] * cur + b_ref[...], cur, u1, u2


def _ffn_specs(tb, tc, rows_first):
    nj = D_FF // tc
    rc = (lambda a, b: (a, b)) if rows_first else (lambda a, b: (b, a))
    def at(f):
        return lambda a, b: f(*rc(a, b))
    blk = pl.BlockSpec((2, tb, tc), at(lambda t, j: (0, t, j)))
    halo = pl.BlockSpec((2, 8, tc), at(lambda t, j: (0, jnp.maximum(t * (tb // 8) - 1, 0), j)))
    wg = pl.BlockSpec((3, tc), at(lambda t, j: (0, j)))
    wv = pl.BlockSpec((3, tc), at(lambda t, j: (0, j + nj)))
    bg = pl.BlockSpec((1, tc), at(lambda t, j: (0, j)))
    bv = pl.BlockSpec((1, tc), at(lambda t, j: (0, j + nj)))
    flat = pl.BlockSpec((tb, tc), at(lambda t, j: (t, j)))
    return blk, halo, wg, wv, bg, bv, flat


def _glu_fwd(u, cw, cb, name):
    T = u.shape[1]
    tb, tc = min(FFN_ROWS, T), FFN_COLS

    def body(u_ref, halo_ref, wg_ref, wv_ref, bg_ref, bv_ref, a_ref):
        first = pl.program_id(0) == 0
        cg = _conv(u_ref, halo_ref, wg_ref, bg_ref, 0, first, tb)[0]
        cv = _conv(u_ref, halo_ref, wv_ref, bv_ref, 1, first, tb)[0]
        a_ref[...] = (_gelu(cg)[0] * cv).astype(a_ref.dtype)

    blk, halo, wg, wv, bg, bv, flat = _ffn_specs(tb, tc, True)
    return pl.pallas_call(
        body, name=name, grid=(T // tb, D_FF // tc), in_specs=[blk, halo, wg, wv, bg, bv], out_specs=flat,
        out_shape=jax.ShapeDtypeStruct((T, D_FF), BF16), compiler_params=_params("parallel", "parallel"),
    )(u, u, cw, cw, cb, cb)


def _glu_bwd(u, cw, cb, da, name):
    T = u.shape[1]
    tb, tc = min(FFN_ROWS, T), FFN_COLS

    def body(u_ref, halo_ref, wg_ref, wv_ref, bg_ref, bv_ref, da_ref, dc_ref, db_ref, dw_ref):
        first = pl.program_id(1) == 0

        @pl.when(first)
        def _():
            db_ref[...] = jnp.zeros_like(db_ref)
            dw_ref[...] = jnp.zeros_like(dw_ref)

        cg, ug, ug1, ug2 = _conv(u_ref, halo_ref, wg_ref, bg_ref, 0, first, tb)
        cv, uv, uv1, uv2 = _conv(u_ref, halo_ref, wv_ref, bv_ref, 1, first, tb)
        da = da_ref[...]
        gl, t = _gelu(cg)
        dcg = da * cv * _gelu_grad(cg, t)
        dcv = da * gl
        dc_ref[0] = dcg
        dc_ref[1] = dcv
        for half, dc, taps in ((0, dcg, (ug2, ug1, ug)), (1, dcv, (uv2, uv1, uv))):
            db_ref[half] += jnp.sum(dc, axis=0, keepdims=True)
            for tap in range(3):
                dw_ref[half, tap:tap + 1, :] += jnp.sum(dc * taps[tap], axis=0, keepdims=True)

    blk, halo, wg, wv, bg, bv, flat = _ffn_specs(tb, tc, False)
    return pl.pallas_call(
        body, name=name, grid=(D_FF // tc, T // tb), in_specs=[blk, halo, wg, wv, bg, bv, flat],
        out_specs=[blk, pl.BlockSpec((2, 1, tc), lambda j, t: (0, 0, j)), pl.BlockSpec((2, 3, tc), lambda j, t: (0, 0, j))],
        out_shape=[jax.ShapeDtypeStruct((2, T, D_FF), F32), jax.ShapeDtypeStruct((2, 1, D_FF), F32),
                   jax.ShapeDtypeStruct((2, 3, D_FF), F32)],
        compiler_params=_params("parallel", "arbitrary"),
    )(u, u, cw, cw, cb, cb, da)


def _conv_bwd(dc, cw, name):
    T = dc.shape[1]
    tb, tc = min(FFN_ROWS, T), FFN_COLS
    nt, nj = T // tb, D_FF // tc

    def body(dc_ref, halo_ref, wg_ref, wv_ref, du_ref):
        last = pl.program_id(0) == nt - 1
        row = lax.broadcasted_iota(jnp.int32, (tb, 1), 0)
        for half, w_ref in ((0, wg_ref), (1, wv_ref)):
            cur = dc_ref[half]
            halo = halo_ref[half]
            h0 = jnp.where(last, 0.0, halo[0:1])
            h1 = jnp.where(last, 0.0, halo[1:2])
            d1 = jnp.where(row == tb - 1, h0, pltpu.roll(cur, tb - 1, 0))
            d2 = jnp.where(row == tb - 1, h1, jnp.where(row == tb - 2, h0, pltpu.roll(cur, tb - 2, 0)))
            w = w_ref[...]
            du_ref[half] = (w[2:3] * cur + w[1:2] * d1 + w[0:1] * d2).astype(du_ref.dtype)

    blk = pl.BlockSpec((2, tb, tc), lambda t, j: (0, t, j))
    halo = pl.BlockSpec((2, 8, tc), lambda t, j: (0, jnp.minimum((t + 1) * (tb // 8), T // 8 - 1), j))
    wg = pl.BlockSpec((3, tc), lambda t, j: (0, j))
    wv = pl.BlockSpec((3, tc), lambda t, j: (0, j + nj))
    return pl.pallas_call(
        body, name=name, grid=(nt, nj), in_specs=[blk, halo, wg, wv], out_specs=blk,
        out_shape=jax.ShapeDtypeStruct((2, T, D_FF), BF16), compiler_params=_params("parallel", "parallel"),
    )(dc, dc, cw, cw)


def _mesh_pos():
    return lax.axis_index("x"), lax.axis_index("y"), lax.axis_index("c")


def _peer(pos, k):
    return (pos[0] ^ ((k >> 2) & 1), pos[1] ^ ((k >> 1) & 1), pos[2] ^ (k & 1))


def _index(pos):
    return 4 * pos[0] + 2 * pos[1] + pos[2]


class _Exchange:
    def __init__(self, kind, buf):
        assert kind in ("gather", "scatter")
        self.kind, self.buf = kind, buf
        self.out_shape = jax.ShapeDtypeStruct(((N_DEV,) + buf.shape) if kind == "gather" else buf.shape, buf.dtype)
        self.spec = pl.BlockSpec(memory_space=pl.ANY)
        self.scratch = [pltpu.SemaphoreType.DMA((N_DEV - 1,)), pltpu.SemaphoreType.DMA((N_DEV - 1,)),
                        pltpu.SemaphoreType.DMA]

    def _src(self, x_ref, dest):
        return x_ref if self.kind == "gather" else x_ref.at[dest]

    def _copies(self, x_ref, out_ref, send_sems, recv_sems, local_sem):
        pos = _mesh_pos()
        me = _index(pos)
        local = pltpu.make_async_copy(self._src(x_ref, me), out_ref.at[me], local_sem)
        sends, recvs = [], []
        for k in range(1, N_DEV):
            peer = _peer(pos, k)
            sends.append(pltpu.make_async_remote_copy(
                src_ref=self._src(x_ref, _index(peer)), dst_ref=out_ref.at[me], send_sem=send_sems.at[k - 1],
                recv_sem=recv_sems.at[k - 1], device_id=peer, device_id_type=pl.DeviceIdType.MESH))
            recvs.append(pltpu.make_async_remote_copy(
                src_ref=self._src(x_ref, me), dst_ref=out_ref.at[_index(peer)], send_sem=send_sems.at[k - 1],
                recv_sem=recv_sems.at[k - 1], device_id=peer, device_id_type=pl.DeviceIdType.MESH))
        return local, sends, recvs

    def start(self, *refs):
        local, sends, _ = self._copies(*refs)
        local.start()
        for cp in sends:
            cp.start()

    def finish(self, *refs):
        local, sends, recvs = self._copies(*refs)
        for cp in recvs:
            cp.wait_recv()
        for cp in sends:
            cp.wait_send()
        local.wait()


def _hosted_call(body, *, name, grid, in_specs, out_specs, out_shape, scratch, args, semantics, exchange=None):
    if exchange is None:
        return pl.pallas_call(
            body, name=name, grid=grid, in_specs=in_specs, out_specs=out_specs, out_shape=out_shape,
            scratch_shapes=scratch, compiler_params=_params(*semantics))(*args)
    n_in, n_out, n_scr = len(in_specs), len(out_specs), len(scratch)

    def hosted(*refs):
        ins, x_ref = refs[:n_in], refs[n_in]
        outs, land_ref = refs[n_in + 1:n_in + 1 + n_out], refs[n_in + 1 + n_out]
        rest = refs[n_in + n_out + 2:]
        sems = rest[n_scr:]
        ids = [pl.program_id(a) for a in range(len(grid))]
        first, last = ids[0] == 0, ids[0] == grid[0] - 1
        for a in range(1, len(grid)):
            first, last = first & (ids[a] == 0), last & (ids[a] == grid[a] - 1)

        @pl.when(first)
        def _():
            exchange.start(x_ref, land_ref, *sems)

        body(*ins, *outs, *rest[:n_scr])

        @pl.when(last)
        def _():
            exchange.finish(x_ref, land_ref, *sems)

    return pl.pallas_call(
        hosted, name=name, grid=grid, in_specs=list(in_specs) + [exchange.spec],
        out_specs=list(out_specs) + [exchange.spec], out_shape=list(out_shape) + [exchange.out_shape],
        scratch_shapes=list(scratch) + exchange.scratch, compiler_params=_params(*(["arbitrary"] * len(grid))),
    )(*args, exchange.buf)


def _exchange_alone(exchange, name):
    def body(x_ref, out_ref, send_sems, recv_sems, local_sem):
        exchange.start(x_ref, out_ref, send_sems, recv_sems, local_sem)
        exchange.finish(x_ref, out_ref, send_sems, recv_sems, local_sem)

    return pl.pallas_call(
        body, name=name, out_shape=exchange.out_shape, in_specs=[exchange.spec], out_specs=exchange.spec,
        scratch_shapes=exchange.scratch)(exchange.buf)


def _adamw(w, g, m, v):
    m = ADAM_B1 * m + (1.0 - ADAM_B1) * g
    v = ADAM_B2 * v + (1.0 - ADAM_B2) * (g * g)
    m_hat = m / (1.0 - ADAM_B1 ** ADAM_STEP)
    v_hat = v / (1.0 - ADAM_B2 ** ADAM_STEP)
    delta = -ADAM_LR * (m_hat / (jnp.sqrt(v_hat) + ADAM_EPS) + ADAM_WD * w)
    return delta, m, v


def _sum_adamw(parts, w, m, v, name):
    R, C = w.shape
    tr = max(t for t in range(16, ROWS + 1, 16) if R % t == 0)

    def body(p_ref, w_ref, m_ref, v_ref, g_ref, d_ref, mo_ref, vo_ref):
        g = p_ref[0].astype(F32)
        for i in range(1, N_DEV):
            g = g + p_ref[i].astype(F32)
        g_ref[...] = g
        d_ref[...], mo_ref[...], vo_ref[...] = _adamw(w_ref[...], g, m_ref[...], v_ref[...])

    row = _row_spec(tr, C)
    return pl.pallas_call(
        body, name=name, grid=(R // tr,),
        in_specs=[pl.BlockSpec((N_DEV, tr, C), lambda i: (0, i, 0)), row, row, row], out_specs=[row] * 4,
        out_shape=[jax.ShapeDtypeStruct((R, C), F32)] * 4, compiler_params=_params("parallel"),
    )(parts, w, m, v)


def _sum_parts(parts, name):
    _, R, C = parts.shape

    def body(p_ref, g_ref):
        g = p_ref[0]
        for i in range(1, N_DEV):
            g = g + p_ref[i]
        g_ref[...] = g

    return pl.pallas_call(body, name=name, out_shape=jax.ShapeDtypeStruct((R, C), F32))(parts)


def _adamw_call(w, g, m, v, name):
    def body(w_ref, g_ref, m_ref, v_ref, d_ref, mo_ref, vo_ref):
        d_ref[...], mo_ref[...], vo_ref[...] = _adamw(w_ref[...], g_ref[...], m_ref[...], v_ref[...])

    return pl.pallas_call(body, name=name, out_shape=[jax.ShapeDtypeStruct(w.shape, F32)] * 3)(w, g, m, v)


BIG = ("w_in", "w_out", "ca_wq", "ca_wk", "ca_wv", "ca_wo", "ffn_w_up", "ffn_w_down")
BIG_FULL = {"w_in": (1024, 2816), "w_out": (1024, 1024), "ca_wq": (1024, 1024), "ca_wk": (1024, 1024),
            "ca_wv": (1024, 1024), "ca_wo": (1024, 1024), "ffn_w_up": (1024, 5632), "ffn_w_down": (2816, 1024)}
FIRST = ("w_in",)
LATER = BIG[1:]
COL_SHARDED = ("w_in", "ffn_w_up")
PACK_COLS = 1024
NORMS = ("mix_pre_norm", "mix_post_norm", "ca_pre_norm", "mem_norm", "ca_post_norm", "ffn_pre_norm", "ffn_post_norm")
SMALL_ROWS = 32


def _big_rows(name):
    r, c = BIG_FULL[name]
    return r * c // N_DEV // PACK_COLS


def _pack_shards(shards, names):
    return jnp.concatenate([shards[n].reshape(_big_rows(n), PACK_COLS) for n in names], axis=0)


def _unpack_shards(pack, shapes, names):
    out, r0 = {}, 0
    for n in names:
        out[n] = pack[r0:r0 + _big_rows(n)].reshape(shapes[n])
        r0 += _big_rows(n)
    return out


def _unpack_gathered(gathered, names):
    out, r0 = {}, 0
    for n in names:
        rows = _big_rows(n)
        blk = gathered[:, r0:r0 + rows]
        r, c = BIG_FULL[n]
        if n in COL_SHARDED:
            out[n] = blk.reshape(N_DEV, r, c // N_DEV).transpose(1, 0, 2).reshape(r, c)
        else:
            out[n] = blk.reshape(r, c)
        r0 += rows
    return out


def _pack_full_grads(grads, names):
    parts = []
    for n in names:
        r, c = BIG_FULL[n]
        g = grads[n]
        if n in COL_SHARDED:
            g = g.reshape(r, N_DEV, c // N_DEV).transpose(1, 0, 2)
        parts.append(g.reshape(N_DEV, _big_rows(n), PACK_COLS))
    return jnp.concatenate(parts, axis=1).astype(BF16)


def _pad_row(vec):
    vec = vec.reshape(-1)
    n = -(-vec.shape[0] // PACK_COLS) * PACK_COLS
    return jnp.pad(vec, (0, n - vec.shape[0])).reshape(-1, PACK_COLS)


def _pack_small(norms, logits, out_norm, sinks, loss, conv_b, conv_w):
    rows = [_pad_row(norms[n]) for n in NORMS]
    rows.append(_pad_row(logits))
    rows.append(_pad_row(jnp.concatenate([out_norm.reshape(-1), sinks.reshape(-1), loss.reshape(-1)])))
    rows.append(_pad_row(conv_b))
    rows.append(_pad_row(conv_w))
    pack = jnp.concatenate(rows, axis=0)
    return jnp.pad(pack, ((0, SMALL_ROWS - pack.shape[0]), (0, 0)))


def _unpack_small(pack):
    norms = {n: pack[i:i + 1] for i, n in enumerate(NORMS)}
    logits = pack[7].reshape(2, 512)
    out_norm = pack[8:9, 0:128]
    sinks = pack[8:9, 128:136]
    loss = pack[8, 136]
    conv_b = pack[9:15].reshape(-1)[:2 * D_FF].reshape(1, 2 * D_FF)
    conv_w = pack[15:32].reshape(-1)[:6 * D_FF].reshape(3, 2 * D_FF)
    return norms, logits, out_norm, sinks, loss, conv_b, conv_w


def kernel(x, mem, mix_pre_norm, w_in, attn_sinks, hgrn_lb_logits, hgrn_out_norm, w_out, mix_post_norm, ca_pre_norm, mem_norm, ca_wq, ca_wk, ca_wv, ca_wo, ca_post_norm, ffn_pre_norm, ffn_w_up, ffn_conv_w, ffn_conv_b, ffn_w_down, ffn_post_norm, loss_target, m_mix_pre_norm, m_w_in, m_attn_sinks, m_hgrn_lb_logits, m_hgrn_out_norm, m_w_out, m_mix_post_norm, m_ca_pre_norm, m_mem_norm, m_ca_wq, m_ca_wk, m_ca_wv, m_ca_wo, m_ca_post_norm, m_ffn_pre_norm, m_ffn_w_up, m_ffn_conv_w, m_ffn_conv_b, m_ffn_w_down, m_ffn_post_norm, v_mix_pre_norm, v_w_in, v_attn_sinks, v_hgrn_lb_logits, v_hgrn_out_norm, v_w_out, v_mix_post_norm, v_ca_pre_norm, v_mem_norm, v_ca_wq, v_ca_wk, v_ca_wv, v_ca_wo, v_ca_post_norm, v_ffn_pre_norm, v_ffn_w_up, v_ffn_conv_w, v_ffn_conv_b, v_ffn_w_down, v_ffn_post_norm):
    names = ["mix_pre_norm", "w_in", "attn_sinks", "hgrn_lb_logits", "hgrn_out_norm", "w_out", "mix_post_norm",
             "ca_pre_norm", "mem_norm", "ca_wq", "ca_wk", "ca_wv", "ca_wo", "ca_post_norm", "ffn_pre_norm",
             "ffn_w_up", "ffn_conv_w", "ffn_conv_b", "ffn_w_down", "ffn_post_norm"]
    w_all = dict(zip(names, [mix_pre_norm, w_in, attn_sinks, hgrn_lb_logits, hgrn_out_norm, w_out, mix_post_norm,
                             ca_pre_norm, mem_norm, ca_wq, ca_wk, ca_wv, ca_wo, ca_post_norm, ffn_pre_norm,
                             ffn_w_up, ffn_conv_w, ffn_conv_b, ffn_w_down, ffn_post_norm]))
    m_all = dict(zip(names, [m_mix_pre_norm, m_w_in, m_attn_sinks, m_hgrn_lb_logits, m_hgrn_out_norm, m_w_out,
                             m_mix_post_norm, m_ca_pre_norm, m_mem_norm, m_ca_wq, m_ca_wk, m_ca_wv, m_ca_wo,
                             m_ca_post_norm, m_ffn_pre_norm, m_ffn_w_up, m_ffn_conv_w, m_ffn_conv_b, m_ffn_w_down,
                             m_ffn_post_norm]))
    v_all = dict(zip(names, [v_mix_pre_norm, v_w_in, v_attn_sinks, v_hgrn_lb_logits, v_hgrn_out_norm, v_w_out,
                             v_mix_post_norm, v_ca_pre_norm, v_mem_norm, v_ca_wq, v_ca_wk, v_ca_wv, v_ca_wo,
                             v_ca_post_norm, v_ffn_pre_norm, v_ffn_w_up, v_ffn_conv_w, v_ffn_conv_b, v_ffn_w_down,
                             v_ffn_post_norm]))
    dev = _index(_mesh_pos())

    shards = {n: w_all[n][0] for n in BIG}
    w_packs = {grp: _pack_shards(shards, grp) for grp in (FIRST, LATER)}
    conv_w_rows = _exchange_alone(_Exchange("gather", _pad_row(ffn_conv_w[0])), "gather_conv_w")
    conv_w_full = conv_w_rows.reshape(N_DEV, -1)[:, :3 * 704].reshape(N_DEV, 3, 704).transpose(1, 0, 2).reshape(3, 2 * D_FF)

    received, grads_small, loss_local, grad_x = _local_step(
        x[0], mem[0], loss_target[0], {grp: w_packs[grp].astype(BF16) for grp in w_packs}, conv_w_full,
        {n: w_all[n] for n in NORMS}, attn_sinks, hgrn_lb_logits, hgrn_out_norm, ffn_conv_b)

    small_pack = _pack_small(grads_small["norms"], grads_small["logits"], grads_small["out_norm"], grads_small["sinks"],
                             loss_local, grads_small["conv_b"], grads_small["conv_w"])
    small_sum = _sum_parts(_exchange_alone(_Exchange("gather", small_pack), "gather_small"), "sum_small")
    g_norms, g_logits, g_out_norm, g_sinks, loss, g_conv_b, g_conv_w_full = _unpack_small(small_sum)
    g_conv_w = lax.dynamic_slice_in_dim(g_conv_w_full, dev * 704, 704, axis=1)

    shard_shapes = {n: w_all[n].shape for n in BIG}
    out_g, out_d, out_m, out_v = {}, {}, {}, {}
    for grp, tag in ((FIRST, "first"), (LATER, "later")):
        packs = _sum_adamw(
            received[grp], w_packs[grp], _pack_shards({n: m_all[n][0] for n in grp}, grp),
            _pack_shards({n: v_all[n][0] for n in grp}, grp), "adamw_" + tag)
        for tree, pack in zip((out_g, out_d, out_m, out_v), packs):
            tree.update(_unpack_shards(pack, shard_shapes, grp))

    small_g = dict(g_norms)
    small_g.update(attn_sinks=g_sinks, hgrn_lb_logits=g_logits, hgrn_out_norm=g_out_norm,
                   ffn_conv_b=g_conv_b, ffn_conv_w=g_conv_w[None])
    small_names = [n for n in names if n not in BIG]

    def small_pack_of(tree):
        return jnp.concatenate([_pad_row(tree[n]) for n in small_names], axis=0)

    ds, ms, vs = _adamw_call(small_pack_of(w_all), small_pack_of(small_g), small_pack_of(m_all), small_pack_of(v_all),
                             "adamw_small")

    def small_unpack(pack):
        out, r0 = {}, 0
        for n in small_names:
            size = 1
            for s in w_all[n].shape:
                size *= s
            rows = -(-size // PACK_COLS)
            out[n] = pack[r0:r0 + rows].reshape(-1)[:size].reshape(w_all[n].shape)
            r0 += rows
        return out

    sd, sm, sv = small_unpack(ds), small_unpack(ms), small_unpack(vs)
    for n in small_names:
        out_g[n] = small_g[n].reshape(w_all[n].shape)
        out_d[n], out_m[n], out_v[n] = sd[n], sm[n], sv[n]

    return (loss, grad_x[None], *[out_g[n] for n in names], *[out_d[n] for n in names],
            *[out_m[n] for n in names], *[out_v[n] for n in names])


def _local_step(x, mem, target, w_packs, conv_w, norms, sinks, lb_logits, out_norm, conv_b):
    g1, g2, g3 = norms["mix_pre_norm"], norms["mix_post_norm"], norms["ca_pre_norm"]
    g4, g5, g6, g7 = norms["mem_norm"], norms["ca_post_norm"], norms["ffn_pre_norm"], norms["ffn_post_norm"]

    h1, gathered = _norm_fwd(x, g1, "mix_norm", exchange=_Exchange("gather", w_packs[FIRST]))
    w_in = _unpack_gathered(gathered, FIRST)["w_in"]
    z = _mm(h1, w_in, mode="nn", out_dtype=F32, name="in_proj", tn=1408)
    attn, lse = _swa_fwd(z, sinks, "swa_fwd")
    lb = _lower_bound(lb_logits, "lower_bound")
    rec, o_rec, states, gathered = _hgrn_fwd(z, lb, out_norm, "hgrn_fwd", exchange=_Exchange("gather", w_packs[LATER]))
    wf = _unpack_gathered(gathered, LATER)
    w_out, wq, wk, wv, wo = wf["w_out"], wf["ca_wq"], wf["ca_wk"], wf["ca_wv"], wf["ca_wo"]
    w_up, w_down = wf["ffn_w_up"], wf["ffn_w_down"]
    cat = jnp.concatenate([attn, rec], axis=1)
    mix = _mm(cat, w_out, mode="nn", out_dtype=F32, name="out_proj")
    x1, h2 = _post_pre(x, mix, g2, g3, "mix_post")
    mem_n = _norm_fwd(mem, g4, "mem_norm")
    q = _mm(h2, wq, mode="nn", out_dtype=BF16, name="ca_q")
    k = _mm(mem_n, wk, mode="nn", out_dtype=BF16, name="ca_k")
    v = _mm(mem_n, wv, mode="nn", out_dtype=BF16, name="ca_v")
    oc = _ca_fwd(q, k, v, "ca_fwd")
    c = _mm(oc, wo, mode="nn", out_dtype=F32, name="ca_o")
    x2, h3 = _post_pre(x1, c, g5, g6, "ca_post")
    u = _mm(h3, w_up, mode="nn", out_dtype=F32, name="ffn_up", tn=1408, split_out=True)
    a = _glu_fwd(u, conv_w, conv_b, "glu_fwd")
    y = _mm(a, w_down, mode="nn", out_dtype=F32, name="ffn_down", tk=2816)
    loss, dx3, dy, dg7 = _final(x2, y, g7, target, "loss_head")

    da = _mm(dy, w_down, mode="nt", out_dtype=F32, name="ffn_down_dx", tn=1408)
    d_w_down = _mm(a, dy, mode="tn", out_dtype=F32, name="ffn_down_dw", tm=1408, tk=512)
    dc, d_cb, d_cw = _glu_bwd(u, conv_w, conv_b, da, "glu_bwd")
    du = _conv_bwd(dc, conv_w, "conv_bwd")
    dh3 = _mm(du, w_up, mode="nt", out_dtype=F32, name="ffn_up_dx", tm=2048, tk=1408, split_a=True)
    d_w_up = _mm(h3, du, mode="tn", out_dtype=F32, name="ffn_up_dw", tm=1024, tn=1408, tk=512, split_b=True)
    dx2, dcv, dg6, dg5 = _norm_bwd2(dx3, dh3, x2, g6, c, g5, "ca_post_bwd")
    doc = _mm(dcv, wo, mode="nt", out_dtype=BF16, name="ca_o_dx")
    d_wo = _mm(oc, dcv, mode="tn", out_dtype=F32, name="ca_o_dw", tm=1024, tk=512)
    dq, dk, dv = _ca_bwd(q, k, v, doc, "ca_bwd")
    d_wq = _mm(h2, dq, mode="tn", out_dtype=F32, name="ca_q_dw", tm=1024, tk=512)
    dh2 = _mm(dq, wq, mode="nt", out_dtype=F32, name="ca_q_dx")
    d_wk = _mm(mem_n, dk, mode="tn", out_dtype=F32, name="ca_k_dw", tm=1024)
    d_wv = _mm(mem_n, dv, mode="tn", out_dtype=F32, name="ca_v_dw", tm=1024)
    dmem_k = _mm(dk, wk, mode="nt", out_dtype=F32, name="ca_k_dx")
    dmem_v = _mm(dv, wv, mode="nt", out_dtype=F32, name="ca_v_dx")
    dg4 = _gain_bwd(mem, dmem_k, dmem_v, "mem_norm_bwd")
    dx1, dmix, dg3, dg2 = _norm_bwd2(dx2, dh2, x1, g3, mix, g2, "mix_post_bwd")
    dcat = _mm(dmix, w_out, mode="nt", out_dtype=F32, name="out_proj_dx")
    d_w_out = _mm(cat, dmix, mode="tn", out_dtype=F32, name="out_proj_dw", tm=1024, tk=512)
    later = {"w_out": d_w_out, "ca_wq": d_wq, "ca_wk": d_wk, "ca_wv": d_wv, "ca_wo": d_wo,
             "ffn_w_up": d_w_up, "ffn_w_down": d_w_down}
    dqr, dfr, dir_, dgr, dlb, donw, got_later = _hgrn_bwd(
        z, lb, out_norm, o_rec, states, dcat, "hgrn_bwd", exchange=_Exchange("scatter", _pack_full_grads(later, LATER)))
    dq_a, dka, dkb, dva, dvb, dsinks = _swa_bwd(z, sinks, dcat, lse, "swa_bwd")
    dz = _assemble_dz(dq_a, dka, dkb, dva, dvb, dqr, dfr, dir_, dgr, "assemble_dz")
    d_w_in = _mm(h1, dz, mode="tn", out_dtype=F32, name="in_proj_dw", tm=1024, tn=1408, tk=512)
    dh1, got_first = _mm(dz, w_in, mode="nt", out_dtype=F32, name="in_proj_dx", tk=2816,
                         exchange=_Exchange("scatter", _pack_full_grads({"w_in": d_w_in}, FIRST)))
    dx, dg1 = _norm_bwd1(dx1, dh1, x, g1, "mix_norm_bwd")

    small = {
        "norms": {"mix_pre_norm": dg1, "mix_post_norm": dg2, "ca_pre_norm": dg3, "mem_norm": dg4,
                  "ca_post_norm": dg5, "ffn_pre_norm": dg6, "ffn_post_norm": dg7},
        "logits": _lower_bound_bwd(lb, dlb, "lower_bound_bwd"),
        "out_norm": donw,
        "sinks": dsinks,
        "conv_b": jnp.concatenate([d_cb[0], d_cb[1]], axis=1),
        "conv_w": jnp.concatenate([d_cw[0], d_cw[1]], axis=1),
    }
    return {FIRST: got_first, LATER: got_later}, small, loss[0, 0:1], dx
```

```python
import jax
import jax.numpy as jnp
from jax import lax
from jax.experimental import pallas as pl
from jax.experimental.pallas import tpu as pltpu

F32 = jnp.float32
BF16 = jnp.bfloat16
EPS = 1e-6
N_DEV = 8
MESH_AXES = ("x", "y", "c")

ATTN_HEAD_DIM = 64
ATTN_Q_HEADS = 8
ATTN_KV_HEADS = 2
ATTN_BLOCK = 128
HGRN_HEADS = 4
HGRN_DIM = 128
HGRN_CHUNK = 64
HGRN_PAIR = 2
HGRN_LEVELS = (32, 16, 8, 4, 2, 1)
CA_HEADS = 4
CA_HEAD_DIM = 256
D_FF = 2816

ADAM_LR = 0.001
ADAM_B1 = 0.9
ADAM_B2 = 0.999
ADAM_EPS = 1e-08
ADAM_WD = 0.01
ADAM_STEP = 10

VMEM_LIMIT = 56 << 20
LANE = 128

NT = (((1,), (1,)), ((), ()))
TN = (((0,), (0,)), ((), ()))


def _params(*sem):
    return pltpu.CompilerParams(dimension_semantics=sem, vmem_limit_bytes=VMEM_LIMIT)


def _tile(n, cap):
    if n <= cap:
        return n
    best = 0
    for t in range(LANE, cap + 1, LANE):
        if n % t == 0:
            best = t
    assert best, (n, cap)
    return best


def _dot(a, b, dims=None):
    if dims is None:
        return jnp.dot(a, b, preferred_element_type=F32)
    return lax.dot_general(a, b, dims, preferred_element_type=F32)


def _bf(x):
    return x.astype(BF16)


def _sigmoid(x):
    return 1.0 / (1.0 + jnp.exp(-x))


def _rms(x):
    r = lax.rsqrt(jnp.mean(x * x, axis=-1, keepdims=True) + EPS)
    return x * r, r


def _rms_bwd(dxh, xh, r):
    return r * (dxh - xh * jnp.mean(dxh * xh, axis=-1, keepdims=True))


def _mm(a, b, *, mode, out_dtype, name, tm=512, tn=1024, tk=1024, split_a=False, split_b=False, split_out=False,
        exchange=None):
    def dims(arr, split):
        if split:
            return arr.shape[1], 2 * arr.shape[2]
        return arr.shape

    ar, ac = dims(a, split_a)
    br, bc = dims(b, split_b)
    if mode == "nn":
        M, K, N = ar, ac, bc
        assert br == K
    elif mode == "nt":
        M, K, N = ar, ac, br
        assert bc == K
    else:
        K, M, N = ar, ac, bc
        assert br == K
    a_cols_half = ac // 2 if split_a else None
    b_cols_half = bc // 2 if split_b else None
    tm = _tile(M, tm)
    tn = _tile((N // 2) if (split_out or (split_b and mode != "nt")) else N, tn)
    tk = _tile((K // 2) if ((split_a and mode != "tn") or (split_b and mode == "nt")) else K, tk)
    if split_a and mode == "tn":
        tm = _tile(M // 2, tm)
    gm, gn, gk = M // tm, N // tn, K // tk
    a_bytes, b_bytes = a.size * a.dtype.itemsize, b.size * b.dtype.itemsize
    rows_outer = gk > 1 or a_bytes + gm * b_bytes <= gn * a_bytes + b_bytes
    grid = (gm, gn, gk) if rows_outer else (gn, gm, gk)

    def spec(split, half, blk, rc):
        def imap(p, q, k):
            r, c = rc(*((p, q) if rows_outer else (q, p)), k)
            if not split:
                return (r, c)
            per_half = half // blk[1]
            return (c // per_half, r, c % per_half)

        return pl.BlockSpec(((None,) + blk) if split else blk, imap)

    if mode == "nn":
        a_spec = spec(split_a, a_cols_half, (tm, tk), lambda i, j, k: (i, k))
        b_spec = spec(split_b, b_cols_half, (tk, tn), lambda i, j, k: (k, j))
        dn = None
    elif mode == "nt":
        a_spec = spec(split_a, a_cols_half, (tm, tk), lambda i, j, k: (i, k))
        b_spec = spec(split_b, b_cols_half, (tn, tk), lambda i, j, k: (j, k))
        dn = NT
    else:
        a_spec = spec(split_a, a_cols_half, (tk, tm), lambda i, j, k: (k, i))
        b_spec = spec(split_b, b_cols_half, (tk, tn), lambda i, j, k: (k, j))
        dn = TN
    o_spec = spec(split_out, N // 2 if split_out else None, (tm, tn), lambda i, j, k: (i, j))
    out_shape = (2, M, N // 2) if split_out else (M, N)

    if gk == 1:
        def body(a_ref, b_ref, o_ref):
            o_ref[...] = _dot(_bf(a_ref[...]), _bf(b_ref[...]), dn).astype(o_ref.dtype)
        scratch = []
    else:
        def body(a_ref, b_ref, o_ref, acc_ref):
            k = pl.program_id(2)

            @pl.when(k == 0)
            def _():
                acc_ref[...] = jnp.zeros_like(acc_ref)

            acc_ref[...] += _dot(_bf(a_ref[...]), _bf(b_ref[...]), dn)

            @pl.when(k == gk - 1)
            def _():
                o_ref[...] = acc_ref[...].astype(o_ref.dtype)
        scratch = [pltpu.VMEM((tm, tn), F32)]

    out = _hosted_call(
        body, name=name, grid=grid, in_specs=[a_spec, b_spec], out_specs=[o_spec],
        out_shape=[jax.ShapeDtypeStruct(out_shape, out_dtype)], scratch=scratch, args=(a, b),
        semantics=("parallel", "parallel", "arbitrary"), exchange=exchange)
    return out[0] if exchange is None else out


ROWS = 256


def _row_spec(tr, cols):
    return pl.BlockSpec((tr, cols), lambda i: (i, 0))


def _vec_spec(cols):
    return pl.BlockSpec((1, cols), lambda i: (0, 0))


def _norm_fwd(x, g, name, exchange=None):
    T, Dm = x.shape
    tr = min(ROWS, T)

    def body(x_ref, g_ref, h_ref):
        xh, _ = _rms(x_ref[...])
        h_ref[...] = (xh * g_ref[...]).astype(h_ref.dtype)

    out = _hosted_call(
        body, name=name, grid=(T // tr,), in_specs=[_row_spec(tr, Dm), _vec_spec(Dm)], out_specs=[_row_spec(tr, Dm)],
        out_shape=[jax.ShapeDtypeStruct((T, Dm), BF16)], scratch=[], args=(x, g), semantics=("parallel",),
        exchange=exchange)
    return out[0] if exchange is None else out


def _post_pre(x, m, g_post, g_pre, name):
    T, Dm = x.shape
    tr = min(ROWS, T)

    def body(x_ref, m_ref, gp_ref, gn_ref, xo_ref, h_ref):
        mh, _ = _rms(m_ref[...])
        xn = x_ref[...] + mh * gp_ref[...]
        xo_ref[...] = xn
        xh, _ = _rms(xn)
        h_ref[...] = (xh * gn_ref[...]).astype(h_ref.dtype)

    return pl.pallas_call(
        body, name=name, grid=(T // tr,),
        in_specs=[_row_spec(tr, Dm), _row_spec(tr, Dm), _vec_spec(Dm), _vec_spec(Dm)],
        out_specs=[_row_spec(tr, Dm), _row_spec(tr, Dm)],
        out_shape=[jax.ShapeDtypeStruct((T, Dm), F32), jax.ShapeDtypeStruct((T, Dm), BF16)],
        compiler_params=_params("parallel"),
    )(x, m, g_post, g_pre)


def _final(x2, y, g_post, target, name):
    T, Dm = x2.shape
    tr = min(ROWS, T)

    def body(x_ref, y_ref, g_ref, t_ref, loss_ref, dx_ref, dy_ref, dg_ref):
        @pl.when(pl.program_id(0) == 0)
        def _():
            loss_ref[...] = jnp.zeros_like(loss_ref)
            dg_ref[...] = jnp.zeros_like(dg_ref)

        g = g_ref[...]
        yh, r = _rms(y_ref[...])
        d = x_ref[...] + yh * g - t_ref[...]
        loss_ref[...] += jnp.zeros((1, LANE), F32) + 0.5 * jnp.sum(jnp.mean(d * d, axis=-1, keepdims=True))
        dx = d * (1.0 / Dm)
        dx_ref[...] = dx
        dy_ref[...] = _rms_bwd(dx * g, yh, r).astype(dy_ref.dtype)
        dg_ref[...] += jnp.sum(dx * yh, axis=0, keepdims=True)

    return pl.pallas_call(
        body, name=name, grid=(T // tr,),
        in_specs=[_row_spec(tr, Dm), _row_spec(tr, Dm), _vec_spec(Dm), _row_spec(tr, Dm)],
        out_specs=[_vec_spec(LANE), _row_spec(tr, Dm), _row_spec(tr, Dm), _vec_spec(Dm)],
        out_shape=[jax.ShapeDtypeStruct((1, LANE), F32), jax.ShapeDtypeStruct((T, Dm), F32),
                   jax.ShapeDtypeStruct((T, Dm), BF16), jax.ShapeDtypeStruct((1, Dm), F32)],
        compiler_params=_params("arbitrary"),
    )(x2, y, g_post, target)


def _norm_bwd2(dx_cur, dh, x_prev, g_pre, m_prev, g_post, name):
    T, Dm = x_prev.shape
    tr = min(ROWS, T)

    def body(dx_ref, dh_ref, x_ref, gn_ref, m_ref, gp_ref, dxo_ref, dm_ref, dgn_ref, dgp_ref):
        @pl.when(pl.program_id(0) == 0)
        def _():
            dgn_ref[...] = jnp.zeros_like(dgn_ref)
            dgp_ref[...] = jnp.zeros_like(dgp_ref)

        dh = dh_ref[...].astype(F32)
        xh, r = _rms(x_ref[...])
        dx = dx_ref[...] + _rms_bwd(dh * gn_ref[...], xh, r)
        dxo_ref[...] = dx
        dgn_ref[...] += jnp.sum(dh * xh, axis=0, keepdims=True)
        mh, rm = _rms(m_ref[...])
        dm_ref[...] = _rms_bwd(dx * gp_ref[...], mh, rm).astype(dm_ref.dtype)
        dgp_ref[...] += jnp.sum(dx * mh, axis=0, keepdims=True)

    return pl.pallas_call(
        body, name=name, grid=(T // tr,),
        in_specs=[_row_spec(tr, Dm), _row_spec(tr, Dm), _row_spec(tr, Dm), _vec_spec(Dm), _row_spec(tr, Dm), _vec_spec(Dm)],
        out_specs=[_row_spec(tr, Dm), _row_spec(tr, Dm), _vec_spec(Dm), _vec_spec(Dm)],
        out_shape=[jax.ShapeDtypeStruct((T, Dm), F32), jax.ShapeDtypeStruct((T, Dm), BF16),
                   jax.ShapeDtypeStruct((1, Dm), F32), jax.ShapeDtypeStruct((1, Dm), F32)],
        compiler_params=_params("arbitrary"),
    )(dx_cur, dh, x_prev, g_pre, m_prev, g_post)


def _norm_bwd1(dx_cur, dh, x_prev, g_pre, name):
    T, Dm = x_prev.shape
    tr = min(ROWS, T)

    def body(dx_ref, dh_ref, x_ref, gn_ref, dxo_ref, dgn_ref):
        @pl.when(pl.program_id(0) == 0)
        def _():
            dgn_ref[...] = jnp.zeros_like(dgn_ref)

        dh = dh_ref[...].astype(F32)
        xh, r = _rms(x_ref[...])
        dxo_ref[...] = dx_ref[...] + _rms_bwd(dh * gn_ref[...], xh, r)
        dgn_ref[...] += jnp.sum(dh * xh, axis=0, keepdims=True)

    return pl.pallas_call(
        body, name=name, grid=(T // tr,),
        in_specs=[_row_spec(tr, Dm), _row_spec(tr, Dm), _row_spec(tr, Dm), _vec_spec(Dm)],
        out_specs=[_row_spec(tr, Dm), _vec_spec(Dm)],
        out_shape=[jax.ShapeDtypeStruct((T, Dm), F32), jax.ShapeDtypeStruct((1, Dm), F32)],
        compiler_params=_params("arbitrary"),
    )(dx_cur, dh, x_prev, g_pre)


def _gain_bwd(x, dh_a, dh_b, name):
    T, Dm = x.shape

    def body(x_ref, a_ref, b_ref, dg_ref):
        xh, _ = _rms(x_ref[...])
        dg_ref[...] = jnp.sum((a_ref[...] + b_ref[...]) * xh, axis=0, keepdims=True)

    return pl.pallas_call(
        body, name=name, grid=(1,), in_specs=[_row_spec(T, Dm)] * 3, out_specs=_vec_spec(Dm),
        out_shape=jax.ShapeDtypeStruct((1, Dm), F32), compiler_params=_params("arbitrary"),
    )(x, dh_a, dh_b)


ATTN_GROUP = ATTN_Q_HEADS // ATTN_KV_HEADS


def _swa_mask(n):
    rows = ATTN_GROUP * ATTN_BLOCK
    row = lax.broadcasted_iota(jnp.int32, (rows, 2 * ATTN_BLOCK), 0) & (ATTN_BLOCK - 1)
    col = lax.broadcasted_iota(jnp.int32, (rows, 2 * ATTN_BLOCK), 1)
    diff = row + ATTN_BLOCK - col
    return (diff >= 0) & (diff < ATTN_BLOCK) & ((col >= ATTN_BLOCK) | (n > 0))


def _swa_rows(ref, hk, dtype):
    hd = ATTN_HEAD_DIM
    return jnp.concatenate(
        [ref[:, hd * (hk * ATTN_GROUP + g):hd * (hk * ATTN_GROUP + g + 1)].astype(dtype) for g in range(ATTN_GROUP)],
        axis=0)


def _swa_per_row(vals):
    seg = lax.broadcasted_iota(jnp.int32, (ATTN_GROUP * ATTN_BLOCK, 1), 0) // ATTN_BLOCK
    col = jnp.zeros((ATTN_GROUP * ATTN_BLOCK, 1), F32)
    for g, val in enumerate(vals):
        col = jnp.where(seg == g, val, col)
    return col


def _swa_specs():
    blk = ATTN_BLOCK
    prev = lambda n: jnp.maximum(n - 1, 0)
    return [
        pl.BlockSpec(memory_space=pltpu.SMEM),
        pl.BlockSpec((blk, 512), lambda n: (n, 0)),
        pl.BlockSpec((blk, 128), lambda n: (prev(n), 4)),
        pl.BlockSpec((blk, 128), lambda n: (n, 4)),
        pl.BlockSpec((blk, 128), lambda n: (prev(n), 5)),
        pl.BlockSpec((blk, 128), lambda n: (n, 5)),
    ]


def _swa_fwd(z, sinks, name, exchange=None):
    T = z.shape[0]
    blk, hd = ATTN_BLOCK, ATTN_HEAD_DIM
    scale = hd ** -0.5

    def body(sink_ref, q_ref, kp_ref, kc_ref, vp_ref, vc_ref, o_ref, lse_ref):
        allowed = _swa_mask(pl.program_id(0))
        hks = range(ATTN_KV_HEADS)
        kss = [slice(hd * hk, hd * hk + hd) for hk in hks]
        k = [_bf(jnp.concatenate([kp_ref[:, ks], kc_ref[:, ks]], axis=0)) for ks in kss]
        v = [_bf(jnp.concatenate([vp_ref[:, ks], vc_ref[:, ks]], axis=0)) for ks in kss]
        s = [jnp.where(allowed, _dot(_swa_rows(q_ref, hk, BF16), k[hk], NT) * scale, -1e30) for hk in hks]
        sink = [_swa_per_row([sink_ref[0, hk * ATTN_GROUP + g] for g in range(ATTN_GROUP)]) for hk in hks]
        m = [jnp.maximum(jnp.max(s[hk], axis=-1, keepdims=True), sink[hk]) for hk in hks]
        p = [jnp.exp(s[hk] - m[hk]) for hk in hks]
        l = [jnp.sum(p[hk], axis=-1, keepdims=True) + jnp.exp(sink[hk] - m[hk]) for hk in hks]
        o = [_dot(_bf(p[hk] / l[hk]), v[hk]).astype(o_ref.dtype) for hk in hks]
        for hk in hks:
            lse = m[hk] + jnp.log(l[hk])
            for g in range(ATTN_GROUP):
                h = hk * ATTN_GROUP + g
                o_ref[:, hd * h:hd * (h + 1)] = o[hk][blk * g:blk * (g + 1)]
                lse_ref[:, h:h + 1] = lse[blk * g:blk * (g + 1)]

    return _hosted_call(
        body, name=name, grid=(T // blk,), in_specs=_swa_specs(),
        out_specs=[pl.BlockSpec((blk, 512), lambda n: (n, 0)), pl.BlockSpec((blk, ATTN_Q_HEADS), lambda n: (n, 0))],
        out_shape=[jax.ShapeDtypeStruct((T, 512), BF16), jax.ShapeDtypeStruct((T, ATTN_Q_HEADS), F32)],
        scratch=[], args=(sinks, z, z, z, z, z), semantics=("parallel",), exchange=exchange)


def _swa_bwd(z, sinks, dcat, lse, name):
    T = z.shape[0]
    blk, hd = ATTN_BLOCK, ATTN_HEAD_DIM
    scale = hd ** -0.5
    group = ATTN_Q_HEADS // ATTN_KV_HEADS

    def body(sink_ref, q_ref, kp_ref, kc_ref, vp_ref, vc_ref, do_ref, lse_ref,
             dq_ref, dka_ref, dkb_ref, dva_ref, dvb_ref, dsink_ref):
        @pl.when(pl.program_id(0) == 0)
        def _():
            dsink_ref[...] = jnp.zeros_like(dsink_ref)

        allowed = _swa_mask(pl.program_id(0))
        lane = lax.broadcasted_iota(jnp.int32, (1, ATTN_Q_HEADS), 1)
        dsink = jnp.zeros((1, ATTN_Q_HEADS), F32)
        hks = range(ATTN_KV_HEADS)
        kss = [slice(hd * hk, hd * hk + hd) for hk in hks]
        k = [_bf(jnp.concatenate([kp_ref[:, ks], kc_ref[:, ks]], axis=0)) for ks in kss]
        v = [_bf(jnp.concatenate([vp_ref[:, ks], vc_ref[:, ks]], axis=0)) for ks in kss]
        qs = [_swa_rows(q_ref, hk, BF16) for hk in hks]
        dos = [_swa_rows(do_ref, hk, BF16) for hk in hks]
        lse = [jnp.concatenate([lse_ref[:, hk * group + g:hk * group + g + 1] for g in range(group)], axis=0)
               for hk in hks]
        s = [_dot(qs[hk], k[hk], NT) * scale for hk in hks]
        dp = [_dot(dos[hk], v[hk], NT) for hk in hks]
        p = [jnp.where(allowed, jnp.exp(jnp.where(allowed, s[hk], -1e30) - lse[hk]), 0.0) for hk in hks]
        delta = [jnp.sum(p[hk] * dp[hk], axis=-1, keepdims=True) for hk in hks]
        ds = [_bf(p[hk] * (dp[hk] - delta[hk]) * scale) for hk in hks]
        dq = [_dot(ds[hk], k[hk]).astype(dq_ref.dtype) for hk in hks]
        dk = [_dot(ds[hk], qs[hk], TN) for hk in hks]
        dv = [_dot(_bf(p[hk]), dos[hk], TN) for hk in hks]
        for hk in hks:
            sink = _swa_per_row([sink_ref[0, hk * group + g] for g in range(group)])
            sink_part = jnp.exp(sink - lse[hk]) * delta[hk]
            for g in range(group):
                h = hk * group + g
                dq_ref[:, hd * h:hd * (h + 1)] = dq[hk][blk * g:blk * (g + 1)]
                dsink = dsink + jnp.where(lane == h, -jnp.sum(sink_part[blk * g:blk * (g + 1)]), 0.0)
            dkb_ref[:, kss[hk]] = dk[hk][:blk]
            dka_ref[:, kss[hk]] = dk[hk][blk:]
            dvb_ref[:, kss[hk]] = dv[hk][:blk]
            dva_ref[:, kss[hk]] = dv[hk][blk:]
        dsink_ref[...] += dsink

    kv_out = pl.BlockSpec((blk, 128), lambda n: (n, 0))
    return pl.pallas_call(
        body, name=name, grid=(T // blk,),
        in_specs=_swa_specs() + [pl.BlockSpec((blk, 512), lambda n: (n, 0)),
                                 pl.BlockSpec((blk, ATTN_Q_HEADS), lambda n: (n, 0))],
        out_specs=[pl.BlockSpec((blk, 512), lambda n: (n, 0)), kv_out, kv_out, kv_out, kv_out,
                   pl.BlockSpec((1, ATTN_Q_HEADS), lambda n: (0, 0))],
        out_shape=[jax.ShapeDtypeStruct((T, 512), BF16)] + [jax.ShapeDtypeStruct((T, 128), F32)] * 4
        + [jax.ShapeDtypeStruct((1, ATTN_Q_HEADS), F32)],
        compiler_params=_params("arbitrary"),
    )(sinks, z, z, z, z, z, dcat, lse)


def _assemble_dz(dq_a, dka, dkb, dva, dvb, dqr, dfr, dir_, dgr, name):
    T = dq_a.shape[0]
    blk = ATTN_BLOCK
    nb = T // blk

    def body(dq_ref, dka_ref, dkb_ref, dva_ref, dvb_ref, dqr_ref, dfr_ref, dir_ref, dgr_ref, o_ref):
        has_next = pl.program_id(0) < nb - 1
        o_ref[:, 0:512] = dq_ref[...]
        o_ref[:, 512:640] = (dka_ref[...] + jnp.where(has_next, dkb_ref[...], 0.0)).astype(o_ref.dtype)
        o_ref[:, 640:768] = (dva_ref[...] + jnp.where(has_next, dvb_ref[...], 0.0)).astype(o_ref.dtype)
        o_ref[:, 768:1280] = dqr_ref[...]
        o_ref[:, 1280:1792] = dfr_ref[...]
        o_ref[:, 1792:2304] = dir_ref[...]
        o_ref[:, 2304:2816] = dgr_ref[...]

    cur = lambda w: pl.BlockSpec((blk, w), lambda n: (n, 0))
    nxt = pl.BlockSpec((blk, 128), lambda n: (jnp.minimum(n + 1, nb - 1), 0))
    return pl.pallas_call(
        body, name=name, grid=(nb,),
        in_specs=[cur(512), cur(128), nxt, cur(128), nxt, cur(512), cur(512), cur(512), cur(512)],
        out_specs=pl.BlockSpec((blk, 2816), lambda n: (n, 0)),
        out_shape=jax.ShapeDtypeStruct((T, 2816), BF16), compiler_params=_params("parallel"),
    )(dq_a, dka, dkb, dva, dvb, dqr, dfr, dir_, dgr)


HGRN_ROWS = 512


def _hgrn_consts():
    c = HGRN_CHUNK
    r = lax.broadcasted_iota(jnp.int32, (c, c), 0)
    s = lax.broadcasted_iota(jnp.int32, (c, c), 1)
    rcol = lax.broadcasted_iota(jnp.int32, (c, 1), 0)
    same_block, upper = [], []
    for m in HGRN_LEVELS:
        same_block.append((r & ~(2 * m - 1)) == (s & ~(2 * m - 1)))
        upper.append((rcol & (2 * m - 1)) >= m)
    cum_mat = jnp.where(s <= r, 1.0, 0.0).astype(BF16)
    rev_mat = jnp.where(s >= r, 1.0, 0.0).astype(BF16)
    return cum_mat, rev_mat, r == s, same_block, upper, rcol & 3


def _hgrn_level_decay(g, b, m, pos4):
    c = HGRN_CHUNK
    if m == 1:
        return jnp.exp(jnp.where((pos4 & 1) == 1, g, 0.0))
    if m == 2:
        after, before = pltpu.roll(g, c - 1, 0), pltpu.roll(g, 1, 0)
        return jnp.exp(jnp.where(pos4 == 0, after, jnp.where(pos4 == 1, 0.0, jnp.where(pos4 == 2, g, g + before))))
    b3 = b.reshape(c // (2 * m), 2 * m, HGRN_DIM)
    bref = jnp.broadcast_to(b3[:, m - 1:m, :], b3.shape).reshape(c, HGRN_DIM)
    return jnp.exp(-jnp.abs(b - bref))


def _split3(x):
    hi = _bf(x)
    r1 = x - hi.astype(F32)
    mid = _bf(r1)
    lo = _bf(r1 - mid.astype(F32))
    return jnp.concatenate([hi, mid, lo], axis=1)


def _dot_hilo(a, b):
    r, c = a.shape[0], b.shape[1]
    a_hi, b_hi = _bf(a), _bf(b)
    a2 = jnp.concatenate([a_hi, _bf(a - a_hi.astype(F32))], axis=0)
    b2 = jnp.concatenate([b_hi, _bf(b - b_hi.astype(F32))], axis=1)
    y = _dot(a2, b2)
    return y[:r, :c] + y[:r, c:] + y[r:, :c]


def _fold3(y):
    w = y.shape[1] // 3
    return y[:, :w] + y[:, w:2 * w] + y[:, 2 * w:]


def _hgrn_gates(qr, fr, lb):
    sq = _sigmoid(qr)
    q = qr * sq * (HGRN_DIM ** -0.5)
    sf = _sigmoid(fr)
    f = lb + (1.0 - lb) * sf
    k = (1.0 - lb) * _sigmoid(-fr)
    return q, sq, sf, f, k, jnp.log(f)


def _hgrn_intra(q, k, g, b, consts):
    _, _, eye, same_block, upper, pos4 = consts
    heads = range(len(q))
    a = [jnp.where(eye, _dot(_bf(q[hh]), _bf(k[hh]), NT), 0.0) for hh in heads]
    saved = [[] for _ in heads]
    for i, m in enumerate(HGRN_LEVELS):
        up = upper[i]
        e = [_hgrn_level_decay(g[hh], b[hh], m, pos4) for hh in heads]
        qt = [jnp.where(up, q[hh] * e[hh], 0.0) for hh in heads]
        kt = [jnp.where(up, 0.0, k[hh] * e[hh]) for hh in heads]
        p = [_dot(_bf(qt[hh]), _bf(kt[hh]), NT) for hh in heads]
        for hh in heads:
            a[hh] = a[hh] + jnp.where(same_block[i], p[hh], 0.0)
            saved[hh].append((e[hh], qt[hh], kt[hh]))
    return a, saved


def _hgrn_specs(tb, nb, rev):
    tmap = (lambda t: nb - 1 - t) if rev else (lambda t: t)
    w = HGRN_PAIR * HGRN_DIM
    zcol = lambda base: pl.BlockSpec((tb, w), lambda h, t: (tmap(t), base // HGRN_PAIR + h))
    return zcol, [zcol(6), zcol(10), zcol(14), zcol(18),
                  pl.BlockSpec((1, w), lambda h, t: (0, h)),
                  pl.BlockSpec((1, HGRN_DIM), lambda h, t: (0, 0))]


def _hgrn_fwd(z, lb, onw, name, exchange=None):
    T = z.shape[0]
    tb = min(HGRN_ROWS, T)
    nb, c, nc = T // tb, HGRN_CHUNK, min(HGRN_ROWS, T) // HGRN_CHUNK

    def body(qr_ref, fr_ref, ir_ref, gr_ref, lb_ref, onw_ref, rec_ref, o_ref, st_ref, state):
        @pl.when(pl.program_id(1) == 0)
        def _():
            state[...] = jnp.zeros_like(state)

        consts = _hgrn_consts()
        lbv = lb_ref[...]
        onwv = onw_ref[...]

        def chunk(ci, carry):
            sl = pl.ds(pl.multiple_of(ci * c, c), c)
            heads = range(HGRN_PAIR)
            lss = [slice(HGRN_DIM * hh, HGRN_DIM * (hh + 1)) for hh in heads]
            gates = [_hgrn_gates(qr_ref[sl, ls], fr_ref[sl, ls], lbv[:, ls]) for ls in lss]
            q, k, g = [t[0] for t in gates], [t[4] for t in gates], [t[5] for t in gates]
            v = [_bf(ir_ref[sl, ls]) for ls in lss]
            b = [_fold3(_dot(consts[0], _split3(g[hh]))) for hh in heads]
            a, _ = _hgrn_intra(q, k, g, b, consts)
            st = [state[hh] for hh in heads]
            for hh in heads:
                st_ref[hh, ci] = st[hh]
            bl = [b[hh][c - 1:c, :] for hh in heads]
            o_state = [_dot(_bf(q[hh] * jnp.exp(b[hh])), _bf(st[hh]), NT) for hh in heads]
            kv = [_dot(v[hh], _bf(k[hh] * jnp.exp(bl[hh] - b[hh])), TN) for hh in heads]
            o = [_dot(_bf(a[hh]), v[hh]) + o_state[hh] for hh in heads]
            for hh in heads:
                state[hh] = st[hh] * jnp.exp(bl[hh]) + kv[hh]
                o_ref[sl, lss[hh]] = o[hh]
                oh, _ = _rms(o[hh])
                gr = gr_ref[sl, lss[hh]]
                rec_ref[sl, lss[hh]] = (oh * onwv * (gr * _sigmoid(gr))).astype(rec_ref.dtype)
            return carry

        lax.fori_loop(0, nc, chunk, 0)

    _, in_specs = _hgrn_specs(tb, nb, False)
    out_blk = pl.BlockSpec((tb, HGRN_PAIR * HGRN_DIM), lambda h, t: (t, h))
    return _hosted_call(
        body, name=name, grid=(HGRN_HEADS // HGRN_PAIR, nb), in_specs=in_specs,
        out_specs=[out_blk, out_blk, pl.BlockSpec((HGRN_PAIR, nc, HGRN_DIM, HGRN_DIM), lambda h, t: (h, t, 0, 0))],
        out_shape=[jax.ShapeDtypeStruct((T, 512), BF16), jax.ShapeDtypeStruct((T, 512), F32),
                   jax.ShapeDtypeStruct((HGRN_HEADS, T // c, HGRN_DIM, HGRN_DIM), F32)],
        scratch=[pltpu.VMEM((HGRN_PAIR, HGRN_DIM, HGRN_DIM), F32)], args=(z, z, z, z, lb, onw),
        semantics=("parallel", "arbitrary"), exchange=exchange)


def _hgrn_bwd(z, lb, onw, o, states, dcat, name, exchange=None):
    T = z.shape[0]
    tb = min(HGRN_ROWS, T)
    nb, c, nc = T // tb, HGRN_CHUNK, min(HGRN_ROWS, T) // HGRN_CHUNK

    def body(qr_ref, fr_ref, ir_ref, gr_ref, lb_ref, onw_ref, o_ref, st_ref, drec_ref,
             dqr_ref, dfr_ref, dir_ref, dgr_ref, dlb_ref, donw_ref, dstate):
        @pl.when(pl.program_id(1) == 0)
        def _():
            dstate[...] = jnp.zeros_like(dstate)
            dlb_ref[...] = jnp.zeros_like(dlb_ref)

        @pl.when((pl.program_id(0) == 0) & (pl.program_id(1) == 0))
        def _():
            donw_ref[...] = jnp.zeros_like(donw_ref)

        consts = _hgrn_consts()
        rev_mat, eye, same_block, upper = consts[1:5]
        lbv = lb_ref[...]
        onwv = onw_ref[...]
        last = lax.broadcasted_iota(jnp.int32, (c, 1), 0) == c - 1

        def chunk(i, carry):
            ci = nc - 1 - i
            sl = pl.ds(pl.multiple_of(ci * c, c), c)
            hs = range(HGRN_PAIR)
            lss = [slice(HGRN_DIM * hh, HGRN_DIM * (hh + 1)) for hh in hs]
            qr = [qr_ref[sl, ls] for ls in lss]
            gates = [_hgrn_gates(qr[hh], fr_ref[sl, lss[hh]], lbv[:, lss[hh]]) for hh in hs]
            q, sq, sf, f, k, g = ([t[j] for t in gates] for j in range(6))
            v = [_bf(ir_ref[sl, ls]) for ls in lss]
            b = [_fold3(_dot(consts[0], _split3(g[hh]))) for hh in hs]
            a, saved = _hgrn_intra(q, k, g, b, consts)
            st = [st_ref[hh, ci] for hh in hs]
            dst = [dstate[hh] for hh in hs]

            gr = [gr_ref[sl, ls] for ls in lss]
            sg = [_sigmoid(gr[hh]) for hh in hs]
            norm = [_rms(o_ref[sl, ls]) for ls in lss]
            oh, r = [t[0] for t in norm], [t[1] for t in norm]
            drec = [drec_ref[sl, ls].astype(F32) for ls in lss]
            don = [drec[hh] * (gr[hh] * sg[hh]) for hh in hs]
            do = [_bf(_rms_bwd(don[hh] * onwv, oh[hh], r[hh])) for hh in hs]
            donw = jnp.sum(don[0] * oh[0], axis=0, keepdims=True)
            for hh in hs:
                dgr_ref[sl, lss[hh]] = (drec[hh] * oh[hh] * onwv
                                        * (sg[hh] * (1.0 + gr[hh] * (1.0 - sg[hh])))).astype(dgr_ref.dtype)
                if hh:
                    donw = donw + jnp.sum(don[hh] * oh[hh], axis=0, keepdims=True)
            donw_ref[...] += donw

            eb = [jnp.exp(b[hh]) for hh in hs]
            bl = [b[hh][c - 1:c, :] for hh in hs]
            ebl = [jnp.exp(bl[hh]) for hh in hs]
            ekb = [jnp.exp(bl[hh] - b[hh]) for hh in hs]
            qe = [q[hh] * eb[hh] for hh in hs]
            ke = [k[hh] * ekb[hh] for hh in hs]
            da = [_dot(do[hh], v[hh], NT) for hh in hs]
            dat = [_dot(v[hh], do[hh], NT) for hh in hs]
            dqe = [_dot(do[hh], _bf(st[hh])) for hh in hs]
            dke = [_dot(v[hh], _bf(dst[hh])) for hh in hs]
            dv_a = [_dot(_bf(a[hh]), do[hh], TN) for hh in hs]
            dv_s = [_dot(_bf(ke[hh]), _bf(dst[hh]), NT) for hh in hs]
            dst_in = [_dot(do[hh], _bf(qe[hh]), TN) for hh in hs]
            dad = [jnp.sum(jnp.where(eye, da[hh], 0.0), axis=1, keepdims=True) for hh in hs]
            dq = [dqe[hh] * eb[hh] + dad[hh] * k[hh] for hh in hs]
            dk = [dke[hh] * ekb[hh] + dad[hh] * q[hh] for hh in hs]
            db_last = [jnp.sum(dke[hh] * ke[hh], axis=0, keepdims=True)
                       + jnp.sum(dst[hh] * st[hh], axis=0, keepdims=True) * ebl[hh] for hh in hs]
            for hh in hs:
                dstate[hh] = dst[hh] * ebl[hh] + dst_in[hh]
                dir_ref[sl, lss[hh]] = (dv_a[hh] + dv_s[hh]).astype(dir_ref.dtype)
            for lvl in range(len(HGRN_LEVELS)):
                xq = [_dot_hilo(jnp.where(same_block[lvl], da[hh], 0.0), saved[hh][lvl][2]) for hh in hs]
                xk = [_dot_hilo(jnp.where(same_block[lvl], dat[hh], 0.0), saved[hh][lvl][1]) for hh in hs]
                for hh in hs:
                    e = saved[hh][lvl][0]
                    dq[hh] = dq[hh] + jnp.where(upper[lvl], xq[hh] * e, 0.0)
                    dk[hh] = dk[hh] + jnp.where(upper[lvl], 0.0, xk[hh] * e)
            db = [q[hh] * dq[hh] - k[hh] * dk[hh] + jnp.where(last, db_last[hh], 0.0) for hh in hs]
            dg = [_fold3(_dot(rev_mat, _split3(db[hh]))) for hh in hs]

            for hh in hs:
                ls = lss[hh]
                dqr_ref[sl, ls] = (dq[hh] * (HGRN_DIM ** -0.5)
                                   * (sq[hh] * (1.0 + qr[hh] * (1.0 - sq[hh])))).astype(dqr_ref.dtype)
                dfk = dg[hh] / f[hh] - dk[hh]
                dfr_ref[sl, ls] = ((1.0 - lbv[:, ls]) * sf[hh] * (1.0 - sf[hh]) * dfk).astype(dfr_ref.dtype)
                dlb_ref[:, ls] += jnp.sum((1.0 - sf[hh]) * dfk, axis=0, keepdims=True)
            return carry

        lax.fori_loop(0, nc, chunk, 0)

    zcol, in_specs = _hgrn_specs(tb, nb, True)
    rblk = pl.BlockSpec((tb, HGRN_PAIR * HGRN_DIM), lambda h, t: (nb - 1 - t, h))
    in_specs = in_specs + [
        rblk,
        pl.BlockSpec((HGRN_PAIR, nc, HGRN_DIM, HGRN_DIM), lambda h, t: (h, nb - 1 - t, 0, 0)),
        pl.BlockSpec((tb, HGRN_PAIR * HGRN_DIM), lambda h, t: (nb - 1 - t, 4 // HGRN_PAIR + h)),
    ]
    return _hosted_call(
        body, name=name, grid=(HGRN_HEADS // HGRN_PAIR, nb), in_specs=in_specs,
        out_specs=[rblk, rblk, rblk, rblk, pl.BlockSpec((1, HGRN_PAIR * HGRN_DIM), lambda h, t: (0, h)),
                   pl.BlockSpec((1, HGRN_DIM), lambda h, t: (0, 0))],
        out_shape=[jax.ShapeDtypeStruct((T, 512), BF16)] * 4
        + [jax.ShapeDtypeStruct((1, 512), F32), jax.ShapeDtypeStruct((1, HGRN_DIM), F32)],
        scratch=[pltpu.VMEM((HGRN_PAIR, HGRN_DIM, HGRN_DIM), F32)], args=(z, z, z, z, lb, onw, o, states, dcat),
        semantics=("arbitrary", "arbitrary"), exchange=exchange)


def _lower_bound(logits, name):
    def body(l_ref, lb_ref):
        l0, l1 = l_ref[0:1, :], l_ref[1:2, :]
        m = jnp.maximum(l0, l1)
        e0, e1 = jnp.exp(l0 - m), jnp.exp(l1 - m)
        lb_ref[...] = e0 / (e0 + e1)

    return pl.pallas_call(
        body, name=name, out_shape=jax.ShapeDtypeStruct((1, logits.shape[1]), F32),
    )(logits)


def _lower_bound_bwd(lb, dlb, name):
    def body(lb_ref, dlb_ref, dl_ref):
        p = lb_ref[...]
        d0 = dlb_ref[...] * p * (1.0 - p)
        dl_ref[0:1, :] = d0
        dl_ref[1:2, :] = -d0

    return pl.pallas_call(
        body, name=name, out_shape=jax.ShapeDtypeStruct((2, lb.shape[1]), F32),
    )(lb, dlb)


CA_ROWS = 512


def _ca_fwd(q, k, v, name):
    T, W = q.shape
    M = k.shape[0]
    tq = min(CA_ROWS, T)
    scale = CA_HEAD_DIM ** -0.5

    def body(q_ref, k_ref, v_ref, o_ref):
        for h in range(CA_HEADS):
            hs = slice(CA_HEAD_DIM * h, CA_HEAD_DIM * (h + 1))
            s = _dot(q_ref[:, hs], k_ref[:, hs], NT) * scale
            p = jnp.exp(s - jnp.max(s, axis=-1, keepdims=True))
            p = p / jnp.sum(p, axis=-1, keepdims=True)
            o_ref[:, hs] = _dot(_bf(p), v_ref[:, hs]).astype(o_ref.dtype)

    full = pl.BlockSpec((M, W), lambda i: (0, 0))
    return pl.pallas_call(
        body, name=name, grid=(T // tq,), in_specs=[_row_spec(tq, W), full, full], out_specs=_row_spec(tq, W),
        out_shape=jax.ShapeDtypeStruct((T, W), BF16), compiler_params=_params("parallel"),
    )(q, k, v)


def _ca_bwd(q, k, v, do, name):
    T, W = q.shape
    M = k.shape[0]
    tq = min(CA_ROWS, T)
    scale = CA_HEAD_DIM ** -0.5

    def body(q_ref, k_ref, v_ref, do_ref, dq_ref, dk_ref, dv_ref):
        @pl.when(pl.program_id(0) == 0)
        def _():
            dk_ref[...] = jnp.zeros_like(dk_ref)
            dv_ref[...] = jnp.zeros_like(dv_ref)

        for h in range(CA_HEADS):
            hs = slice(CA_HEAD_DIM * h, CA_HEAD_DIM * (h + 1))
            qh, kh, vh, doh = q_ref[:, hs], k_ref[:, hs], v_ref[:, hs], do_ref[:, hs]
            s = _dot(qh, kh, NT) * scale
            p = jnp.exp(s - jnp.max(s, axis=-1, keepdims=True))
            p = p / jnp.sum(p, axis=-1, keepdims=True)
            dp = _dot(doh, vh, NT)
            ds = _bf(p * (dp - jnp.sum(p * dp, axis=-1, keepdims=True)) * scale)
            dq_ref[:, hs] = _dot(ds, kh).astype(dq_ref.dtype)
            dk_ref[:, hs] += _dot(ds, qh, TN)
            dv_ref[:, hs] += _dot(_bf(p), doh, TN)

    full = pl.BlockSpec((M, W), lambda i: (0, 0))
    return pl.pallas_call(
        body, name=name, grid=(T // tq,), in_specs=[_row_spec(tq, W), full, full, _row_spec(tq, W)],
        out_specs=[_row_spec(tq, W), full, full],
        out_shape=[jax.ShapeDtypeStruct((T, W), BF16), jax.ShapeDtypeStruct((M, W), F32), jax.ShapeDtypeStruct((M, W), F32)],
        compiler_params=_params("arbitrary"),
    )(q, k, v, do)


FFN_ROWS = 256
FFN_COLS = 1408
GELU_C0 = 0.7978845608028654
GELU_C1 = 0.044715


def _gelu(x):
    t = jnp.tanh(GELU_C0 * (x + GELU_C1 * x * x * x))
    return 0.5 * x * (1.0 + t), t


def _gelu_grad(x, t):
    return 0.5 * (1.0 + t) + 0.5 * x * (1.0 - t * t) * GELU_C0 * (1.0 + 3.0 * GELU_C1 * x * x)


def _shift_down(cur, halo, first, tb):
    row = lax.broadcasted_iota(jnp.int32, (tb, 1), 0)
    h6 = jnp.where(first, 0.0, halo[6:7])
    h7 = jnp.where(first, 0.0, halo[7:8])
    u1 = jnp.where(row == 0, h7, pltpu.roll(cur, 1, 0))
    u2 = jnp.where(row == 0, h6, jnp.where(row == 1, h7, pltpu.roll(cur, 2, 0)))
    return u1, u2


def _conv(u_ref, halo_ref, w_ref, b_ref, half, first, tb):
    cur = u_ref[half]
    u1, u2 = _shift_down(cur, halo_ref[half], first, tb)
    w = w_ref[...]
    return w[0:1] * u2 + w[1:2] * u1 + w[2:3] * cur + b_ref[...], cur, u1, u2


def _ffn_specs(tb, tc, rows_first):
    nj = D_FF // tc
    rc = (lambda a, b: (a, b)) if rows_first else (lambda a, b: (b, a))
    def at(f):
        return lambda a, b: f(*rc(a, b))
    blk = pl.BlockSpec((2, tb, tc), at(lambda t, j: (0, t, j)))
    halo = pl.BlockSpec((2, 8, tc), at(lambda t, j: (0, jnp.maximum(t * (tb // 8) - 1, 0), j)))
    wg = pl.BlockSpec((3, tc), at(lambda t, j: (0, j)))
    wv = pl.BlockSpec((3, tc), at(lambda t, j: (0, j + nj)))
    bg = pl.BlockSpec((1, tc), at(lambda t, j: (0, j)))
    bv = pl.BlockSpec((1, tc), at(lambda t, j: (0, j + nj)))
    flat = pl.BlockSpec((tb, tc), at(lambda t, j: (t, j)))
    return blk, halo, wg, wv, bg, bv, flat


def _glu_fwd(u, cw, cb, name):
    T = u.shape[1]
    tb, tc = min(FFN_ROWS, T), FFN_COLS

    def body(u_ref, halo_ref, wg_ref, wv_ref, bg_ref, bv_ref, a_ref):
        first = pl.program_id(0) == 0
        cg = _conv(u_ref, halo_ref, wg_ref, bg_ref, 0, first, tb)[0]
        cv = _conv(u_ref, halo_ref, wv_ref, bv_ref, 1, first, tb)[0]
        a_ref[...] = (_gelu(cg)[0] * cv).astype(a_ref.dtype)

    blk, halo, wg, wv, bg, bv, flat = _ffn_specs(tb, tc, True)
    return pl.pallas_call(
        body, name=name, grid=(T // tb, D_FF // tc), in_specs=[blk, halo, wg, wv, bg, bv], out_specs=flat,
        out_shape=jax.ShapeDtypeStruct((T, D_FF), BF16), compiler_params=_params("parallel", "parallel"),
    )(u, u, cw, cw, cb, cb)


def _glu_bwd(u, cw, cb, da, name, exchange=None):
    T = u.shape[1]
    tb, tc = min(FFN_ROWS, T), FFN_COLS

    def body(u_ref, halo_ref, wg_ref, wv_ref, bg_ref, bv_ref, da_ref, dc_ref, db_ref, dw_ref):
        first = pl.program_id(1) == 0

        @pl.when(first)
        def _():
            db_ref[...] = jnp.zeros_like(db_ref)
            dw_ref[...] = jnp.zeros_like(dw_ref)

        cg, ug, ug1, ug2 = _conv(u_ref, halo_ref, wg_ref, bg_ref, 0, first, tb)
        cv, uv, uv1, uv2 = _conv(u_ref, halo_ref, wv_ref, bv_ref, 1, first, tb)
        da = da_ref[...]
        gl, t = _gelu(cg)
        dcg = da * cv * _gelu_grad(cg, t)
        dcv = da * gl
        dc_ref[0] = dcg
        dc_ref[1] = dcv
        for half, dc, taps in ((0, dcg, (ug2, ug1, ug)), (1, dcv, (uv2, uv1, uv))):
            db_ref[half] += jnp.sum(dc, axis=0, keepdims=True)
            for tap in range(3):
                dw_ref[half, tap:tap + 1, :] += jnp.sum(dc * taps[tap], axis=0, keepdims=True)

    blk, halo, wg, wv, bg, bv, flat = _ffn_specs(tb, tc, False)
    return _hosted_call(
        body, name=name, grid=(D_FF // tc, T // tb), in_specs=[blk, halo, wg, wv, bg, bv, flat],
        out_specs=[blk, pl.BlockSpec((2, 1, tc), lambda j, t: (0, 0, j)), pl.BlockSpec((2, 3, tc), lambda j, t: (0, 0, j))],
        out_shape=[jax.ShapeDtypeStruct((2, T, D_FF), F32), jax.ShapeDtypeStruct((2, 1, D_FF), F32),
                   jax.ShapeDtypeStruct((2, 3, D_FF), F32)],
        scratch=[], args=(u, u, cw, cw, cb, cb, da), semantics=("parallel", "arbitrary"), exchange=exchange)


def _conv_bwd(dc, cw, name):
    T = dc.shape[1]
    tb, tc = min(FFN_ROWS, T), FFN_COLS
    nt, nj = T // tb, D_FF // tc

    def body(dc_ref, halo_ref, wg_ref, wv_ref, du_ref):
        last = pl.program_id(0) == nt - 1
        row = lax.broadcasted_iota(jnp.int32, (tb, 1), 0)
        for half, w_ref in ((0, wg_ref), (1, wv_ref)):
            cur = dc_ref[half]
            halo = halo_ref[half]
            h0 = jnp.where(last, 0.0, halo[0:1])
            h1 = jnp.where(last, 0.0, halo[1:2])
            d1 = jnp.where(row == tb - 1, h0, pltpu.roll(cur, tb - 1, 0))
            d2 = jnp.where(row == tb - 1, h1, jnp.where(row == tb - 2, h0, pltpu.roll(cur, tb - 2, 0)))
            w = w_ref[...]
            du_ref[half] = (w[2:3] * cur + w[1:2] * d1 + w[0:1] * d2).astype(du_ref.dtype)

    blk = pl.BlockSpec((2, tb, tc), lambda t, j: (0, t, j))
    halo = pl.BlockSpec((2, 8, tc), lambda t, j: (0, jnp.minimum((t + 1) * (tb // 8), T // 8 - 1), j))
    wg = pl.BlockSpec((3, tc), lambda t, j: (0, j))
    wv = pl.BlockSpec((3, tc), lambda t, j: (0, j + nj))
    return pl.pallas_call(
        body, name=name, grid=(nt, nj), in_specs=[blk, halo, wg, wv], out_specs=blk,
        out_shape=jax.ShapeDtypeStruct((2, T, D_FF), BF16), compiler_params=_params("parallel", "parallel"),
    )(dc, dc, cw, cw)


def _mesh_pos():
    return lax.axis_index("x"), lax.axis_index("y"), lax.axis_index("c")


def _peer(pos, k):
    return (pos[0] ^ ((k >> 2) & 1), pos[1] ^ ((k >> 1) & 1), pos[2] ^ (k & 1))


def _index(pos):
    return 4 * pos[0] + 2 * pos[1] + pos[2]


class _Exchange:
    def __init__(self, kind, buf):
        assert kind in ("gather", "scatter")
        self.kind, self.buf = kind, buf
        self.out_shape = jax.ShapeDtypeStruct(((N_DEV,) + buf.shape) if kind == "gather" else buf.shape, buf.dtype)
        self.spec = pl.BlockSpec(memory_space=pl.ANY)
        self.scratch = [pltpu.SemaphoreType.DMA((N_DEV - 1,)), pltpu.SemaphoreType.DMA((N_DEV - 1,)),
                        pltpu.SemaphoreType.DMA]

    def _src(self, x_ref, dest):
        return x_ref if self.kind == "gather" else x_ref.at[dest]

    def _copies(self, x_ref, out_ref, send_sems, recv_sems, local_sem):
        pos = _mesh_pos()
        me = _index(pos)
        local = pltpu.make_async_copy(self._src(x_ref, me), out_ref.at[me], local_sem)
        sends, recvs = [], []
        for k in range(1, N_DEV):
            peer = _peer(pos, k)
            sends.append(pltpu.make_async_remote_copy(
                src_ref=self._src(x_ref, _index(peer)), dst_ref=out_ref.at[me], send_sem=send_sems.at[k - 1],
                recv_sem=recv_sems.at[k - 1], device_id=peer, device_id_type=pl.DeviceIdType.MESH))
            recvs.append(pltpu.make_async_remote_copy(
                src_ref=self._src(x_ref, me), dst_ref=out_ref.at[_index(peer)], send_sem=send_sems.at[k - 1],
                recv_sem=recv_sems.at[k - 1], device_id=peer, device_id_type=pl.DeviceIdType.MESH))
        return local, sends, recvs

    def start(self, *refs):
        local, sends, _ = self._copies(*refs)
        local.start()
        for cp in sends:
            cp.start()

    def finish(self, *refs):
        local, sends, recvs = self._copies(*refs)
        for cp in recvs:
            cp.wait_recv()
        for cp in sends:
            cp.wait_send()
        local.wait()


def _hosted_call(body, *, name, grid, in_specs, out_specs, out_shape, scratch, args, semantics, exchange=None):
    if exchange is None:
        return pl.pallas_call(
            body, name=name, grid=grid, in_specs=in_specs, out_specs=out_specs, out_shape=out_shape,
            scratch_shapes=scratch, compiler_params=_params(*semantics))(*args)
    n_in, n_out, n_scr = len(in_specs), len(out_specs), len(scratch)

    def hosted(*refs):
        ins, x_ref = refs[:n_in], refs[n_in]
        outs, land_ref = refs[n_in + 1:n_in + 1 + n_out], refs[n_in + 1 + n_out]
        rest = refs[n_in + n_out + 2:]
        sems = rest[n_scr:]
        ids = [pl.program_id(a) for a in range(len(grid))]
        first, last = ids[0] == 0, ids[0] == grid[0] - 1
        for a in range(1, len(grid)):
            first, last = first & (ids[a] == 0), last & (ids[a] == grid[a] - 1)

        @pl.when(first)
        def _():
            exchange.start(x_ref, land_ref, *sems)

        body(*ins, *outs, *rest[:n_scr])

        @pl.when(last)
        def _():
            exchange.finish(x_ref, land_ref, *sems)

    return pl.pallas_call(
        hosted, name=name, grid=grid, in_specs=list(in_specs) + [exchange.spec],
        out_specs=list(out_specs) + [exchange.spec], out_shape=list(out_shape) + [exchange.out_shape],
        scratch_shapes=list(scratch) + exchange.scratch, compiler_params=_params(*(["arbitrary"] * len(grid))),
    )(*args, exchange.buf)


def _exchange_alone(exchange, name):
    def body(x_ref, out_ref, send_sems, recv_sems, local_sem):
        exchange.start(x_ref, out_ref, send_sems, recv_sems, local_sem)
        exchange.finish(x_ref, out_ref, send_sems, recv_sems, local_sem)

    return pl.pallas_call(
        body, name=name, out_shape=exchange.out_shape, in_specs=[exchange.spec], out_specs=exchange.spec,
        scratch_shapes=exchange.scratch)(exchange.buf)


def _adamw(w, g, m, v):
    m = ADAM_B1 * m + (1.0 - ADAM_B1) * g
    v = ADAM_B2 * v + (1.0 - ADAM_B2) * (g * g)
    m_hat = m / (1.0 - ADAM_B1 ** ADAM_STEP)
    v_hat = v / (1.0 - ADAM_B2 ** ADAM_STEP)
    delta = -ADAM_LR * (m_hat / (jnp.sqrt(v_hat) + ADAM_EPS) + ADAM_WD * w)
    return delta, m, v


def _sum_adamw(parts, w, m, v, name):
    R, C = w.shape
    tr = max(t for t in range(16, ROWS + 1, 16) if R % t == 0)

    def body(p_ref, w_ref, m_ref, v_ref, g_ref, d_ref, mo_ref, vo_ref):
        g = p_ref[0].astype(F32)
        for i in range(1, N_DEV):
            g = g + p_ref[i].astype(F32)
        g_ref[...] = g
        d_ref[...], mo_ref[...], vo_ref[...] = _adamw(w_ref[...], g, m_ref[...], v_ref[...])

    row = _row_spec(tr, C)
    return pl.pallas_call(
        body, name=name, grid=(R // tr,),
        in_specs=[pl.BlockSpec((N_DEV, tr, C), lambda i: (0, i, 0)), row, row, row], out_specs=[row] * 4,
        out_shape=[jax.ShapeDtypeStruct((R, C), F32)] * 4, compiler_params=_params("parallel"),
    )(parts, w, m, v)


def _sum_parts(parts, name):
    _, R, C = parts.shape

    def body(p_ref, g_ref):
        g = p_ref[0]
        for i in range(1, N_DEV):
            g = g + p_ref[i]
        g_ref[...] = g

    return pl.pallas_call(body, name=name, out_shape=jax.ShapeDtypeStruct((R, C), F32))(parts)


def _adamw_call(w, g, m, v, name):
    def body(w_ref, g_ref, m_ref, v_ref, d_ref, mo_ref, vo_ref):
        d_ref[...], mo_ref[...], vo_ref[...] = _adamw(w_ref[...], g_ref[...], m_ref[...], v_ref[...])

    return pl.pallas_call(body, name=name, out_shape=[jax.ShapeDtypeStruct(w.shape, F32)] * 3)(w, g, m, v)


BIG = ("w_in", "w_out", "ca_wq", "ca_wk", "ca_wv", "ca_wo", "ffn_w_up", "ffn_w_down")
BIG_FULL = {"w_in": (1024, 2816), "w_out": (1024, 1024), "ca_wq": (1024, 1024), "ca_wk": (1024, 1024),
            "ca_wv": (1024, 1024), "ca_wo": (1024, 1024), "ffn_w_up": (1024, 5632), "ffn_w_down": (2816, 1024)}
G_IN, G_MID, G_UP, G_DOWN = ("w_in",), ("w_out", "ca_wq", "ca_wk", "ca_wv", "ca_wo"), ("ffn_w_up",), ("ffn_w_down",)
GROUPS = (G_IN, G_MID, G_UP, G_DOWN)
COL_SHARDED = ("w_in", "ffn_w_up")
PACK_COLS = 1024
NORMS = ("mix_pre_norm", "mix_post_norm", "ca_pre_norm", "mem_norm", "ca_post_norm", "ffn_pre_norm", "ffn_post_norm")
SMALL_ROWS = 32


def _big_rows(name):
    r, c = BIG_FULL[name]
    return r * c // N_DEV // PACK_COLS


def _pack_shards(shards, names):
    return jnp.concatenate([shards[n].reshape(_big_rows(n), PACK_COLS) for n in names], axis=0)


def _unpack_shards(pack, shapes, names):
    out, r0 = {}, 0
    for n in names:
        out[n] = pack[r0:r0 + _big_rows(n)].reshape(shapes[n])
        r0 += _big_rows(n)
    return out


def _unpack_gathered(gathered, names):
    out, r0 = {}, 0
    for n in names:
        rows = _big_rows(n)
        blk = gathered[:, r0:r0 + rows]
        r, c = BIG_FULL[n]
        if n in COL_SHARDED:
            out[n] = blk.reshape(N_DEV, r, c // N_DEV).transpose(1, 0, 2).reshape(r, c)
        else:
            out[n] = blk.reshape(r, c)
        r0 += rows
    return out


def _pack_full_grads(grads, names):
    parts = []
    for n in names:
        r, c = BIG_FULL[n]
        g = grads[n]
        if n in COL_SHARDED:
            g = g.reshape(r, N_DEV, c // N_DEV).transpose(1, 0, 2)
        parts.append(g.reshape(N_DEV, _big_rows(n), PACK_COLS))
    return jnp.concatenate(parts, axis=1).astype(BF16)


def _pad_row(vec):
    vec = vec.reshape(-1)
    n = -(-vec.shape[0] // PACK_COLS) * PACK_COLS
    return jnp.pad(vec, (0, n - vec.shape[0])).reshape(-1, PACK_COLS)


def _pack_small(norms, logits, out_norm, sinks, loss, conv_b, conv_w):
    rows = [_pad_row(norms[n]) for n in NORMS]
    rows.append(_pad_row(logits))
    rows.append(_pad_row(jnp.concatenate([out_norm.reshape(-1), sinks.reshape(-1), loss.reshape(-1)])))
    rows.append(_pad_row(conv_b))
    rows.append(_pad_row(conv_w))
    pack = jnp.concatenate(rows, axis=0)
    return jnp.pad(pack, ((0, SMALL_ROWS - pack.shape[0]), (0, 0)))


def _unpack_small(pack):
    norms = {n: pack[i:i + 1] for i, n in enumerate(NORMS)}
    logits = pack[7].reshape(2, 512)
    out_norm = pack[8:9, 0:128]
    sinks = pack[8:9, 128:136]
    loss = pack[8, 136]
    conv_b = pack[9:15].reshape(-1)[:2 * D_FF].reshape(1, 2 * D_FF)
    conv_w = pack[15:32].reshape(-1)[:6 * D_FF].reshape(3, 2 * D_FF)
    return norms, logits, out_norm, sinks, loss, conv_b, conv_w


def kernel(x, mem, mix_pre_norm, w_in, attn_sinks, hgrn_lb_logits, hgrn_out_norm, w_out, mix_post_norm, ca_pre_norm, mem_norm, ca_wq, ca_wk, ca_wv, ca_wo, ca_post_norm, ffn_pre_norm, ffn_w_up, ffn_conv_w, ffn_conv_b, ffn_w_down, ffn_post_norm, loss_target, m_mix_pre_norm, m_w_in, m_attn_sinks, m_hgrn_lb_logits, m_hgrn_out_norm, m_w_out, m_mix_post_norm, m_ca_pre_norm, m_mem_norm, m_ca_wq, m_ca_wk, m_ca_wv, m_ca_wo, m_ca_post_norm, m_ffn_pre_norm, m_ffn_w_up, m_ffn_conv_w, m_ffn_conv_b, m_ffn_w_down, m_ffn_post_norm, v_mix_pre_norm, v_w_in, v_attn_sinks, v_hgrn_lb_logits, v_hgrn_out_norm, v_w_out, v_mix_post_norm, v_ca_pre_norm, v_mem_norm, v_ca_wq, v_ca_wk, v_ca_wv, v_ca_wo, v_ca_post_norm, v_ffn_pre_norm, v_ffn_w_up, v_ffn_conv_w, v_ffn_conv_b, v_ffn_w_down, v_ffn_post_norm):
    names = ["mix_pre_norm", "w_in", "attn_sinks", "hgrn_lb_logits", "hgrn_out_norm", "w_out", "mix_post_norm",
             "ca_pre_norm", "mem_norm", "ca_wq", "ca_wk", "ca_wv", "ca_wo", "ca_post_norm", "ffn_pre_norm",
             "ffn_w_up", "ffn_conv_w", "ffn_conv_b", "ffn_w_down", "ffn_post_norm"]
    w_all = dict(zip(names, [mix_pre_norm, w_in, attn_sinks, hgrn_lb_logits, hgrn_out_norm, w_out, mix_post_norm,
                             ca_pre_norm, mem_norm, ca_wq, ca_wk, ca_wv, ca_wo, ca_post_norm, ffn_pre_norm,
                             ffn_w_up, ffn_conv_w, ffn_conv_b, ffn_w_down, ffn_post_norm]))
    m_all = dict(zip(names, [m_mix_pre_norm, m_w_in, m_attn_sinks, m_hgrn_lb_logits, m_hgrn_out_norm, m_w_out,
                             m_mix_post_norm, m_ca_pre_norm, m_mem_norm, m_ca_wq, m_ca_wk, m_ca_wv, m_ca_wo,
                             m_ca_post_norm, m_ffn_pre_norm, m_ffn_w_up, m_ffn_conv_w, m_ffn_conv_b, m_ffn_w_down,
                             m_ffn_post_norm]))
    v_all = dict(zip(names, [v_mix_pre_norm, v_w_in, v_attn_sinks, v_hgrn_lb_logits, v_hgrn_out_norm, v_w_out,
                             v_mix_post_norm, v_ca_pre_norm, v_mem_norm, v_ca_wq, v_ca_wk, v_ca_wv, v_ca_wo,
                             v_ca_post_norm, v_ffn_pre_norm, v_ffn_w_up, v_ffn_conv_w, v_ffn_conv_b, v_ffn_w_down,
                             v_ffn_post_norm]))
    dev = _index(_mesh_pos())

    shards = {n: w_all[n][0] for n in BIG}
    w_packs = {grp: _pack_shards(shards, grp) for grp in GROUPS}
    conv_w_rows = _exchange_alone(_Exchange("gather", _pad_row(ffn_conv_w[0])), "gather_conv_w")
    conv_w_full = conv_w_rows.reshape(N_DEV, -1)[:, :3 * 704].reshape(N_DEV, 3, 704).transpose(1, 0, 2).reshape(3, 2 * D_FF)

    received, grads_small, loss_local, grad_x = _local_step(
        x[0], mem[0], loss_target[0], {grp: w_packs[grp].astype(BF16) for grp in w_packs}, conv_w_full,
        {n: w_all[n] for n in NORMS}, attn_sinks, hgrn_lb_logits, hgrn_out_norm, ffn_conv_b)

    small_pack = _pack_small(grads_small["norms"], grads_small["logits"], grads_small["out_norm"], grads_small["sinks"],
                             loss_local, grads_small["conv_b"], grads_small["conv_w"])
    small_sum = _sum_parts(_exchange_alone(_Exchange("gather", small_pack), "gather_small"), "sum_small")
    g_norms, g_logits, g_out_norm, g_sinks, loss, g_conv_b, g_conv_w_full = _unpack_small(small_sum)
    g_conv_w = lax.dynamic_slice_in_dim(g_conv_w_full, dev * 704, 704, axis=1)

    shard_shapes = {n: w_all[n].shape for n in BIG}
    out_g, out_d, out_m, out_v = {}, {}, {}, {}
    for grp, tag in zip(GROUPS, ("in", "mid", "up", "down")):
        packs = _sum_adamw(
            received[grp], w_packs[grp], _pack_shards({n: m_all[n][0] for n in grp}, grp),
            _pack_shards({n: v_all[n][0] for n in grp}, grp), "adamw_" + tag)
        for tree, pack in zip((out_g, out_d, out_m, out_v), packs):
            tree.update(_unpack_shards(pack, shard_shapes, grp))

    small_g = dict(g_norms)
    small_g.update(attn_sinks=g_sinks, hgrn_lb_logits=g_logits, hgrn_out_norm=g_out_norm,
                   ffn_conv_b=g_conv_b, ffn_conv_w=g_conv_w[None])
    small_names = [n for n in names if n not in BIG]

    def small_pack_of(tree):
        return jnp.concatenate([_pad_row(tree[n]) for n in small_names], axis=0)

    ds, ms, vs = _adamw_call(small_pack_of(w_all), small_pack_of(small_g), small_pack_of(m_all), small_pack_of(v_all),
                             "adamw_small")

    def small_unpack(pack):
        out, r0 = {}, 0
        for n in small_names:
            size = 1
            for s in w_all[n].shape:
                size *= s
            rows = -(-size // PACK_COLS)
            out[n] = pack[r0:r0 + rows].reshape(-1)[:size].reshape(w_all[n].shape)
            r0 += rows
        return out

    sd, sm, sv = small_unpack(ds), small_unpack(ms), small_unpack(vs)
    for n in small_names:
        out_g[n] = small_g[n].reshape(w_all[n].shape)
        out_d[n], out_m[n], out_v[n] = sd[n], sm[n], sv[n]

    return (loss, grad_x[None], *[out_g[n] for n in names], *[out_d[n] for n in names],
            *[out_m[n] for n in names], *[out_v[n] for n in names])


def _local_step(x, mem, target, w_packs, conv_w, norms, sinks, lb_logits, out_norm, conv_b):
    g1, g2, g3 = norms["mix_pre_norm"], norms["mix_post_norm"], norms["ca_pre_norm"]
    g4, g5, g6, g7 = norms["mem_norm"], norms["ca_post_norm"], norms["ffn_pre_norm"], norms["ffn_post_norm"]

    h1, gathered = _norm_fwd(x, g1, "mix_norm", exchange=_Exchange("gather", w_packs[G_IN]))
    w_in = _unpack_gathered(gathered, G_IN)["w_in"]
    z = _mm(h1, w_in, mode="nn", out_dtype=F32, name="in_proj", tn=1408)
    attn, lse, gathered = _swa_fwd(z, sinks, "swa_fwd", exchange=_Exchange("gather", w_packs[G_DOWN]))
    w_down = _unpack_gathered(gathered, G_DOWN)["ffn_w_down"]
    lb = _lower_bound(lb_logits, "lower_bound")
    rec, o_rec, states, gathered = _hgrn_fwd(
        z, lb, out_norm, "hgrn_fwd",
        exchange=_Exchange("gather", jnp.concatenate([w_packs[G_MID], w_packs[G_UP]], axis=0)))
    wf = _unpack_gathered(gathered, G_MID + G_UP)
    w_out, wq, wk, wv, wo, w_up = (wf[n] for n in G_MID + G_UP)
    cat = jnp.concatenate([attn, rec], axis=1)
    mix = _mm(cat, w_out, mode="nn", out_dtype=F32, name="out_proj")
    x1, h2 = _post_pre(x, mix, g2, g3, "mix_post")
    mem_n = _norm_fwd(mem, g4, "mem_norm")
    q = _mm(h2, wq, mode="nn", out_dtype=BF16, name="ca_q")
    k = _mm(mem_n, wk, mode="nn", out_dtype=BF16, name="ca_k")
    v = _mm(mem_n, wv, mode="nn", out_dtype=BF16, name="ca_v")
    oc = _ca_fwd(q, k, v, "ca_fwd")
    c = _mm(oc, wo, mode="nn", out_dtype=F32, name="ca_o")
    x2, h3 = _post_pre(x1, c, g5, g6, "ca_post")
    u = _mm(h3, w_up, mode="nn", out_dtype=F32, name="ffn_up", tn=1408, split_out=True)
    a = _glu_fwd(u, conv_w, conv_b, "glu_fwd")
    y = _mm(a, w_down, mode="nn", out_dtype=F32, name="ffn_down", tk=2816)
    loss, dx3, dy, dg7 = _final(x2, y, g7, target, "loss_head")

    da = _mm(dy, w_down, mode="nt", out_dtype=F32, name="ffn_down_dx", tn=1408)
    d_w_down = _mm(a, dy, mode="tn", out_dtype=F32, name="ffn_down_dw", tm=1408, tk=512)
    dc, d_cb, d_cw, got_down = _glu_bwd(
        u, conv_w, conv_b, da, "glu_bwd",
        exchange=_Exchange("scatter", _pack_full_grads({"ffn_w_down": d_w_down}, G_DOWN)))
    du = _conv_bwd(dc, conv_w, "conv_bwd")
    d_w_up = _mm(h3, du, mode="tn", out_dtype=F32, name="ffn_up_dw", tm=1024, tn=1408, tk=512, split_b=True)
    dh3, got_up = _mm(du, w_up, mode="nt", out_dtype=F32, name="ffn_up_dx", tm=2048, tk=1408, split_a=True,
                      exchange=_Exchange("scatter", _pack_full_grads({"ffn_w_up": d_w_up}, G_UP)))
    dx2, dcv, dg6, dg5 = _norm_bwd2(dx3, dh3, x2, g6, c, g5, "ca_post_bwd")
    doc = _mm(dcv, wo, mode="nt", out_dtype=BF16, name="ca_o_dx")
    d_wo = _mm(oc, dcv, mode="tn", out_dtype=F32, name="ca_o_dw", tm=1024, tk=512)
    dq, dk, dv = _ca_bwd(q, k, v, doc, "ca_bwd")
    d_wq = _mm(h2, dq, mode="tn", out_dtype=F32, name="ca_q_dw", tm=1024, tk=512)
    dh2 = _mm(dq, wq, mode="nt", out_dtype=F32, name="ca_q_dx")
    d_wk = _mm(mem_n, dk, mode="tn", out_dtype=F32, name="ca_k_dw", tm=1024)
    d_wv = _mm(mem_n, dv, mode="tn", out_dtype=F32, name="ca_v_dw", tm=1024)
    dmem_k = _mm(dk, wk, mode="nt", out_dtype=F32, name="ca_k_dx")
    dmem_v = _mm(dv, wv, mode="nt", out_dtype=F32, name="ca_v_dx")
    dg4 = _gain_bwd(mem, dmem_k, dmem_v, "mem_norm_bwd")
    dx1, dmix, dg3, dg2 = _norm_bwd2(dx2, dh2, x1, g3, mix, g2, "mix_post_bwd")
    dcat = _mm(dmix, w_out, mode="nt", out_dtype=F32, name="out_proj_dx")
    d_w_out = _mm(cat, dmix, mode="tn", out_dtype=F32, name="out_proj_dw", tm=1024, tk=512)
    mid = {"w_out": d_w_out, "ca_wq": d_wq, "ca_wk": d_wk, "ca_wv": d_wv, "ca_wo": d_wo}
    dqr, dfr, dir_, dgr, dlb, donw, got_mid = _hgrn_bwd(
        z, lb, out_norm, o_rec, states, dcat, "hgrn_bwd", exchange=_Exchange("scatter", _pack_full_grads(mid, G_MID)))
    dq_a, dka, dkb, dva, dvb, dsinks = _swa_bwd(z, sinks, dcat, lse, "swa_bwd")
    dz = _assemble_dz(dq_a, dka, dkb, dva, dvb, dqr, dfr, dir_, dgr, "assemble_dz")
    d_w_in = _mm(h1, dz, mode="tn", out_dtype=F32, name="in_proj_dw", tm=1024, tn=1408, tk=512)
    dh1, got_in = _mm(dz, w_in, mode="nt", out_dtype=F32, name="in_proj_dx", tk=2816,
                      exchange=_Exchange("scatter", _pack_full_grads({"w_in": d_w_in}, G_IN)))
    dx, dg1 = _norm_bwd1(dx1, dh1, x, g1, "mix_norm_bwd")

    small = {
        "norms": {"mix_pre_norm": dg1, "mix_post_norm": dg2, "ca_pre_norm": dg3, "mem_norm": dg4,
                  "ca_post_norm": dg5, "ffn_pre_norm": dg6, "ffn_post_norm": dg7},
        "logits": _lower_bound_bwd(lb, dlb, "lower_bound_bwd"),
        "out_norm": donw,
        "sinks": dsinks,
        "conv_b": jnp.concatenate([d_cb[0], d_cb[1]], axis=1),
        "conv_w": jnp.concatenate([d_cw[0], d_cw[1]], axis=1),
    }
    return {G_IN: got_in, G_MID: got_mid, G_UP: got_up, G_DOWN: got_down}, small, loss[0, 0:1], dx
```

```python
import jax
import jax.numpy as jnp
from jax import lax
from jax.experimental import pallas as pl
from jax.experimental.pallas import tpu as pltpu

F32 = jnp.float32
BF16 = jnp.bfloat16
EPS = 1e-6
N_DEV = 8
MESH_AXES = ("x", "y", "c")

ATTN_HEAD_DIM = 64
ATTN_Q_HEADS = 8
ATTN_KV_HEADS = 2
ATTN_BLOCK = 128
HGRN_HEADS = 4
HGRN_DIM = 128
HGRN_CHUNK = 64
HGRN_PAIR = 2
HGRN_LEVELS = (32, 16, 8, 4, 2, 1)
CA_HEADS = 4
CA_HEAD_DIM = 256
D_FF = 2816

ADAM_LR = 0.001
ADAM_B1 = 0.9
ADAM_B2 = 0.999
ADAM_EPS = 1e-08
ADAM_WD = 0.01
ADAM_STEP = 10

VMEM_LIMIT = 56 << 20
LANE = 128

NT = (((1,), (1,)), ((), ()))
TN = (((0,), (0,)), ((), ()))


def _params(*sem):
    return pltpu.CompilerParams(dimension_semantics=sem, vmem_limit_bytes=VMEM_LIMIT)


def _tile(n, cap):
    if n <= cap:
        return n
    best = 0
    for t in range(LANE, cap + 1, LANE):
        if n % t == 0:
            best = t
    assert best, (n, cap)
    return best


def _dot(a, b, dims=None):
    if dims is None:
        return jnp.dot(a, b, preferred_element_type=F32)
    return lax.dot_general(a, b, dims, preferred_element_type=F32)


def _bf(x):
    return x.astype(BF16)


def _sigmoid(x):
    return 1.0 / (1.0 + jnp.exp(-x))


def _rms(x):
    r = lax.rsqrt(jnp.mean(x * x, axis=-1, keepdims=True) + EPS)
    return x * r, r


def _rms_bwd(dxh, xh, r):
    return r * (dxh - xh * jnp.mean(dxh * xh, axis=-1, keepdims=True))


def _mm(a, b, *, mode, out_dtype, name, tm=512, tn=1024, tk=1024, split_a=False, split_b=False, split_out=False,
        exchange=None):
    def dims(arr, split):
        if split:
            return arr.shape[1], 2 * arr.shape[2]
        return arr.shape

    ar, ac = dims(a, split_a)
    br, bc = dims(b, split_b)
    if mode == "nn":
        M, K, N = ar, ac, bc
        assert br == K
    elif mode == "nt":
        M, K, N = ar, ac, br
        assert bc == K
    else:
        K, M, N = ar, ac, bc
        assert br == K
    a_cols_half = ac // 2 if split_a else None
    b_cols_half = bc // 2 if split_b else None
    tm = _tile(M, tm)
    tn = _tile((N // 2) if (split_out or (split_b and mode != "nt")) else N, tn)
    tk = _tile((K // 2) if ((split_a and mode != "tn") or (split_b and mode == "nt")) else K, tk)
    if split_a and mode == "tn":
        tm = _tile(M // 2, tm)
    gm, gn, gk = M // tm, N // tn, K // tk
    a_bytes, b_bytes = a.size * a.dtype.itemsize, b.size * b.dtype.itemsize
    rows_outer = gk > 1 or a_bytes + gm * b_bytes <= gn * a_bytes + b_bytes
    grid = (gm, gn, gk) if rows_outer else (gn, gm, gk)

    def spec(split, half, blk, rc):
        def imap(p, q, k):
            r, c = rc(*((p, q) if rows_outer else (q, p)), k)
            if not split:
                return (r, c)
            per_half = half // blk[1]
            return (c // per_half, r, c % per_half)

        return pl.BlockSpec(((None,) + blk) if split else blk, imap)

    if mode == "nn":
        a_spec = spec(split_a, a_cols_half, (tm, tk), lambda i, j, k: (i, k))
        b_spec = spec(split_b, b_cols_half, (tk, tn), lambda i, j, k: (k, j))
        dn = None
    elif mode == "nt":
        a_spec = spec(split_a, a_cols_half, (tm, tk), lambda i, j, k: (i, k))
        b_spec = spec(split_b, b_cols_half, (tn, tk), lambda i, j, k: (j, k))
        dn = NT
    else:
        a_spec = spec(split_a, a_cols_half, (tk, tm), lambda i, j, k: (k, i))
        b_spec = spec(split_b, b_cols_half, (tk, tn), lambda i, j, k: (k, j))
        dn = TN
    o_spec = spec(split_out, N // 2 if split_out else None, (tm, tn), lambda i, j, k: (i, j))
    out_shape = (2, M, N // 2) if split_out else (M, N)

    if gk == 1:
        def body(a_ref, b_ref, o_ref):
            o_ref[...] = _dot(_bf(a_ref[...]), _bf(b_ref[...]), dn).astype(o_ref.dtype)
        scratch = []
    else:
        def body(a_ref, b_ref, o_ref, acc_ref):
            k = pl.program_id(2)

            @pl.when(k == 0)
            def _():
                acc_ref[...] = jnp.zeros_like(acc_ref)

            acc_ref[...] += _dot(_bf(a_ref[...]), _bf(b_ref[...]), dn)

            @pl.when(k == gk - 1)
            def _():
                o_ref[...] = acc_ref[...].astype(o_ref.dtype)
        scratch = [pltpu.VMEM((tm, tn), F32)]

    out = _hosted_call(
        body, name=name, grid=grid, in_specs=[a_spec, b_spec], out_specs=[o_spec],
        out_shape=[jax.ShapeDtypeStruct(out_shape, out_dtype)], scratch=scratch, args=(a, b),
        semantics=("parallel", "parallel", "arbitrary"), exchange=exchange)
    return out[0] if exchange is None else out


ROWS = 256


def _row_spec(tr, cols):
    return pl.BlockSpec((tr, cols), lambda i: (i, 0))


def _vec_spec(cols):
    return pl.BlockSpec((1, cols), lambda i: (0, 0))


def _norm_fwd(x, g, name, exchange=None):
    T, Dm = x.shape
    tr = min(ROWS, T)

    def body(x_ref, g_ref, h_ref):
        xh, _ = _rms(x_ref[...])
        h_ref[...] = (xh * g_ref[...]).astype(h_ref.dtype)

    out = _hosted_call(
        body, name=name, grid=(T // tr,), in_specs=[_row_spec(tr, Dm), _vec_spec(Dm)], out_specs=[_row_spec(tr, Dm)],
        out_shape=[jax.ShapeDtypeStruct((T, Dm), BF16)], scratch=[], args=(x, g), semantics=("parallel",),
        exchange=exchange)
    return out[0] if exchange is None else out


def _post_pre(x, m, g_post, g_pre, name):
    T, Dm = x.shape
    tr = min(ROWS, T)

    def body(x_ref, m_ref, gp_ref, gn_ref, xo_ref, h_ref):
        mh, _ = _rms(m_ref[...])
        xn = x_ref[...] + mh * gp_ref[...]
        xo_ref[...] = xn
        xh, _ = _rms(xn)
        h_ref[...] = (xh * gn_ref[...]).astype(h_ref.dtype)

    return pl.pallas_call(
        body, name=name, grid=(T // tr,),
        in_specs=[_row_spec(tr, Dm), _row_spec(tr, Dm), _vec_spec(Dm), _vec_spec(Dm)],
        out_specs=[_row_spec(tr, Dm), _row_spec(tr, Dm)],
        out_shape=[jax.ShapeDtypeStruct((T, Dm), F32), jax.ShapeDtypeStruct((T, Dm), BF16)],
        compiler_params=_params("parallel"),
    )(x, m, g_post, g_pre)


def _final(x2, y, g_post, target, name):
    T, Dm = x2.shape
    tr = min(ROWS, T)

    def body(x_ref, y_ref, g_ref, t_ref, loss_ref, dx_ref, dy_ref, dg_ref):
        @pl.when(pl.program_id(0) == 0)
        def _():
            loss_ref[...] = jnp.zeros_like(loss_ref)
            dg_ref[...] = jnp.zeros_like(dg_ref)

        g = g_ref[...]
        yh, r = _rms(y_ref[...])
        d = x_ref[...] + yh * g - t_ref[...]
        loss_ref[...] += jnp.zeros((1, LANE), F32) + 0.5 * jnp.sum(jnp.mean(d * d, axis=-1, keepdims=True))
        dx = d * (1.0 / Dm)
        dx_ref[...] = dx
        dy_ref[...] = _rms_bwd(dx * g, yh, r).astype(dy_ref.dtype)
        dg_ref[...] += jnp.sum(dx * yh, axis=0, keepdims=True)

    return pl.pallas_call(
        body, name=name, grid=(T // tr,),
        in_specs=[_row_spec(tr, Dm), _row_spec(tr, Dm), _vec_spec(Dm), _row_spec(tr, Dm)],
        out_specs=[_vec_spec(LANE), _row_spec(tr, Dm), _row_spec(tr, Dm), _vec_spec(Dm)],
        out_shape=[jax.ShapeDtypeStruct((1, LANE), F32), jax.ShapeDtypeStruct((T, Dm), F32),
                   jax.ShapeDtypeStruct((T, Dm), BF16), jax.ShapeDtypeStruct((1, Dm), F32)],
        compiler_params=_params("arbitrary"),
    )(x2, y, g_post, target)


def _norm_bwd2(dx_cur, dh, x_prev, g_pre, m_prev, g_post, name):
    T, Dm = x_prev.shape
    tr = min(ROWS, T)

    def body(dx_ref, dh_ref, x_ref, gn_ref, m_ref, gp_ref, dxo_ref, dm_ref, dgn_ref, dgp_ref):
        @pl.when(pl.program_id(0) == 0)
        def _():
            dgn_ref[...] = jnp.zeros_like(dgn_ref)
            dgp_ref[...] = jnp.zeros_like(dgp_ref)

        dh = dh_ref[...].astype(F32)
        xh, r = _rms(x_ref[...])
        dx = dx_ref[...] + _rms_bwd(dh * gn_ref[...], xh, r)
        dxo_ref[...] = dx
        dgn_ref[...] += jnp.sum(dh * xh, axis=0, keepdims=True)
        mh, rm = _rms(m_ref[...])
        dm_ref[...] = _rms_bwd(dx * gp_ref[...], mh, rm).astype(dm_ref.dtype)
        dgp_ref[...] += jnp.sum(dx * mh, axis=0, keepdims=True)

    return pl.pallas_call(
        body, name=name, grid=(T // tr,),
        in_specs=[_row_spec(tr, Dm), _row_spec(tr, Dm), _row_spec(tr, Dm), _vec_spec(Dm), _row_spec(tr, Dm), _vec_spec(Dm)],
        out_specs=[_row_spec(tr, Dm), _row_spec(tr, Dm), _vec_spec(Dm), _vec_spec(Dm)],
        out_shape=[jax.ShapeDtypeStruct((T, Dm), F32), jax.ShapeDtypeStruct((T, Dm), BF16),
                   jax.ShapeDtypeStruct((1, Dm), F32), jax.ShapeDtypeStruct((1, Dm), F32)],
        compiler_params=_params("arbitrary"),
    )(dx_cur, dh, x_prev, g_pre, m_prev, g_post)


def _norm_bwd1(dx_cur, dh, x_prev, g_pre, name):
    T, Dm = x_prev.shape
    tr = min(ROWS, T)

    def body(dx_ref, dh_ref, x_ref, gn_ref, dxo_ref, dgn_ref):
        @pl.when(pl.program_id(0) == 0)
        def _():
            dgn_ref[...] = jnp.zeros_like(dgn_ref)

        dh = dh_ref[...].astype(F32)
        xh, r = _rms(x_ref[...])
        dxo_ref[...] = dx_ref[...] + _rms_bwd(dh * gn_ref[...], xh, r)
        dgn_ref[...] += jnp.sum(dh * xh, axis=0, keepdims=True)

    return pl.pallas_call(
        body, name=name, grid=(T // tr,),
        in_specs=[_row_spec(tr, Dm), _row_spec(tr, Dm), _row_spec(tr, Dm), _vec_spec(Dm)],
        out_specs=[_row_spec(tr, Dm), _vec_spec(Dm)],
        out_shape=[jax.ShapeDtypeStruct((T, Dm), F32), jax.ShapeDtypeStruct((1, Dm), F32)],
        compiler_params=_params("arbitrary"),
    )(dx_cur, dh, x_prev, g_pre)


def _gain_bwd(x, dh_a, dh_b, name):
    T, Dm = x.shape

    def body(x_ref, a_ref, b_ref, dg_ref):
        xh, _ = _rms(x_ref[...])
        dg_ref[...] = jnp.sum((a_ref[...] + b_ref[...]) * xh, axis=0, keepdims=True)

    return pl.pallas_call(
        body, name=name, grid=(1,), in_specs=[_row_spec(T, Dm)] * 3, out_specs=_vec_spec(Dm),
        out_shape=jax.ShapeDtypeStruct((1, Dm), F32), compiler_params=_params("arbitrary"),
    )(x, dh_a, dh_b)


ATTN_GROUP = ATTN_Q_HEADS // ATTN_KV_HEADS


def _swa_mask(n):
    rows = ATTN_GROUP * ATTN_BLOCK
    row = lax.broadcasted_iota(jnp.int32, (rows, 2 * ATTN_BLOCK), 0) & (ATTN_BLOCK - 1)
    col = lax.broadcasted_iota(jnp.int32, (rows, 2 * ATTN_BLOCK), 1)
    diff = row + ATTN_BLOCK - col
    return (diff >= 0) & (diff < ATTN_BLOCK) & ((col >= ATTN_BLOCK) | (n > 0))


def _swa_rows(ref, hk, dtype):
    hd = ATTN_HEAD_DIM
    return jnp.concatenate(
        [ref[:, hd * (hk * ATTN_GROUP + g):hd * (hk * ATTN_GROUP + g + 1)].astype(dtype) for g in range(ATTN_GROUP)],
        axis=0)


def _swa_per_row(vals):
    seg = lax.broadcasted_iota(jnp.int32, (ATTN_GROUP * ATTN_BLOCK, 1), 0) // ATTN_BLOCK
    col = jnp.zeros((ATTN_GROUP * ATTN_BLOCK, 1), F32)
    for g, val in enumerate(vals):
        col = jnp.where(seg == g, val, col)
    return col


def _swa_specs():
    blk = ATTN_BLOCK
    prev = lambda n: jnp.maximum(n - 1, 0)
    return [
        pl.BlockSpec(memory_space=pltpu.SMEM),
        pl.BlockSpec((blk, 512), lambda n: (n, 0)),
        pl.BlockSpec((blk, 128), lambda n: (prev(n), 4)),
        pl.BlockSpec((blk, 128), lambda n: (n, 4)),
        pl.BlockSpec((blk, 128), lambda n: (prev(n), 5)),
        pl.BlockSpec((blk, 128), lambda n: (n, 5)),
    ]


def _swa_fwd(z, sinks, name, exchange=None):
    T = z.shape[0]
    blk, hd = ATTN_BLOCK, ATTN_HEAD_DIM
    scale = hd ** -0.5

    def body(sink_ref, q_ref, kp_ref, kc_ref, vp_ref, vc_ref, o_ref, lse_ref):
        allowed = _swa_mask(pl.program_id(0))
        hks = range(ATTN_KV_HEADS)
        kss = [slice(hd * hk, hd * hk + hd) for hk in hks]
        k = [_bf(jnp.concatenate([kp_ref[:, ks], kc_ref[:, ks]], axis=0)) for ks in kss]
        v = [_bf(jnp.concatenate([vp_ref[:, ks], vc_ref[:, ks]], axis=0)) for ks in kss]
        s = [jnp.where(allowed, _dot(_swa_rows(q_ref, hk, BF16), k[hk], NT) * scale, -1e30) for hk in hks]
        sink = [_swa_per_row([sink_ref[0, hk * ATTN_GROUP + g] for g in range(ATTN_GROUP)]) for hk in hks]
        m = [jnp.maximum(jnp.max(s[hk], axis=-1, keepdims=True), sink[hk]) for hk in hks]
        p = [jnp.exp(s[hk] - m[hk]) for hk in hks]
        l = [jnp.sum(p[hk], axis=-1, keepdims=True) + jnp.exp(sink[hk] - m[hk]) for hk in hks]
        o = [_dot(_bf(p[hk] / l[hk]), v[hk]).astype(o_ref.dtype) for hk in hks]
        for hk in hks:
            lse = m[hk] + jnp.log(l[hk])
            for g in range(ATTN_GROUP):
                h = hk * ATTN_GROUP + g
                o_ref[:, hd * h:hd * (h + 1)] = o[hk][blk * g:blk * (g + 1)]
                lse_ref[:, h:h + 1] = lse[blk * g:blk * (g + 1)]

    return _hosted_call(
        body, name=name, grid=(T // blk,), in_specs=_swa_specs(),
        out_specs=[pl.BlockSpec((blk, 512), lambda n: (n, 0)), pl.BlockSpec((blk, ATTN_Q_HEADS), lambda n: (n, 0))],
        out_shape=[jax.ShapeDtypeStruct((T, 512), BF16), jax.ShapeDtypeStruct((T, ATTN_Q_HEADS), F32)],
        scratch=[], args=(sinks, z, z, z, z, z), semantics=("parallel",), exchange=exchange)


def _swa_bwd(z, sinks, dcat, lse, name):
    T = z.shape[0]
    blk, hd = ATTN_BLOCK, ATTN_HEAD_DIM
    scale = hd ** -0.5
    group = ATTN_Q_HEADS // ATTN_KV_HEADS

    def body(sink_ref, q_ref, kp_ref, kc_ref, vp_ref, vc_ref, do_ref, lse_ref,
             dq_ref, dka_ref, dkb_ref, dva_ref, dvb_ref, dsink_ref):
        @pl.when(pl.program_id(0) == 0)
        def _():
            dsink_ref[...] = jnp.zeros_like(dsink_ref)

        allowed = _swa_mask(pl.program_id(0))
        lane = lax.broadcasted_iota(jnp.int32, (1, ATTN_Q_HEADS), 1)
        dsink = jnp.zeros((1, ATTN_Q_HEADS), F32)
        hks = range(ATTN_KV_HEADS)
        kss = [slice(hd * hk, hd * hk + hd) for hk in hks]
        k = [_bf(jnp.concatenate([kp_ref[:, ks], kc_ref[:, ks]], axis=0)) for ks in kss]
        v = [_bf(jnp.concatenate([vp_ref[:, ks], vc_ref[:, ks]], axis=0)) for ks in kss]
        qs = [_swa_rows(q_ref, hk, BF16) for hk in hks]
        dos = [_swa_rows(do_ref, hk, BF16) for hk in hks]
        lse = [jnp.concatenate([lse_ref[:, hk * group + g:hk * group + g + 1] for g in range(group)], axis=0)
               for hk in hks]
        s = [_dot(qs[hk], k[hk], NT) * scale for hk in hks]
        dp = [_dot(dos[hk], v[hk], NT) for hk in hks]
        p = [jnp.where(allowed, jnp.exp(jnp.where(allowed, s[hk], -1e30) - lse[hk]), 0.0) for hk in hks]
        delta = [jnp.sum(p[hk] * dp[hk], axis=-1, keepdims=True) for hk in hks]
        ds = [_bf(p[hk] * (dp[hk] - delta[hk]) * scale) for hk in hks]
        dq = [_dot(ds[hk], k[hk]).astype(dq_ref.dtype) for hk in hks]
        dk = [_dot(ds[hk], qs[hk], TN) for hk in hks]
        dv = [_dot(_bf(p[hk]), dos[hk], TN) for hk in hks]
        for hk in hks:
            sink = _swa_per_row([sink_ref[0, hk * group + g] for g in range(group)])
            sink_part = jnp.exp(sink - lse[hk]) * delta[hk]
            for g in range(group):
                h = hk * group + g
                dq_ref[:, hd * h:hd * (h + 1)] = dq[hk][blk * g:blk * (g + 1)]
                dsink = dsink + jnp.where(lane == h, -jnp.sum(sink_part[blk * g:blk * (g + 1)]), 0.0)
            dkb_ref[:, kss[hk]] = dk[hk][:blk]
            dka_ref[:, kss[hk]] = dk[hk][blk:]
            dvb_ref[:, kss[hk]] = dv[hk][:blk]
            dva_ref[:, kss[hk]] = dv[hk][blk:]
        dsink_ref[...] += dsink

    kv_out = pl.BlockSpec((blk, 128), lambda n: (n, 0))
    return pl.pallas_call(
        body, name=name, grid=(T // blk,),
        in_specs=_swa_specs() + [pl.BlockSpec((blk, 512), lambda n: (n, 0)),
                                 pl.BlockSpec((blk, ATTN_Q_HEADS), lambda n: (n, 0))],
        out_specs=[pl.BlockSpec((blk, 512), lambda n: (n, 0)), kv_out, kv_out, kv_out, kv_out,
                   pl.BlockSpec((1, ATTN_Q_HEADS), lambda n: (0, 0))],
        out_shape=[jax.ShapeDtypeStruct((T, 512), BF16)] + [jax.ShapeDtypeStruct((T, 128), F32)] * 4
        + [jax.ShapeDtypeStruct((1, ATTN_Q_HEADS), F32)],
        compiler_params=_params("arbitrary"),
    )(sinks, z, z, z, z, z, dcat, lse)


def _assemble_dz(dq_a, dka, dkb, dva, dvb, dqr, dfr, dir_, dgr, name):
    T = dq_a.shape[0]
    blk = ATTN_BLOCK
    nb = T // blk

    def body(dq_ref, dka_ref, dkb_ref, dva_ref, dvb_ref, dqr_ref, dfr_ref, dir_ref, dgr_ref, o_ref):
        has_next = pl.program_id(0) < nb - 1
        o_ref[:, 0:512] = dq_ref[...]
        o_ref[:, 512:640] = (dka_ref[...] + jnp.where(has_next, dkb_ref[...], 0.0)).astype(o_ref.dtype)
        o_ref[:, 640:768] = (dva_ref[...] + jnp.where(has_next, dvb_ref[...], 0.0)).astype(o_ref.dtype)
        o_ref[:, 768:1280] = dqr_ref[...]
        o_ref[:, 1280:1792] = dfr_ref[...]
        o_ref[:, 1792:2304] = dir_ref[...]
        o_ref[:, 2304:2816] = dgr_ref[...]

    cur = lambda w: pl.BlockSpec((blk, w), lambda n: (n, 0))
    nxt = pl.BlockSpec((blk, 128), lambda n: (jnp.minimum(n + 1, nb - 1), 0))
    return pl.pallas_call(
        body, name=name, grid=(nb,),
        in_specs=[cur(512), cur(128), nxt, cur(128), nxt, cur(512), cur(512), cur(512), cur(512)],
        out_specs=pl.BlockSpec((blk, 2816), lambda n: (n, 0)),
        out_shape=jax.ShapeDtypeStruct((T, 2816), BF16), compiler_params=_params("parallel"),
    )(dq_a, dka, dkb, dva, dvb, dqr, dfr, dir_, dgr)


HGRN_ROWS = 512


def _hgrn_consts():
    c = HGRN_CHUNK
    r = lax.broadcasted_iota(jnp.int32, (c, c), 0)
    s = lax.broadcasted_iota(jnp.int32, (c, c), 1)
    rcol = lax.broadcasted_iota(jnp.int32, (c, 1), 0)
    same_block, upper = [], []
    for m in HGRN_LEVELS:
        same_block.append((r & ~(2 * m - 1)) == (s & ~(2 * m - 1)))
        upper.append((rcol & (2 * m - 1)) >= m)
    cum_mat = jnp.where(s <= r, 1.0, 0.0).astype(BF16)
    rev_mat = jnp.where(s >= r, 1.0, 0.0).astype(BF16)
    return cum_mat, rev_mat, r == s, same_block, upper, rcol & 3


def _hgrn_level_decay(g, b, m, pos4):
    c = HGRN_CHUNK
    if m == 1:
        return jnp.exp(jnp.where((pos4 & 1) == 1, g, 0.0))
    if m == 2:
        after, before = pltpu.roll(g, c - 1, 0), pltpu.roll(g, 1, 0)
        return jnp.exp(jnp.where(pos4 == 0, after, jnp.where(pos4 == 1, 0.0, jnp.where(pos4 == 2, g, g + before))))
    b3 = b.reshape(c // (2 * m), 2 * m, HGRN_DIM)
    bref = jnp.broadcast_to(b3[:, m - 1:m, :], b3.shape).reshape(c, HGRN_DIM)
    return jnp.exp(-jnp.abs(b - bref))


def _split3(x):
    hi = _bf(x)
    r1 = x - hi.astype(F32)
    mid = _bf(r1)
    lo = _bf(r1 - mid.astype(F32))
    return jnp.concatenate([hi, mid, lo], axis=1)


def _dot_hilo(a, b):
    r, c = a.shape[0], b.shape[1]
    a_hi, b_hi = _bf(a), _bf(b)
    a2 = jnp.concatenate([a_hi, _bf(a - a_hi.astype(F32))], axis=0)
    b2 = jnp.concatenate([b_hi, _bf(b - b_hi.astype(F32))], axis=1)
    y = _dot(a2, b2)
    return y[:r, :c] + y[:r, c:] + y[r:, :c]


def _fold3(y):
    w = y.shape[1] // 3
    return y[:, :w] + y[:, w:2 * w] + y[:, 2 * w:]


def _hgrn_gates(qr, fr, lb):
    sq = _sigmoid(qr)
    q = qr * sq * (HGRN_DIM ** -0.5)
    sf = _sigmoid(fr)
    f = lb + (1.0 - lb) * sf
    k = (1.0 - lb) * _sigmoid(-fr)
    return q, sq, sf, f, k, jnp.log(f)


def _hgrn_intra(q, k, g, b, consts):
    _, _, eye, same_block, upper, pos4 = consts
    heads = range(len(q))
    a = [jnp.where(eye, _dot(_bf(q[hh]), _bf(k[hh]), NT), 0.0) for hh in heads]
    saved = [[] for _ in heads]
    for i, m in enumerate(HGRN_LEVELS):
        up = upper[i]
        e = [_hgrn_level_decay(g[hh], b[hh], m, pos4) for hh in heads]
        qt = [jnp.where(up, q[hh] * e[hh], 0.0) for hh in heads]
        kt = [jnp.where(up, 0.0, k[hh] * e[hh]) for hh in heads]
        p = [_dot(_bf(qt[hh]), _bf(kt[hh]), NT) for hh in heads]
        for hh in heads:
            a[hh] = a[hh] + jnp.where(same_block[i], p[hh], 0.0)
            saved[hh].append((e[hh], qt[hh], kt[hh]))
    return a, saved


def _hgrn_specs(tb, nb, rev):
    tmap = (lambda t: nb - 1 - t) if rev else (lambda t: t)
    w = HGRN_PAIR * HGRN_DIM
    zcol = lambda base: pl.BlockSpec((tb, w), lambda h, t: (tmap(t), base // HGRN_PAIR + h))
    return zcol, [zcol(6), zcol(10), zcol(14), zcol(18),
                  pl.BlockSpec((1, w), lambda h, t: (0, h)),
                  pl.BlockSpec((1, HGRN_DIM), lambda h, t: (0, 0))]


def _hgrn_fwd(z, lb, onw, name, exchange=None):
    T = z.shape[0]
    tb = min(HGRN_ROWS, T)
    nb, c, nc = T // tb, HGRN_CHUNK, min(HGRN_ROWS, T) // HGRN_CHUNK

    def body(qr_ref, fr_ref, ir_ref, gr_ref, lb_ref, onw_ref, rec_ref, o_ref, st_ref, state):
        @pl.when(pl.program_id(1) == 0)
        def _():
            state[...] = jnp.zeros_like(state)

        consts = _hgrn_consts()
        lbv = lb_ref[...]
        onwv = onw_ref[...]

        def chunk(ci, carry):
            sl = pl.ds(pl.multiple_of(ci * c, c), c)
            heads = range(HGRN_PAIR)
            lss = [slice(HGRN_DIM * hh, HGRN_DIM * (hh + 1)) for hh in heads]
            gates = [_hgrn_gates(qr_ref[sl, ls], fr_ref[sl, ls], lbv[:, ls]) for ls in lss]
            q, k, g = [t[0] for t in gates], [t[4] for t in gates], [t[5] for t in gates]
            v = [_bf(ir_ref[sl, ls]) for ls in lss]
            b = [_fold3(_dot(consts[0], _split3(g[hh]))) for hh in heads]
            a, _ = _hgrn_intra(q, k, g, b, consts)
            st = [state[hh] for hh in heads]
            for hh in heads:
                st_ref[hh, ci] = st[hh]
            bl = [b[hh][c - 1:c, :] for hh in heads]
            o_state = [_dot(_bf(q[hh] * jnp.exp(b[hh])), _bf(st[hh]), NT) for hh in heads]
            kv = [_dot(v[hh], _bf(k[hh] * jnp.exp(bl[hh] - b[hh])), TN) for hh in heads]
            o = [_dot(_bf(a[hh]), v[hh]) + o_state[hh] for hh in heads]
            for hh in heads:
                state[hh] = st[hh] * jnp.exp(bl[hh]) + kv[hh]
                o_ref[sl, lss[hh]] = o[hh]
                oh, _ = _rms(o[hh])
                gr = gr_ref[sl, lss[hh]]
                rec_ref[sl, lss[hh]] = (oh * onwv * (gr * _sigmoid(gr))).astype(rec_ref.dtype)
            return carry

        lax.fori_loop(0, nc, chunk, 0)

    _, in_specs = _hgrn_specs(tb, nb, False)
    out_blk = pl.BlockSpec((tb, HGRN_PAIR * HGRN_DIM), lambda h, t: (t, h))
    return _hosted_call(
        body, name=name, grid=(HGRN_HEADS // HGRN_PAIR, nb), in_specs=in_specs,
        out_specs=[out_blk, out_blk, pl.BlockSpec((HGRN_PAIR, nc, HGRN_DIM, HGRN_DIM), lambda h, t: (h, t, 0, 0))],
        out_shape=[jax.ShapeDtypeStruct((T, 512), BF16), jax.ShapeDtypeStruct((T, 512), F32),
                   jax.ShapeDtypeStruct((HGRN_HEADS, T // c, HGRN_DIM, HGRN_DIM), F32)],
        scratch=[pltpu.VMEM((HGRN_PAIR, HGRN_DIM, HGRN_DIM), F32)], args=(z, z, z, z, lb, onw),
        semantics=("parallel", "arbitrary"), exchange=exchange)


def _hgrn_bwd(z, lb, onw, o, states, dcat, name, exchange=None):
    T = z.shape[0]
    tb = min(HGRN_ROWS, T)
    nb, c, nc = T // tb, HGRN_CHUNK, min(HGRN_ROWS, T) // HGRN_CHUNK

    def body(qr_ref, fr_ref, ir_ref, gr_ref, lb_ref, onw_ref, o_ref, st_ref, drec_ref,
             dqr_ref, dfr_ref, dir_ref, dgr_ref, dlb_ref, donw_ref, dstate):
        @pl.when(pl.program_id(1) == 0)
        def _():
            dstate[...] = jnp.zeros_like(dstate)
            dlb_ref[...] = jnp.zeros_like(dlb_ref)

        @pl.when((pl.program_id(0) == 0) & (pl.program_id(1) == 0))
        def _():
            donw_ref[...] = jnp.zeros_like(donw_ref)

        consts = _hgrn_consts()
        rev_mat, eye, same_block, upper = consts[1:5]
        lbv = lb_ref[...]
        onwv = onw_ref[...]
        last = lax.broadcasted_iota(jnp.int32, (c, 1), 0) == c - 1

        def chunk(i, carry):
            ci = nc - 1 - i
            sl = pl.ds(pl.multiple_of(ci * c, c), c)
            hs = range(HGRN_PAIR)
            lss = [slice(HGRN_DIM * hh, HGRN_DIM * (hh + 1)) for hh in hs]
            qr = [qr_ref[sl, ls] for ls in lss]
            gates = [_hgrn_gates(qr[hh], fr_ref[sl, lss[hh]], lbv[:, lss[hh]]) for hh in hs]
            q, sq, sf, f, k, g = ([t[j] for t in gates] for j in range(6))
            v = [_bf(ir_ref[sl, ls]) for ls in lss]
            b = [_fold3(_dot(consts[0], _split3(g[hh]))) for hh in hs]
            a, saved = _hgrn_intra(q, k, g, b, consts)
            st = [st_ref[hh, ci] for hh in hs]
            dst = [dstate[hh] for hh in hs]

            gr = [gr_ref[sl, ls] for ls in lss]
            sg = [_sigmoid(gr[hh]) for hh in hs]
            norm = [_rms(o_ref[sl, ls]) for ls in lss]
            oh, r = [t[0] for t in norm], [t[1] for t in norm]
            drec = [drec_ref[sl, ls].astype(F32) for ls in lss]
            don = [drec[hh] * (gr[hh] * sg[hh]) for hh in hs]
            do = [_bf(_rms_bwd(don[hh] * onwv, oh[hh], r[hh])) for hh in hs]
            donw = jnp.sum(don[0] * oh[0], axis=0, keepdims=True)
            for hh in hs:
                dgr_ref[sl, lss[hh]] = (drec[hh] * oh[hh] * onwv
                                        * (sg[hh] * (1.0 + gr[hh] * (1.0 - sg[hh])))).astype(dgr_ref.dtype)
                if hh:
                    donw = donw + jnp.sum(don[hh] * oh[hh], axis=0, keepdims=True)
            donw_ref[...] += donw

            eb = [jnp.exp(b[hh]) for hh in hs]
            bl = [b[hh][c - 1:c, :] for hh in hs]
            ebl = [jnp.exp(bl[hh]) for hh in hs]
            ekb = [jnp.exp(bl[hh] - b[hh]) for hh in hs]
            qe = [q[hh] * eb[hh] for hh in hs]
            ke = [k[hh] * ekb[hh] for hh in hs]
            da = [_dot(do[hh], v[hh], NT) for hh in hs]
            dat = [_dot(v[hh], do[hh], NT) for hh in hs]
            dqe = [_dot(do[hh], _bf(st[hh])) for hh in hs]
            dke = [_dot(v[hh], _bf(dst[hh])) for hh in hs]
            dv_a = [_dot(_bf(a[hh]), do[hh], TN) for hh in hs]
            dv_s = [_dot(_bf(ke[hh]), _bf(dst[hh]), NT) for hh in hs]
            dst_in = [_dot(do[hh], _bf(qe[hh]), TN) for hh in hs]
            dad = [jnp.sum(jnp.where(eye, da[hh], 0.0), axis=1, keepdims=True) for hh in hs]
            dq = [dqe[hh] * eb[hh] + dad[hh] * k[hh] for hh in hs]
            dk = [dke[hh] * ekb[hh] + dad[hh] * q[hh] for hh in hs]
            db_last = [jnp.sum(dke[hh] * ke[hh], axis=0, keepdims=True)
                       + jnp.sum(dst[hh] * st[hh], axis=0, keepdims=True) * ebl[hh] for hh in hs]
            for hh in hs:
                dstate[hh] = dst[hh] * ebl[hh] + dst_in[hh]
                dir_ref[sl, lss[hh]] = (dv_a[hh] + dv_s[hh]).astype(dir_ref.dtype)
            for lvl in range(len(HGRN_LEVELS)):
                xq = [_dot_hilo(jnp.where(same_block[lvl], da[hh], 0.0), saved[hh][lvl][2]) for hh in hs]
                xk = [_dot_hilo(jnp.where(same_block[lvl], dat[hh], 0.0), saved[hh][lvl][1]) for hh in hs]
                for hh in hs:
                    e = saved[hh][lvl][0]
                    dq[hh] = dq[hh] + jnp.where(upper[lvl], xq[hh] * e, 0.0)
                    dk[hh] = dk[hh] + jnp.where(upper[lvl], 0.0, xk[hh] * e)
            db = [q[hh] * dq[hh] - k[hh] * dk[hh] + jnp.where(last, db_last[hh], 0.0) for hh in hs]
            dg = [_fold3(_dot(rev_mat, _split3(db[hh]))) for hh in hs]

            for hh in hs:
                ls = lss[hh]
                dqr_ref[sl, ls] = (dq[hh] * (HGRN_DIM ** -0.5)
                                   * (sq[hh] * (1.0 + qr[hh] * (1.0 - sq[hh])))).astype(dqr_ref.dtype)
                dfk = dg[hh] / f[hh] - dk[hh]
                dfr_ref[sl, ls] = ((1.0 - lbv[:, ls]) * sf[hh] * (1.0 - sf[hh]) * dfk).astype(dfr_ref.dtype)
                dlb_ref[:, ls] += jnp.sum((1.0 - sf[hh]) * dfk, axis=0, keepdims=True)
            return carry

        lax.fori_loop(0, nc, chunk, 0)

    zcol, in_specs = _hgrn_specs(tb, nb, True)
    rblk = pl.BlockSpec((tb, HGRN_PAIR * HGRN_DIM), lambda h, t: (nb - 1 - t, h))
    in_specs = in_specs + [
        rblk,
        pl.BlockSpec((HGRN_PAIR, nc, HGRN_DIM, HGRN_DIM), lambda h, t: (h, nb - 1 - t, 0, 0)),
        pl.BlockSpec((tb, HGRN_PAIR * HGRN_DIM), lambda h, t: (nb - 1 - t, 4 // HGRN_PAIR + h)),
    ]
    return _hosted_call(
        body, name=name, grid=(HGRN_HEADS // HGRN_PAIR, nb), in_specs=in_specs,
        out_specs=[rblk, rblk, rblk, rblk, pl.BlockSpec((1, HGRN_PAIR * HGRN_DIM), lambda h, t: (0, h)),
                   pl.BlockSpec((1, HGRN_DIM), lambda h, t: (0, 0))],
        out_shape=[jax.ShapeDtypeStruct((T, 512), BF16)] * 4
        + [jax.ShapeDtypeStruct((1, 512), F32), jax.ShapeDtypeStruct((1, HGRN_DIM), F32)],
        scratch=[pltpu.VMEM((HGRN_PAIR, HGRN_DIM, HGRN_DIM), F32)], args=(z, z, z, z, lb, onw, o, states, dcat),
        semantics=("arbitrary", "arbitrary"), exchange=exchange)


def _lower_bound(logits, name):
    def body(l_ref, lb_ref):
        l0, l1 = l_ref[0:1, :], l_ref[1:2, :]
        m = jnp.maximum(l0, l1)
        e0, e1 = jnp.exp(l0 - m), jnp.exp(l1 - m)
        lb_ref[...] = e0 / (e0 + e1)

    return pl.pallas_call(
        body, name=name, out_shape=jax.ShapeDtypeStruct((1, logits.shape[1]), F32),
    )(logits)


def _lower_bound_bwd(lb, dlb, name):
    def body(lb_ref, dlb_ref, dl_ref):
        p = lb_ref[...]
        d0 = dlb_ref[...] * p * (1.0 - p)
        dl_ref[0:1, :] = d0
        dl_ref[1:2, :] = -d0

    return pl.pallas_call(
        body, name=name, out_shape=jax.ShapeDtypeStruct((2, lb.shape[1]), F32),
    )(lb, dlb)


CA_ROWS = 512


def _ca_fwd(q, k, v, name):
    T, W = q.shape
    M = k.shape[0]
    tq = min(CA_ROWS, T)
    scale = CA_HEAD_DIM ** -0.5

    def body(q_ref, k_ref, v_ref, o_ref):
        for h in range(CA_HEADS):
            hs = slice(CA_HEAD_DIM * h, CA_HEAD_DIM * (h + 1))
            s = _dot(q_ref[:, hs], k_ref[:, hs], NT) * scale
            p = jnp.exp(s - jnp.max(s, axis=-1, keepdims=True))
            p = p / jnp.sum(p, axis=-1, keepdims=True)
            o_ref[:, hs] = _dot(_bf(p), v_ref[:, hs]).astype(o_ref.dtype)

    full = pl.BlockSpec((M, W), lambda i: (0, 0))
    return pl.pallas_call(
        body, name=name, grid=(T // tq,), in_specs=[_row_spec(tq, W), full, full], out_specs=_row_spec(tq, W),
        out_shape=jax.ShapeDtypeStruct((T, W), BF16), compiler_params=_params("parallel"),
    )(q, k, v)


def _ca_bwd(q, k, v, do, name):
    T, W = q.shape
    M = k.shape[0]
    tq = min(CA_ROWS, T)
    scale = CA_HEAD_DIM ** -0.5

    def body(q_ref, k_ref, v_ref, do_ref, dq_ref, dk_ref, dv_ref):
        @pl.when(pl.program_id(0) == 0)
        def _():
            dk_ref[...] = jnp.zeros_like(dk_ref)
            dv_ref[...] = jnp.zeros_like(dv_ref)

        for h in range(CA_HEADS):
            hs = slice(CA_HEAD_DIM * h, CA_HEAD_DIM * (h + 1))
            qh, kh, vh, doh = q_ref[:, hs], k_ref[:, hs], v_ref[:, hs], do_ref[:, hs]
            s = _dot(qh, kh, NT) * scale
            p = jnp.exp(s - jnp.max(s, axis=-1, keepdims=True))
            p = p / jnp.sum(p, axis=-1, keepdims=True)
            dp = _dot(doh, vh, NT)
            ds = _bf(p * (dp - jnp.sum(p * dp, axis=-1, keepdims=True)) * scale)
            dq_ref[:, hs] = _dot(ds, kh).astype(dq_ref.dtype)
            dk_ref[:, hs] += _dot(ds, qh, TN)
            dv_ref[:, hs] += _dot(_bf(p), doh, TN)

    full = pl.BlockSpec((M, W), lambda i: (0, 0))
    return pl.pallas_call(
        body, name=name, grid=(T // tq,), in_specs=[_row_spec(tq, W), full, full, _row_spec(tq, W)],
        out_specs=[_row_spec(tq, W), full, full],
        out_shape=[jax.ShapeDtypeStruct((T, W), BF16), jax.ShapeDtypeStruct((M, W), F32), jax.ShapeDtypeStruct((M, W), F32)],
        compiler_params=_params("arbitrary"),
    )(q, k, v, do)


FFN_ROWS = 256
FFN_COLS = 1408
GELU_C0 = 0.7978845608028654
GELU_C1 = 0.044715


def _gelu(x):
    t = jnp.tanh(GELU_C0 * (x + GELU_C1 * x * x * x))
    return 0.5 * x * (1.0 + t), t


def _gelu_grad(x, t):
    return 0.5 * (1.0 + t) + 0.5 * x * (1.0 - t * t) * GELU_C0 * (1.0 + 3.0 * GELU_C1 * x * x)


def _shift_down(cur, halo, first, tb):
    row = lax.broadcasted_iota(jnp.int32, (tb, 1), 0)
    h6 = jnp.where(first, 0.0, halo[6:7])
    h7 = jnp.where(first, 0.0, halo[7:8])
    u1 = jnp.where(row == 0, h7, pltpu.roll(cur, 1, 0))
    u2 = jnp.where(row == 0, h6, jnp.where(row == 1, h7, pltpu.roll(cur, 2, 0)))
    return u1, u2


def _conv(u_ref, halo_ref, w_ref, b_ref, half, first, tb):
    cur = u_ref[half]
    u1, u2 = _shift_down(cur, halo_ref[half], first, tb)
    w = w_ref[...]
    return w[0:1] * u2 + w[1:2] * u1 + w[2:3] * cur + b_ref[...], cur, u1, u2


def _ffn_specs(tb, tc, rows_first):
    nj = D_FF // tc
    rc = (lambda a, b: (a, b)) if rows_first else (lambda a, b: (b, a))
    def at(f):
        return lambda a, b: f(*rc(a, b))
    blk = pl.BlockSpec((2, tb, tc), at(lambda t, j: (0, t, j)))
    halo = pl.BlockSpec((2, 8, tc), at(lambda t, j: (0, jnp.maximum(t * (tb // 8) - 1, 0), j)))
    wg = pl.BlockSpec((3, tc), at(lambda t, j: (0, j)))
    wv = pl.BlockSpec((3, tc), at(lambda t, j: (0, j + nj)))
    bg = pl.BlockSpec((1, tc), at(lambda t, j: (0, j)))
    bv = pl.BlockSpec((1, tc), at(lambda t, j: (0, j + nj)))
    flat = pl.BlockSpec((tb, tc), at(lambda t, j: (t, j)))
    return blk, halo, wg, wv, bg, bv, flat


def _glu_fwd(u, cw, cb, name):
    T = u.shape[1]
    tb, tc = min(FFN_ROWS, T), FFN_COLS

    def body(u_ref, halo_ref, wg_ref, wv_ref, bg_ref, bv_ref, a_ref):
        first = pl.program_id(0) == 0
        cg = _conv(u_ref, halo_ref, wg_ref, bg_ref, 0, first, tb)[0]
        cv = _conv(u_ref, halo_ref, wv_ref, bv_ref, 1, first, tb)[0]
        a_ref[...] = (_gelu(cg)[0] * cv).astype(a_ref.dtype)

    blk, halo, wg, wv, bg, bv, flat = _ffn_specs(tb, tc, True)
    return pl.pallas_call(
        body, name=name, grid=(T // tb, D_FF // tc), in_specs=[blk, halo, wg, wv, bg, bv], out_specs=flat,
        out_shape=jax.ShapeDtypeStruct((T, D_FF), BF16), compiler_params=_params("parallel", "parallel"),
    )(u, u, cw, cw, cb, cb)


def _glu_bwd(u, cw, cb, da, name, exchange=None):
    T = u.shape[1]
    tb, tc = min(FFN_ROWS, T), FFN_COLS

    def body(u_ref, halo_ref, wg_ref, wv_ref, bg_ref, bv_ref, da_ref, dc_ref, db_ref, dw_ref):
        first = pl.program_id(1) == 0

        @pl.when(first)
        def _():
            db_ref[...] = jnp.zeros_like(db_ref)
            dw_ref[...] = jnp.zeros_like(dw_ref)

        cg, ug, ug1, ug2 = _conv(u_ref, halo_ref, wg_ref, bg_ref, 0, first, tb)
        cv, uv, uv1, uv2 = _conv(u_ref, halo_ref, wv_ref, bv_ref, 1, first, tb)
        da = da_ref[...]
        gl, t = _gelu(cg)
        dcg = da * cv * _gelu_grad(cg, t)
        dcv = da * gl
        dc_ref[0] = dcg
        dc_ref[1] = dcv
        for half, dc, taps in ((0, dcg, (ug2, ug1, ug)), (1, dcv, (uv2, uv1, uv))):
            db_ref[half] += jnp.sum(dc, axis=0, keepdims=True)
            for tap in range(3):
                dw_ref[half, tap:tap + 1, :] += jnp.sum(dc * taps[tap], axis=0, keepdims=True)

    blk, halo, wg, wv, bg, bv, flat = _ffn_specs(tb, tc, False)
    return _hosted_call(
        body, name=name, grid=(D_FF // tc, T // tb), in_specs=[blk, halo, wg, wv, bg, bv, flat],
        out_specs=[blk, pl.BlockSpec((2, 1, tc), lambda j, t: (0, 0, j)), pl.BlockSpec((2, 3, tc), lambda j, t: (0, 0, j))],
        out_shape=[jax.ShapeDtypeStruct((2, T, D_FF), F32), jax.ShapeDtypeStruct((2, 1, D_FF), F32),
                   jax.ShapeDtypeStruct((2, 3, D_FF), F32)],
        scratch=[], args=(u, u, cw, cw, cb, cb, da), semantics=("parallel", "arbitrary"), exchange=exchange)


def _conv_bwd(dc, cw, name):
    T = dc.shape[1]
    tb, tc = min(FFN_ROWS, T), FFN_COLS
    nt, nj = T // tb, D_FF // tc

    def body(dc_ref, halo_ref, wg_ref, wv_ref, du_ref):
        last = pl.program_id(0) == nt - 1
        row = lax.broadcasted_iota(jnp.int32, (tb, 1), 0)
        for half, w_ref in ((0, wg_ref), (1, wv_ref)):
            cur = dc_ref[half]
            halo = halo_ref[half]
            h0 = jnp.where(last, 0.0, halo[0:1])
            h1 = jnp.where(last, 0.0, halo[1:2])
            d1 = jnp.where(row == tb - 1, h0, pltpu.roll(cur, tb - 1, 0))
            d2 = jnp.where(row == tb - 1, h1, jnp.where(row == tb - 2, h0, pltpu.roll(cur, tb - 2, 0)))
            w = w_ref[...]
            du_ref[half] = (w[2:3] * cur + w[1:2] * d1 + w[0:1] * d2).astype(du_ref.dtype)

    blk = pl.BlockSpec((2, tb, tc), lambda t, j: (0, t, j))
    halo = pl.BlockSpec((2, 8, tc), lambda t, j: (0, jnp.minimum((t + 1) * (tb // 8), T // 8 - 1), j))
    wg = pl.BlockSpec((3, tc), lambda t, j: (0, j))
    wv = pl.BlockSpec((3, tc), lambda t, j: (0, j + nj))
    return pl.pallas_call(
        body, name=name, grid=(nt, nj), in_specs=[blk, halo, wg, wv], out_specs=blk,
        out_shape=jax.ShapeDtypeStruct((2, T, D_FF), BF16), compiler_params=_params("parallel", "parallel"),
    )(dc, dc, cw, cw)


def _mesh_pos():
    return lax.axis_index("x"), lax.axis_index("y"), lax.axis_index("c")


def _peer(pos, k):
    return (pos[0] ^ ((k >> 2) & 1), pos[1] ^ ((k >> 1) & 1), pos[2] ^ (k & 1))


def _index(pos):
    return 4 * pos[0] + 2 * pos[1] + pos[2]


class _Exchange:
    def __init__(self, kind, buf):
        assert kind in ("gather", "scatter")
        self.kind, self.buf = kind, buf
        self.out_shape = jax.ShapeDtypeStruct(((N_DEV,) + buf.shape) if kind == "gather" else buf.shape, buf.dtype)
        self.spec = pl.BlockSpec(memory_space=pl.ANY)
        self.scratch = [pltpu.SemaphoreType.DMA((N_DEV - 1,)), pltpu.SemaphoreType.DMA((N_DEV - 1,)),
                        pltpu.SemaphoreType.DMA]

    def _src(self, x_ref, dest):
        return x_ref if self.kind == "gather" else x_ref.at[dest]

    def _copies(self, x_ref, out_ref, send_sems, recv_sems, local_sem):
        pos = _mesh_pos()
        me = _index(pos)
        local = pltpu.make_async_copy(self._src(x_ref, me), out_ref.at[me], local_sem)
        sends, recvs = [], []
        for k in range(1, N_DEV):
            peer = _peer(pos, k)
            sends.append(pltpu.make_async_remote_copy(
                src_ref=self._src(x_ref, _index(peer)), dst_ref=out_ref.at[me], send_sem=send_sems.at[k - 1],
                recv_sem=recv_sems.at[k - 1], device_id=peer, device_id_type=pl.DeviceIdType.MESH))
            recvs.append(pltpu.make_async_remote_copy(
                src_ref=self._src(x_ref, me), dst_ref=out_ref.at[_index(peer)], send_sem=send_sems.at[k - 1],
                recv_sem=recv_sems.at[k - 1], device_id=peer, device_id_type=pl.DeviceIdType.MESH))
        return local, sends, recvs

    def start(self, *refs):
        local, sends, _ = self._copies(*refs)
        local.start()
        for cp in sends:
            cp.start()

    def finish(self, *refs):
        local, sends, recvs = self._copies(*refs)
        for cp in recvs:
            cp.wait_recv()
        for cp in sends:
            cp.wait_send()
        local.wait()


def _hosted_call(body, *, name, grid, in_specs, out_specs, out_shape, scratch, args, semantics, exchange=None):
    if exchange is None:
        return pl.pallas_call(
            body, name=name, grid=grid, in_specs=in_specs, out_specs=out_specs, out_shape=out_shape,
            scratch_shapes=scratch, compiler_params=_params(*semantics))(*args)
    n_in, n_out, n_scr = len(in_specs), len(out_specs), len(scratch)

    def hosted(*refs):
        ins, x_ref = refs[:n_in], refs[n_in]
        outs, land_ref = refs[n_in + 1:n_in + 1 + n_out], refs[n_in + 1 + n_out]
        rest = refs[n_in + n_out + 2:]
        sems = rest[n_scr:]
        ids = [pl.program_id(a) for a in range(len(grid))]
        first, last = ids[0] == 0, ids[0] == grid[0] - 1
        for a in range(1, len(grid)):
            first, last = first & (ids[a] == 0), last & (ids[a] == grid[a] - 1)

        @pl.when(first)
        def _():
            exchange.start(x_ref, land_ref, *sems)

        body(*ins, *outs, *rest[:n_scr])

        @pl.when(last)
        def _():
            exchange.finish(x_ref, land_ref, *sems)

    return pl.pallas_call(
        hosted, name=name, grid=grid, in_specs=list(in_specs) + [exchange.spec],
        out_specs=list(out_specs) + [exchange.spec], out_shape=list(out_shape) + [exchange.out_shape],
        scratch_shapes=list(scratch) + exchange.scratch, compiler_params=_params(*(["arbitrary"] * len(grid))),
    )(*args, exchange.buf)


def _exchange_alone(exchange, name):
    def body(x_ref, out_ref, send_sems, recv_sems, local_sem):
        exchange.start(x_ref, out_ref, send_sems, recv_sems, local_sem)
        exchange.finish(x_ref, out_ref, send_sems, recv_sems, local_sem)

    return pl.pallas_call(
        body, name=name, out_shape=exchange.out_shape, in_specs=[exchange.spec], out_specs=exchange.spec,
        scratch_shapes=exchange.scratch)(exchange.buf)


def _adamw(w, g, m, v):
    m = ADAM_B1 * m + (1.0 - ADAM_B1) * g
    v = ADAM_B2 * v + (1.0 - ADAM_B2) * (g * g)
    m_hat = m / (1.0 - ADAM_B1 ** ADAM_STEP)
    v_hat = v / (1.0 - ADAM_B2 ** ADAM_STEP)
    delta = -ADAM_LR * (m_hat / (jnp.sqrt(v_hat) + ADAM_EPS) + ADAM_WD * w)
    return delta, m, v


def _sum_rows(parts, r0, rows, name, wmv=None):
    C = parts.shape[2]
    tr = max(t for t in range(16, ROWS + 1, 16) if rows % t == 0 and r0 % t == 0)

    def total(p_ref):
        g = p_ref[0].astype(F32)
        for i in range(1, N_DEV):
            g = g + p_ref[i].astype(F32)
        return g

    p_spec = pl.BlockSpec((N_DEV, tr, C), lambda i: (0, r0 // tr + i, 0))
    if wmv is None:
        def body(p_ref, g_ref):
            g_ref[...] = total(p_ref)

        return pl.pallas_call(
            body, name=name, grid=(rows // tr,), in_specs=[p_spec], out_specs=_row_spec(tr, C),
            out_shape=jax.ShapeDtypeStruct((rows, C), F32), compiler_params=_params("parallel"))(parts)

    def body(p_ref, w_ref, m_ref, v_ref, g_ref, d_ref, mo_ref, vo_ref):
        g = total(p_ref)
        g_ref[0] = g
        d_ref[0], mo_ref[0], vo_ref[0] = _adamw(w_ref[0], g, m_ref[0], v_ref[0])

    blk = pl.BlockSpec((1, tr, C), lambda i: (0, i, 0))
    return pl.pallas_call(
        body, name=name, grid=(rows // tr,), in_specs=[p_spec, blk, blk, blk], out_specs=[blk] * 4,
        out_shape=[jax.ShapeDtypeStruct((1, rows, C), F32)] * 4, compiler_params=_params("parallel"))(parts, *wmv)


def _sum_parts(parts, name):
    _, R, C = parts.shape

    def body(p_ref, g_ref):
        g = p_ref[0]
        for i in range(1, N_DEV):
            g = g + p_ref[i]
        g_ref[...] = g

    return pl.pallas_call(body, name=name, out_shape=jax.ShapeDtypeStruct((R, C), F32))(parts)


def _adamw_call(w, g, m, v, name):
    _, R, C = w.shape
    tr = min(ROWS, R)

    def body(w_ref, g_ref, m_ref, v_ref, d_ref, mo_ref, vo_ref):
        d_ref[...], mo_ref[...], vo_ref[...] = _adamw(w_ref[...], g_ref[...], m_ref[...], v_ref[...])

    blk = pl.BlockSpec((1, tr, C), lambda i: (0, i, 0))
    return pl.pallas_call(
        body, name=name, grid=(R // tr,), in_specs=[blk] * 4, out_specs=[blk] * 3,
        out_shape=[jax.ShapeDtypeStruct(w.shape, F32)] * 3, compiler_params=_params("parallel"))(w, g, m, v)


NORMS = ("mix_pre_norm", "mix_post_norm", "ca_pre_norm", "mem_norm", "ca_post_norm", "ffn_pre_norm", "ffn_post_norm")
SMALL = ("mix_pre_norm", "attn_sinks", "hgrn_lb_logits", "hgrn_out_norm", "mix_post_norm", "ca_pre_norm", "mem_norm",
         "ca_post_norm", "ffn_pre_norm", "ffn_conv_w", "ffn_conv_b", "ffn_post_norm")
SMALL_ROWS = 40
ROW_LOGITS, ROW_MISC, ROW_CONV_B, ROW_CONV_W = 7, 8, 9, 15
LANE_SINKS, LANE_LOSS = 128, 256
FF_PIECES = ((0, 1024), (1024, 2048), (2048, D_FF))


def _pack_small(norm_grads, dlogits, donw, dsinks, loss, d_cb, d_cw, name):
    def body(*refs):
        norm_refs = refs[:len(NORMS)]
        dl_ref, donw_ref, dsink_ref, loss_ref, cb_ref, cw_ref, out_ref = refs[len(NORMS):]
        out_ref[...] = jnp.zeros_like(out_ref)
        for i, ref in enumerate(norm_refs):
            out_ref[i:i + 1, :] = ref[...]
        out_ref[ROW_LOGITS:ROW_LOGITS + 1, 0:512] = dl_ref[0:1, :]
        out_ref[ROW_LOGITS:ROW_LOGITS + 1, 512:1024] = dl_ref[1:2, :]
        out_ref[ROW_MISC:ROW_MISC + 1, 0:HGRN_DIM] = donw_ref[...]
        out_ref[ROW_MISC:ROW_MISC + 1, LANE_SINKS:LANE_SINKS + ATTN_Q_HEADS] = dsink_ref[...]
        out_ref[ROW_MISC:ROW_MISC + 1, LANE_LOSS:LANE_LOSS + LANE] = loss_ref[...]
        for h in range(2):
            for j, (c0, c1) in enumerate(FF_PIECES):
                r = ROW_CONV_B + 3 * h + j
                out_ref[r:r + 1, 0:c1 - c0] = cb_ref[h, :, c0:c1]
                for t in range(3):
                    r = ROW_CONV_W + 3 * (3 * h + t) + j
                    out_ref[r:r + 1, 0:c1 - c0] = cw_ref[h, t:t + 1, c0:c1]

    return pl.pallas_call(
        body, name=name, out_shape=jax.ShapeDtypeStruct((SMALL_ROWS, 1024), F32),
    )(*norm_grads, dlogits, donw, dsinks, loss, d_cb, d_cw)


def _adamw_small(total, g_conv_w, w, m, v, name):
    n = len(SMALL)

    def body(*refs):
        t_ref, gcw_ref = refs[:2]
        w_refs, m_refs, v_refs = (dict(zip(SMALL, refs[2 + n * i:2 + n * (i + 1)])) for i in range(3))
        outs = refs[2 + 3 * n:]
        loss_ref = outs[0]
        g_refs, d_refs, mo_refs, vo_refs = (dict(zip(SMALL, outs[1 + n * i:1 + n * (i + 1)])) for i in range(4))
        loss_ref[...] = t_ref[ROW_MISC:ROW_MISC + 1, LANE_LOSS:LANE_LOSS + 1]

        def step(nm, idx, g):
            g_refs[nm][idx] = g
            d_refs[nm][idx], mo_refs[nm][idx], vo_refs[nm][idx] = _adamw(w_refs[nm][idx], g, m_refs[nm][idx], v_refs[nm][idx])

        everything = (slice(None), slice(None))
        for i, nm in enumerate(NORMS):
            step(nm, everything, t_ref[i:i + 1, :])
        step("hgrn_lb_logits", (slice(0, 1), slice(None)), t_ref[ROW_LOGITS:ROW_LOGITS + 1, 0:512])
        step("hgrn_lb_logits", (slice(1, 2), slice(None)), t_ref[ROW_LOGITS:ROW_LOGITS + 1, 512:1024])
        step("hgrn_out_norm", everything, t_ref[ROW_MISC:ROW_MISC + 1, 0:HGRN_DIM])
        step("attn_sinks", everything, t_ref[ROW_MISC:ROW_MISC + 1, LANE_SINKS:LANE_SINKS + ATTN_Q_HEADS])
        for h in range(2):
            for j, (c0, c1) in enumerate(FF_PIECES):
                r = ROW_CONV_B + 3 * h + j
                step("ffn_conv_b", (slice(None), slice(D_FF * h + c0, D_FF * h + c1)), t_ref[r:r + 1, 0:c1 - c0])
        step("ffn_conv_w", (slice(None), slice(None), slice(None)), gcw_ref[...])

    shapes = [jax.ShapeDtypeStruct(w[nm].shape, F32) for nm in SMALL]
    out = pl.pallas_call(
        body, name=name, out_shape=[jax.ShapeDtypeStruct((1, 1), F32)] + shapes * 4,
    )(total, g_conv_w, *[w[nm] for nm in SMALL], *[m[nm] for nm in SMALL], *[v[nm] for nm in SMALL])
    trees = [dict(zip(SMALL, out[1 + n * i:1 + n * (i + 1)])) for i in range(4)]
    return out[0], trees


BIG = ("w_in", "w_out", "ca_wq", "ca_wk", "ca_wv", "ca_wo", "ffn_w_up", "ffn_w_down")
BIG_FULL = {"w_in": (1024, 2816), "w_out": (1024, 1024), "ca_wq": (1024, 1024), "ca_wk": (1024, 1024),
            "ca_wv": (1024, 1024), "ca_wo": (1024, 1024), "ffn_w_up": (1024, 5632), "ffn_w_down": (2816, 1024)}
G_IN, G_MID, G_UP, G_DOWN = ("w_in",), ("w_out", "ca_wq", "ca_wk", "ca_wv", "ca_wo"), ("ffn_w_up",), ("ffn_w_down",)
GROUPS = (G_IN, G_MID, G_UP, G_DOWN)
COL_SHARDED = ("w_in", "ffn_w_up")
PACK_COLS = 1024


def _big_rows(name):
    r, c = BIG_FULL[name]
    return r * c // N_DEV // PACK_COLS


def _pack_shards(w, names):
    rows = [w[n][0].T if n in COL_SHARDED else w[n][0] for n in names]
    return (rows[0] if len(rows) == 1 else jnp.concatenate(rows, axis=0)).astype(BF16)


def _unpack_gathered(gathered, names):
    out, r0 = {}, 0
    for n in names:
        rows = _big_rows(n)
        out[n] = gathered[:, r0:r0 + rows].reshape(N_DEV * rows, PACK_COLS)
        r0 += rows
    return out


def _pack_full_grads(grads, names):
    parts = [grads[n].reshape(N_DEV, _big_rows(n), PACK_COLS) for n in names]
    return parts[0] if len(parts) == 1 else jnp.concatenate(parts, axis=1)


def kernel(x, mem, mix_pre_norm, w_in, attn_sinks, hgrn_lb_logits, hgrn_out_norm, w_out, mix_post_norm, ca_pre_norm, mem_norm, ca_wq, ca_wk, ca_wv, ca_wo, ca_post_norm, ffn_pre_norm, ffn_w_up, ffn_conv_w, ffn_conv_b, ffn_w_down, ffn_post_norm, loss_target, m_mix_pre_norm, m_w_in, m_attn_sinks, m_hgrn_lb_logits, m_hgrn_out_norm, m_w_out, m_mix_post_norm, m_ca_pre_norm, m_mem_norm, m_ca_wq, m_ca_wk, m_ca_wv, m_ca_wo, m_ca_post_norm, m_ffn_pre_norm, m_ffn_w_up, m_ffn_conv_w, m_ffn_conv_b, m_ffn_w_down, m_ffn_post_norm, v_mix_pre_norm, v_w_in, v_attn_sinks, v_hgrn_lb_logits, v_hgrn_out_norm, v_w_out, v_mix_post_norm, v_ca_pre_norm, v_mem_norm, v_ca_wq, v_ca_wk, v_ca_wv, v_ca_wo, v_ca_post_norm, v_ffn_pre_norm, v_ffn_w_up, v_ffn_conv_w, v_ffn_conv_b, v_ffn_w_down, v_ffn_post_norm):
    names = ["mix_pre_norm", "w_in", "attn_sinks", "hgrn_lb_logits", "hgrn_out_norm", "w_out", "mix_post_norm",
             "ca_pre_norm", "mem_norm", "ca_wq", "ca_wk", "ca_wv", "ca_wo", "ca_post_norm", "ffn_pre_norm",
             "ffn_w_up", "ffn_conv_w", "ffn_conv_b", "ffn_w_down", "ffn_post_norm"]
    w_all = dict(zip(names, [mix_pre_norm, w_in, attn_sinks, hgrn_lb_logits, hgrn_out_norm, w_out, mix_post_norm,
                             ca_pre_norm, mem_norm, ca_wq, ca_wk, ca_wv, ca_wo, ca_post_norm, ffn_pre_norm,
                             ffn_w_up, ffn_conv_w, ffn_conv_b, ffn_w_down, ffn_post_norm]))
    m_all = dict(zip(names, [m_mix_pre_norm, m_w_in, m_attn_sinks, m_hgrn_lb_logits, m_hgrn_out_norm, m_w_out,
                             m_mix_post_norm, m_ca_pre_norm, m_mem_norm, m_ca_wq, m_ca_wk, m_ca_wv, m_ca_wo,
                             m_ca_post_norm, m_ffn_pre_norm, m_ffn_w_up, m_ffn_conv_w, m_ffn_conv_b, m_ffn_w_down,
                             m_ffn_post_norm]))
    v_all = dict(zip(names, [v_mix_pre_norm, v_w_in, v_attn_sinks, v_hgrn_lb_logits, v_hgrn_out_norm, v_w_out,
                             v_mix_post_norm, v_ca_pre_norm, v_mem_norm, v_ca_wq, v_ca_wk, v_ca_wv, v_ca_wo,
                             v_ca_post_norm, v_ffn_pre_norm, v_ffn_w_up, v_ffn_conv_w, v_ffn_conv_b, v_ffn_w_down,
                             v_ffn_post_norm]))
    dev = _index(_mesh_pos())

    w_packs = {grp: _pack_shards(w_all, grp) for grp in GROUPS}
    shard_w = D_FF * 2 // N_DEV
    conv_w_rows = _exchange_alone(_Exchange("gather", ffn_conv_w[0]), "gather_conv_w")
    conv_w_full = conv_w_rows.transpose(1, 0, 2).reshape(3, 2 * D_FF)

    received, small_pack, grad_x = _local_step(
        x[0], mem[0], loss_target[0], w_packs, conv_w_full,
        {n: w_all[n] for n in NORMS}, attn_sinks, hgrn_lb_logits, hgrn_out_norm, ffn_conv_b)

    total = _sum_parts(_exchange_alone(_Exchange("gather", small_pack), "gather_small"), "sum_small")
    cw = total[ROW_CONV_W:ROW_CONV_W + 18].reshape(2, 3, 3 * PACK_COLS)[:, :, :D_FF]
    cw = cw.transpose(1, 0, 2).reshape(3, 2 * D_FF)
    g_conv_w = lax.dynamic_slice_in_dim(cw, dev * shard_w, shard_w, axis=1)[None]
    loss, (out_g, out_d, out_m, out_v) = _adamw_small(total, g_conv_w, w_all, m_all, v_all, "adamw_small")

    for grp in GROUPS:
        r0 = 0
        for n in grp:
            rows = _big_rows(n)
            if n in COL_SHARDED:
                g = _sum_rows(received[grp], r0, rows, "sum_" + n).T[None]
                d, mo, vo = _adamw_call(w_all[n], g, m_all[n], v_all[n], "adamw_" + n)
            else:
                g, d, mo, vo = _sum_rows(received[grp], r0, rows, "adamw_" + n, wmv=(w_all[n], m_all[n], v_all[n]))
            out_g[n], out_d[n], out_m[n], out_v[n] = g, d, mo, vo
            r0 += rows

    return (loss[0, 0], grad_x[None], *[out_g[n] for n in names], *[out_d[n] for n in names],
            *[out_m[n] for n in names], *[out_v[n] for n in names])


def _local_step(x, mem, target, w_packs, conv_w, norms, sinks, lb_logits, out_norm, conv_b):
    g1, g2, g3 = norms["mix_pre_norm"], norms["mix_post_norm"], norms["ca_pre_norm"]
    g4, g5, g6, g7 = norms["mem_norm"], norms["ca_post_norm"], norms["ffn_pre_norm"], norms["ffn_post_norm"]

    h1, gathered = _norm_fwd(x, g1, "mix_norm", exchange=_Exchange("gather", w_packs[G_IN]))
    w_in_t = _unpack_gathered(gathered, G_IN)["w_in"]
    z = _mm(h1, w_in_t, mode="nt", out_dtype=F32, name="in_proj", tn=1408)
    attn, lse, gathered = _swa_fwd(z, sinks, "swa_fwd", exchange=_Exchange("gather", w_packs[G_DOWN]))
    w_down = _unpack_gathered(gathered, G_DOWN)["ffn_w_down"]
    lb = _lower_bound(lb_logits, "lower_bound")
    rec, o_rec, states, gathered = _hgrn_fwd(
        z, lb, out_norm, "hgrn_fwd",
        exchange=_Exchange("gather", jnp.concatenate([w_packs[G_MID], w_packs[G_UP]], axis=0)))
    wf = _unpack_gathered(gathered, G_MID + G_UP)
    w_out, wq, wk, wv, wo, w_up_t = (wf[n] for n in G_MID + G_UP)
    cat = jnp.concatenate([attn, rec], axis=1)
    mix = _mm(cat, w_out, mode="nn", out_dtype=F32, name="out_proj")
    x1, h2 = _post_pre(x, mix, g2, g3, "mix_post")
    mem_n = _norm_fwd(mem, g4, "mem_norm")
    q = _mm(h2, wq, mode="nn", out_dtype=BF16, name="ca_q")
    k = _mm(mem_n, wk, mode="nn", out_dtype=BF16, name="ca_k")
    v = _mm(mem_n, wv, mode="nn", out_dtype=BF16, name="ca_v")
    oc = _ca_fwd(q, k, v, "ca_fwd")
    c = _mm(oc, wo, mode="nn", out_dtype=F32, name="ca_o")
    x2, h3 = _post_pre(x1, c, g5, g6, "ca_post")
    u = _mm(h3, w_up_t, mode="nt", out_dtype=F32, name="ffn_up", tn=1408, split_out=True)
    a = _glu_fwd(u, conv_w, conv_b, "glu_fwd")
    y = _mm(a, w_down, mode="nn", out_dtype=F32, name="ffn_down", tk=2816)
    loss, dx3, dy, dg7 = _final(x2, y, g7, target, "loss_head")

    da = _mm(dy, w_down, mode="nt", out_dtype=F32, name="ffn_down_dx", tn=1408)
    d_w_down = _mm(a, dy, mode="tn", out_dtype=BF16, name="ffn_down_dw", tm=1408, tk=512)
    dc, d_cb, d_cw, got_down = _glu_bwd(
        u, conv_w, conv_b, da, "glu_bwd",
        exchange=_Exchange("scatter", _pack_full_grads({"ffn_w_down": d_w_down}, G_DOWN)))
    du = _conv_bwd(dc, conv_w, "conv_bwd")
    d_w_up_t = _mm(du, h3, mode="tn", out_dtype=BF16, name="ffn_up_dw", tm=1408, tk=512, split_a=True)
    dh3, got_up = _mm(du, w_up_t, mode="nn", out_dtype=F32, name="ffn_up_dx", tm=2048, tk=1408, split_a=True,
                      exchange=_Exchange("scatter", _pack_full_grads({"ffn_w_up": d_w_up_t}, G_UP)))
    dx2, dcv, dg6, dg5 = _norm_bwd2(dx3, dh3, x2, g6, c, g5, "ca_post_bwd")
    doc = _mm(dcv, wo, mode="nt", out_dtype=BF16, name="ca_o_dx")
    d_wo = _mm(oc, dcv, mode="tn", out_dtype=BF16, name="ca_o_dw", tm=1024, tk=512)
    dq, dk, dv = _ca_bwd(q, k, v, doc, "ca_bwd")
    d_wq = _mm(h2, dq, mode="tn", out_dtype=BF16, name="ca_q_dw", tm=1024, tk=512)
    dh2 = _mm(dq, wq, mode="nt", out_dtype=F32, name="ca_q_dx")
    d_wk = _mm(mem_n, dk, mode="tn", out_dtype=BF16, name="ca_k_dw", tm=1024)
    d_wv = _mm(mem_n, dv, mode="tn", out_dtype=BF16, name="ca_v_dw", tm=1024)
    dmem_k = _mm(dk, wk, mode="nt", out_dtype=F32, name="ca_k_dx")
    dmem_v = _mm(dv, wv, mode="nt", out_dtype=F32, name="ca_v_dx")
    dg4 = _gain_bwd(mem, dmem_k, dmem_v, "mem_norm_bwd")
    dx1, dmix, dg3, dg2 = _norm_bwd2(dx2, dh2, x1, g3, mix, g2, "mix_post_bwd")
    dcat = _mm(dmix, w_out, mode="nt", out_dtype=F32, name="out_proj_dx")
    d_w_out = _mm(cat, dmix, mode="tn", out_dtype=BF16, name="out_proj_dw", tm=1024, tk=512)
    mid = {"w_out": d_w_out, "ca_wq": d_wq, "ca_wk": d_wk, "ca_wv": d_wv, "ca_wo": d_wo}
    dqr, dfr, dir_, dgr, dlb, donw, got_mid = _hgrn_bwd(
        z, lb, out_norm, o_rec, states, dcat, "hgrn_bwd", exchange=_Exchange("scatter", _pack_full_grads(mid, G_MID)))
    dq_a, dka, dkb, dva, dvb, dsinks = _swa_bwd(z, sinks, dcat, lse, "swa_bwd")
    dz = _assemble_dz(dq_a, dka, dkb, dva, dvb, dqr, dfr, dir_, dgr, "assemble_dz")
    d_w_in_t = _mm(dz, h1, mode="tn", out_dtype=BF16, name="in_proj_dw", tm=1408, tk=512)
    dh1, got_in = _mm(dz, w_in_t, mode="nn", out_dtype=F32, name="in_proj_dx", tk=2816,
                      exchange=_Exchange("scatter", _pack_full_grads({"w_in": d_w_in_t}, G_IN)))
    dx, dg1 = _norm_bwd1(dx1, dh1, x, g1, "mix_norm_bwd")

    small_pack = _pack_small(
        (dg1, dg2, dg3, dg4, dg5, dg6, dg7), _lower_bound_bwd(lb, dlb, "lower_bound_bwd"), donw, dsinks, loss,
        d_cb, d_cw, "pack_small")
    return {G_IN: got_in, G_MID: got_mid, G_UP: got_up, G_DOWN: got_down}, small_pack, dx
```

```python
import jax
import jax.numpy as jnp
from jax import lax
from jax.experimental import pallas as pl
from jax.experimental.pallas import tpu as pltpu

F32 = jnp.float32
BF16 = jnp.bfloat16
EPS = 1e-6
N_DEV = 8
MESH_AXES = ("x", "y", "c")

ATTN_HEAD_DIM = 64
ATTN_Q_HEADS = 8
ATTN_KV_HEADS = 2
ATTN_BLOCK = 128
HGRN_HEADS = 4
HGRN_DIM = 128
HGRN_CHUNK = 64
HGRN_PAIR = 2
HGRN_LEVELS = (32, 16, 8, 4, 2, 1)
CA_HEADS = 4
CA_HEAD_DIM = 256
D_FF = 2816

ADAM_LR = 0.001
ADAM_B1 = 0.9
ADAM_B2 = 0.999
ADAM_EPS = 1e-08
ADAM_WD = 0.01
ADAM_STEP = 10

VMEM_LIMIT = 56 << 20
LANE = 128

NT = (((1,), (1,)), ((), ()))
TN = (((0,), (0,)), ((), ()))


def _params(*sem):
    return pltpu.CompilerParams(dimension_semantics=sem, vmem_limit_bytes=VMEM_LIMIT)


def _tile(n, cap):
    if n <= cap:
        return n
    best = 0
    for t in range(LANE, cap + 1, LANE):
        if n % t == 0:
            best = t
    assert best, (n, cap)
    return best


def _dot(a, b, dims=None):
    if dims is None:
        return jnp.dot(a, b, preferred_element_type=F32)
    return lax.dot_general(a, b, dims, preferred_element_type=F32)


def _bf(x):
    return x.astype(BF16)


def _sigmoid(x):
    return 1.0 / (1.0 + jnp.exp(-x))


def _rms(x):
    r = lax.rsqrt(jnp.mean(x * x, axis=-1, keepdims=True) + EPS)
    return x * r, r


def _rms_bwd(dxh, xh, r):
    return r * (dxh - xh * jnp.mean(dxh * xh, axis=-1, keepdims=True))


def _mm(a, b, *, mode, out_dtype, name, tm=1024, tn=1024, tk=1024, split_a=False, split_b=False, split_out=False,
        exchange=None):
    def dims(arr, split):
        if split:
            return arr.shape[1], 2 * arr.shape[2]
        return arr.shape

    ar, ac = dims(a, split_a)
    br, bc = dims(b, split_b)
    if mode == "nn":
        M, K, N = ar, ac, bc
        assert br == K
    elif mode == "nt":
        M, K, N = ar, ac, br
        assert bc == K
    else:
        K, M, N = ar, ac, bc
        assert br == K
    a_cols_half = ac // 2 if split_a else None
    b_cols_half = bc // 2 if split_b else None
    tm = _tile(M, tm)
    tn = _tile((N // 2) if (split_out or (split_b and mode != "nt")) else N, tn)
    tk = _tile((K // 2) if ((split_a and mode != "tn") or (split_b and mode == "nt")) else K, tk)
    if split_a and mode == "tn":
        tm = _tile(M // 2, tm)
    gm, gn, gk = M // tm, N // tn, K // tk
    a_bytes, b_bytes = a.size * a.dtype.itemsize, b.size * b.dtype.itemsize
    rows_outer = gk > 1 or a_bytes + gm * b_bytes <= gn * a_bytes + b_bytes
    grid = (gm, gn, gk) if rows_outer else (gn, gm, gk)

    def spec(split, half, blk, rc):
        def imap(p, q, k):
            r, c = rc(*((p, q) if rows_outer else (q, p)), k)
            if not split:
                return (r, c)
            per_half = half // blk[1]
            return (c // per_half, r, c % per_half)

        return pl.BlockSpec(((None,) + blk) if split else blk, imap)

    if mode == "nn":
        a_spec = spec(split_a, a_cols_half, (tm, tk), lambda i, j, k: (i, k))
        b_spec = spec(split_b, b_cols_half, (tk, tn), lambda i, j, k: (k, j))
        dn = None
    elif mode == "nt":
        a_spec = spec(split_a, a_cols_half, (tm, tk), lambda i, j, k: (i, k))
        b_spec = spec(split_b, b_cols_half, (tn, tk), lambda i, j, k: (j, k))
        dn = NT
    else:
        a_spec = spec(split_a, a_cols_half, (tk, tm), lambda i, j, k: (k, i))
        b_spec = spec(split_b, b_cols_half, (tk, tn), lambda i, j, k: (k, j))
        dn = TN
    o_spec = spec(split_out, N // 2 if split_out else None, (tm, tn), lambda i, j, k: (i, j))
    out_shape = (2, M, N // 2) if split_out else (M, N)

    if gk == 1:
        def body(a_ref, b_ref, o_ref):
            o_ref[...] = _dot(_bf(a_ref[...]), _bf(b_ref[...]), dn).astype(o_ref.dtype)
        scratch = []
    else:
        def body(a_ref, b_ref, o_ref, acc_ref):
            k = pl.program_id(2)

            @pl.when(k == 0)
            def _():
                acc_ref[...] = jnp.zeros_like(acc_ref)

            acc_ref[...] += _dot(_bf(a_ref[...]), _bf(b_ref[...]), dn)

            @pl.when(k == gk - 1)
            def _():
                o_ref[...] = acc_ref[...].astype(o_ref.dtype)
        scratch = [pltpu.VMEM((tm, tn), F32)]

    out = _hosted_call(
        body, name=name, grid=grid, in_specs=[a_spec, b_spec], out_specs=[o_spec],
        out_shape=[jax.ShapeDtypeStruct(out_shape, out_dtype)], scratch=scratch, args=(a, b),
        semantics=("parallel", "parallel", "arbitrary"), exchange=exchange)
    return out[0] if exchange is None else out


ROWS = 512


def _row_spec(tr, cols):
    return pl.BlockSpec((tr, cols), lambda i: (i, 0))


def _vec_spec(cols):
    return pl.BlockSpec((1, cols), lambda i: (0, 0))


def _norm_fwd(x, g, name, exchange=None):
    T, Dm = x.shape
    tr = min(ROWS, T)

    def body(x_ref, g_ref, h_ref):
        xh, _ = _rms(x_ref[...])
        h_ref[...] = (xh * g_ref[...]).astype(h_ref.dtype)

    out = _hosted_call(
        body, name=name, grid=(T // tr,), in_specs=[_row_spec(tr, Dm), _vec_spec(Dm)], out_specs=[_row_spec(tr, Dm)],
        out_shape=[jax.ShapeDtypeStruct((T, Dm), BF16)], scratch=[], args=(x, g), semantics=("parallel",),
        exchange=exchange)
    return out[0] if exchange is None else out


def _post_pre(x, m, g_post, g_pre, name):
    T, Dm = x.shape
    tr = min(ROWS, T)

    def body(x_ref, m_ref, gp_ref, gn_ref, xo_ref, h_ref):
        mh, _ = _rms(m_ref[...])
        xn = x_ref[...] + mh * gp_ref[...]
        xo_ref[...] = xn
        xh, _ = _rms(xn)
        h_ref[...] = (xh * gn_ref[...]).astype(h_ref.dtype)

    return pl.pallas_call(
        body, name=name, grid=(T // tr,),
        in_specs=[_row_spec(tr, Dm), _row_spec(tr, Dm), _vec_spec(Dm), _vec_spec(Dm)],
        out_specs=[_row_spec(tr, Dm), _row_spec(tr, Dm)],
        out_shape=[jax.ShapeDtypeStruct((T, Dm), F32), jax.ShapeDtypeStruct((T, Dm), BF16)],
        compiler_params=_params("parallel"),
    )(x, m, g_post, g_pre)


def _final(x2, y, g_post, target, name):
    T, Dm = x2.shape
    tr = min(ROWS, T)

    def body(x_ref, y_ref, g_ref, t_ref, loss_ref, dx_ref, dy_ref, dg_ref):
        @pl.when(pl.program_id(0) == 0)
        def _():
            loss_ref[...] = jnp.zeros_like(loss_ref)
            dg_ref[...] = jnp.zeros_like(dg_ref)

        g = g_ref[...]
        yh, r = _rms(y_ref[...])
        d = x_ref[...] + yh * g - t_ref[...]
        loss_ref[...] += jnp.zeros((1, LANE), F32) + 0.5 * jnp.sum(jnp.mean(d * d, axis=-1, keepdims=True))
        dx = d * (1.0 / Dm)
        dx_ref[...] = dx
        dy_ref[...] = _rms_bwd(dx * g, yh, r).astype(dy_ref.dtype)
        dg_ref[...] += jnp.sum(dx * yh, axis=0, keepdims=True)

    return pl.pallas_call(
        body, name=name, grid=(T // tr,),
        in_specs=[_row_spec(tr, Dm), _row_spec(tr, Dm), _vec_spec(Dm), _row_spec(tr, Dm)],
        out_specs=[_vec_spec(LANE), _row_spec(tr, Dm), _row_spec(tr, Dm), _vec_spec(Dm)],
        out_shape=[jax.ShapeDtypeStruct((1, LANE), F32), jax.ShapeDtypeStruct((T, Dm), F32),
                   jax.ShapeDtypeStruct((T, Dm), BF16), jax.ShapeDtypeStruct((1, Dm), F32)],
        compiler_params=_params("arbitrary"),
    )(x2, y, g_post, target)


def _norm_bwd2(dx_cur, dh, x_prev, g_pre, m_prev, g_post, name):
    T, Dm = x_prev.shape
    tr = min(ROWS, T)

    def body(dx_ref, dh_ref, x_ref, gn_ref, m_ref, gp_ref, dxo_ref, dm_ref, dgn_ref, dgp_ref):
        @pl.when(pl.program_id(0) == 0)
        def _():
            dgn_ref[...] = jnp.zeros_like(dgn_ref)
            dgp_ref[...] = jnp.zeros_like(dgp_ref)

        dh = dh_ref[...].astype(F32)
        xh, r = _rms(x_ref[...])
        dx = dx_ref[...] + _rms_bwd(dh * gn_ref[...], xh, r)
        dxo_ref[...] = dx
        dgn_ref[...] += jnp.sum(dh * xh, axis=0, keepdims=True)
        mh, rm = _rms(m_ref[...])
        dm_ref[...] = _rms_bwd(dx * gp_ref[...], mh, rm).astype(dm_ref.dtype)
        dgp_ref[...] += jnp.sum(dx * mh, axis=0, keepdims=True)

    return pl.pallas_call(
        body, name=name, grid=(T // tr,),
        in_specs=[_row_spec(tr, Dm), _row_spec(tr, Dm), _row_spec(tr, Dm), _vec_spec(Dm), _row_spec(tr, Dm), _vec_spec(Dm)],
        out_specs=[_row_spec(tr, Dm), _row_spec(tr, Dm), _vec_spec(Dm), _vec_spec(Dm)],
        out_shape=[jax.ShapeDtypeStruct((T, Dm), F32), jax.ShapeDtypeStruct((T, Dm), BF16),
                   jax.ShapeDtypeStruct((1, Dm), F32), jax.ShapeDtypeStruct((1, Dm), F32)],
        compiler_params=_params("arbitrary"),
    )(dx_cur, dh, x_prev, g_pre, m_prev, g_post)


def _norm_bwd1(dx_cur, dh, x_prev, g_pre, name):
    T, Dm = x_prev.shape
    tr = min(ROWS, T)

    def body(dx_ref, dh_ref, x_ref, gn_ref, dxo_ref, dgn_ref):
        @pl.when(pl.program_id(0) == 0)
        def _():
            dgn_ref[...] = jnp.zeros_like(dgn_ref)

        dh = dh_ref[...].astype(F32)
        xh, r = _rms(x_ref[...])
        dxo_ref[...] = dx_ref[...] + _rms_bwd(dh * gn_ref[...], xh, r)
        dgn_ref[...] += jnp.sum(dh * xh, axis=0, keepdims=True)

    return pl.pallas_call(
        body, name=name, grid=(T // tr,),
        in_specs=[_row_spec(tr, Dm), _row_spec(tr, Dm), _row_spec(tr, Dm), _vec_spec(Dm)],
        out_specs=[_row_spec(tr, Dm), _vec_spec(Dm)],
        out_shape=[jax.ShapeDtypeStruct((T, Dm), F32), jax.ShapeDtypeStruct((1, Dm), F32)],
        compiler_params=_params("arbitrary"),
    )(dx_cur, dh, x_prev, g_pre)


def _gain_bwd(x, dh_a, dh_b, name):
    T, Dm = x.shape

    def body(x_ref, a_ref, b_ref, dg_ref):
        xh, _ = _rms(x_ref[...])
        dg_ref[...] = jnp.sum((a_ref[...] + b_ref[...]) * xh, axis=0, keepdims=True)

    return pl.pallas_call(
        body, name=name, grid=(1,), in_specs=[_row_spec(T, Dm)] * 3, out_specs=_vec_spec(Dm),
        out_shape=jax.ShapeDtypeStruct((1, Dm), F32), compiler_params=_params("arbitrary"),
    )(x, dh_a, dh_b)


ATTN_GROUP = ATTN_Q_HEADS // ATTN_KV_HEADS


def _swa_mask(n):
    rows = ATTN_GROUP * ATTN_BLOCK
    row = lax.broadcasted_iota(jnp.int32, (rows, 2 * ATTN_BLOCK), 0) & (ATTN_BLOCK - 1)
    col = lax.broadcasted_iota(jnp.int32, (rows, 2 * ATTN_BLOCK), 1)
    diff = row + ATTN_BLOCK - col
    return (diff >= 0) & (diff < ATTN_BLOCK) & ((col >= ATTN_BLOCK) | (n > 0))


def _swa_rows(ref, hk, dtype):
    hd = ATTN_HEAD_DIM
    return jnp.concatenate(
        [ref[:, hd * (hk * ATTN_GROUP + g):hd * (hk * ATTN_GROUP + g + 1)].astype(dtype) for g in range(ATTN_GROUP)],
        axis=0)


def _swa_per_row(vals):
    seg = lax.broadcasted_iota(jnp.int32, (ATTN_GROUP * ATTN_BLOCK, 1), 0) // ATTN_BLOCK
    col = jnp.zeros((ATTN_GROUP * ATTN_BLOCK, 1), F32)
    for g, val in enumerate(vals):
        col = jnp.where(seg == g, val, col)
    return col


def _swa_specs():
    blk = ATTN_BLOCK
    prev = lambda n: jnp.maximum(n - 1, 0)
    return [
        pl.BlockSpec(memory_space=pltpu.SMEM),
        pl.BlockSpec((blk, 512), lambda n: (n, 0)),
        pl.BlockSpec((blk, 128), lambda n: (prev(n), 4)),
        pl.BlockSpec((blk, 128), lambda n: (n, 4)),
        pl.BlockSpec((blk, 128), lambda n: (prev(n), 5)),
        pl.BlockSpec((blk, 128), lambda n: (n, 5)),
    ]


def _swa_fwd(z, sinks, name, exchange=None):
    T = z.shape[0]
    blk, hd = ATTN_BLOCK, ATTN_HEAD_DIM
    scale = hd ** -0.5

    def body(sink_ref, q_ref, kp_ref, kc_ref, vp_ref, vc_ref, o_ref, lse_ref):
        allowed = _swa_mask(pl.program_id(0))
        hks = range(ATTN_KV_HEADS)
        kss = [slice(hd * hk, hd * hk + hd) for hk in hks]
        k = [_bf(jnp.concatenate([kp_ref[:, ks], kc_ref[:, ks]], axis=0)) for ks in kss]
        v = [_bf(jnp.concatenate([vp_ref[:, ks], vc_ref[:, ks]], axis=0)) for ks in kss]
        s = [jnp.where(allowed, _dot(_swa_rows(q_ref, hk, BF16), k[hk], NT) * scale, -1e30) for hk in hks]
        sink = [_swa_per_row([sink_ref[0, hk * ATTN_GROUP + g] for g in range(ATTN_GROUP)]) for hk in hks]
        m = [jnp.maximum(jnp.max(s[hk], axis=-1, keepdims=True), sink[hk]) for hk in hks]
        p = [jnp.exp(s[hk] - m[hk]) for hk in hks]
        l = [jnp.sum(p[hk], axis=-1, keepdims=True) + jnp.exp(sink[hk] - m[hk]) for hk in hks]
        o = [_dot(_bf(p[hk] / l[hk]), v[hk]).astype(o_ref.dtype) for hk in hks]
        for hk in hks:
            lse = m[hk] + jnp.log(l[hk])
            for g in range(ATTN_GROUP):
                h = hk * ATTN_GROUP + g
                o_ref[:, hd * h:hd * (h + 1)] = o[hk][blk * g:blk * (g + 1)]
                lse_ref[:, h:h + 1] = lse[blk * g:blk * (g + 1)]

    return _hosted_call(
        body, name=name, grid=(T // blk,), in_specs=_swa_specs(),
        out_specs=[pl.BlockSpec((blk, 512), lambda n: (n, 0)), pl.BlockSpec((blk, ATTN_Q_HEADS), lambda n: (n, 0))],
        out_shape=[jax.ShapeDtypeStruct((T, 512), BF16), jax.ShapeDtypeStruct((T, ATTN_Q_HEADS), F32)],
        scratch=[], args=(sinks, z, z, z, z, z), semantics=("parallel",), exchange=exchange)


def _swa_bwd(z, sinks, dcat, lse, name):
    T = z.shape[0]
    blk, hd = ATTN_BLOCK, ATTN_HEAD_DIM
    scale = hd ** -0.5
    group = ATTN_Q_HEADS // ATTN_KV_HEADS

    def body(sink_ref, q_ref, kp_ref, kc_ref, vp_ref, vc_ref, do_ref, lse_ref,
             dq_ref, dka_ref, dkb_ref, dva_ref, dvb_ref, dsink_ref):
        @pl.when(pl.program_id(0) == 0)
        def _():
            dsink_ref[...] = jnp.zeros_like(dsink_ref)

        allowed = _swa_mask(pl.program_id(0))
        lane = lax.broadcasted_iota(jnp.int32, (1, ATTN_Q_HEADS), 1)
        dsink = jnp.zeros((1, ATTN_Q_HEADS), F32)
        hks = range(ATTN_KV_HEADS)
        kss = [slice(hd * hk, hd * hk + hd) for hk in hks]
        k = [_bf(jnp.concatenate([kp_ref[:, ks], kc_ref[:, ks]], axis=0)) for ks in kss]
        v = [_bf(jnp.concatenate([vp_ref[:, ks], vc_ref[:, ks]], axis=0)) for ks in kss]
        qs = [_swa_rows(q_ref, hk, BF16) for hk in hks]
        dos = [_swa_rows(do_ref, hk, BF16) for hk in hks]
        lse = [jnp.concatenate([lse_ref[:, hk * group + g:hk * group + g + 1] for g in range(group)], axis=0)
               for hk in hks]
        s = [_dot(qs[hk], k[hk], NT) * scale for hk in hks]
        dp = [_dot(dos[hk], v[hk], NT) for hk in hks]
        p = [jnp.where(allowed, jnp.exp(jnp.where(allowed, s[hk], -1e30) - lse[hk]), 0.0) for hk in hks]
        delta = [jnp.sum(p[hk] * dp[hk], axis=-1, keepdims=True) for hk in hks]
        ds = [_bf(p[hk] * (dp[hk] - delta[hk]) * scale) for hk in hks]
        dq = [_dot(ds[hk], k[hk]).astype(dq_ref.dtype) for hk in hks]
        dk = [_dot(ds[hk], qs[hk], TN) for hk in hks]
        dv = [_dot(_bf(p[hk]), dos[hk], TN) for hk in hks]
        for hk in hks:
            sink = _swa_per_row([sink_ref[0, hk * group + g] for g in range(group)])
            sink_part = jnp.exp(sink - lse[hk]) * delta[hk]
            for g in range(group):
                h = hk * group + g
                dq_ref[:, hd * h:hd * (h + 1)] = dq[hk][blk * g:blk * (g + 1)]
                dsink = dsink + jnp.where(lane == h, -jnp.sum(sink_part[blk * g:blk * (g + 1)]), 0.0)
            dkb_ref[:, kss[hk]] = dk[hk][:blk]
            dka_ref[:, kss[hk]] = dk[hk][blk:]
            dvb_ref[:, kss[hk]] = dv[hk][:blk]
            dva_ref[:, kss[hk]] = dv[hk][blk:]
        dsink_ref[...] += dsink

    kv_out = pl.BlockSpec((blk, 128), lambda n: (n, 0))
    return pl.pallas_call(
        body, name=name, grid=(T // blk,),
        in_specs=_swa_specs() + [pl.BlockSpec((blk, 512), lambda n: (n, 0)),
                                 pl.BlockSpec((blk, ATTN_Q_HEADS), lambda n: (n, 0))],
        out_specs=[pl.BlockSpec((blk, 512), lambda n: (n, 0)), kv_out, kv_out, kv_out, kv_out,
                   pl.BlockSpec((1, ATTN_Q_HEADS), lambda n: (0, 0))],
        out_shape=[jax.ShapeDtypeStruct((T, 512), BF16)] + [jax.ShapeDtypeStruct((T, 128), F32)] * 4
        + [jax.ShapeDtypeStruct((1, ATTN_Q_HEADS), F32)],
        compiler_params=_params("arbitrary"),
    )(sinks, z, z, z, z, z, dcat, lse)


def _assemble_dz(dq_a, dka, dkb, dva, dvb, dqr, dfr, dir_, dgr, name):
    T = dq_a.shape[0]
    blk = ATTN_BLOCK
    nb = T // blk

    def body(dq_ref, dka_ref, dkb_ref, dva_ref, dvb_ref, dqr_ref, dfr_ref, dir_ref, dgr_ref, o_ref):
        has_next = pl.program_id(0) < nb - 1
        o_ref[:, 0:512] = dq_ref[...]
        o_ref[:, 512:640] = (dka_ref[...] + jnp.where(has_next, dkb_ref[...], 0.0)).astype(o_ref.dtype)
        o_ref[:, 640:768] = (dva_ref[...] + jnp.where(has_next, dvb_ref[...], 0.0)).astype(o_ref.dtype)
        o_ref[:, 768:1280] = dqr_ref[...]
        o_ref[:, 1280:1792] = dfr_ref[...]
        o_ref[:, 1792:2304] = dir_ref[...]
        o_ref[:, 2304:2816] = dgr_ref[...]

    cur = lambda w: pl.BlockSpec((blk, w), lambda n: (n, 0))
    nxt = pl.BlockSpec((blk, 128), lambda n: (jnp.minimum(n + 1, nb - 1), 0))
    return pl.pallas_call(
        body, name=name, grid=(nb,),
        in_specs=[cur(512), cur(128), nxt, cur(128), nxt, cur(512), cur(512), cur(512), cur(512)],
        out_specs=pl.BlockSpec((blk, 2816), lambda n: (n, 0)),
        out_shape=jax.ShapeDtypeStruct((T, 2816), BF16), compiler_params=_params("parallel"),
    )(dq_a, dka, dkb, dva, dvb, dqr, dfr, dir_, dgr)


HGRN_ROWS = 512


def _hgrn_consts():
    c = HGRN_CHUNK
    r = lax.broadcasted_iota(jnp.int32, (c, c), 0)
    s = lax.broadcasted_iota(jnp.int32, (c, c), 1)
    rcol = lax.broadcasted_iota(jnp.int32, (c, 1), 0)
    same_block, upper = [], []
    for m in HGRN_LEVELS:
        same_block.append((r & ~(2 * m - 1)) == (s & ~(2 * m - 1)))
        upper.append((rcol & (2 * m - 1)) >= m)
    cum_mat = jnp.where(s <= r, 1.0, 0.0).astype(BF16)
    rev_mat = jnp.where(s >= r, 1.0, 0.0).astype(BF16)
    return cum_mat, rev_mat, r == s, same_block, upper, rcol & 3


def _hgrn_level_decay(g, b, m, pos4):
    c = HGRN_CHUNK
    if m == 1:
        return jnp.exp(jnp.where((pos4 & 1) == 1, g, 0.0))
    if m == 2:
        after, before = pltpu.roll(g, c - 1, 0), pltpu.roll(g, 1, 0)
        return jnp.exp(jnp.where(pos4 == 0, after, jnp.where(pos4 == 1, 0.0, jnp.where(pos4 == 2, g, g + before))))
    b3 = b.reshape(c // (2 * m), 2 * m, HGRN_DIM)
    bref = jnp.broadcast_to(b3[:, m - 1:m, :], b3.shape).reshape(c, HGRN_DIM)
    return jnp.exp(-jnp.abs(b - bref))


def _split3(x):
    hi = _bf(x)
    r1 = x - hi.astype(F32)
    mid = _bf(r1)
    lo = _bf(r1 - mid.astype(F32))
    return jnp.concatenate([hi, mid, lo], axis=1)


def _dot_hilo(a, b):
    r, c = a.shape[0], b.shape[1]
    a_hi, b_hi = _bf(a), _bf(b)
    a2 = jnp.concatenate([a_hi, _bf(a - a_hi.astype(F32))], axis=0)
    b2 = jnp.concatenate([b_hi, _bf(b - b_hi.astype(F32))], axis=1)
    y = _dot(a2, b2)
    return y[:r, :c] + y[:r, c:] + y[r:, :c]


def _fold3(y):
    w = y.shape[1] // 3
    return y[:, :w] + y[:, w:2 * w] + y[:, 2 * w:]


def _hgrn_gates(qr, fr, lb):
    sq = _sigmoid(qr)
    q = qr * sq * (HGRN_DIM ** -0.5)
    sf = _sigmoid(fr)
    f = lb + (1.0 - lb) * sf
    k = (1.0 - lb) * _sigmoid(-fr)
    return q, sq, sf, f, k, jnp.log(f)


def _hgrn_intra(q, k, g, b, consts):
    _, _, eye, same_block, upper, pos4 = consts
    heads = range(len(q))
    a = [jnp.where(eye, _dot(_bf(q[hh]), _bf(k[hh]), NT), 0.0) for hh in heads]
    saved = [[] for _ in heads]
    for i, m in enumerate(HGRN_LEVELS):
        up = upper[i]
        e = [_hgrn_level_decay(g[hh], b[hh], m, pos4) for hh in heads]
        qt = [jnp.where(up, q[hh] * e[hh], 0.0) for hh in heads]
        kt = [jnp.where(up, 0.0, k[hh] * e[hh]) for hh in heads]
        p = [_dot(_bf(qt[hh]), _bf(kt[hh]), NT) for hh in heads]
        for hh in heads:
            a[hh] = a[hh] + jnp.where(same_block[i], p[hh], 0.0)
            saved[hh].append((e[hh], qt[hh], kt[hh]))
    return a, saved


def _hgrn_specs(tb, nb, rev):
    tmap = (lambda t: nb - 1 - t) if rev else (lambda t: t)
    w = HGRN_PAIR * HGRN_DIM
    zcol = lambda base: pl.BlockSpec((tb, w), lambda h, t: (tmap(t), base // HGRN_PAIR + h))
    return zcol, [zcol(6), zcol(10), zcol(14), zcol(18),
                  pl.BlockSpec((1, w), lambda h, t: (0, h)),
                  pl.BlockSpec((1, HGRN_DIM), lambda h, t: (0, 0))]


def _hgrn_fwd(z, lb, onw, name, exchange=None):
    T = z.shape[0]
    tb = min(HGRN_ROWS, T)
    nb, c, nc = T // tb, HGRN_CHUNK, min(HGRN_ROWS, T) // HGRN_CHUNK

    def body(qr_ref, fr_ref, ir_ref, gr_ref, lb_ref, onw_ref, rec_ref, o_ref, st_ref, state):
        @pl.when(pl.program_id(1) == 0)
        def _():
            state[...] = jnp.zeros_like(state)

        consts = _hgrn_consts()
        lbv = lb_ref[...]
        onwv = onw_ref[...]

        def chunk(ci, carry):
            sl = pl.ds(pl.multiple_of(ci * c, c), c)
            heads = range(HGRN_PAIR)
            lss = [slice(HGRN_DIM * hh, HGRN_DIM * (hh + 1)) for hh in heads]
            gates = [_hgrn_gates(qr_ref[sl, ls], fr_ref[sl, ls], lbv[:, ls]) for ls in lss]
            q, k, g = [t[0] for t in gates], [t[4] for t in gates], [t[5] for t in gates]
            v = [_bf(ir_ref[sl, ls]) for ls in lss]
            b = [_fold3(_dot(consts[0], _split3(g[hh]))) for hh in heads]
            a, _ = _hgrn_intra(q, k, g, b, consts)
            st = [state[hh] for hh in heads]
            for hh in heads:
                st_ref[hh, ci] = st[hh]
            bl = [b[hh][c - 1:c, :] for hh in heads]
            o_state = [_dot(_bf(q[hh] * jnp.exp(b[hh])), _bf(st[hh]), NT) for hh in heads]
            kv = [_dot(v[hh], _bf(k[hh] * jnp.exp(bl[hh] - b[hh])), TN) for hh in heads]
            o = [_dot(_bf(a[hh]), v[hh]) + o_state[hh] for hh in heads]
            for hh in heads:
                state[hh] = st[hh] * jnp.exp(bl[hh]) + kv[hh]
                o_ref[sl, lss[hh]] = o[hh]
                oh, _ = _rms(o[hh])
                gr = gr_ref[sl, lss[hh]]
                rec_ref[sl, lss[hh]] = (oh * onwv * (gr * _sigmoid(gr))).astype(rec_ref.dtype)
            return carry

        lax.fori_loop(0, nc, chunk, 0)

    _, in_specs = _hgrn_specs(tb, nb, False)
    out_blk = pl.BlockSpec((tb, HGRN_PAIR * HGRN_DIM), lambda h, t: (t, h))
    return _hosted_call(
        body, name=name, grid=(HGRN_HEADS // HGRN_PAIR, nb), in_specs=in_specs,
        out_specs=[out_blk, out_blk, pl.BlockSpec((HGRN_PAIR, nc, HGRN_DIM, HGRN_DIM), lambda h, t: (h, t, 0, 0))],
        out_shape=[jax.ShapeDtypeStruct((T, 512), BF16), jax.ShapeDtypeStruct((T, 512), F32),
                   jax.ShapeDtypeStruct((HGRN_HEADS, T // c, HGRN_DIM, HGRN_DIM), F32)],
        scratch=[pltpu.VMEM((HGRN_PAIR, HGRN_DIM, HGRN_DIM), F32)], args=(z, z, z, z, lb, onw),
        semantics=("parallel", "arbitrary"), exchange=exchange)


def _hgrn_bwd(z, lb, onw, o, states, dcat, name, exchange=None):
    T = z.shape[0]
    tb = min(HGRN_ROWS, T)
    nb, c, nc = T // tb, HGRN_CHUNK, min(HGRN_ROWS, T) // HGRN_CHUNK

    def body(qr_ref, fr_ref, ir_ref, gr_ref, lb_ref, onw_ref, o_ref, st_ref, drec_ref,
             dqr_ref, dfr_ref, dir_ref, dgr_ref, dlb_ref, donw_ref, dstate):
        @pl.when(pl.program_id(1) == 0)
        def _():
            dstate[...] = jnp.zeros_like(dstate)
            dlb_ref[...] = jnp.zeros_like(dlb_ref)

        @pl.when((pl.program_id(0) == 0) & (pl.program_id(1) == 0))
        def _():
            donw_ref[...] = jnp.zeros_like(donw_ref)

        consts = _hgrn_consts()
        rev_mat, eye, same_block, upper = consts[1:5]
        lbv = lb_ref[...]
        onwv = onw_ref[...]
        last = lax.broadcasted_iota(jnp.int32, (c, 1), 0) == c - 1

        def chunk(i, carry):
            ci = nc - 1 - i
            sl = pl.ds(pl.multiple_of(ci * c, c), c)
            hs = range(HGRN_PAIR)
            lss = [slice(HGRN_DIM * hh, HGRN_DIM * (hh + 1)) for hh in hs]
            qr = [qr_ref[sl, ls] for ls in lss]
            gates = [_hgrn_gates(qr[hh], fr_ref[sl, lss[hh]], lbv[:, lss[hh]]) for hh in hs]
            q, sq, sf, f, k, g = ([t[j] for t in gates] for j in range(6))
            v = [_bf(ir_ref[sl, ls]) for ls in lss]
            b = [_fold3(_dot(consts[0], _split3(g[hh]))) for hh in hs]
            a, saved = _hgrn_intra(q, k, g, b, consts)
            st = [st_ref[hh, ci] for hh in hs]
            dst = [dstate[hh] for hh in hs]

            gr = [gr_ref[sl, ls] for ls in lss]
            sg = [_sigmoid(gr[hh]) for hh in hs]
            norm = [_rms(o_ref[sl, ls]) for ls in lss]
            oh, r = [t[0] for t in norm], [t[1] for t in norm]
            drec = [drec_ref[sl, ls].astype(F32) for ls in lss]
            don = [drec[hh] * (gr[hh] * sg[hh]) for hh in hs]
            do = [_bf(_rms_bwd(don[hh] * onwv, oh[hh], r[hh])) for hh in hs]
            donw = jnp.sum(don[0] * oh[0], axis=0, keepdims=True)
            for hh in hs:
                dgr_ref[sl, lss[hh]] = (drec[hh] * oh[hh] * onwv
                                        * (sg[hh] * (1.0 + gr[hh] * (1.0 - sg[hh])))).astype(dgr_ref.dtype)
                if hh:
                    donw = donw + jnp.sum(don[hh] * oh[hh], axis=0, keepdims=True)
            donw_ref[...] += donw

            eb = [jnp.exp(b[hh]) for hh in hs]
            bl = [b[hh][c - 1:c, :] for hh in hs]
            ebl = [jnp.exp(bl[hh]) for hh in hs]
            ekb = [jnp.exp(bl[hh] - b[hh]) for hh in hs]
            qe = [q[hh] * eb[hh] for hh in hs]
            ke = [k[hh] * ekb[hh] for hh in hs]
            da = [_dot(do[hh], v[hh], NT) for hh in hs]
            dat = [_dot(v[hh], do[hh], NT) for hh in hs]
            dqe = [_dot(do[hh], _bf(st[hh])) for hh in hs]
            dke = [_dot(v[hh], _bf(dst[hh])) for hh in hs]
            dv_a = [_dot(_bf(a[hh]), do[hh], TN) for hh in hs]
            dv_s = [_dot(_bf(ke[hh]), _bf(dst[hh]), NT) for hh in hs]
            dst_in = [_dot(do[hh], _bf(qe[hh]), TN) for hh in hs]
            dad = [jnp.sum(jnp.where(eye, da[hh], 0.0), axis=1, keepdims=True) for hh in hs]
            dq = [dqe[hh] * eb[hh] + dad[hh] * k[hh] for hh in hs]
            dk = [dke[hh] * ekb[hh] + dad[hh] * q[hh] for hh in hs]
            db_last = [jnp.sum(dke[hh] * ke[hh], axis=0, keepdims=True)
                       + jnp.sum(dst[hh] * st[hh], axis=0, keepdims=True) * ebl[hh] for hh in hs]
            for hh in hs:
                dstate[hh] = dst[hh] * ebl[hh] + dst_in[hh]
                dir_ref[sl, lss[hh]] = (dv_a[hh] + dv_s[hh]).astype(dir_ref.dtype)
            for lvl in range(len(HGRN_LEVELS)):
                xq = [_dot_hilo(jnp.where(same_block[lvl], da[hh], 0.0), saved[hh][lvl][2]) for hh in hs]
                xk = [_dot_hilo(jnp.where(same_block[lvl], dat[hh], 0.0), saved[hh][lvl][1]) for hh in hs]
                for hh in hs:
                    e = saved[hh][lvl][0]
                    dq[hh] = dq[hh] + jnp.where(upper[lvl], xq[hh] * e, 0.0)
                    dk[hh] = dk[hh] + jnp.where(upper[lvl], 0.0, xk[hh] * e)
            db = [q[hh] * dq[hh] - k[hh] * dk[hh] + jnp.where(last, db_last[hh], 0.0) for hh in hs]
            dg = [_fold3(_dot(rev_mat, _split3(db[hh]))) for hh in hs]

            for hh in hs:
                ls = lss[hh]
                dqr_ref[sl, ls] = (dq[hh] * (HGRN_DIM ** -0.5)
                                   * (sq[hh] * (1.0 + qr[hh] * (1.0 - sq[hh])))).astype(dqr_ref.dtype)
                dfk = dg[hh] / f[hh] - dk[hh]
                dfr_ref[sl, ls] = ((1.0 - lbv[:, ls]) * sf[hh] * (1.0 - sf[hh]) * dfk).astype(dfr_ref.dtype)
                dlb_ref[:, ls] += jnp.sum((1.0 - sf[hh]) * dfk, axis=0, keepdims=True)
            return carry

        lax.fori_loop(0, nc, chunk, 0)

    zcol, in_specs = _hgrn_specs(tb, nb, True)
    rblk = pl.BlockSpec((tb, HGRN_PAIR * HGRN_DIM), lambda h, t: (nb - 1 - t, h))
    in_specs = in_specs + [
        rblk,
        pl.BlockSpec((HGRN_PAIR, nc, HGRN_DIM, HGRN_DIM), lambda h, t: (h, nb - 1 - t, 0, 0)),
        pl.BlockSpec((tb, HGRN_PAIR * HGRN_DIM), lambda h, t: (nb - 1 - t, 4 // HGRN_PAIR + h)),
    ]
    return _hosted_call(
        body, name=name, grid=(HGRN_HEADS // HGRN_PAIR, nb), in_specs=in_specs,
        out_specs=[rblk, rblk, rblk, rblk, pl.BlockSpec((1, HGRN_PAIR * HGRN_DIM), lambda h, t: (0, h)),
                   pl.BlockSpec((1, HGRN_DIM), lambda h, t: (0, 0))],
        out_shape=[jax.ShapeDtypeStruct((T, 512), BF16)] * 4
        + [jax.ShapeDtypeStruct((1, 512), F32), jax.ShapeDtypeStruct((1, HGRN_DIM), F32)],
        scratch=[pltpu.VMEM((HGRN_PAIR, HGRN_DIM, HGRN_DIM), F32)], args=(z, z, z, z, lb, onw, o, states, dcat),
        semantics=("arbitrary", "arbitrary"), exchange=exchange)


def _lower_bound(logits, name):
    def body(l_ref, lb_ref):
        l0, l1 = l_ref[0:1, :], l_ref[1:2, :]
        m = jnp.maximum(l0, l1)
        e0, e1 = jnp.exp(l0 - m), jnp.exp(l1 - m)
        lb_ref[...] = e0 / (e0 + e1)

    return pl.pallas_call(
        body, name=name, out_shape=jax.ShapeDtypeStruct((1, logits.shape[1]), F32),
    )(logits)


def _lower_bound_bwd(lb, dlb, name):
    def body(lb_ref, dlb_ref, dl_ref):
        p = lb_ref[...]
        d0 = dlb_ref[...] * p * (1.0 - p)
        dl_ref[0:1, :] = d0
        dl_ref[1:2, :] = -d0

    return pl.pallas_call(
        body, name=name, out_shape=jax.ShapeDtypeStruct((2, lb.shape[1]), F32),
    )(lb, dlb)


CA_ROWS = 512


def _ca_fwd(q, k, v, name):
    T, W = q.shape
    M = k.shape[0]
    tq = min(CA_ROWS, T)
    scale = CA_HEAD_DIM ** -0.5

    def body(q_ref, k_ref, v_ref, o_ref):
        for h in range(CA_HEADS):
            hs = slice(CA_HEAD_DIM * h, CA_HEAD_DIM * (h + 1))
            s = _dot(q_ref[:, hs], k_ref[:, hs], NT) * scale
            p = jnp.exp(s - jnp.max(s, axis=-1, keepdims=True))
            p = p / jnp.sum(p, axis=-1, keepdims=True)
            o_ref[:, hs] = _dot(_bf(p), v_ref[:, hs]).astype(o_ref.dtype)

    full = pl.BlockSpec((M, W), lambda i: (0, 0))
    return pl.pallas_call(
        body, name=name, grid=(T // tq,), in_specs=[_row_spec(tq, W), full, full], out_specs=_row_spec(tq, W),
        out_shape=jax.ShapeDtypeStruct((T, W), BF16), compiler_params=_params("parallel"),
    )(q, k, v)


def _ca_bwd(q, k, v, do, name):
    T, W = q.shape
    M = k.shape[0]
    tq = min(CA_ROWS, T)
    scale = CA_HEAD_DIM ** -0.5

    def body(q_ref, k_ref, v_ref, do_ref, dq_ref, dk_ref, dv_ref):
        @pl.when(pl.program_id(0) == 0)
        def _():
            dk_ref[...] = jnp.zeros_like(dk_ref)
            dv_ref[...] = jnp.zeros_like(dv_ref)

        for h in range(CA_HEADS):
            hs = slice(CA_HEAD_DIM * h, CA_HEAD_DIM * (h + 1))
            qh, kh, vh, doh = q_ref[:, hs], k_ref[:, hs], v_ref[:, hs], do_ref[:, hs]
            s = _dot(qh, kh, NT) * scale
            p = jnp.exp(s - jnp.max(s, axis=-1, keepdims=True))
            p = p / jnp.sum(p, axis=-1, keepdims=True)
            dp = _dot(doh, vh, NT)
            ds = _bf(p * (dp - jnp.sum(p * dp, axis=-1, keepdims=True)) * scale)
            dq_ref[:, hs] = _dot(ds, kh).astype(dq_ref.dtype)
            dk_ref[:, hs] += _dot(ds, qh, TN)
            dv_ref[:, hs] += _dot(_bf(p), doh, TN)

    full = pl.BlockSpec((M, W), lambda i: (0, 0))
    return pl.pallas_call(
        body, name=name, grid=(T // tq,), in_specs=[_row_spec(tq, W), full, full, _row_spec(tq, W)],
        out_specs=[_row_spec(tq, W), full, full],
        out_shape=[jax.ShapeDtypeStruct((T, W), BF16), jax.ShapeDtypeStruct((M, W), F32), jax.ShapeDtypeStruct((M, W), F32)],
        compiler_params=_params("arbitrary"),
    )(q, k, v, do)


FFN_ROWS = 256
FFN_COLS = 1408
GELU_C0 = 0.7978845608028654
GELU_C1 = 0.044715


def _gelu(x):
    t = jnp.tanh(GELU_C0 * (x + GELU_C1 * x * x * x))
    return 0.5 * x * (1.0 + t), t


def _gelu_grad(x, t):
    return 0.5 * (1.0 + t) + 0.5 * x * (1.0 - t * t) * GELU_C0 * (1.0 + 3.0 * GELU_C1 * x * x)


def _shift_down(cur, halo, first, tb):
    row = lax.broadcasted_iota(jnp.int32, (tb, 1), 0)
    h6 = jnp.where(first, 0.0, halo[6:7])
    h7 = jnp.where(first, 0.0, halo[7:8])
    u1 = jnp.where(row == 0, h7, pltpu.roll(cur, 1, 0))
    u2 = jnp.where(row == 0, h6, jnp.where(row == 1, h7, pltpu.roll(cur, 2, 0)))
    return u1, u2


def _conv(u_ref, halo_ref, w_ref, b_ref, half, first, tb):
    cur = u_ref[half]
    u1, u2 = _shift_down(cur, halo_ref[half], first, tb)
    w = w_ref[...]
    return w[0:1] * u2 + w[1:2] * u1 + w[2:3] * cur + b_ref[...], cur, u1, u2


def _ffn_specs(tb, tc, rows_first):
    nj = D_FF // tc
    rc = (lambda a, b: (a, b)) if rows_first else (lambda a, b: (b, a))
    def at(f):
        return lambda a, b: f(*rc(a, b))
    blk = pl.BlockSpec((2, tb, tc), at(lambda t, j: (0, t, j)))
    halo = pl.BlockSpec((2, 8, tc), at(lambda t, j: (0, jnp.maximum(t * (tb // 8) - 1, 0), j)))
    wg = pl.BlockSpec((3, tc), at(lambda t, j: (0, j)))
    wv = pl.BlockSpec((3, tc), at(lambda t, j: (0, j + nj)))
    bg = pl.BlockSpec((1, tc), at(lambda t, j: (0, j)))
    bv = pl.BlockSpec((1, tc), at(lambda t, j: (0, j + nj)))
    flat = pl.BlockSpec((tb, tc), at(lambda t, j: (t, j)))
    return blk, halo, wg, wv, bg, bv, flat


def _glu_fwd(u, cw, cb, name):
    T = u.shape[1]
    tb, tc = min(FFN_ROWS, T), FFN_COLS

    def body(u_ref, halo_ref, wg_ref, wv_ref, bg_ref, bv_ref, a_ref):
        first = pl.program_id(0) == 0
        cg = _conv(u_ref, halo_ref, wg_ref, bg_ref, 0, first, tb)[0]
        cv = _conv(u_ref, halo_ref, wv_ref, bv_ref, 1, first, tb)[0]
        a_ref[...] = (_gelu(cg)[0] * cv).astype(a_ref.dtype)

    blk, halo, wg, wv, bg, bv, flat = _ffn_specs(tb, tc, True)
    return pl.pallas_call(
        body, name=name, grid=(T // tb, D_FF // tc), in_specs=[blk, halo, wg, wv, bg, bv], out_specs=flat,
        out_shape=jax.ShapeDtypeStruct((T, D_FF), BF16), compiler_params=_params("parallel", "parallel"),
    )(u, u, cw, cw, cb, cb)


def _glu_bwd(u, cw, cb, da, name, exchange=None):
    T = u.shape[1]
    tb, tc = min(FFN_ROWS, T), FFN_COLS

    def body(u_ref, halo_ref, wg_ref, wv_ref, bg_ref, bv_ref, da_ref, dc_ref, db_ref, dw_ref):
        first = pl.program_id(1) == 0

        @pl.when(first)
        def _():
            db_ref[...] = jnp.zeros_like(db_ref)
            dw_ref[...] = jnp.zeros_like(dw_ref)

        cg, ug, ug1, ug2 = _conv(u_ref, halo_ref, wg_ref, bg_ref, 0, first, tb)
        cv, uv, uv1, uv2 = _conv(u_ref, halo_ref, wv_ref, bv_ref, 1, first, tb)
        da = da_ref[...]
        gl, t = _gelu(cg)
        dcg = da * cv * _gelu_grad(cg, t)
        dcv = da * gl
        dc_ref[0] = dcg
        dc_ref[1] = dcv
        for half, dc, taps in ((0, dcg, (ug2, ug1, ug)), (1, dcv, (uv2, uv1, uv))):
            db_ref[half] += jnp.sum(dc, axis=0, keepdims=True)
            for tap in range(3):
                dw_ref[half, tap:tap + 1, :] += jnp.sum(dc * taps[tap], axis=0, keepdims=True)

    blk, halo, wg, wv, bg, bv, flat = _ffn_specs(tb, tc, False)
    return _hosted_call(
        body, name=name, grid=(D_FF // tc, T // tb), in_specs=[blk, halo, wg, wv, bg, bv, flat],
        out_specs=[blk, pl.BlockSpec((2, 1, tc), lambda j, t: (0, 0, j)), pl.BlockSpec((2, 3, tc), lambda j, t: (0, 0, j))],
        out_shape=[jax.ShapeDtypeStruct((2, T, D_FF), F32), jax.ShapeDtypeStruct((2, 1, D_FF), F32),
                   jax.ShapeDtypeStruct((2, 3, D_FF), F32)],
        scratch=[], args=(u, u, cw, cw, cb, cb, da), semantics=("parallel", "arbitrary"), exchange=exchange)


def _conv_bwd(dc, cw, name):
    T = dc.shape[1]
    tb, tc = min(FFN_ROWS, T), FFN_COLS
    nt, nj = T // tb, D_FF // tc

    def body(dc_ref, halo_ref, wg_ref, wv_ref, du_ref):
        last = pl.program_id(0) == nt - 1
        row = lax.broadcasted_iota(jnp.int32, (tb, 1), 0)
        for half, w_ref in ((0, wg_ref), (1, wv_ref)):
            cur = dc_ref[half]
            halo = halo_ref[half]
            h0 = jnp.where(last, 0.0, halo[0:1])
            h1 = jnp.where(last, 0.0, halo[1:2])
            d1 = jnp.where(row == tb - 1, h0, pltpu.roll(cur, tb - 1, 0))
            d2 = jnp.where(row == tb - 1, h1, jnp.where(row == tb - 2, h0, pltpu.roll(cur, tb - 2, 0)))
            w = w_ref[...]
            du_ref[half] = (w[2:3] * cur + w[1:2] * d1 + w[0:1] * d2).astype(du_ref.dtype)

    blk = pl.BlockSpec((2, tb, tc), lambda t, j: (0, t, j))
    halo = pl.BlockSpec((2, 8, tc), lambda t, j: (0, jnp.minimum((t + 1) * (tb // 8), T // 8 - 1), j))
    wg = pl.BlockSpec((3, tc), lambda t, j: (0, j))
    wv = pl.BlockSpec((3, tc), lambda t, j: (0, j + nj))
    return pl.pallas_call(
        body, name=name, grid=(nt, nj), in_specs=[blk, halo, wg, wv], out_specs=blk,
        out_shape=jax.ShapeDtypeStruct((2, T, D_FF), BF16), compiler_params=_params("parallel", "parallel"),
    )(dc, dc, cw, cw)


def _mesh_pos():
    return lax.axis_index("x"), lax.axis_index("y"), lax.axis_index("c")


def _peer(pos, k):
    return (pos[0] ^ ((k >> 2) & 1), pos[1] ^ ((k >> 1) & 1), pos[2] ^ (k & 1))


def _index(pos):
    return 4 * pos[0] + 2 * pos[1] + pos[2]


class _Exchange:
    def __init__(self, kind, buf):
        assert kind in ("gather", "scatter")
        self.kind, self.buf = kind, buf
        self.out_shape = jax.ShapeDtypeStruct(((N_DEV,) + buf.shape) if kind == "gather" else buf.shape, buf.dtype)
        self.spec = pl.BlockSpec(memory_space=pl.ANY)
        self.scratch = [pltpu.SemaphoreType.DMA((N_DEV - 1,)), pltpu.SemaphoreType.DMA((N_DEV - 1,)),
                        pltpu.SemaphoreType.DMA]

    def _src(self, x_ref, dest):
        return x_ref if self.kind == "gather" else x_ref.at[dest]

    def _copies(self, x_ref, out_ref, send_sems, recv_sems, local_sem):
        pos = _mesh_pos()
        me = _index(pos)
        local = pltpu.make_async_copy(self._src(x_ref, me), out_ref.at[me], local_sem)
        sends, recvs = [], []
        for k in range(1, N_DEV):
            peer = _peer(pos, k)
            sends.append(pltpu.make_async_remote_copy(
                src_ref=self._src(x_ref, _index(peer)), dst_ref=out_ref.at[me], send_sem=send_sems.at[k - 1],
                recv_sem=recv_sems.at[k - 1], device_id=peer, device_id_type=pl.DeviceIdType.MESH))
            recvs.append(pltpu.make_async_remote_copy(
                src_ref=self._src(x_ref, me), dst_ref=out_ref.at[_index(peer)], send_sem=send_sems.at[k - 1],
                recv_sem=recv_sems.at[k - 1], device_id=peer, device_id_type=pl.DeviceIdType.MESH))
        return local, sends, recvs

    def start(self, *refs):
        local, sends, _ = self._copies(*refs)
        local.start()
        for cp in sends:
            cp.start()

    def finish(self, *refs):
        local, sends, recvs = self._copies(*refs)
        for cp in recvs:
            cp.wait_recv()
        for cp in sends:
            cp.wait_send()
        local.wait()


def _hosted_call(body, *, name, grid, in_specs, out_specs, out_shape, scratch, args, semantics, exchange=None):
    if exchange is None:
        return pl.pallas_call(
            body, name=name, grid=grid, in_specs=in_specs, out_specs=out_specs, out_shape=out_shape,
            scratch_shapes=scratch, compiler_params=_params(*semantics))(*args)
    n_in, n_out, n_scr = len(in_specs), len(out_specs), len(scratch)

    def hosted(*refs):
        ins, x_ref = refs[:n_in], refs[n_in]
        outs, land_ref = refs[n_in + 1:n_in + 1 + n_out], refs[n_in + 1 + n_out]
        rest = refs[n_in + n_out + 2:]
        sems = rest[n_scr:]
        ids = [pl.program_id(a) for a in range(len(grid))]
        first, last = ids[0] == 0, ids[0] == grid[0] - 1
        for a in range(1, len(grid)):
            first, last = first & (ids[a] == 0), last & (ids[a] == grid[a] - 1)

        @pl.when(first)
        def _():
            exchange.start(x_ref, land_ref, *sems)

        body(*ins, *outs, *rest[:n_scr])

        @pl.when(last)
        def _():
            exchange.finish(x_ref, land_ref, *sems)

    return pl.pallas_call(
        hosted, name=name, grid=grid, in_specs=list(in_specs) + [exchange.spec],
        out_specs=list(out_specs) + [exchange.spec], out_shape=list(out_shape) + [exchange.out_shape],
        scratch_shapes=list(scratch) + exchange.scratch, compiler_params=_params(*(["arbitrary"] * len(grid))),
    )(*args, exchange.buf)


def _exchange_alone(exchange, name):
    def body(x_ref, out_ref, send_sems, recv_sems, local_sem):
        exchange.start(x_ref, out_ref, send_sems, recv_sems, local_sem)
        exchange.finish(x_ref, out_ref, send_sems, recv_sems, local_sem)

    return pl.pallas_call(
        body, name=name, out_shape=exchange.out_shape, in_specs=[exchange.spec], out_specs=exchange.spec,
        scratch_shapes=exchange.scratch)(exchange.buf)


def _adamw(w, g, m, v):
    m = ADAM_B1 * m + (1.0 - ADAM_B1) * g
    v = ADAM_B2 * v + (1.0 - ADAM_B2) * (g * g)
    m_hat = m / (1.0 - ADAM_B1 ** ADAM_STEP)
    v_hat = v / (1.0 - ADAM_B2 ** ADAM_STEP)
    delta = -ADAM_LR * (m_hat / (jnp.sqrt(v_hat) + ADAM_EPS) + ADAM_WD * w)
    return delta, m, v


def _sum_rows(parts, r0, rows, name, wmv=None):
    C = parts.shape[2]
    tr = max(t for t in range(16, ROWS + 1, 16) if rows % t == 0 and r0 % t == 0)

    def total(p_ref):
        g = p_ref[0].astype(F32)
        for i in range(1, N_DEV):
            g = g + p_ref[i].astype(F32)
        return g

    p_spec = pl.BlockSpec((N_DEV, tr, C), lambda i: (0, r0 // tr + i, 0))
    if wmv is None:
        def body(p_ref, g_ref):
            g_ref[...] = total(p_ref)

        return pl.pallas_call(
            body, name=name, grid=(rows // tr,), in_specs=[p_spec], out_specs=_row_spec(tr, C),
            out_shape=jax.ShapeDtypeStruct((rows, C), F32), compiler_params=_params("parallel"))(parts)

    def body(p_ref, w_ref, m_ref, v_ref, g_ref, d_ref, mo_ref, vo_ref):
        g = total(p_ref)
        g_ref[0] = g
        d_ref[0], mo_ref[0], vo_ref[0] = _adamw(w_ref[0], g, m_ref[0], v_ref[0])

    blk = pl.BlockSpec((1, tr, C), lambda i: (0, i, 0))
    return pl.pallas_call(
        body, name=name, grid=(rows // tr,), in_specs=[p_spec, blk, blk, blk], out_specs=[blk] * 4,
        out_shape=[jax.ShapeDtypeStruct((1, rows, C), F32)] * 4, compiler_params=_params("parallel"))(parts, *wmv)


def _sum_parts(parts, name):
    _, R, C = parts.shape

    def body(p_ref, g_ref):
        g = p_ref[0]
        for i in range(1, N_DEV):
            g = g + p_ref[i]
        g_ref[...] = g

    return pl.pallas_call(body, name=name, out_shape=jax.ShapeDtypeStruct((R, C), F32))(parts)


def _adamw_call(w, g, m, v, name):
    _, R, C = w.shape
    tr = min(ROWS, R)

    def body(w_ref, g_ref, m_ref, v_ref, d_ref, mo_ref, vo_ref):
        d_ref[...], mo_ref[...], vo_ref[...] = _adamw(w_ref[...], g_ref[...], m_ref[...], v_ref[...])

    blk = pl.BlockSpec((1, tr, C), lambda i: (0, i, 0))
    return pl.pallas_call(
        body, name=name, grid=(R // tr,), in_specs=[blk] * 4, out_specs=[blk] * 3,
        out_shape=[jax.ShapeDtypeStruct(w.shape, F32)] * 3, compiler_params=_params("parallel"))(w, g, m, v)


NORMS = ("mix_pre_norm", "mix_post_norm", "ca_pre_norm", "mem_norm", "ca_post_norm", "ffn_pre_norm", "ffn_post_norm")
SMALL = ("mix_pre_norm", "attn_sinks", "hgrn_lb_logits", "hgrn_out_norm", "mix_post_norm", "ca_pre_norm", "mem_norm",
         "ca_post_norm", "ffn_pre_norm", "ffn_conv_w", "ffn_conv_b", "ffn_post_norm")
SMALL_ROWS = 40
ROW_LOGITS, ROW_MISC, ROW_CONV_B, ROW_CONV_W = 7, 8, 9, 15
LANE_SINKS, LANE_LOSS = 128, 256
FF_PIECES = ((0, 1024), (1024, 2048), (2048, D_FF))


def _pack_small(norm_grads, dlogits, donw, dsinks, loss, d_cb, d_cw, name):
    def body(*refs):
        norm_refs = refs[:len(NORMS)]
        dl_ref, donw_ref, dsink_ref, loss_ref, cb_ref, cw_ref, out_ref = refs[len(NORMS):]
        out_ref[...] = jnp.zeros_like(out_ref)
        for i, ref in enumerate(norm_refs):
            out_ref[i:i + 1, :] = ref[...]
        out_ref[ROW_LOGITS:ROW_LOGITS + 1, 0:512] = dl_ref[0:1, :]
        out_ref[ROW_LOGITS:ROW_LOGITS + 1, 512:1024] = dl_ref[1:2, :]
        out_ref[ROW_MISC:ROW_MISC + 1, 0:HGRN_DIM] = donw_ref[...]
        out_ref[ROW_MISC:ROW_MISC + 1, LANE_SINKS:LANE_SINKS + ATTN_Q_HEADS] = dsink_ref[...]
        out_ref[ROW_MISC:ROW_MISC + 1, LANE_LOSS:LANE_LOSS + LANE] = loss_ref[...]
        for h in range(2):
            for j, (c0, c1) in enumerate(FF_PIECES):
                r = ROW_CONV_B + 3 * h + j
                out_ref[r:r + 1, 0:c1 - c0] = cb_ref[h, :, c0:c1]
                for t in range(3):
                    r = ROW_CONV_W + 3 * (3 * h + t) + j
                    out_ref[r:r + 1, 0:c1 - c0] = cw_ref[h, t:t + 1, c0:c1]

    return pl.pallas_call(
        body, name=name, out_shape=jax.ShapeDtypeStruct((SMALL_ROWS, 1024), F32),
    )(*norm_grads, dlogits, donw, dsinks, loss, d_cb, d_cw)


def _adamw_small(total, g_conv_w, w, m, v, name):
    n = len(SMALL)

    def body(*refs):
        t_ref, gcw_ref = refs[:2]
        w_refs, m_refs, v_refs = (dict(zip(SMALL, refs[2 + n * i:2 + n * (i + 1)])) for i in range(3))
        outs = refs[2 + 3 * n:]
        loss_ref = outs[0]
        g_refs, d_refs, mo_refs, vo_refs = (dict(zip(SMALL, outs[1 + n * i:1 + n * (i + 1)])) for i in range(4))
        loss_ref[...] = t_ref[ROW_MISC:ROW_MISC + 1, LANE_LOSS:LANE_LOSS + 1]

        def step(nm, idx, g):
            g_refs[nm][idx] = g
            d_refs[nm][idx], mo_refs[nm][idx], vo_refs[nm][idx] = _adamw(w_refs[nm][idx], g, m_refs[nm][idx], v_refs[nm][idx])

        everything = (slice(None), slice(None))
        for i, nm in enumerate(NORMS):
            step(nm, everything, t_ref[i:i + 1, :])
        step("hgrn_lb_logits", (slice(0, 1), slice(None)), t_ref[ROW_LOGITS:ROW_LOGITS + 1, 0:512])
        step("hgrn_lb_logits", (slice(1, 2), slice(None)), t_ref[ROW_LOGITS:ROW_LOGITS + 1, 512:1024])
        step("hgrn_out_norm", everything, t_ref[ROW_MISC:ROW_MISC + 1, 0:HGRN_DIM])
        step("attn_sinks", everything, t_ref[ROW_MISC:ROW_MISC + 1, LANE_SINKS:LANE_SINKS + ATTN_Q_HEADS])
        for h in range(2):
            for j, (c0, c1) in enumerate(FF_PIECES):
                r = ROW_CONV_B + 3 * h + j
                step("ffn_conv_b", (slice(None), slice(D_FF * h + c0, D_FF * h + c1)), t_ref[r:r + 1, 0:c1 - c0])
        step("ffn_conv_w", (slice(None), slice(None), slice(None)), gcw_ref[...])

    shapes = [jax.ShapeDtypeStruct(w[nm].shape, F32) for nm in SMALL]
    out = pl.pallas_call(
        body, name=name, out_shape=[jax.ShapeDtypeStruct((1, 1), F32)] + shapes * 4,
    )(total, g_conv_w, *[w[nm] for nm in SMALL], *[m[nm] for nm in SMALL], *[v[nm] for nm in SMALL])
    trees = [dict(zip(SMALL, out[1 + n * i:1 + n * (i + 1)])) for i in range(4)]
    return out[0], trees


BIG = ("w_in", "w_out", "ca_wq", "ca_wk", "ca_wv", "ca_wo", "ffn_w_up", "ffn_w_down")
BIG_FULL = {"w_in": (1024, 2816), "w_out": (1024, 1024), "ca_wq": (1024, 1024), "ca_wk": (1024, 1024),
            "ca_wv": (1024, 1024), "ca_wo": (1024, 1024), "ffn_w_up": (1024, 5632), "ffn_w_down": (2816, 1024)}
G_IN, G_MID, G_UP, G_DOWN = ("w_in",), ("w_out", "ca_wq", "ca_wk", "ca_wv", "ca_wo"), ("ffn_w_up",), ("ffn_w_down",)
GROUPS = (G_IN, G_MID, G_UP, G_DOWN)
COL_SHARDED = ("w_in", "ffn_w_up")
PACK_COLS = 1024


def _big_rows(name):
    r, c = BIG_FULL[name]
    return r * c // N_DEV // PACK_COLS


def _pack_shards(w, names):
    rows = [w[n][0].T if n in COL_SHARDED else w[n][0] for n in names]
    return (rows[0] if len(rows) == 1 else jnp.concatenate(rows, axis=0)).astype(BF16)


def _unpack_gathered(gathered, names):
    out, r0 = {}, 0
    for n in names:
        rows = _big_rows(n)
        out[n] = gathered[:, r0:r0 + rows].reshape(N_DEV * rows, PACK_COLS)
        r0 += rows
    return out


def _pack_full_grads(grads, names):
    parts = [grads[n].reshape(N_DEV, _big_rows(n), PACK_COLS) for n in names]
    return parts[0] if len(parts) == 1 else jnp.concatenate(parts, axis=1)


def kernel(x, mem, mix_pre_norm, w_in, attn_sinks, hgrn_lb_logits, hgrn_out_norm, w_out, mix_post_norm, ca_pre_norm, mem_norm, ca_wq, ca_wk, ca_wv, ca_wo, ca_post_norm, ffn_pre_norm, ffn_w_up, ffn_conv_w, ffn_conv_b, ffn_w_down, ffn_post_norm, loss_target, m_mix_pre_norm, m_w_in, m_attn_sinks, m_hgrn_lb_logits, m_hgrn_out_norm, m_w_out, m_mix_post_norm, m_ca_pre_norm, m_mem_norm, m_ca_wq, m_ca_wk, m_ca_wv, m_ca_wo, m_ca_post_norm, m_ffn_pre_norm, m_ffn_w_up, m_ffn_conv_w, m_ffn_conv_b, m_ffn_w_down, m_ffn_post_norm, v_mix_pre_norm, v_w_in, v_attn_sinks, v_hgrn_lb_logits, v_hgrn_out_norm, v_w_out, v_mix_post_norm, v_ca_pre_norm, v_mem_norm, v_ca_wq, v_ca_wk, v_ca_wv, v_ca_wo, v_ca_post_norm, v_ffn_pre_norm, v_ffn_w_up, v_ffn_conv_w, v_ffn_conv_b, v_ffn_w_down, v_ffn_post_norm):
    names = ["mix_pre_norm", "w_in", "attn_sinks", "hgrn_lb_logits", "hgrn_out_norm", "w_out", "mix_post_norm",
             "ca_pre_norm", "mem_norm", "ca_wq", "ca_wk", "ca_wv", "ca_wo", "ca_post_norm", "ffn_pre_norm",
             "ffn_w_up", "ffn_conv_w", "ffn_conv_b", "ffn_w_down", "ffn_post_norm"]
    w_all = dict(zip(names, [mix_pre_norm, w_in, attn_sinks, hgrn_lb_logits, hgrn_out_norm, w_out, mix_post_norm,
                             ca_pre_norm, mem_norm, ca_wq, ca_wk, ca_wv, ca_wo, ca_post_norm, ffn_pre_norm,
                             ffn_w_up, ffn_conv_w, ffn_conv_b, ffn_w_down, ffn_post_norm]))
    m_all = dict(zip(names, [m_mix_pre_norm, m_w_in, m_attn_sinks, m_hgrn_lb_logits, m_hgrn_out_norm, m_w_out,
                             m_mix_post_norm, m_ca_pre_norm, m_mem_norm, m_ca_wq, m_ca_wk, m_ca_wv, m_ca_wo,
                             m_ca_post_norm, m_ffn_pre_norm, m_ffn_w_up, m_ffn_conv_w, m_ffn_conv_b, m_ffn_w_down,
                             m_ffn_post_norm]))
    v_all = dict(zip(names, [v_mix_pre_norm, v_w_in, v_attn_sinks, v_hgrn_lb_logits, v_hgrn_out_norm, v_w_out,
                             v_mix_post_norm, v_ca_pre_norm, v_mem_norm, v_ca_wq, v_ca_wk, v_ca_wv, v_ca_wo,
                             v_ca_post_norm, v_ffn_pre_norm, v_ffn_w_up, v_ffn_conv_w, v_ffn_conv_b, v_ffn_w_down,
                             v_ffn_post_norm]))
    dev = _index(_mesh_pos())

    w_packs = {grp: _pack_shards(w_all, grp) for grp in GROUPS}
    shard_w = D_FF * 2 // N_DEV
    conv_w_rows = _exchange_alone(_Exchange("gather", ffn_conv_w[0]), "gather_conv_w")
    conv_w_full = conv_w_rows.transpose(1, 0, 2).reshape(3, 2 * D_FF)

    received, small_pack, grad_x = _local_step(
        x[0], mem[0], loss_target[0], w_packs, conv_w_full,
        {n: w_all[n] for n in NORMS}, attn_sinks, hgrn_lb_logits, hgrn_out_norm, ffn_conv_b)

    total = _sum_parts(_exchange_alone(_Exchange("gather", small_pack), "gather_small"), "sum_small")
    cw = total[ROW_CONV_W:ROW_CONV_W + 18].reshape(2, 3, 3 * PACK_COLS)[:, :, :D_FF]
    cw = cw.transpose(1, 0, 2).reshape(3, 2 * D_FF)
    g_conv_w = lax.dynamic_slice_in_dim(cw, dev * shard_w, shard_w, axis=1)[None]
    loss, (out_g, out_d, out_m, out_v) = _adamw_small(total, g_conv_w, w_all, m_all, v_all, "adamw_small")

    for grp in GROUPS:
        r0 = 0
        for n in grp:
            rows = _big_rows(n)
            if n in COL_SHARDED:
                g = _sum_rows(received[grp], r0, rows, "sum_" + n).T[None]
                d, mo, vo = _adamw_call(w_all[n], g, m_all[n], v_all[n], "adamw_" + n)
            else:
                g, d, mo, vo = _sum_rows(received[grp], r0, rows, "adamw_" + n, wmv=(w_all[n], m_all[n], v_all[n]))
            out_g[n], out_d[n], out_m[n], out_v[n] = g, d, mo, vo
            r0 += rows

    return (loss[0, 0], grad_x[None], *[out_g[n] for n in names], *[out_d[n] for n in names],
            *[out_m[n] for n in names], *[out_v[n] for n in names])


def _local_step(x, mem, target, w_packs, conv_w, norms, sinks, lb_logits, out_norm, conv_b):
    g1, g2, g3 = norms["mix_pre_norm"], norms["mix_post_norm"], norms["ca_pre_norm"]
    g4, g5, g6, g7 = norms["mem_norm"], norms["ca_post_norm"], norms["ffn_pre_norm"], norms["ffn_post_norm"]

    h1, gathered = _norm_fwd(x, g1, "mix_norm", exchange=_Exchange("gather", w_packs[G_IN]))
    w_in_t = _unpack_gathered(gathered, G_IN)["w_in"]
    up_shard = w_packs[G_UP]
    up_half = up_shard.shape[0] // 2
    z, up_first = _mm(h1, w_in_t, mode="nt", out_dtype=F32, name="in_proj", tn=1408,
                      exchange=_Exchange("gather", up_shard[:up_half]))
    attn, lse, gathered = _swa_fwd(z, sinks, "swa_fwd", exchange=_Exchange("gather", w_packs[G_DOWN]))
    w_down = _unpack_gathered(gathered, G_DOWN)["ffn_w_down"]
    lb = _lower_bound(lb_logits, "lower_bound")
    rec, o_rec, states, gathered = _hgrn_fwd(
        z, lb, out_norm, "hgrn_fwd",
        exchange=_Exchange("gather", jnp.concatenate([w_packs[G_MID], up_shard[up_half:]], axis=0)))
    mid_rows = w_packs[G_MID].shape[0]
    wf = _unpack_gathered(gathered[:, :mid_rows], G_MID)
    w_out, wq, wk, wv, wo = (wf[n] for n in G_MID)
    w_up_t = jnp.concatenate([up_first, gathered[:, mid_rows:]], axis=1).reshape(-1, PACK_COLS)
    cat = jnp.concatenate([attn, rec], axis=1)
    mix = _mm(cat, w_out, mode="nn", out_dtype=F32, name="out_proj")
    x1, h2 = _post_pre(x, mix, g2, g3, "mix_post")
    mem_n = _norm_fwd(mem, g4, "mem_norm")
    q = _mm(h2, wq, mode="nn", out_dtype=BF16, name="ca_q")
    k = _mm(mem_n, wk, mode="nn", out_dtype=BF16, name="ca_k")
    v = _mm(mem_n, wv, mode="nn", out_dtype=BF16, name="ca_v")
    oc = _ca_fwd(q, k, v, "ca_fwd")
    c = _mm(oc, wo, mode="nn", out_dtype=F32, name="ca_o")
    x2, h3 = _post_pre(x1, c, g5, g6, "ca_post")
    u = _mm(h3, w_up_t, mode="nt", out_dtype=F32, name="ffn_up", tn=1408, split_out=True)
    a = _glu_fwd(u, conv_w, conv_b, "glu_fwd")
    y = _mm(a, w_down, mode="nn", out_dtype=F32, name="ffn_down", tk=2816)
    loss, dx3, dy, dg7 = _final(x2, y, g7, target, "loss_head")

    da = _mm(dy, w_down, mode="nt", out_dtype=F32, name="ffn_down_dx", tn=1408)
    d_w_down = _mm(a, dy, mode="tn", out_dtype=BF16, name="ffn_down_dw", tm=1408, tk=1024)
    dc, d_cb, d_cw, got_down = _glu_bwd(
        u, conv_w, conv_b, da, "glu_bwd",
        exchange=_Exchange("scatter", _pack_full_grads({"ffn_w_down": d_w_down}, G_DOWN)))
    du = _conv_bwd(dc, conv_w, "conv_bwd")
    d_w_up_t = _mm(du, h3, mode="tn", out_dtype=BF16, name="ffn_up_dw", tm=1408, tk=1024, split_a=True)
    dh3, got_up = _mm(du, w_up_t, mode="nn", out_dtype=BF16, name="ffn_up_dx", tm=2048, tk=1408, split_a=True,
                      exchange=_Exchange("scatter", _pack_full_grads({"ffn_w_up": d_w_up_t}, G_UP)))
    dx2, dcv, dg6, dg5 = _norm_bwd2(dx3, dh3, x2, g6, c, g5, "ca_post_bwd")
    doc = _mm(dcv, wo, mode="nt", out_dtype=BF16, name="ca_o_dx")
    d_wo = _mm(oc, dcv, mode="tn", out_dtype=BF16, name="ca_o_dw", tm=1024, tk=1024)
    dq, dk, dv = _ca_bwd(q, k, v, doc, "ca_bwd")
    d_wq = _mm(h2, dq, mode="tn", out_dtype=BF16, name="ca_q_dw", tm=1024, tk=1024)
    dh2 = _mm(dq, wq, mode="nt", out_dtype=BF16, name="ca_q_dx")
    d_wk = _mm(mem_n, dk, mode="tn", out_dtype=BF16, name="ca_k_dw", tm=1024)
    d_wv = _mm(mem_n, dv, mode="tn", out_dtype=BF16, name="ca_v_dw", tm=1024)
    dmem_k = _mm(dk, wk, mode="nt", out_dtype=F32, name="ca_k_dx")
    dmem_v = _mm(dv, wv, mode="nt", out_dtype=F32, name="ca_v_dx")
    dg4 = _gain_bwd(mem, dmem_k, dmem_v, "mem_norm_bwd")
    dx1, dmix, dg3, dg2 = _norm_bwd2(dx2, dh2, x1, g3, mix, g2, "mix_post_bwd")
    dcat = _mm(dmix, w_out, mode="nt", out_dtype=BF16, name="out_proj_dx")
    d_w_out = _mm(cat, dmix, mode="tn", out_dtype=BF16, name="out_proj_dw", tm=1024, tk=1024)
    mid = {"w_out": d_w_out, "ca_wq": d_wq, "ca_wk": d_wk, "ca_wv": d_wv, "ca_wo": d_wo}
    dqr, dfr, dir_, dgr, dlb, donw, got_mid = _hgrn_bwd(
        z, lb, out_norm, o_rec, states, dcat, "hgrn_bwd", exchange=_Exchange("scatter", _pack_full_grads(mid, G_MID)))
    dq_a, dka, dkb, dva, dvb, dsinks = _swa_bwd(z, sinks, dcat, lse, "swa_bwd")
    dz = _assemble_dz(dq_a, dka, dkb, dva, dvb, dqr, dfr, dir_, dgr, "assemble_dz")
    d_w_in_t = _mm(dz, h1, mode="tn", out_dtype=BF16, name="in_proj_dw", tm=1408, tk=1024)
    dh1, got_in = _mm(dz, w_in_t, mode="nn", out_dtype=BF16, name="in_proj_dx", tk=2816,
                      exchange=_Exchange("scatter", _pack_full_grads({"w_in": d_w_in_t}, G_IN)))
    dx, dg1 = _norm_bwd1(dx1, dh1, x, g1, "mix_norm_bwd")

    small_pack = _pack_small(
        (dg1, dg2, dg3, dg4, dg5, dg6, dg7), _lower_bound_bwd(lb, dlb, "lower_bound_bwd"), donw, dsinks, loss,
        d_cb, d_cw, "pack_small")
    return {G_IN: got_in, G_MID: got_mid, G_UP: got_up, G_DOWN: got_down}, small_pack, dx
```

```python
import jax
import jax.numpy as jnp
from jax import lax
from jax.experimental import pallas as pl
from jax.experimental.pallas import tpu as pltpu

F32 = jnp.float32
BF16 = jnp.bfloat16
EPS = 1e-6
N_DEV = 8
MESH_AXES = ("x", "y", "c")

ATTN_HEAD_DIM = 64
ATTN_Q_HEADS = 8
ATTN_KV_HEADS = 2
ATTN_BLOCK = 128
HGRN_HEADS = 4
HGRN_DIM = 128
HGRN_CHUNK = 64
HGRN_PAIR = 4
Z_Q, Z_F, Z_I, Z_G = 768, 1280, 1792, 2304
HGRN_LEVELS = (32, 16, 8, 4, 2, 1)
CA_HEADS = 4
CA_HEAD_DIM = 256
D_FF = 2816

ADAM_LR = 0.001
ADAM_B1 = 0.9
ADAM_B2 = 0.999
ADAM_EPS = 1e-08
ADAM_WD = 0.01
ADAM_STEP = 10

VMEM_LIMIT = 56 << 20
LANE = 128

NT = (((1,), (1,)), ((), ()))
TN = (((0,), (0,)), ((), ()))


def _params(*sem):
    return pltpu.CompilerParams(dimension_semantics=sem, vmem_limit_bytes=VMEM_LIMIT)


def _tile(n, cap):
    if n <= cap:
        return n
    best = 0
    for t in range(LANE, cap + 1, LANE):
        if n % t == 0:
            best = t
    assert best, (n, cap)
    return best


def _dot(a, b, dims=None):
    if dims is None:
        return jnp.dot(a, b, preferred_element_type=F32)
    return lax.dot_general(a, b, dims, preferred_element_type=F32)


def _bf(x):
    return x.astype(BF16)


def _sigmoid(x):
    return 1.0 / (1.0 + jnp.exp(-x))


def _rms(x):
    r = lax.rsqrt(jnp.mean(x * x, axis=-1, keepdims=True) + EPS)
    return x * r, r


def _rms_bwd(dxh, xh, r):
    return r * (dxh - xh * jnp.mean(dxh * xh, axis=-1, keepdims=True))


def _mm(a, b, *, mode, out_dtype, name, tm=1024, tn=1024, tk=1024, split_a=False, split_b=False, split_out=False,
        exchange=None):
    def dims(arr, split):
        if split:
            return arr.shape[1], 2 * arr.shape[2]
        return arr.shape

    ar, ac = dims(a, split_a)
    br, bc = dims(b, split_b)
    if mode == "nn":
        M, K, N = ar, ac, bc
        assert br == K
    elif mode == "nt":
        M, K, N = ar, ac, br
        assert bc == K
    else:
        K, M, N = ar, ac, bc
        assert br == K
    a_cols_half = ac // 2 if split_a else None
    b_cols_half = bc // 2 if split_b else None
    tm = _tile(M, tm)
    tn = _tile((N // 2) if (split_out or (split_b and mode != "nt")) else N, tn)
    tk = _tile((K // 2) if ((split_a and mode != "tn") or (split_b and mode == "nt")) else K, tk)
    if split_a and mode == "tn":
        tm = _tile(M // 2, tm)
    gm, gn, gk = M // tm, N // tn, K // tk
    a_bytes, b_bytes = a.size * a.dtype.itemsize, b.size * b.dtype.itemsize
    rows_outer = gk > 1 or a_bytes + gm * b_bytes <= gn * a_bytes + b_bytes
    grid = (gm, gn, gk) if rows_outer else (gn, gm, gk)

    def spec(split, half, blk, rc):
        def imap(p, q, k):
            r, c = rc(*((p, q) if rows_outer else (q, p)), k)
            if not split:
                return (r, c)
            per_half = half // blk[1]
            return (c // per_half, r, c % per_half)

        return pl.BlockSpec(((None,) + blk) if split else blk, imap)

    if mode == "nn":
        a_spec = spec(split_a, a_cols_half, (tm, tk), lambda i, j, k: (i, k))
        b_spec = spec(split_b, b_cols_half, (tk, tn), lambda i, j, k: (k, j))
        dn = None
    elif mode == "nt":
        a_spec = spec(split_a, a_cols_half, (tm, tk), lambda i, j, k: (i, k))
        b_spec = spec(split_b, b_cols_half, (tn, tk), lambda i, j, k: (j, k))
        dn = NT
    else:
        a_spec = spec(split_a, a_cols_half, (tk, tm), lambda i, j, k: (k, i))
        b_spec = spec(split_b, b_cols_half, (tk, tn), lambda i, j, k: (k, j))
        dn = TN
    o_spec = spec(split_out, N // 2 if split_out else None, (tm, tn), lambda i, j, k: (i, j))
    out_shape = (2, M, N // 2) if split_out else (M, N)

    if gk == 1:
        def body(a_ref, b_ref, o_ref):
            o_ref[...] = _dot(_bf(a_ref[...]), _bf(b_ref[...]), dn).astype(o_ref.dtype)
        scratch = []
    else:
        def body(a_ref, b_ref, o_ref, acc_ref):
            k = pl.program_id(2)

            @pl.when(k == 0)
            def _():
                acc_ref[...] = jnp.zeros_like(acc_ref)

            acc_ref[...] += _dot(_bf(a_ref[...]), _bf(b_ref[...]), dn)

            @pl.when(k == gk - 1)
            def _():
                o_ref[...] = acc_ref[...].astype(o_ref.dtype)
        scratch = [pltpu.VMEM((tm, tn), F32)]

    out = _hosted_call(
        body, name=name, grid=grid, in_specs=[a_spec, b_spec], out_specs=[o_spec],
        out_shape=[jax.ShapeDtypeStruct(out_shape, out_dtype)], scratch=scratch, args=(a, b),
        semantics=("parallel", "parallel", "arbitrary"), exchange=exchange)
    return out[0] if exchange is None else out


ROWS = 512


def _row_spec(tr, cols):
    return pl.BlockSpec((tr, cols), lambda i: (i, 0))


def _vec_spec(cols):
    return pl.BlockSpec((1, cols), lambda i: (0, 0))


def _norm_fwd(x, g, name, exchange=None):
    T, Dm = x.shape
    tr = min(ROWS, T)

    def body(x_ref, g_ref, h_ref):
        xh, _ = _rms(x_ref[...])
        h_ref[...] = (xh * g_ref[...]).astype(h_ref.dtype)

    out = _hosted_call(
        body, name=name, grid=(T // tr,), in_specs=[_row_spec(tr, Dm), _vec_spec(Dm)], out_specs=[_row_spec(tr, Dm)],
        out_shape=[jax.ShapeDtypeStruct((T, Dm), BF16)], scratch=[], args=(x, g), semantics=("parallel",),
        exchange=exchange)
    return out[0] if exchange is None else out


def _post_pre(x, m, g_post, g_pre, name, exchange=None):
    T, Dm = x.shape
    tr = min(ROWS, T)

    def body(x_ref, m_ref, gp_ref, gn_ref, xo_ref, h_ref):
        mh, _ = _rms(m_ref[...])
        xn = x_ref[...] + mh * gp_ref[...]
        xo_ref[...] = xn
        xh, _ = _rms(xn)
        h_ref[...] = (xh * gn_ref[...]).astype(h_ref.dtype)

    return _hosted_call(
        body, name=name, grid=(T // tr,),
        in_specs=[_row_spec(tr, Dm), _row_spec(tr, Dm), _vec_spec(Dm), _vec_spec(Dm)],
        out_specs=[_row_spec(tr, Dm), _row_spec(tr, Dm)],
        out_shape=[jax.ShapeDtypeStruct((T, Dm), F32), jax.ShapeDtypeStruct((T, Dm), BF16)],
        scratch=[], args=(x, m, g_post, g_pre), semantics=("parallel",), exchange=exchange)


def _final(x2, y, g_post, target, name):
    T, Dm = x2.shape
    tr = min(ROWS, T)

    def body(x_ref, y_ref, g_ref, t_ref, loss_ref, dx_ref, dy_ref, dg_ref):
        @pl.when(pl.program_id(0) == 0)
        def _():
            loss_ref[...] = jnp.zeros_like(loss_ref)
            dg_ref[...] = jnp.zeros_like(dg_ref)

        g = g_ref[...]
        yh, r = _rms(y_ref[...])
        d = x_ref[...] + yh * g - t_ref[...]
        loss_ref[...] += jnp.zeros((1, LANE), F32) + 0.5 * jnp.sum(jnp.mean(d * d, axis=-1, keepdims=True))
        dx = d * (1.0 / Dm)
        dx_ref[...] = dx
        dy_ref[...] = _rms_bwd(dx * g, yh, r).astype(dy_ref.dtype)
        dg_ref[...] += jnp.sum(dx * yh, axis=0, keepdims=True)

    return pl.pallas_call(
        body, name=name, grid=(T // tr,),
        in_specs=[_row_spec(tr, Dm), _row_spec(tr, Dm), _vec_spec(Dm), _row_spec(tr, Dm)],
        out_specs=[_vec_spec(LANE), _row_spec(tr, Dm), _row_spec(tr, Dm), _vec_spec(Dm)],
        out_shape=[jax.ShapeDtypeStruct((1, LANE), F32), jax.ShapeDtypeStruct((T, Dm), F32),
                   jax.ShapeDtypeStruct((T, Dm), BF16), jax.ShapeDtypeStruct((1, Dm), F32)],
        compiler_params=_params("arbitrary"),
    )(x2, y, g_post, target)


def _norm_bwd2(dx_cur, dh, x_prev, g_pre, m_prev, g_post, name):
    T, Dm = x_prev.shape
    tr = min(ROWS, T)

    def body(dx_ref, dh_ref, x_ref, gn_ref, m_ref, gp_ref, dxo_ref, dm_ref, dgn_ref, dgp_ref):
        @pl.when(pl.program_id(0) == 0)
        def _():
            dgn_ref[...] = jnp.zeros_like(dgn_ref)
            dgp_ref[...] = jnp.zeros_like(dgp_ref)

        dh = dh_ref[...].astype(F32)
        xh, r = _rms(x_ref[...])
        dx = dx_ref[...] + _rms_bwd(dh * gn_ref[...], xh, r)
        dxo_ref[...] = dx
        dgn_ref[...] += jnp.sum(dh * xh, axis=0, keepdims=True)
        mh, rm = _rms(m_ref[...])
        dm_ref[...] = _rms_bwd(dx * gp_ref[...], mh, rm).astype(dm_ref.dtype)
        dgp_ref[...] += jnp.sum(dx * mh, axis=0, keepdims=True)

    return pl.pallas_call(
        body, name=name, grid=(T // tr,),
        in_specs=[_row_spec(tr, Dm), _row_spec(tr, Dm), _row_spec(tr, Dm), _vec_spec(Dm), _row_spec(tr, Dm), _vec_spec(Dm)],
        out_specs=[_row_spec(tr, Dm), _row_spec(tr, Dm), _vec_spec(Dm), _vec_spec(Dm)],
        out_shape=[jax.ShapeDtypeStruct((T, Dm), F32), jax.ShapeDtypeStruct((T, Dm), BF16),
                   jax.ShapeDtypeStruct((1, Dm), F32), jax.ShapeDtypeStruct((1, Dm), F32)],
        compiler_params=_params("arbitrary"),
    )(dx_cur, dh, x_prev, g_pre, m_prev, g_post)


def _norm_bwd1(dx_cur, dh, x_prev, g_pre, name):
    T, Dm = x_prev.shape
    tr = min(ROWS, T)

    def body(dx_ref, dh_ref, x_ref, gn_ref, dxo_ref, dgn_ref):
        @pl.when(pl.program_id(0) == 0)
        def _():
            dgn_ref[...] = jnp.zeros_like(dgn_ref)

        dh = dh_ref[...].astype(F32)
        xh, r = _rms(x_ref[...])
        dxo_ref[...] = dx_ref[...] + _rms_bwd(dh * gn_ref[...], xh, r)
        dgn_ref[...] += jnp.sum(dh * xh, axis=0, keepdims=True)

    return pl.pallas_call(
        body, name=name, grid=(T // tr,),
        in_specs=[_row_spec(tr, Dm), _row_spec(tr, Dm), _row_spec(tr, Dm), _vec_spec(Dm)],
        out_specs=[_row_spec(tr, Dm), _vec_spec(Dm)],
        out_shape=[jax.ShapeDtypeStruct((T, Dm), F32), jax.ShapeDtypeStruct((1, Dm), F32)],
        compiler_params=_params("arbitrary"),
    )(dx_cur, dh, x_prev, g_pre)


def _gain_bwd(x, dh_a, dh_b, name):
    T, Dm = x.shape

    def body(x_ref, a_ref, b_ref, dg_ref):
        xh, _ = _rms(x_ref[...])
        dg_ref[...] = jnp.sum((a_ref[...] + b_ref[...]) * xh, axis=0, keepdims=True)

    return pl.pallas_call(
        body, name=name, grid=(1,), in_specs=[_row_spec(T, Dm)] * 3, out_specs=_vec_spec(Dm),
        out_shape=jax.ShapeDtypeStruct((1, Dm), F32), compiler_params=_params("arbitrary"),
    )(x, dh_a, dh_b)


ATTN_GROUP = ATTN_Q_HEADS // ATTN_KV_HEADS


def _swa_mask(n):
    rows = ATTN_GROUP * ATTN_BLOCK
    row = lax.broadcasted_iota(jnp.int32, (rows, 2 * ATTN_BLOCK), 0) & (ATTN_BLOCK - 1)
    col = lax.broadcasted_iota(jnp.int32, (rows, 2 * ATTN_BLOCK), 1)
    diff = row + ATTN_BLOCK - col
    return (diff >= 0) & (diff < ATTN_BLOCK) & ((col >= ATTN_BLOCK) | (n > 0))


def _swa_rows(ref, hk, dtype):
    hd = ATTN_HEAD_DIM
    return jnp.concatenate(
        [ref[:, hd * (hk * ATTN_GROUP + g):hd * (hk * ATTN_GROUP + g + 1)].astype(dtype) for g in range(ATTN_GROUP)],
        axis=0)


def _swa_per_row(vals):
    seg = lax.broadcasted_iota(jnp.int32, (ATTN_GROUP * ATTN_BLOCK, 1), 0) // ATTN_BLOCK
    col = jnp.zeros((ATTN_GROUP * ATTN_BLOCK, 1), F32)
    for g, val in enumerate(vals):
        col = jnp.where(seg == g, val, col)
    return col


def _swa_specs():
    blk = ATTN_BLOCK
    prev = lambda n: jnp.maximum(n - 1, 0)
    return [
        pl.BlockSpec(memory_space=pltpu.SMEM),
        pl.BlockSpec((blk, 512), lambda n: (n, 0)),
        pl.BlockSpec((blk, 128), lambda n: (prev(n), 4)),
        pl.BlockSpec((blk, 128), lambda n: (n, 4)),
        pl.BlockSpec((blk, 128), lambda n: (prev(n), 5)),
        pl.BlockSpec((blk, 128), lambda n: (n, 5)),
    ]


def _swa_fwd(z, sinks, name, exchange=None):
    T = z.shape[0]
    blk, hd = ATTN_BLOCK, ATTN_HEAD_DIM
    scale = hd ** -0.5

    def body(sink_ref, q_ref, kp_ref, kc_ref, vp_ref, vc_ref, o_ref, lse_ref):
        allowed = _swa_mask(pl.program_id(0))
        hks = range(ATTN_KV_HEADS)
        kss = [slice(hd * hk, hd * hk + hd) for hk in hks]
        k = [_bf(jnp.concatenate([kp_ref[:, ks], kc_ref[:, ks]], axis=0)) for ks in kss]
        v = [_bf(jnp.concatenate([vp_ref[:, ks], vc_ref[:, ks]], axis=0)) for ks in kss]
        s = [jnp.where(allowed, _dot(_swa_rows(q_ref, hk, BF16), k[hk], NT) * scale, -1e30) for hk in hks]
        sink = [_swa_per_row([sink_ref[0, hk * ATTN_GROUP + g] for g in range(ATTN_GROUP)]) for hk in hks]
        m = [jnp.maximum(jnp.max(s[hk], axis=-1, keepdims=True), sink[hk]) for hk in hks]
        p = [jnp.exp(s[hk] - m[hk]) for hk in hks]
        l = [jnp.sum(p[hk], axis=-1, keepdims=True) + jnp.exp(sink[hk] - m[hk]) for hk in hks]
        o = [_dot(_bf(p[hk] / l[hk]), v[hk]).astype(o_ref.dtype) for hk in hks]
        for hk in hks:
            lse = m[hk] + jnp.log(l[hk])
            for g in range(ATTN_GROUP):
                h = hk * ATTN_GROUP + g
                o_ref[:, hd * h:hd * (h + 1)] = o[hk][blk * g:blk * (g + 1)]
                lse_ref[:, h:h + 1] = lse[blk * g:blk * (g + 1)]

    return _hosted_call(
        body, name=name, grid=(T // blk,), in_specs=_swa_specs(),
        out_specs=[pl.BlockSpec((blk, 512), lambda n: (n, 0)), pl.BlockSpec((blk, ATTN_Q_HEADS), lambda n: (n, 0))],
        out_shape=[jax.ShapeDtypeStruct((T, 512), BF16), jax.ShapeDtypeStruct((T, ATTN_Q_HEADS), F32)],
        scratch=[], args=(sinks, z, z, z, z, z), semantics=("parallel",), exchange=exchange)


def _swa_bwd(z, sinks, dcat, lse, name):
    T = z.shape[0]
    blk, hd = ATTN_BLOCK, ATTN_HEAD_DIM
    scale = hd ** -0.5
    group = ATTN_Q_HEADS // ATTN_KV_HEADS

    def body(sink_ref, q_ref, kp_ref, kc_ref, vp_ref, vc_ref, do_ref, lse_ref,
             dq_ref, dka_ref, dkb_ref, dva_ref, dvb_ref, dsink_ref):
        @pl.when(pl.program_id(0) == 0)
        def _():
            dsink_ref[...] = jnp.zeros_like(dsink_ref)

        allowed = _swa_mask(pl.program_id(0))
        lane = lax.broadcasted_iota(jnp.int32, (1, ATTN_Q_HEADS), 1)
        dsink = jnp.zeros((1, ATTN_Q_HEADS), F32)
        hks = range(ATTN_KV_HEADS)
        kss = [slice(hd * hk, hd * hk + hd) for hk in hks]
        k = [_bf(jnp.concatenate([kp_ref[:, ks], kc_ref[:, ks]], axis=0)) for ks in kss]
        v = [_bf(jnp.concatenate([vp_ref[:, ks], vc_ref[:, ks]], axis=0)) for ks in kss]
        qs = [_swa_rows(q_ref, hk, BF16) for hk in hks]
        dos = [_swa_rows(do_ref, hk, BF16) for hk in hks]
        lse = [jnp.concatenate([lse_ref[:, hk * group + g:hk * group + g + 1] for g in range(group)], axis=0)
               for hk in hks]
        s = [_dot(qs[hk], k[hk], NT) * scale for hk in hks]
        dp = [_dot(dos[hk], v[hk], NT) for hk in hks]
        p = [jnp.where(allowed, jnp.exp(jnp.where(allowed, s[hk], -1e30) - lse[hk]), 0.0) for hk in hks]
        delta = [jnp.sum(p[hk] * dp[hk], axis=-1, keepdims=True) for hk in hks]
        ds = [_bf(p[hk] * (dp[hk] - delta[hk]) * scale) for hk in hks]
        dq = [_dot(ds[hk], k[hk]).astype(dq_ref.dtype) for hk in hks]
        dk = [_dot(ds[hk], qs[hk], TN) for hk in hks]
        dv = [_dot(_bf(p[hk]), dos[hk], TN) for hk in hks]
        for hk in hks:
            sink = _swa_per_row([sink_ref[0, hk * group + g] for g in range(group)])
            sink_part = jnp.exp(sink - lse[hk]) * delta[hk]
            for g in range(group):
                h = hk * group + g
                dq_ref[:, hd * h:hd * (h + 1)] = dq[hk][blk * g:blk * (g + 1)]
                dsink = dsink + jnp.where(lane == h, -jnp.sum(sink_part[blk * g:blk * (g + 1)]), 0.0)
            dkb_ref[:, kss[hk]] = dk[hk][:blk]
            dka_ref[:, kss[hk]] = dk[hk][blk:]
            dvb_ref[:, kss[hk]] = dv[hk][:blk]
            dva_ref[:, kss[hk]] = dv[hk][blk:]
        dsink_ref[...] += dsink

    kv_out = pl.BlockSpec((blk, 128), lambda n: (n, 0))
    return pl.pallas_call(
        body, name=name, grid=(T // blk,),
        in_specs=_swa_specs() + [pl.BlockSpec((blk, 512), lambda n: (n, 0)),
                                 pl.BlockSpec((blk, ATTN_Q_HEADS), lambda n: (n, 0))],
        out_specs=[pl.BlockSpec((blk, 512), lambda n: (n, 0)), kv_out, kv_out, kv_out, kv_out,
                   pl.BlockSpec((1, ATTN_Q_HEADS), lambda n: (0, 0))],
        out_shape=[jax.ShapeDtypeStruct((T, 512), BF16)] + [jax.ShapeDtypeStruct((T, 128), F32)] * 4
        + [jax.ShapeDtypeStruct((1, ATTN_Q_HEADS), F32)],
        compiler_params=_params("arbitrary"),
    )(sinks, z, z, z, z, z, dcat, lse)


def _assemble_dz(dq_a, dka, dkb, dva, dvb, dqr, dfr, dir_, dgr, name):
    T = dq_a.shape[0]
    blk = ATTN_BLOCK
    nb = T // blk

    def body(dq_ref, dka_ref, dkb_ref, dva_ref, dvb_ref, dqr_ref, dfr_ref, dir_ref, dgr_ref, o_ref):
        has_next = pl.program_id(0) < nb - 1
        o_ref[:, 0:512] = dq_ref[...]
        o_ref[:, 512:640] = (dka_ref[...] + jnp.where(has_next, dkb_ref[...], 0.0)).astype(o_ref.dtype)
        o_ref[:, 640:768] = (dva_ref[...] + jnp.where(has_next, dvb_ref[...], 0.0)).astype(o_ref.dtype)
        o_ref[:, 768:1280] = dqr_ref[...]
        o_ref[:, 1280:1792] = dfr_ref[...]
        o_ref[:, 1792:2304] = dir_ref[...]
        o_ref[:, 2304:2816] = dgr_ref[...]

    cur = lambda w: pl.BlockSpec((blk, w), lambda n: (n, 0))
    nxt = pl.BlockSpec((blk, 128), lambda n: (jnp.minimum(n + 1, nb - 1), 0))
    return pl.pallas_call(
        body, name=name, grid=(nb,),
        in_specs=[cur(512), cur(128), nxt, cur(128), nxt, cur(512), cur(512), cur(512), cur(512)],
        out_specs=pl.BlockSpec((blk, 2816), lambda n: (n, 0)),
        out_shape=jax.ShapeDtypeStruct((T, 2816), BF16), compiler_params=_params("parallel"),
    )(dq_a, dka, dkb, dva, dvb, dqr, dfr, dir_, dgr)


HGRN_ROWS = 512


def _hgrn_consts():
    c = HGRN_CHUNK
    r = lax.broadcasted_iota(jnp.int32, (c, c), 0)
    s = lax.broadcasted_iota(jnp.int32, (c, c), 1)
    rcol = lax.broadcasted_iota(jnp.int32, (c, 1), 0)
    same_block, upper = [], []
    for m in HGRN_LEVELS:
        same_block.append((r & ~(2 * m - 1)) == (s & ~(2 * m - 1)))
        upper.append((rcol & (2 * m - 1)) >= m)
    cum_mat = jnp.where(s <= r, 1.0, 0.0).astype(BF16)
    rev_mat = jnp.where(s >= r, 1.0, 0.0).astype(BF16)
    return cum_mat, rev_mat, r == s, same_block, upper, rcol & 3


def _hgrn_level_decay(g, b, m, pos4):
    c = HGRN_CHUNK
    if m == 1:
        return jnp.exp(jnp.where((pos4 & 1) == 1, g, 0.0))
    if m == 2:
        after, before = pltpu.roll(g, c - 1, 0), pltpu.roll(g, 1, 0)
        return jnp.exp(jnp.where(pos4 == 0, after, jnp.where(pos4 == 1, 0.0, jnp.where(pos4 == 2, g, g + before))))
    b3 = b.reshape(c // (2 * m), 2 * m, HGRN_DIM)
    bref = jnp.broadcast_to(b3[:, m - 1:m, :], b3.shape).reshape(c, HGRN_DIM)
    return jnp.exp(-jnp.abs(b - bref))


def _split3(x):
    hi = _bf(x)
    r1 = x - hi.astype(F32)
    mid = _bf(r1)
    lo = _bf(r1 - mid.astype(F32))
    return jnp.concatenate([hi, mid, lo], axis=1)


def _dot_hilo(a, b):
    r, c = a.shape[0], b.shape[1]
    a_hi, b_hi = _bf(a), _bf(b)
    a2 = jnp.concatenate([a_hi, _bf(a - a_hi.astype(F32))], axis=0)
    b2 = jnp.concatenate([b_hi, _bf(b - b_hi.astype(F32))], axis=1)
    y = _dot(a2, b2)
    return y[:r, :c] + y[:r, c:] + y[r:, :c]


def _fold3(y):
    w = y.shape[1] // 3
    return y[:, :w] + y[:, w:2 * w] + y[:, 2 * w:]


def _hgrn_gates(qr, fr, lb):
    sq = _sigmoid(qr)
    q = qr * sq * (HGRN_DIM ** -0.5)
    sf = _sigmoid(fr)
    f = lb + (1.0 - lb) * sf
    k = (1.0 - lb) * _sigmoid(-fr)
    return q, sq, sf, f, k, jnp.log(f)


def _hgrn_intra(q, k, g, b, consts):
    _, _, eye, same_block, upper, pos4 = consts
    heads = range(len(q))
    a = [jnp.where(eye, _dot(_bf(q[hh]), _bf(k[hh]), NT), 0.0) for hh in heads]
    saved = [[] for _ in heads]
    for i, m in enumerate(HGRN_LEVELS):
        up = upper[i]
        e = [_hgrn_level_decay(g[hh], b[hh], m, pos4) for hh in heads]
        qt = [jnp.where(up, q[hh] * e[hh], 0.0) for hh in heads]
        kt = [jnp.where(up, 0.0, k[hh] * e[hh]) for hh in heads]
        p = [_dot(_bf(qt[hh]), _bf(kt[hh]), NT) for hh in heads]
        for hh in heads:
            a[hh] = a[hh] + jnp.where(same_block[i], p[hh], 0.0)
            saved[hh].append((e[hh], qt[hh], kt[hh]))
    return a, saved


def _hgrn_specs(tb, nb, rev):
    tmap = (lambda t: nb - 1 - t) if rev else (lambda t: t)
    assert HGRN_PAIR == HGRN_HEADS
    return [pl.BlockSpec((tb, 2816), lambda h, t: (tmap(t), 0)),
            pl.BlockSpec((1, HGRN_PAIR * HGRN_DIM), lambda h, t: (0, h)),
            pl.BlockSpec((1, HGRN_DIM), lambda h, t: (0, 0))]


def _hgrn_z(z_ref, sl, base, head):
    return z_ref[sl, base + HGRN_DIM * head:base + HGRN_DIM * (head + 1)]


def _hgrn_fwd(z, lb, onw, name, exchange=None):
    T = z.shape[0]
    tb = min(HGRN_ROWS, T)
    nb, c, nc = T // tb, HGRN_CHUNK, min(HGRN_ROWS, T) // HGRN_CHUNK

    def body(z_ref, lb_ref, onw_ref, rec_ref, o_ref, st_ref, state):
        @pl.when(pl.program_id(1) == 0)
        def _():
            state[...] = jnp.zeros_like(state)

        consts = _hgrn_consts()
        lbv = lb_ref[...]
        onwv = onw_ref[...]

        def chunk(ci, carry):
            sl = pl.ds(pl.multiple_of(ci * c, c), c)
            heads = range(HGRN_PAIR)
            lss = [slice(HGRN_DIM * hh, HGRN_DIM * (hh + 1)) for hh in heads]
            gates = [_hgrn_gates(_hgrn_z(z_ref, sl, Z_Q, hh), _hgrn_z(z_ref, sl, Z_F, hh), lbv[:, lss[hh]])
                     for hh in heads]
            q, k, g = [t[0] for t in gates], [t[4] for t in gates], [t[5] for t in gates]
            v = [_bf(_hgrn_z(z_ref, sl, Z_I, hh)) for hh in heads]
            b = [_fold3(_dot(consts[0], _split3(g[hh]))) for hh in heads]
            a, _ = _hgrn_intra(q, k, g, b, consts)
            st = [state[hh] for hh in heads]
            for hh in heads:
                st_ref[hh, ci] = st[hh]
            bl = [b[hh][c - 1:c, :] for hh in heads]
            o_state = [_dot(_bf(q[hh] * jnp.exp(b[hh])), _bf(st[hh]), NT) for hh in heads]
            kv = [_dot(v[hh], _bf(k[hh] * jnp.exp(bl[hh] - b[hh])), TN) for hh in heads]
            o = [_dot(_bf(a[hh]), v[hh]) + o_state[hh] for hh in heads]
            for hh in heads:
                state[hh] = st[hh] * jnp.exp(bl[hh]) + kv[hh]
                o_ref[sl, lss[hh]] = o[hh]
                oh, _ = _rms(o[hh])
                gr = _hgrn_z(z_ref, sl, Z_G, hh)
                rec_ref[sl, lss[hh]] = (oh * onwv * (gr * _sigmoid(gr))).astype(rec_ref.dtype)
            return carry

        lax.fori_loop(0, nc, chunk, 0)

    in_specs = _hgrn_specs(tb, nb, False)
    out_blk = pl.BlockSpec((tb, HGRN_PAIR * HGRN_DIM), lambda h, t: (t, h))
    return _hosted_call(
        body, name=name, grid=(HGRN_HEADS // HGRN_PAIR, nb), in_specs=in_specs,
        out_specs=[out_blk, out_blk, pl.BlockSpec((HGRN_PAIR, nc, HGRN_DIM, HGRN_DIM), lambda h, t: (h, t, 0, 0))],
        out_shape=[jax.ShapeDtypeStruct((T, 512), BF16), jax.ShapeDtypeStruct((T, 512), F32),
                   jax.ShapeDtypeStruct((HGRN_HEADS, T // c, HGRN_DIM, HGRN_DIM), F32)],
        scratch=[pltpu.VMEM((HGRN_PAIR, HGRN_DIM, HGRN_DIM), F32)], args=(z, lb, onw),
        semantics=("parallel", "arbitrary"), exchange=exchange)


def _hgrn_bwd(z, lb, onw, o, states, dcat, name, exchange=None):
    T = z.shape[0]
    tb = min(HGRN_ROWS, T)
    nb, c, nc = T // tb, HGRN_CHUNK, min(HGRN_ROWS, T) // HGRN_CHUNK

    def body(z_ref, lb_ref, onw_ref, o_ref, st_ref, drec_ref,
             dqr_ref, dfr_ref, dir_ref, dgr_ref, dlb_ref, donw_ref, dstate):
        @pl.when(pl.program_id(1) == 0)
        def _():
            dstate[...] = jnp.zeros_like(dstate)
            dlb_ref[...] = jnp.zeros_like(dlb_ref)

        @pl.when((pl.program_id(0) == 0) & (pl.program_id(1) == 0))
        def _():
            donw_ref[...] = jnp.zeros_like(donw_ref)

        consts = _hgrn_consts()
        rev_mat, eye, same_block, upper = consts[1:5]
        lbv = lb_ref[...]
        onwv = onw_ref[...]
        last = lax.broadcasted_iota(jnp.int32, (c, 1), 0) == c - 1

        def chunk(i, carry):
            ci = nc - 1 - i
            sl = pl.ds(pl.multiple_of(ci * c, c), c)
            hs = range(HGRN_PAIR)
            lss = [slice(HGRN_DIM * hh, HGRN_DIM * (hh + 1)) for hh in hs]
            qr = [_hgrn_z(z_ref, sl, Z_Q, hh) for hh in hs]
            gates = [_hgrn_gates(qr[hh], _hgrn_z(z_ref, sl, Z_F, hh), lbv[:, lss[hh]]) for hh in hs]
            q, sq, sf, f, k, g = ([t[j] for t in gates] for j in range(6))
            v = [_bf(_hgrn_z(z_ref, sl, Z_I, hh)) for hh in hs]
            b = [_fold3(_dot(consts[0], _split3(g[hh]))) for hh in hs]
            a, saved = _hgrn_intra(q, k, g, b, consts)
            st = [st_ref[hh, ci] for hh in hs]
            dst = [dstate[hh] for hh in hs]

            gr = [_hgrn_z(z_ref, sl, Z_G, hh) for hh in hs]
            sg = [_sigmoid(gr[hh]) for hh in hs]
            norm = [_rms(o_ref[sl, ls]) for ls in lss]
            oh, r = [t[0] for t in norm], [t[1] for t in norm]
            drec = [drec_ref[sl, ls].astype(F32) for ls in lss]
            don = [drec[hh] * (gr[hh] * sg[hh]) for hh in hs]
            do = [_bf(_rms_bwd(don[hh] * onwv, oh[hh], r[hh])) for hh in hs]
            donw = jnp.sum(don[0] * oh[0], axis=0, keepdims=True)
            for hh in hs:
                dgr_ref[sl, lss[hh]] = (drec[hh] * oh[hh] * onwv
                                        * (sg[hh] * (1.0 + gr[hh] * (1.0 - sg[hh])))).astype(dgr_ref.dtype)
                if hh:
                    donw = donw + jnp.sum(don[hh] * oh[hh], axis=0, keepdims=True)
            donw_ref[...] += donw

            eb = [jnp.exp(b[hh]) for hh in hs]
            bl = [b[hh][c - 1:c, :] for hh in hs]
            ebl = [jnp.exp(bl[hh]) for hh in hs]
            ekb = [jnp.exp(bl[hh] - b[hh]) for hh in hs]
            qe = [q[hh] * eb[hh] for hh in hs]
            ke = [k[hh] * ekb[hh] for hh in hs]
            da = [_dot(do[hh], v[hh], NT) for hh in hs]
            dat = [_dot(v[hh], do[hh], NT) for hh in hs]
            dqe = [_dot(do[hh], _bf(st[hh])) for hh in hs]
            dke = [_dot(v[hh], _bf(dst[hh])) for hh in hs]
            dv_a = [_dot(_bf(a[hh]), do[hh], TN) for hh in hs]
            dv_s = [_dot(_bf(ke[hh]), _bf(dst[hh]), NT) for hh in hs]
            dst_in = [_dot(do[hh], _bf(qe[hh]), TN) for hh in hs]
            dad = [jnp.sum(jnp.where(eye, da[hh], 0.0), axis=1, keepdims=True) for hh in hs]
            dq = [dqe[hh] * eb[hh] + dad[hh] * k[hh] for hh in hs]
            dk = [dke[hh] * ekb[hh] + dad[hh] * q[hh] for hh in hs]
            db_last = [jnp.sum(dke[hh] * ke[hh], axis=0, keepdims=True)
                       + jnp.sum(dst[hh] * st[hh], axis=0, keepdims=True) * ebl[hh] for hh in hs]
            for hh in hs:
                dstate[hh] = dst[hh] * ebl[hh] + dst_in[hh]
                dir_ref[sl, lss[hh]] = (dv_a[hh] + dv_s[hh]).astype(dir_ref.dtype)
            for lvl in range(len(HGRN_LEVELS)):
                xq = [_dot_hilo(jnp.where(same_block[lvl], da[hh], 0.0), saved[hh][lvl][2]) for hh in hs]
                xk = [_dot_hilo(jnp.where(same_block[lvl], dat[hh], 0.0), saved[hh][lvl][1]) for hh in hs]
                for hh in hs:
                    e = saved[hh][lvl][0]
                    dq[hh] = dq[hh] + jnp.where(upper[lvl], xq[hh] * e, 0.0)
                    dk[hh] = dk[hh] + jnp.where(upper[lvl], 0.0, xk[hh] * e)
            db = [q[hh] * dq[hh] - k[hh] * dk[hh] + jnp.where(last, db_last[hh], 0.0) for hh in hs]
            dg = [_fold3(_dot(rev_mat, _split3(db[hh]))) for hh in hs]

            for hh in hs:
                ls = lss[hh]
                dqr_ref[sl, ls] = (dq[hh] * (HGRN_DIM ** -0.5)
                                   * (sq[hh] * (1.0 + qr[hh] * (1.0 - sq[hh])))).astype(dqr_ref.dtype)
                dfk = dg[hh] / f[hh] - dk[hh]
                dfr_ref[sl, ls] = ((1.0 - lbv[:, ls]) * sf[hh] * (1.0 - sf[hh]) * dfk).astype(dfr_ref.dtype)
                dlb_ref[:, ls] += jnp.sum((1.0 - sf[hh]) * dfk, axis=0, keepdims=True)
            return carry

        lax.fori_loop(0, nc, chunk, 0)

    in_specs = _hgrn_specs(tb, nb, True)
    rblk = pl.BlockSpec((tb, HGRN_PAIR * HGRN_DIM), lambda h, t: (nb - 1 - t, h))
    in_specs = in_specs + [
        rblk,
        pl.BlockSpec((HGRN_PAIR, nc, HGRN_DIM, HGRN_DIM), lambda h, t: (h, nb - 1 - t, 0, 0)),
        pl.BlockSpec((tb, HGRN_PAIR * HGRN_DIM), lambda h, t: (nb - 1 - t, 4 // HGRN_PAIR + h)),
    ]
    return _hosted_call(
        body, name=name, grid=(HGRN_HEADS // HGRN_PAIR, nb), in_specs=in_specs,
        out_specs=[rblk, rblk, rblk, rblk, pl.BlockSpec((1, HGRN_PAIR * HGRN_DIM), lambda h, t: (0, h)),
                   pl.BlockSpec((1, HGRN_DIM), lambda h, t: (0, 0))],
        out_shape=[jax.ShapeDtypeStruct((T, 512), BF16)] * 4
        + [jax.ShapeDtypeStruct((1, 512), F32), jax.ShapeDtypeStruct((1, HGRN_DIM), F32)],
        scratch=[pltpu.VMEM((HGRN_PAIR, HGRN_DIM, HGRN_DIM), F32)], args=(z, lb, onw, o, states, dcat),
        semantics=("arbitrary", "arbitrary"), exchange=exchange)


def _lower_bound(logits, name):
    def body(l_ref, lb_ref):
        l0, l1 = l_ref[0:1, :], l_ref[1:2, :]
        m = jnp.maximum(l0, l1)
        e0, e1 = jnp.exp(l0 - m), jnp.exp(l1 - m)
        lb_ref[...] = e0 / (e0 + e1)

    return pl.pallas_call(
        body, name=name, out_shape=jax.ShapeDtypeStruct((1, logits.shape[1]), F32),
    )(logits)


def _lower_bound_bwd(lb, dlb, name):
    def body(lb_ref, dlb_ref, dl_ref):
        p = lb_ref[...]
        d0 = dlb_ref[...] * p * (1.0 - p)
        dl_ref[0:1, :] = d0
        dl_ref[1:2, :] = -d0

    return pl.pallas_call(
        body, name=name, out_shape=jax.ShapeDtypeStruct((2, lb.shape[1]), F32),
    )(lb, dlb)


CA_ROWS = 512


def _ca_fwd(q, k, v, name):
    T, W = q.shape
    M = k.shape[0]
    tq = min(CA_ROWS, T)
    scale = CA_HEAD_DIM ** -0.5

    def body(q_ref, k_ref, v_ref, o_ref):
        for h in range(CA_HEADS):
            hs = slice(CA_HEAD_DIM * h, CA_HEAD_DIM * (h + 1))
            s = _dot(q_ref[:, hs], k_ref[:, hs], NT) * scale
            p = jnp.exp(s - jnp.max(s, axis=-1, keepdims=True))
            p = p / jnp.sum(p, axis=-1, keepdims=True)
            o_ref[:, hs] = _dot(_bf(p), v_ref[:, hs]).astype(o_ref.dtype)

    full = pl.BlockSpec((M, W), lambda i: (0, 0))
    return pl.pallas_call(
        body, name=name, grid=(T // tq,), in_specs=[_row_spec(tq, W), full, full], out_specs=_row_spec(tq, W),
        out_shape=jax.ShapeDtypeStruct((T, W), BF16), compiler_params=_params("parallel"),
    )(q, k, v)


def _ca_bwd(q, k, v, do, name):
    T, W = q.shape
    M = k.shape[0]
    tq = min(CA_ROWS, T)
    scale = CA_HEAD_DIM ** -0.5

    def body(q_ref, k_ref, v_ref, do_ref, dq_ref, dk_ref, dv_ref):
        @pl.when(pl.program_id(0) == 0)
        def _():
            dk_ref[...] = jnp.zeros_like(dk_ref)
            dv_ref[...] = jnp.zeros_like(dv_ref)

        for h in range(CA_HEADS):
            hs = slice(CA_HEAD_DIM * h, CA_HEAD_DIM * (h + 1))
            qh, kh, vh, doh = q_ref[:, hs], k_ref[:, hs], v_ref[:, hs], do_ref[:, hs]
            s = _dot(qh, kh, NT) * scale
            p = jnp.exp(s - jnp.max(s, axis=-1, keepdims=True))
            p = p / jnp.sum(p, axis=-1, keepdims=True)
            dp = _dot(doh, vh, NT)
            ds = _bf(p * (dp - jnp.sum(p * dp, axis=-1, keepdims=True)) * scale)
            dq_ref[:, hs] = _dot(ds, kh).astype(dq_ref.dtype)
            dk_ref[:, hs] += _dot(ds, qh, TN)
            dv_ref[:, hs] += _dot(_bf(p), doh, TN)

    full = pl.BlockSpec((M, W), lambda i: (0, 0))
    return pl.pallas_call(
        body, name=name, grid=(T // tq,), in_specs=[_row_spec(tq, W), full, full, _row_spec(tq, W)],
        out_specs=[_row_spec(tq, W), full, full],
        out_shape=[jax.ShapeDtypeStruct((T, W), BF16), jax.ShapeDtypeStruct((M, W), F32), jax.ShapeDtypeStruct((M, W), F32)],
        compiler_params=_params("arbitrary"),
    )(q, k, v, do)


FFN_ROWS = 256
FFN_COLS = 1408
GELU_C0 = 0.7978845608028654
GELU_C1 = 0.044715


def _gelu(x):
    t = jnp.tanh(GELU_C0 * (x + GELU_C1 * x * x * x))
    return 0.5 * x * (1.0 + t), t


def _gelu_grad(x, t):
    return 0.5 * (1.0 + t) + 0.5 * x * (1.0 - t * t) * GELU_C0 * (1.0 + 3.0 * GELU_C1 * x * x)


def _shift_down(cur, halo, first, tb):
    row = lax.broadcasted_iota(jnp.int32, (tb, 1), 0)
    h6 = jnp.where(first, 0.0, halo[6:7])
    h7 = jnp.where(first, 0.0, halo[7:8])
    u1 = jnp.where(row == 0, h7, pltpu.roll(cur, 1, 0))
    u2 = jnp.where(row == 0, h6, jnp.where(row == 1, h7, pltpu.roll(cur, 2, 0)))
    return u1, u2


def _conv(u_ref, halo_ref, w_ref, b_ref, half, first, tb):
    cur = u_ref[half]
    u1, u2 = _shift_down(cur, halo_ref[half], first, tb)
    w = w_ref[...]
    return w[0:1] * u2 + w[1:2] * u1 + w[2:3] * cur + b_ref[...], cur, u1, u2


def _ffn_specs(tb, tc, rows_first):
    nj = D_FF // tc
    rc = (lambda a, b: (a, b)) if rows_first else (lambda a, b: (b, a))
    def at(f):
        return lambda a, b: f(*rc(a, b))
    blk = pl.BlockSpec((2, tb, tc), at(lambda t, j: (0, t, j)))
    halo = pl.BlockSpec((2, 8, tc), at(lambda t, j: (0, jnp.maximum(t * (tb // 8) - 1, 0), j)))
    wg = pl.BlockSpec((3, tc), at(lambda t, j: (0, j)))
    wv = pl.BlockSpec((3, tc), at(lambda t, j: (0, j + nj)))
    bg = pl.BlockSpec((1, tc), at(lambda t, j: (0, j)))
    bv = pl.BlockSpec((1, tc), at(lambda t, j: (0, j + nj)))
    flat = pl.BlockSpec((tb, tc), at(lambda t, j: (t, j)))
    return blk, halo, wg, wv, bg, bv, flat


def _glu_fwd(u, cw, cb, name):
    T = u.shape[1]
    tb, tc = min(FFN_ROWS, T), FFN_COLS

    def body(u_ref, halo_ref, wg_ref, wv_ref, bg_ref, bv_ref, a_ref):
        first = pl.program_id(0) == 0
        cg = _conv(u_ref, halo_ref, wg_ref, bg_ref, 0, first, tb)[0]
        cv = _conv(u_ref, halo_ref, wv_ref, bv_ref, 1, first, tb)[0]
        a_ref[...] = (_gelu(cg)[0] * cv).astype(a_ref.dtype)

    blk, halo, wg, wv, bg, bv, flat = _ffn_specs(tb, tc, True)
    return pl.pallas_call(
        body, name=name, grid=(T // tb, D_FF // tc), in_specs=[blk, halo, wg, wv, bg, bv], out_specs=flat,
        out_shape=jax.ShapeDtypeStruct((T, D_FF), BF16), compiler_params=_params("parallel", "parallel"),
    )(u, u, cw, cw, cb, cb)


def _glu_bwd(u, cw, cb, da, name, exchange=None):
    T = u.shape[1]
    tb, tc = min(FFN_ROWS, T), FFN_COLS

    def body(u_ref, halo_ref, wg_ref, wv_ref, bg_ref, bv_ref, da_ref, dc_ref, db_ref, dw_ref):
        first = pl.program_id(1) == 0

        @pl.when(first)
        def _():
            db_ref[...] = jnp.zeros_like(db_ref)
            dw_ref[...] = jnp.zeros_like(dw_ref)

        cg, ug, ug1, ug2 = _conv(u_ref, halo_ref, wg_ref, bg_ref, 0, first, tb)
        cv, uv, uv1, uv2 = _conv(u_ref, halo_ref, wv_ref, bv_ref, 1, first, tb)
        da = da_ref[...]
        gl, t = _gelu(cg)
        dcg = da * cv * _gelu_grad(cg, t)
        dcv = da * gl
        dc_ref[0] = dcg
        dc_ref[1] = dcv
        for half, dc, taps in ((0, dcg, (ug2, ug1, ug)), (1, dcv, (uv2, uv1, uv))):
            db_ref[half] += jnp.sum(dc, axis=0, keepdims=True)
            for tap in range(3):
                dw_ref[half, tap:tap + 1, :] += jnp.sum(dc * taps[tap], axis=0, keepdims=True)

    blk, halo, wg, wv, bg, bv, flat = _ffn_specs(tb, tc, False)
    return _hosted_call(
        body, name=name, grid=(D_FF // tc, T // tb), in_specs=[blk, halo, wg, wv, bg, bv, flat],
        out_specs=[blk, pl.BlockSpec((2, 1, tc), lambda j, t: (0, 0, j)), pl.BlockSpec((2, 3, tc), lambda j, t: (0, 0, j))],
        out_shape=[jax.ShapeDtypeStruct((2, T, D_FF), F32), jax.ShapeDtypeStruct((2, 1, D_FF), F32),
                   jax.ShapeDtypeStruct((2, 3, D_FF), F32)],
        scratch=[], args=(u, u, cw, cw, cb, cb, da), semantics=("parallel", "arbitrary"), exchange=exchange)


def _conv_bwd(dc, cw, name):
    T = dc.shape[1]
    tb, tc = min(FFN_ROWS, T), FFN_COLS
    nt, nj = T // tb, D_FF // tc

    def body(dc_ref, halo_ref, wg_ref, wv_ref, du_ref):
        last = pl.program_id(0) == nt - 1
        row = lax.broadcasted_iota(jnp.int32, (tb, 1), 0)
        for half, w_ref in ((0, wg_ref), (1, wv_ref)):
            cur = dc_ref[half]
            halo = halo_ref[half]
            h0 = jnp.where(last, 0.0, halo[0:1])
            h1 = jnp.where(last, 0.0, halo[1:2])
            d1 = jnp.where(row == tb - 1, h0, pltpu.roll(cur, tb - 1, 0))
            d2 = jnp.where(row == tb - 1, h1, jnp.where(row == tb - 2, h0, pltpu.roll(cur, tb - 2, 0)))
            w = w_ref[...]
            du_ref[half] = (w[2:3] * cur + w[1:2] * d1 + w[0:1] * d2).astype(du_ref.dtype)

    blk = pl.BlockSpec((2, tb, tc), lambda t, j: (0, t, j))
    halo = pl.BlockSpec((2, 8, tc), lambda t, j: (0, jnp.minimum((t + 1) * (tb // 8), T // 8 - 1), j))
    wg = pl.BlockSpec((3, tc), lambda t, j: (0, j))
    wv = pl.BlockSpec((3, tc), lambda t, j: (0, j + nj))
    return pl.pallas_call(
        body, name=name, grid=(nt, nj), in_specs=[blk, halo, wg, wv], out_specs=blk,
        out_shape=jax.ShapeDtypeStruct((2, T, D_FF), BF16), compiler_params=_params("parallel", "parallel"),
    )(dc, dc, cw, cw)


def _mesh_pos():
    return lax.axis_index("x"), lax.axis_index("y"), lax.axis_index("c")


def _peer(pos, k):
    return (pos[0] ^ ((k >> 2) & 1), pos[1] ^ ((k >> 1) & 1), pos[2] ^ (k & 1))


def _index(pos):
    return 4 * pos[0] + 2 * pos[1] + pos[2]


class _Exchange:
    def __init__(self, kind, buf):
        assert kind in ("gather", "scatter")
        self.kind, self.buf = kind, buf
        self.out_shape = jax.ShapeDtypeStruct(((N_DEV,) + buf.shape) if kind == "gather" else buf.shape, buf.dtype)
        self.spec = pl.BlockSpec(memory_space=pl.ANY)
        self.scratch = [pltpu.SemaphoreType.DMA((N_DEV - 1,)), pltpu.SemaphoreType.DMA((N_DEV - 1,)),
                        pltpu.SemaphoreType.DMA]

    def _src(self, x_ref, dest):
        return x_ref if self.kind == "gather" else x_ref.at[dest]

    def _copies(self, x_ref, out_ref, send_sems, recv_sems, local_sem):
        pos = _mesh_pos()
        me = _index(pos)
        local = pltpu.make_async_copy(self._src(x_ref, me), out_ref.at[me], local_sem)
        sends, recvs = [], []
        for k in range(1, N_DEV):
            peer = _peer(pos, k)
            sends.append(pltpu.make_async_remote_copy(
                src_ref=self._src(x_ref, _index(peer)), dst_ref=out_ref.at[me], send_sem=send_sems.at[k - 1],
                recv_sem=recv_sems.at[k - 1], device_id=peer, device_id_type=pl.DeviceIdType.MESH))
            recvs.append(pltpu.make_async_remote_copy(
                src_ref=self._src(x_ref, me), dst_ref=out_ref.at[_index(peer)], send_sem=send_sems.at[k - 1],
                recv_sem=recv_sems.at[k - 1], device_id=peer, device_id_type=pl.DeviceIdType.MESH))
        return local, sends, recvs

    def start(self, *refs):
        local, sends, _ = self._copies(*refs)
        local.start()
        for cp in sends:
            cp.start()

    def finish(self, *refs):
        local, sends, recvs = self._copies(*refs)
        for cp in recvs:
            cp.wait_recv()
        for cp in sends:
            cp.wait_send()
        local.wait()


def _hosted_call(body, *, name, grid, in_specs, out_specs, out_shape, scratch, args, semantics, exchange=None):
    if exchange is None:
        return pl.pallas_call(
            body, name=name, grid=grid, in_specs=in_specs, out_specs=out_specs, out_shape=out_shape,
            scratch_shapes=scratch, compiler_params=_params(*semantics))(*args)
    n_in, n_out, n_scr = len(in_specs), len(out_specs), len(scratch)

    def hosted(*refs):
        ins, x_ref = refs[:n_in], refs[n_in]
        outs, land_ref = refs[n_in + 1:n_in + 1 + n_out], refs[n_in + 1 + n_out]
        rest = refs[n_in + n_out + 2:]
        sems = rest[n_scr:]
        ids = [pl.program_id(a) for a in range(len(grid))]
        first, last = ids[0] == 0, ids[0] == grid[0] - 1
        for a in range(1, len(grid)):
            first, last = first & (ids[a] == 0), last & (ids[a] == grid[a] - 1)

        @pl.when(first)
        def _():
            exchange.start(x_ref, land_ref, *sems)

        body(*ins, *outs, *rest[:n_scr])

        @pl.when(last)
        def _():
            exchange.finish(x_ref, land_ref, *sems)

    return pl.pallas_call(
        hosted, name=name, grid=grid, in_specs=list(in_specs) + [exchange.spec],
        out_specs=list(out_specs) + [exchange.spec], out_shape=list(out_shape) + [exchange.out_shape],
        scratch_shapes=list(scratch) + exchange.scratch, compiler_params=_params(*(["arbitrary"] * len(grid))),
    )(*args, exchange.buf)


def _exchange_alone(exchange, name):
    def body(x_ref, out_ref, send_sems, recv_sems, local_sem):
        exchange.start(x_ref, out_ref, send_sems, recv_sems, local_sem)
        exchange.finish(x_ref, out_ref, send_sems, recv_sems, local_sem)

    return pl.pallas_call(
        body, name=name, out_shape=exchange.out_shape, in_specs=[exchange.spec], out_specs=exchange.spec,
        scratch_shapes=exchange.scratch)(exchange.buf)


def _adamw(w, g, m, v):
    m = ADAM_B1 * m + (1.0 - ADAM_B1) * g
    v = ADAM_B2 * v + (1.0 - ADAM_B2) * (g * g)
    m_hat = m / (1.0 - ADAM_B1 ** ADAM_STEP)
    v_hat = v / (1.0 - ADAM_B2 ** ADAM_STEP)
    delta = -ADAM_LR * (m_hat / (jnp.sqrt(v_hat) + ADAM_EPS) + ADAM_WD * w)
    return delta, m, v


def _sum_rows(parts, r0, rows, name, wmv=None):
    C = parts.shape[2]
    tr = max(t for t in range(16, ROWS + 1, 16) if rows % t == 0 and r0 % t == 0)

    def total(p_ref):
        g = p_ref[0].astype(F32)
        for i in range(1, N_DEV):
            g = g + p_ref[i].astype(F32)
        return g

    p_spec = pl.BlockSpec((N_DEV, tr, C), lambda i: (0, r0 // tr + i, 0))
    if wmv is None:
        def body(p_ref, g_ref):
            g_ref[...] = total(p_ref)

        return pl.pallas_call(
            body, name=name, grid=(rows // tr,), in_specs=[p_spec], out_specs=_row_spec(tr, C),
            out_shape=jax.ShapeDtypeStruct((rows, C), F32), compiler_params=_params("parallel"))(parts)

    def body(p_ref, w_ref, m_ref, v_ref, g_ref, d_ref, mo_ref, vo_ref):
        g = total(p_ref)
        g_ref[0] = g
        d_ref[0], mo_ref[0], vo_ref[0] = _adamw(w_ref[0], g, m_ref[0], v_ref[0])

    blk = pl.BlockSpec((1, tr, C), lambda i: (0, i, 0))
    return pl.pallas_call(
        body, name=name, grid=(rows // tr,), in_specs=[p_spec, blk, blk, blk], out_specs=[blk] * 4,
        out_shape=[jax.ShapeDtypeStruct((1, rows, C), F32)] * 4, compiler_params=_params("parallel"))(parts, *wmv)


def _sum_parts(parts, name):
    _, R, C = parts.shape

    def body(p_ref, g_ref):
        g = p_ref[0]
        for i in range(1, N_DEV):
            g = g + p_ref[i]
        g_ref[...] = g

    return pl.pallas_call(body, name=name, out_shape=jax.ShapeDtypeStruct((R, C), F32))(parts)


def _adamw_call(w, g, m, v, name):
    _, R, C = w.shape
    tr = min(ROWS, R)

    def body(w_ref, g_ref, m_ref, v_ref, d_ref, mo_ref, vo_ref):
        d_ref[...], mo_ref[...], vo_ref[...] = _adamw(w_ref[...], g_ref[...], m_ref[...], v_ref[...])

    blk = pl.BlockSpec((1, tr, C), lambda i: (0, i, 0))
    return pl.pallas_call(
        body, name=name, grid=(R // tr,), in_specs=[blk] * 4, out_specs=[blk] * 3,
        out_shape=[jax.ShapeDtypeStruct(w.shape, F32)] * 3, compiler_params=_params("parallel"))(w, g, m, v)


NORMS = ("mix_pre_norm", "mix_post_norm", "ca_pre_norm", "mem_norm", "ca_post_norm", "ffn_pre_norm", "ffn_post_norm")
SMALL = ("mix_pre_norm", "attn_sinks", "hgrn_lb_logits", "hgrn_out_norm", "mix_post_norm", "ca_pre_norm", "mem_norm",
         "ca_post_norm", "ffn_pre_norm", "ffn_conv_w", "ffn_conv_b", "ffn_post_norm")
SMALL_ROWS = 40
ROW_LOGITS, ROW_MISC, ROW_CONV_B, ROW_CONV_W = 7, 8, 9, 15
LANE_SINKS, LANE_LOSS = 128, 256
FF_PIECES = ((0, 1024), (1024, 2048), (2048, D_FF))


def _pack_small(norm_grads, dlogits, donw, dsinks, loss, d_cb, d_cw, name):
    def body(*refs):
        norm_refs = refs[:len(NORMS)]
        dl_ref, donw_ref, dsink_ref, loss_ref, cb_ref, cw_ref, out_ref = refs[len(NORMS):]
        out_ref[...] = jnp.zeros_like(out_ref)
        for i, ref in enumerate(norm_refs):
            out_ref[i:i + 1, :] = ref[...]
        out_ref[ROW_LOGITS:ROW_LOGITS + 1, 0:512] = dl_ref[0:1, :]
        out_ref[ROW_LOGITS:ROW_LOGITS + 1, 512:1024] = dl_ref[1:2, :]
        out_ref[ROW_MISC:ROW_MISC + 1, 0:HGRN_DIM] = donw_ref[...]
        out_ref[ROW_MISC:ROW_MISC + 1, LANE_SINKS:LANE_SINKS + ATTN_Q_HEADS] = dsink_ref[...]
        out_ref[ROW_MISC:ROW_MISC + 1, LANE_LOSS:LANE_LOSS + LANE] = loss_ref[...]
        for h in range(2):
            for j, (c0, c1) in enumerate(FF_PIECES):
                r = ROW_CONV_B + 3 * h + j
                out_ref[r:r + 1, 0:c1 - c0] = cb_ref[h, :, c0:c1]
                for t in range(3):
                    r = ROW_CONV_W + 3 * (3 * h + t) + j
                    out_ref[r:r + 1, 0:c1 - c0] = cw_ref[h, t:t + 1, c0:c1]

    return pl.pallas_call(
        body, name=name, out_shape=jax.ShapeDtypeStruct((SMALL_ROWS, 1024), F32),
    )(*norm_grads, dlogits, donw, dsinks, loss, d_cb, d_cw)


def _adamw_small(total, g_conv_w, w, m, v, name):
    n = len(SMALL)

    def body(*refs):
        t_ref, gcw_ref = refs[:2]
        w_refs, m_refs, v_refs = (dict(zip(SMALL, refs[2 + n * i:2 + n * (i + 1)])) for i in range(3))
        outs = refs[2 + 3 * n:]
        loss_ref = outs[0]
        g_refs, d_refs, mo_refs, vo_refs = (dict(zip(SMALL, outs[1 + n * i:1 + n * (i + 1)])) for i in range(4))
        loss_ref[...] = t_ref[ROW_MISC:ROW_MISC + 1, LANE_LOSS:LANE_LOSS + 1]

        def step(nm, idx, g):
            g_refs[nm][idx] = g
            d_refs[nm][idx], mo_refs[nm][idx], vo_refs[nm][idx] = _adamw(w_refs[nm][idx], g, m_refs[nm][idx], v_refs[nm][idx])

        everything = (slice(None), slice(None))
        for i, nm in enumerate(NORMS):
            step(nm, everything, t_ref[i:i + 1, :])
        step("hgrn_lb_logits", (slice(0, 1), slice(None)), t_ref[ROW_LOGITS:ROW_LOGITS + 1, 0:512])
        step("hgrn_lb_logits", (slice(1, 2), slice(None)), t_ref[ROW_LOGITS:ROW_LOGITS + 1, 512:1024])
        step("hgrn_out_norm", everything, t_ref[ROW_MISC:ROW_MISC + 1, 0:HGRN_DIM])
        step("attn_sinks", everything, t_ref[ROW_MISC:ROW_MISC + 1, LANE_SINKS:LANE_SINKS + ATTN_Q_HEADS])
        for h in range(2):
            for j, (c0, c1) in enumerate(FF_PIECES):
                r = ROW_CONV_B + 3 * h + j
                step("ffn_conv_b", (slice(None), slice(D_FF * h + c0, D_FF * h + c1)), t_ref[r:r + 1, 0:c1 - c0])
        step("ffn_conv_w", (slice(None), slice(None), slice(None)), gcw_ref[...])

    shapes = [jax.ShapeDtypeStruct(w[nm].shape, F32) for nm in SMALL]
    out = pl.pallas_call(
        body, name=name, out_shape=[jax.ShapeDtypeStruct((1, 1), F32)] + shapes * 4,
    )(total, g_conv_w, *[w[nm] for nm in SMALL], *[m[nm] for nm in SMALL], *[v[nm] for nm in SMALL])
    trees = [dict(zip(SMALL, out[1 + n * i:1 + n * (i + 1)])) for i in range(4)]
    return out[0], trees


BIG = ("w_in", "w_out", "ca_wq", "ca_wk", "ca_wv", "ca_wo", "ffn_w_up", "ffn_w_down")
BIG_FULL = {"w_in": (1024, 2816), "w_out": (1024, 1024), "ca_wq": (1024, 1024), "ca_wk": (1024, 1024),
            "ca_wv": (1024, 1024), "ca_wo": (1024, 1024), "ffn_w_up": (1024, 5632), "ffn_w_down": (2816, 1024)}
G_IN, G_MID, G_UP, G_DOWN = ("w_in",), ("w_out", "ca_wq", "ca_wk", "ca_wv", "ca_wo"), ("ffn_w_up",), ("ffn_w_down",)
GROUPS = (G_IN, G_MID, G_UP, G_DOWN)
COL_SHARDED = ("w_in", "ffn_w_up")
PACK_COLS = 1024


def _big_rows(name):
    r, c = BIG_FULL[name]
    return r * c // N_DEV // PACK_COLS


def _pack_shards(w, names):
    rows = [w[n][0].T if n in COL_SHARDED else w[n][0] for n in names]
    return (rows[0] if len(rows) == 1 else jnp.concatenate(rows, axis=0)).astype(BF16)


def _unpack_gathered(gathered, names):
    out, r0 = {}, 0
    for n in names:
        rows = _big_rows(n)
        out[n] = gathered[:, r0:r0 + rows].reshape(N_DEV * rows, PACK_COLS)
        r0 += rows
    return out


def _pack_full_grads(grads, names):
    parts = [grads[n].reshape(N_DEV, _big_rows(n), PACK_COLS) for n in names]
    return parts[0] if len(parts) == 1 else jnp.concatenate(parts, axis=1)


def kernel(x, mem, mix_pre_norm, w_in, attn_sinks, hgrn_lb_logits, hgrn_out_norm, w_out, mix_post_norm, ca_pre_norm, mem_norm, ca_wq, ca_wk, ca_wv, ca_wo, ca_post_norm, ffn_pre_norm, ffn_w_up, ffn_conv_w, ffn_conv_b, ffn_w_down, ffn_post_norm, loss_target, m_mix_pre_norm, m_w_in, m_attn_sinks, m_hgrn_lb_logits, m_hgrn_out_norm, m_w_out, m_mix_post_norm, m_ca_pre_norm, m_mem_norm, m_ca_wq, m_ca_wk, m_ca_wv, m_ca_wo, m_ca_post_norm, m_ffn_pre_norm, m_ffn_w_up, m_ffn_conv_w, m_ffn_conv_b, m_ffn_w_down, m_ffn_post_norm, v_mix_pre_norm, v_w_in, v_attn_sinks, v_hgrn_lb_logits, v_hgrn_out_norm, v_w_out, v_mix_post_norm, v_ca_pre_norm, v_mem_norm, v_ca_wq, v_ca_wk, v_ca_wv, v_ca_wo, v_ca_post_norm, v_ffn_pre_norm, v_ffn_w_up, v_ffn_conv_w, v_ffn_conv_b, v_ffn_w_down, v_ffn_post_norm):
    names = ["mix_pre_norm", "w_in", "attn_sinks", "hgrn_lb_logits", "hgrn_out_norm", "w_out", "mix_post_norm",
             "ca_pre_norm", "mem_norm", "ca_wq", "ca_wk", "ca_wv", "ca_wo", "ca_post_norm", "ffn_pre_norm",
             "ffn_w_up", "ffn_conv_w", "ffn_conv_b", "ffn_w_down", "ffn_post_norm"]
    w_all = dict(zip(names, [mix_pre_norm, w_in, attn_sinks, hgrn_lb_logits, hgrn_out_norm, w_out, mix_post_norm,
                             ca_pre_norm, mem_norm, ca_wq, ca_wk, ca_wv, ca_wo, ca_post_norm, ffn_pre_norm,
                             ffn_w_up, ffn_conv_w, ffn_conv_b, ffn_w_down, ffn_post_norm]))
    m_all = dict(zip(names, [m_mix_pre_norm, m_w_in, m_attn_sinks, m_hgrn_lb_logits, m_hgrn_out_norm, m_w_out,
                             m_mix_post_norm, m_ca_pre_norm, m_mem_norm, m_ca_wq, m_ca_wk, m_ca_wv, m_ca_wo,
                             m_ca_post_norm, m_ffn_pre_norm, m_ffn_w_up, m_ffn_conv_w, m_ffn_conv_b, m_ffn_w_down,
                             m_ffn_post_norm]))
    v_all = dict(zip(names, [v_mix_pre_norm, v_w_in, v_attn_sinks, v_hgrn_lb_logits, v_hgrn_out_norm, v_w_out,
                             v_mix_post_norm, v_ca_pre_norm, v_mem_norm, v_ca_wq, v_ca_wk, v_ca_wv, v_ca_wo,
                             v_ca_post_norm, v_ffn_pre_norm, v_ffn_w_up, v_ffn_conv_w, v_ffn_conv_b, v_ffn_w_down,
                             v_ffn_post_norm]))
    dev = _index(_mesh_pos())

    w_packs = {grp: _pack_shards(w_all, grp) for grp in GROUPS}
    shard_w = D_FF * 2 // N_DEV
    conv_w_rows = _exchange_alone(_Exchange("gather", ffn_conv_w[0]), "gather_conv_w")
    conv_w_full = conv_w_rows.transpose(1, 0, 2).reshape(3, 2 * D_FF)

    received, small_pack, grad_x = _local_step(
        x[0], mem[0], loss_target[0], w_packs, conv_w_full,
        {n: w_all[n] for n in NORMS}, attn_sinks, hgrn_lb_logits, hgrn_out_norm, ffn_conv_b)

    total = _sum_parts(_exchange_alone(_Exchange("gather", small_pack), "gather_small"), "sum_small")
    cw = total[ROW_CONV_W:ROW_CONV_W + 18].reshape(2, 3, 3 * PACK_COLS)[:, :, :D_FF]
    cw = cw.transpose(1, 0, 2).reshape(3, 2 * D_FF)
    g_conv_w = lax.dynamic_slice_in_dim(cw, dev * shard_w, shard_w, axis=1)[None]
    loss, (out_g, out_d, out_m, out_v) = _adamw_small(total, g_conv_w, w_all, m_all, v_all, "adamw_small")

    for grp in GROUPS:
        r0 = 0
        for n in grp:
            rows = _big_rows(n)
            if n in COL_SHARDED:
                g = _sum_rows(received[grp], r0, rows, "sum_" + n).T[None]
                d, mo, vo = _adamw_call(w_all[n], g, m_all[n], v_all[n], "adamw_" + n)
            else:
                g, d, mo, vo = _sum_rows(received[grp], r0, rows, "adamw_" + n, wmv=(w_all[n], m_all[n], v_all[n]))
            out_g[n], out_d[n], out_m[n], out_v[n] = g, d, mo, vo
            r0 += rows

    return (loss[0, 0], grad_x[None], *[out_g[n] for n in names], *[out_d[n] for n in names],
            *[out_m[n] for n in names], *[out_v[n] for n in names])


def _local_step(x, mem, target, w_packs, conv_w, norms, sinks, lb_logits, out_norm, conv_b):
    g1, g2, g3 = norms["mix_pre_norm"], norms["mix_post_norm"], norms["ca_pre_norm"]
    g4, g5, g6, g7 = norms["mem_norm"], norms["ca_post_norm"], norms["ffn_pre_norm"], norms["ffn_post_norm"]

    h1, gathered = _norm_fwd(x, g1, "mix_norm", exchange=_Exchange("gather", w_packs[G_IN]))
    w_in_t = _unpack_gathered(gathered, G_IN)["w_in"]
    up_shard = w_packs[G_UP]
    up_rows = up_shard.shape[0]
    up_cuts = (0, up_rows // 2, 3 * up_rows // 4, up_rows)
    up_parts = [up_shard[a:b] for a, b in zip(up_cuts[:-1], up_cuts[1:])]
    z, up_0 = _mm(h1, w_in_t, mode="nt", out_dtype=F32, name="in_proj", tn=1408,
                  exchange=_Exchange("gather", up_parts[0]))
    attn, lse, gathered = _swa_fwd(z, sinks, "swa_fwd", exchange=_Exchange("gather", w_packs[G_DOWN]))
    w_down = _unpack_gathered(gathered, G_DOWN)["ffn_w_down"]
    lb = _lower_bound(lb_logits, "lower_bound")
    rec, o_rec, states, gathered = _hgrn_fwd(z, lb, out_norm, "hgrn_fwd", exchange=_Exchange("gather", w_packs[G_MID]))
    w_out, wq, wk, wv, wo = (_unpack_gathered(gathered, G_MID)[n] for n in G_MID)
    cat = jnp.concatenate([attn, rec], axis=1)
    mix = _mm(cat, w_out, mode="nn", out_dtype=F32, name="out_proj")
    x1, h2, up_1 = _post_pre(x, mix, g2, g3, "mix_post", exchange=_Exchange("gather", up_parts[1]))
    mem_n = _norm_fwd(mem, g4, "mem_norm")
    q = _mm(h2, wq, mode="nn", out_dtype=BF16, name="ca_q")
    k = _mm(mem_n, wk, mode="nn", out_dtype=BF16, name="ca_k")
    v = _mm(mem_n, wv, mode="nn", out_dtype=BF16, name="ca_v")
    oc = _ca_fwd(q, k, v, "ca_fwd")
    c = _mm(oc, wo, mode="nn", out_dtype=F32, name="ca_o")
    x2, h3, up_2 = _post_pre(x1, c, g5, g6, "ca_post", exchange=_Exchange("gather", up_parts[2]))
    w_up_t = jnp.concatenate([up_0, up_1, up_2], axis=1).reshape(-1, PACK_COLS)
    u = _mm(h3, w_up_t, mode="nt", out_dtype=F32, name="ffn_up", tn=1408, split_out=True)
    a = _glu_fwd(u, conv_w, conv_b, "glu_fwd")
    y = _mm(a, w_down, mode="nn", out_dtype=F32, name="ffn_down", tk=2816)
    loss, dx3, dy, dg7 = _final(x2, y, g7, target, "loss_head")

    da = _mm(dy, w_down, mode="nt", out_dtype=F32, name="ffn_down_dx", tn=1408)
    d_w_down = _mm(a, dy, mode="tn", out_dtype=BF16, name="ffn_down_dw", tm=1408, tk=1024)
    dc, d_cb, d_cw, got_down = _glu_bwd(
        u, conv_w, conv_b, da, "glu_bwd",
        exchange=_Exchange("scatter", _pack_full_grads({"ffn_w_down": d_w_down}, G_DOWN)))
    du = _conv_bwd(dc, conv_w, "conv_bwd")
    d_w_up_t = _mm(du, h3, mode="tn", out_dtype=BF16, name="ffn_up_dw", tm=1408, tk=1024, split_a=True)
    dh3, got_up = _mm(du, w_up_t, mode="nn", out_dtype=BF16, name="ffn_up_dx", tm=2048, tk=1408, split_a=True,
                      exchange=_Exchange("scatter", _pack_full_grads({"ffn_w_up": d_w_up_t}, G_UP)))
    dx2, dcv, dg6, dg5 = _norm_bwd2(dx3, dh3, x2, g6, c, g5, "ca_post_bwd")
    doc = _mm(dcv, wo, mode="nt", out_dtype=BF16, name="ca_o_dx")
    d_wo = _mm(oc, dcv, mode="tn", out_dtype=BF16, name="ca_o_dw", tm=1024, tk=1024)
    dq, dk, dv = _ca_bwd(q, k, v, doc, "ca_bwd")
    d_wq = _mm(h2, dq, mode="tn", out_dtype=BF16, name="ca_q_dw", tm=1024, tk=1024)
    dh2 = _mm(dq, wq, mode="nt", out_dtype=BF16, name="ca_q_dx")
    d_wk = _mm(mem_n, dk, mode="tn", out_dtype=BF16, name="ca_k_dw", tm=1024)
    d_wv = _mm(mem_n, dv, mode="tn", out_dtype=BF16, name="ca_v_dw", tm=1024)
    dmem_k = _mm(dk, wk, mode="nt", out_dtype=F32, name="ca_k_dx")
    dmem_v = _mm(dv, wv, mode="nt", out_dtype=F32, name="ca_v_dx")
    dg4 = _gain_bwd(mem, dmem_k, dmem_v, "mem_norm_bwd")
    dx1, dmix, dg3, dg2 = _norm_bwd2(dx2, dh2, x1, g3, mix, g2, "mix_post_bwd")
    dcat = _mm(dmix, w_out, mode="nt", out_dtype=BF16, name="out_proj_dx")
    d_w_out = _mm(cat, dmix, mode="tn", out_dtype=BF16, name="out_proj_dw", tm=1024, tk=1024)
    mid = {"w_out": d_w_out, "ca_wq": d_wq, "ca_wk": d_wk, "ca_wv": d_wv, "ca_wo": d_wo}
    dqr, dfr, dir_, dgr, dlb, donw, got_mid = _hgrn_bwd(
        z, lb, out_norm, o_rec, states, dcat, "hgrn_bwd", exchange=_Exchange("scatter", _pack_full_grads(mid, G_MID)))
    dq_a, dka, dkb, dva, dvb, dsinks = _swa_bwd(z, sinks, dcat, lse, "swa_bwd")
    dz = _assemble_dz(dq_a, dka, dkb, dva, dvb, dqr, dfr, dir_, dgr, "assemble_dz")
    d_w_in_t = _mm(dz, h1, mode="tn", out_dtype=BF16, name="in_proj_dw", tm=1408, tk=1024)
    dh1, got_in = _mm(dz, w_in_t, mode="nn", out_dtype=BF16, name="in_proj_dx", tk=2816,
                      exchange=_Exchange("scatter", _pack_full_grads({"w_in": d_w_in_t}, G_IN)))
    dx, dg1 = _norm_bwd1(dx1, dh1, x, g1, "mix_norm_bwd")

    small_pack = _pack_small(
        (dg1, dg2, dg3, dg4, dg5, dg6, dg7), _lower_bound_bwd(lb, dlb, "lower_bound_bwd"), donw, dsinks, loss,
        d_cb, d_cw, "pack_small")
    return {G_IN: got_in, G_MID: got_mid, G_UP: got_up, G_DOWN: got_down}, small_pack, dx
```

```python
import jax
import jax.numpy as jnp
from jax import lax
from jax.experimental import pallas as pl
from jax.experimental.pallas import tpu as pltpu

F32 = jnp.float32
BF16 = jnp.bfloat16
EPS = 1e-6
N_DEV = 8
MESH_AXES = ("x", "y", "c")

ATTN_HEAD_DIM = 64
ATTN_Q_HEADS = 8
ATTN_KV_HEADS = 2
ATTN_BLOCK = 128
HGRN_HEADS = 4
HGRN_DIM = 128
HGRN_CHUNK = 64
HGRN_PAIR = 4
Z_Q, Z_F, Z_I, Z_G = 768, 1280, 1792, 2304
HGRN_LEVELS = (32, 16, 8, 4, 2, 1)
CA_HEADS = 4
CA_HEAD_DIM = 256
D_FF = 2816

ADAM_LR = 0.001
ADAM_B1 = 0.9
ADAM_B2 = 0.999
ADAM_EPS = 1e-08
ADAM_WD = 0.01
ADAM_STEP = 10

VMEM_LIMIT = 56 << 20
LANE = 128

NT = (((1,), (1,)), ((), ()))
TN = (((0,), (0,)), ((), ()))


def _params(*sem):
    return pltpu.CompilerParams(dimension_semantics=sem, vmem_limit_bytes=VMEM_LIMIT)


def _tile(n, cap):
    if n <= cap:
        return n
    best = 0
    for t in range(LANE, cap + 1, LANE):
        if n % t == 0:
            best = t
    assert best, (n, cap)
    return best


def _dot(a, b, dims=None):
    if dims is None:
        return jnp.dot(a, b, preferred_element_type=F32)
    return lax.dot_general(a, b, dims, preferred_element_type=F32)


def _bf(x):
    return x.astype(BF16)


def _sigmoid(x):
    return 1.0 / (1.0 + jnp.exp(-x))


def _rms(x):
    r = lax.rsqrt(jnp.mean(x * x, axis=-1, keepdims=True) + EPS)
    return x * r, r


def _rms_bwd(dxh, xh, r):
    return r * (dxh - xh * jnp.mean(dxh * xh, axis=-1, keepdims=True))


def _mm(a, b, *, mode, out_dtype, name, tm=1024, tn=1024, tk=1024, split_a=False, split_b=False, split_out=False,
        exchange=None):
    def dims(arr, split):
        if split:
            return arr.shape[1], 2 * arr.shape[2]
        return arr.shape

    ar, ac = dims(a, split_a)
    br, bc = dims(b, split_b)
    if mode == "nn":
        M, K, N = ar, ac, bc
        assert br == K
    elif mode == "nt":
        M, K, N = ar, ac, br
        assert bc == K
    else:
        K, M, N = ar, ac, bc
        assert br == K
    a_cols_half = ac // 2 if split_a else None
    b_cols_half = bc // 2 if split_b else None
    tm = _tile(M, tm)
    tn = _tile((N // 2) if (split_out or (split_b and mode != "nt")) else N, tn)
    tk = _tile((K // 2) if ((split_a and mode != "tn") or (split_b and mode == "nt")) else K, tk)
    if split_a and mode == "tn":
        tm = _tile(M // 2, tm)
    gm, gn, gk = M // tm, N // tn, K // tk
    a_bytes, b_bytes = a.size * a.dtype.itemsize, b.size * b.dtype.itemsize
    rows_outer = gk > 1 or a_bytes + gm * b_bytes <= gn * a_bytes + b_bytes
    grid = (gm, gn, gk) if rows_outer else (gn, gm, gk)

    def spec(split, half, blk, rc):
        def imap(p, q, k):
            r, c = rc(*((p, q) if rows_outer else (q, p)), k)
            if not split:
                return (r, c)
            per_half = half // blk[1]
            return (c // per_half, r, c % per_half)

        return pl.BlockSpec(((None,) + blk) if split else blk, imap)

    if mode == "nn":
        a_spec = spec(split_a, a_cols_half, (tm, tk), lambda i, j, k: (i, k))
        b_spec = spec(split_b, b_cols_half, (tk, tn), lambda i, j, k: (k, j))
        dn = None
    elif mode == "nt":
        a_spec = spec(split_a, a_cols_half, (tm, tk), lambda i, j, k: (i, k))
        b_spec = spec(split_b, b_cols_half, (tn, tk), lambda i, j, k: (j, k))
        dn = NT
    else:
        a_spec = spec(split_a, a_cols_half, (tk, tm), lambda i, j, k: (k, i))
        b_spec = spec(split_b, b_cols_half, (tk, tn), lambda i, j, k: (k, j))
        dn = TN
    o_spec = spec(split_out, N // 2 if split_out else None, (tm, tn), lambda i, j, k: (i, j))
    out_shape = (2, M, N // 2) if split_out else (M, N)

    if gk == 1:
        def body(a_ref, b_ref, o_ref):
            o_ref[...] = _dot(_bf(a_ref[...]), _bf(b_ref[...]), dn).astype(o_ref.dtype)
        scratch = []
    else:
        def body(a_ref, b_ref, o_ref, acc_ref):
            k = pl.program_id(2)

            @pl.when(k == 0)
            def _():
                acc_ref[...] = jnp.zeros_like(acc_ref)

            acc_ref[...] += _dot(_bf(a_ref[...]), _bf(b_ref[...]), dn)

            @pl.when(k == gk - 1)
            def _():
                o_ref[...] = acc_ref[...].astype(o_ref.dtype)
        scratch = [pltpu.VMEM((tm, tn), F32)]

    out = _hosted_call(
        body, name=name, grid=grid, in_specs=[a_spec, b_spec], out_specs=[o_spec],
        out_shape=[jax.ShapeDtypeStruct(out_shape, out_dtype)], scratch=scratch, args=(a, b),
        semantics=("parallel", "parallel", "arbitrary"), exchange=exchange)
    return out[0] if exchange is None else out


ROWS = 512


def _row_spec(tr, cols):
    return pl.BlockSpec((tr, cols), lambda i: (i, 0))


def _vec_spec(cols):
    return pl.BlockSpec((1, cols), lambda i: (0, 0))


def _norm_fwd(x, g, name, exchange=None):
    T, Dm = x.shape
    tr = min(ROWS, T)

    def body(x_ref, g_ref, h_ref):
        xh, _ = _rms(x_ref[...])
        h_ref[...] = (xh * g_ref[...]).astype(h_ref.dtype)

    out = _hosted_call(
        body, name=name, grid=(T // tr,), in_specs=[_row_spec(tr, Dm), _vec_spec(Dm)], out_specs=[_row_spec(tr, Dm)],
        out_shape=[jax.ShapeDtypeStruct((T, Dm), BF16)], scratch=[], args=(x, g), semantics=("parallel",),
        exchange=exchange)
    return out[0] if exchange is None else out


def _post_pre(x, m, g_post, g_pre, name, exchange=None):
    T, Dm = x.shape
    tr = min(ROWS, T)

    def body(x_ref, m_ref, gp_ref, gn_ref, xo_ref, h_ref):
        mh, _ = _rms(m_ref[...].astype(F32))
        xn = x_ref[...] + mh * gp_ref[...]
        xo_ref[...] = xn
        xh, _ = _rms(xn)
        h_ref[...] = (xh * gn_ref[...]).astype(h_ref.dtype)

    return _hosted_call(
        body, name=name, grid=(T // tr,),
        in_specs=[_row_spec(tr, Dm), _row_spec(tr, Dm), _vec_spec(Dm), _vec_spec(Dm)],
        out_specs=[_row_spec(tr, Dm), _row_spec(tr, Dm)],
        out_shape=[jax.ShapeDtypeStruct((T, Dm), F32), jax.ShapeDtypeStruct((T, Dm), BF16)],
        scratch=[], args=(x, m, g_post, g_pre), semantics=("parallel",), exchange=exchange)


def _final(x2, y, g_post, target, name):
    T, Dm = x2.shape
    tr = min(ROWS, T)

    def body(x_ref, y_ref, g_ref, t_ref, loss_ref, dx_ref, dy_ref, dg_ref):
        @pl.when(pl.program_id(0) == 0)
        def _():
            loss_ref[...] = jnp.zeros_like(loss_ref)
            dg_ref[...] = jnp.zeros_like(dg_ref)

        g = g_ref[...]
        yh, r = _rms(y_ref[...].astype(F32))
        d = x_ref[...] + yh * g - t_ref[...]
        loss_ref[...] += jnp.zeros((1, LANE), F32) + 0.5 * jnp.sum(jnp.mean(d * d, axis=-1, keepdims=True))
        dx = d * (1.0 / Dm)
        dx_ref[...] = dx
        dy_ref[...] = _rms_bwd(dx * g, yh, r).astype(dy_ref.dtype)
        dg_ref[...] += jnp.sum(dx * yh, axis=0, keepdims=True)

    return pl.pallas_call(
        body, name=name, grid=(T // tr,),
        in_specs=[_row_spec(tr, Dm), _row_spec(tr, Dm), _vec_spec(Dm), _row_spec(tr, Dm)],
        out_specs=[_vec_spec(LANE), _row_spec(tr, Dm), _row_spec(tr, Dm), _vec_spec(Dm)],
        out_shape=[jax.ShapeDtypeStruct((1, LANE), F32), jax.ShapeDtypeStruct((T, Dm), F32),
                   jax.ShapeDtypeStruct((T, Dm), BF16), jax.ShapeDtypeStruct((1, Dm), F32)],
        compiler_params=_params("arbitrary"),
    )(x2, y, g_post, target)


def _norm_bwd2(dx_cur, dh, x_prev, g_pre, m_prev, g_post, name):
    T, Dm = x_prev.shape
    tr = min(ROWS, T)

    def body(dx_ref, dh_ref, x_ref, gn_ref, m_ref, gp_ref, dxo_ref, dm_ref, dgn_ref, dgp_ref):
        @pl.when(pl.program_id(0) == 0)
        def _():
            dgn_ref[...] = jnp.zeros_like(dgn_ref)
            dgp_ref[...] = jnp.zeros_like(dgp_ref)

        dh = dh_ref[...].astype(F32)
        xh, r = _rms(x_ref[...])
        dx = dx_ref[...] + _rms_bwd(dh * gn_ref[...], xh, r)
        dxo_ref[...] = dx
        dgn_ref[...] += jnp.sum(dh * xh, axis=0, keepdims=True)
        mh, rm = _rms(m_ref[...].astype(F32))
        dm_ref[...] = _rms_bwd(dx * gp_ref[...], mh, rm).astype(dm_ref.dtype)
        dgp_ref[...] += jnp.sum(dx * mh, axis=0, keepdims=True)

    return pl.pallas_call(
        body, name=name, grid=(T // tr,),
        in_specs=[_row_spec(tr, Dm), _row_spec(tr, Dm), _row_spec(tr, Dm), _vec_spec(Dm), _row_spec(tr, Dm), _vec_spec(Dm)],
        out_specs=[_row_spec(tr, Dm), _row_spec(tr, Dm), _vec_spec(Dm), _vec_spec(Dm)],
        out_shape=[jax.ShapeDtypeStruct((T, Dm), F32), jax.ShapeDtypeStruct((T, Dm), BF16),
                   jax.ShapeDtypeStruct((1, Dm), F32), jax.ShapeDtypeStruct((1, Dm), F32)],
        compiler_params=_params("arbitrary"),
    )(dx_cur, dh, x_prev, g_pre, m_prev, g_post)


def _norm_bwd1(dx_cur, dh, x_prev, g_pre, name):
    T, Dm = x_prev.shape
    tr = min(ROWS, T)

    def body(dx_ref, dh_ref, x_ref, gn_ref, dxo_ref, dgn_ref):
        @pl.when(pl.program_id(0) == 0)
        def _():
            dgn_ref[...] = jnp.zeros_like(dgn_ref)

        dh = dh_ref[...].astype(F32)
        xh, r = _rms(x_ref[...])
        dxo_ref[...] = dx_ref[...] + _rms_bwd(dh * gn_ref[...], xh, r)
        dgn_ref[...] += jnp.sum(dh * xh, axis=0, keepdims=True)

    return pl.pallas_call(
        body, name=name, grid=(T // tr,),
        in_specs=[_row_spec(tr, Dm), _row_spec(tr, Dm), _row_spec(tr, Dm), _vec_spec(Dm)],
        out_specs=[_row_spec(tr, Dm), _vec_spec(Dm)],
        out_shape=[jax.ShapeDtypeStruct((T, Dm), F32), jax.ShapeDtypeStruct((1, Dm), F32)],
        compiler_params=_params("arbitrary"),
    )(dx_cur, dh, x_prev, g_pre)


def _gain_bwd(x, dh_a, dh_b, name):
    T, Dm = x.shape

    def body(x_ref, a_ref, b_ref, dg_ref):
        xh, _ = _rms(x_ref[...])
        dg_ref[...] = jnp.sum((a_ref[...] + b_ref[...]) * xh, axis=0, keepdims=True)

    return pl.pallas_call(
        body, name=name, grid=(1,), in_specs=[_row_spec(T, Dm)] * 3, out_specs=_vec_spec(Dm),
        out_shape=jax.ShapeDtypeStruct((1, Dm), F32), compiler_params=_params("arbitrary"),
    )(x, dh_a, dh_b)


ATTN_GROUP = ATTN_Q_HEADS // ATTN_KV_HEADS


def _swa_mask(n):
    rows = ATTN_GROUP * ATTN_BLOCK
    row = lax.broadcasted_iota(jnp.int32, (rows, 2 * ATTN_BLOCK), 0) & (ATTN_BLOCK - 1)
    col = lax.broadcasted_iota(jnp.int32, (rows, 2 * ATTN_BLOCK), 1)
    diff = row + ATTN_BLOCK - col
    return (diff >= 0) & (diff < ATTN_BLOCK) & ((col >= ATTN_BLOCK) | (n > 0))


def _swa_rows(ref, hk, dtype):
    hd = ATTN_HEAD_DIM
    return jnp.concatenate(
        [ref[:, hd * (hk * ATTN_GROUP + g):hd * (hk * ATTN_GROUP + g + 1)].astype(dtype) for g in range(ATTN_GROUP)],
        axis=0)


def _swa_per_row(vals):
    seg = lax.broadcasted_iota(jnp.int32, (ATTN_GROUP * ATTN_BLOCK, 1), 0) // ATTN_BLOCK
    col = jnp.zeros((ATTN_GROUP * ATTN_BLOCK, 1), F32)
    for g, val in enumerate(vals):
        col = jnp.where(seg == g, val, col)
    return col


def _swa_specs():
    blk = ATTN_BLOCK
    prev = lambda n: jnp.maximum(n - 1, 0)
    return [
        pl.BlockSpec(memory_space=pltpu.SMEM),
        pl.BlockSpec((blk, 512), lambda n: (n, 0)),
        pl.BlockSpec((blk, 128), lambda n: (prev(n), 4)),
        pl.BlockSpec((blk, 128), lambda n: (n, 4)),
        pl.BlockSpec((blk, 128), lambda n: (prev(n), 5)),
        pl.BlockSpec((blk, 128), lambda n: (n, 5)),
    ]


def _swa_fwd(z, sinks, name, exchange=None):
    T = z.shape[0]
    blk, hd = ATTN_BLOCK, ATTN_HEAD_DIM
    scale = hd ** -0.5

    def body(sink_ref, q_ref, kp_ref, kc_ref, vp_ref, vc_ref, o_ref, lse_ref):
        allowed = _swa_mask(pl.program_id(0))
        hks = range(ATTN_KV_HEADS)
        kss = [slice(hd * hk, hd * hk + hd) for hk in hks]
        k = [_bf(jnp.concatenate([kp_ref[:, ks], kc_ref[:, ks]], axis=0)) for ks in kss]
        v = [_bf(jnp.concatenate([vp_ref[:, ks], vc_ref[:, ks]], axis=0)) for ks in kss]
        s = [jnp.where(allowed, _dot(_swa_rows(q_ref, hk, BF16), k[hk], NT) * scale, -1e30) for hk in hks]
        sink = [_swa_per_row([sink_ref[0, hk * ATTN_GROUP + g] for g in range(ATTN_GROUP)]) for hk in hks]
        m = [jnp.maximum(jnp.max(s[hk], axis=-1, keepdims=True), sink[hk]) for hk in hks]
        p = [jnp.exp(s[hk] - m[hk]) for hk in hks]
        l = [jnp.sum(p[hk], axis=-1, keepdims=True) + jnp.exp(sink[hk] - m[hk]) for hk in hks]
        o = [_dot(_bf(p[hk] / l[hk]), v[hk]).astype(o_ref.dtype) for hk in hks]
        for hk in hks:
            lse = m[hk] + jnp.log(l[hk])
            for g in range(ATTN_GROUP):
                h = hk * ATTN_GROUP + g
                o_ref[:, hd * h:hd * (h + 1)] = o[hk][blk * g:blk * (g + 1)]
                lse_ref[:, h:h + 1] = lse[blk * g:blk * (g + 1)]

    return _hosted_call(
        body, name=name, grid=(T // blk,), in_specs=_swa_specs(),
        out_specs=[pl.BlockSpec((blk, 512), lambda n: (n, 0)), pl.BlockSpec((blk, ATTN_Q_HEADS), lambda n: (n, 0))],
        out_shape=[jax.ShapeDtypeStruct((T, 512), BF16), jax.ShapeDtypeStruct((T, ATTN_Q_HEADS), F32)],
        scratch=[], args=(sinks, z, z, z, z, z), semantics=("parallel",), exchange=exchange)


def _swa_bwd(z, sinks, dcat, lse, name):
    T = z.shape[0]
    blk, hd = ATTN_BLOCK, ATTN_HEAD_DIM
    scale = hd ** -0.5
    group = ATTN_Q_HEADS // ATTN_KV_HEADS

    def body(sink_ref, q_ref, kp_ref, kc_ref, vp_ref, vc_ref, do_ref, lse_ref,
             dq_ref, dka_ref, dkb_ref, dva_ref, dvb_ref, dsink_ref):
        @pl.when(pl.program_id(0) == 0)
        def _():
            dsink_ref[...] = jnp.zeros_like(dsink_ref)

        allowed = _swa_mask(pl.program_id(0))
        lane = lax.broadcasted_iota(jnp.int32, (1, ATTN_Q_HEADS), 1)
        dsink = jnp.zeros((1, ATTN_Q_HEADS), F32)
        hks = range(ATTN_KV_HEADS)
        kss = [slice(hd * hk, hd * hk + hd) for hk in hks]
        k = [_bf(jnp.concatenate([kp_ref[:, ks], kc_ref[:, ks]], axis=0)) for ks in kss]
        v = [_bf(jnp.concatenate([vp_ref[:, ks], vc_ref[:, ks]], axis=0)) for ks in kss]
        qs = [_swa_rows(q_ref, hk, BF16) for hk in hks]
        dos = [_swa_rows(do_ref, hk, BF16) for hk in hks]
        lse = [jnp.concatenate([lse_ref[:, hk * group + g:hk * group + g + 1] for g in range(group)], axis=0)
               for hk in hks]
        s = [_dot(qs[hk], k[hk], NT) * scale for hk in hks]
        dp = [_dot(dos[hk], v[hk], NT) for hk in hks]
        p = [jnp.where(allowed, jnp.exp(jnp.where(allowed, s[hk], -1e30) - lse[hk]), 0.0) for hk in hks]
        delta = [jnp.sum(p[hk] * dp[hk], axis=-1, keepdims=True) for hk in hks]
        ds = [_bf(p[hk] * (dp[hk] - delta[hk]) * scale) for hk in hks]
        dq = [_dot(ds[hk], k[hk]).astype(dq_ref.dtype) for hk in hks]
        dk = [_dot(ds[hk], qs[hk], TN) for hk in hks]
        dv = [_dot(_bf(p[hk]), dos[hk], TN) for hk in hks]
        for hk in hks:
            sink = _swa_per_row([sink_ref[0, hk * group + g] for g in range(group)])
            sink_part = jnp.exp(sink - lse[hk]) * delta[hk]
            for g in range(group):
                h = hk * group + g
                dq_ref[:, hd * h:hd * (h + 1)] = dq[hk][blk * g:blk * (g + 1)]
                dsink = dsink + jnp.where(lane == h, -jnp.sum(sink_part[blk * g:blk * (g + 1)]), 0.0)
            dkb_ref[:, kss[hk]] = dk[hk][:blk]
            dka_ref[:, kss[hk]] = dk[hk][blk:]
            dvb_ref[:, kss[hk]] = dv[hk][:blk]
            dva_ref[:, kss[hk]] = dv[hk][blk:]
        dsink_ref[...] += dsink

    kv_out = pl.BlockSpec((blk, 128), lambda n: (n, 0))
    return pl.pallas_call(
        body, name=name, grid=(T // blk,),
        in_specs=_swa_specs() + [pl.BlockSpec((blk, 512), lambda n: (n, 0)),
                                 pl.BlockSpec((blk, ATTN_Q_HEADS), lambda n: (n, 0))],
        out_specs=[pl.BlockSpec((blk, 512), lambda n: (n, 0)), kv_out, kv_out, kv_out, kv_out,
                   pl.BlockSpec((1, ATTN_Q_HEADS), lambda n: (0, 0))],
        out_shape=[jax.ShapeDtypeStruct((T, 512), BF16)] + [jax.ShapeDtypeStruct((T, 128), F32)] * 4
        + [jax.ShapeDtypeStruct((1, ATTN_Q_HEADS), F32)],
        compiler_params=_params("arbitrary"),
    )(sinks, z, z, z, z, z, dcat, lse)


def _assemble_dz(dq_a, dka, dkb, dva, dvb, dqr, dfr, dir_, dgr, name):
    T = dq_a.shape[0]
    blk = ATTN_BLOCK
    nb = T // blk

    def body(dq_ref, dka_ref, dkb_ref, dva_ref, dvb_ref, dqr_ref, dfr_ref, dir_ref, dgr_ref, o_ref):
        has_next = pl.program_id(0) < nb - 1
        o_ref[:, 0:512] = dq_ref[...]
        o_ref[:, 512:640] = (dka_ref[...] + jnp.where(has_next, dkb_ref[...], 0.0)).astype(o_ref.dtype)
        o_ref[:, 640:768] = (dva_ref[...] + jnp.where(has_next, dvb_ref[...], 0.0)).astype(o_ref.dtype)
        o_ref[:, 768:1280] = dqr_ref[...]
        o_ref[:, 1280:1792] = dfr_ref[...]
        o_ref[:, 1792:2304] = dir_ref[...]
        o_ref[:, 2304:2816] = dgr_ref[...]

    cur = lambda w: pl.BlockSpec((blk, w), lambda n: (n, 0))
    nxt = pl.BlockSpec((blk, 128), lambda n: (jnp.minimum(n + 1, nb - 1), 0))
    return pl.pallas_call(
        body, name=name, grid=(nb,),
        in_specs=[cur(512), cur(128), nxt, cur(128), nxt, cur(512), cur(512), cur(512), cur(512)],
        out_specs=pl.BlockSpec((blk, 2816), lambda n: (n, 0)),
        out_shape=jax.ShapeDtypeStruct((T, 2816), BF16), compiler_params=_params("parallel"),
    )(dq_a, dka, dkb, dva, dvb, dqr, dfr, dir_, dgr)


HGRN_ROWS = 512


def _hgrn_consts():
    c = HGRN_CHUNK
    r = lax.broadcasted_iota(jnp.int32, (c, c), 0)
    s = lax.broadcasted_iota(jnp.int32, (c, c), 1)
    rcol = lax.broadcasted_iota(jnp.int32, (c, 1), 0)
    same_block, upper = [], []
    for m in HGRN_LEVELS:
        same_block.append((r & ~(2 * m - 1)) == (s & ~(2 * m - 1)))
        upper.append((rcol & (2 * m - 1)) >= m)
    cum_mat = jnp.where(s <= r, 1.0, 0.0).astype(BF16)
    rev_mat = jnp.where(s >= r, 1.0, 0.0).astype(BF16)
    return cum_mat, rev_mat, r == s, same_block, upper, rcol & 3


def _hgrn_level_decay(g, b, m, pos4):
    c = HGRN_CHUNK
    if m == 1:
        return jnp.exp(jnp.where((pos4 & 1) == 1, g, 0.0))
    if m == 2:
        after, before = pltpu.roll(g, c - 1, 0), pltpu.roll(g, 1, 0)
        return jnp.exp(jnp.where(pos4 == 0, after, jnp.where(pos4 == 1, 0.0, jnp.where(pos4 == 2, g, g + before))))
    b3 = b.reshape(c // (2 * m), 2 * m, HGRN_DIM)
    bref = jnp.broadcast_to(b3[:, m - 1:m, :], b3.shape).reshape(c, HGRN_DIM)
    return jnp.exp(-jnp.abs(b - bref))


def _split3(x):
    hi = _bf(x)
    r1 = x - hi.astype(F32)
    mid = _bf(r1)
    lo = _bf(r1 - mid.astype(F32))
    return jnp.concatenate([hi, mid, lo], axis=1)


def _dot_hilo(a, b):
    r, c = a.shape[0], b.shape[1]
    a_hi, b_hi = _bf(a), _bf(b)
    a2 = jnp.concatenate([a_hi, _bf(a - a_hi.astype(F32))], axis=0)
    b2 = jnp.concatenate([b_hi, _bf(b - b_hi.astype(F32))], axis=1)
    y = _dot(a2, b2)
    return y[:r, :c] + y[:r, c:] + y[r:, :c]


def _fold3(y):
    w = y.shape[1] // 3
    return y[:, :w] + y[:, w:2 * w] + y[:, 2 * w:]


def _hgrn_gates(qr, fr, lb):
    sq = _sigmoid(qr)
    q = qr * sq * (HGRN_DIM ** -0.5)
    sf = _sigmoid(fr)
    f = lb + (1.0 - lb) * sf
    k = (1.0 - lb) * _sigmoid(-fr)
    return q, sq, sf, f, k, jnp.log(f)


def _hgrn_intra(q, k, g, b, consts):
    _, _, eye, same_block, upper, pos4 = consts
    heads = range(len(q))
    a = [jnp.where(eye, _dot(_bf(q[hh]), _bf(k[hh]), NT), 0.0) for hh in heads]
    saved = [[] for _ in heads]
    for i, m in enumerate(HGRN_LEVELS):
        up = upper[i]
        e = [_hgrn_level_decay(g[hh], b[hh], m, pos4) for hh in heads]
        qt = [jnp.where(up, q[hh] * e[hh], 0.0) for hh in heads]
        kt = [jnp.where(up, 0.0, k[hh] * e[hh]) for hh in heads]
        p = [_dot(_bf(qt[hh]), _bf(kt[hh]), NT) for hh in heads]
        for hh in heads:
            a[hh] = a[hh] + jnp.where(same_block[i], p[hh], 0.0)
            saved[hh].append((e[hh], qt[hh], kt[hh]))
    return a, saved


def _hgrn_specs(tb, nb, rev):
    tmap = (lambda t: nb - 1 - t) if rev else (lambda t: t)
    assert HGRN_PAIR == HGRN_HEADS
    return [pl.BlockSpec((tb, 2816), lambda h, t: (tmap(t), 0)),
            pl.BlockSpec((1, HGRN_PAIR * HGRN_DIM), lambda h, t: (0, h)),
            pl.BlockSpec((1, HGRN_DIM), lambda h, t: (0, 0))]


def _hgrn_z(z_ref, sl, base, head):
    return z_ref[sl, base + HGRN_DIM * head:base + HGRN_DIM * (head + 1)].astype(F32)


def _hgrn_fwd(z, lb, onw, name, exchange=None):
    T = z.shape[0]
    tb = min(HGRN_ROWS, T)
    nb, c, nc = T // tb, HGRN_CHUNK, min(HGRN_ROWS, T) // HGRN_CHUNK

    def body(z_ref, lb_ref, onw_ref, rec_ref, o_ref, st_ref, state):
        @pl.when(pl.program_id(1) == 0)
        def _():
            state[...] = jnp.zeros_like(state)

        consts = _hgrn_consts()
        lbv = lb_ref[...]
        onwv = onw_ref[...]

        def chunk(ci, carry):
            sl = pl.ds(pl.multiple_of(ci * c, c), c)
            heads = range(HGRN_PAIR)
            lss = [slice(HGRN_DIM * hh, HGRN_DIM * (hh + 1)) for hh in heads]
            gates = [_hgrn_gates(_hgrn_z(z_ref, sl, Z_Q, hh), _hgrn_z(z_ref, sl, Z_F, hh), lbv[:, lss[hh]])
                     for hh in heads]
            q, k, g = [t[0] for t in gates], [t[4] for t in gates], [t[5] for t in gates]
            v = [_bf(_hgrn_z(z_ref, sl, Z_I, hh)) for hh in heads]
            b = [_fold3(_dot(consts[0], _split3(g[hh]))) for hh in heads]
            a, _ = _hgrn_intra(q, k, g, b, consts)
            st = [state[hh] for hh in heads]
            for hh in heads:
                st_ref[hh, ci] = st[hh]
            bl = [b[hh][c - 1:c, :] for hh in heads]
            o_state = [_dot(_bf(q[hh] * jnp.exp(b[hh])), _bf(st[hh]), NT) for hh in heads]
            kv = [_dot(v[hh], _bf(k[hh] * jnp.exp(bl[hh] - b[hh])), TN) for hh in heads]
            o = [_dot(_bf(a[hh]), v[hh]) + o_state[hh] for hh in heads]
            for hh in heads:
                state[hh] = st[hh] * jnp.exp(bl[hh]) + kv[hh]
                o_ref[sl, lss[hh]] = o[hh]
                oh, _ = _rms(o[hh])
                gr = _hgrn_z(z_ref, sl, Z_G, hh)
                rec_ref[sl, lss[hh]] = (oh * onwv * (gr * _sigmoid(gr))).astype(rec_ref.dtype)
            return carry

        lax.fori_loop(0, nc, chunk, 0)

    in_specs = _hgrn_specs(tb, nb, False)
    out_blk = pl.BlockSpec((tb, HGRN_PAIR * HGRN_DIM), lambda h, t: (t, h))
    return _hosted_call(
        body, name=name, grid=(HGRN_HEADS // HGRN_PAIR, nb), in_specs=in_specs,
        out_specs=[out_blk, out_blk, pl.BlockSpec((HGRN_PAIR, nc, HGRN_DIM, HGRN_DIM), lambda h, t: (h, t, 0, 0))],
        out_shape=[jax.ShapeDtypeStruct((T, 512), BF16), jax.ShapeDtypeStruct((T, 512), F32),
                   jax.ShapeDtypeStruct((HGRN_HEADS, T // c, HGRN_DIM, HGRN_DIM), F32)],
        scratch=[pltpu.VMEM((HGRN_PAIR, HGRN_DIM, HGRN_DIM), F32)], args=(z, lb, onw),
        semantics=("parallel", "arbitrary"), exchange=exchange)


def _hgrn_bwd(z, lb, onw, o, states, dcat, name, exchange=None):
    T = z.shape[0]
    tb = min(HGRN_ROWS, T)
    nb, c, nc = T // tb, HGRN_CHUNK, min(HGRN_ROWS, T) // HGRN_CHUNK

    def body(z_ref, lb_ref, onw_ref, o_ref, st_ref, drec_ref,
             dqr_ref, dfr_ref, dir_ref, dgr_ref, dlb_ref, donw_ref, dstate):
        @pl.when(pl.program_id(1) == 0)
        def _():
            dstate[...] = jnp.zeros_like(dstate)
            dlb_ref[...] = jnp.zeros_like(dlb_ref)

        @pl.when((pl.program_id(0) == 0) & (pl.program_id(1) == 0))
        def _():
            donw_ref[...] = jnp.zeros_like(donw_ref)

        consts = _hgrn_consts()
        rev_mat, eye, same_block, upper = consts[1:5]
        lbv = lb_ref[...]
        onwv = onw_ref[...]
        last = lax.broadcasted_iota(jnp.int32, (c, 1), 0) == c - 1

        def chunk(i, carry):
            ci = nc - 1 - i
            sl = pl.ds(pl.multiple_of(ci * c, c), c)
            hs = range(HGRN_PAIR)
            lss = [slice(HGRN_DIM * hh, HGRN_DIM * (hh + 1)) for hh in hs]
            qr = [_hgrn_z(z_ref, sl, Z_Q, hh) for hh in hs]
            gates = [_hgrn_gates(qr[hh], _hgrn_z(z_ref, sl, Z_F, hh), lbv[:, lss[hh]]) for hh in hs]
            q, sq, sf, f, k, g = ([t[j] for t in gates] for j in range(6))
            v = [_bf(_hgrn_z(z_ref, sl, Z_I, hh)) for hh in hs]
            b = [_fold3(_dot(consts[0], _split3(g[hh]))) for hh in hs]
            a, saved = _hgrn_intra(q, k, g, b, consts)
            st = [st_ref[hh, ci] for hh in hs]
            dst = [dstate[hh] for hh in hs]

            gr = [_hgrn_z(z_ref, sl, Z_G, hh) for hh in hs]
            sg = [_sigmoid(gr[hh]) for hh in hs]
            norm = [_rms(o_ref[sl, ls]) for ls in lss]
            oh, r = [t[0] for t in norm], [t[1] for t in norm]
            drec = [drec_ref[sl, ls].astype(F32) for ls in lss]
            don = [drec[hh] * (gr[hh] * sg[hh]) for hh in hs]
            do = [_bf(_rms_bwd(don[hh] * onwv, oh[hh], r[hh])) for hh in hs]
            donw = jnp.sum(don[0] * oh[0], axis=0, keepdims=True)
            for hh in hs:
                dgr_ref[sl, lss[hh]] = (drec[hh] * oh[hh] * onwv
                                        * (sg[hh] * (1.0 + gr[hh] * (1.0 - sg[hh])))).astype(dgr_ref.dtype)
                if hh:
                    donw = donw + jnp.sum(don[hh] * oh[hh], axis=0, keepdims=True)
            donw_ref[...] += donw

            eb = [jnp.exp(b[hh]) for hh in hs]
            bl = [b[hh][c - 1:c, :] for hh in hs]
            ebl = [jnp.exp(bl[hh]) for hh in hs]
            ekb = [jnp.exp(bl[hh] - b[hh]) for hh in hs]
            qe = [q[hh] * eb[hh] for hh in hs]
            ke = [k[hh] * ekb[hh] for hh in hs]
            da = [_dot(do[hh], v[hh], NT) for hh in hs]
            dat = [_dot(v[hh], do[hh], NT) for hh in hs]
            dqe = [_dot(do[hh], _bf(st[hh])) for hh in hs]
            dke = [_dot(v[hh], _bf(dst[hh])) for hh in hs]
            dv_a = [_dot(_bf(a[hh]), do[hh], TN) for hh in hs]
            dv_s = [_dot(_bf(ke[hh]), _bf(dst[hh]), NT) for hh in hs]
            dst_in = [_dot(do[hh], _bf(qe[hh]), TN) for hh in hs]
            dad = [jnp.sum(jnp.where(eye, da[hh], 0.0), axis=1, keepdims=True) for hh in hs]
            dq = [dqe[hh] * eb[hh] + dad[hh] * k[hh] for hh in hs]
            dk = [dke[hh] * ekb[hh] + dad[hh] * q[hh] for hh in hs]
            db_last = [jnp.sum(dke[hh] * ke[hh], axis=0, keepdims=True)
                       + jnp.sum(dst[hh] * st[hh], axis=0, keepdims=True) * ebl[hh] for hh in hs]
            for hh in hs:
                dstate[hh] = dst[hh] * ebl[hh] + dst_in[hh]
                dir_ref[sl, lss[hh]] = (dv_a[hh] + dv_s[hh]).astype(dir_ref.dtype)
            for lvl in range(len(HGRN_LEVELS)):
                xq = [_dot_hilo(jnp.where(same_block[lvl], da[hh], 0.0), saved[hh][lvl][2]) for hh in hs]
                xk = [_dot_hilo(jnp.where(same_block[lvl], dat[hh], 0.0), saved[hh][lvl][1]) for hh in hs]
                for hh in hs:
                    e = saved[hh][lvl][0]
                    dq[hh] = dq[hh] + jnp.where(upper[lvl], xq[hh] * e, 0.0)
                    dk[hh] = dk[hh] + jnp.where(upper[lvl], 0.0, xk[hh] * e)
            db = [q[hh] * dq[hh] - k[hh] * dk[hh] + jnp.where(last, db_last[hh], 0.0) for hh in hs]
            dg = [_fold3(_dot(rev_mat, _split3(db[hh]))) for hh in hs]

            for hh in hs:
                ls = lss[hh]
                dqr_ref[sl, ls] = (dq[hh] * (HGRN_DIM ** -0.5)
                                   * (sq[hh] * (1.0 + qr[hh] * (1.0 - sq[hh])))).astype(dqr_ref.dtype)
                dfk = dg[hh] / f[hh] - dk[hh]
                dfr_ref[sl, ls] = ((1.0 - lbv[:, ls]) * sf[hh] * (1.0 - sf[hh]) * dfk).astype(dfr_ref.dtype)
                dlb_ref[:, ls] += jnp.sum((1.0 - sf[hh]) * dfk, axis=0, keepdims=True)
            return carry

        lax.fori_loop(0, nc, chunk, 0)

    in_specs = _hgrn_specs(tb, nb, True)
    rblk = pl.BlockSpec((tb, HGRN_PAIR * HGRN_DIM), lambda h, t: (nb - 1 - t, h))
    in_specs = in_specs + [
        rblk,
        pl.BlockSpec((HGRN_PAIR, nc, HGRN_DIM, HGRN_DIM), lambda h, t: (h, nb - 1 - t, 0, 0)),
        pl.BlockSpec((tb, HGRN_PAIR * HGRN_DIM), lambda h, t: (nb - 1 - t, 4 // HGRN_PAIR + h)),
    ]
    return _hosted_call(
        body, name=name, grid=(HGRN_HEADS // HGRN_PAIR, nb), in_specs=in_specs,
        out_specs=[rblk, rblk, rblk, rblk, pl.BlockSpec((1, HGRN_PAIR * HGRN_DIM), lambda h, t: (0, h)),
                   pl.BlockSpec((1, HGRN_DIM), lambda h, t: (0, 0))],
        out_shape=[jax.ShapeDtypeStruct((T, 512), BF16)] * 4
        + [jax.ShapeDtypeStruct((1, 512), F32), jax.ShapeDtypeStruct((1, HGRN_DIM), F32)],
        scratch=[pltpu.VMEM((HGRN_PAIR, HGRN_DIM, HGRN_DIM), F32)], args=(z, lb, onw, o, states, dcat),
        semantics=("arbitrary", "arbitrary"), exchange=exchange)


def _lower_bound(logits, name):
    def body(l_ref, lb_ref):
        l0, l1 = l_ref[0:1, :], l_ref[1:2, :]
        m = jnp.maximum(l0, l1)
        e0, e1 = jnp.exp(l0 - m), jnp.exp(l1 - m)
        lb_ref[...] = e0 / (e0 + e1)

    return pl.pallas_call(
        body, name=name, out_shape=jax.ShapeDtypeStruct((1, logits.shape[1]), F32),
    )(logits)


def _lower_bound_bwd(lb, dlb, name):
    def body(lb_ref, dlb_ref, dl_ref):
        p = lb_ref[...]
        d0 = dlb_ref[...] * p * (1.0 - p)
        dl_ref[0:1, :] = d0
        dl_ref[1:2, :] = -d0

    return pl.pallas_call(
        body, name=name, out_shape=jax.ShapeDtypeStruct((2, lb.shape[1]), F32),
    )(lb, dlb)


CA_ROWS = 512


def _ca_fwd(q, k, v, name):
    T, W = q.shape
    M = k.shape[0]
    tq = min(CA_ROWS, T)
    scale = CA_HEAD_DIM ** -0.5

    def body(q_ref, k_ref, v_ref, o_ref):
        for h in range(CA_HEADS):
            hs = slice(CA_HEAD_DIM * h, CA_HEAD_DIM * (h + 1))
            s = _dot(q_ref[:, hs], k_ref[:, hs], NT) * scale
            p = jnp.exp(s - jnp.max(s, axis=-1, keepdims=True))
            p = p / jnp.sum(p, axis=-1, keepdims=True)
            o_ref[:, hs] = _dot(_bf(p), v_ref[:, hs]).astype(o_ref.dtype)

    full = pl.BlockSpec((M, W), lambda i: (0, 0))
    return pl.pallas_call(
        body, name=name, grid=(T // tq,), in_specs=[_row_spec(tq, W), full, full], out_specs=_row_spec(tq, W),
        out_shape=jax.ShapeDtypeStruct((T, W), BF16), compiler_params=_params("parallel"),
    )(q, k, v)


def _ca_bwd(q, k, v, do, name):
    T, W = q.shape
    M = k.shape[0]
    tq = min(CA_ROWS, T)
    scale = CA_HEAD_DIM ** -0.5

    def body(q_ref, k_ref, v_ref, do_ref, dq_ref, dk_ref, dv_ref):
        @pl.when(pl.program_id(0) == 0)
        def _():
            dk_ref[...] = jnp.zeros_like(dk_ref)
            dv_ref[...] = jnp.zeros_like(dv_ref)

        for h in range(CA_HEADS):
            hs = slice(CA_HEAD_DIM * h, CA_HEAD_DIM * (h + 1))
            qh, kh, vh, doh = q_ref[:, hs], k_ref[:, hs], v_ref[:, hs], do_ref[:, hs]
            s = _dot(qh, kh, NT) * scale
            p = jnp.exp(s - jnp.max(s, axis=-1, keepdims=True))
            p = p / jnp.sum(p, axis=-1, keepdims=True)
            dp = _dot(doh, vh, NT)
            ds = _bf(p * (dp - jnp.sum(p * dp, axis=-1, keepdims=True)) * scale)
            dq_ref[:, hs] = _dot(ds, kh).astype(dq_ref.dtype)
            dk_ref[:, hs] += _dot(ds, qh, TN)
            dv_ref[:, hs] += _dot(_bf(p), doh, TN)

    full = pl.BlockSpec((M, W), lambda i: (0, 0))
    return pl.pallas_call(
        body, name=name, grid=(T // tq,), in_specs=[_row_spec(tq, W), full, full, _row_spec(tq, W)],
        out_specs=[_row_spec(tq, W), full, full],
        out_shape=[jax.ShapeDtypeStruct((T, W), BF16), jax.ShapeDtypeStruct((M, W), F32), jax.ShapeDtypeStruct((M, W), F32)],
        compiler_params=_params("arbitrary"),
    )(q, k, v, do)


FFN_ROWS = 256
FFN_COLS = 1408
GELU_C0 = 0.7978845608028654
GELU_C1 = 0.044715


def _gelu(x):
    t = jnp.tanh(GELU_C0 * (x + GELU_C1 * x * x * x))
    return 0.5 * x * (1.0 + t), t


def _gelu_grad(x, t):
    return 0.5 * (1.0 + t) + 0.5 * x * (1.0 - t * t) * GELU_C0 * (1.0 + 3.0 * GELU_C1 * x * x)


def _shift_down(cur, halo, first, tb):
    row = lax.broadcasted_iota(jnp.int32, (tb, 1), 0)
    h6 = jnp.where(first, 0.0, halo[6:7])
    h7 = jnp.where(first, 0.0, halo[7:8])
    u1 = jnp.where(row == 0, h7, pltpu.roll(cur, 1, 0))
    u2 = jnp.where(row == 0, h6, jnp.where(row == 1, h7, pltpu.roll(cur, 2, 0)))
    return u1, u2


def _conv(u_ref, halo_ref, w_ref, b_ref, half, first, tb):
    cur = u_ref[half]
    u1, u2 = _shift_down(cur, halo_ref[half], first, tb)
    w = w_ref[...]
    return w[0:1] * u2 + w[1:2] * u1 + w[2:3] * cur + b_ref[...], cur, u1, u2


def _ffn_specs(tb, tc, rows_first):
    nj = D_FF // tc
    rc = (lambda a, b: (a, b)) if rows_first else (lambda a, b: (b, a))
    def at(f):
        return lambda a, b: f(*rc(a, b))
    blk = pl.BlockSpec((2, tb, tc), at(lambda t, j: (0, t, j)))
    halo = pl.BlockSpec((2, 8, tc), at(lambda t, j: (0, jnp.maximum(t * (tb // 8) - 1, 0), j)))
    wg = pl.BlockSpec((3, tc), at(lambda t, j: (0, j)))
    wv = pl.BlockSpec((3, tc), at(lambda t, j: (0, j + nj)))
    bg = pl.BlockSpec((1, tc), at(lambda t, j: (0, j)))
    bv = pl.BlockSpec((1, tc), at(lambda t, j: (0, j + nj)))
    flat = pl.BlockSpec((tb, tc), at(lambda t, j: (t, j)))
    return blk, halo, wg, wv, bg, bv, flat


def _glu_fwd(u, cw, cb, name):
    T = u.shape[1]
    tb, tc = min(FFN_ROWS, T), FFN_COLS

    def body(u_ref, halo_ref, wg_ref, wv_ref, bg_ref, bv_ref, a_ref):
        first = pl.program_id(0) == 0
        cg = _conv(u_ref, halo_ref, wg_ref, bg_ref, 0, first, tb)[0]
        cv = _conv(u_ref, halo_ref, wv_ref, bv_ref, 1, first, tb)[0]
        a_ref[...] = (_gelu(cg)[0] * cv).astype(a_ref.dtype)

    blk, halo, wg, wv, bg, bv, flat = _ffn_specs(tb, tc, True)
    return pl.pallas_call(
        body, name=name, grid=(T // tb, D_FF // tc), in_specs=[blk, halo, wg, wv, bg, bv], out_specs=flat,
        out_shape=jax.ShapeDtypeStruct((T, D_FF), BF16), compiler_params=_params("parallel", "parallel"),
    )(u, u, cw, cw, cb, cb)


def _glu_bwd(u, cw, cb, da, name, exchange=None):
    T = u.shape[1]
    tb, tc = min(FFN_ROWS, T), FFN_COLS

    def body(u_ref, halo_ref, wg_ref, wv_ref, bg_ref, bv_ref, da_ref, dc_ref, db_ref, dw_ref):
        first = pl.program_id(1) == 0

        @pl.when(first)
        def _():
            db_ref[...] = jnp.zeros_like(db_ref)
            dw_ref[...] = jnp.zeros_like(dw_ref)

        cg, ug, ug1, ug2 = _conv(u_ref, halo_ref, wg_ref, bg_ref, 0, first, tb)
        cv, uv, uv1, uv2 = _conv(u_ref, halo_ref, wv_ref, bv_ref, 1, first, tb)
        da = da_ref[...]
        gl, t = _gelu(cg)
        dcg = da * cv * _gelu_grad(cg, t)
        dcv = da * gl
        dc_ref[0] = dcg
        dc_ref[1] = dcv
        for half, dc, taps in ((0, dcg, (ug2, ug1, ug)), (1, dcv, (uv2, uv1, uv))):
            db_ref[half] += jnp.sum(dc, axis=0, keepdims=True)
            for tap in range(3):
                dw_ref[half, tap:tap + 1, :] += jnp.sum(dc * taps[tap], axis=0, keepdims=True)

    blk, halo, wg, wv, bg, bv, flat = _ffn_specs(tb, tc, False)
    return _hosted_call(
        body, name=name, grid=(D_FF // tc, T // tb), in_specs=[blk, halo, wg, wv, bg, bv, flat],
        out_specs=[blk, pl.BlockSpec((2, 1, tc), lambda j, t: (0, 0, j)), pl.BlockSpec((2, 3, tc), lambda j, t: (0, 0, j))],
        out_shape=[jax.ShapeDtypeStruct((2, T, D_FF), F32), jax.ShapeDtypeStruct((2, 1, D_FF), F32),
                   jax.ShapeDtypeStruct((2, 3, D_FF), F32)],
        scratch=[], args=(u, u, cw, cw, cb, cb, da), semantics=("parallel", "arbitrary"), exchange=exchange)


def _conv_bwd(dc, cw, name):
    T = dc.shape[1]
    tb, tc = min(FFN_ROWS, T), FFN_COLS
    nt, nj = T // tb, D_FF // tc

    def body(dc_ref, halo_ref, wg_ref, wv_ref, du_ref):
        last = pl.program_id(0) == nt - 1
        row = lax.broadcasted_iota(jnp.int32, (tb, 1), 0)
        for half, w_ref in ((0, wg_ref), (1, wv_ref)):
            cur = dc_ref[half]
            halo = halo_ref[half]
            h0 = jnp.where(last, 0.0, halo[0:1])
            h1 = jnp.where(last, 0.0, halo[1:2])
            d1 = jnp.where(row == tb - 1, h0, pltpu.roll(cur, tb - 1, 0))
            d2 = jnp.where(row == tb - 1, h1, jnp.where(row == tb - 2, h0, pltpu.roll(cur, tb - 2, 0)))
            w = w_ref[...]
            du_ref[half] = (w[2:3] * cur + w[1:2] * d1 + w[0:1] * d2).astype(du_ref.dtype)

    blk = pl.BlockSpec((2, tb, tc), lambda t, j: (0, t, j))
    halo = pl.BlockSpec((2, 8, tc), lambda t, j: (0, jnp.minimum((t + 1) * (tb // 8), T // 8 - 1), j))
    wg = pl.BlockSpec((3, tc), lambda t, j: (0, j))
    wv = pl.BlockSpec((3, tc), lambda t, j: (0, j + nj))
    return pl.pallas_call(
        body, name=name, grid=(nt, nj), in_specs=[blk, halo, wg, wv], out_specs=blk,
        out_shape=jax.ShapeDtypeStruct((2, T, D_FF), BF16), compiler_params=_params("parallel", "parallel"),
    )(dc, dc, cw, cw)


def _mesh_pos():
    return lax.axis_index("x"), lax.axis_index("y"), lax.axis_index("c")


def _peer(pos, k):
    return (pos[0] ^ ((k >> 2) & 1), pos[1] ^ ((k >> 1) & 1), pos[2] ^ (k & 1))


def _index(pos):
    return 4 * pos[0] + 2 * pos[1] + pos[2]


class _Exchange:
    def __init__(self, kind, buf, relay=False):
        assert kind in ("gather", "scatter") and not (relay and kind == "scatter")
        self.kind, self.buf, self.relay = kind, buf, relay
        self.out_shape = jax.ShapeDtypeStruct(((N_DEV,) + buf.shape) if kind == "gather" else buf.shape, buf.dtype)
        self.spec = pl.BlockSpec(memory_space=pl.ANY)
        self.scratch = [pltpu.SemaphoreType.DMA((N_DEV - 1,)), pltpu.SemaphoreType.DMA((N_DEV - 1,)),
                        pltpu.SemaphoreType.DMA]

    def _src(self, x_ref, dest):
        return x_ref if self.kind == "gather" else x_ref.at[dest]

    def _copies(self, x_ref, out_ref, send_sems, recv_sems, local_sem):
        pos = _mesh_pos()
        me = _index(pos)
        local = pltpu.make_async_copy(self._src(x_ref, me), out_ref.at[me], local_sem)
        sends, recvs = [], []
        for k in range(1, N_DEV):
            peer = _peer(pos, k)
            sends.append(pltpu.make_async_remote_copy(
                src_ref=self._src(x_ref, _index(peer)), dst_ref=out_ref.at[me], send_sem=send_sems.at[k - 1],
                recv_sem=recv_sems.at[k - 1], device_id=peer, device_id_type=pl.DeviceIdType.MESH))
            recvs.append(pltpu.make_async_remote_copy(
                src_ref=self._src(x_ref, me), dst_ref=out_ref.at[_index(peer)], send_sem=send_sems.at[k - 1],
                recv_sem=recv_sems.at[k - 1], device_id=peer, device_id_type=pl.DeviceIdType.MESH))
        return local, sends, recvs

    def _relay_copies(self, x_ref, out_ref, send_sems, recv_sems, local_sem):
        x, y, c = _mesh_pos()
        me, sibling = (x, y, c), (x, y, 1 - c)
        chips = [(1 - x, y), (x, 1 - y), (1 - x, 1 - y)]

        def copy(k, block, to, own=False):
            return pltpu.make_async_remote_copy(
                src_ref=x_ref if own else out_ref.at[_index(block)], dst_ref=out_ref.at[_index(block)],
                send_sem=send_sems.at[k], recv_sem=recv_sems.at[k], device_id=to, device_id_type=pl.DeviceIdType.MESH)

        local = pltpu.make_async_copy(x_ref, out_ref.at[_index(me)], local_sem)
        first = [copy(0, me, sibling, own=True)] + [copy(1 + j, me, (*chip, c), own=True) for j, chip in enumerate(chips)]
        landed = [copy(1 + j, (*chip, c), me) for j, chip in enumerate(chips)]
        passed = [copy(4 + j, (*chip, c), sibling) for j, chip in enumerate(chips)]
        from_sibling = [copy(0, sibling, me)] + [copy(4 + j, (*chip, 1 - c), me) for j, chip in enumerate(chips)]
        return local, first, landed, passed, from_sibling

    def start(self, *refs):
        if self.relay:
            local, first = self._relay_copies(*refs)[:2]
            local.start()
            for cp in first:
                cp.start()
            return
        local, sends, _ = self._copies(*refs)
        local.start()
        for cp in sends:
            cp.start()

    def finish(self, *refs):
        if self.relay:
            local, first, landed, passed, from_sibling = self._relay_copies(*refs)
            for got, forward in zip(landed, passed):
                got.wait_recv()
                forward.start()
            for cp in from_sibling:
                cp.wait_recv()
            for cp in first + passed:
                cp.wait_send()
            local.wait()
            return
        local, sends, recvs = self._copies(*refs)
        for cp in recvs:
            cp.wait_recv()
        for cp in sends:
            cp.wait_send()
        local.wait()


def _hosted_call(body, *, name, grid, in_specs, out_specs, out_shape, scratch, args, semantics, exchange=None):
    if exchange is None:
        return pl.pallas_call(
            body, name=name, grid=grid, in_specs=in_specs, out_specs=out_specs, out_shape=out_shape,
            scratch_shapes=scratch, compiler_params=_params(*semantics))(*args)
    n_in, n_out, n_scr = len(in_specs), len(out_specs), len(scratch)

    def hosted(*refs):
        ins, x_ref = refs[:n_in], refs[n_in]
        outs, land_ref = refs[n_in + 1:n_in + 1 + n_out], refs[n_in + 1 + n_out]
        rest = refs[n_in + n_out + 2:]
        sems = rest[n_scr:]
        ids = [pl.program_id(a) for a in range(len(grid))]
        first, last = ids[0] == 0, ids[0] == grid[0] - 1
        for a in range(1, len(grid)):
            first, last = first & (ids[a] == 0), last & (ids[a] == grid[a] - 1)

        @pl.when(first)
        def _():
            exchange.start(x_ref, land_ref, *sems)

        body(*ins, *outs, *rest[:n_scr])

        @pl.when(last)
        def _():
            exchange.finish(x_ref, land_ref, *sems)

    return pl.pallas_call(
        hosted, name=name, grid=grid, in_specs=list(in_specs) + [exchange.spec],
        out_specs=list(out_specs) + [exchange.spec], out_shape=list(out_shape) + [exchange.out_shape],
        scratch_shapes=list(scratch) + exchange.scratch, compiler_params=_params(*(["arbitrary"] * len(grid))),
    )(*args, exchange.buf)


def _exchange_alone(exchange, name):
    def body(x_ref, out_ref, send_sems, recv_sems, local_sem):
        exchange.start(x_ref, out_ref, send_sems, recv_sems, local_sem)
        exchange.finish(x_ref, out_ref, send_sems, recv_sems, local_sem)

    return pl.pallas_call(
        body, name=name, out_shape=exchange.out_shape, in_specs=[exchange.spec], out_specs=exchange.spec,
        scratch_shapes=exchange.scratch)(exchange.buf)


def _adamw(w, g, m, v):
    m = ADAM_B1 * m + (1.0 - ADAM_B1) * g
    v = ADAM_B2 * v + (1.0 - ADAM_B2) * (g * g)
    m_hat = m / (1.0 - ADAM_B1 ** ADAM_STEP)
    v_hat = v / (1.0 - ADAM_B2 ** ADAM_STEP)
    delta = -ADAM_LR * (m_hat / (jnp.sqrt(v_hat) + ADAM_EPS) + ADAM_WD * w)
    return delta, m, v


def _sum_rows(parts, r0, rows, name, wmv=None):
    C = parts.shape[2]
    tr = max(t for t in range(16, ROWS + 1, 16) if rows % t == 0 and r0 % t == 0)

    def total(p_ref):
        g = p_ref[0].astype(F32)
        for i in range(1, N_DEV):
            g = g + p_ref[i].astype(F32)
        return g

    p_spec = pl.BlockSpec((N_DEV, tr, C), lambda i: (0, r0 // tr + i, 0))
    if wmv is None:
        def body(p_ref, g_ref):
            g_ref[...] = total(p_ref)

        return pl.pallas_call(
            body, name=name, grid=(rows // tr,), in_specs=[p_spec], out_specs=_row_spec(tr, C),
            out_shape=jax.ShapeDtypeStruct((rows, C), F32), compiler_params=_params("parallel"))(parts)

    def body(p_ref, w_ref, m_ref, v_ref, g_ref, d_ref, mo_ref, vo_ref):
        g = total(p_ref)
        g_ref[0] = g
        d_ref[0], mo_ref[0], vo_ref[0] = _adamw(w_ref[0], g, m_ref[0], v_ref[0])

    blk = pl.BlockSpec((1, tr, C), lambda i: (0, i, 0))
    return pl.pallas_call(
        body, name=name, grid=(rows // tr,), in_specs=[p_spec, blk, blk, blk], out_specs=[blk] * 4,
        out_shape=[jax.ShapeDtypeStruct((1, rows, C), F32)] * 4, compiler_params=_params("parallel"))(parts, *wmv)


def _sum_parts(parts, name):
    _, R, C = parts.shape

    def body(p_ref, g_ref):
        g = p_ref[0]
        for i in range(1, N_DEV):
            g = g + p_ref[i]
        g_ref[...] = g

    return pl.pallas_call(body, name=name, out_shape=jax.ShapeDtypeStruct((R, C), F32))(parts)


def _adamw_call(w, g, m, v, name):
    _, R, C = w.shape
    tr = min(ROWS, R)

    def body(w_ref, g_ref, m_ref, v_ref, d_ref, mo_ref, vo_ref):
        d_ref[...], mo_ref[...], vo_ref[...] = _adamw(w_ref[...], g_ref[...], m_ref[...], v_ref[...])

    blk = pl.BlockSpec((1, tr, C), lambda i: (0, i, 0))
    return pl.pallas_call(
        body, name=name, grid=(R // tr,), in_specs=[blk] * 4, out_specs=[blk] * 3,
        out_shape=[jax.ShapeDtypeStruct(w.shape, F32)] * 3, compiler_params=_params("parallel"))(w, g, m, v)


NORMS = ("mix_pre_norm", "mix_post_norm", "ca_pre_norm", "mem_norm", "ca_post_norm", "ffn_pre_norm", "ffn_post_norm")
SMALL = ("mix_pre_norm", "attn_sinks", "hgrn_lb_logits", "hgrn_out_norm", "mix_post_norm", "ca_pre_norm", "mem_norm",
         "ca_post_norm", "ffn_pre_norm", "ffn_conv_w", "ffn_conv_b", "ffn_post_norm")
SMALL_ROWS = 40
ROW_LOGITS, ROW_MISC, ROW_CONV_B, ROW_CONV_W = 7, 8, 9, 15
LANE_SINKS, LANE_LOSS = 128, 256
FF_PIECES = ((0, 1024), (1024, 2048), (2048, D_FF))


def _pack_small(norm_grads, dlogits, donw, dsinks, loss, d_cb, d_cw, name):
    def body(*refs):
        norm_refs = refs[:len(NORMS)]
        dl_ref, donw_ref, dsink_ref, loss_ref, cb_ref, cw_ref, out_ref = refs[len(NORMS):]
        out_ref[...] = jnp.zeros_like(out_ref)
        for i, ref in enumerate(norm_refs):
            out_ref[i:i + 1, :] = ref[...]
        out_ref[ROW_LOGITS:ROW_LOGITS + 1, 0:512] = dl_ref[0:1, :]
        out_ref[ROW_LOGITS:ROW_LOGITS + 1, 512:1024] = dl_ref[1:2, :]
        out_ref[ROW_MISC:ROW_MISC + 1, 0:HGRN_DIM] = donw_ref[...]
        out_ref[ROW_MISC:ROW_MISC + 1, LANE_SINKS:LANE_SINKS + ATTN_Q_HEADS] = dsink_ref[...]
        out_ref[ROW_MISC:ROW_MISC + 1, LANE_LOSS:LANE_LOSS + LANE] = loss_ref[...]
        for h in range(2):
            for j, (c0, c1) in enumerate(FF_PIECES):
                r = ROW_CONV_B + 3 * h + j
                out_ref[r:r + 1, 0:c1 - c0] = cb_ref[h, :, c0:c1]
                for t in range(3):
                    r = ROW_CONV_W + 3 * (3 * h + t) + j
                    out_ref[r:r + 1, 0:c1 - c0] = cw_ref[h, t:t + 1, c0:c1]

    return pl.pallas_call(
        body, name=name, out_shape=jax.ShapeDtypeStruct((SMALL_ROWS, 1024), F32),
    )(*norm_grads, dlogits, donw, dsinks, loss, d_cb, d_cw)


def _adamw_small(total, g_conv_w, w, m, v, name):
    n = len(SMALL)

    def body(*refs):
        t_ref, gcw_ref = refs[:2]
        w_refs, m_refs, v_refs = (dict(zip(SMALL, refs[2 + n * i:2 + n * (i + 1)])) for i in range(3))
        outs = refs[2 + 3 * n:]
        loss_ref = outs[0]
        g_refs, d_refs, mo_refs, vo_refs = (dict(zip(SMALL, outs[1 + n * i:1 + n * (i + 1)])) for i in range(4))
        loss_ref[...] = t_ref[ROW_MISC:ROW_MISC + 1, LANE_LOSS:LANE_LOSS + 1]

        def step(nm, idx, g):
            g_refs[nm][idx] = g
            d_refs[nm][idx], mo_refs[nm][idx], vo_refs[nm][idx] = _adamw(w_refs[nm][idx], g, m_refs[nm][idx], v_refs[nm][idx])

        everything = (slice(None), slice(None))
        for i, nm in enumerate(NORMS):
            step(nm, everything, t_ref[i:i + 1, :])
        step("hgrn_lb_logits", (slice(0, 1), slice(None)), t_ref[ROW_LOGITS:ROW_LOGITS + 1, 0:512])
        step("hgrn_lb_logits", (slice(1, 2), slice(None)), t_ref[ROW_LOGITS:ROW_LOGITS + 1, 512:1024])
        step("hgrn_out_norm", everything, t_ref[ROW_MISC:ROW_MISC + 1, 0:HGRN_DIM])
        step("attn_sinks", everything, t_ref[ROW_MISC:ROW_MISC + 1, LANE_SINKS:LANE_SINKS + ATTN_Q_HEADS])
        for h in range(2):
            for j, (c0, c1) in enumerate(FF_PIECES):
                r = ROW_CONV_B + 3 * h + j
                step("ffn_conv_b", (slice(None), slice(D_FF * h + c0, D_FF * h + c1)), t_ref[r:r + 1, 0:c1 - c0])
        step("ffn_conv_w", (slice(None), slice(None), slice(None)), gcw_ref[...])

    shapes = [jax.ShapeDtypeStruct(w[nm].shape, F32) for nm in SMALL]
    out = pl.pallas_call(
        body, name=name, out_shape=[jax.ShapeDtypeStruct((1, 1), F32)] + shapes * 4,
    )(total, g_conv_w, *[w[nm] for nm in SMALL], *[m[nm] for nm in SMALL], *[v[nm] for nm in SMALL])
    trees = [dict(zip(SMALL, out[1 + n * i:1 + n * (i + 1)])) for i in range(4)]
    return out[0], trees


BIG = ("w_in", "w_out", "ca_wq", "ca_wk", "ca_wv", "ca_wo", "ffn_w_up", "ffn_w_down")
BIG_FULL = {"w_in": (1024, 2816), "w_out": (1024, 1024), "ca_wq": (1024, 1024), "ca_wk": (1024, 1024),
            "ca_wv": (1024, 1024), "ca_wo": (1024, 1024), "ffn_w_up": (1024, 5632), "ffn_w_down": (2816, 1024)}
G_IN, G_MID, G_UP, G_DOWN = ("w_in",), ("w_out", "ca_wq", "ca_wk", "ca_wv", "ca_wo"), ("ffn_w_up",), ("ffn_w_down",)
GROUPS = (G_IN, G_MID, G_UP, G_DOWN)
COL_SHARDED = ("w_in", "ffn_w_up")
PACK_COLS = 1024


def _big_rows(name):
    r, c = BIG_FULL[name]
    return r * c // N_DEV // PACK_COLS


def _pack_shards(w, names):
    rows = [w[n][0].T if n in COL_SHARDED else w[n][0] for n in names]
    return (rows[0] if len(rows) == 1 else jnp.concatenate(rows, axis=0)).astype(BF16)


def _unpack_gathered(gathered, names):
    out, r0 = {}, 0
    for n in names:
        rows = _big_rows(n)
        out[n] = gathered[:, r0:r0 + rows].reshape(N_DEV * rows, PACK_COLS)
        r0 += rows
    return out


def _pack_full_grads(grads, names):
    parts = [grads[n].reshape(N_DEV, _big_rows(n), PACK_COLS) for n in names]
    return parts[0] if len(parts) == 1 else jnp.concatenate(parts, axis=1)


def kernel(x, mem, mix_pre_norm, w_in, attn_sinks, hgrn_lb_logits, hgrn_out_norm, w_out, mix_post_norm, ca_pre_norm, mem_norm, ca_wq, ca_wk, ca_wv, ca_wo, ca_post_norm, ffn_pre_norm, ffn_w_up, ffn_conv_w, ffn_conv_b, ffn_w_down, ffn_post_norm, loss_target, m_mix_pre_norm, m_w_in, m_attn_sinks, m_hgrn_lb_logits, m_hgrn_out_norm, m_w_out, m_mix_post_norm, m_ca_pre_norm, m_mem_norm, m_ca_wq, m_ca_wk, m_ca_wv, m_ca_wo, m_ca_post_norm, m_ffn_pre_norm, m_ffn_w_up, m_ffn_conv_w, m_ffn_conv_b, m_ffn_w_down, m_ffn_post_norm, v_mix_pre_norm, v_w_in, v_attn_sinks, v_hgrn_lb_logits, v_hgrn_out_norm, v_w_out, v_mix_post_norm, v_ca_pre_norm, v_mem_norm, v_ca_wq, v_ca_wk, v_ca_wv, v_ca_wo, v_ca_post_norm, v_ffn_pre_norm, v_ffn_w_up, v_ffn_conv_w, v_ffn_conv_b, v_ffn_w_down, v_ffn_post_norm):
    names = ["mix_pre_norm", "w_in", "attn_sinks", "hgrn_lb_logits", "hgrn_out_norm", "w_out", "mix_post_norm",
             "ca_pre_norm", "mem_norm", "ca_wq", "ca_wk", "ca_wv", "ca_wo", "ca_post_norm", "ffn_pre_norm",
             "ffn_w_up", "ffn_conv_w", "ffn_conv_b", "ffn_w_down", "ffn_post_norm"]
    w_all = dict(zip(names, [mix_pre_norm, w_in, attn_sinks, hgrn_lb_logits, hgrn_out_norm, w_out, mix_post_norm,
                             ca_pre_norm, mem_norm, ca_wq, ca_wk, ca_wv, ca_wo, ca_post_norm, ffn_pre_norm,
                             ffn_w_up, ffn_conv_w, ffn_conv_b, ffn_w_down, ffn_post_norm]))
    m_all = dict(zip(names, [m_mix_pre_norm, m_w_in, m_attn_sinks, m_hgrn_lb_logits, m_hgrn_out_norm, m_w_out,
                             m_mix_post_norm, m_ca_pre_norm, m_mem_norm, m_ca_wq, m_ca_wk, m_ca_wv, m_ca_wo,
                             m_ca_post_norm, m_ffn_pre_norm, m_ffn_w_up, m_ffn_conv_w, m_ffn_conv_b, m_ffn_w_down,
                             m_ffn_post_norm]))
    v_all = dict(zip(names, [v_mix_pre_norm, v_w_in, v_attn_sinks, v_hgrn_lb_logits, v_hgrn_out_norm, v_w_out,
                             v_mix_post_norm, v_ca_pre_norm, v_mem_norm, v_ca_wq, v_ca_wk, v_ca_wv, v_ca_wo,
                             v_ca_post_norm, v_ffn_pre_norm, v_ffn_w_up, v_ffn_conv_w, v_ffn_conv_b, v_ffn_w_down,
                             v_ffn_post_norm]))
    dev = _index(_mesh_pos())

    w_packs = {grp: _pack_shards(w_all, grp) for grp in GROUPS}
    shard_w = D_FF * 2 // N_DEV
    conv_w_rows = _exchange_alone(_Exchange("gather", ffn_conv_w[0]), "gather_conv_w")
    conv_w_full = conv_w_rows.transpose(1, 0, 2).reshape(3, 2 * D_FF)

    received, small_pack, grad_x = _local_step(
        x[0], mem[0], loss_target[0], w_packs, conv_w_full,
        {n: w_all[n] for n in NORMS}, attn_sinks, hgrn_lb_logits, hgrn_out_norm, ffn_conv_b)

    total = _sum_parts(_exchange_alone(_Exchange("gather", small_pack), "gather_small"), "sum_small")
    cw = total[ROW_CONV_W:ROW_CONV_W + 18].reshape(2, 3, 3 * PACK_COLS)[:, :, :D_FF]
    cw = cw.transpose(1, 0, 2).reshape(3, 2 * D_FF)
    g_conv_w = lax.dynamic_slice_in_dim(cw, dev * shard_w, shard_w, axis=1)[None]
    loss, (out_g, out_d, out_m, out_v) = _adamw_small(total, g_conv_w, w_all, m_all, v_all, "adamw_small")

    for grp in GROUPS:
        r0 = 0
        for n in grp:
            rows = _big_rows(n)
            if n in COL_SHARDED:
                g = _sum_rows(received[grp], r0, rows, "sum_" + n).T[None]
                d, mo, vo = _adamw_call(w_all[n], g, m_all[n], v_all[n], "adamw_" + n)
            else:
                g, d, mo, vo = _sum_rows(received[grp], r0, rows, "adamw_" + n, wmv=(w_all[n], m_all[n], v_all[n]))
            out_g[n], out_d[n], out_m[n], out_v[n] = g, d, mo, vo
            r0 += rows

    return (loss[0, 0], grad_x[None], *[out_g[n] for n in names], *[out_d[n] for n in names],
            *[out_m[n] for n in names], *[out_v[n] for n in names])


def _local_step(x, mem, target, w_packs, conv_w, norms, sinks, lb_logits, out_norm, conv_b):
    g1, g2, g3 = norms["mix_pre_norm"], norms["mix_post_norm"], norms["ca_pre_norm"]
    g4, g5, g6, g7 = norms["mem_norm"], norms["ca_post_norm"], norms["ffn_pre_norm"], norms["ffn_post_norm"]

    h1, gathered = _norm_fwd(x, g1, "mix_norm", exchange=_Exchange("gather", w_packs[G_IN], relay=True))
    w_in_t = _unpack_gathered(gathered, G_IN)["w_in"]
    up_shard = w_packs[G_UP]
    up_rows = up_shard.shape[0]
    up_cuts = (0, up_rows // 2, 3 * up_rows // 4, up_rows)
    up_parts = [up_shard[a:b] for a, b in zip(up_cuts[:-1], up_cuts[1:])]
    z, up_0 = _mm(h1, w_in_t, mode="nt", out_dtype=BF16, name="in_proj", tn=1408,
                  exchange=_Exchange("gather", up_parts[0]))
    attn, lse, gathered = _swa_fwd(z, sinks, "swa_fwd", exchange=_Exchange("gather", w_packs[G_DOWN]))
    w_down = _unpack_gathered(gathered, G_DOWN)["ffn_w_down"]
    lb = _lower_bound(lb_logits, "lower_bound")
    rec, o_rec, states, gathered = _hgrn_fwd(z, lb, out_norm, "hgrn_fwd", exchange=_Exchange("gather", w_packs[G_MID]))
    w_out, wq, wk, wv, wo = (_unpack_gathered(gathered, G_MID)[n] for n in G_MID)
    cat = jnp.concatenate([attn, rec], axis=1)
    mix = _mm(cat, w_out, mode="nn", out_dtype=BF16, name="out_proj")
    x1, h2, up_1 = _post_pre(x, mix, g2, g3, "mix_post", exchange=_Exchange("gather", up_parts[1]))
    mem_n = _norm_fwd(mem, g4, "mem_norm")
    q = _mm(h2, wq, mode="nn", out_dtype=BF16, name="ca_q")
    k = _mm(mem_n, wk, mode="nn", out_dtype=BF16, name="ca_k")
    v = _mm(mem_n, wv, mode="nn", out_dtype=BF16, name="ca_v")
    oc = _ca_fwd(q, k, v, "ca_fwd")
    c = _mm(oc, wo, mode="nn", out_dtype=BF16, name="ca_o")
    x2, h3, up_2 = _post_pre(x1, c, g5, g6, "ca_post", exchange=_Exchange("gather", up_parts[2]))
    w_up_t = jnp.concatenate([up_0, up_1, up_2], axis=1).reshape(-1, PACK_COLS)
    u = _mm(h3, w_up_t, mode="nt", out_dtype=F32, name="ffn_up", tn=1408, split_out=True)
    a = _glu_fwd(u, conv_w, conv_b, "glu_fwd")
    y = _mm(a, w_down, mode="nn", out_dtype=BF16, name="ffn_down", tk=2816)
    loss, dx3, dy, dg7 = _final(x2, y, g7, target, "loss_head")

    da = _mm(dy, w_down, mode="nt", out_dtype=F32, name="ffn_down_dx", tn=1408)
    d_w_down = _mm(a, dy, mode="tn", out_dtype=BF16, name="ffn_down_dw", tm=1408, tk=1024)
    dc, d_cb, d_cw, got_down = _glu_bwd(
        u, conv_w, conv_b, da, "glu_bwd",
        exchange=_Exchange("scatter", _pack_full_grads({"ffn_w_down": d_w_down}, G_DOWN)))
    du = _conv_bwd(dc, conv_w, "conv_bwd")
    d_w_up_t = _mm(du, h3, mode="tn", out_dtype=BF16, name="ffn_up_dw", tm=1408, tk=1024, split_a=True)
    dh3, got_up = _mm(du, w_up_t, mode="nn", out_dtype=BF16, name="ffn_up_dx", tm=2048, tk=1408, split_a=True,
                      exchange=_Exchange("scatter", _pack_full_grads({"ffn_w_up": d_w_up_t}, G_UP)))
    dx2, dcv, dg6, dg5 = _norm_bwd2(dx3, dh3, x2, g6, c, g5, "ca_post_bwd")
    doc = _mm(dcv, wo, mode="nt", out_dtype=BF16, name="ca_o_dx")
    d_wo = _mm(oc, dcv, mode="tn", out_dtype=BF16, name="ca_o_dw", tm=1024, tk=1024)
    dq, dk, dv = _ca_bwd(q, k, v, doc, "ca_bwd")
    d_wq = _mm(h2, dq, mode="tn", out_dtype=BF16, name="ca_q_dw", tm=1024, tk=1024)
    dh2 = _mm(dq, wq, mode="nt", out_dtype=BF16, name="ca_q_dx")
    d_wk = _mm(mem_n, dk, mode="tn", out_dtype=BF16, name="ca_k_dw", tm=1024)
    d_wv = _mm(mem_n, dv, mode="tn", out_dtype=BF16, name="ca_v_dw", tm=1024)
    dmem_k = _mm(dk, wk, mode="nt", out_dtype=F32, name="ca_k_dx")
    dmem_v = _mm(dv, wv, mode="nt", out_dtype=F32, name="ca_v_dx")
    dg4 = _gain_bwd(mem, dmem_k, dmem_v, "mem_norm_bwd")
    dx1, dmix, dg3, dg2 = _norm_bwd2(dx2, dh2, x1, g3, mix, g2, "mix_post_bwd")
    dcat = _mm(dmix, w_out, mode="nt", out_dtype=BF16, name="out_proj_dx")
    d_w_out = _mm(cat, dmix, mode="tn", out_dtype=BF16, name="out_proj_dw", tm=1024, tk=1024)
    mid = {"w_out": d_w_out, "ca_wq": d_wq, "ca_wk": d_wk, "ca_wv": d_wv, "ca_wo": d_wo}
    dqr, dfr, dir_, dgr, dlb, donw, got_mid = _hgrn_bwd(
        z, lb, out_norm, o_rec, states, dcat, "hgrn_bwd", exchange=_Exchange("scatter", _pack_full_grads(mid, G_MID)))
    dq_a, dka, dkb, dva, dvb, dsinks = _swa_bwd(z, sinks, dcat, lse, "swa_bwd")
    dz = _assemble_dz(dq_a, dka, dkb, dva, dvb, dqr, dfr, dir_, dgr, "assemble_dz")
    d_w_in_t = _mm(dz, h1, mode="tn", out_dtype=BF16, name="in_proj_dw", tm=1408, tk=1024)
    dh1, got_in = _mm(dz, w_in_t, mode="nn", out_dtype=BF16, name="in_proj_dx", tk=2816,
                      exchange=_Exchange("scatter", _pack_full_grads({"w_in": d_w_in_t}, G_IN)))
    dx, dg1 = _norm_bwd1(dx1, dh1, x, g1, "mix_norm_bwd")

    small_pack = _pack_small(
        (dg1, dg2, dg3, dg4, dg5, dg6, dg7), _lower_bound_bwd(lb, dlb, "lower_bound_bwd"), donw, dsinks, loss,
        d_cb, d_cw, "pack_small")
    return {G_IN: got_in, G_MID: got_mid, G_UP: got_up, G_DOWN: got_down}, small_pack, dx
```

```python
import jax
import jax.numpy as jnp
from jax import lax
from jax.experimental import pallas as pl
from jax.experimental.pallas import tpu as pltpu

F32 = jnp.float32
BF16 = jnp.bfloat16
EPS = 1e-6
N_DEV = 8
MESH_AXES = ("x", "y", "c")

ATTN_HEAD_DIM = 64
ATTN_Q_HEADS = 8
ATTN_KV_HEADS = 2
ATTN_BLOCK = 128
HGRN_HEADS = 4
HGRN_DIM = 128
HGRN_CHUNK = 64
HGRN_PAIR = 4
Z_Q, Z_F, Z_I, Z_G = 768, 1280, 1792, 2304
HGRN_LEVELS = (32, 16, 8, 4, 2, 1)
CA_HEADS = 4
CA_HEAD_DIM = 256
D_FF = 2816

ADAM_LR = 0.001
ADAM_B1 = 0.9
ADAM_B2 = 0.999
ADAM_EPS = 1e-08
ADAM_WD = 0.01
ADAM_STEP = 10

VMEM_LIMIT = 56 << 20
LANE = 128

NT = (((1,), (1,)), ((), ()))
TN = (((0,), (0,)), ((), ()))


def _params(*sem):
    return pltpu.CompilerParams(dimension_semantics=sem, vmem_limit_bytes=VMEM_LIMIT)


def _tile(n, cap):
    if n <= cap:
        return n
    best = 0
    for t in range(LANE, cap + 1, LANE):
        if n % t == 0:
            best = t
    assert best, (n, cap)
    return best


def _dot(a, b, dims=None):
    if dims is None:
        return jnp.dot(a, b, preferred_element_type=F32)
    return lax.dot_general(a, b, dims, preferred_element_type=F32)


def _bf(x):
    return x.astype(BF16)


def _sigmoid(x):
    return 1.0 / (1.0 + jnp.exp(-x))


def _rms(x):
    r = lax.rsqrt(jnp.mean(x * x, axis=-1, keepdims=True) + EPS)
    return x * r, r


def _rms_bwd(dxh, xh, r):
    return r * (dxh - xh * jnp.mean(dxh * xh, axis=-1, keepdims=True))


def _mm(a, b, *, mode, out_dtype, name, tm=1024, tn=1024, tk=1024, split_a=False, split_b=False, split_out=False,
        exchange=None):
    def dims(arr, split):
        if split:
            return arr.shape[1], 2 * arr.shape[2]
        return arr.shape

    ar, ac = dims(a, split_a)
    br, bc = dims(b, split_b)
    if mode == "nn":
        M, K, N = ar, ac, bc
        assert br == K
    elif mode == "nt":
        M, K, N = ar, ac, br
        assert bc == K
    else:
        K, M, N = ar, ac, bc
        assert br == K
    a_cols_half = ac // 2 if split_a else None
    b_cols_half = bc // 2 if split_b else None
    tm = _tile(M, tm)
    tn = _tile((N // 2) if (split_out or (split_b and mode != "nt")) else N, tn)
    tk = _tile((K // 2) if ((split_a and mode != "tn") or (split_b and mode == "nt")) else K, tk)
    if split_a and mode == "tn":
        tm = _tile(M // 2, tm)
    gm, gn, gk = M // tm, N // tn, K // tk
    a_bytes, b_bytes = a.size * a.dtype.itemsize, b.size * b.dtype.itemsize
    rows_outer = gk > 1 or a_bytes + gm * b_bytes <= gn * a_bytes + b_bytes
    grid = (gm, gn, gk) if rows_outer else (gn, gm, gk)

    def spec(split, half, blk, rc):
        def imap(p, q, k):
            r, c = rc(*((p, q) if rows_outer else (q, p)), k)
            if not split:
                return (r, c)
            per_half = half // blk[1]
            return (c // per_half, r, c % per_half)

        return pl.BlockSpec(((None,) + blk) if split else blk, imap)

    if mode == "nn":
        a_spec = spec(split_a, a_cols_half, (tm, tk), lambda i, j, k: (i, k))
        b_spec = spec(split_b, b_cols_half, (tk, tn), lambda i, j, k: (k, j))
        dn = None
    elif mode == "nt":
        a_spec = spec(split_a, a_cols_half, (tm, tk), lambda i, j, k: (i, k))
        b_spec = spec(split_b, b_cols_half, (tn, tk), lambda i, j, k: (j, k))
        dn = NT
    else:
        a_spec = spec(split_a, a_cols_half, (tk, tm), lambda i, j, k: (k, i))
        b_spec = spec(split_b, b_cols_half, (tk, tn), lambda i, j, k: (k, j))
        dn = TN
    o_spec = spec(split_out, N // 2 if split_out else None, (tm, tn), lambda i, j, k: (i, j))
    out_shape = (2, M, N // 2) if split_out else (M, N)

    if gk == 1:
        def body(a_ref, b_ref, o_ref):
            o_ref[...] = _dot(_bf(a_ref[...]), _bf(b_ref[...]), dn).astype(o_ref.dtype)
        scratch = []
    else:
        def body(a_ref, b_ref, o_ref, acc_ref):
            k = pl.program_id(2)

            @pl.when(k == 0)
            def _():
                acc_ref[...] = jnp.zeros_like(acc_ref)

            acc_ref[...] += _dot(_bf(a_ref[...]), _bf(b_ref[...]), dn)

            @pl.when(k == gk - 1)
            def _():
                o_ref[...] = acc_ref[...].astype(o_ref.dtype)
        scratch = [pltpu.VMEM((tm, tn), F32)]

    out = _hosted_call(
        body, name=name, grid=grid, in_specs=[a_spec, b_spec], out_specs=[o_spec],
        out_shape=[jax.ShapeDtypeStruct(out_shape, out_dtype)], scratch=scratch, args=(a, b),
        semantics=("parallel", "parallel", "arbitrary"), exchange=exchange)
    return out[0] if exchange is None else out


ROWS = 512


def _row_spec(tr, cols):
    return pl.BlockSpec((tr, cols), lambda i: (i, 0))


def _vec_spec(cols):
    return pl.BlockSpec((1, cols), lambda i: (0, 0))


def _norm_fwd(x, g, name, exchange=None):
    T, Dm = x.shape
    tr = min(ROWS, T)

    def body(x_ref, g_ref, h_ref):
        xh, _ = _rms(x_ref[...])
        h_ref[...] = (xh * g_ref[...]).astype(h_ref.dtype)

    out = _hosted_call(
        body, name=name, grid=(T // tr,), in_specs=[_row_spec(tr, Dm), _vec_spec(Dm)], out_specs=[_row_spec(tr, Dm)],
        out_shape=[jax.ShapeDtypeStruct((T, Dm), BF16)], scratch=[], args=(x, g), semantics=("parallel",),
        exchange=exchange)
    return out[0] if exchange is None else out


def _post_pre(x, m, g_post, g_pre, name, exchange=None):
    T, Dm = x.shape
    tr = min(ROWS, T)

    def body(x_ref, m_ref, gp_ref, gn_ref, xo_ref, h_ref):
        mh, _ = _rms(m_ref[...].astype(F32))
        xn = x_ref[...] + mh * gp_ref[...]
        xo_ref[...] = xn
        xh, _ = _rms(xn)
        h_ref[...] = (xh * gn_ref[...]).astype(h_ref.dtype)

    return _hosted_call(
        body, name=name, grid=(T // tr,),
        in_specs=[_row_spec(tr, Dm), _row_spec(tr, Dm), _vec_spec(Dm), _vec_spec(Dm)],
        out_specs=[_row_spec(tr, Dm), _row_spec(tr, Dm)],
        out_shape=[jax.ShapeDtypeStruct((T, Dm), F32), jax.ShapeDtypeStruct((T, Dm), BF16)],
        scratch=[], args=(x, m, g_post, g_pre), semantics=("parallel",), exchange=exchange)


def _final(x2, y, g_post, target, name):
    T, Dm = x2.shape
    tr = min(ROWS, T)

    def body(x_ref, y_ref, g_ref, t_ref, loss_ref, dx_ref, dy_ref, dg_ref):
        @pl.when(pl.program_id(0) == 0)
        def _():
            loss_ref[...] = jnp.zeros_like(loss_ref)
            dg_ref[...] = jnp.zeros_like(dg_ref)

        g = g_ref[...]
        yh, r = _rms(y_ref[...].astype(F32))
        d = x_ref[...] + yh * g - t_ref[...]
        loss_ref[...] += jnp.zeros((1, LANE), F32) + 0.5 * jnp.sum(jnp.mean(d * d, axis=-1, keepdims=True))
        dx = d * (1.0 / Dm)
        dx_ref[...] = dx
        dy_ref[...] = _rms_bwd(dx * g, yh, r).astype(dy_ref.dtype)
        dg_ref[...] += jnp.sum(dx * yh, axis=0, keepdims=True)

    return pl.pallas_call(
        body, name=name, grid=(T // tr,),
        in_specs=[_row_spec(tr, Dm), _row_spec(tr, Dm), _vec_spec(Dm), _row_spec(tr, Dm)],
        out_specs=[_vec_spec(LANE), _row_spec(tr, Dm), _row_spec(tr, Dm), _vec_spec(Dm)],
        out_shape=[jax.ShapeDtypeStruct((1, LANE), F32), jax.ShapeDtypeStruct((T, Dm), F32),
                   jax.ShapeDtypeStruct((T, Dm), BF16), jax.ShapeDtypeStruct((1, Dm), F32)],
        compiler_params=_params("arbitrary"),
    )(x2, y, g_post, target)


def _norm_bwd2(dx_cur, dh, x_prev, g_pre, m_prev, g_post, name):
    T, Dm = x_prev.shape
    tr = min(ROWS, T)

    def body(dx_ref, dh_ref, x_ref, gn_ref, m_ref, gp_ref, dxo_ref, dm_ref, dgn_ref, dgp_ref):
        @pl.when(pl.program_id(0) == 0)
        def _():
            dgn_ref[...] = jnp.zeros_like(dgn_ref)
            dgp_ref[...] = jnp.zeros_like(dgp_ref)

        dh = dh_ref[...].astype(F32)
        xh, r = _rms(x_ref[...])
        dx = dx_ref[...] + _rms_bwd(dh * gn_ref[...], xh, r)
        dxo_ref[...] = dx
        dgn_ref[...] += jnp.sum(dh * xh, axis=0, keepdims=True)
        mh, rm = _rms(m_ref[...].astype(F32))
        dm_ref[...] = _rms_bwd(dx * gp_ref[...], mh, rm).astype(dm_ref.dtype)
        dgp_ref[...] += jnp.sum(dx * mh, axis=0, keepdims=True)

    return pl.pallas_call(
        body, name=name, grid=(T // tr,),
        in_specs=[_row_spec(tr, Dm), _row_spec(tr, Dm), _row_spec(tr, Dm), _vec_spec(Dm), _row_spec(tr, Dm), _vec_spec(Dm)],
        out_specs=[_row_spec(tr, Dm), _row_spec(tr, Dm), _vec_spec(Dm), _vec_spec(Dm)],
        out_shape=[jax.ShapeDtypeStruct((T, Dm), F32), jax.ShapeDtypeStruct((T, Dm), BF16),
                   jax.ShapeDtypeStruct((1, Dm), F32), jax.ShapeDtypeStruct((1, Dm), F32)],
        compiler_params=_params("arbitrary"),
    )(dx_cur, dh, x_prev, g_pre, m_prev, g_post)


def _norm_bwd1(dx_cur, dh_a, dh_b, x_prev, g_pre, name):
    T, Dm = x_prev.shape
    tr = min(ROWS, T)

    def body(dx_ref, dha_ref, dhb_ref, x_ref, gn_ref, dxo_ref, dgn_ref):
        @pl.when(pl.program_id(0) == 0)
        def _():
            dgn_ref[...] = jnp.zeros_like(dgn_ref)

        dh = dha_ref[...].astype(F32) + dhb_ref[...].astype(F32)
        xh, r = _rms(x_ref[...])
        dxo_ref[...] = dx_ref[...] + _rms_bwd(dh * gn_ref[...], xh, r)
        dgn_ref[...] += jnp.sum(dh * xh, axis=0, keepdims=True)

    return pl.pallas_call(
        body, name=name, grid=(T // tr,),
        in_specs=[_row_spec(tr, Dm), _row_spec(tr, Dm), _row_spec(tr, Dm), _row_spec(tr, Dm), _vec_spec(Dm)],
        out_specs=[_row_spec(tr, Dm), _vec_spec(Dm)],
        out_shape=[jax.ShapeDtypeStruct((T, Dm), F32), jax.ShapeDtypeStruct((1, Dm), F32)],
        compiler_params=_params("arbitrary"),
    )(dx_cur, dh_a, dh_b, x_prev, g_pre)


def _gain_bwd(x, dh_a, dh_b, name):
    T, Dm = x.shape

    def body(x_ref, a_ref, b_ref, dg_ref):
        xh, _ = _rms(x_ref[...])
        dg_ref[...] = jnp.sum((a_ref[...] + b_ref[...]) * xh, axis=0, keepdims=True)

    return pl.pallas_call(
        body, name=name, grid=(1,), in_specs=[_row_spec(T, Dm)] * 3, out_specs=_vec_spec(Dm),
        out_shape=jax.ShapeDtypeStruct((1, Dm), F32), compiler_params=_params("arbitrary"),
    )(x, dh_a, dh_b)


ATTN_GROUP = ATTN_Q_HEADS // ATTN_KV_HEADS


def _swa_mask(n):
    rows = ATTN_GROUP * ATTN_BLOCK
    row = lax.broadcasted_iota(jnp.int32, (rows, 2 * ATTN_BLOCK), 0) & (ATTN_BLOCK - 1)
    col = lax.broadcasted_iota(jnp.int32, (rows, 2 * ATTN_BLOCK), 1)
    diff = row + ATTN_BLOCK - col
    return (diff >= 0) & (diff < ATTN_BLOCK) & ((col >= ATTN_BLOCK) | (n > 0))


def _swa_rows(ref, hk, dtype):
    hd = ATTN_HEAD_DIM
    return jnp.concatenate(
        [ref[:, hd * (hk * ATTN_GROUP + g):hd * (hk * ATTN_GROUP + g + 1)].astype(dtype) for g in range(ATTN_GROUP)],
        axis=0)


def _swa_per_row(vals):
    seg = lax.broadcasted_iota(jnp.int32, (ATTN_GROUP * ATTN_BLOCK, 1), 0) // ATTN_BLOCK
    col = jnp.zeros((ATTN_GROUP * ATTN_BLOCK, 1), F32)
    for g, val in enumerate(vals):
        col = jnp.where(seg == g, val, col)
    return col


def _swa_specs():
    blk = ATTN_BLOCK
    prev = lambda n: jnp.maximum(n - 1, 0)
    return [
        pl.BlockSpec(memory_space=pltpu.SMEM),
        pl.BlockSpec((blk, 512), lambda n: (n, 0)),
        pl.BlockSpec((blk, 128), lambda n: (prev(n), 4)),
        pl.BlockSpec((blk, 128), lambda n: (n, 4)),
        pl.BlockSpec((blk, 128), lambda n: (prev(n), 5)),
        pl.BlockSpec((blk, 128), lambda n: (n, 5)),
    ]


def _swa_fwd(z, sinks, name, exchange=None):
    T = z.shape[0]
    blk, hd = ATTN_BLOCK, ATTN_HEAD_DIM
    scale = hd ** -0.5

    def body(sink_ref, q_ref, kp_ref, kc_ref, vp_ref, vc_ref, o_ref, lse_ref):
        allowed = _swa_mask(pl.program_id(0))
        hks = range(ATTN_KV_HEADS)
        kss = [slice(hd * hk, hd * hk + hd) for hk in hks]
        k = [_bf(jnp.concatenate([kp_ref[:, ks], kc_ref[:, ks]], axis=0)) for ks in kss]
        v = [_bf(jnp.concatenate([vp_ref[:, ks], vc_ref[:, ks]], axis=0)) for ks in kss]
        s = [jnp.where(allowed, _dot(_swa_rows(q_ref, hk, BF16), k[hk], NT) * scale, -1e30) for hk in hks]
        sink = [_swa_per_row([sink_ref[0, hk * ATTN_GROUP + g] for g in range(ATTN_GROUP)]) for hk in hks]
        m = [jnp.maximum(jnp.max(s[hk], axis=-1, keepdims=True), sink[hk]) for hk in hks]
        p = [jnp.exp(s[hk] - m[hk]) for hk in hks]
        l = [jnp.sum(p[hk], axis=-1, keepdims=True) + jnp.exp(sink[hk] - m[hk]) for hk in hks]
        o = [_dot(_bf(p[hk] / l[hk]), v[hk]).astype(o_ref.dtype) for hk in hks]
        for hk in hks:
            lse = m[hk] + jnp.log(l[hk])
            for g in range(ATTN_GROUP):
                h = hk * ATTN_GROUP + g
                o_ref[:, hd * h:hd * (h + 1)] = o[hk][blk * g:blk * (g + 1)]
                lse_ref[:, h:h + 1] = lse[blk * g:blk * (g + 1)]

    return _hosted_call(
        body, name=name, grid=(T // blk,), in_specs=_swa_specs(),
        out_specs=[pl.BlockSpec((blk, 512), lambda n: (n, 0)), pl.BlockSpec((blk, ATTN_Q_HEADS), lambda n: (n, 0))],
        out_shape=[jax.ShapeDtypeStruct((T, 512), BF16), jax.ShapeDtypeStruct((T, ATTN_Q_HEADS), F32)],
        scratch=[], args=(sinks, z, z, z, z, z), semantics=("parallel",), exchange=exchange)


def _swa_bwd(z, sinks, dcat, lse, name):
    T = z.shape[0]
    blk, hd = ATTN_BLOCK, ATTN_HEAD_DIM
    scale = hd ** -0.5
    group = ATTN_Q_HEADS // ATTN_KV_HEADS

    def body(sink_ref, q_ref, kp_ref, kc_ref, vp_ref, vc_ref, do_ref, lse_ref,
             dq_ref, dka_ref, dkb_ref, dva_ref, dvb_ref, dsink_ref):
        @pl.when(pl.program_id(0) == 0)
        def _():
            dsink_ref[...] = jnp.zeros_like(dsink_ref)

        allowed = _swa_mask(pl.program_id(0))
        lane = lax.broadcasted_iota(jnp.int32, (1, ATTN_Q_HEADS), 1)
        dsink = jnp.zeros((1, ATTN_Q_HEADS), F32)
        hks = range(ATTN_KV_HEADS)
        kss = [slice(hd * hk, hd * hk + hd) for hk in hks]
        k = [_bf(jnp.concatenate([kp_ref[:, ks], kc_ref[:, ks]], axis=0)) for ks in kss]
        v = [_bf(jnp.concatenate([vp_ref[:, ks], vc_ref[:, ks]], axis=0)) for ks in kss]
        qs = [_swa_rows(q_ref, hk, BF16) for hk in hks]
        dos = [_swa_rows(do_ref, hk, BF16) for hk in hks]
        lse = [jnp.concatenate([lse_ref[:, hk * group + g:hk * group + g + 1] for g in range(group)], axis=0)
               for hk in hks]
        s = [_dot(qs[hk], k[hk], NT) * scale for hk in hks]
        dp = [_dot(dos[hk], v[hk], NT) for hk in hks]
        p = [jnp.where(allowed, jnp.exp(jnp.where(allowed, s[hk], -1e30) - lse[hk]), 0.0) for hk in hks]
        delta = [jnp.sum(p[hk] * dp[hk], axis=-1, keepdims=True) for hk in hks]
        ds = [_bf(p[hk] * (dp[hk] - delta[hk]) * scale) for hk in hks]
        dq = [_dot(ds[hk], k[hk]).astype(dq_ref.dtype) for hk in hks]
        dk = [_dot(ds[hk], qs[hk], TN) for hk in hks]
        dv = [_dot(_bf(p[hk]), dos[hk], TN) for hk in hks]
        for hk in hks:
            sink = _swa_per_row([sink_ref[0, hk * group + g] for g in range(group)])
            sink_part = jnp.exp(sink - lse[hk]) * delta[hk]
            for g in range(group):
                h = hk * group + g
                dq_ref[:, hd * h:hd * (h + 1)] = dq[hk][blk * g:blk * (g + 1)]
                dsink = dsink + jnp.where(lane == h, -jnp.sum(sink_part[blk * g:blk * (g + 1)]), 0.0)
            dkb_ref[:, kss[hk]] = dk[hk][:blk]
            dka_ref[:, kss[hk]] = dk[hk][blk:]
            dvb_ref[:, kss[hk]] = dv[hk][:blk]
            dva_ref[:, kss[hk]] = dv[hk][blk:]
        dsink_ref[...] += dsink

    kv_out = pl.BlockSpec((blk, 128), lambda n: (n, 0))
    return pl.pallas_call(
        body, name=name, grid=(T // blk,),
        in_specs=_swa_specs() + [pl.BlockSpec((blk, 512), lambda n: (n, 0)),
                                 pl.BlockSpec((blk, ATTN_Q_HEADS), lambda n: (n, 0))],
        out_specs=[pl.BlockSpec((blk, 512), lambda n: (n, 0)), kv_out, kv_out, kv_out, kv_out,
                   pl.BlockSpec((1, ATTN_Q_HEADS), lambda n: (0, 0))],
        out_shape=[jax.ShapeDtypeStruct((T, 512), BF16)] + [jax.ShapeDtypeStruct((T, 128), F32)] * 4
        + [jax.ShapeDtypeStruct((1, ATTN_Q_HEADS), F32)],
        compiler_params=_params("arbitrary"),
    )(sinks, z, z, z, z, z, dcat, lse)


def _assemble_attn(dq_a, dka, dkb, dva, dvb, name):
    T = dq_a.shape[0]
    blk = ATTN_BLOCK
    nb = T // blk

    def body(dq_ref, dka_ref, dkb_ref, dva_ref, dvb_ref, o_ref):
        has_next = pl.program_id(0) < nb - 1
        o_ref[:, 0:512] = dq_ref[...]
        o_ref[:, 512:640] = (dka_ref[...] + jnp.where(has_next, dkb_ref[...], 0.0)).astype(o_ref.dtype)
        o_ref[:, 640:768] = (dva_ref[...] + jnp.where(has_next, dvb_ref[...], 0.0)).astype(o_ref.dtype)

    cur = lambda w: pl.BlockSpec((blk, w), lambda n: (n, 0))
    nxt = pl.BlockSpec((blk, 128), lambda n: (jnp.minimum(n + 1, nb - 1), 0))
    return pl.pallas_call(
        body, name=name, grid=(nb,), in_specs=[cur(512), cur(128), nxt, cur(128), nxt],
        out_specs=pl.BlockSpec((blk, Z_Q), lambda n: (n, 0)),
        out_shape=jax.ShapeDtypeStruct((T, Z_Q), BF16), compiler_params=_params("parallel"),
    )(dq_a, dka, dkb, dva, dvb)


HGRN_ROWS = 512


def _hgrn_consts():
    c = HGRN_CHUNK
    r = lax.broadcasted_iota(jnp.int32, (c, c), 0)
    s = lax.broadcasted_iota(jnp.int32, (c, c), 1)
    rcol = lax.broadcasted_iota(jnp.int32, (c, 1), 0)
    same_block, upper = [], []
    for m in HGRN_LEVELS:
        same_block.append((r & ~(2 * m - 1)) == (s & ~(2 * m - 1)))
        upper.append((rcol & (2 * m - 1)) >= m)
    cum_mat = jnp.where(s <= r, 1.0, 0.0).astype(BF16)
    rev_mat = jnp.where(s >= r, 1.0, 0.0).astype(BF16)
    return cum_mat, rev_mat, r == s, same_block, upper, rcol & 3


def _hgrn_level_decay(g, b, m, pos4):
    c = HGRN_CHUNK
    if m == 1:
        return jnp.exp(jnp.where((pos4 & 1) == 1, g, 0.0))
    if m == 2:
        after, before = pltpu.roll(g, c - 1, 0), pltpu.roll(g, 1, 0)
        return jnp.exp(jnp.where(pos4 == 0, after, jnp.where(pos4 == 1, 0.0, jnp.where(pos4 == 2, g, g + before))))
    b3 = b.reshape(c // (2 * m), 2 * m, HGRN_DIM)
    bref = jnp.broadcast_to(b3[:, m - 1:m, :], b3.shape).reshape(c, HGRN_DIM)
    return jnp.exp(-jnp.abs(b - bref))


def _split3(x):
    hi = _bf(x)
    r1 = x - hi.astype(F32)
    mid = _bf(r1)
    lo = _bf(r1 - mid.astype(F32))
    return jnp.concatenate([hi, mid, lo], axis=1)


def _dot_hilo(a, b):
    r, c = a.shape[0], b.shape[1]
    a_hi, b_hi = _bf(a), _bf(b)
    a2 = jnp.concatenate([a_hi, _bf(a - a_hi.astype(F32))], axis=0)
    b2 = jnp.concatenate([b_hi, _bf(b - b_hi.astype(F32))], axis=1)
    y = _dot(a2, b2)
    return y[:r, :c] + y[:r, c:] + y[r:, :c]


def _fold3(y):
    w = y.shape[1] // 3
    return y[:, :w] + y[:, w:2 * w] + y[:, 2 * w:]


def _hgrn_gates(qr, fr, lb):
    sq = _sigmoid(qr)
    q = qr * sq * (HGRN_DIM ** -0.5)
    sf = _sigmoid(fr)
    f = lb + (1.0 - lb) * sf
    k = (1.0 - lb) * _sigmoid(-fr)
    return q, sq, sf, f, k, jnp.log(f)


def _hgrn_intra(q, k, g, b, consts):
    _, _, eye, same_block, upper, pos4 = consts
    heads = range(len(q))
    a = [jnp.where(eye, _dot(_bf(q[hh]), _bf(k[hh]), NT), 0.0) for hh in heads]
    saved = [[] for _ in heads]
    for i, m in enumerate(HGRN_LEVELS):
        up = upper[i]
        e = [_hgrn_level_decay(g[hh], b[hh], m, pos4) for hh in heads]
        qt = [jnp.where(up, q[hh] * e[hh], 0.0) for hh in heads]
        kt = [jnp.where(up, 0.0, k[hh] * e[hh]) for hh in heads]
        p = [_dot(_bf(qt[hh]), _bf(kt[hh]), NT) for hh in heads]
        for hh in heads:
            a[hh] = a[hh] + jnp.where(same_block[i], p[hh], 0.0)
            saved[hh].append((e[hh], qt[hh], kt[hh]))
    return a, saved


def _hgrn_specs(tb, nb, rev):
    tmap = (lambda t: nb - 1 - t) if rev else (lambda t: t)
    assert HGRN_PAIR == HGRN_HEADS
    return [pl.BlockSpec((tb, 2816), lambda h, t: (tmap(t), 0)),
            pl.BlockSpec((1, HGRN_PAIR * HGRN_DIM), lambda h, t: (0, h)),
            pl.BlockSpec((1, HGRN_DIM), lambda h, t: (0, 0))]


def _hgrn_z(z_ref, sl, base, head):
    return z_ref[sl, base + HGRN_DIM * head:base + HGRN_DIM * (head + 1)].astype(F32)


def _hgrn_fwd(z, lb, onw, name, exchange=None):
    T = z.shape[0]
    tb = min(HGRN_ROWS, T)
    nb, c, nc = T // tb, HGRN_CHUNK, min(HGRN_ROWS, T) // HGRN_CHUNK

    def body(z_ref, lb_ref, onw_ref, rec_ref, o_ref, st_ref, state):
        @pl.when(pl.program_id(1) == 0)
        def _():
            state[...] = jnp.zeros_like(state)

        consts = _hgrn_consts()
        lbv = lb_ref[...]
        onwv = onw_ref[...]

        def chunk(ci, carry):
            sl = pl.ds(pl.multiple_of(ci * c, c), c)
            heads = range(HGRN_PAIR)
            lss = [slice(HGRN_DIM * hh, HGRN_DIM * (hh + 1)) for hh in heads]
            gates = [_hgrn_gates(_hgrn_z(z_ref, sl, Z_Q, hh), _hgrn_z(z_ref, sl, Z_F, hh), lbv[:, lss[hh]])
                     for hh in heads]
            q, k, g = [t[0] for t in gates], [t[4] for t in gates], [t[5] for t in gates]
            v = [_bf(_hgrn_z(z_ref, sl, Z_I, hh)) for hh in heads]
            b = [_fold3(_dot(consts[0], _split3(g[hh]))) for hh in heads]
            a, _ = _hgrn_intra(q, k, g, b, consts)
            st = [state[hh] for hh in heads]
            for hh in heads:
                st_ref[hh, ci] = st[hh]
            bl = [b[hh][c - 1:c, :] for hh in heads]
            o_state = [_dot(_bf(q[hh] * jnp.exp(b[hh])), _bf(st[hh]), NT) for hh in heads]
            kv = [_dot(v[hh], _bf(k[hh] * jnp.exp(bl[hh] - b[hh])), TN) for hh in heads]
            o = [_dot(_bf(a[hh]), v[hh]) + o_state[hh] for hh in heads]
            for hh in heads:
                state[hh] = st[hh] * jnp.exp(bl[hh]) + kv[hh]
                o_ref[sl, lss[hh]] = o[hh]
                oh, _ = _rms(o[hh])
                gr = _hgrn_z(z_ref, sl, Z_G, hh)
                rec_ref[sl, lss[hh]] = (oh * onwv * (gr * _sigmoid(gr))).astype(rec_ref.dtype)
            return carry

        lax.fori_loop(0, nc, chunk, 0)

    in_specs = _hgrn_specs(tb, nb, False)
    out_blk = pl.BlockSpec((tb, HGRN_PAIR * HGRN_DIM), lambda h, t: (t, h))
    return _hosted_call(
        body, name=name, grid=(HGRN_HEADS // HGRN_PAIR, nb), in_specs=in_specs,
        out_specs=[out_blk, out_blk, pl.BlockSpec((HGRN_PAIR, nc, HGRN_DIM, HGRN_DIM), lambda h, t: (h, t, 0, 0))],
        out_shape=[jax.ShapeDtypeStruct((T, 512), BF16), jax.ShapeDtypeStruct((T, 512), F32),
                   jax.ShapeDtypeStruct((HGRN_HEADS, T // c, HGRN_DIM, HGRN_DIM), F32)],
        scratch=[pltpu.VMEM((HGRN_PAIR, HGRN_DIM, HGRN_DIM), F32)], args=(z, lb, onw),
        semantics=("parallel", "arbitrary"), exchange=exchange)


def _hgrn_bwd(z, lb, onw, o, states, dcat, name, exchange=None):
    T = z.shape[0]
    tb = min(HGRN_ROWS, T)
    nb, c, nc = T // tb, HGRN_CHUNK, min(HGRN_ROWS, T) // HGRN_CHUNK

    def body(z_ref, lb_ref, onw_ref, o_ref, st_ref, drec_ref,
             dz_ref, dlb_ref, donw_ref, dstate):
        @pl.when(pl.program_id(1) == 0)
        def _():
            dstate[...] = jnp.zeros_like(dstate)
            dlb_ref[...] = jnp.zeros_like(dlb_ref)

        @pl.when((pl.program_id(0) == 0) & (pl.program_id(1) == 0))
        def _():
            donw_ref[...] = jnp.zeros_like(donw_ref)

        consts = _hgrn_consts()
        rev_mat, eye, same_block, upper = consts[1:5]
        lbv = lb_ref[...]
        onwv = onw_ref[...]
        last = lax.broadcasted_iota(jnp.int32, (c, 1), 0) == c - 1

        def chunk(i, carry):
            ci = nc - 1 - i
            sl = pl.ds(pl.multiple_of(ci * c, c), c)
            hs = range(HGRN_PAIR)
            lss = [slice(HGRN_DIM * hh, HGRN_DIM * (hh + 1)) for hh in hs]

            def dz_out(base, hh, val):
                col = base - Z_Q + HGRN_DIM * hh
                dz_ref[sl, col:col + HGRN_DIM] = val.astype(dz_ref.dtype)

            qr = [_hgrn_z(z_ref, sl, Z_Q, hh) for hh in hs]
            gates = [_hgrn_gates(qr[hh], _hgrn_z(z_ref, sl, Z_F, hh), lbv[:, lss[hh]]) for hh in hs]
            q, sq, sf, f, k, g = ([t[j] for t in gates] for j in range(6))
            v = [_bf(_hgrn_z(z_ref, sl, Z_I, hh)) for hh in hs]
            b = [_fold3(_dot(consts[0], _split3(g[hh]))) for hh in hs]
            a, saved = _hgrn_intra(q, k, g, b, consts)
            st = [st_ref[hh, ci] for hh in hs]
            dst = [dstate[hh] for hh in hs]

            gr = [_hgrn_z(z_ref, sl, Z_G, hh) for hh in hs]
            sg = [_sigmoid(gr[hh]) for hh in hs]
            norm = [_rms(o_ref[sl, ls]) for ls in lss]
            oh, r = [t[0] for t in norm], [t[1] for t in norm]
            drec = [drec_ref[sl, ls].astype(F32) for ls in lss]
            don = [drec[hh] * (gr[hh] * sg[hh]) for hh in hs]
            do = [_bf(_rms_bwd(don[hh] * onwv, oh[hh], r[hh])) for hh in hs]
            donw = jnp.sum(don[0] * oh[0], axis=0, keepdims=True)
            for hh in hs:
                dz_out(Z_G, hh, drec[hh] * oh[hh] * onwv * (sg[hh] * (1.0 + gr[hh] * (1.0 - sg[hh]))))
                if hh:
                    donw = donw + jnp.sum(don[hh] * oh[hh], axis=0, keepdims=True)
            donw_ref[...] += donw

            eb = [jnp.exp(b[hh]) for hh in hs]
            bl = [b[hh][c - 1:c, :] for hh in hs]
            ebl = [jnp.exp(bl[hh]) for hh in hs]
            ekb = [jnp.exp(bl[hh] - b[hh]) for hh in hs]
            qe = [q[hh] * eb[hh] for hh in hs]
            ke = [k[hh] * ekb[hh] for hh in hs]
            da = [_dot(do[hh], v[hh], NT) for hh in hs]
            dat = [_dot(v[hh], do[hh], NT) for hh in hs]
            dqe = [_dot(do[hh], _bf(st[hh])) for hh in hs]
            dke = [_dot(v[hh], _bf(dst[hh])) for hh in hs]
            dv_a = [_dot(_bf(a[hh]), do[hh], TN) for hh in hs]
            dv_s = [_dot(_bf(ke[hh]), _bf(dst[hh]), NT) for hh in hs]
            dst_in = [_dot(do[hh], _bf(qe[hh]), TN) for hh in hs]
            dad = [jnp.sum(jnp.where(eye, da[hh], 0.0), axis=1, keepdims=True) for hh in hs]
            dq = [dqe[hh] * eb[hh] + dad[hh] * k[hh] for hh in hs]
            dk = [dke[hh] * ekb[hh] + dad[hh] * q[hh] for hh in hs]
            db_last = [jnp.sum(dke[hh] * ke[hh], axis=0, keepdims=True)
                       + jnp.sum(dst[hh] * st[hh], axis=0, keepdims=True) * ebl[hh] for hh in hs]
            for hh in hs:
                dstate[hh] = dst[hh] * ebl[hh] + dst_in[hh]
                dz_out(Z_I, hh, dv_a[hh] + dv_s[hh])
            for lvl in range(len(HGRN_LEVELS)):
                xq = [_dot_hilo(jnp.where(same_block[lvl], da[hh], 0.0), saved[hh][lvl][2]) for hh in hs]
                xk = [_dot_hilo(jnp.where(same_block[lvl], dat[hh], 0.0), saved[hh][lvl][1]) for hh in hs]
                for hh in hs:
                    e = saved[hh][lvl][0]
                    dq[hh] = dq[hh] + jnp.where(upper[lvl], xq[hh] * e, 0.0)
                    dk[hh] = dk[hh] + jnp.where(upper[lvl], 0.0, xk[hh] * e)
            db = [q[hh] * dq[hh] - k[hh] * dk[hh] + jnp.where(last, db_last[hh], 0.0) for hh in hs]
            dg = [_fold3(_dot(rev_mat, _split3(db[hh]))) for hh in hs]

            for hh in hs:
                ls = lss[hh]
                dz_out(Z_Q, hh, dq[hh] * (HGRN_DIM ** -0.5) * (sq[hh] * (1.0 + qr[hh] * (1.0 - sq[hh]))))
                dfk = dg[hh] / f[hh] - dk[hh]
                dz_out(Z_F, hh, (1.0 - lbv[:, ls]) * sf[hh] * (1.0 - sf[hh]) * dfk)
                dlb_ref[:, ls] += jnp.sum((1.0 - sf[hh]) * dfk, axis=0, keepdims=True)
            return carry

        lax.fori_loop(0, nc, chunk, 0)

    in_specs = _hgrn_specs(tb, nb, True)
    rblk = pl.BlockSpec((tb, HGRN_PAIR * HGRN_DIM), lambda h, t: (nb - 1 - t, h))
    in_specs = in_specs + [
        rblk,
        pl.BlockSpec((HGRN_PAIR, nc, HGRN_DIM, HGRN_DIM), lambda h, t: (h, nb - 1 - t, 0, 0)),
        pl.BlockSpec((tb, HGRN_PAIR * HGRN_DIM), lambda h, t: (nb - 1 - t, 4 // HGRN_PAIR + h)),
    ]
    return _hosted_call(
        body, name=name, grid=(HGRN_HEADS // HGRN_PAIR, nb), in_specs=in_specs,
        out_specs=[pl.BlockSpec((tb, 2816 - Z_Q), lambda h, t: (nb - 1 - t, 0)),
                   pl.BlockSpec((1, HGRN_PAIR * HGRN_DIM), lambda h, t: (0, h)),
                   pl.BlockSpec((1, HGRN_DIM), lambda h, t: (0, 0))],
        out_shape=[jax.ShapeDtypeStruct((T, 2816 - Z_Q), BF16), jax.ShapeDtypeStruct((1, 512), F32),
                   jax.ShapeDtypeStruct((1, HGRN_DIM), F32)],
        scratch=[pltpu.VMEM((HGRN_PAIR, HGRN_DIM, HGRN_DIM), F32)], args=(z, lb, onw, o, states, dcat),
        semantics=("arbitrary", "arbitrary"), exchange=exchange)


def _lower_bound(logits, name):
    def body(l_ref, lb_ref):
        l0, l1 = l_ref[0:1, :], l_ref[1:2, :]
        m = jnp.maximum(l0, l1)
        e0, e1 = jnp.exp(l0 - m), jnp.exp(l1 - m)
        lb_ref[...] = e0 / (e0 + e1)

    return pl.pallas_call(
        body, name=name, out_shape=jax.ShapeDtypeStruct((1, logits.shape[1]), F32),
    )(logits)


def _lower_bound_bwd(lb, dlb, name):
    def body(lb_ref, dlb_ref, dl_ref):
        p = lb_ref[...]
        d0 = dlb_ref[...] * p * (1.0 - p)
        dl_ref[0:1, :] = d0
        dl_ref[1:2, :] = -d0

    return pl.pallas_call(
        body, name=name, out_shape=jax.ShapeDtypeStruct((2, lb.shape[1]), F32),
    )(lb, dlb)


CA_ROWS = 512


def _ca_fwd(q, k, v, name):
    T, W = q.shape
    M = k.shape[0]
    tq = min(CA_ROWS, T)
    scale = CA_HEAD_DIM ** -0.5

    def body(q_ref, k_ref, v_ref, o_ref):
        for h in range(CA_HEADS):
            hs = slice(CA_HEAD_DIM * h, CA_HEAD_DIM * (h + 1))
            s = _dot(q_ref[:, hs], k_ref[:, hs], NT) * scale
            p = jnp.exp(s - jnp.max(s, axis=-1, keepdims=True))
            p = p / jnp.sum(p, axis=-1, keepdims=True)
            o_ref[:, hs] = _dot(_bf(p), v_ref[:, hs]).astype(o_ref.dtype)

    full = pl.BlockSpec((M, W), lambda i: (0, 0))
    return pl.pallas_call(
        body, name=name, grid=(T // tq,), in_specs=[_row_spec(tq, W), full, full], out_specs=_row_spec(tq, W),
        out_shape=jax.ShapeDtypeStruct((T, W), BF16), compiler_params=_params("parallel"),
    )(q, k, v)


def _ca_bwd(q, k, v, do, name):
    T, W = q.shape
    M = k.shape[0]
    tq = min(CA_ROWS, T)
    scale = CA_HEAD_DIM ** -0.5

    def body(q_ref, k_ref, v_ref, do_ref, dq_ref, dk_ref, dv_ref):
        @pl.when(pl.program_id(0) == 0)
        def _():
            dk_ref[...] = jnp.zeros_like(dk_ref)
            dv_ref[...] = jnp.zeros_like(dv_ref)

        for h in range(CA_HEADS):
            hs = slice(CA_HEAD_DIM * h, CA_HEAD_DIM * (h + 1))
            qh, kh, vh, doh = q_ref[:, hs], k_ref[:, hs], v_ref[:, hs], do_ref[:, hs]
            s = _dot(qh, kh, NT) * scale
            p = jnp.exp(s - jnp.max(s, axis=-1, keepdims=True))
            p = p / jnp.sum(p, axis=-1, keepdims=True)
            dp = _dot(doh, vh, NT)
            ds = _bf(p * (dp - jnp.sum(p * dp, axis=-1, keepdims=True)) * scale)
            dq_ref[:, hs] = _dot(ds, kh).astype(dq_ref.dtype)
            dk_ref[:, hs] += _dot(ds, qh, TN)
            dv_ref[:, hs] += _dot(_bf(p), doh, TN)

    full = pl.BlockSpec((M, W), lambda i: (0, 0))
    return pl.pallas_call(
        body, name=name, grid=(T // tq,), in_specs=[_row_spec(tq, W), full, full, _row_spec(tq, W)],
        out_specs=[_row_spec(tq, W), full, full],
        out_shape=[jax.ShapeDtypeStruct((T, W), BF16), jax.ShapeDtypeStruct((M, W), F32), jax.ShapeDtypeStruct((M, W), F32)],
        compiler_params=_params("arbitrary"),
    )(q, k, v, do)


FFN_ROWS = 256
FFN_COLS = 1408
GELU_C0 = 0.7978845608028654
GELU_C1 = 0.044715


def _gelu(x):
    t = jnp.tanh(GELU_C0 * (x + GELU_C1 * x * x * x))
    return 0.5 * x * (1.0 + t), t


def _gelu_grad(x, t):
    return 0.5 * (1.0 + t) + 0.5 * x * (1.0 - t * t) * GELU_C0 * (1.0 + 3.0 * GELU_C1 * x * x)


def _shift_down(cur, halo, first, tb):
    row = lax.broadcasted_iota(jnp.int32, (tb, 1), 0)
    h6 = jnp.where(first, 0.0, halo[6:7])
    h7 = jnp.where(first, 0.0, halo[7:8])
    u1 = jnp.where(row == 0, h7, pltpu.roll(cur, 1, 0))
    u2 = jnp.where(row == 0, h6, jnp.where(row == 1, h7, pltpu.roll(cur, 2, 0)))
    return u1, u2


def _conv(u_ref, halo_ref, w_ref, b_ref, half, first, tb):
    cur = u_ref[half]
    u1, u2 = _shift_down(cur, halo_ref[half], first, tb)
    w = w_ref[...]
    return w[0:1] * u2 + w[1:2] * u1 + w[2:3] * cur + b_ref[...], cur, u1, u2


def _ffn_specs(tb, tc, rows_first):
    nj = D_FF // tc
    rc = (lambda a, b: (a, b)) if rows_first else (lambda a, b: (b, a))
    def at(f):
        return lambda a, b: f(*rc(a, b))
    blk = pl.BlockSpec((2, tb, tc), at(lambda t, j: (0, t, j)))
    halo = pl.BlockSpec((2, 8, tc), at(lambda t, j: (0, jnp.maximum(t * (tb // 8) - 1, 0), j)))
    wg = pl.BlockSpec((3, tc), at(lambda t, j: (0, j)))
    wv = pl.BlockSpec((3, tc), at(lambda t, j: (0, j + nj)))
    bg = pl.BlockSpec((1, tc), at(lambda t, j: (0, j)))
    bv = pl.BlockSpec((1, tc), at(lambda t, j: (0, j + nj)))
    flat = pl.BlockSpec((tb, tc), at(lambda t, j: (t, j)))
    return blk, halo, wg, wv, bg, bv, flat


def _glu_fwd(u, cw, cb, name):
    T = u.shape[1]
    tb, tc = min(FFN_ROWS, T), FFN_COLS

    def body(u_ref, halo_ref, wg_ref, wv_ref, bg_ref, bv_ref, a_ref):
        first = pl.program_id(0) == 0
        cg = _conv(u_ref, halo_ref, wg_ref, bg_ref, 0, first, tb)[0]
        cv = _conv(u_ref, halo_ref, wv_ref, bv_ref, 1, first, tb)[0]
        a_ref[...] = (_gelu(cg)[0] * cv).astype(a_ref.dtype)

    blk, halo, wg, wv, bg, bv, flat = _ffn_specs(tb, tc, True)
    return pl.pallas_call(
        body, name=name, grid=(T // tb, D_FF // tc), in_specs=[blk, halo, wg, wv, bg, bv], out_specs=flat,
        out_shape=jax.ShapeDtypeStruct((T, D_FF), BF16), compiler_params=_params("parallel", "parallel"),
    )(u, u, cw, cw, cb, cb)


def _glu_bwd(u, cw, cb, da, name, exchange=None):
    T = u.shape[1]
    tb, tc = min(FFN_ROWS, T), FFN_COLS

    def body(u_ref, halo_ref, wg_ref, wv_ref, bg_ref, bv_ref, da_ref, dc_ref, db_ref, dw_ref):
        first = pl.program_id(1) == 0

        @pl.when(first)
        def _():
            db_ref[...] = jnp.zeros_like(db_ref)
            dw_ref[...] = jnp.zeros_like(dw_ref)

        cg, ug, ug1, ug2 = _conv(u_ref, halo_ref, wg_ref, bg_ref, 0, first, tb)
        cv, uv, uv1, uv2 = _conv(u_ref, halo_ref, wv_ref, bv_ref, 1, first, tb)
        da = da_ref[...]
        gl, t = _gelu(cg)
        dcg = da * cv * _gelu_grad(cg, t)
        dcv = da * gl
        dc_ref[0] = dcg
        dc_ref[1] = dcv
        for half, dc, taps in ((0, dcg, (ug2, ug1, ug)), (1, dcv, (uv2, uv1, uv))):
            db_ref[half] += jnp.sum(dc, axis=0, keepdims=True)
            for tap in range(3):
                dw_ref[half, tap:tap + 1, :] += jnp.sum(dc * taps[tap], axis=0, keepdims=True)

    blk, halo, wg, wv, bg, bv, flat = _ffn_specs(tb, tc, False)
    return _hosted_call(
        body, name=name, grid=(D_FF // tc, T // tb), in_specs=[blk, halo, wg, wv, bg, bv, flat],
        out_specs=[blk, pl.BlockSpec((2, 1, tc), lambda j, t: (0, 0, j)), pl.BlockSpec((2, 3, tc), lambda j, t: (0, 0, j))],
        out_shape=[jax.ShapeDtypeStruct((2, T, D_FF), F32), jax.ShapeDtypeStruct((2, 1, D_FF), F32),
                   jax.ShapeDtypeStruct((2, 3, D_FF), F32)],
        scratch=[], args=(u, u, cw, cw, cb, cb, da), semantics=("parallel", "arbitrary"), exchange=exchange)


def _conv_bwd(dc, cw, name):
    T = dc.shape[1]
    tb, tc = min(FFN_ROWS, T), FFN_COLS
    nt, nj = T // tb, D_FF // tc

    def body(dc_ref, halo_ref, wg_ref, wv_ref, du_ref):
        last = pl.program_id(0) == nt - 1
        row = lax.broadcasted_iota(jnp.int32, (tb, 1), 0)
        for half, w_ref in ((0, wg_ref), (1, wv_ref)):
            cur = dc_ref[half]
            halo = halo_ref[half]
            h0 = jnp.where(last, 0.0, halo[0:1])
            h1 = jnp.where(last, 0.0, halo[1:2])
            d1 = jnp.where(row == tb - 1, h0, pltpu.roll(cur, tb - 1, 0))
            d2 = jnp.where(row == tb - 1, h1, jnp.where(row == tb - 2, h0, pltpu.roll(cur, tb - 2, 0)))
            w = w_ref[...]
            du_ref[half] = (w[2:3] * cur + w[1:2] * d1 + w[0:1] * d2).astype(du_ref.dtype)

    blk = pl.BlockSpec((2, tb, tc), lambda t, j: (0, t, j))
    halo = pl.BlockSpec((2, 8, tc), lambda t, j: (0, jnp.minimum((t + 1) * (tb // 8), T // 8 - 1), j))
    wg = pl.BlockSpec((3, tc), lambda t, j: (0, j))
    wv = pl.BlockSpec((3, tc), lambda t, j: (0, j + nj))
    return pl.pallas_call(
        body, name=name, grid=(nt, nj), in_specs=[blk, halo, wg, wv], out_specs=blk,
        out_shape=jax.ShapeDtypeStruct((2, T, D_FF), BF16), compiler_params=_params("parallel", "parallel"),
    )(dc, dc, cw, cw)


def _mesh_pos():
    return lax.axis_index("x"), lax.axis_index("y"), lax.axis_index("c")


def _peer(pos, k):
    return (pos[0] ^ ((k >> 2) & 1), pos[1] ^ ((k >> 1) & 1), pos[2] ^ (k & 1))


def _index(pos):
    return 4 * pos[0] + 2 * pos[1] + pos[2]


class _Exchange:
    def __init__(self, kind, buf, relay=False):
        assert kind in ("gather", "scatter") and not (relay and kind == "scatter")
        self.kind, self.buf, self.relay = kind, buf, relay
        self.out_shape = jax.ShapeDtypeStruct(((N_DEV,) + buf.shape) if kind == "gather" else buf.shape, buf.dtype)
        self.spec = pl.BlockSpec(memory_space=pl.ANY)
        self.scratch = [pltpu.SemaphoreType.DMA((N_DEV - 1,)), pltpu.SemaphoreType.DMA((N_DEV - 1,)),
                        pltpu.SemaphoreType.DMA]

    def _src(self, x_ref, dest):
        return x_ref if self.kind == "gather" else x_ref.at[dest]

    def _copies(self, x_ref, out_ref, send_sems, recv_sems, local_sem):
        pos = _mesh_pos()
        me = _index(pos)
        local = pltpu.make_async_copy(self._src(x_ref, me), out_ref.at[me], local_sem)
        sends, recvs = [], []
        for k in range(1, N_DEV):
            peer = _peer(pos, k)
            sends.append(pltpu.make_async_remote_copy(
                src_ref=self._src(x_ref, _index(peer)), dst_ref=out_ref.at[me], send_sem=send_sems.at[k - 1],
                recv_sem=recv_sems.at[k - 1], device_id=peer, device_id_type=pl.DeviceIdType.MESH))
            recvs.append(pltpu.make_async_remote_copy(
                src_ref=self._src(x_ref, me), dst_ref=out_ref.at[_index(peer)], send_sem=send_sems.at[k - 1],
                recv_sem=recv_sems.at[k - 1], device_id=peer, device_id_type=pl.DeviceIdType.MESH))
        return local, sends, recvs

    def _relay_copies(self, x_ref, out_ref, send_sems, recv_sems, local_sem):
        x, y, c = _mesh_pos()
        me, sibling = (x, y, c), (x, y, 1 - c)
        chips = [(1 - x, y), (x, 1 - y), (1 - x, 1 - y)]

        def copy(k, block, to, own=False):
            return pltpu.make_async_remote_copy(
                src_ref=x_ref if own else out_ref.at[_index(block)], dst_ref=out_ref.at[_index(block)],
                send_sem=send_sems.at[k], recv_sem=recv_sems.at[k], device_id=to, device_id_type=pl.DeviceIdType.MESH)

        local = pltpu.make_async_copy(x_ref, out_ref.at[_index(me)], local_sem)
        first = [copy(0, me, sibling, own=True)] + [copy(1 + j, me, (*chip, c), own=True) for j, chip in enumerate(chips)]
        landed = [copy(1 + j, (*chip, c), me) for j, chip in enumerate(chips)]
        passed = [copy(4 + j, (*chip, c), sibling) for j, chip in enumerate(chips)]
        from_sibling = [copy(0, sibling, me)] + [copy(4 + j, (*chip, 1 - c), me) for j, chip in enumerate(chips)]
        return local, first, landed, passed, from_sibling

    def start(self, *refs):
        if self.relay:
            local, first = self._relay_copies(*refs)[:2]
            local.start()
            for cp in first:
                cp.start()
            return
        local, sends, _ = self._copies(*refs)
        local.start()
        for cp in sends:
            cp.start()

    def finish(self, *refs):
        if self.relay:
            local, first, landed, passed, from_sibling = self._relay_copies(*refs)
            for got, forward in zip(landed, passed):
                got.wait_recv()
                forward.start()
            for cp in from_sibling:
                cp.wait_recv()
            for cp in first + passed:
                cp.wait_send()
            local.wait()
            return
        local, sends, recvs = self._copies(*refs)
        for cp in recvs:
            cp.wait_recv()
        for cp in sends:
            cp.wait_send()
        local.wait()


def _hosted_call(body, *, name, grid, in_specs, out_specs, out_shape, scratch, args, semantics, exchange=None):
    if exchange is None:
        return pl.pallas_call(
            body, name=name, grid=grid, in_specs=in_specs, out_specs=out_specs, out_shape=out_shape,
            scratch_shapes=scratch, compiler_params=_params(*semantics))(*args)
    n_in, n_out, n_scr = len(in_specs), len(out_specs), len(scratch)

    def hosted(*refs):
        ins, x_ref = refs[:n_in], refs[n_in]
        outs, land_ref = refs[n_in + 1:n_in + 1 + n_out], refs[n_in + 1 + n_out]
        rest = refs[n_in + n_out + 2:]
        sems = rest[n_scr:]
        ids = [pl.program_id(a) for a in range(len(grid))]
        first, last = ids[0] == 0, ids[0] == grid[0] - 1
        for a in range(1, len(grid)):
            first, last = first & (ids[a] == 0), last & (ids[a] == grid[a] - 1)

        @pl.when(first)
        def _():
            exchange.start(x_ref, land_ref, *sems)

        body(*ins, *outs, *rest[:n_scr])

        @pl.when(last)
        def _():
            exchange.finish(x_ref, land_ref, *sems)

    return pl.pallas_call(
        hosted, name=name, grid=grid, in_specs=list(in_specs) + [exchange.spec],
        out_specs=list(out_specs) + [exchange.spec], out_shape=list(out_shape) + [exchange.out_shape],
        scratch_shapes=list(scratch) + exchange.scratch, compiler_params=_params(*(["arbitrary"] * len(grid))),
    )(*args, exchange.buf)


def _exchange_alone(exchange, name):
    def body(x_ref, out_ref, send_sems, recv_sems, local_sem):
        exchange.start(x_ref, out_ref, send_sems, recv_sems, local_sem)
        exchange.finish(x_ref, out_ref, send_sems, recv_sems, local_sem)

    return pl.pallas_call(
        body, name=name, out_shape=exchange.out_shape, in_specs=[exchange.spec], out_specs=exchange.spec,
        scratch_shapes=exchange.scratch)(exchange.buf)


def _adamw(w, g, m, v):
    m = ADAM_B1 * m + (1.0 - ADAM_B1) * g
    v = ADAM_B2 * v + (1.0 - ADAM_B2) * (g * g)
    m_hat = m / (1.0 - ADAM_B1 ** ADAM_STEP)
    v_hat = v / (1.0 - ADAM_B2 ** ADAM_STEP)
    delta = -ADAM_LR * (m_hat / (jnp.sqrt(v_hat) + ADAM_EPS) + ADAM_WD * w)
    return delta, m, v


def _sum_rows(parts, r0, rows, name, wmv=None):
    C = parts.shape[2]
    tr = max(t for t in range(16, ROWS + 1, 16) if rows % t == 0 and r0 % t == 0)

    def total(p_ref):
        g = p_ref[0].astype(F32)
        for i in range(1, N_DEV):
            g = g + p_ref[i].astype(F32)
        return g

    p_spec = pl.BlockSpec((N_DEV, tr, C), lambda i: (0, r0 // tr + i, 0))
    if wmv is None:
        def body(p_ref, g_ref):
            g_ref[...] = total(p_ref)

        return pl.pallas_call(
            body, name=name, grid=(rows // tr,), in_specs=[p_spec], out_specs=_row_spec(tr, C),
            out_shape=jax.ShapeDtypeStruct((rows, C), F32), compiler_params=_params("parallel"))(parts)

    def body(p_ref, w_ref, m_ref, v_ref, g_ref, d_ref, mo_ref, vo_ref):
        g = total(p_ref)
        g_ref[0] = g
        d_ref[0], mo_ref[0], vo_ref[0] = _adamw(w_ref[0], g, m_ref[0], v_ref[0])

    blk = pl.BlockSpec((1, tr, C), lambda i: (0, i, 0))
    return pl.pallas_call(
        body, name=name, grid=(rows // tr,), in_specs=[p_spec, blk, blk, blk], out_specs=[blk] * 4,
        out_shape=[jax.ShapeDtypeStruct((1, rows, C), F32)] * 4, compiler_params=_params("parallel"))(parts, *wmv)


def _sum_parts(parts, name):
    _, R, C = parts.shape

    def body(p_ref, g_ref):
        g = p_ref[0]
        for i in range(1, N_DEV):
            g = g + p_ref[i]
        g_ref[...] = g

    return pl.pallas_call(body, name=name, out_shape=jax.ShapeDtypeStruct((R, C), F32))(parts)


def _adamw_call(w, g, m, v, name):
    _, R, C = w.shape
    tr = min(ROWS, R)

    def body(w_ref, g_ref, m_ref, v_ref, d_ref, mo_ref, vo_ref):
        d_ref[...], mo_ref[...], vo_ref[...] = _adamw(w_ref[...], g_ref[...], m_ref[...], v_ref[...])

    blk = pl.BlockSpec((1, tr, C), lambda i: (0, i, 0))
    return pl.pallas_call(
        body, name=name, grid=(R // tr,), in_specs=[blk] * 4, out_specs=[blk] * 3,
        out_shape=[jax.ShapeDtypeStruct(w.shape, F32)] * 3, compiler_params=_params("parallel"))(w, g, m, v)


NORMS = ("mix_pre_norm", "mix_post_norm", "ca_pre_norm", "mem_norm", "ca_post_norm", "ffn_pre_norm", "ffn_post_norm")
SMALL = ("mix_pre_norm", "attn_sinks", "hgrn_lb_logits", "hgrn_out_norm", "mix_post_norm", "ca_pre_norm", "mem_norm",
         "ca_post_norm", "ffn_pre_norm", "ffn_conv_w", "ffn_conv_b", "ffn_post_norm")
SMALL_ROWS = 40
ROW_LOGITS, ROW_MISC, ROW_CONV_B, ROW_CONV_W = 7, 8, 9, 15
LANE_SINKS, LANE_LOSS = 128, 256
FF_PIECES = ((0, 1024), (1024, 2048), (2048, D_FF))


def _pack_small(norm_grads, dlogits, donw, dsinks, loss, d_cb, d_cw, name):
    def body(*refs):
        norm_refs = refs[:len(NORMS)]
        dl_ref, donw_ref, dsink_ref, loss_ref, cb_ref, cw_ref, out_ref = refs[len(NORMS):]
        out_ref[...] = jnp.zeros_like(out_ref)
        for i, ref in enumerate(norm_refs):
            out_ref[i:i + 1, :] = ref[...]
        out_ref[ROW_LOGITS:ROW_LOGITS + 1, 0:512] = dl_ref[0:1, :]
        out_ref[ROW_LOGITS:ROW_LOGITS + 1, 512:1024] = dl_ref[1:2, :]
        out_ref[ROW_MISC:ROW_MISC + 1, 0:HGRN_DIM] = donw_ref[...]
        out_ref[ROW_MISC:ROW_MISC + 1, LANE_SINKS:LANE_SINKS + ATTN_Q_HEADS] = dsink_ref[...]
        out_ref[ROW_MISC:ROW_MISC + 1, LANE_LOSS:LANE_LOSS + LANE] = loss_ref[...]
        for h in range(2):
            for j, (c0, c1) in enumerate(FF_PIECES):
                r = ROW_CONV_B + 3 * h + j
                out_ref[r:r + 1, 0:c1 - c0] = cb_ref[h, :, c0:c1]
                for t in range(3):
                    r = ROW_CONV_W + 3 * (3 * h + t) + j
                    out_ref[r:r + 1, 0:c1 - c0] = cw_ref[h, t:t + 1, c0:c1]

    return pl.pallas_call(
        body, name=name, out_shape=jax.ShapeDtypeStruct((SMALL_ROWS, 1024), F32),
    )(*norm_grads, dlogits, donw, dsinks, loss, d_cb, d_cw)


def _adamw_small(total, g_conv_w, w, m, v, name):
    n = len(SMALL)

    def body(*refs):
        t_ref, gcw_ref = refs[:2]
        w_refs, m_refs, v_refs = (dict(zip(SMALL, refs[2 + n * i:2 + n * (i + 1)])) for i in range(3))
        outs = refs[2 + 3 * n:]
        loss_ref = outs[0]
        g_refs, d_refs, mo_refs, vo_refs = (dict(zip(SMALL, outs[1 + n * i:1 + n * (i + 1)])) for i in range(4))
        loss_ref[...] = t_ref[ROW_MISC:ROW_MISC + 1, LANE_LOSS:LANE_LOSS + 1]

        def step(nm, idx, g):
            g_refs[nm][idx] = g
            d_refs[nm][idx], mo_refs[nm][idx], vo_refs[nm][idx] = _adamw(w_refs[nm][idx], g, m_refs[nm][idx], v_refs[nm][idx])

        everything = (slice(None), slice(None))
        for i, nm in enumerate(NORMS):
            step(nm, everything, t_ref[i:i + 1, :])
        step("hgrn_lb_logits", (slice(0, 1), slice(None)), t_ref[ROW_LOGITS:ROW_LOGITS + 1, 0:512])
        step("hgrn_lb_logits", (slice(1, 2), slice(None)), t_ref[ROW_LOGITS:ROW_LOGITS + 1, 512:1024])
        step("hgrn_out_norm", everything, t_ref[ROW_MISC:ROW_MISC + 1, 0:HGRN_DIM])
        step("attn_sinks", everything, t_ref[ROW_MISC:ROW_MISC + 1, LANE_SINKS:LANE_SINKS + ATTN_Q_HEADS])
        for h in range(2):
            for j, (c0, c1) in enumerate(FF_PIECES):
                r = ROW_CONV_B + 3 * h + j
                step("ffn_conv_b", (slice(None), slice(D_FF * h + c0, D_FF * h + c1)), t_ref[r:r + 1, 0:c1 - c0])
        step("ffn_conv_w", (slice(None), slice(None), slice(None)), gcw_ref[...])

    shapes = [jax.ShapeDtypeStruct(w[nm].shape, F32) for nm in SMALL]
    out = pl.pallas_call(
        body, name=name, out_shape=[jax.ShapeDtypeStruct((1, 1), F32)] + shapes * 4,
    )(total, g_conv_w, *[w[nm] for nm in SMALL], *[m[nm] for nm in SMALL], *[v[nm] for nm in SMALL])
    trees = [dict(zip(SMALL, out[1 + n * i:1 + n * (i + 1)])) for i in range(4)]
    return out[0], trees


BIG = ("w_in", "w_out", "ca_wq", "ca_wk", "ca_wv", "ca_wo", "ffn_w_up", "ffn_w_down")
BIG_FULL = {"w_in": (1024, 2816), "w_out": (1024, 1024), "ca_wq": (1024, 1024), "ca_wk": (1024, 1024),
            "ca_wv": (1024, 1024), "ca_wo": (1024, 1024), "ffn_w_up": (1024, 5632), "ffn_w_down": (2816, 1024)}
G_IN, G_MID, G_UP, G_DOWN = ("w_in",), ("w_out", "ca_wq", "ca_wk", "ca_wv", "ca_wo"), ("ffn_w_up",), ("ffn_w_down",)
GROUPS = (G_IN, G_MID, G_UP, G_DOWN)
COL_SHARDED = ("w_in", "ffn_w_up")
PACK_COLS = 1024


def _big_rows(name):
    r, c = BIG_FULL[name]
    return r * c // N_DEV // PACK_COLS


def _pack_shards(w, names):
    rows = [w[n][0].T if n in COL_SHARDED else w[n][0] for n in names]
    return (rows[0] if len(rows) == 1 else jnp.concatenate(rows, axis=0)).astype(BF16)


def _unpack_gathered(gathered, names):
    out, r0 = {}, 0
    for n in names:
        rows = _big_rows(n)
        out[n] = gathered[:, r0:r0 + rows].reshape(N_DEV * rows, PACK_COLS)
        r0 += rows
    return out


def _pack_full_grads(grads, names):
    parts = [grads[n].reshape(N_DEV, _big_rows(n), PACK_COLS) for n in names]
    return parts[0] if len(parts) == 1 else jnp.concatenate(parts, axis=1)


def kernel(x, mem, mix_pre_norm, w_in, attn_sinks, hgrn_lb_logits, hgrn_out_norm, w_out, mix_post_norm, ca_pre_norm, mem_norm, ca_wq, ca_wk, ca_wv, ca_wo, ca_post_norm, ffn_pre_norm, ffn_w_up, ffn_conv_w, ffn_conv_b, ffn_w_down, ffn_post_norm, loss_target, m_mix_pre_norm, m_w_in, m_attn_sinks, m_hgrn_lb_logits, m_hgrn_out_norm, m_w_out, m_mix_post_norm, m_ca_pre_norm, m_mem_norm, m_ca_wq, m_ca_wk, m_ca_wv, m_ca_wo, m_ca_post_norm, m_ffn_pre_norm, m_ffn_w_up, m_ffn_conv_w, m_ffn_conv_b, m_ffn_w_down, m_ffn_post_norm, v_mix_pre_norm, v_w_in, v_attn_sinks, v_hgrn_lb_logits, v_hgrn_out_norm, v_w_out, v_mix_post_norm, v_ca_pre_norm, v_mem_norm, v_ca_wq, v_ca_wk, v_ca_wv, v_ca_wo, v_ca_post_norm, v_ffn_pre_norm, v_ffn_w_up, v_ffn_conv_w, v_ffn_conv_b, v_ffn_w_down, v_ffn_post_norm):
    names = ["mix_pre_norm", "w_in", "attn_sinks", "hgrn_lb_logits", "hgrn_out_norm", "w_out", "mix_post_norm",
             "ca_pre_norm", "mem_norm", "ca_wq", "ca_wk", "ca_wv", "ca_wo", "ca_post_norm", "ffn_pre_norm",
             "ffn_w_up", "ffn_conv_w", "ffn_conv_b", "ffn_w_down", "ffn_post_norm"]
    w_all = dict(zip(names, [mix_pre_norm, w_in, attn_sinks, hgrn_lb_logits, hgrn_out_norm, w_out, mix_post_norm,
                             ca_pre_norm, mem_norm, ca_wq, ca_wk, ca_wv, ca_wo, ca_post_norm, ffn_pre_norm,
                             ffn_w_up, ffn_conv_w, ffn_conv_b, ffn_w_down, ffn_post_norm]))
    m_all = dict(zip(names, [m_mix_pre_norm, m_w_in, m_attn_sinks, m_hgrn_lb_logits, m_hgrn_out_norm, m_w_out,
                             m_mix_post_norm, m_ca_pre_norm, m_mem_norm, m_ca_wq, m_ca_wk, m_ca_wv, m_ca_wo,
                             m_ca_post_norm, m_ffn_pre_norm, m_ffn_w_up, m_ffn_conv_w, m_ffn_conv_b, m_ffn_w_down,
                             m_ffn_post_norm]))
    v_all = dict(zip(names, [v_mix_pre_norm, v_w_in, v_attn_sinks, v_hgrn_lb_logits, v_hgrn_out_norm, v_w_out,
                             v_mix_post_norm, v_ca_pre_norm, v_mem_norm, v_ca_wq, v_ca_wk, v_ca_wv, v_ca_wo,
                             v_ca_post_norm, v_ffn_pre_norm, v_ffn_w_up, v_ffn_conv_w, v_ffn_conv_b, v_ffn_w_down,
                             v_ffn_post_norm]))
    dev = _index(_mesh_pos())

    w_packs = {grp: _pack_shards(w_all, grp) for grp in GROUPS}
    shard_w = D_FF * 2 // N_DEV
    conv_w_rows = _exchange_alone(_Exchange("gather", ffn_conv_w[0]), "gather_conv_w")
    conv_w_full = conv_w_rows.transpose(1, 0, 2).reshape(3, 2 * D_FF)

    received, small_pack, grad_x = _local_step(
        x[0], mem[0], loss_target[0], w_packs, conv_w_full,
        {n: w_all[n] for n in NORMS}, attn_sinks, hgrn_lb_logits, hgrn_out_norm, ffn_conv_b)

    total = _sum_parts(_exchange_alone(_Exchange("gather", small_pack), "gather_small"), "sum_small")
    cw = total[ROW_CONV_W:ROW_CONV_W + 18].reshape(2, 3, 3 * PACK_COLS)[:, :, :D_FF]
    cw = cw.transpose(1, 0, 2).reshape(3, 2 * D_FF)
    g_conv_w = lax.dynamic_slice_in_dim(cw, dev * shard_w, shard_w, axis=1)[None]
    loss, (out_g, out_d, out_m, out_v) = _adamw_small(total, g_conv_w, w_all, m_all, v_all, "adamw_small")

    for grp in GROUPS:
        r0 = 0
        for n in grp:
            rows = _big_rows(n)
            if n in COL_SHARDED:
                g = _sum_rows(received[grp], r0, rows, "sum_" + n).T[None]
                d, mo, vo = _adamw_call(w_all[n], g, m_all[n], v_all[n], "adamw_" + n)
            else:
                g, d, mo, vo = _sum_rows(received[grp], r0, rows, "adamw_" + n, wmv=(w_all[n], m_all[n], v_all[n]))
            out_g[n], out_d[n], out_m[n], out_v[n] = g, d, mo, vo
            r0 += rows

    return (loss[0, 0], grad_x[None], *[out_g[n] for n in names], *[out_d[n] for n in names],
            *[out_m[n] for n in names], *[out_v[n] for n in names])


def _local_step(x, mem, target, w_packs, conv_w, norms, sinks, lb_logits, out_norm, conv_b):
    g1, g2, g3 = norms["mix_pre_norm"], norms["mix_post_norm"], norms["ca_pre_norm"]
    g4, g5, g6, g7 = norms["mem_norm"], norms["ca_post_norm"], norms["ffn_pre_norm"], norms["ffn_post_norm"]

    h1, gathered = _norm_fwd(x, g1, "mix_norm", exchange=_Exchange("gather", w_packs[G_IN], relay=True))
    w_in_t = _unpack_gathered(gathered, G_IN)["w_in"]
    up_shard = w_packs[G_UP]
    up_rows = up_shard.shape[0]
    up_cuts = (0, up_rows // 2, 3 * up_rows // 4, up_rows)
    up_parts = [up_shard[a:b] for a, b in zip(up_cuts[:-1], up_cuts[1:])]
    z, up_0 = _mm(h1, w_in_t, mode="nt", out_dtype=BF16, name="in_proj", tn=1408,
                  exchange=_Exchange("gather", up_parts[0], relay=True))
    attn, lse, gathered = _swa_fwd(z, sinks, "swa_fwd", exchange=_Exchange("gather", w_packs[G_DOWN], relay=True))
    w_down = _unpack_gathered(gathered, G_DOWN)["ffn_w_down"]
    lb = _lower_bound(lb_logits, "lower_bound")
    rec, o_rec, states, gathered = _hgrn_fwd(
        z, lb, out_norm, "hgrn_fwd", exchange=_Exchange("gather", w_packs[G_MID], relay=True))
    w_out, wq, wk, wv, wo = (_unpack_gathered(gathered, G_MID)[n] for n in G_MID)
    cat = jnp.concatenate([attn, rec], axis=1)
    mix = _mm(cat, w_out, mode="nn", out_dtype=BF16, name="out_proj")
    x1, h2, up_1 = _post_pre(x, mix, g2, g3, "mix_post", exchange=_Exchange("gather", up_parts[1], relay=True))
    mem_n = _norm_fwd(mem, g4, "mem_norm")
    q = _mm(h2, wq, mode="nn", out_dtype=BF16, name="ca_q")
    k = _mm(mem_n, wk, mode="nn", out_dtype=BF16, name="ca_k")
    v = _mm(mem_n, wv, mode="nn", out_dtype=BF16, name="ca_v")
    oc = _ca_fwd(q, k, v, "ca_fwd")
    c = _mm(oc, wo, mode="nn", out_dtype=BF16, name="ca_o")
    x2, h3, up_2 = _post_pre(x1, c, g5, g6, "ca_post", exchange=_Exchange("gather", up_parts[2], relay=True))
    w_up_t = jnp.concatenate([up_0, up_1, up_2], axis=1).reshape(-1, PACK_COLS)
    u = _mm(h3, w_up_t, mode="nt", out_dtype=F32, name="ffn_up", tn=1408, split_out=True)
    a = _glu_fwd(u, conv_w, conv_b, "glu_fwd")
    y = _mm(a, w_down, mode="nn", out_dtype=BF16, name="ffn_down", tk=2816)
    loss, dx3, dy, dg7 = _final(x2, y, g7, target, "loss_head")

    da = _mm(dy, w_down, mode="nt", out_dtype=F32, name="ffn_down_dx", tn=1408)
    d_w_down = _mm(a, dy, mode="tn", out_dtype=BF16, name="ffn_down_dw", tm=1408, tk=1024)
    dc, d_cb, d_cw, got_down = _glu_bwd(
        u, conv_w, conv_b, da, "glu_bwd",
        exchange=_Exchange("scatter", _pack_full_grads({"ffn_w_down": d_w_down}, G_DOWN)))
    du = _conv_bwd(dc, conv_w, "conv_bwd")
    d_w_up_t = _mm(du, h3, mode="tn", out_dtype=BF16, name="ffn_up_dw", tm=1408, tk=1024, split_a=True)
    dh3, got_up = _mm(du, w_up_t, mode="nn", out_dtype=BF16, name="ffn_up_dx", tm=2048, tk=1408, split_a=True,
                      exchange=_Exchange("scatter", _pack_full_grads({"ffn_w_up": d_w_up_t}, G_UP)))
    dx2, dcv, dg6, dg5 = _norm_bwd2(dx3, dh3, x2, g6, c, g5, "ca_post_bwd")
    doc = _mm(dcv, wo, mode="nt", out_dtype=BF16, name="ca_o_dx")
    d_wo = _mm(oc, dcv, mode="tn", out_dtype=BF16, name="ca_o_dw", tm=1024, tk=1024)
    dq, dk, dv = _ca_bwd(q, k, v, doc, "ca_bwd")
    d_wq = _mm(h2, dq, mode="tn", out_dtype=BF16, name="ca_q_dw", tm=1024, tk=1024)
    dh2 = _mm(dq, wq, mode="nt", out_dtype=BF16, name="ca_q_dx")
    d_wk = _mm(mem_n, dk, mode="tn", out_dtype=BF16, name="ca_k_dw", tm=1024)
    d_wv = _mm(mem_n, dv, mode="tn", out_dtype=BF16, name="ca_v_dw", tm=1024)
    dmem_k = _mm(dk, wk, mode="nt", out_dtype=F32, name="ca_k_dx")
    dmem_v = _mm(dv, wv, mode="nt", out_dtype=F32, name="ca_v_dx")
    dg4 = _gain_bwd(mem, dmem_k, dmem_v, "mem_norm_bwd")
    dx1, dmix, dg3, dg2 = _norm_bwd2(dx2, dh2, x1, g3, mix, g2, "mix_post_bwd")
    dcat = _mm(dmix, w_out, mode="nt", out_dtype=BF16, name="out_proj_dx")
    d_w_out = _mm(cat, dmix, mode="tn", out_dtype=BF16, name="out_proj_dw", tm=1024, tk=1024)
    mid = {"w_out": d_w_out, "ca_wq": d_wq, "ca_wk": d_wk, "ca_wv": d_wv, "ca_wo": d_wo}
    dz_rec, dlb, donw, got_mid = _hgrn_bwd(
        z, lb, out_norm, o_rec, states, dcat, "hgrn_bwd", exchange=_Exchange("scatter", _pack_full_grads(mid, G_MID)))
    dq_a, dka, dkb, dva, dvb, dsinks = _swa_bwd(z, sinks, dcat, lse, "swa_bwd")
    dz_att = _assemble_attn(dq_a, dka, dkb, dva, dvb, "assemble_attn")
    d_w_in_t = jnp.concatenate(
        [_mm(dz_att, h1, mode="tn", out_dtype=BF16, name="in_proj_dw_att", tm=Z_Q, tk=1024),
         _mm(dz_rec, h1, mode="tn", out_dtype=BF16, name="in_proj_dw_rec", tm=1024, tk=1024)], axis=0)
    dh1_att = _mm(dz_att, w_in_t[:Z_Q], mode="nn", out_dtype=BF16, name="in_proj_dx_att", tk=Z_Q)
    dh1_rec, got_in = _mm(dz_rec, w_in_t[Z_Q:], mode="nn", out_dtype=BF16, name="in_proj_dx_rec", tk=2816 - Z_Q,
                          exchange=_Exchange("scatter", _pack_full_grads({"w_in": d_w_in_t}, G_IN)))
    dx, dg1 = _norm_bwd1(dx1, dh1_att, dh1_rec, x, g1, "mix_norm_bwd")

    small_pack = _pack_small(
        (dg1, dg2, dg3, dg4, dg5, dg6, dg7), _lower_bound_bwd(lb, dlb, "lower_bound_bwd"), donw, dsinks, loss,
        d_cb, d_cw, "pack_small")
    return {G_IN: got_in, G_MID: got_mid, G_UP: got_up, G_DOWN: got_down}, small_pack, dx
```

```python
import jax
import jax.numpy as jnp
from jax import lax
from jax.experimental import pallas as pl
from jax.experimental.pallas import tpu as pltpu

F32 = jnp.float32
BF16 = jnp.bfloat16
EPS = 1e-6
N_DEV = 8
MESH_AXES = ("x", "y", "c")

ATTN_HEAD_DIM = 64
ATTN_Q_HEADS = 8
ATTN_KV_HEADS = 2
ATTN_BLOCK = 128
HGRN_HEADS = 4
HGRN_DIM = 128
HGRN_CHUNK = 64
HGRN_PAIR = 4
Z_Q, Z_F, Z_I, Z_G = 768, 1280, 1792, 2304
HGRN_LEVELS = (32, 16, 8, 4, 2, 1)
CA_HEADS = 4
CA_HEAD_DIM = 256
D_FF = 2816

ADAM_LR = 0.001
ADAM_B1 = 0.9
ADAM_B2 = 0.999
ADAM_EPS = 1e-08
ADAM_WD = 0.01
ADAM_STEP = 10

VMEM_LIMIT = 56 << 20
LANE = 128

NT = (((1,), (1,)), ((), ()))
TN = (((0,), (0,)), ((), ()))


def _params(*sem):
    return pltpu.CompilerParams(dimension_semantics=sem, vmem_limit_bytes=VMEM_LIMIT)


def _tile(n, cap):
    if n <= cap:
        return n
    best = 0
    for t in range(LANE, cap + 1, LANE):
        if n % t == 0:
            best = t
    assert best, (n, cap)
    return best


def _dot(a, b, dims=None):
    if dims is None:
        return jnp.dot(a, b, preferred_element_type=F32)
    return lax.dot_general(a, b, dims, preferred_element_type=F32)


def _bf(x):
    return x.astype(BF16)


def _sigmoid(x):
    return 1.0 / (1.0 + jnp.exp(-x))


def _rms(x):
    r = lax.rsqrt(jnp.mean(x * x, axis=-1, keepdims=True) + EPS)
    return x * r, r


def _rms_bwd(dxh, xh, r):
    return r * (dxh - xh * jnp.mean(dxh * xh, axis=-1, keepdims=True))


def _mm(a, b, *, mode, out_dtype, name, tm=1024, tn=1024, tk=1024, split_a=False, split_b=False, split_out=False,
        exchange=None):
    def dims(arr, split):
        if split:
            return arr.shape[1], 2 * arr.shape[2]
        return arr.shape

    ar, ac = dims(a, split_a)
    br, bc = dims(b, split_b)
    if mode == "nn":
        M, K, N = ar, ac, bc
        assert br == K
    elif mode == "nt":
        M, K, N = ar, ac, br
        assert bc == K
    else:
        K, M, N = ar, ac, bc
        assert br == K
    a_cols_half = ac // 2 if split_a else None
    b_cols_half = bc // 2 if split_b else None
    tm = _tile(M, tm)
    tn = _tile((N // 2) if (split_out or (split_b and mode != "nt")) else N, tn)
    tk = _tile((K // 2) if ((split_a and mode != "tn") or (split_b and mode == "nt")) else K, tk)
    if split_a and mode == "tn":
        tm = _tile(M // 2, tm)
    gm, gn, gk = M // tm, N // tn, K // tk
    a_bytes, b_bytes = a.size * a.dtype.itemsize, b.size * b.dtype.itemsize
    rows_outer = gk > 1 or a_bytes + gm * b_bytes <= gn * a_bytes + b_bytes
    grid = (gm, gn, gk) if rows_outer else (gn, gm, gk)

    def spec(split, half, blk, rc):
        def imap(p, q, k):
            r, c = rc(*((p, q) if rows_outer else (q, p)), k)
            if not split:
                return (r, c)
            per_half = half // blk[1]
            return (c // per_half, r, c % per_half)

        return pl.BlockSpec(((None,) + blk) if split else blk, imap)

    if mode == "nn":
        a_spec = spec(split_a, a_cols_half, (tm, tk), lambda i, j, k: (i, k))
        b_spec = spec(split_b, b_cols_half, (tk, tn), lambda i, j, k: (k, j))
        dn = None
    elif mode == "nt":
        a_spec = spec(split_a, a_cols_half, (tm, tk), lambda i, j, k: (i, k))
        b_spec = spec(split_b, b_cols_half, (tn, tk), lambda i, j, k: (j, k))
        dn = NT
    else:
        a_spec = spec(split_a, a_cols_half, (tk, tm), lambda i, j, k: (k, i))
        b_spec = spec(split_b, b_cols_half, (tk, tn), lambda i, j, k: (k, j))
        dn = TN
    o_spec = spec(split_out, N // 2 if split_out else None, (tm, tn), lambda i, j, k: (i, j))
    out_shape = (2, M, N // 2) if split_out else (M, N)

    if gk == 1:
        def body(a_ref, b_ref, o_ref):
            o_ref[...] = _dot(_bf(a_ref[...]), _bf(b_ref[...]), dn).astype(o_ref.dtype)
        scratch = []
    else:
        def body(a_ref, b_ref, o_ref, acc_ref):
            k = pl.program_id(2)

            @pl.when(k == 0)
            def _():
                acc_ref[...] = jnp.zeros_like(acc_ref)

            acc_ref[...] += _dot(_bf(a_ref[...]), _bf(b_ref[...]), dn)

            @pl.when(k == gk - 1)
            def _():
                o_ref[...] = acc_ref[...].astype(o_ref.dtype)
        scratch = [pltpu.VMEM((tm, tn), F32)]

    out = _hosted_call(
        body, name=name, grid=grid, in_specs=[a_spec, b_spec], out_specs=[o_spec],
        out_shape=[jax.ShapeDtypeStruct(out_shape, out_dtype)], scratch=scratch, args=(a, b),
        semantics=("parallel", "parallel", "arbitrary"), exchange=exchange)
    return out[0] if exchange is None else out


ROWS = 512


def _row_spec(tr, cols):
    return pl.BlockSpec((tr, cols), lambda i: (i, 0))


def _vec_spec(cols):
    return pl.BlockSpec((1, cols), lambda i: (0, 0))


def _norm_fwd(x, g, name, exchange=None):
    T, Dm = x.shape
    tr = min(ROWS, T)

    def body(x_ref, g_ref, h_ref):
        xh, _ = _rms(x_ref[...])
        h_ref[...] = (xh * g_ref[...]).astype(h_ref.dtype)

    out = _hosted_call(
        body, name=name, grid=(T // tr,), in_specs=[_row_spec(tr, Dm), _vec_spec(Dm)], out_specs=[_row_spec(tr, Dm)],
        out_shape=[jax.ShapeDtypeStruct((T, Dm), BF16)], scratch=[], args=(x, g), semantics=("parallel",),
        exchange=exchange)
    return out[0] if exchange is None else out


def _post_pre(x, m, g_post, g_pre, name, exchange=None):
    T, Dm = x.shape
    tr = min(ROWS, T)

    def body(x_ref, m_ref, gp_ref, gn_ref, xo_ref, h_ref):
        mh, _ = _rms(m_ref[...].astype(F32))
        xn = x_ref[...] + mh * gp_ref[...]
        xo_ref[...] = xn
        xh, _ = _rms(xn)
        h_ref[...] = (xh * gn_ref[...]).astype(h_ref.dtype)

    return _hosted_call(
        body, name=name, grid=(T // tr,),
        in_specs=[_row_spec(tr, Dm), _row_spec(tr, Dm), _vec_spec(Dm), _vec_spec(Dm)],
        out_specs=[_row_spec(tr, Dm), _row_spec(tr, Dm)],
        out_shape=[jax.ShapeDtypeStruct((T, Dm), F32), jax.ShapeDtypeStruct((T, Dm), BF16)],
        scratch=[], args=(x, m, g_post, g_pre), semantics=("parallel",), exchange=exchange)


def _final(x2, y, g_post, target, name):
    T, Dm = x2.shape
    tr = min(ROWS, T)

    def body(x_ref, y_ref, g_ref, t_ref, loss_ref, dx_ref, dy_ref, dg_ref):
        @pl.when(pl.program_id(0) == 0)
        def _():
            loss_ref[...] = jnp.zeros_like(loss_ref)
            dg_ref[...] = jnp.zeros_like(dg_ref)

        g = g_ref[...]
        yh, r = _rms(y_ref[...].astype(F32))
        d = x_ref[...] + yh * g - t_ref[...]
        loss_ref[...] += jnp.zeros((1, LANE), F32) + 0.5 * jnp.sum(jnp.mean(d * d, axis=-1, keepdims=True))
        dx = d * (1.0 / Dm)
        dx_ref[...] = dx
        dy_ref[...] = _rms_bwd(dx * g, yh, r).astype(dy_ref.dtype)
        dg_ref[...] += jnp.sum(dx * yh, axis=0, keepdims=True)

    return pl.pallas_call(
        body, name=name, grid=(T // tr,),
        in_specs=[_row_spec(tr, Dm), _row_spec(tr, Dm), _vec_spec(Dm), _row_spec(tr, Dm)],
        out_specs=[_vec_spec(LANE), _row_spec(tr, Dm), _row_spec(tr, Dm), _vec_spec(Dm)],
        out_shape=[jax.ShapeDtypeStruct((1, LANE), F32), jax.ShapeDtypeStruct((T, Dm), F32),
                   jax.ShapeDtypeStruct((T, Dm), BF16), jax.ShapeDtypeStruct((1, Dm), F32)],
        compiler_params=_params("arbitrary"),
    )(x2, y, g_post, target)


def _norm_bwd2(dx_cur, dh, x_prev, g_pre, m_prev, g_post, name):
    T, Dm = x_prev.shape
    tr = min(ROWS, T)

    def body(dx_ref, dh_ref, x_ref, gn_ref, m_ref, gp_ref, dxo_ref, dm_ref, dgn_ref, dgp_ref):
        @pl.when(pl.program_id(0) == 0)
        def _():
            dgn_ref[...] = jnp.zeros_like(dgn_ref)
            dgp_ref[...] = jnp.zeros_like(dgp_ref)

        dh = dh_ref[...].astype(F32)
        xh, r = _rms(x_ref[...])
        dx = dx_ref[...] + _rms_bwd(dh * gn_ref[...], xh, r)
        dxo_ref[...] = dx
        dgn_ref[...] += jnp.sum(dh * xh, axis=0, keepdims=True)
        mh, rm = _rms(m_ref[...].astype(F32))
        dm_ref[...] = _rms_bwd(dx * gp_ref[...], mh, rm).astype(dm_ref.dtype)
        dgp_ref[...] += jnp.sum(dx * mh, axis=0, keepdims=True)

    return pl.pallas_call(
        body, name=name, grid=(T // tr,),
        in_specs=[_row_spec(tr, Dm), _row_spec(tr, Dm), _row_spec(tr, Dm), _vec_spec(Dm), _row_spec(tr, Dm), _vec_spec(Dm)],
        out_specs=[_row_spec(tr, Dm), _row_spec(tr, Dm), _vec_spec(Dm), _vec_spec(Dm)],
        out_shape=[jax.ShapeDtypeStruct((T, Dm), F32), jax.ShapeDtypeStruct((T, Dm), BF16),
                   jax.ShapeDtypeStruct((1, Dm), F32), jax.ShapeDtypeStruct((1, Dm), F32)],
        compiler_params=_params("arbitrary"),
    )(dx_cur, dh, x_prev, g_pre, m_prev, g_post)


def _norm_bwd1(dx_cur, dh, x_prev, g_pre, name):
    T, Dm = x_prev.shape
    tr = min(ROWS, T)

    def body(dx_ref, dh_ref, x_ref, gn_ref, dxo_ref, dgn_ref):
        @pl.when(pl.program_id(0) == 0)
        def _():
            dgn_ref[...] = jnp.zeros_like(dgn_ref)

        dh = dh_ref[...].astype(F32)
        xh, r = _rms(x_ref[...])
        dxo_ref[...] = dx_ref[...] + _rms_bwd(dh * gn_ref[...], xh, r)
        dgn_ref[...] += jnp.sum(dh * xh, axis=0, keepdims=True)

    return pl.pallas_call(
        body, name=name, grid=(T // tr,),
        in_specs=[_row_spec(tr, Dm), _row_spec(tr, Dm), _row_spec(tr, Dm), _vec_spec(Dm)],
        out_specs=[_row_spec(tr, Dm), _vec_spec(Dm)],
        out_shape=[jax.ShapeDtypeStruct((T, Dm), F32), jax.ShapeDtypeStruct((1, Dm), F32)],
        compiler_params=_params("arbitrary"),
    )(dx_cur, dh, x_prev, g_pre)


def _gain_bwd(x, dh_a, dh_b, name):
    T, Dm = x.shape

    def body(x_ref, a_ref, b_ref, dg_ref):
        xh, _ = _rms(x_ref[...])
        dg_ref[...] = jnp.sum((a_ref[...] + b_ref[...]) * xh, axis=0, keepdims=True)

    return pl.pallas_call(
        body, name=name, grid=(1,), in_specs=[_row_spec(T, Dm)] * 3, out_specs=_vec_spec(Dm),
        out_shape=jax.ShapeDtypeStruct((1, Dm), F32), compiler_params=_params("arbitrary"),
    )(x, dh_a, dh_b)


ATTN_GROUP = ATTN_Q_HEADS // ATTN_KV_HEADS
ASSEMBLE_ROWS = 1024


def _swa_mask(n):
    rows = ATTN_GROUP * ATTN_BLOCK
    row = lax.broadcasted_iota(jnp.int32, (rows, 2 * ATTN_BLOCK), 0) & (ATTN_BLOCK - 1)
    col = lax.broadcasted_iota(jnp.int32, (rows, 2 * ATTN_BLOCK), 1)
    diff = row + ATTN_BLOCK - col
    return (diff >= 0) & (diff < ATTN_BLOCK) & ((col >= ATTN_BLOCK) | (n > 0))


def _swa_rows(ref, hk, dtype):
    hd = ATTN_HEAD_DIM
    return jnp.concatenate(
        [ref[:, hd * (hk * ATTN_GROUP + g):hd * (hk * ATTN_GROUP + g + 1)].astype(dtype) for g in range(ATTN_GROUP)],
        axis=0)


def _swa_per_row(vals):
    seg = lax.broadcasted_iota(jnp.int32, (ATTN_GROUP * ATTN_BLOCK, 1), 0) // ATTN_BLOCK
    col = jnp.zeros((ATTN_GROUP * ATTN_BLOCK, 1), F32)
    for g, val in enumerate(vals):
        col = jnp.where(seg == g, val, col)
    return col


def _swa_specs():
    blk = ATTN_BLOCK
    prev = lambda n: jnp.maximum(n - 1, 0)
    return [
        pl.BlockSpec(memory_space=pltpu.SMEM),
        pl.BlockSpec((blk, 512), lambda n: (n, 0)),
        pl.BlockSpec((blk, 128), lambda n: (prev(n), 4)),
        pl.BlockSpec((blk, 128), lambda n: (n, 4)),
        pl.BlockSpec((blk, 128), lambda n: (prev(n), 5)),
        pl.BlockSpec((blk, 128), lambda n: (n, 5)),
    ]


def _swa_fwd(z, sinks, name, exchange=None):
    T = z.shape[0]
    blk, hd = ATTN_BLOCK, ATTN_HEAD_DIM
    scale = hd ** -0.5

    def body(sink_ref, q_ref, kp_ref, kc_ref, vp_ref, vc_ref, o_ref, lse_ref):
        allowed = _swa_mask(pl.program_id(0))
        hks = range(ATTN_KV_HEADS)
        kss = [slice(hd * hk, hd * hk + hd) for hk in hks]
        k = [_bf(jnp.concatenate([kp_ref[:, ks], kc_ref[:, ks]], axis=0)) for ks in kss]
        v = [_bf(jnp.concatenate([vp_ref[:, ks], vc_ref[:, ks]], axis=0)) for ks in kss]
        s = [jnp.where(allowed, _dot(_swa_rows(q_ref, hk, BF16), k[hk], NT) * scale, -1e30) for hk in hks]
        sink = [_swa_per_row([sink_ref[0, hk * ATTN_GROUP + g] for g in range(ATTN_GROUP)]) for hk in hks]
        m = [jnp.maximum(jnp.max(s[hk], axis=-1, keepdims=True), sink[hk]) for hk in hks]
        p = [jnp.exp(s[hk] - m[hk]) for hk in hks]
        l = [jnp.sum(p[hk], axis=-1, keepdims=True) + jnp.exp(sink[hk] - m[hk]) for hk in hks]
        o = [_dot(_bf(p[hk] / l[hk]), v[hk]).astype(o_ref.dtype) for hk in hks]
        for hk in hks:
            lse = m[hk] + jnp.log(l[hk])
            for g in range(ATTN_GROUP):
                h = hk * ATTN_GROUP + g
                o_ref[:, hd * h:hd * (h + 1)] = o[hk][blk * g:blk * (g + 1)]
                lse_ref[:, h:h + 1] = lse[blk * g:blk * (g + 1)]

    return _hosted_call(
        body, name=name, grid=(T // blk,), in_specs=_swa_specs(),
        out_specs=[pl.BlockSpec((blk, 512), lambda n: (n, 0)), pl.BlockSpec((blk, ATTN_Q_HEADS), lambda n: (n, 0))],
        out_shape=[jax.ShapeDtypeStruct((T, 512), BF16), jax.ShapeDtypeStruct((T, ATTN_Q_HEADS), F32)],
        scratch=[], args=(sinks, z, z, z, z, z), semantics=("parallel",), exchange=exchange)


def _swa_bwd(z, sinks, dcat, lse, name):
    T = z.shape[0]
    blk, hd = ATTN_BLOCK, ATTN_HEAD_DIM
    scale = hd ** -0.5
    group = ATTN_Q_HEADS // ATTN_KV_HEADS

    def body(sink_ref, q_ref, kp_ref, kc_ref, vp_ref, vc_ref, do_ref, lse_ref,
             dq_ref, dka_ref, dkb_ref, dva_ref, dvb_ref, dsink_ref):
        @pl.when(pl.program_id(0) == 0)
        def _():
            dsink_ref[...] = jnp.zeros_like(dsink_ref)

        allowed = _swa_mask(pl.program_id(0))
        lane = lax.broadcasted_iota(jnp.int32, (1, ATTN_Q_HEADS), 1)
        dsink = jnp.zeros((1, ATTN_Q_HEADS), F32)
        hks = range(ATTN_KV_HEADS)
        kss = [slice(hd * hk, hd * hk + hd) for hk in hks]
        k = [_bf(jnp.concatenate([kp_ref[:, ks], kc_ref[:, ks]], axis=0)) for ks in kss]
        v = [_bf(jnp.concatenate([vp_ref[:, ks], vc_ref[:, ks]], axis=0)) for ks in kss]
        qs = [_swa_rows(q_ref, hk, BF16) for hk in hks]
        dos = [_swa_rows(do_ref, hk, BF16) for hk in hks]
        lse = [jnp.concatenate([lse_ref[:, hk * group + g:hk * group + g + 1] for g in range(group)], axis=0)
               for hk in hks]
        s = [_dot(qs[hk], k[hk], NT) * scale for hk in hks]
        dp = [_dot(dos[hk], v[hk], NT) for hk in hks]
        p = [jnp.where(allowed, jnp.exp(jnp.where(allowed, s[hk], -1e30) - lse[hk]), 0.0) for hk in hks]
        delta = [jnp.sum(p[hk] * dp[hk], axis=-1, keepdims=True) for hk in hks]
        ds = [_bf(p[hk] * (dp[hk] - delta[hk]) * scale) for hk in hks]
        dq = [_dot(ds[hk], k[hk]).astype(dq_ref.dtype) for hk in hks]
        dk = [_dot(ds[hk], qs[hk], TN) for hk in hks]
        dv = [_dot(_bf(p[hk]), dos[hk], TN) for hk in hks]
        for hk in hks:
            sink = _swa_per_row([sink_ref[0, hk * group + g] for g in range(group)])
            sink_part = jnp.exp(sink - lse[hk]) * delta[hk]
            for g in range(group):
                h = hk * group + g
                dq_ref[:, hd * h:hd * (h + 1)] = dq[hk][blk * g:blk * (g + 1)]
                dsink = dsink + jnp.where(lane == h, -jnp.sum(sink_part[blk * g:blk * (g + 1)]), 0.0)
            dkb_ref[:, kss[hk]] = dk[hk][:blk]
            dka_ref[:, kss[hk]] = dk[hk][blk:]
            dvb_ref[:, kss[hk]] = dv[hk][:blk]
            dva_ref[:, kss[hk]] = dv[hk][blk:]
        dsink_ref[...] += dsink

    kv_out = pl.BlockSpec((blk, 128), lambda n: (n, 0))
    return pl.pallas_call(
        body, name=name, grid=(T // blk,),
        in_specs=_swa_specs() + [pl.BlockSpec((blk, 512), lambda n: (n, 0)),
                                 pl.BlockSpec((blk, ATTN_Q_HEADS), lambda n: (n, 0))],
        out_specs=[pl.BlockSpec((blk, 512), lambda n: (n, 0)), kv_out, kv_out, kv_out, kv_out,
                   pl.BlockSpec((1, ATTN_Q_HEADS), lambda n: (0, 0))],
        out_shape=[jax.ShapeDtypeStruct((T, 512), BF16)] + [jax.ShapeDtypeStruct((T, 128), F32)] * 4
        + [jax.ShapeDtypeStruct((1, ATTN_Q_HEADS), F32)],
        compiler_params=_params("arbitrary"),
    )(sinks, z, z, z, z, z, dcat, lse)


def _assemble_dz(dq_a, dka, dkb, dva, dvb, dqr, dfr, dir_, dgr, name):
    T = dq_a.shape[0]
    blk = ATTN_BLOCK
    rows = min(ASSEMBLE_ROWS, T)
    nb, per = T // rows, rows // blk

    def body(dq_ref, dka_ref, dkb_ref, dkn_ref, dva_ref, dvb_ref, dvn_ref, dqr_ref, dfr_ref, dir_ref, dgr_ref, o_ref):
        has_next = pl.program_id(0) < nb - 1

        def with_next(a_ref, b_ref, n_ref):
            after = jnp.where(has_next, n_ref[...], 0.0)
            shifted = after if per == 1 else jnp.concatenate([b_ref[blk:, :], after], axis=0)
            return (a_ref[...] + shifted).astype(o_ref.dtype)

        o_ref[:, 0:512] = dq_ref[...]
        o_ref[:, 512:640] = with_next(dka_ref, dkb_ref, dkn_ref)
        o_ref[:, 640:768] = with_next(dva_ref, dvb_ref, dvn_ref)
        o_ref[:, 768:1280] = dqr_ref[...]
        o_ref[:, 1280:1792] = dfr_ref[...]
        o_ref[:, 1792:2304] = dir_ref[...]
        o_ref[:, 2304:2816] = dgr_ref[...]

    cur = lambda w: pl.BlockSpec((rows, w), lambda n: (n, 0))
    nxt = pl.BlockSpec((blk, 128), lambda n: (jnp.minimum((n + 1) * per, T // blk - 1), 0))
    return pl.pallas_call(
        body, name=name, grid=(nb,),
        in_specs=[cur(512), cur(128), cur(128), nxt, cur(128), cur(128), nxt, cur(512), cur(512), cur(512), cur(512)],
        out_specs=pl.BlockSpec((rows, 2816), lambda n: (n, 0)),
        out_shape=jax.ShapeDtypeStruct((T, 2816), BF16), compiler_params=_params("parallel"),
    )(dq_a, dka, dkb, dkb, dva, dvb, dvb, dqr, dfr, dir_, dgr)


HGRN_ROWS = 512


def _hgrn_consts():
    c = HGRN_CHUNK
    r = lax.broadcasted_iota(jnp.int32, (c, c), 0)
    s = lax.broadcasted_iota(jnp.int32, (c, c), 1)
    rcol = lax.broadcasted_iota(jnp.int32, (c, 1), 0)
    same_block, upper = [], []
    for m in HGRN_LEVELS:
        same_block.append((r & ~(2 * m - 1)) == (s & ~(2 * m - 1)))
        upper.append((rcol & (2 * m - 1)) >= m)
    cum_mat = jnp.where(s <= r, 1.0, 0.0).astype(BF16)
    rev_mat = jnp.where(s >= r, 1.0, 0.0).astype(BF16)
    return cum_mat, rev_mat, r == s, same_block, upper, rcol & 3, s == r - 1


def _hgrn_level_decay(g, b, m, pos4):
    c = HGRN_CHUNK
    if m == 1:
        return jnp.exp(jnp.where((pos4 & 1) == 1, g, 0.0))
    if m == 2:
        after, before = pltpu.roll(g, c - 1, 0), pltpu.roll(g, 1, 0)
        return jnp.exp(jnp.where(pos4 == 0, after, jnp.where(pos4 == 1, 0.0, jnp.where(pos4 == 2, g, g + before))))
    b3 = b.reshape(c // (2 * m), 2 * m, HGRN_DIM)
    bref = jnp.broadcast_to(b3[:, m - 1:m, :], b3.shape).reshape(c, HGRN_DIM)
    return jnp.exp(-jnp.abs(b - bref))


def _split3(x):
    hi = _bf(x)
    r1 = x - hi.astype(F32)
    mid = _bf(r1)
    lo = _bf(r1 - mid.astype(F32))
    return jnp.concatenate([hi, mid, lo], axis=1)


def _dot_hilo(a, b):
    r, c = a.shape[0], b.shape[1]
    a_hi, b_hi = _bf(a), _bf(b)
    a2 = jnp.concatenate([a_hi, _bf(a - a_hi.astype(F32))], axis=0)
    b2 = jnp.concatenate([b_hi, _bf(b - b_hi.astype(F32))], axis=1)
    y = _dot(a2, b2)
    return y[:r, :c] + y[:r, c:] + y[r:, :c]


def _fold3(y):
    w = y.shape[1] // 3
    return y[:, :w] + y[:, w:2 * w] + y[:, 2 * w:]


def _hgrn_gates(qr, fr, lb):
    sq = _sigmoid(qr)
    q = qr * sq * (HGRN_DIM ** -0.5)
    sf = _sigmoid(fr)
    f = lb + (1.0 - lb) * sf
    k = (1.0 - lb) * _sigmoid(-fr)
    return q, sq, sf, f, k, jnp.log(f)


def _hgrn_intra(q, k, g, b, consts, scores=True):
    _, _, eye, same_block, upper, pos4, below = consts
    heads = range(len(q))
    a = None
    if scores:
        a = [jnp.where(eye, jnp.sum(q[hh] * k[hh], axis=1, keepdims=True), 0.0) for hh in heads]
    saved = [[] for _ in heads]
    for i, m in enumerate(HGRN_LEVELS):
        up = upper[i]
        e = [_hgrn_level_decay(g[hh], b[hh], m, pos4) for hh in heads]
        qt = [jnp.where(up, q[hh] * e[hh], 0.0) for hh in heads]
        kt = [jnp.where(up, 0.0, k[hh] * e[hh]) for hh in heads]
        for hh in heads:
            saved[hh].append((e[hh], qt[hh], kt[hh]))
        if not scores:
            continue
        if m == 1:
            for hh in heads:
                pair = jnp.sum(qt[hh] * pltpu.roll(kt[hh], 1, 0), axis=1, keepdims=True)
                a[hh] = a[hh] + jnp.where(below, pair, 0.0)
            continue
        p = [_dot(_bf(qt[hh]), _bf(kt[hh]), NT) for hh in heads]
        for hh in heads:
            a[hh] = a[hh] + jnp.where(same_block[i], p[hh], 0.0)
    return a, saved


def _hgrn_specs(tb, nb, rev):
    tmap = (lambda t: nb - 1 - t) if rev else (lambda t: t)
    assert HGRN_PAIR == HGRN_HEADS
    return [pl.BlockSpec((tb, 2816), lambda h, t: (tmap(t), 0)),
            pl.BlockSpec((1, HGRN_PAIR * HGRN_DIM), lambda h, t: (0, h)),
            pl.BlockSpec((1, HGRN_DIM), lambda h, t: (0, 0))]


def _hgrn_z(z_ref, sl, base, head):
    return z_ref[sl, base + HGRN_DIM * head:base + HGRN_DIM * (head + 1)].astype(F32)


def _hgrn_fwd(z, lb, onw, name, exchange=None):
    T = z.shape[0]
    tb = min(HGRN_ROWS, T)
    nb, c, nc = T // tb, HGRN_CHUNK, min(HGRN_ROWS, T) // HGRN_CHUNK

    def body(z_ref, lb_ref, onw_ref, rec_ref, o_ref, st_ref, a_ref, state):
        @pl.when(pl.program_id(1) == 0)
        def _():
            state[...] = jnp.zeros_like(state)

        consts = _hgrn_consts()
        lbv = lb_ref[...]
        onwv = onw_ref[...]

        def chunk(ci, carry):
            sl = pl.ds(pl.multiple_of(ci * c, c), c)
            heads = range(HGRN_PAIR)
            lss = [slice(HGRN_DIM * hh, HGRN_DIM * (hh + 1)) for hh in heads]
            gates = [_hgrn_gates(_hgrn_z(z_ref, sl, Z_Q, hh), _hgrn_z(z_ref, sl, Z_F, hh), lbv[:, lss[hh]])
                     for hh in heads]
            q, k, g = [t[0] for t in gates], [t[4] for t in gates], [t[5] for t in gates]
            v = [_bf(_hgrn_z(z_ref, sl, Z_I, hh)) for hh in heads]
            b = [_fold3(_dot(consts[0], _split3(g[hh]))) for hh in heads]
            a, _ = _hgrn_intra(q, k, g, b, consts)
            st = [state[hh] for hh in heads]
            for hh in heads:
                st_ref[hh, ci] = st[hh]
            bl = [b[hh][c - 1:c, :] for hh in heads]
            o_state = [_dot(_bf(q[hh] * jnp.exp(b[hh])), _bf(st[hh]), NT) for hh in heads]
            kv = [_dot(v[hh], _bf(k[hh] * jnp.exp(bl[hh] - b[hh])), TN) for hh in heads]
            a = [_bf(a[hh]) for hh in heads]
            o = [_dot(a[hh], v[hh]) + o_state[hh] for hh in heads]
            for hh in heads:
                a_ref[sl, c * hh:c * (hh + 1)] = a[hh]
                state[hh] = st[hh] * jnp.exp(bl[hh]) + kv[hh]
                o_ref[sl, lss[hh]] = o[hh]
                oh, _ = _rms(o[hh])
                gr = _hgrn_z(z_ref, sl, Z_G, hh)
                rec_ref[sl, lss[hh]] = (oh * onwv * (gr * _sigmoid(gr))).astype(rec_ref.dtype)
            return carry

        lax.fori_loop(0, nc, chunk, 0)

    in_specs = _hgrn_specs(tb, nb, False)
    out_blk = pl.BlockSpec((tb, HGRN_PAIR * HGRN_DIM), lambda h, t: (t, h))
    return _hosted_call(
        body, name=name, grid=(HGRN_HEADS // HGRN_PAIR, nb), in_specs=in_specs,
        out_specs=[out_blk, out_blk, pl.BlockSpec((HGRN_PAIR, nc, HGRN_DIM, HGRN_DIM), lambda h, t: (h, t, 0, 0)),
                   pl.BlockSpec((tb, HGRN_PAIR * c), lambda h, t: (t, h))],
        out_shape=[jax.ShapeDtypeStruct((T, 512), BF16), jax.ShapeDtypeStruct((T, 512), F32),
                   jax.ShapeDtypeStruct((HGRN_HEADS, T // c, HGRN_DIM, HGRN_DIM), F32),
                   jax.ShapeDtypeStruct((T, HGRN_HEADS * c), BF16)],
        scratch=[pltpu.VMEM((HGRN_PAIR, HGRN_DIM, HGRN_DIM), F32)], args=(z, lb, onw),
        semantics=("parallel", "arbitrary"), exchange=exchange)


def _hgrn_bwd(z, lb, onw, o, states, scores, dcat, name, exchange=None):
    T = z.shape[0]
    tb = min(HGRN_ROWS, T)
    nb, c, nc = T // tb, HGRN_CHUNK, min(HGRN_ROWS, T) // HGRN_CHUNK

    def body(z_ref, lb_ref, onw_ref, o_ref, st_ref, drec_ref, a_ref,
             dqr_ref, dfr_ref, dir_ref, dgr_ref, dlb_ref, donw_ref, dstate):
        @pl.when(pl.program_id(1) == 0)
        def _():
            dstate[...] = jnp.zeros_like(dstate)
            dlb_ref[...] = jnp.zeros_like(dlb_ref)

        @pl.when((pl.program_id(0) == 0) & (pl.program_id(1) == 0))
        def _():
            donw_ref[...] = jnp.zeros_like(donw_ref)

        consts = _hgrn_consts()
        rev_mat, eye, same_block, upper = consts[1:5]
        below = consts[6]
        lbv = lb_ref[...]
        onwv = onw_ref[...]
        last = lax.broadcasted_iota(jnp.int32, (c, 1), 0) == c - 1

        def chunk(i, carry):
            ci = nc - 1 - i
            sl = pl.ds(pl.multiple_of(ci * c, c), c)
            hs = range(HGRN_PAIR)
            lss = [slice(HGRN_DIM * hh, HGRN_DIM * (hh + 1)) for hh in hs]
            qr = [_hgrn_z(z_ref, sl, Z_Q, hh) for hh in hs]
            gates = [_hgrn_gates(qr[hh], _hgrn_z(z_ref, sl, Z_F, hh), lbv[:, lss[hh]]) for hh in hs]
            q, sq, sf, f, k, g = ([t[j] for t in gates] for j in range(6))
            v = [_bf(_hgrn_z(z_ref, sl, Z_I, hh)) for hh in hs]
            b = [_fold3(_dot(consts[0], _split3(g[hh]))) for hh in hs]
            _, saved = _hgrn_intra(q, k, g, b, consts, scores=False)
            a = [a_ref[sl, c * hh:c * (hh + 1)] for hh in hs]
            st = [st_ref[hh, ci] for hh in hs]
            dst = [dstate[hh] for hh in hs]

            gr = [_hgrn_z(z_ref, sl, Z_G, hh) for hh in hs]
            sg = [_sigmoid(gr[hh]) for hh in hs]
            norm = [_rms(o_ref[sl, ls]) for ls in lss]
            oh, r = [t[0] for t in norm], [t[1] for t in norm]
            drec = [drec_ref[sl, ls].astype(F32) for ls in lss]
            don = [drec[hh] * (gr[hh] * sg[hh]) for hh in hs]
            do = [_bf(_rms_bwd(don[hh] * onwv, oh[hh], r[hh])) for hh in hs]
            donw = jnp.sum(don[0] * oh[0], axis=0, keepdims=True)
            for hh in hs:
                dgr_ref[sl, lss[hh]] = (drec[hh] * oh[hh] * onwv
                                        * (sg[hh] * (1.0 + gr[hh] * (1.0 - sg[hh])))).astype(dgr_ref.dtype)
                if hh:
                    donw = donw + jnp.sum(don[hh] * oh[hh], axis=0, keepdims=True)
            donw_ref[...] += donw

            eb = [jnp.exp(b[hh]) for hh in hs]
            bl = [b[hh][c - 1:c, :] for hh in hs]
            ebl = [jnp.exp(bl[hh]) for hh in hs]
            ekb = [jnp.exp(bl[hh] - b[hh]) for hh in hs]
            qe = [q[hh] * eb[hh] for hh in hs]
            ke = [k[hh] * ekb[hh] for hh in hs]
            da = [_dot(do[hh], v[hh], NT) for hh in hs]
            dat = [_dot(v[hh], do[hh], NT) for hh in hs]
            dqe = [_dot(do[hh], _bf(st[hh])) for hh in hs]
            dke = [_dot(v[hh], _bf(dst[hh])) for hh in hs]
            dv_a = [_dot(a[hh], do[hh], TN) for hh in hs]
            dv_s = [_dot(_bf(ke[hh]), _bf(dst[hh]), NT) for hh in hs]
            dst_in = [_dot(do[hh], _bf(qe[hh]), TN) for hh in hs]
            dad = [jnp.sum(jnp.where(eye, da[hh], 0.0), axis=1, keepdims=True) for hh in hs]
            dq = [dqe[hh] * eb[hh] + dad[hh] * k[hh] for hh in hs]
            dk = [dke[hh] * ekb[hh] + dad[hh] * q[hh] for hh in hs]
            db_last = [jnp.sum(dke[hh] * ke[hh], axis=0, keepdims=True)
                       + jnp.sum(dst[hh] * st[hh], axis=0, keepdims=True) * ebl[hh] for hh in hs]
            for hh in hs:
                dstate[hh] = dst[hh] * ebl[hh] + dst_in[hh]
                dir_ref[sl, lss[hh]] = (dv_a[hh] + dv_s[hh]).astype(dir_ref.dtype)
            for lvl, m in enumerate(HGRN_LEVELS):
                if m == 1:
                    pair = [jnp.sum(jnp.where(below, da[hh], 0.0), axis=1, keepdims=True) for hh in hs]
                    xq = [pair[hh] * pltpu.roll(saved[hh][lvl][2], 1, 0) for hh in hs]
                    xk = [pltpu.roll(pair[hh] * saved[hh][lvl][1], c - 1, 0) for hh in hs]
                else:
                    xq = [_dot_hilo(jnp.where(same_block[lvl], da[hh], 0.0), saved[hh][lvl][2]) for hh in hs]
                    xk = [_dot_hilo(jnp.where(same_block[lvl], dat[hh], 0.0), saved[hh][lvl][1]) for hh in hs]
                for hh in hs:
                    e = saved[hh][lvl][0]
                    dq[hh] = dq[hh] + jnp.where(upper[lvl], xq[hh] * e, 0.0)
                    dk[hh] = dk[hh] + jnp.where(upper[lvl], 0.0, xk[hh] * e)
            db = [q[hh] * dq[hh] - k[hh] * dk[hh] + jnp.where(last, db_last[hh], 0.0) for hh in hs]
            dg = [_fold3(_dot(rev_mat, _split3(db[hh]))) for hh in hs]

            for hh in hs:
                ls = lss[hh]
                dqr_ref[sl, ls] = (dq[hh] * (HGRN_DIM ** -0.5)
                                   * (sq[hh] * (1.0 + qr[hh] * (1.0 - sq[hh])))).astype(dqr_ref.dtype)
                dfk = dg[hh] / f[hh] - dk[hh]
                dfr_ref[sl, ls] = ((1.0 - lbv[:, ls]) * sf[hh] * (1.0 - sf[hh]) * dfk).astype(dfr_ref.dtype)
                dlb_ref[:, ls] += jnp.sum((1.0 - sf[hh]) * dfk, axis=0, keepdims=True)
            return carry

        lax.fori_loop(0, nc, chunk, 0)

    in_specs = _hgrn_specs(tb, nb, True)
    rblk = pl.BlockSpec((tb, HGRN_PAIR * HGRN_DIM), lambda h, t: (nb - 1 - t, h))
    in_specs = in_specs + [
        rblk,
        pl.BlockSpec((HGRN_PAIR, nc, HGRN_DIM, HGRN_DIM), lambda h, t: (h, nb - 1 - t, 0, 0)),
        pl.BlockSpec((tb, HGRN_PAIR * HGRN_DIM), lambda h, t: (nb - 1 - t, 4 // HGRN_PAIR + h)),
        pl.BlockSpec((tb, HGRN_PAIR * c), lambda h, t: (nb - 1 - t, h)),
    ]
    return _hosted_call(
        body, name=name, grid=(HGRN_HEADS // HGRN_PAIR, nb), in_specs=in_specs,
        out_specs=[rblk, rblk, rblk, rblk, pl.BlockSpec((1, HGRN_PAIR * HGRN_DIM), lambda h, t: (0, h)),
                   pl.BlockSpec((1, HGRN_DIM), lambda h, t: (0, 0))],
        out_shape=[jax.ShapeDtypeStruct((T, 512), BF16)] * 4
        + [jax.ShapeDtypeStruct((1, 512), F32), jax.ShapeDtypeStruct((1, HGRN_DIM), F32)],
        scratch=[pltpu.VMEM((HGRN_PAIR, HGRN_DIM, HGRN_DIM), F32)], args=(z, lb, onw, o, states, dcat, scores),
        semantics=("arbitrary", "arbitrary"), exchange=exchange)


def _lower_bound(logits, name):
    def body(l_ref, lb_ref):
        l0, l1 = l_ref[0:1, :], l_ref[1:2, :]
        m = jnp.maximum(l0, l1)
        e0, e1 = jnp.exp(l0 - m), jnp.exp(l1 - m)
        lb_ref[...] = e0 / (e0 + e1)

    return pl.pallas_call(
        body, name=name, out_shape=jax.ShapeDtypeStruct((1, logits.shape[1]), F32),
    )(logits)


def _lower_bound_bwd(lb, dlb, name):
    def body(lb_ref, dlb_ref, dl_ref):
        p = lb_ref[...]
        d0 = dlb_ref[...] * p * (1.0 - p)
        dl_ref[0:1, :] = d0
        dl_ref[1:2, :] = -d0

    return pl.pallas_call(
        body, name=name, out_shape=jax.ShapeDtypeStruct((2, lb.shape[1]), F32),
    )(lb, dlb)


CA_ROWS = 512


def _ca_fwd(q, k, v, name):
    T, W = q.shape
    M = k.shape[0]
    tq = min(CA_ROWS, T)
    scale = CA_HEAD_DIM ** -0.5

    def body(q_ref, k_ref, v_ref, o_ref):
        for h in range(CA_HEADS):
            hs = slice(CA_HEAD_DIM * h, CA_HEAD_DIM * (h + 1))
            s = _dot(q_ref[:, hs], k_ref[:, hs], NT) * scale
            p = jnp.exp(s - jnp.max(s, axis=-1, keepdims=True))
            p = p / jnp.sum(p, axis=-1, keepdims=True)
            o_ref[:, hs] = _dot(_bf(p), v_ref[:, hs]).astype(o_ref.dtype)

    full = pl.BlockSpec((M, W), lambda i: (0, 0))
    return pl.pallas_call(
        body, name=name, grid=(T // tq,), in_specs=[_row_spec(tq, W), full, full], out_specs=_row_spec(tq, W),
        out_shape=jax.ShapeDtypeStruct((T, W), BF16), compiler_params=_params("parallel"),
    )(q, k, v)


def _ca_bwd(q, k, v, do, name):
    T, W = q.shape
    M = k.shape[0]
    tq = min(CA_ROWS, T)
    scale = CA_HEAD_DIM ** -0.5

    def body(q_ref, k_ref, v_ref, do_ref, dq_ref, dk_ref, dv_ref):
        @pl.when(pl.program_id(0) == 0)
        def _():
            dk_ref[...] = jnp.zeros_like(dk_ref)
            dv_ref[...] = jnp.zeros_like(dv_ref)

        for h in range(CA_HEADS):
            hs = slice(CA_HEAD_DIM * h, CA_HEAD_DIM * (h + 1))
            qh, kh, vh, doh = q_ref[:, hs], k_ref[:, hs], v_ref[:, hs], do_ref[:, hs]
            s = _dot(qh, kh, NT) * scale
            p = jnp.exp(s - jnp.max(s, axis=-1, keepdims=True))
            p = p / jnp.sum(p, axis=-1, keepdims=True)
            dp = _dot(doh, vh, NT)
            ds = _bf(p * (dp - jnp.sum(p * dp, axis=-1, keepdims=True)) * scale)
            dq_ref[:, hs] = _dot(ds, kh).astype(dq_ref.dtype)
            dk_ref[:, hs] += _dot(ds, qh, TN)
            dv_ref[:, hs] += _dot(_bf(p), doh, TN)

    full = pl.BlockSpec((M, W), lambda i: (0, 0))
    return pl.pallas_call(
        body, name=name, grid=(T // tq,), in_specs=[_row_spec(tq, W), full, full, _row_spec(tq, W)],
        out_specs=[_row_spec(tq, W), full, full],
        out_shape=[jax.ShapeDtypeStruct((T, W), BF16), jax.ShapeDtypeStruct((M, W), F32), jax.ShapeDtypeStruct((M, W), F32)],
        compiler_params=_params("arbitrary"),
    )(q, k, v, do)


FFN_ROWS = 256
FFN_COLS = 1408
GELU_C0 = 0.7978845608028654
GELU_C1 = 0.044715


def _gelu(x):
    t = jnp.tanh(GELU_C0 * (x + GELU_C1 * x * x * x))
    return 0.5 * x * (1.0 + t), t


def _gelu_grad(x, t):
    return 0.5 * (1.0 + t) + 0.5 * x * (1.0 - t * t) * GELU_C0 * (1.0 + 3.0 * GELU_C1 * x * x)


def _shift_down(cur, halo, first, tb):
    row = lax.broadcasted_iota(jnp.int32, (tb, 1), 0)
    h6 = jnp.where(first, 0.0, halo[6:7])
    h7 = jnp.where(first, 0.0, halo[7:8])
    u1 = jnp.where(row == 0, h7, pltpu.roll(cur, 1, 0))
    u2 = jnp.where(row == 0, h6, jnp.where(row == 1, h7, pltpu.roll(cur, 2, 0)))
    return u1, u2


def _conv(u_ref, halo_ref, w_ref, b_ref, half, first, tb):
    cur = u_ref[half]
    u1, u2 = _shift_down(cur, halo_ref[half], first, tb)
    w = w_ref[...]
    return w[0:1] * u2 + w[1:2] * u1 + w[2:3] * cur + b_ref[...], cur, u1, u2


def _ffn_specs(tb, tc, rows_first):
    nj = D_FF // tc
    rc = (lambda a, b: (a, b)) if rows_first else (lambda a, b: (b, a))
    def at(f):
        return lambda a, b: f(*rc(a, b))
    blk = pl.BlockSpec((2, tb, tc), at(lambda t, j: (0, t, j)))
    halo = pl.BlockSpec((2, 8, tc), at(lambda t, j: (0, jnp.maximum(t * (tb // 8) - 1, 0), j)))
    wg = pl.BlockSpec((3, tc), at(lambda t, j: (0, j)))
    wv = pl.BlockSpec((3, tc), at(lambda t, j: (0, j + nj)))
    bg = pl.BlockSpec((1, tc), at(lambda t, j: (0, j)))
    bv = pl.BlockSpec((1, tc), at(lambda t, j: (0, j + nj)))
    flat = pl.BlockSpec((tb, tc), at(lambda t, j: (t, j)))
    return blk, halo, wg, wv, bg, bv, flat


def _glu_fwd(u, cw, cb, name):
    T = u.shape[1]
    tb, tc = min(FFN_ROWS, T), FFN_COLS

    def body(u_ref, halo_ref, wg_ref, wv_ref, bg_ref, bv_ref, a_ref):
        first = pl.program_id(0) == 0
        cg = _conv(u_ref, halo_ref, wg_ref, bg_ref, 0, first, tb)[0]
        cv = _conv(u_ref, halo_ref, wv_ref, bv_ref, 1, first, tb)[0]
        a_ref[...] = (_gelu(cg)[0] * cv).astype(a_ref.dtype)

    blk, halo, wg, wv, bg, bv, flat = _ffn_specs(tb, tc, True)
    return pl.pallas_call(
        body, name=name, grid=(T // tb, D_FF // tc), in_specs=[blk, halo, wg, wv, bg, bv], out_specs=flat,
        out_shape=jax.ShapeDtypeStruct((T, D_FF), BF16), compiler_params=_params("parallel", "parallel"),
    )(u, u, cw, cw, cb, cb)


def _glu_bwd(u, cw, cb, da, name, exchange=None):
    T = u.shape[1]
    tb, tc = min(FFN_ROWS, T), FFN_COLS

    def body(u_ref, halo_ref, wg_ref, wv_ref, bg_ref, bv_ref, da_ref, dc_ref, db_ref, dw_ref):
        first = pl.program_id(1) == 0

        @pl.when(first)
        def _():
            db_ref[...] = jnp.zeros_like(db_ref)
            dw_ref[...] = jnp.zeros_like(dw_ref)

        cg, ug, ug1, ug2 = _conv(u_ref, halo_ref, wg_ref, bg_ref, 0, first, tb)
        cv, uv, uv1, uv2 = _conv(u_ref, halo_ref, wv_ref, bv_ref, 1, first, tb)
        da = da_ref[...]
        gl, t = _gelu(cg)
        dcg = da * cv * _gelu_grad(cg, t)
        dcv = da * gl
        dc_ref[0] = dcg
        dc_ref[1] = dcv
        for half, dc, taps in ((0, dcg, (ug2, ug1, ug)), (1, dcv, (uv2, uv1, uv))):
            db_ref[half] += jnp.sum(dc, axis=0, keepdims=True)
            for tap in range(3):
                dw_ref[half, tap:tap + 1, :] += jnp.sum(dc * taps[tap], axis=0, keepdims=True)

    blk, halo, wg, wv, bg, bv, flat = _ffn_specs(tb, tc, False)
    return _hosted_call(
        body, name=name, grid=(D_FF // tc, T // tb), in_specs=[blk, halo, wg, wv, bg, bv, flat],
        out_specs=[blk, pl.BlockSpec((2, 1, tc), lambda j, t: (0, 0, j)), pl.BlockSpec((2, 3, tc), lambda j, t: (0, 0, j))],
        out_shape=[jax.ShapeDtypeStruct((2, T, D_FF), F32), jax.ShapeDtypeStruct((2, 1, D_FF), F32),
                   jax.ShapeDtypeStruct((2, 3, D_FF), F32)],
        scratch=[], args=(u, u, cw, cw, cb, cb, da), semantics=("parallel", "arbitrary"), exchange=exchange)


def _conv_bwd(dc, cw, name):
    T = dc.shape[1]
    tb, tc = min(FFN_ROWS, T), FFN_COLS
    nt, nj = T // tb, D_FF // tc

    def body(dc_ref, halo_ref, wg_ref, wv_ref, du_ref):
        last = pl.program_id(0) == nt - 1
        row = lax.broadcasted_iota(jnp.int32, (tb, 1), 0)
        for half, w_ref in ((0, wg_ref), (1, wv_ref)):
            cur = dc_ref[half]
            halo = halo_ref[half]
            h0 = jnp.where(last, 0.0, halo[0:1])
            h1 = jnp.where(last, 0.0, halo[1:2])
            d1 = jnp.where(row == tb - 1, h0, pltpu.roll(cur, tb - 1, 0))
            d2 = jnp.where(row == tb - 1, h1, jnp.where(row == tb - 2, h0, pltpu.roll(cur, tb - 2, 0)))
            w = w_ref[...]
            du_ref[half] = (w[2:3] * cur + w[1:2] * d1 + w[0:1] * d2).astype(du_ref.dtype)

    blk = pl.BlockSpec((2, tb, tc), lambda t, j: (0, t, j))
    halo = pl.BlockSpec((2, 8, tc), lambda t, j: (0, jnp.minimum((t + 1) * (tb // 8), T // 8 - 1), j))
    wg = pl.BlockSpec((3, tc), lambda t, j: (0, j))
    wv = pl.BlockSpec((3, tc), lambda t, j: (0, j + nj))
    return pl.pallas_call(
        body, name=name, grid=(nt, nj), in_specs=[blk, halo, wg, wv], out_specs=blk,
        out_shape=jax.ShapeDtypeStruct((2, T, D_FF), BF16), compiler_params=_params("parallel", "parallel"),
    )(dc, dc, cw, cw)


def _mesh_pos():
    return lax.axis_index("x"), lax.axis_index("y"), lax.axis_index("c")


def _peer(pos, k):
    return (pos[0] ^ ((k >> 2) & 1), pos[1] ^ ((k >> 1) & 1), pos[2] ^ (k & 1))


def _index(pos):
    return 4 * pos[0] + 2 * pos[1] + pos[2]


class _Exchange:
    def __init__(self, kind, buf, relay=False):
        assert kind in ("gather", "scatter") and not (relay and kind == "scatter")
        self.kind, self.buf, self.relay = kind, buf, relay
        self.out_shape = jax.ShapeDtypeStruct(((N_DEV,) + buf.shape) if kind == "gather" else buf.shape, buf.dtype)
        self.spec = pl.BlockSpec(memory_space=pl.ANY)
        self.scratch = [pltpu.SemaphoreType.DMA((N_DEV - 1,)), pltpu.SemaphoreType.DMA((N_DEV - 1,)),
                        pltpu.SemaphoreType.DMA]

    def _src(self, x_ref, dest):
        return x_ref if self.kind == "gather" else x_ref.at[dest]

    def _copies(self, x_ref, out_ref, send_sems, recv_sems, local_sem):
        pos = _mesh_pos()
        me = _index(pos)
        local = pltpu.make_async_copy(self._src(x_ref, me), out_ref.at[me], local_sem)
        sends, recvs = [], []
        for k in range(1, N_DEV):
            peer = _peer(pos, k)
            sends.append(pltpu.make_async_remote_copy(
                src_ref=self._src(x_ref, _index(peer)), dst_ref=out_ref.at[me], send_sem=send_sems.at[k - 1],
                recv_sem=recv_sems.at[k - 1], device_id=peer, device_id_type=pl.DeviceIdType.MESH))
            recvs.append(pltpu.make_async_remote_copy(
                src_ref=self._src(x_ref, me), dst_ref=out_ref.at[_index(peer)], send_sem=send_sems.at[k - 1],
                recv_sem=recv_sems.at[k - 1], device_id=peer, device_id_type=pl.DeviceIdType.MESH))
        return local, sends, recvs

    def _relay_copies(self, x_ref, out_ref, send_sems, recv_sems, local_sem):
        x, y, c = _mesh_pos()
        me, sibling = (x, y, c), (x, y, 1 - c)
        chips = [(1 - x, y), (x, 1 - y), (1 - x, 1 - y)]

        def copy(k, block, to, own=False):
            return pltpu.make_async_remote_copy(
                src_ref=x_ref if own else out_ref.at[_index(block)], dst_ref=out_ref.at[_index(block)],
                send_sem=send_sems.at[k], recv_sem=recv_sems.at[k], device_id=to, device_id_type=pl.DeviceIdType.MESH)

        local = pltpu.make_async_copy(x_ref, out_ref.at[_index(me)], local_sem)
        first = [copy(0, me, sibling, own=True)] + [copy(1 + j, me, (*chip, c), own=True) for j, chip in enumerate(chips)]
        landed = [copy(1 + j, (*chip, c), me) for j, chip in enumerate(chips)]
        passed = [copy(4 + j, (*chip, c), sibling) for j, chip in enumerate(chips)]
        from_sibling = [copy(0, sibling, me)] + [copy(4 + j, (*chip, 1 - c), me) for j, chip in enumerate(chips)]
        return local, first, landed, passed, from_sibling

    def start(self, *refs):
        if self.relay:
            local, first = self._relay_copies(*refs)[:2]
            local.start()
            for cp in first:
                cp.start()
            return
        local, sends, _ = self._copies(*refs)
        local.start()
        for cp in sends:
            cp.start()

    def finish(self, *refs):
        if self.relay:
            local, first, landed, passed, from_sibling = self._relay_copies(*refs)
            for got, forward in zip(landed, passed):
                got.wait_recv()
                forward.start()
            for cp in from_sibling:
                cp.wait_recv()
            for cp in first + passed:
                cp.wait_send()
            local.wait()
            return
        local, sends, recvs = self._copies(*refs)
        for cp in recvs:
            cp.wait_recv()
        for cp in sends:
            cp.wait_send()
        local.wait()


def _hosted_call(body, *, name, grid, in_specs, out_specs, out_shape, scratch, args, semantics, exchange=None):
    if exchange is None:
        return pl.pallas_call(
            body, name=name, grid=grid, in_specs=in_specs, out_specs=out_specs, out_shape=out_shape,
            scratch_shapes=scratch, compiler_params=_params(*semantics))(*args)
    n_in, n_out, n_scr = len(in_specs), len(out_specs), len(scratch)

    def hosted(*refs):
        ins, x_ref = refs[:n_in], refs[n_in]
        outs, land_ref = refs[n_in + 1:n_in + 1 + n_out], refs[n_in + 1 + n_out]
        rest = refs[n_in + n_out + 2:]
        sems = rest[n_scr:]
        ids = [pl.program_id(a) for a in range(len(grid))]
        first, last = ids[0] == 0, ids[0] == grid[0] - 1
        for a in range(1, len(grid)):
            first, last = first & (ids[a] == 0), last & (ids[a] == grid[a] - 1)

        @pl.when(first)
        def _():
            exchange.start(x_ref, land_ref, *sems)

        body(*ins, *outs, *rest[:n_scr])

        @pl.when(last)
        def _():
            exchange.finish(x_ref, land_ref, *sems)

    return pl.pallas_call(
        hosted, name=name, grid=grid, in_specs=list(in_specs) + [exchange.spec],
        out_specs=list(out_specs) + [exchange.spec], out_shape=list(out_shape) + [exchange.out_shape],
        scratch_shapes=list(scratch) + exchange.scratch, compiler_params=_params(*(["arbitrary"] * len(grid))),
    )(*args, exchange.buf)


def _exchange_alone(exchange, name):
    def body(x_ref, out_ref, send_sems, recv_sems, local_sem):
        exchange.start(x_ref, out_ref, send_sems, recv_sems, local_sem)
        exchange.finish(x_ref, out_ref, send_sems, recv_sems, local_sem)

    return pl.pallas_call(
        body, name=name, out_shape=exchange.out_shape, in_specs=[exchange.spec], out_specs=exchange.spec,
        scratch_shapes=exchange.scratch)(exchange.buf)


def _adamw(w, g, m, v):
    m = ADAM_B1 * m + (1.0 - ADAM_B1) * g
    v = ADAM_B2 * v + (1.0 - ADAM_B2) * (g * g)
    m_hat = m / (1.0 - ADAM_B1 ** ADAM_STEP)
    v_hat = v / (1.0 - ADAM_B2 ** ADAM_STEP)
    delta = -ADAM_LR * (m_hat / (jnp.sqrt(v_hat) + ADAM_EPS) + ADAM_WD * w)
    return delta, m, v


def _sum_rows(parts, r0, rows, name, wmv=None):
    C = parts.shape[2]
    tr = max(t for t in range(16, ROWS + 1, 16) if rows % t == 0 and r0 % t == 0)

    def total(p_ref):
        g = p_ref[0].astype(F32)
        for i in range(1, N_DEV):
            g = g + p_ref[i].astype(F32)
        return g

    p_spec = pl.BlockSpec((N_DEV, tr, C), lambda i: (0, r0 // tr + i, 0))
    if wmv is None:
        def body(p_ref, g_ref):
            g_ref[...] = total(p_ref)

        return pl.pallas_call(
            body, name=name, grid=(rows // tr,), in_specs=[p_spec], out_specs=_row_spec(tr, C),
            out_shape=jax.ShapeDtypeStruct((rows, C), F32), compiler_params=_params("parallel"))(parts)

    def body(p_ref, w_ref, m_ref, v_ref, g_ref, d_ref, mo_ref, vo_ref):
        g = total(p_ref)
        g_ref[0] = g
        d_ref[0], mo_ref[0], vo_ref[0] = _adamw(w_ref[0], g, m_ref[0], v_ref[0])

    blk = pl.BlockSpec((1, tr, C), lambda i: (0, i, 0))
    return pl.pallas_call(
        body, name=name, grid=(rows // tr,), in_specs=[p_spec, blk, blk, blk], out_specs=[blk] * 4,
        out_shape=[jax.ShapeDtypeStruct((1, rows, C), F32)] * 4, compiler_params=_params("parallel"))(parts, *wmv)


def _sum_parts(parts, name):
    _, R, C = parts.shape

    def body(p_ref, g_ref):
        g = p_ref[0]
        for i in range(1, N_DEV):
            g = g + p_ref[i]
        g_ref[...] = g

    return pl.pallas_call(body, name=name, out_shape=jax.ShapeDtypeStruct((R, C), F32))(parts)


def _adamw_call(w, g, m, v, name):
    _, R, C = w.shape
    tr = min(ROWS, R)

    def body(w_ref, g_ref, m_ref, v_ref, d_ref, mo_ref, vo_ref):
        d_ref[...], mo_ref[...], vo_ref[...] = _adamw(w_ref[...], g_ref[...], m_ref[...], v_ref[...])

    blk = pl.BlockSpec((1, tr, C), lambda i: (0, i, 0))
    return pl.pallas_call(
        body, name=name, grid=(R // tr,), in_specs=[blk] * 4, out_specs=[blk] * 3,
        out_shape=[jax.ShapeDtypeStruct(w.shape, F32)] * 3, compiler_params=_params("parallel"))(w, g, m, v)


NORMS = ("mix_pre_norm", "mix_post_norm", "ca_pre_norm", "mem_norm", "ca_post_norm", "ffn_pre_norm", "ffn_post_norm")
SMALL = ("mix_pre_norm", "attn_sinks", "hgrn_lb_logits", "hgrn_out_norm", "mix_post_norm", "ca_pre_norm", "mem_norm",
         "ca_post_norm", "ffn_pre_norm", "ffn_conv_w", "ffn_conv_b", "ffn_post_norm")
SMALL_ROWS = 40
ROW_LOGITS, ROW_MISC, ROW_CONV_B, ROW_CONV_W = 7, 8, 9, 15
LANE_SINKS, LANE_LOSS = 128, 256
FF_PIECES = ((0, 1024), (1024, 2048), (2048, D_FF))


def _pack_small(norm_grads, dlogits, donw, dsinks, loss, d_cb, d_cw, name):
    def body(*refs):
        norm_refs = refs[:len(NORMS)]
        dl_ref, donw_ref, dsink_ref, loss_ref, cb_ref, cw_ref, out_ref = refs[len(NORMS):]
        out_ref[...] = jnp.zeros_like(out_ref)
        for i, ref in enumerate(norm_refs):
            out_ref[i:i + 1, :] = ref[...]
        out_ref[ROW_LOGITS:ROW_LOGITS + 1, 0:512] = dl_ref[0:1, :]
        out_ref[ROW_LOGITS:ROW_LOGITS + 1, 512:1024] = dl_ref[1:2, :]
        out_ref[ROW_MISC:ROW_MISC + 1, 0:HGRN_DIM] = donw_ref[...]
        out_ref[ROW_MISC:ROW_MISC + 1, LANE_SINKS:LANE_SINKS + ATTN_Q_HEADS] = dsink_ref[...]
        out_ref[ROW_MISC:ROW_MISC + 1, LANE_LOSS:LANE_LOSS + LANE] = loss_ref[...]
        for h in range(2):
            for j, (c0, c1) in enumerate(FF_PIECES):
                r = ROW_CONV_B + 3 * h + j
                out_ref[r:r + 1, 0:c1 - c0] = cb_ref[h, :, c0:c1]
                for t in range(3):
                    r = ROW_CONV_W + 3 * (3 * h + t) + j
                    out_ref[r:r + 1, 0:c1 - c0] = cw_ref[h, t:t + 1, c0:c1]

    return pl.pallas_call(
        body, name=name, out_shape=jax.ShapeDtypeStruct((SMALL_ROWS, 1024), F32),
    )(*norm_grads, dlogits, donw, dsinks, loss, d_cb, d_cw)


def _adamw_small(total, g_conv_w, w, m, v, name):
    n = len(SMALL)

    def body(*refs):
        t_ref, gcw_ref = refs[:2]
        w_refs, m_refs, v_refs = (dict(zip(SMALL, refs[2 + n * i:2 + n * (i + 1)])) for i in range(3))
        outs = refs[2 + 3 * n:]
        loss_ref = outs[0]
        g_refs, d_refs, mo_refs, vo_refs = (dict(zip(SMALL, outs[1 + n * i:1 + n * (i + 1)])) for i in range(4))
        loss_ref[...] = t_ref[ROW_MISC:ROW_MISC + 1, LANE_LOSS:LANE_LOSS + 1]

        def step(nm, idx, g):
            g_refs[nm][idx] = g
            d_refs[nm][idx], mo_refs[nm][idx], vo_refs[nm][idx] = _adamw(w_refs[nm][idx], g, m_refs[nm][idx], v_refs[nm][idx])

        everything = (slice(None), slice(None))
        for i, nm in enumerate(NORMS):
            step(nm, everything, t_ref[i:i + 1, :])
        step("hgrn_lb_logits", (slice(0, 1), slice(None)), t_ref[ROW_LOGITS:ROW_LOGITS + 1, 0:512])
        step("hgrn_lb_logits", (slice(1, 2), slice(None)), t_ref[ROW_LOGITS:ROW_LOGITS + 1, 512:1024])
        step("hgrn_out_norm", everything, t_ref[ROW_MISC:ROW_MISC + 1, 0:HGRN_DIM])
        step("attn_sinks", everything, t_ref[ROW_MISC:ROW_MISC + 1, LANE_SINKS:LANE_SINKS + ATTN_Q_HEADS])
        for h in range(2):
            for j, (c0, c1) in enumerate(FF_PIECES):
                r = ROW_CONV_B + 3 * h + j
                step("ffn_conv_b", (slice(None), slice(D_FF * h + c0, D_FF * h + c1)), t_ref[r:r + 1, 0:c1 - c0])
        step("ffn_conv_w", (slice(None), slice(None), slice(None)), gcw_ref[...])

    shapes = [jax.ShapeDtypeStruct(w[nm].shape, F32) for nm in SMALL]
    out = pl.pallas_call(
        body, name=name, out_shape=[jax.ShapeDtypeStruct((1, 1), F32)] + shapes * 4,
    )(total, g_conv_w, *[w[nm] for nm in SMALL], *[m[nm] for nm in SMALL], *[v[nm] for nm in SMALL])
    trees = [dict(zip(SMALL, out[1 + n * i:1 + n * (i + 1)])) for i in range(4)]
    return out[0], trees


BIG = ("w_in", "w_out", "ca_wq", "ca_wk", "ca_wv", "ca_wo", "ffn_w_up", "ffn_w_down")
BIG_FULL = {"w_in": (1024, 2816), "w_out": (1024, 1024), "ca_wq": (1024, 1024), "ca_wk": (1024, 1024),
            "ca_wv": (1024, 1024), "ca_wo": (1024, 1024), "ffn_w_up": (1024, 5632), "ffn_w_down": (2816, 1024)}
G_IN, G_MID, G_UP, G_DOWN = ("w_in",), ("w_out", "ca_wq", "ca_wk", "ca_wv", "ca_wo"), ("ffn_w_up",), ("ffn_w_down",)
GROUPS = (G_IN, G_MID, G_UP, G_DOWN)
COL_SHARDED = ("w_in", "ffn_w_up")
PACK_COLS = 1024


def _big_rows(name):
    r, c = BIG_FULL[name]
    return r * c // N_DEV // PACK_COLS


def _pack_shards(w, names):
    rows = [w[n][0].T if n in COL_SHARDED else w[n][0] for n in names]
    return (rows[0] if len(rows) == 1 else jnp.concatenate(rows, axis=0)).astype(BF16)


def _unpack_gathered(gathered, names):
    out, r0 = {}, 0
    for n in names:
        rows = _big_rows(n)
        out[n] = gathered[:, r0:r0 + rows].reshape(N_DEV * rows, PACK_COLS)
        r0 += rows
    return out


def _pack_full_grads(grads, names):
    parts = [grads[n].reshape(N_DEV, _big_rows(n), PACK_COLS) for n in names]
    return parts[0] if len(parts) == 1 else jnp.concatenate(parts, axis=1)


def kernel(x, mem, mix_pre_norm, w_in, attn_sinks, hgrn_lb_logits, hgrn_out_norm, w_out, mix_post_norm, ca_pre_norm, mem_norm, ca_wq, ca_wk, ca_wv, ca_wo, ca_post_norm, ffn_pre_norm, ffn_w_up, ffn_conv_w, ffn_conv_b, ffn_w_down, ffn_post_norm, loss_target, m_mix_pre_norm, m_w_in, m_attn_sinks, m_hgrn_lb_logits, m_hgrn_out_norm, m_w_out, m_mix_post_norm, m_ca_pre_norm, m_mem_norm, m_ca_wq, m_ca_wk, m_ca_wv, m_ca_wo, m_ca_post_norm, m_ffn_pre_norm, m_ffn_w_up, m_ffn_conv_w, m_ffn_conv_b, m_ffn_w_down, m_ffn_post_norm, v_mix_pre_norm, v_w_in, v_attn_sinks, v_hgrn_lb_logits, v_hgrn_out_norm, v_w_out, v_mix_post_norm, v_ca_pre_norm, v_mem_norm, v_ca_wq, v_ca_wk, v_ca_wv, v_ca_wo, v_ca_post_norm, v_ffn_pre_norm, v_ffn_w_up, v_ffn_conv_w, v_ffn_conv_b, v_ffn_w_down, v_ffn_post_norm):
    names = ["mix_pre_norm", "w_in", "attn_sinks", "hgrn_lb_logits", "hgrn_out_norm", "w_out", "mix_post_norm",
             "ca_pre_norm", "mem_norm", "ca_wq", "ca_wk", "ca_wv", "ca_wo", "ca_post_norm", "ffn_pre_norm",
             "ffn_w_up", "ffn_conv_w", "ffn_conv_b", "ffn_w_down", "ffn_post_norm"]
    w_all = dict(zip(names, [mix_pre_norm, w_in, attn_sinks, hgrn_lb_logits, hgrn_out_norm, w_out, mix_post_norm,
                             ca_pre_norm, mem_norm, ca_wq, ca_wk, ca_wv, ca_wo, ca_post_norm, ffn_pre_norm,
                             ffn_w_up, ffn_conv_w, ffn_conv_b, ffn_w_down, ffn_post_norm]))
    m_all = dict(zip(names, [m_mix_pre_norm, m_w_in, m_attn_sinks, m_hgrn_lb_logits, m_hgrn_out_norm, m_w_out,
                             m_mix_post_norm, m_ca_pre_norm, m_mem_norm, m_ca_wq, m_ca_wk, m_ca_wv, m_ca_wo,
                             m_ca_post_norm, m_ffn_pre_norm, m_ffn_w_up, m_ffn_conv_w, m_ffn_conv_b, m_ffn_w_down,
                             m_ffn_post_norm]))
    v_all = dict(zip(names, [v_mix_pre_norm, v_w_in, v_attn_sinks, v_hgrn_lb_logits, v_hgrn_out_norm, v_w_out,
                             v_mix_post_norm, v_ca_pre_norm, v_mem_norm, v_ca_wq, v_ca_wk, v_ca_wv, v_ca_wo,
                             v_ca_post_norm, v_ffn_pre_norm, v_ffn_w_up, v_ffn_conv_w, v_ffn_conv_b, v_ffn_w_down,
                             v_ffn_post_norm]))
    dev = _index(_mesh_pos())

    w_packs = {grp: _pack_shards(w_all, grp) for grp in GROUPS}
    shard_w = D_FF * 2 // N_DEV
    conv_w_rows = _exchange_alone(_Exchange("gather", ffn_conv_w[0]), "gather_conv_w")
    conv_w_full = conv_w_rows.transpose(1, 0, 2).reshape(3, 2 * D_FF)

    received, small_pack, grad_x = _local_step(
        x[0], mem[0], loss_target[0], w_packs, conv_w_full,
        {n: w_all[n] for n in NORMS}, attn_sinks, hgrn_lb_logits, hgrn_out_norm, ffn_conv_b)

    total = _sum_parts(_exchange_alone(_Exchange("gather", small_pack), "gather_small"), "sum_small")
    cw = total[ROW_CONV_W:ROW_CONV_W + 18].reshape(2, 3, 3 * PACK_COLS)[:, :, :D_FF]
    cw = cw.transpose(1, 0, 2).reshape(3, 2 * D_FF)
    g_conv_w = lax.dynamic_slice_in_dim(cw, dev * shard_w, shard_w, axis=1)[None]
    loss, (out_g, out_d, out_m, out_v) = _adamw_small(total, g_conv_w, w_all, m_all, v_all, "adamw_small")

    for grp in GROUPS:
        r0 = 0
        for n in grp:
            rows = _big_rows(n)
            if n in COL_SHARDED:
                g = _sum_rows(received[grp], r0, rows, "sum_" + n).T[None]
                d, mo, vo = _adamw_call(w_all[n], g, m_all[n], v_all[n], "adamw_" + n)
            else:
                g, d, mo, vo = _sum_rows(received[grp], r0, rows, "adamw_" + n, wmv=(w_all[n], m_all[n], v_all[n]))
            out_g[n], out_d[n], out_m[n], out_v[n] = g, d, mo, vo
            r0 += rows

    return (loss[0, 0], grad_x[None], *[out_g[n] for n in names], *[out_d[n] for n in names],
            *[out_m[n] for n in names], *[out_v[n] for n in names])


def _local_step(x, mem, target, w_packs, conv_w, norms, sinks, lb_logits, out_norm, conv_b):
    g1, g2, g3 = norms["mix_pre_norm"], norms["mix_post_norm"], norms["ca_pre_norm"]
    g4, g5, g6, g7 = norms["mem_norm"], norms["ca_post_norm"], norms["ffn_pre_norm"], norms["ffn_post_norm"]

    h1, gathered = _norm_fwd(x, g1, "mix_norm", exchange=_Exchange("gather", w_packs[G_IN], relay=True))
    w_in_t = _unpack_gathered(gathered, G_IN)["w_in"]
    up_shard = w_packs[G_UP]
    up_rows = up_shard.shape[0]
    up_cuts = (0, up_rows // 2, 3 * up_rows // 4, up_rows)
    up_parts = [up_shard[a:b] for a, b in zip(up_cuts[:-1], up_cuts[1:])]
    z, up_0 = _mm(h1, w_in_t, mode="nt", out_dtype=BF16, name="in_proj", tn=1408,
                  exchange=_Exchange("gather", up_parts[0]))
    attn, lse, gathered = _swa_fwd(z, sinks, "swa_fwd", exchange=_Exchange("gather", w_packs[G_DOWN]))
    w_down = _unpack_gathered(gathered, G_DOWN)["ffn_w_down"]
    lb = _lower_bound(lb_logits, "lower_bound")
    rec, o_rec, states, scores, gathered = _hgrn_fwd(
        z, lb, out_norm, "hgrn_fwd", exchange=_Exchange("gather", w_packs[G_MID]))
    w_out, wq, wk, wv, wo = (_unpack_gathered(gathered, G_MID)[n] for n in G_MID)
    cat = jnp.concatenate([attn, rec], axis=1)
    mix = _mm(cat, w_out, mode="nn", out_dtype=BF16, name="out_proj")
    x1, h2, up_1 = _post_pre(x, mix, g2, g3, "mix_post", exchange=_Exchange("gather", up_parts[1]))
    mem_n = _norm_fwd(mem, g4, "mem_norm")
    q = _mm(h2, wq, mode="nn", out_dtype=BF16, name="ca_q")
    k = _mm(mem_n, wk, mode="nn", out_dtype=BF16, name="ca_k")
    v = _mm(mem_n, wv, mode="nn", out_dtype=BF16, name="ca_v")
    oc = _ca_fwd(q, k, v, "ca_fwd")
    c = _mm(oc, wo, mode="nn", out_dtype=BF16, name="ca_o")
    x2, h3, up_2 = _post_pre(x1, c, g5, g6, "ca_post", exchange=_Exchange("gather", up_parts[2]))
    w_up_t = jnp.concatenate([up_0, up_1, up_2], axis=1).reshape(-1, PACK_COLS)
    u = _mm(h3, w_up_t, mode="nt", out_dtype=F32, name="ffn_up", tn=1408, split_out=True)
    a = _glu_fwd(u, conv_w, conv_b, "glu_fwd")
    y = _mm(a, w_down, mode="nn", out_dtype=BF16, name="ffn_down", tk=2816)
    loss, dx3, dy, dg7 = _final(x2, y, g7, target, "loss_head")

    da = _mm(dy, w_down, mode="nt", out_dtype=F32, name="ffn_down_dx", tn=1408)
    d_w_down = _mm(a, dy, mode="tn", out_dtype=BF16, name="ffn_down_dw", tm=1408, tk=1024)
    dc, d_cb, d_cw, got_down = _glu_bwd(
        u, conv_w, conv_b, da, "glu_bwd",
        exchange=_Exchange("scatter", _pack_full_grads({"ffn_w_down": d_w_down}, G_DOWN)))
    du = _conv_bwd(dc, conv_w, "conv_bwd")
    d_w_up_t = _mm(du, h3, mode="tn", out_dtype=BF16, name="ffn_up_dw", tm=1408, tk=1024, split_a=True)
    dh3, got_up = _mm(du, w_up_t, mode="nn", out_dtype=BF16, name="ffn_up_dx", tm=2048, tk=1408, split_a=True,
                      exchange=_Exchange("scatter", _pack_full_grads({"ffn_w_up": d_w_up_t}, G_UP)))
    dx2, dcv, dg6, dg5 = _norm_bwd2(dx3, dh3, x2, g6, c, g5, "ca_post_bwd")
    doc = _mm(dcv, wo, mode="nt", out_dtype=BF16, name="ca_o_dx")
    d_wo = _mm(oc, dcv, mode="tn", out_dtype=BF16, name="ca_o_dw", tm=1024, tk=1024)
    dq, dk, dv = _ca_bwd(q, k, v, doc, "ca_bwd")
    d_wq = _mm(h2, dq, mode="tn", out_dtype=BF16, name="ca_q_dw", tm=1024, tk=1024)
    dh2 = _mm(dq, wq, mode="nt", out_dtype=BF16, name="ca_q_dx")
    d_wk = _mm(mem_n, dk, mode="tn", out_dtype=BF16, name="ca_k_dw", tm=1024)
    d_wv = _mm(mem_n, dv, mode="tn", out_dtype=BF16, name="ca_v_dw", tm=1024)
    dmem_k = _mm(dk, wk, mode="nt", out_dtype=F32, name="ca_k_dx")
    dmem_v = _mm(dv, wv, mode="nt", out_dtype=F32, name="ca_v_dx")
    dg4 = _gain_bwd(mem, dmem_k, dmem_v, "mem_norm_bwd")
    dx1, dmix, dg3, dg2 = _norm_bwd2(dx2, dh2, x1, g3, mix, g2, "mix_post_bwd")
    dcat = _mm(dmix, w_out, mode="nt", out_dtype=BF16, name="out_proj_dx")
    d_w_out = _mm(cat, dmix, mode="tn", out_dtype=BF16, name="out_proj_dw", tm=1024, tk=1024)
    mid = {"w_out": d_w_out, "ca_wq": d_wq, "ca_wk": d_wk, "ca_wv": d_wv, "ca_wo": d_wo}
    dqr, dfr, dir_, dgr, dlb, donw, got_mid = _hgrn_bwd(
        z, lb, out_norm, o_rec, states, scores, dcat, "hgrn_bwd",
        exchange=_Exchange("scatter", _pack_full_grads(mid, G_MID)))
    dq_a, dka, dkb, dva, dvb, dsinks = _swa_bwd(z, sinks, dcat, lse, "swa_bwd")
    dz = _assemble_dz(dq_a, dka, dkb, dva, dvb, dqr, dfr, dir_, dgr, "assemble_dz")
    d_w_in_t = _mm(dz, h1, mode="tn", out_dtype=BF16, name="in_proj_dw", tm=1408, tk=1024)
    dh1, got_in = _mm(dz, w_in_t, mode="nn", out_dtype=BF16, name="in_proj_dx", tk=2816,
                      exchange=_Exchange("scatter", _pack_full_grads({"w_in": d_w_in_t}, G_IN)))
    dx, dg1 = _norm_bwd1(dx1, dh1, x, g1, "mix_norm_bwd")

    small_pack = _pack_small(
        (dg1, dg2, dg3, dg4, dg5, dg6, dg7), _lower_bound_bwd(lb, dlb, "lower_bound_bwd"), donw, dsinks, loss,
        d_cb, d_cw, "pack_small")
    return {G_IN: got_in, G_MID: got_mid, G_UP: got_up, G_DOWN: got_down}, small_pack, dx
```

```python
import jax
import jax.numpy as jnp
from jax import lax
from jax.experimental import pallas as pl
from jax.experimental.pallas import tpu as pltpu

F32 = jnp.float32
BF16 = jnp.bfloat16
EPS = 1e-6
N_DEV = 8
MESH_AXES = ("x", "y", "c")

ATTN_HEAD_DIM = 64
ATTN_Q_HEADS = 8
ATTN_KV_HEADS = 2
ATTN_BLOCK = 128
HGRN_HEADS = 4
HGRN_DIM = 128
HGRN_CHUNK = 64
HGRN_PAIR = 4
Z_Q, Z_F, Z_I, Z_G = 768, 1280, 1792, 2304
HGRN_LEVELS = (32, 16, 8, 4, 2, 1)
CA_HEADS = 4
CA_HEAD_DIM = 256
D_FF = 2816

ADAM_LR = 0.001
ADAM_B1 = 0.9
ADAM_B2 = 0.999
ADAM_EPS = 1e-08
ADAM_WD = 0.01
ADAM_STEP = 10

VMEM_LIMIT = 58 << 20
EPILOGUE_ROWS = 256
LANE = 128

NT = (((1,), (1,)), ((), ()))
TN = (((0,), (0,)), ((), ()))


def _params(*sem):
    return pltpu.CompilerParams(dimension_semantics=sem, vmem_limit_bytes=VMEM_LIMIT)


def _tile(n, cap):
    if n <= cap:
        return n
    best = 0
    for t in range(LANE, cap + 1, LANE):
        if n % t == 0:
            best = t
    assert best, (n, cap)
    return best


def _dot(a, b, dims=None):
    if dims is None:
        return jnp.dot(a, b, preferred_element_type=F32)
    return lax.dot_general(a, b, dims, preferred_element_type=F32)


def _bf(x):
    return x.astype(BF16)


def _sigmoid(x):
    return 1.0 / (1.0 + jnp.exp(-x))


def _rms(x):
    r = lax.rsqrt(jnp.mean(x * x, axis=-1, keepdims=True) + EPS)
    return x * r, r


def _rms_bwd(dxh, xh, r):
    return r * (dxh - xh * jnp.mean(dxh * xh, axis=-1, keepdims=True))


def _mm(a, b, *, mode, out_dtype, name, tm=1024, tn=1024, tk=1024, split_a=False, split_b=False, split_out=False,
        exchange=None, epilogue=None):
    def dims(arr, split):
        if split:
            return arr.shape[1], 2 * arr.shape[2]
        return arr.shape

    ar, ac = dims(a, split_a)
    br, bc = dims(b, split_b)
    if mode == "nn":
        M, K, N = ar, ac, bc
        assert br == K
    elif mode == "nt":
        M, K, N = ar, ac, br
        assert bc == K
    else:
        K, M, N = ar, ac, bc
        assert br == K
    a_cols_half = ac // 2 if split_a else None
    b_cols_half = bc // 2 if split_b else None
    tm = _tile(M, tm)
    tn = _tile((N // 2) if (split_out or (split_b and mode != "nt")) else N, tn)
    tk = _tile((K // 2) if ((split_a and mode != "tn") or (split_b and mode == "nt")) else K, tk)
    if split_a and mode == "tn":
        tm = _tile(M // 2, tm)
    gm, gn, gk = M // tm, N // tn, K // tk
    a_bytes, b_bytes = a.size * a.dtype.itemsize, b.size * b.dtype.itemsize
    rows_outer = gk > 1 or a_bytes + gm * b_bytes <= gn * a_bytes + b_bytes
    grid = (gm, gn, gk) if rows_outer else (gn, gm, gk)

    def spec(split, half, blk, rc):
        def imap(p, q, k):
            r, c = rc(*((p, q) if rows_outer else (q, p)), k)
            if not split:
                return (r, c)
            per_half = half // blk[1]
            return (c // per_half, r, c % per_half)

        return pl.BlockSpec(((None,) + blk) if split else blk, imap)

    if mode == "nn":
        a_spec = spec(split_a, a_cols_half, (tm, tk), lambda i, j, k: (i, k))
        b_spec = spec(split_b, b_cols_half, (tk, tn), lambda i, j, k: (k, j))
        dn = None
    elif mode == "nt":
        a_spec = spec(split_a, a_cols_half, (tm, tk), lambda i, j, k: (i, k))
        b_spec = spec(split_b, b_cols_half, (tn, tk), lambda i, j, k: (j, k))
        dn = NT
    else:
        a_spec = spec(split_a, a_cols_half, (tk, tm), lambda i, j, k: (k, i))
        b_spec = spec(split_b, b_cols_half, (tk, tn), lambda i, j, k: (k, j))
        dn = TN
    o_spec = spec(split_out, N // 2 if split_out else None, (tm, tn), lambda i, j, k: (i, j))
    out_shape = (2, M, N // 2) if split_out else (M, N)

    in_specs, out_specs, args = [a_spec, b_spec], [o_spec], (a, b)
    out_shapes = [jax.ShapeDtypeStruct(out_shape, out_dtype)]
    semantics = ("parallel", "parallel", "arbitrary")

    def store(result, extra, outs):
        outs[0][...] = result[...].astype(outs[0].dtype)

    if epilogue is not None:
        assert gn == 1 and not split_out
        n_vec = epilogue.n_out_vecs
        row = pl.BlockSpec((tm, N), lambda p, q, k: ((p if rows_outer else q), 0))
        vec = pl.BlockSpec((1, N), lambda p, q, k: (0, 0))
        in_specs += [row] * len(epilogue.rows) + [vec] * len(epilogue.vecs)
        args += tuple(epilogue.rows) + tuple(epilogue.vecs)
        out_specs = [row] * len(epilogue.out_rows) + [vec] * n_vec
        out_shapes = ([jax.ShapeDtypeStruct((M, N), dt) for dt in epilogue.out_rows]
                      + [jax.ShapeDtypeStruct((1, N), F32)] * n_vec)
        semantics = ("arbitrary",) * 3

        def store(result, extra, outs):
            n_rows, n_out_rows, sub = len(epilogue.rows), len(epilogue.out_rows), min(EPILOGUE_ROWS, tm)
            for r in range(0, tm, sub):
                rows = pl.ds(r, sub)
                epilogue.fn(result[r:r + sub], *[ref.at[rows] for ref in extra[:n_rows]], *extra[n_rows:],
                            *[ref.at[rows] for ref in outs[:n_out_rows]], *outs[n_out_rows:])

    n_extra = len(in_specs) - 2
    n_out = len(out_specs)

    def body(a_ref, b_ref, *refs):
        extra, outs, scratch_refs = refs[:n_extra], refs[n_extra:n_extra + n_out], refs[n_extra + n_out:]
        k = pl.program_id(2)
        if epilogue is not None:
            @pl.when((pl.program_id(0) == 0) & (pl.program_id(1) == 0) & (k == 0))
            def _():
                for ref in outs[n_out - epilogue.n_out_vecs:]:
                    ref[...] = jnp.zeros_like(ref)

        part = _dot(_bf(a_ref[...]), _bf(b_ref[...]), dn)
        if gk == 1:
            store(part, extra, outs)
            return
        acc_ref = scratch_refs[0]

        @pl.when(k == 0)
        def _():
            acc_ref[...] = jnp.zeros_like(acc_ref)

        acc_ref[...] += part

        @pl.when(k == gk - 1)
        def _():
            store(acc_ref, extra, outs)

    out = _hosted_call(
        body, name=name, grid=grid, in_specs=in_specs, out_specs=out_specs, out_shape=out_shapes,
        scratch=[] if gk == 1 else [pltpu.VMEM((tm, tn), F32)], args=args, semantics=semantics, exchange=exchange)
    return out[0] if (exchange is None and epilogue is None) else out


ROWS = 512


def _row_spec(tr, cols):
    return pl.BlockSpec((tr, cols), lambda i: (i, 0))


def _vec_spec(cols):
    return pl.BlockSpec((1, cols), lambda i: (0, 0))


def _norm_fwd(x, g, name, exchange=None):
    T, Dm = x.shape
    tr = min(ROWS, T)

    def body(x_ref, g_ref, h_ref):
        xh, _ = _rms(x_ref[...])
        h_ref[...] = (xh * g_ref[...]).astype(h_ref.dtype)

    out = _hosted_call(
        body, name=name, grid=(T // tr,), in_specs=[_row_spec(tr, Dm), _vec_spec(Dm)], out_specs=[_row_spec(tr, Dm)],
        out_shape=[jax.ShapeDtypeStruct((T, Dm), BF16)], scratch=[], args=(x, g), semantics=("parallel",),
        exchange=exchange)
    return out[0] if exchange is None else out


def _post_pre(x, m, g_post, g_pre, name, exchange=None):
    T, Dm = x.shape
    tr = min(ROWS, T)

    def body(x_ref, m_ref, gp_ref, gn_ref, xo_ref, h_ref):
        mh, _ = _rms(m_ref[...].astype(F32))
        xn = x_ref[...] + mh * gp_ref[...]
        xo_ref[...] = xn
        xh, _ = _rms(xn)
        h_ref[...] = (xh * gn_ref[...]).astype(h_ref.dtype)

    return _hosted_call(
        body, name=name, grid=(T // tr,),
        in_specs=[_row_spec(tr, Dm), _row_spec(tr, Dm), _vec_spec(Dm), _vec_spec(Dm)],
        out_specs=[_row_spec(tr, Dm), _row_spec(tr, Dm)],
        out_shape=[jax.ShapeDtypeStruct((T, Dm), F32), jax.ShapeDtypeStruct((T, Dm), BF16)],
        scratch=[], args=(x, m, g_post, g_pre), semantics=("parallel",), exchange=exchange)


def _final(x2, y, g_post, target, name):
    T, Dm = x2.shape
    tr = min(ROWS, T)

    def body(x_ref, y_ref, g_ref, t_ref, loss_ref, dx_ref, dy_ref, dg_ref):
        @pl.when(pl.program_id(0) == 0)
        def _():
            loss_ref[...] = jnp.zeros_like(loss_ref)
            dg_ref[...] = jnp.zeros_like(dg_ref)

        g = g_ref[...]
        yh, r = _rms(y_ref[...].astype(F32))
        d = x_ref[...] + yh * g - t_ref[...]
        loss_ref[...] += jnp.zeros((1, LANE), F32) + 0.5 * jnp.sum(jnp.mean(d * d, axis=-1, keepdims=True))
        dx = d * (1.0 / Dm)
        dx_ref[...] = dx
        dy_ref[...] = _rms_bwd(dx * g, yh, r).astype(dy_ref.dtype)
        dg_ref[...] += jnp.sum(dx * yh, axis=0, keepdims=True)

    return pl.pallas_call(
        body, name=name, grid=(T // tr,),
        in_specs=[_row_spec(tr, Dm), _row_spec(tr, Dm), _vec_spec(Dm), _row_spec(tr, Dm)],
        out_specs=[_vec_spec(LANE), _row_spec(tr, Dm), _row_spec(tr, Dm), _vec_spec(Dm)],
        out_shape=[jax.ShapeDtypeStruct((1, LANE), F32), jax.ShapeDtypeStruct((T, Dm), F32),
                   jax.ShapeDtypeStruct((T, Dm), BF16), jax.ShapeDtypeStruct((1, Dm), F32)],
        compiler_params=_params("arbitrary"),
    )(x2, y, g_post, target)


class _RowEpilogue:
    def __init__(self, fn, rows, vecs, out_rows, n_out_vecs):
        self.fn, self.rows, self.vecs, self.out_rows, self.n_out_vecs = fn, rows, vecs, out_rows, n_out_vecs


def _norm_bwd2(dx_cur, x_prev, m_prev, g_pre, g_post):
    def fn(dh, dx_ref, x_ref, m_ref, gn_ref, gp_ref, dxo_ref, dm_ref, dgn_ref, dgp_ref):
        xh, r = _rms(x_ref[...])
        dx = dx_ref[...] + _rms_bwd(dh * gn_ref[...], xh, r)
        dxo_ref[...] = dx
        dgn_ref[...] += jnp.sum(dh * xh, axis=0, keepdims=True)
        mh, rm = _rms(m_ref[...].astype(F32))
        dm_ref[...] = _rms_bwd(dx * gp_ref[...], mh, rm).astype(dm_ref.dtype)
        dgp_ref[...] += jnp.sum(dx * mh, axis=0, keepdims=True)

    return _RowEpilogue(fn, [dx_cur, x_prev, m_prev], [g_pre, g_post], [F32, BF16], 2)


def _norm_bwd1(dx_cur, x_prev, g_pre):
    def fn(dh, dx_ref, x_ref, gn_ref, dxo_ref, dgn_ref):
        xh, r = _rms(x_ref[...])
        dxo_ref[...] = dx_ref[...] + _rms_bwd(dh * gn_ref[...], xh, r)
        dgn_ref[...] += jnp.sum(dh * xh, axis=0, keepdims=True)

    return _RowEpilogue(fn, [dx_cur, x_prev], [g_pre], [F32], 1)


def _gain_bwd(x, dh_a, dh_b, name):
    T, Dm = x.shape

    def body(x_ref, a_ref, b_ref, dg_ref):
        xh, _ = _rms(x_ref[...])
        dg_ref[...] = jnp.sum((a_ref[...] + b_ref[...]) * xh, axis=0, keepdims=True)

    return pl.pallas_call(
        body, name=name, grid=(1,), in_specs=[_row_spec(T, Dm)] * 3, out_specs=_vec_spec(Dm),
        out_shape=jax.ShapeDtypeStruct((1, Dm), F32), compiler_params=_params("arbitrary"),
    )(x, dh_a, dh_b)


ATTN_GROUP = ATTN_Q_HEADS // ATTN_KV_HEADS
ASSEMBLE_ROWS = 1024


def _swa_mask(n):
    rows = ATTN_GROUP * ATTN_BLOCK
    row = lax.broadcasted_iota(jnp.int32, (rows, 2 * ATTN_BLOCK), 0) & (ATTN_BLOCK - 1)
    col = lax.broadcasted_iota(jnp.int32, (rows, 2 * ATTN_BLOCK), 1)
    diff = row + ATTN_BLOCK - col
    return (diff >= 0) & (diff < ATTN_BLOCK) & ((col >= ATTN_BLOCK) | (n > 0))


def _swa_rows(ref, hk, dtype):
    hd = ATTN_HEAD_DIM
    return jnp.concatenate(
        [ref[:, hd * (hk * ATTN_GROUP + g):hd * (hk * ATTN_GROUP + g + 1)].astype(dtype) for g in range(ATTN_GROUP)],
        axis=0)


def _swa_per_row(vals):
    seg = lax.broadcasted_iota(jnp.int32, (ATTN_GROUP * ATTN_BLOCK, 1), 0) // ATTN_BLOCK
    col = jnp.zeros((ATTN_GROUP * ATTN_BLOCK, 1), F32)
    for g, val in enumerate(vals):
        col = jnp.where(seg == g, val, col)
    return col


def _swa_specs():
    blk = ATTN_BLOCK
    prev = lambda n: jnp.maximum(n - 1, 0)
    return [
        pl.BlockSpec(memory_space=pltpu.SMEM),
        pl.BlockSpec((blk, 512), lambda n: (n, 0)),
        pl.BlockSpec((blk, 128), lambda n: (prev(n), 4)),
        pl.BlockSpec((blk, 128), lambda n: (n, 4)),
        pl.BlockSpec((blk, 128), lambda n: (prev(n), 5)),
        pl.BlockSpec((blk, 128), lambda n: (n, 5)),
    ]


def _swa_fwd(z, sinks, name, exchange=None):
    T = z.shape[0]
    blk, hd = ATTN_BLOCK, ATTN_HEAD_DIM
    scale = hd ** -0.5

    def body(sink_ref, q_ref, kp_ref, kc_ref, vp_ref, vc_ref, o_ref, lse_ref):
        allowed = _swa_mask(pl.program_id(0))
        hks = range(ATTN_KV_HEADS)
        kss = [slice(hd * hk, hd * hk + hd) for hk in hks]
        k = [_bf(jnp.concatenate([kp_ref[:, ks], kc_ref[:, ks]], axis=0)) for ks in kss]
        v = [_bf(jnp.concatenate([vp_ref[:, ks], vc_ref[:, ks]], axis=0)) for ks in kss]
        s = [jnp.where(allowed, _dot(_swa_rows(q_ref, hk, BF16), k[hk], NT) * scale, -1e30) for hk in hks]
        sink = [_swa_per_row([sink_ref[0, hk * ATTN_GROUP + g] for g in range(ATTN_GROUP)]) for hk in hks]
        m = [jnp.maximum(jnp.max(s[hk], axis=-1, keepdims=True), sink[hk]) for hk in hks]
        p = [jnp.exp(s[hk] - m[hk]) for hk in hks]
        l = [jnp.sum(p[hk], axis=-1, keepdims=True) + jnp.exp(sink[hk] - m[hk]) for hk in hks]
        o = [_dot(_bf(p[hk] / l[hk]), v[hk]).astype(o_ref.dtype) for hk in hks]
        for hk in hks:
            lse = m[hk] + jnp.log(l[hk])
            for g in range(ATTN_GROUP):
                h = hk * ATTN_GROUP + g
                o_ref[:, hd * h:hd * (h + 1)] = o[hk][blk * g:blk * (g + 1)]
                lse_ref[:, h:h + 1] = lse[blk * g:blk * (g + 1)]

    return _hosted_call(
        body, name=name, grid=(T // blk,), in_specs=_swa_specs(),
        out_specs=[pl.BlockSpec((blk, 512), lambda n: (n, 0)), pl.BlockSpec((blk, ATTN_Q_HEADS), lambda n: (n, 0))],
        out_shape=[jax.ShapeDtypeStruct((T, 512), BF16), jax.ShapeDtypeStruct((T, ATTN_Q_HEADS), F32)],
        scratch=[], args=(sinks, z, z, z, z, z), semantics=("parallel",), exchange=exchange)


def _swa_bwd(z, sinks, dcat, lse, name):
    T = z.shape[0]
    blk, hd = ATTN_BLOCK, ATTN_HEAD_DIM
    scale = hd ** -0.5
    group = ATTN_Q_HEADS // ATTN_KV_HEADS

    def body(sink_ref, q_ref, kp_ref, kc_ref, vp_ref, vc_ref, do_ref, lse_ref,
             dq_ref, dka_ref, dkb_ref, dva_ref, dvb_ref, dsink_ref):
        @pl.when(pl.program_id(0) == 0)
        def _():
            dsink_ref[...] = jnp.zeros_like(dsink_ref)

        allowed = _swa_mask(pl.program_id(0))
        lane = lax.broadcasted_iota(jnp.int32, (1, ATTN_Q_HEADS), 1)
        dsink = jnp.zeros((1, ATTN_Q_HEADS), F32)
        hks = range(ATTN_KV_HEADS)
        kss = [slice(hd * hk, hd * hk + hd) for hk in hks]
        k = [_bf(jnp.concatenate([kp_ref[:, ks], kc_ref[:, ks]], axis=0)) for ks in kss]
        v = [_bf(jnp.concatenate([vp_ref[:, ks], vc_ref[:, ks]], axis=0)) for ks in kss]
        qs = [_swa_rows(q_ref, hk, BF16) for hk in hks]
        dos = [_swa_rows(do_ref, hk, BF16) for hk in hks]
        lse = [jnp.concatenate([lse_ref[:, hk * group + g:hk * group + g + 1] for g in range(group)], axis=0)
               for hk in hks]
        s = [_dot(qs[hk], k[hk], NT) * scale for hk in hks]
        dp = [_dot(dos[hk], v[hk], NT) for hk in hks]
        p = [jnp.where(allowed, jnp.exp(jnp.where(allowed, s[hk], -1e30) - lse[hk]), 0.0) for hk in hks]
        delta = [jnp.sum(p[hk] * dp[hk], axis=-1, keepdims=True) for hk in hks]
        ds = [_bf(p[hk] * (dp[hk] - delta[hk]) * scale) for hk in hks]
        dq = [_dot(ds[hk], k[hk]).astype(dq_ref.dtype) for hk in hks]
        dk = [_dot(ds[hk], qs[hk], TN) for hk in hks]
        dv = [_dot(_bf(p[hk]), dos[hk], TN) for hk in hks]
        for hk in hks:
            sink = _swa_per_row([sink_ref[0, hk * group + g] for g in range(group)])
            sink_part = jnp.exp(sink - lse[hk]) * delta[hk]
            for g in range(group):
                h = hk * group + g
                dq_ref[:, hd * h:hd * (h + 1)] = dq[hk][blk * g:blk * (g + 1)]
                dsink = dsink + jnp.where(lane == h, -jnp.sum(sink_part[blk * g:blk * (g + 1)]), 0.0)
            dkb_ref[:, kss[hk]] = dk[hk][:blk]
            dka_ref[:, kss[hk]] = dk[hk][blk:]
            dvb_ref[:, kss[hk]] = dv[hk][:blk]
            dva_ref[:, kss[hk]] = dv[hk][blk:]
        dsink_ref[...] += dsink

    kv_out = pl.BlockSpec((blk, 128), lambda n: (n, 0))
    return pl.pallas_call(
        body, name=name, grid=(T // blk,),
        in_specs=_swa_specs() + [pl.BlockSpec((blk, 512), lambda n: (n, 0)),
                                 pl.BlockSpec((blk, ATTN_Q_HEADS), lambda n: (n, 0))],
        out_specs=[pl.BlockSpec((blk, 512), lambda n: (n, 0)), kv_out, kv_out, kv_out, kv_out,
                   pl.BlockSpec((1, ATTN_Q_HEADS), lambda n: (0, 0))],
        out_shape=[jax.ShapeDtypeStruct((T, 512), BF16)] + [jax.ShapeDtypeStruct((T, 128), F32)] * 4
        + [jax.ShapeDtypeStruct((1, ATTN_Q_HEADS), F32)],
        compiler_params=_params("arbitrary"),
    )(sinks, z, z, z, z, z, dcat, lse)


def _assemble_dz(dq_a, dka, dkb, dva, dvb, dqr, dfr, dir_, dgr, name):
    T = dq_a.shape[0]
    blk = ATTN_BLOCK
    rows = min(ASSEMBLE_ROWS, T)
    nb, per = T // rows, rows // blk

    def body(dq_ref, dka_ref, dkb_ref, dkn_ref, dva_ref, dvb_ref, dvn_ref, dqr_ref, dfr_ref, dir_ref, dgr_ref, o_ref):
        has_next = pl.program_id(0) < nb - 1

        def with_next(a_ref, b_ref, n_ref):
            after = jnp.where(has_next, n_ref[...], 0.0)
            shifted = after if per == 1 else jnp.concatenate([b_ref[blk:, :], after], axis=0)
            return (a_ref[...] + shifted).astype(o_ref.dtype)

        o_ref[:, 0:512] = dq_ref[...]
        o_ref[:, 512:640] = with_next(dka_ref, dkb_ref, dkn_ref)
        o_ref[:, 640:768] = with_next(dva_ref, dvb_ref, dvn_ref)
        o_ref[:, 768:1280] = dqr_ref[...]
        o_ref[:, 1280:1792] = dfr_ref[...]
        o_ref[:, 1792:2304] = dir_ref[...]
        o_ref[:, 2304:2816] = dgr_ref[...]

    cur = lambda w: pl.BlockSpec((rows, w), lambda n: (n, 0))
    nxt = pl.BlockSpec((blk, 128), lambda n: (jnp.minimum((n + 1) * per, T // blk - 1), 0))
    return pl.pallas_call(
        body, name=name, grid=(nb,),
        in_specs=[cur(512), cur(128), cur(128), nxt, cur(128), cur(128), nxt, cur(512), cur(512), cur(512), cur(512)],
        out_specs=pl.BlockSpec((rows, 2816), lambda n: (n, 0)),
        out_shape=jax.ShapeDtypeStruct((T, 2816), BF16), compiler_params=_params("parallel"),
    )(dq_a, dka, dkb, dkb, dva, dvb, dvb, dqr, dfr, dir_, dgr)


HGRN_ROWS = 512


def _hgrn_consts():
    c = HGRN_CHUNK
    r = lax.broadcasted_iota(jnp.int32, (c, c), 0)
    s = lax.broadcasted_iota(jnp.int32, (c, c), 1)
    rcol = lax.broadcasted_iota(jnp.int32, (c, 1), 0)
    same_block, upper = [], []
    for m in HGRN_LEVELS:
        same_block.append((r & ~(2 * m - 1)) == (s & ~(2 * m - 1)))
        upper.append((rcol & (2 * m - 1)) >= m)
    cum_mat = jnp.where(s <= r, 1.0, 0.0).astype(BF16)
    rev_mat = jnp.where(s >= r, 1.0, 0.0).astype(BF16)
    return cum_mat, rev_mat, r == s, same_block, upper, rcol & 3, s == r - 1


def _hgrn_level_decay(g, b, m, pos4):
    c = HGRN_CHUNK
    if m == 1:
        return jnp.exp(jnp.where((pos4 & 1) == 1, g, 0.0))
    if m == 2:
        after, before = pltpu.roll(g, c - 1, 0), pltpu.roll(g, 1, 0)
        return jnp.exp(jnp.where(pos4 == 0, after, jnp.where(pos4 == 1, 0.0, jnp.where(pos4 == 2, g, g + before))))
    b3 = b.reshape(c // (2 * m), 2 * m, HGRN_DIM)
    bref = jnp.broadcast_to(b3[:, m - 1:m, :], b3.shape).reshape(c, HGRN_DIM)
    return jnp.exp(-jnp.abs(b - bref))


def _split3(x):
    hi = _bf(x)
    r1 = x - hi.astype(F32)
    mid = _bf(r1)
    lo = _bf(r1 - mid.astype(F32))
    return jnp.concatenate([hi, mid, lo], axis=1)


def _dot_hilo(a, b):
    r, c = a.shape[0], b.shape[1]
    a_hi, b_hi = _bf(a), _bf(b)
    a2 = jnp.concatenate([a_hi, _bf(a - a_hi.astype(F32))], axis=0)
    b2 = jnp.concatenate([b_hi, _bf(b - b_hi.astype(F32))], axis=1)
    y = _dot(a2, b2)
    return y[:r, :c] + y[:r, c:] + y[r:, :c]


def _fold3(y):
    w = y.shape[1] // 3
    return y[:, :w] + y[:, w:2 * w] + y[:, 2 * w:]


def _hgrn_gates(qr, fr, lb):
    sq = _sigmoid(qr)
    q = qr * sq * (HGRN_DIM ** -0.5)
    sf = _sigmoid(fr)
    f = lb + (1.0 - lb) * sf
    k = (1.0 - lb) * _sigmoid(-fr)
    return q, sq, sf, f, k, jnp.log(f)


def _hgrn_intra(q, k, g, b, consts, scores=True):
    _, _, eye, same_block, upper, pos4, below = consts
    heads = range(len(q))
    a = None
    if scores:
        a = [jnp.where(eye, jnp.sum(q[hh] * k[hh], axis=1, keepdims=True), 0.0) for hh in heads]
    saved = [[] for _ in heads]
    for i, m in enumerate(HGRN_LEVELS):
        up = upper[i]
        e = [_hgrn_level_decay(g[hh], b[hh], m, pos4) for hh in heads]
        qt = [jnp.where(up, q[hh] * e[hh], 0.0) for hh in heads]
        kt = [jnp.where(up, 0.0, k[hh] * e[hh]) for hh in heads]
        for hh in heads:
            saved[hh].append((e[hh], qt[hh], kt[hh]))
        if not scores:
            continue
        if m == 1:
            for hh in heads:
                pair = jnp.sum(qt[hh] * pltpu.roll(kt[hh], 1, 0), axis=1, keepdims=True)
                a[hh] = a[hh] + jnp.where(below, pair, 0.0)
            continue
        p = [_dot(_bf(qt[hh]), _bf(kt[hh]), NT) for hh in heads]
        for hh in heads:
            a[hh] = a[hh] + jnp.where(same_block[i], p[hh], 0.0)
    return a, saved


def _hgrn_specs(tb, nb, rev):
    tmap = (lambda t: nb - 1 - t) if rev else (lambda t: t)
    assert HGRN_PAIR == HGRN_HEADS
    return [pl.BlockSpec((tb, 2816), lambda h, t: (tmap(t), 0)),
            pl.BlockSpec((1, HGRN_PAIR * HGRN_DIM), lambda h, t: (0, h)),
            pl.BlockSpec((1, HGRN_DIM), lambda h, t: (0, 0))]


def _hgrn_z(z_ref, sl, base, head):
    return z_ref[sl, base + HGRN_DIM * head:base + HGRN_DIM * (head + 1)].astype(F32)


def _hgrn_fwd(z, lb, onw, name, exchange=None):
    T = z.shape[0]
    tb = min(HGRN_ROWS, T)
    nb, c, nc = T // tb, HGRN_CHUNK, min(HGRN_ROWS, T) // HGRN_CHUNK

    def body(z_ref, lb_ref, onw_ref, rec_ref, o_ref, st_ref, a_ref, state):
        @pl.when(pl.program_id(1) == 0)
        def _():
            state[...] = jnp.zeros_like(state)

        consts = _hgrn_consts()
        lbv = lb_ref[...]
        onwv = onw_ref[...]

        def chunk(ci, carry):
            sl = pl.ds(pl.multiple_of(ci * c, c), c)
            heads = range(HGRN_PAIR)
            lss = [slice(HGRN_DIM * hh, HGRN_DIM * (hh + 1)) for hh in heads]
            gates = [_hgrn_gates(_hgrn_z(z_ref, sl, Z_Q, hh), _hgrn_z(z_ref, sl, Z_F, hh), lbv[:, lss[hh]])
                     for hh in heads]
            q, k, g = [t[0] for t in gates], [t[4] for t in gates], [t[5] for t in gates]
            v = [_bf(_hgrn_z(z_ref, sl, Z_I, hh)) for hh in heads]
            b = [_fold3(_dot(consts[0], _split3(g[hh]))) for hh in heads]
            a, _ = _hgrn_intra(q, k, g, b, consts)
            st = [state[hh] for hh in heads]
            for hh in heads:
                st_ref[hh, ci] = st[hh]
            bl = [b[hh][c - 1:c, :] for hh in heads]
            o_state = [_dot(_bf(q[hh] * jnp.exp(b[hh])), _bf(st[hh]), NT) for hh in heads]
            kv = [_dot(v[hh], _bf(k[hh] * jnp.exp(bl[hh] - b[hh])), TN) for hh in heads]
            a = [_bf(a[hh]) for hh in heads]
            o = [_dot(a[hh], v[hh]) + o_state[hh] for hh in heads]
            for hh in heads:
                a_ref[sl, c * hh:c * (hh + 1)] = a[hh]
                state[hh] = st[hh] * jnp.exp(bl[hh]) + kv[hh]
                o_ref[sl, lss[hh]] = o[hh]
                oh, _ = _rms(o[hh])
                gr = _hgrn_z(z_ref, sl, Z_G, hh)
                rec_ref[sl, lss[hh]] = (oh * onwv * (gr * _sigmoid(gr))).astype(rec_ref.dtype)
            return carry

        lax.fori_loop(0, nc, chunk, 0)

    in_specs = _hgrn_specs(tb, nb, False)
    out_blk = pl.BlockSpec((tb, HGRN_PAIR * HGRN_DIM), lambda h, t: (t, h))
    return _hosted_call(
        body, name=name, grid=(HGRN_HEADS // HGRN_PAIR, nb), in_specs=in_specs,
        out_specs=[out_blk, out_blk, pl.BlockSpec((HGRN_PAIR, nc, HGRN_DIM, HGRN_DIM), lambda h, t: (h, t, 0, 0)),
                   pl.BlockSpec((tb, HGRN_PAIR * c), lambda h, t: (t, h))],
        out_shape=[jax.ShapeDtypeStruct((T, 512), BF16), jax.ShapeDtypeStruct((T, 512), F32),
                   jax.ShapeDtypeStruct((HGRN_HEADS, T // c, HGRN_DIM, HGRN_DIM), F32),
                   jax.ShapeDtypeStruct((T, HGRN_HEADS * c), BF16)],
        scratch=[pltpu.VMEM((HGRN_PAIR, HGRN_DIM, HGRN_DIM), F32)], args=(z, lb, onw),
        semantics=("parallel", "arbitrary"), exchange=exchange)


def _hgrn_bwd(z, lb, onw, o, states, scores, dcat, name, exchange=None):
    T = z.shape[0]
    tb = min(HGRN_ROWS, T)
    nb, c, nc = T // tb, HGRN_CHUNK, min(HGRN_ROWS, T) // HGRN_CHUNK

    def body(z_ref, lb_ref, onw_ref, o_ref, st_ref, drec_ref, a_ref,
             dqr_ref, dfr_ref, dir_ref, dgr_ref, dlb_ref, donw_ref, dstate):
        @pl.when(pl.program_id(1) == 0)
        def _():
            dstate[...] = jnp.zeros_like(dstate)
            dlb_ref[...] = jnp.zeros_like(dlb_ref)

        @pl.when((pl.program_id(0) == 0) & (pl.program_id(1) == 0))
        def _():
            donw_ref[...] = jnp.zeros_like(donw_ref)

        consts = _hgrn_consts()
        rev_mat, eye, same_block, upper = consts[1:5]
        below = consts[6]
        lbv = lb_ref[...]
        onwv = onw_ref[...]
        last = lax.broadcasted_iota(jnp.int32, (c, 1), 0) == c - 1

        def chunk(i, carry):
            ci = nc - 1 - i
            sl = pl.ds(pl.multiple_of(ci * c, c), c)
            hs = range(HGRN_PAIR)
            lss = [slice(HGRN_DIM * hh, HGRN_DIM * (hh + 1)) for hh in hs]
            qr = [_hgrn_z(z_ref, sl, Z_Q, hh) for hh in hs]
            gates = [_hgrn_gates(qr[hh], _hgrn_z(z_ref, sl, Z_F, hh), lbv[:, lss[hh]]) for hh in hs]
            q, sq, sf, f, k, g = ([t[j] for t in gates] for j in range(6))
            v = [_bf(_hgrn_z(z_ref, sl, Z_I, hh)) for hh in hs]
            b = [_fold3(_dot(consts[0], _split3(g[hh]))) for hh in hs]
            _, saved = _hgrn_intra(q, k, g, b, consts, scores=False)
            a = [a_ref[sl, c * hh:c * (hh + 1)] for hh in hs]
            st = [st_ref[hh, ci] for hh in hs]
            dst = [dstate[hh] for hh in hs]

            gr = [_hgrn_z(z_ref, sl, Z_G, hh) for hh in hs]
            sg = [_sigmoid(gr[hh]) for hh in hs]
            norm = [_rms(o_ref[sl, ls]) for ls in lss]
            oh, r = [t[0] for t in norm], [t[1] for t in norm]
            drec = [drec_ref[sl, ls].astype(F32) for ls in lss]
            don = [drec[hh] * (gr[hh] * sg[hh]) for hh in hs]
            do = [_bf(_rms_bwd(don[hh] * onwv, oh[hh], r[hh])) for hh in hs]
            donw = jnp.sum(don[0] * oh[0], axis=0, keepdims=True)
            for hh in hs:
                dgr_ref[sl, lss[hh]] = (drec[hh] * oh[hh] * onwv
                                        * (sg[hh] * (1.0 + gr[hh] * (1.0 - sg[hh])))).astype(dgr_ref.dtype)
                if hh:
                    donw = donw + jnp.sum(don[hh] * oh[hh], axis=0, keepdims=True)
            donw_ref[...] += donw

            eb = [jnp.exp(b[hh]) for hh in hs]
            bl = [b[hh][c - 1:c, :] for hh in hs]
            ebl = [jnp.exp(bl[hh]) for hh in hs]
            ekb = [jnp.exp(bl[hh] - b[hh]) for hh in hs]
            qe = [q[hh] * eb[hh] for hh in hs]
            ke = [k[hh] * ekb[hh] for hh in hs]
            da = [_dot(do[hh], v[hh], NT) for hh in hs]
            dat = [_dot(v[hh], do[hh], NT) for hh in hs]
            dqe = [_dot(do[hh], _bf(st[hh])) for hh in hs]
            dke = [_dot(v[hh], _bf(dst[hh])) for hh in hs]
            dv_a = [_dot(a[hh], do[hh], TN) for hh in hs]
            dv_s = [_dot(_bf(ke[hh]), _bf(dst[hh]), NT) for hh in hs]
            dst_in = [_dot(do[hh], _bf(qe[hh]), TN) for hh in hs]
            dad = [jnp.sum(jnp.where(eye, da[hh], 0.0), axis=1, keepdims=True) for hh in hs]
            dq = [dqe[hh] * eb[hh] + dad[hh] * k[hh] for hh in hs]
            dk = [dke[hh] * ekb[hh] + dad[hh] * q[hh] for hh in hs]
            db_last = [jnp.sum(dke[hh] * ke[hh], axis=0, keepdims=True)
                       + jnp.sum(dst[hh] * st[hh], axis=0, keepdims=True) * ebl[hh] for hh in hs]
            for hh in hs:
                dstate[hh] = dst[hh] * ebl[hh] + dst_in[hh]
                dir_ref[sl, lss[hh]] = (dv_a[hh] + dv_s[hh]).astype(dir_ref.dtype)
            for lvl, m in enumerate(HGRN_LEVELS):
                if m == 1:
                    pair = [jnp.sum(jnp.where(below, da[hh], 0.0), axis=1, keepdims=True) for hh in hs]
                    xq = [pair[hh] * pltpu.roll(saved[hh][lvl][2], 1, 0) for hh in hs]
                    xk = [pltpu.roll(pair[hh] * saved[hh][lvl][1], c - 1, 0) for hh in hs]
                else:
                    xq = [_dot_hilo(jnp.where(same_block[lvl], da[hh], 0.0), saved[hh][lvl][2]) for hh in hs]
                    xk = [_dot_hilo(jnp.where(same_block[lvl], dat[hh], 0.0), saved[hh][lvl][1]) for hh in hs]
                for hh in hs:
                    e = saved[hh][lvl][0]
                    dq[hh] = dq[hh] + jnp.where(upper[lvl], xq[hh] * e, 0.0)
                    dk[hh] = dk[hh] + jnp.where(upper[lvl], 0.0, xk[hh] * e)
            db = [q[hh] * dq[hh] - k[hh] * dk[hh] + jnp.where(last, db_last[hh], 0.0) for hh in hs]
            dg = [_fold3(_dot(rev_mat, _split3(db[hh]))) for hh in hs]

            for hh in hs:
                ls = lss[hh]
                dqr_ref[sl, ls] = (dq[hh] * (HGRN_DIM ** -0.5)
                                   * (sq[hh] * (1.0 + qr[hh] * (1.0 - sq[hh])))).astype(dqr_ref.dtype)
                dfk = dg[hh] / f[hh] - dk[hh]
                dfr_ref[sl, ls] = ((1.0 - lbv[:, ls]) * sf[hh] * (1.0 - sf[hh]) * dfk).astype(dfr_ref.dtype)
                dlb_ref[:, ls] += jnp.sum((1.0 - sf[hh]) * dfk, axis=0, keepdims=True)
            return carry

        lax.fori_loop(0, nc, chunk, 0)

    in_specs = _hgrn_specs(tb, nb, True)
    rblk = pl.BlockSpec((tb, HGRN_PAIR * HGRN_DIM), lambda h, t: (nb - 1 - t, h))
    in_specs = in_specs + [
        rblk,
        pl.BlockSpec((HGRN_PAIR, nc, HGRN_DIM, HGRN_DIM), lambda h, t: (h, nb - 1 - t, 0, 0)),
        pl.BlockSpec((tb, HGRN_PAIR * HGRN_DIM), lambda h, t: (nb - 1 - t, 4 // HGRN_PAIR + h)),
        pl.BlockSpec((tb, HGRN_PAIR * c), lambda h, t: (nb - 1 - t, h)),
    ]
    return _hosted_call(
        body, name=name, grid=(HGRN_HEADS // HGRN_PAIR, nb), in_specs=in_specs,
        out_specs=[rblk, rblk, rblk, rblk, pl.BlockSpec((1, HGRN_PAIR * HGRN_DIM), lambda h, t: (0, h)),
                   pl.BlockSpec((1, HGRN_DIM), lambda h, t: (0, 0))],
        out_shape=[jax.ShapeDtypeStruct((T, 512), BF16)] * 4
        + [jax.ShapeDtypeStruct((1, 512), F32), jax.ShapeDtypeStruct((1, HGRN_DIM), F32)],
        scratch=[pltpu.VMEM((HGRN_PAIR, HGRN_DIM, HGRN_DIM), F32)], args=(z, lb, onw, o, states, dcat, scores),
        semantics=("arbitrary", "arbitrary"), exchange=exchange)


def _lower_bound(logits, name):
    def body(l_ref, lb_ref):
        l0, l1 = l_ref[0:1, :], l_ref[1:2, :]
        m = jnp.maximum(l0, l1)
        e0, e1 = jnp.exp(l0 - m), jnp.exp(l1 - m)
        lb_ref[...] = e0 / (e0 + e1)

    return pl.pallas_call(
        body, name=name, out_shape=jax.ShapeDtypeStruct((1, logits.shape[1]), F32),
    )(logits)


def _lower_bound_bwd(lb, dlb, name):
    def body(lb_ref, dlb_ref, dl_ref):
        p = lb_ref[...]
        d0 = dlb_ref[...] * p * (1.0 - p)
        dl_ref[0:1, :] = d0
        dl_ref[1:2, :] = -d0

    return pl.pallas_call(
        body, name=name, out_shape=jax.ShapeDtypeStruct((2, lb.shape[1]), F32),
    )(lb, dlb)


CA_ROWS = 512


def _ca_fwd(q, k, v, name):
    T, W = q.shape
    M = k.shape[0]
    tq = min(CA_ROWS, T)
    scale = CA_HEAD_DIM ** -0.5

    def body(q_ref, k_ref, v_ref, o_ref):
        for h in range(CA_HEADS):
            hs = slice(CA_HEAD_DIM * h, CA_HEAD_DIM * (h + 1))
            s = _dot(q_ref[:, hs], k_ref[:, hs], NT) * scale
            p = jnp.exp(s - jnp.max(s, axis=-1, keepdims=True))
            p = p / jnp.sum(p, axis=-1, keepdims=True)
            o_ref[:, hs] = _dot(_bf(p), v_ref[:, hs]).astype(o_ref.dtype)

    full = pl.BlockSpec((M, W), lambda i: (0, 0))
    return pl.pallas_call(
        body, name=name, grid=(T // tq,), in_specs=[_row_spec(tq, W), full, full], out_specs=_row_spec(tq, W),
        out_shape=jax.ShapeDtypeStruct((T, W), BF16), compiler_params=_params("parallel"),
    )(q, k, v)


def _ca_bwd(q, k, v, do, name):
    T, W = q.shape
    M = k.shape[0]
    tq = min(CA_ROWS, T)
    scale = CA_HEAD_DIM ** -0.5

    def body(q_ref, k_ref, v_ref, do_ref, dq_ref, dk_ref, dv_ref):
        @pl.when(pl.program_id(0) == 0)
        def _():
            dk_ref[...] = jnp.zeros_like(dk_ref)
            dv_ref[...] = jnp.zeros_like(dv_ref)

        for h in range(CA_HEADS):
            hs = slice(CA_HEAD_DIM * h, CA_HEAD_DIM * (h + 1))
            qh, kh, vh, doh = q_ref[:, hs], k_ref[:, hs], v_ref[:, hs], do_ref[:, hs]
            s = _dot(qh, kh, NT) * scale
            p = jnp.exp(s - jnp.max(s, axis=-1, keepdims=True))
            p = p / jnp.sum(p, axis=-1, keepdims=True)
            dp = _dot(doh, vh, NT)
            ds = _bf(p * (dp - jnp.sum(p * dp, axis=-1, keepdims=True)) * scale)
            dq_ref[:, hs] = _dot(ds, kh).astype(dq_ref.dtype)
            dk_ref[:, hs] += _dot(ds, qh, TN)
            dv_ref[:, hs] += _dot(_bf(p), doh, TN)

    full = pl.BlockSpec((M, W), lambda i: (0, 0))
    return pl.pallas_call(
        body, name=name, grid=(T // tq,), in_specs=[_row_spec(tq, W), full, full, _row_spec(tq, W)],
        out_specs=[_row_spec(tq, W), full, full],
        out_shape=[jax.ShapeDtypeStruct((T, W), BF16), jax.ShapeDtypeStruct((M, W), F32), jax.ShapeDtypeStruct((M, W), F32)],
        compiler_params=_params("arbitrary"),
    )(q, k, v, do)


FFN_ROWS = 256
FFN_COLS = 1408
GELU_C0 = 0.7978845608028654
GELU_C1 = 0.044715


def _gelu(x):
    t = jnp.tanh(GELU_C0 * (x + GELU_C1 * x * x * x))
    return 0.5 * x * (1.0 + t), t


def _gelu_grad(x, t):
    return 0.5 * (1.0 + t) + 0.5 * x * (1.0 - t * t) * GELU_C0 * (1.0 + 3.0 * GELU_C1 * x * x)


def _shift_down(cur, halo, first, tb):
    row = lax.broadcasted_iota(jnp.int32, (tb, 1), 0)
    h6 = jnp.where(first, 0.0, halo[6:7])
    h7 = jnp.where(first, 0.0, halo[7:8])
    u1 = jnp.where(row == 0, h7, pltpu.roll(cur, 1, 0))
    u2 = jnp.where(row == 0, h6, jnp.where(row == 1, h7, pltpu.roll(cur, 2, 0)))
    return u1, u2


def _conv(u_ref, halo_ref, w_ref, b_ref, half, first, tb):
    cur = u_ref[half]
    u1, u2 = _shift_down(cur, halo_ref[half], first, tb)
    w = w_ref[...]
    return w[0:1] * u2 + w[1:2] * u1 + w[2:3] * cur + b_ref[...], cur, u1, u2


def _ffn_specs(tb, tc, rows_first):
    nj = D_FF // tc
    rc = (lambda a, b: (a, b)) if rows_first else (lambda a, b: (b, a))
    def at(f):
        return lambda a, b: f(*rc(a, b))
    blk = pl.BlockSpec((2, tb, tc), at(lambda t, j: (0, t, j)))
    halo = pl.BlockSpec((2, 8, tc), at(lambda t, j: (0, jnp.maximum(t * (tb // 8) - 1, 0), j)))
    wg = pl.BlockSpec((3, tc), at(lambda t, j: (0, j)))
    wv = pl.BlockSpec((3, tc), at(lambda t, j: (0, j + nj)))
    bg = pl.BlockSpec((1, tc), at(lambda t, j: (0, j)))
    bv = pl.BlockSpec((1, tc), at(lambda t, j: (0, j + nj)))
    flat = pl.BlockSpec((tb, tc), at(lambda t, j: (t, j)))
    return blk, halo, wg, wv, bg, bv, flat


def _glu_fwd(u, cw, cb, name):
    T = u.shape[1]
    tb, tc = min(FFN_ROWS, T), FFN_COLS

    def body(u_ref, halo_ref, wg_ref, wv_ref, bg_ref, bv_ref, a_ref):
        first = pl.program_id(0) == 0
        cg = _conv(u_ref, halo_ref, wg_ref, bg_ref, 0, first, tb)[0]
        cv = _conv(u_ref, halo_ref, wv_ref, bv_ref, 1, first, tb)[0]
        a_ref[...] = (_gelu(cg)[0] * cv).astype(a_ref.dtype)

    blk, halo, wg, wv, bg, bv, flat = _ffn_specs(tb, tc, True)
    return pl.pallas_call(
        body, name=name, grid=(T // tb, D_FF // tc), in_specs=[blk, halo, wg, wv, bg, bv], out_specs=flat,
        out_shape=jax.ShapeDtypeStruct((T, D_FF), BF16), compiler_params=_params("parallel", "parallel"),
    )(u, u, cw, cw, cb, cb)


def _glu_bwd(u, cw, cb, da, name, exchange=None):
    T = u.shape[1]
    tb, tc = min(FFN_ROWS, T), FFN_COLS

    def body(u_ref, halo_ref, wg_ref, wv_ref, bg_ref, bv_ref, da_ref, dc_ref, db_ref, dw_ref):
        first = pl.program_id(1) == 0

        @pl.when(first)
        def _():
            db_ref[...] = jnp.zeros_like(db_ref)
            dw_ref[...] = jnp.zeros_like(dw_ref)

        cg, ug, ug1, ug2 = _conv(u_ref, halo_ref, wg_ref, bg_ref, 0, first, tb)
        cv, uv, uv1, uv2 = _conv(u_ref, halo_ref, wv_ref, bv_ref, 1, first, tb)
        da = da_ref[...]
        gl, t = _gelu(cg)
        dcg = da * cv * _gelu_grad(cg, t)
        dcv = da * gl
        dc_ref[0] = dcg
        dc_ref[1] = dcv
        for half, dc, taps in ((0, dcg, (ug2, ug1, ug)), (1, dcv, (uv2, uv1, uv))):
            db_ref[half] += jnp.sum(dc, axis=0, keepdims=True)
            for tap in range(3):
                dw_ref[half, tap:tap + 1, :] += jnp.sum(dc * taps[tap], axis=0, keepdims=True)

    blk, halo, wg, wv, bg, bv, flat = _ffn_specs(tb, tc, False)
    return _hosted_call(
        body, name=name, grid=(D_FF // tc, T // tb), in_specs=[blk, halo, wg, wv, bg, bv, flat],
        out_specs=[blk, pl.BlockSpec((2, 1, tc), lambda j, t: (0, 0, j)), pl.BlockSpec((2, 3, tc), lambda j, t: (0, 0, j))],
        out_shape=[jax.ShapeDtypeStruct((2, T, D_FF), F32), jax.ShapeDtypeStruct((2, 1, D_FF), F32),
                   jax.ShapeDtypeStruct((2, 3, D_FF), F32)],
        scratch=[], args=(u, u, cw, cw, cb, cb, da), semantics=("parallel", "arbitrary"), exchange=exchange)


def _conv_bwd(dc, cw, name):
    T = dc.shape[1]
    tb, tc = min(FFN_ROWS, T), FFN_COLS
    nt, nj = T // tb, D_FF // tc

    def body(dc_ref, halo_ref, wg_ref, wv_ref, du_ref):
        last = pl.program_id(0) == nt - 1
        row = lax.broadcasted_iota(jnp.int32, (tb, 1), 0)
        for half, w_ref in ((0, wg_ref), (1, wv_ref)):
            cur = dc_ref[half]
            halo = halo_ref[half]
            h0 = jnp.where(last, 0.0, halo[0:1])
            h1 = jnp.where(last, 0.0, halo[1:2])
            d1 = jnp.where(row == tb - 1, h0, pltpu.roll(cur, tb - 1, 0))
            d2 = jnp.where(row == tb - 1, h1, jnp.where(row == tb - 2, h0, pltpu.roll(cur, tb - 2, 0)))
            w = w_ref[...]
            du_ref[half] = (w[2:3] * cur + w[1:2] * d1 + w[0:1] * d2).astype(du_ref.dtype)

    blk = pl.BlockSpec((2, tb, tc), lambda t, j: (0, t, j))
    halo = pl.BlockSpec((2, 8, tc), lambda t, j: (0, jnp.minimum((t + 1) * (tb // 8), T // 8 - 1), j))
    wg = pl.BlockSpec((3, tc), lambda t, j: (0, j))
    wv = pl.BlockSpec((3, tc), lambda t, j: (0, j + nj))
    return pl.pallas_call(
        body, name=name, grid=(nt, nj), in_specs=[blk, halo, wg, wv], out_specs=blk,
        out_shape=jax.ShapeDtypeStruct((2, T, D_FF), BF16), compiler_params=_params("parallel", "parallel"),
    )(dc, dc, cw, cw)


def _mesh_pos():
    return lax.axis_index("x"), lax.axis_index("y"), lax.axis_index("c")


def _peer(pos, k):
    return (pos[0] ^ ((k >> 2) & 1), pos[1] ^ ((k >> 1) & 1), pos[2] ^ (k & 1))


def _index(pos):
    return 4 * pos[0] + 2 * pos[1] + pos[2]


class _Exchange:
    def __init__(self, kind, buf, relay=False):
        assert kind in ("gather", "scatter") and not (relay and kind == "scatter")
        self.kind, self.buf, self.relay = kind, buf, relay
        self.out_shape = jax.ShapeDtypeStruct(((N_DEV,) + buf.shape) if kind == "gather" else buf.shape, buf.dtype)
        self.spec = pl.BlockSpec(memory_space=pl.ANY)
        self.scratch = [pltpu.SemaphoreType.DMA((N_DEV - 1,)), pltpu.SemaphoreType.DMA((N_DEV - 1,)),
                        pltpu.SemaphoreType.DMA]

    def _src(self, x_ref, dest):
        return x_ref if self.kind == "gather" else x_ref.at[dest]

    def _copies(self, x_ref, out_ref, send_sems, recv_sems, local_sem):
        pos = _mesh_pos()
        me = _index(pos)
        local = pltpu.make_async_copy(self._src(x_ref, me), out_ref.at[me], local_sem)
        sends, recvs = [], []
        for k in range(1, N_DEV):
            peer = _peer(pos, k)
            sends.append(pltpu.make_async_remote_copy(
                src_ref=self._src(x_ref, _index(peer)), dst_ref=out_ref.at[me], send_sem=send_sems.at[k - 1],
                recv_sem=recv_sems.at[k - 1], device_id=peer, device_id_type=pl.DeviceIdType.MESH))
            recvs.append(pltpu.make_async_remote_copy(
                src_ref=self._src(x_ref, me), dst_ref=out_ref.at[_index(peer)], send_sem=send_sems.at[k - 1],
                recv_sem=recv_sems.at[k - 1], device_id=peer, device_id_type=pl.DeviceIdType.MESH))
        return local, sends, recvs

    def _relay_copies(self, x_ref, out_ref, send_sems, recv_sems, local_sem):
        x, y, c = _mesh_pos()
        me, sibling = (x, y, c), (x, y, 1 - c)
        chips = [(1 - x, y), (x, 1 - y), (1 - x, 1 - y)]

        def copy(k, block, to, own=False):
            return pltpu.make_async_remote_copy(
                src_ref=x_ref if own else out_ref.at[_index(block)], dst_ref=out_ref.at[_index(block)],
                send_sem=send_sems.at[k], recv_sem=recv_sems.at[k], device_id=to, device_id_type=pl.DeviceIdType.MESH)

        local = pltpu.make_async_copy(x_ref, out_ref.at[_index(me)], local_sem)
        first = [copy(0, me, sibling, own=True)] + [copy(1 + j, me, (*chip, c), own=True) for j, chip in enumerate(chips)]
        landed = [copy(1 + j, (*chip, c), me) for j, chip in enumerate(chips)]
        passed = [copy(4 + j, (*chip, c), sibling) for j, chip in enumerate(chips)]
        from_sibling = [copy(0, sibling, me)] + [copy(4 + j, (*chip, 1 - c), me) for j, chip in enumerate(chips)]
        return local, first, landed, passed, from_sibling

    def start(self, *refs):
        if self.relay:
            local, first = self._relay_copies(*refs)[:2]
            local.start()
            for cp in first:
                cp.start()
            return
        local, sends, _ = self._copies(*refs)
        local.start()
        for cp in sends:
            cp.start()

    def finish(self, *refs):
        if self.relay:
            local, first, landed, passed, from_sibling = self._relay_copies(*refs)
            for got, forward in zip(landed, passed):
                got.wait_recv()
                forward.start()
            for cp in from_sibling:
                cp.wait_recv()
            for cp in first + passed:
                cp.wait_send()
            local.wait()
            return
        local, sends, recvs = self._copies(*refs)
        for cp in recvs:
            cp.wait_recv()
        for cp in sends:
            cp.wait_send()
        local.wait()


def _hosted_call(body, *, name, grid, in_specs, out_specs, out_shape, scratch, args, semantics, exchange=None):
    if exchange is None:
        return pl.pallas_call(
            body, name=name, grid=grid, in_specs=in_specs, out_specs=out_specs, out_shape=out_shape,
            scratch_shapes=scratch, compiler_params=_params(*semantics))(*args)
    n_in, n_out, n_scr = len(in_specs), len(out_specs), len(scratch)

    def hosted(*refs):
        ins, x_ref = refs[:n_in], refs[n_in]
        outs, land_ref = refs[n_in + 1:n_in + 1 + n_out], refs[n_in + 1 + n_out]
        rest = refs[n_in + n_out + 2:]
        sems = rest[n_scr:]
        ids = [pl.program_id(a) for a in range(len(grid))]
        first, last = ids[0] == 0, ids[0] == grid[0] - 1
        for a in range(1, len(grid)):
            first, last = first & (ids[a] == 0), last & (ids[a] == grid[a] - 1)

        @pl.when(first)
        def _():
            exchange.start(x_ref, land_ref, *sems)

        body(*ins, *outs, *rest[:n_scr])

        @pl.when(last)
        def _():
            exchange.finish(x_ref, land_ref, *sems)

    return pl.pallas_call(
        hosted, name=name, grid=grid, in_specs=list(in_specs) + [exchange.spec],
        out_specs=list(out_specs) + [exchange.spec], out_shape=list(out_shape) + [exchange.out_shape],
        scratch_shapes=list(scratch) + exchange.scratch, compiler_params=_params(*(["arbitrary"] * len(grid))),
    )(*args, exchange.buf)


def _exchange_alone(exchange, name):
    def body(x_ref, out_ref, send_sems, recv_sems, local_sem):
        exchange.start(x_ref, out_ref, send_sems, recv_sems, local_sem)
        exchange.finish(x_ref, out_ref, send_sems, recv_sems, local_sem)

    return pl.pallas_call(
        body, name=name, out_shape=exchange.out_shape, in_specs=[exchange.spec], out_specs=exchange.spec,
        scratch_shapes=exchange.scratch)(exchange.buf)


def _adamw(w, g, m, v):
    m = ADAM_B1 * m + (1.0 - ADAM_B1) * g
    v = ADAM_B2 * v + (1.0 - ADAM_B2) * (g * g)
    m_hat = m / (1.0 - ADAM_B1 ** ADAM_STEP)
    v_hat = v / (1.0 - ADAM_B2 ** ADAM_STEP)
    delta = -ADAM_LR * (m_hat / (jnp.sqrt(v_hat) + ADAM_EPS) + ADAM_WD * w)
    return delta, m, v


def _sum_rows(parts, r0, rows, name, wmv=None):
    C = parts.shape[2]
    tr = max(t for t in range(16, ROWS + 1, 16) if rows % t == 0 and r0 % t == 0)

    def total(p_ref):
        g = p_ref[0].astype(F32)
        for i in range(1, N_DEV):
            g = g + p_ref[i].astype(F32)
        return g

    p_spec = pl.BlockSpec((N_DEV, tr, C), lambda i: (0, r0 // tr + i, 0))
    if wmv is None:
        def body(p_ref, g_ref):
            g_ref[...] = total(p_ref)

        return pl.pallas_call(
            body, name=name, grid=(rows // tr,), in_specs=[p_spec], out_specs=_row_spec(tr, C),
            out_shape=jax.ShapeDtypeStruct((rows, C), F32), compiler_params=_params("parallel"))(parts)

    def body(p_ref, w_ref, m_ref, v_ref, g_ref, d_ref, mo_ref, vo_ref):
        g = total(p_ref)
        g_ref[0] = g
        d_ref[0], mo_ref[0], vo_ref[0] = _adamw(w_ref[0], g, m_ref[0], v_ref[0])

    blk = pl.BlockSpec((1, tr, C), lambda i: (0, i, 0))
    return pl.pallas_call(
        body, name=name, grid=(rows // tr,), in_specs=[p_spec, blk, blk, blk], out_specs=[blk] * 4,
        out_shape=[jax.ShapeDtypeStruct((1, rows, C), F32)] * 4, compiler_params=_params("parallel"))(parts, *wmv)


def _sum_parts(parts, name):
    _, R, C = parts.shape

    def body(p_ref, g_ref):
        g = p_ref[0]
        for i in range(1, N_DEV):
            g = g + p_ref[i]
        g_ref[...] = g

    return pl.pallas_call(body, name=name, out_shape=jax.ShapeDtypeStruct((R, C), F32))(parts)


def _adamw_call(w, g, m, v, name):
    _, R, C = w.shape
    tr = min(ROWS, R)

    def body(w_ref, g_ref, m_ref, v_ref, d_ref, mo_ref, vo_ref):
        d_ref[...], mo_ref[...], vo_ref[...] = _adamw(w_ref[...], g_ref[...], m_ref[...], v_ref[...])

    blk = pl.BlockSpec((1, tr, C), lambda i: (0, i, 0))
    return pl.pallas_call(
        body, name=name, grid=(R // tr,), in_specs=[blk] * 4, out_specs=[blk] * 3,
        out_shape=[jax.ShapeDtypeStruct(w.shape, F32)] * 3, compiler_params=_params("parallel"))(w, g, m, v)


NORMS = ("mix_pre_norm", "mix_post_norm", "ca_pre_norm", "mem_norm", "ca_post_norm", "ffn_pre_norm", "ffn_post_norm")
SMALL = ("mix_pre_norm", "attn_sinks", "hgrn_lb_logits", "hgrn_out_norm", "mix_post_norm", "ca_pre_norm", "mem_norm",
         "ca_post_norm", "ffn_pre_norm", "ffn_conv_w", "ffn_conv_b", "ffn_post_norm")
SMALL_ROWS = 40
ROW_LOGITS, ROW_MISC, ROW_CONV_B, ROW_CONV_W = 7, 8, 9, 15
LANE_SINKS, LANE_LOSS = 128, 256
FF_PIECES = ((0, 1024), (1024, 2048), (2048, D_FF))


def _pack_small(norm_grads, dlogits, donw, dsinks, loss, d_cb, d_cw, name):
    def body(*refs):
        norm_refs = refs[:len(NORMS)]
        dl_ref, donw_ref, dsink_ref, loss_ref, cb_ref, cw_ref, out_ref = refs[len(NORMS):]
        out_ref[...] = jnp.zeros_like(out_ref)
        for i, ref in enumerate(norm_refs):
            out_ref[i:i + 1, :] = ref[...]
        out_ref[ROW_LOGITS:ROW_LOGITS + 1, 0:512] = dl_ref[0:1, :]
        out_ref[ROW_LOGITS:ROW_LOGITS + 1, 512:1024] = dl_ref[1:2, :]
        out_ref[ROW_MISC:ROW_MISC + 1, 0:HGRN_DIM] = donw_ref[...]
        out_ref[ROW_MISC:ROW_MISC + 1, LANE_SINKS:LANE_SINKS + ATTN_Q_HEADS] = dsink_ref[...]
        out_ref[ROW_MISC:ROW_MISC + 1, LANE_LOSS:LANE_LOSS + LANE] = loss_ref[...]
        for h in range(2):
            for j, (c0, c1) in enumerate(FF_PIECES):
                r = ROW_CONV_B + 3 * h + j
                out_ref[r:r + 1, 0:c1 - c0] = cb_ref[h, :, c0:c1]
                for t in range(3):
                    r = ROW_CONV_W + 3 * (3 * h + t) + j
                    out_ref[r:r + 1, 0:c1 - c0] = cw_ref[h, t:t + 1, c0:c1]

    return pl.pallas_call(
        body, name=name, out_shape=jax.ShapeDtypeStruct((SMALL_ROWS, 1024), F32),
    )(*norm_grads, dlogits, donw, dsinks, loss, d_cb, d_cw)


def _adamw_small(total, g_conv_w, w, m, v, name):
    n = len(SMALL)

    def body(*refs):
        t_ref, gcw_ref = refs[:2]
        w_refs, m_refs, v_refs = (dict(zip(SMALL, refs[2 + n * i:2 + n * (i + 1)])) for i in range(3))
        outs = refs[2 + 3 * n:]
        loss_ref = outs[0]
        g_refs, d_refs, mo_refs, vo_refs = (dict(zip(SMALL, outs[1 + n * i:1 + n * (i + 1)])) for i in range(4))
        loss_ref[...] = t_ref[ROW_MISC:ROW_MISC + 1, LANE_LOSS:LANE_LOSS + 1]

        def step(nm, idx, g):
            g_refs[nm][idx] = g
            d_refs[nm][idx], mo_refs[nm][idx], vo_refs[nm][idx] = _adamw(w_refs[nm][idx], g, m_refs[nm][idx], v_refs[nm][idx])

        everything = (slice(None), slice(None))
        for i, nm in enumerate(NORMS):
            step(nm, everything, t_ref[i:i + 1, :])
        step("hgrn_lb_logits", (slice(0, 1), slice(None)), t_ref[ROW_LOGITS:ROW_LOGITS + 1, 0:512])
        step("hgrn_lb_logits", (slice(1, 2), slice(None)), t_ref[ROW_LOGITS:ROW_LOGITS + 1, 512:1024])
        step("hgrn_out_norm", everything, t_ref[ROW_MISC:ROW_MISC + 1, 0:HGRN_DIM])
        step("attn_sinks", everything, t_ref[ROW_MISC:ROW_MISC + 1, LANE_SINKS:LANE_SINKS + ATTN_Q_HEADS])
        for h in range(2):
            for j, (c0, c1) in enumerate(FF_PIECES):
                r = ROW_CONV_B + 3 * h + j
                step("ffn_conv_b", (slice(None), slice(D_FF * h + c0, D_FF * h + c1)), t_ref[r:r + 1, 0:c1 - c0])
        step("ffn_conv_w", (slice(None), slice(None), slice(None)), gcw_ref[...])

    shapes = [jax.ShapeDtypeStruct(w[nm].shape, F32) for nm in SMALL]
    out = pl.pallas_call(
        body, name=name, out_shape=[jax.ShapeDtypeStruct((1, 1), F32)] + shapes * 4,
    )(total, g_conv_w, *[w[nm] for nm in SMALL], *[m[nm] for nm in SMALL], *[v[nm] for nm in SMALL])
    trees = [dict(zip(SMALL, out[1 + n * i:1 + n * (i + 1)])) for i in range(4)]
    return out[0], trees


BIG = ("w_in", "w_out", "ca_wq", "ca_wk", "ca_wv", "ca_wo", "ffn_w_up", "ffn_w_down")
BIG_FULL = {"w_in": (1024, 2816), "w_out": (1024, 1024), "ca_wq": (1024, 1024), "ca_wk": (1024, 1024),
            "ca_wv": (1024, 1024), "ca_wo": (1024, 1024), "ffn_w_up": (1024, 5632), "ffn_w_down": (2816, 1024)}
G_IN, G_MID, G_UP, G_DOWN = ("w_in",), ("w_out", "ca_wq", "ca_wk", "ca_wv", "ca_wo"), ("ffn_w_up",), ("ffn_w_down",)
GROUPS = (G_IN, G_MID, G_UP, G_DOWN)
COL_SHARDED = ("w_in", "ffn_w_up")
PACK_COLS = 1024


def _big_rows(name):
    r, c = BIG_FULL[name]
    return r * c // N_DEV // PACK_COLS


def _pack_shards(w, names):
    rows = [w[n][0].T if n in COL_SHARDED else w[n][0] for n in names]
    return (rows[0] if len(rows) == 1 else jnp.concatenate(rows, axis=0)).astype(BF16)


def _unpack_gathered(gathered, names):
    out, r0 = {}, 0
    for n in names:
        rows = _big_rows(n)
        out[n] = gathered[:, r0:r0 + rows].reshape(N_DEV * rows, PACK_COLS)
        r0 += rows
    return out


def _pack_full_grads(grads, names):
    parts = [grads[n].reshape(N_DEV, _big_rows(n), PACK_COLS) for n in names]
    return parts[0] if len(parts) == 1 else jnp.concatenate(parts, axis=1)


def kernel(x, mem, mix_pre_norm, w_in, attn_sinks, hgrn_lb_logits, hgrn_out_norm, w_out, mix_post_norm, ca_pre_norm, mem_norm, ca_wq, ca_wk, ca_wv, ca_wo, ca_post_norm, ffn_pre_norm, ffn_w_up, ffn_conv_w, ffn_conv_b, ffn_w_down, ffn_post_norm, loss_target, m_mix_pre_norm, m_w_in, m_attn_sinks, m_hgrn_lb_logits, m_hgrn_out_norm, m_w_out, m_mix_post_norm, m_ca_pre_norm, m_mem_norm, m_ca_wq, m_ca_wk, m_ca_wv, m_ca_wo, m_ca_post_norm, m_ffn_pre_norm, m_ffn_w_up, m_ffn_conv_w, m_ffn_conv_b, m_ffn_w_down, m_ffn_post_norm, v_mix_pre_norm, v_w_in, v_attn_sinks, v_hgrn_lb_logits, v_hgrn_out_norm, v_w_out, v_mix_post_norm, v_ca_pre_norm, v_mem_norm, v_ca_wq, v_ca_wk, v_ca_wv, v_ca_wo, v_ca_post_norm, v_ffn_pre_norm, v_ffn_w_up, v_ffn_conv_w, v_ffn_conv_b, v_ffn_w_down, v_ffn_post_norm):
    names = ["mix_pre_norm", "w_in", "attn_sinks", "hgrn_lb_logits", "hgrn_out_norm", "w_out", "mix_post_norm",
             "ca_pre_norm", "mem_norm", "ca_wq", "ca_wk", "ca_wv", "ca_wo", "ca_post_norm", "ffn_pre_norm",
             "ffn_w_up", "ffn_conv_w", "ffn_conv_b", "ffn_w_down", "ffn_post_norm"]
    w_all = dict(zip(names, [mix_pre_norm, w_in, attn_sinks, hgrn_lb_logits, hgrn_out_norm, w_out, mix_post_norm,
                             ca_pre_norm, mem_norm, ca_wq, ca_wk, ca_wv, ca_wo, ca_post_norm, ffn_pre_norm,
                             ffn_w_up, ffn_conv_w, ffn_conv_b, ffn_w_down, ffn_post_norm]))
    m_all = dict(zip(names, [m_mix_pre_norm, m_w_in, m_attn_sinks, m_hgrn_lb_logits, m_hgrn_out_norm, m_w_out,
                             m_mix_post_norm, m_ca_pre_norm, m_mem_norm, m_ca_wq, m_ca_wk, m_ca_wv, m_ca_wo,
                             m_ca_post_norm, m_ffn_pre_norm, m_ffn_w_up, m_ffn_conv_w, m_ffn_conv_b, m_ffn_w_down,
                             m_ffn_post_norm]))
    v_all = dict(zip(names, [v_mix_pre_norm, v_w_in, v_attn_sinks, v_hgrn_lb_logits, v_hgrn_out_norm, v_w_out,
                             v_mix_post_norm, v_ca_pre_norm, v_mem_norm, v_ca_wq, v_ca_wk, v_ca_wv, v_ca_wo,
                             v_ca_post_norm, v_ffn_pre_norm, v_ffn_w_up, v_ffn_conv_w, v_ffn_conv_b, v_ffn_w_down,
                             v_ffn_post_norm]))
    dev = _index(_mesh_pos())

    w_packs = {grp: _pack_shards(w_all, grp) for grp in GROUPS}
    shard_w = D_FF * 2 // N_DEV
    conv_w_rows = _exchange_alone(_Exchange("gather", ffn_conv_w[0]), "gather_conv_w")
    conv_w_full = conv_w_rows.transpose(1, 0, 2).reshape(3, 2 * D_FF)

    received, small_pack, grad_x = _local_step(
        x[0], mem[0], loss_target[0], w_packs, conv_w_full,
        {n: w_all[n] for n in NORMS}, attn_sinks, hgrn_lb_logits, hgrn_out_norm, ffn_conv_b)

    total = _sum_parts(_exchange_alone(_Exchange("gather", small_pack), "gather_small"), "sum_small")
    cw = total[ROW_CONV_W:ROW_CONV_W + 18].reshape(2, 3, 3 * PACK_COLS)[:, :, :D_FF]
    cw = cw.transpose(1, 0, 2).reshape(3, 2 * D_FF)
    g_conv_w = lax.dynamic_slice_in_dim(cw, dev * shard_w, shard_w, axis=1)[None]
    loss, (out_g, out_d, out_m, out_v) = _adamw_small(total, g_conv_w, w_all, m_all, v_all, "adamw_small")

    for grp in GROUPS:
        r0 = 0
        for n in grp:
            rows = _big_rows(n)
            if n in COL_SHARDED:
                g = _sum_rows(received[grp], r0, rows, "sum_" + n).T[None]
                d, mo, vo = _adamw_call(w_all[n], g, m_all[n], v_all[n], "adamw_" + n)
            else:
                g, d, mo, vo = _sum_rows(received[grp], r0, rows, "adamw_" + n, wmv=(w_all[n], m_all[n], v_all[n]))
            out_g[n], out_d[n], out_m[n], out_v[n] = g, d, mo, vo
            r0 += rows

    return (loss[0, 0], grad_x[None], *[out_g[n] for n in names], *[out_d[n] for n in names],
            *[out_m[n] for n in names], *[out_v[n] for n in names])


def _local_step(x, mem, target, w_packs, conv_w, norms, sinks, lb_logits, out_norm, conv_b):
    g1, g2, g3 = norms["mix_pre_norm"], norms["mix_post_norm"], norms["ca_pre_norm"]
    g4, g5, g6, g7 = norms["mem_norm"], norms["ca_post_norm"], norms["ffn_pre_norm"], norms["ffn_post_norm"]

    h1, gathered = _norm_fwd(x, g1, "mix_norm", exchange=_Exchange("gather", w_packs[G_IN], relay=True))
    w_in_t = _unpack_gathered(gathered, G_IN)["w_in"]
    up_shard = w_packs[G_UP]
    up_rows = up_shard.shape[0]
    up_cuts = (0, up_rows // 2, 3 * up_rows // 4, up_rows)
    up_parts = [up_shard[a:b] for a, b in zip(up_cuts[:-1], up_cuts[1:])]
    z, up_0 = _mm(h1, w_in_t, mode="nt", out_dtype=BF16, name="in_proj", tn=1408,
                  exchange=_Exchange("gather", up_parts[0]))
    attn, lse, gathered = _swa_fwd(z, sinks, "swa_fwd", exchange=_Exchange("gather", w_packs[G_DOWN]))
    w_down = _unpack_gathered(gathered, G_DOWN)["ffn_w_down"]
    lb = _lower_bound(lb_logits, "lower_bound")
    rec, o_rec, states, scores, gathered = _hgrn_fwd(
        z, lb, out_norm, "hgrn_fwd", exchange=_Exchange("gather", w_packs[G_MID]))
    w_out, wq, wk, wv, wo = (_unpack_gathered(gathered, G_MID)[n] for n in G_MID)
    cat = jnp.concatenate([attn, rec], axis=1)
    mix = _mm(cat, w_out, mode="nn", out_dtype=BF16, name="out_proj")
    x1, h2, up_1 = _post_pre(x, mix, g2, g3, "mix_post", exchange=_Exchange("gather", up_parts[1]))
    mem_n = _norm_fwd(mem, g4, "mem_norm")
    q = _mm(h2, wq, mode="nn", out_dtype=BF16, name="ca_q")
    k = _mm(mem_n, wk, mode="nn", out_dtype=BF16, name="ca_k")
    v = _mm(mem_n, wv, mode="nn", out_dtype=BF16, name="ca_v")
    oc = _ca_fwd(q, k, v, "ca_fwd")
    c = _mm(oc, wo, mode="nn", out_dtype=BF16, name="ca_o")
    x2, h3, up_2 = _post_pre(x1, c, g5, g6, "ca_post", exchange=_Exchange("gather", up_parts[2]))
    w_up_t = jnp.concatenate([up_0, up_1, up_2], axis=1).reshape(-1, PACK_COLS)
    u = _mm(h3, w_up_t, mode="nt", out_dtype=F32, name="ffn_up", tn=1408, split_out=True)
    a = _glu_fwd(u, conv_w, conv_b, "glu_fwd")
    y = _mm(a, w_down, mode="nn", out_dtype=BF16, name="ffn_down", tk=2816)
    loss, dx3, dy, dg7 = _final(x2, y, g7, target, "loss_head")

    da = _mm(dy, w_down, mode="nt", out_dtype=F32, name="ffn_down_dx", tn=1408)
    d_w_down = _mm(a, dy, mode="tn", out_dtype=BF16, name="ffn_down_dw", tm=1408, tk=1024)
    dc, d_cb, d_cw, got_down = _glu_bwd(
        u, conv_w, conv_b, da, "glu_bwd",
        exchange=_Exchange("scatter", _pack_full_grads({"ffn_w_down": d_w_down}, G_DOWN)))
    du = _conv_bwd(dc, conv_w, "conv_bwd")
    d_w_up_t = _mm(du, h3, mode="tn", out_dtype=BF16, name="ffn_up_dw", tm=1408, tk=1024, split_a=True)
    dx2, dcv, dg6, dg5, got_up = _mm(
        du, w_up_t, mode="nn", out_dtype=BF16, name="ffn_up_dx", tm=1024, tk=1408, split_a=True,
        exchange=_Exchange("scatter", _pack_full_grads({"ffn_w_up": d_w_up_t}, G_UP)),
        epilogue=_norm_bwd2(dx3, x2, c, g6, g5))
    doc = _mm(dcv, wo, mode="nt", out_dtype=BF16, name="ca_o_dx")
    d_wo = _mm(oc, dcv, mode="tn", out_dtype=BF16, name="ca_o_dw", tm=1024, tk=1024)
    dq, dk, dv = _ca_bwd(q, k, v, doc, "ca_bwd")
    d_wq = _mm(h2, dq, mode="tn", out_dtype=BF16, name="ca_q_dw", tm=1024, tk=1024)
    dx1, dmix, dg3, dg2 = _mm(dq, wq, mode="nt", out_dtype=BF16, name="ca_q_dx",
                              epilogue=_norm_bwd2(dx2, x1, mix, g3, g2))
    d_wk = _mm(mem_n, dk, mode="tn", out_dtype=BF16, name="ca_k_dw", tm=1024)
    d_wv = _mm(mem_n, dv, mode="tn", out_dtype=BF16, name="ca_v_dw", tm=1024)
    dmem_k = _mm(dk, wk, mode="nt", out_dtype=F32, name="ca_k_dx")
    dmem_v = _mm(dv, wv, mode="nt", out_dtype=F32, name="ca_v_dx")
    dg4 = _gain_bwd(mem, dmem_k, dmem_v, "mem_norm_bwd")
    dcat = _mm(dmix, w_out, mode="nt", out_dtype=BF16, name="out_proj_dx")
    d_w_out = _mm(cat, dmix, mode="tn", out_dtype=BF16, name="out_proj_dw", tm=1024, tk=1024)
    mid = {"w_out": d_w_out, "ca_wq": d_wq, "ca_wk": d_wk, "ca_wv": d_wv, "ca_wo": d_wo}
    dqr, dfr, dir_, dgr, dlb, donw, got_mid = _hgrn_bwd(
        z, lb, out_norm, o_rec, states, scores, dcat, "hgrn_bwd",
        exchange=_Exchange("scatter", _pack_full_grads(mid, G_MID)))
    dq_a, dka, dkb, dva, dvb, dsinks = _swa_bwd(z, sinks, dcat, lse, "swa_bwd")
    dz = _assemble_dz(dq_a, dka, dkb, dva, dvb, dqr, dfr, dir_, dgr, "assemble_dz")
    d_w_in_t = _mm(dz, h1, mode="tn", out_dtype=BF16, name="in_proj_dw", tm=1408, tk=1024)
    dx, dg1, got_in = _mm(dz, w_in_t, mode="nn", out_dtype=BF16, name="in_proj_dx", tm=512, tk=2816,
                          exchange=_Exchange("scatter", _pack_full_grads({"w_in": d_w_in_t}, G_IN)),
                          epilogue=_norm_bwd1(dx1, x, g1))

    small_pack = _pack_small(
        (dg1, dg2, dg3, dg4, dg5, dg6, dg7), _lower_bound_bwd(lb, dlb, "lower_bound_bwd"), donw, dsinks, loss,
        d_cb, d_cw, "pack_small")
    return {G_IN: got_in, G_MID: got_mid, G_UP: got_up, G_DOWN: got_down}, small_pack, dx
```

```python
import jax
import jax.numpy as jnp
from jax import lax
from jax.experimental import pallas as pl
from jax.experimental.pallas import tpu as pltpu

F32 = jnp.float32
BF16 = jnp.bfloat16
EPS = 1e-6
N_DEV = 8
MESH_AXES = ("x", "y", "c")

ATTN_HEAD_DIM = 64
ATTN_Q_HEADS = 8
ATTN_KV_HEADS = 2
ATTN_BLOCK = 128
HGRN_HEADS = 4
HGRN_DIM = 128
HGRN_CHUNK = 64
HGRN_PAIR = 4
Z_Q, Z_F, Z_I, Z_G = 768, 1280, 1792, 2304
HGRN_LEVELS = (32, 16, 8, 4, 2, 1)
CA_HEADS = 4
CA_HEAD_DIM = 256
D_FF = 2816

ADAM_LR = 0.001
ADAM_B1 = 0.9
ADAM_B2 = 0.999
ADAM_EPS = 1e-08
ADAM_WD = 0.01
ADAM_STEP = 10

VMEM_LIMIT = 58 << 20
EPILOGUE_ROWS = 256
LANE = 128

NT = (((1,), (1,)), ((), ()))
TN = (((0,), (0,)), ((), ()))


def _params(*sem):
    return pltpu.CompilerParams(dimension_semantics=sem, vmem_limit_bytes=VMEM_LIMIT)


def _tile(n, cap):
    if n <= cap:
        return n
    best = 0
    for t in range(LANE, cap + 1, LANE):
        if n % t == 0:
            best = t
    assert best, (n, cap)
    return best


def _dot(a, b, dims=None):
    if dims is None:
        return jnp.dot(a, b, preferred_element_type=F32)
    return lax.dot_general(a, b, dims, preferred_element_type=F32)


def _bf(x):
    return x.astype(BF16)


def _sigmoid(x):
    return 1.0 / (1.0 + jnp.exp(-x))


def _rms(x):
    r = lax.rsqrt(jnp.mean(x * x, axis=-1, keepdims=True) + EPS)
    return x * r, r


def _rms_bwd(dxh, xh, r):
    return r * (dxh - xh * jnp.mean(dxh * xh, axis=-1, keepdims=True))


def _mm(a, b, *, mode, out_dtype, name, tm=1024, tn=1024, tk=1024, split_a=False, split_b=False, split_out=False,
        exchange=None, epilogue=None):
    def dims(arr, split):
        if split:
            return arr.shape[1], 2 * arr.shape[2]
        return arr.shape

    ar, ac = dims(a, split_a)
    br, bc = dims(b, split_b)
    if mode == "nn":
        M, K, N = ar, ac, bc
        assert br == K
    elif mode == "nt":
        M, K, N = ar, ac, br
        assert bc == K
    else:
        K, M, N = ar, ac, bc
        assert br == K
    a_cols_half = ac // 2 if split_a else None
    b_cols_half = bc // 2 if split_b else None
    tm = _tile(M, tm)
    tn = _tile((N // 2) if (split_out or (split_b and mode != "nt")) else N, tn)
    tk = _tile((K // 2) if ((split_a and mode != "tn") or (split_b and mode == "nt")) else K, tk)
    if split_a and mode == "tn":
        tm = _tile(M // 2, tm)
    gm, gn, gk = M // tm, N // tn, K // tk
    a_bytes, b_bytes = a.size * a.dtype.itemsize, b.size * b.dtype.itemsize
    rows_outer = gk > 1 or a_bytes + gm * b_bytes <= gn * a_bytes + b_bytes
    grid = (gm, gn, gk) if rows_outer else (gn, gm, gk)

    def spec(split, half, blk, rc):
        def imap(p, q, k):
            r, c = rc(*((p, q) if rows_outer else (q, p)), k)
            if not split:
                return (r, c)
            per_half = half // blk[1]
            return (c // per_half, r, c % per_half)

        return pl.BlockSpec(((None,) + blk) if split else blk, imap)

    if mode == "nn":
        a_spec = spec(split_a, a_cols_half, (tm, tk), lambda i, j, k: (i, k))
        b_spec = spec(split_b, b_cols_half, (tk, tn), lambda i, j, k: (k, j))
        dn = None
    elif mode == "nt":
        a_spec = spec(split_a, a_cols_half, (tm, tk), lambda i, j, k: (i, k))
        b_spec = spec(split_b, b_cols_half, (tn, tk), lambda i, j, k: (j, k))
        dn = NT
    else:
        a_spec = spec(split_a, a_cols_half, (tk, tm), lambda i, j, k: (k, i))
        b_spec = spec(split_b, b_cols_half, (tk, tn), lambda i, j, k: (k, j))
        dn = TN
    o_spec = spec(split_out, N // 2 if split_out else None, (tm, tn), lambda i, j, k: (i, j))
    out_shape = (2, M, N // 2) if split_out else (M, N)

    in_specs, out_specs, args = [a_spec, b_spec], [o_spec], (a, b)
    out_shapes = [jax.ShapeDtypeStruct(out_shape, out_dtype)]
    semantics = ("parallel", "parallel", "arbitrary")

    def store(result, extra, outs):
        outs[0][...] = result[...].astype(outs[0].dtype)

    if epilogue is not None:
        assert gn == 1 and not split_out
        n_vec = epilogue.n_out_vecs
        row = pl.BlockSpec((tm, N), lambda p, q, k: ((p if rows_outer else q), 0))
        vec = pl.BlockSpec((1, N), lambda p, q, k: (0, 0))
        in_specs += [row] * len(epilogue.rows) + [vec] * len(epilogue.vecs)
        args += tuple(epilogue.rows) + tuple(epilogue.vecs)
        out_specs = [row] * len(epilogue.out_rows) + [vec] * n_vec
        out_shapes = ([jax.ShapeDtypeStruct((M, N), dt) for dt in epilogue.out_rows]
                      + [jax.ShapeDtypeStruct((1, N), F32)] * n_vec)
        semantics = ("arbitrary",) * 3

        def store(result, extra, outs):
            n_rows, n_out_rows, sub = len(epilogue.rows), len(epilogue.out_rows), min(EPILOGUE_ROWS, tm)
            for r in range(0, tm, sub):
                rows = pl.ds(r, sub)
                epilogue.fn(result[r:r + sub], *[ref.at[rows] for ref in extra[:n_rows]], *extra[n_rows:],
                            *[ref.at[rows] for ref in outs[:n_out_rows]], *outs[n_out_rows:])

    n_extra = len(in_specs) - 2
    n_out = len(out_specs)

    def body(a_ref, b_ref, *refs):
        extra, outs, scratch_refs = refs[:n_extra], refs[n_extra:n_extra + n_out], refs[n_extra + n_out:]
        k = pl.program_id(2)
        if epilogue is not None:
            @pl.when((pl.program_id(0) == 0) & (pl.program_id(1) == 0) & (k == 0))
            def _():
                for ref in outs[n_out - epilogue.n_out_vecs:]:
                    ref[...] = jnp.zeros_like(ref)

        if gk == 1:
            store(_dot(_bf(a_ref[...]), _bf(b_ref[...]), dn), extra, outs)
            return
        acc_ref = scratch_refs[0]

        @pl.when(k == 0)
        def _():
            acc_ref[...] = jnp.zeros_like(acc_ref)

        acc_ref[...] += _dot(_bf(a_ref[...]), _bf(b_ref[...]), dn)

        @pl.when(k == gk - 1)
        def _():
            store(acc_ref, extra, outs)

    out = _hosted_call(
        body, name=name, grid=grid, in_specs=in_specs, out_specs=out_specs, out_shape=out_shapes,
        scratch=[] if gk == 1 else [pltpu.VMEM((tm, tn), F32)], args=args, semantics=semantics, exchange=exchange)
    return out[0] if (exchange is None and epilogue is None) else out


ROWS = 512


def _row_spec(tr, cols):
    return pl.BlockSpec((tr, cols), lambda i: (i, 0))


def _vec_spec(cols):
    return pl.BlockSpec((1, cols), lambda i: (0, 0))


def _norm_fwd(x, g, name, exchange=None):
    T, Dm = x.shape
    tr = min(ROWS, T)

    def body(x_ref, g_ref, h_ref):
        xh, _ = _rms(x_ref[...])
        h_ref[...] = (xh * g_ref[...]).astype(h_ref.dtype)

    out = _hosted_call(
        body, name=name, grid=(T // tr,), in_specs=[_row_spec(tr, Dm), _vec_spec(Dm)], out_specs=[_row_spec(tr, Dm)],
        out_shape=[jax.ShapeDtypeStruct((T, Dm), BF16)], scratch=[], args=(x, g), semantics=("parallel",),
        exchange=exchange)
    return out[0] if exchange is None else out


def _post_pre(x, m, g_post, g_pre, name, exchange=None):
    T, Dm = x.shape
    tr = min(ROWS, T)

    def body(x_ref, m_ref, gp_ref, gn_ref, xo_ref, h_ref):
        mh, _ = _rms(m_ref[...].astype(F32))
        xn = x_ref[...] + mh * gp_ref[...]
        xo_ref[...] = xn
        xh, _ = _rms(xn)
        h_ref[...] = (xh * gn_ref[...]).astype(h_ref.dtype)

    return _hosted_call(
        body, name=name, grid=(T // tr,),
        in_specs=[_row_spec(tr, Dm), _row_spec(tr, Dm), _vec_spec(Dm), _vec_spec(Dm)],
        out_specs=[_row_spec(tr, Dm), _row_spec(tr, Dm)],
        out_shape=[jax.ShapeDtypeStruct((T, Dm), F32), jax.ShapeDtypeStruct((T, Dm), BF16)],
        scratch=[], args=(x, m, g_post, g_pre), semantics=("parallel",), exchange=exchange)


def _final(x2, y, g_post, target, name):
    T, Dm = x2.shape
    tr = min(ROWS, T)

    def body(x_ref, y_ref, g_ref, t_ref, loss_ref, dx_ref, dy_ref, dg_ref):
        @pl.when(pl.program_id(0) == 0)
        def _():
            loss_ref[...] = jnp.zeros_like(loss_ref)
            dg_ref[...] = jnp.zeros_like(dg_ref)

        g = g_ref[...]
        yh, r = _rms(y_ref[...].astype(F32))
        d = x_ref[...] + yh * g - t_ref[...]
        loss_ref[...] += jnp.zeros((1, LANE), F32) + 0.5 * jnp.sum(jnp.mean(d * d, axis=-1, keepdims=True))
        dx = d * (1.0 / Dm)
        dx_ref[...] = dx
        dy_ref[...] = _rms_bwd(dx * g, yh, r).astype(dy_ref.dtype)
        dg_ref[...] += jnp.sum(dx * yh, axis=0, keepdims=True)

    return pl.pallas_call(
        body, name=name, grid=(T // tr,),
        in_specs=[_row_spec(tr, Dm), _row_spec(tr, Dm), _vec_spec(Dm), _row_spec(tr, Dm)],
        out_specs=[_vec_spec(LANE), _row_spec(tr, Dm), _row_spec(tr, Dm), _vec_spec(Dm)],
        out_shape=[jax.ShapeDtypeStruct((1, LANE), F32), jax.ShapeDtypeStruct((T, Dm), F32),
                   jax.ShapeDtypeStruct((T, Dm), BF16), jax.ShapeDtypeStruct((1, Dm), F32)],
        compiler_params=_params("arbitrary"),
    )(x2, y, g_post, target)


class _RowEpilogue:
    def __init__(self, fn, rows, vecs, out_rows, n_out_vecs):
        self.fn, self.rows, self.vecs, self.out_rows, self.n_out_vecs = fn, rows, vecs, out_rows, n_out_vecs


def _norm_bwd2(dx_cur, x_prev, m_prev, g_pre, g_post):
    def fn(dh, dx_ref, x_ref, m_ref, gn_ref, gp_ref, dxo_ref, dm_ref, dgn_ref, dgp_ref):
        xh, r = _rms(x_ref[...])
        dx = dx_ref[...] + _rms_bwd(dh * gn_ref[...], xh, r)
        dxo_ref[...] = dx
        dgn_ref[...] += jnp.sum(dh * xh, axis=0, keepdims=True)
        mh, rm = _rms(m_ref[...].astype(F32))
        dm_ref[...] = _rms_bwd(dx * gp_ref[...], mh, rm).astype(dm_ref.dtype)
        dgp_ref[...] += jnp.sum(dx * mh, axis=0, keepdims=True)

    return _RowEpilogue(fn, [dx_cur, x_prev, m_prev], [g_pre, g_post], [F32, BF16], 2)


def _norm_bwd1(dx_cur, x_prev, g_pre):
    def fn(dh, dx_ref, x_ref, gn_ref, dxo_ref, dgn_ref):
        xh, r = _rms(x_ref[...])
        dxo_ref[...] = dx_ref[...] + _rms_bwd(dh * gn_ref[...], xh, r)
        dgn_ref[...] += jnp.sum(dh * xh, axis=0, keepdims=True)

    return _RowEpilogue(fn, [dx_cur, x_prev], [g_pre], [F32], 1)


def _gain_bwd(x, dh_a, dh_b, name):
    T, Dm = x.shape

    def body(x_ref, a_ref, b_ref, dg_ref):
        xh, _ = _rms(x_ref[...])
        dg_ref[...] = jnp.sum((a_ref[...] + b_ref[...]) * xh, axis=0, keepdims=True)

    return pl.pallas_call(
        body, name=name, grid=(1,), in_specs=[_row_spec(T, Dm)] * 3, out_specs=_vec_spec(Dm),
        out_shape=jax.ShapeDtypeStruct((1, Dm), F32), compiler_params=_params("arbitrary"),
    )(x, dh_a, dh_b)


ATTN_GROUP = ATTN_Q_HEADS // ATTN_KV_HEADS
ASSEMBLE_ROWS = 1024


def _swa_mask(n):
    rows = ATTN_GROUP * ATTN_BLOCK
    row = lax.broadcasted_iota(jnp.int32, (rows, 2 * ATTN_BLOCK), 0) & (ATTN_BLOCK - 1)
    col = lax.broadcasted_iota(jnp.int32, (rows, 2 * ATTN_BLOCK), 1)
    diff = row + ATTN_BLOCK - col
    return (diff >= 0) & (diff < ATTN_BLOCK) & ((col >= ATTN_BLOCK) | (n > 0))


def _swa_rows(ref, hk, dtype):
    hd = ATTN_HEAD_DIM
    return jnp.concatenate(
        [ref[:, hd * (hk * ATTN_GROUP + g):hd * (hk * ATTN_GROUP + g + 1)].astype(dtype) for g in range(ATTN_GROUP)],
        axis=0)


def _swa_per_row(vals):
    seg = lax.broadcasted_iota(jnp.int32, (ATTN_GROUP * ATTN_BLOCK, 1), 0) // ATTN_BLOCK
    col = jnp.zeros((ATTN_GROUP * ATTN_BLOCK, 1), F32)
    for g, val in enumerate(vals):
        col = jnp.where(seg == g, val, col)
    return col


def _swa_specs():
    blk = ATTN_BLOCK
    prev = lambda n: jnp.maximum(n - 1, 0)
    return [
        pl.BlockSpec(memory_space=pltpu.SMEM),
        pl.BlockSpec((blk, 512), lambda n: (n, 0)),
        pl.BlockSpec((blk, 128), lambda n: (prev(n), 4)),
        pl.BlockSpec((blk, 128), lambda n: (n, 4)),
        pl.BlockSpec((blk, 128), lambda n: (prev(n), 5)),
        pl.BlockSpec((blk, 128), lambda n: (n, 5)),
    ]


def _swa_fwd(z, sinks, name, exchange=None):
    T = z.shape[0]
    blk, hd = ATTN_BLOCK, ATTN_HEAD_DIM
    scale = hd ** -0.5

    def body(sink_ref, q_ref, kp_ref, kc_ref, vp_ref, vc_ref, o_ref, lse_ref):
        allowed = _swa_mask(pl.program_id(0))
        hks = range(ATTN_KV_HEADS)
        kss = [slice(hd * hk, hd * hk + hd) for hk in hks]
        k = [_bf(jnp.concatenate([kp_ref[:, ks], kc_ref[:, ks]], axis=0)) for ks in kss]
        v = [_bf(jnp.concatenate([vp_ref[:, ks], vc_ref[:, ks]], axis=0)) for ks in kss]
        s = [jnp.where(allowed, _dot(_swa_rows(q_ref, hk, BF16), k[hk], NT) * scale, -1e30) for hk in hks]
        sink = [_swa_per_row([sink_ref[0, hk * ATTN_GROUP + g] for g in range(ATTN_GROUP)]) for hk in hks]
        m = [jnp.maximum(jnp.max(s[hk], axis=-1, keepdims=True), sink[hk]) for hk in hks]
        p = [jnp.exp(s[hk] - m[hk]) for hk in hks]
        l = [jnp.sum(p[hk], axis=-1, keepdims=True) + jnp.exp(sink[hk] - m[hk]) for hk in hks]
        o = [_dot(_bf(p[hk] / l[hk]), v[hk]).astype(o_ref.dtype) for hk in hks]
        for hk in hks:
            lse = m[hk] + jnp.log(l[hk])
            for g in range(ATTN_GROUP):
                h = hk * ATTN_GROUP + g
                o_ref[:, hd * h:hd * (h + 1)] = o[hk][blk * g:blk * (g + 1)]
                lse_ref[:, h:h + 1] = lse[blk * g:blk * (g + 1)]

    return _hosted_call(
        body, name=name, grid=(T // blk,), in_specs=_swa_specs(),
        out_specs=[pl.BlockSpec((blk, 512), lambda n: (n, 0)), pl.BlockSpec((blk, ATTN_Q_HEADS), lambda n: (n, 0))],
        out_shape=[jax.ShapeDtypeStruct((T, 512), BF16), jax.ShapeDtypeStruct((T, ATTN_Q_HEADS), F32)],
        scratch=[], args=(sinks, z, z, z, z, z), semantics=("parallel",), exchange=exchange)


def _swa_bwd(z, sinks, dcat, lse, name):
    T = z.shape[0]
    blk, hd = ATTN_BLOCK, ATTN_HEAD_DIM
    scale = hd ** -0.5
    group = ATTN_Q_HEADS // ATTN_KV_HEADS

    def body(sink_ref, q_ref, kp_ref, kc_ref, vp_ref, vc_ref, do_ref, lse_ref,
             dq_ref, dka_ref, dkb_ref, dva_ref, dvb_ref, dsink_ref):
        @pl.when(pl.program_id(0) == 0)
        def _():
            dsink_ref[...] = jnp.zeros_like(dsink_ref)

        allowed = _swa_mask(pl.program_id(0))
        lane = lax.broadcasted_iota(jnp.int32, (1, ATTN_Q_HEADS), 1)
        dsink = jnp.zeros((1, ATTN_Q_HEADS), F32)
        hks = range(ATTN_KV_HEADS)
        kss = [slice(hd * hk, hd * hk + hd) for hk in hks]
        k = [_bf(jnp.concatenate([kp_ref[:, ks], kc_ref[:, ks]], axis=0)) for ks in kss]
        v = [_bf(jnp.concatenate([vp_ref[:, ks], vc_ref[:, ks]], axis=0)) for ks in kss]
        qs = [_swa_rows(q_ref, hk, BF16) for hk in hks]
        dos = [_swa_rows(do_ref, hk, BF16) for hk in hks]
        lse = [jnp.concatenate([lse_ref[:, hk * group + g:hk * group + g + 1] for g in range(group)], axis=0)
               for hk in hks]
        s = [_dot(qs[hk], k[hk], NT) * scale for hk in hks]
        dp = [_dot(dos[hk], v[hk], NT) for hk in hks]
        p = [jnp.where(allowed, jnp.exp(jnp.where(allowed, s[hk], -1e30) - lse[hk]), 0.0) for hk in hks]
        delta = [jnp.sum(p[hk] * dp[hk], axis=-1, keepdims=True) for hk in hks]
        ds = [_bf(p[hk] * (dp[hk] - delta[hk]) * scale) for hk in hks]
        dq = [_dot(ds[hk], k[hk]).astype(dq_ref.dtype) for hk in hks]
        dk = [_dot(ds[hk], qs[hk], TN) for hk in hks]
        dv = [_dot(_bf(p[hk]), dos[hk], TN) for hk in hks]
        for hk in hks:
            sink = _swa_per_row([sink_ref[0, hk * group + g] for g in range(group)])
            sink_part = jnp.exp(sink - lse[hk]) * delta[hk]
            for g in range(group):
                h = hk * group + g
                dq_ref[:, hd * h:hd * (h + 1)] = dq[hk][blk * g:blk * (g + 1)]
                dsink = dsink + jnp.where(lane == h, -jnp.sum(sink_part[blk * g:blk * (g + 1)]), 0.0)
            dkb_ref[:, kss[hk]] = dk[hk][:blk]
            dka_ref[:, kss[hk]] = dk[hk][blk:]
            dvb_ref[:, kss[hk]] = dv[hk][:blk]
            dva_ref[:, kss[hk]] = dv[hk][blk:]
        dsink_ref[...] += dsink

    kv_out = pl.BlockSpec((blk, 128), lambda n: (n, 0))
    return pl.pallas_call(
        body, name=name, grid=(T // blk,),
        in_specs=_swa_specs() + [pl.BlockSpec((blk, 512), lambda n: (n, 0)),
                                 pl.BlockSpec((blk, ATTN_Q_HEADS), lambda n: (n, 0))],
        out_specs=[pl.BlockSpec((blk, 512), lambda n: (n, 0)), kv_out, kv_out, kv_out, kv_out,
                   pl.BlockSpec((1, ATTN_Q_HEADS), lambda n: (0, 0))],
        out_shape=[jax.ShapeDtypeStruct((T, 512), BF16)] + [jax.ShapeDtypeStruct((T, 128), F32)] * 4
        + [jax.ShapeDtypeStruct((1, ATTN_Q_HEADS), F32)],
        compiler_params=_params("arbitrary"),
    )(sinks, z, z, z, z, z, dcat, lse)


def _assemble_dz(dq_a, dka, dkb, dva, dvb, dqr, dfr, dir_, dgr, name):
    T = dq_a.shape[0]
    blk = ATTN_BLOCK
    rows = min(ASSEMBLE_ROWS, T)
    nb, per = T // rows, rows // blk

    def body(dq_ref, dka_ref, dkb_ref, dkn_ref, dva_ref, dvb_ref, dvn_ref, dqr_ref, dfr_ref, dir_ref, dgr_ref, o_ref):
        has_next = pl.program_id(0) < nb - 1

        def with_next(a_ref, b_ref, n_ref):
            after = jnp.where(has_next, n_ref[...], 0.0)
            shifted = after if per == 1 else jnp.concatenate([b_ref[blk:, :], after], axis=0)
            return (a_ref[...] + shifted).astype(o_ref.dtype)

        o_ref[:, 0:512] = dq_ref[...]
        o_ref[:, 512:640] = with_next(dka_ref, dkb_ref, dkn_ref)
        o_ref[:, 640:768] = with_next(dva_ref, dvb_ref, dvn_ref)
        o_ref[:, 768:1280] = dqr_ref[...]
        o_ref[:, 1280:1792] = dfr_ref[...]
        o_ref[:, 1792:2304] = dir_ref[...]
        o_ref[:, 2304:2816] = dgr_ref[...]

    cur = lambda w: pl.BlockSpec((rows, w), lambda n: (n, 0))
    nxt = pl.BlockSpec((blk, 128), lambda n: (jnp.minimum((n + 1) * per, T // blk - 1), 0))
    return pl.pallas_call(
        body, name=name, grid=(nb,),
        in_specs=[cur(512), cur(128), cur(128), nxt, cur(128), cur(128), nxt, cur(512), cur(512), cur(512), cur(512)],
        out_specs=pl.BlockSpec((rows, 2816), lambda n: (n, 0)),
        out_shape=jax.ShapeDtypeStruct((T, 2816), BF16), compiler_params=_params("parallel"),
    )(dq_a, dka, dkb, dkb, dva, dvb, dvb, dqr, dfr, dir_, dgr)


HGRN_ROWS = 512


def _hgrn_consts():
    c = HGRN_CHUNK
    r = lax.broadcasted_iota(jnp.int32, (c, c), 0)
    s = lax.broadcasted_iota(jnp.int32, (c, c), 1)
    rcol = lax.broadcasted_iota(jnp.int32, (c, 1), 0)
    same_block, upper = [], []
    for m in HGRN_LEVELS:
        same_block.append((r & ~(2 * m - 1)) == (s & ~(2 * m - 1)))
        upper.append((rcol & (2 * m - 1)) >= m)
    cum_mat = jnp.where(s <= r, 1.0, 0.0).astype(BF16)
    rev_mat = jnp.where(s >= r, 1.0, 0.0).astype(BF16)
    return cum_mat, rev_mat, r == s, same_block, upper, rcol & 3, s == r - 1


def _hgrn_level_decay(g, b, m, pos4):
    c = HGRN_CHUNK
    if m == 1:
        return jnp.exp(jnp.where((pos4 & 1) == 1, g, 0.0))
    if m == 2:
        after, before = pltpu.roll(g, c - 1, 0), pltpu.roll(g, 1, 0)
        return jnp.exp(jnp.where(pos4 == 0, after, jnp.where(pos4 == 1, 0.0, jnp.where(pos4 == 2, g, g + before))))
    b3 = b.reshape(c // (2 * m), 2 * m, HGRN_DIM)
    bref = jnp.broadcast_to(b3[:, m - 1:m, :], b3.shape).reshape(c, HGRN_DIM)
    return jnp.exp(-jnp.abs(b - bref))


def _split3(x):
    hi = _bf(x)
    r1 = x - hi.astype(F32)
    mid = _bf(r1)
    lo = _bf(r1 - mid.astype(F32))
    return jnp.concatenate([hi, mid, lo], axis=1)


def _dot_hilo(a, b):
    r, c = a.shape[0], b.shape[1]
    a_hi, b_hi = _bf(a), _bf(b)
    a2 = jnp.concatenate([a_hi, _bf(a - a_hi.astype(F32))], axis=0)
    b2 = jnp.concatenate([b_hi, _bf(b - b_hi.astype(F32))], axis=1)
    y = _dot(a2, b2)
    return y[:r, :c] + y[:r, c:] + y[r:, :c]


def _fold3(y):
    w = y.shape[1] // 3
    return y[:, :w] + y[:, w:2 * w] + y[:, 2 * w:]


def _hgrn_gates(qr, fr, lb):
    sq = _sigmoid(qr)
    q = qr * sq * (HGRN_DIM ** -0.5)
    sf = _sigmoid(fr)
    f = lb + (1.0 - lb) * sf
    k = (1.0 - lb) * _sigmoid(-fr)
    return q, sq, sf, f, k, jnp.log(f)


def _hgrn_intra(q, k, g, b, consts, scores=True):
    _, _, eye, same_block, upper, pos4, below = consts
    heads = range(len(q))
    a = None
    if scores:
        a = [jnp.where(eye, jnp.sum(q[hh] * k[hh], axis=1, keepdims=True), 0.0) for hh in heads]
    saved = [[] for _ in heads]
    for i, m in enumerate(HGRN_LEVELS):
        up = upper[i]
        e = [_hgrn_level_decay(g[hh], b[hh], m, pos4) for hh in heads]
        qt = [jnp.where(up, q[hh] * e[hh], 0.0) for hh in heads]
        kt = [jnp.where(up, 0.0, k[hh] * e[hh]) for hh in heads]
        for hh in heads:
            saved[hh].append((e[hh], qt[hh], kt[hh]))
        if not scores:
            continue
        if m == 1:
            for hh in heads:
                pair = jnp.sum(qt[hh] * pltpu.roll(kt[hh], 1, 0), axis=1, keepdims=True)
                a[hh] = a[hh] + jnp.where(below, pair, 0.0)
            continue
        p = [_dot(_bf(qt[hh]), _bf(kt[hh]), NT) for hh in heads]
        for hh in heads:
            a[hh] = a[hh] + jnp.where(same_block[i], p[hh], 0.0)
    return a, saved


def _hgrn_specs(tb, nb, rev):
    tmap = (lambda t: nb - 1 - t) if rev else (lambda t: t)
    assert HGRN_PAIR == HGRN_HEADS
    return [pl.BlockSpec((tb, 2816), lambda h, t: (tmap(t), 0)),
            pl.BlockSpec((1, HGRN_PAIR * HGRN_DIM), lambda h, t: (0, h)),
            pl.BlockSpec((1, HGRN_DIM), lambda h, t: (0, 0))]


def _hgrn_z(z_ref, sl, base, head):
    return z_ref[sl, base + HGRN_DIM * head:base + HGRN_DIM * (head + 1)].astype(F32)


def _hgrn_fwd(z, lb, onw, name, exchange=None):
    T = z.shape[0]
    tb = min(HGRN_ROWS, T)
    nb, c, nc = T // tb, HGRN_CHUNK, min(HGRN_ROWS, T) // HGRN_CHUNK

    def body(z_ref, lb_ref, onw_ref, rec_ref, o_ref, st_ref, a_ref, state):
        @pl.when(pl.program_id(1) == 0)
        def _():
            state[...] = jnp.zeros_like(state)

        consts = _hgrn_consts()
        lbv = lb_ref[...]
        onwv = onw_ref[...]

        def chunk(ci, carry):
            sl = pl.ds(pl.multiple_of(ci * c, c), c)
            heads = range(HGRN_PAIR)
            lss = [slice(HGRN_DIM * hh, HGRN_DIM * (hh + 1)) for hh in heads]
            gates = [_hgrn_gates(_hgrn_z(z_ref, sl, Z_Q, hh), _hgrn_z(z_ref, sl, Z_F, hh), lbv[:, lss[hh]])
                     for hh in heads]
            q, k, g = [t[0] for t in gates], [t[4] for t in gates], [t[5] for t in gates]
            v = [_bf(_hgrn_z(z_ref, sl, Z_I, hh)) for hh in heads]
            b = [_fold3(_dot(consts[0], _split3(g[hh]))) for hh in heads]
            a, _ = _hgrn_intra(q, k, g, b, consts)
            st = [state[hh] for hh in heads]
            for hh in heads:
                st_ref[hh, ci] = st[hh]
            bl = [b[hh][c - 1:c, :] for hh in heads]
            o_state = [_dot(_bf(q[hh] * jnp.exp(b[hh])), _bf(st[hh]), NT) for hh in heads]
            kv = [_dot(v[hh], _bf(k[hh] * jnp.exp(bl[hh] - b[hh])), TN) for hh in heads]
            a = [_bf(a[hh]) for hh in heads]
            o = [_dot(a[hh], v[hh]) + o_state[hh] for hh in heads]
            for hh in heads:
                a_ref[sl, c * hh:c * (hh + 1)] = a[hh]
                state[hh] = st[hh] * jnp.exp(bl[hh]) + kv[hh]
                o_ref[sl, lss[hh]] = o[hh]
                oh, _ = _rms(o[hh])
                gr = _hgrn_z(z_ref, sl, Z_G, hh)
                rec_ref[sl, lss[hh]] = (oh * onwv * (gr * _sigmoid(gr))).astype(rec_ref.dtype)
            return carry

        lax.fori_loop(0, nc, chunk, 0)

    in_specs = _hgrn_specs(tb, nb, False)
    out_blk = pl.BlockSpec((tb, HGRN_PAIR * HGRN_DIM), lambda h, t: (t, h))
    return _hosted_call(
        body, name=name, grid=(HGRN_HEADS // HGRN_PAIR, nb), in_specs=in_specs,
        out_specs=[out_blk, out_blk, pl.BlockSpec((HGRN_PAIR, nc, HGRN_DIM, HGRN_DIM), lambda h, t: (h, t, 0, 0)),
                   pl.BlockSpec((tb, HGRN_PAIR * c), lambda h, t: (t, h))],
        out_shape=[jax.ShapeDtypeStruct((T, 512), BF16), jax.ShapeDtypeStruct((T, 512), F32),
                   jax.ShapeDtypeStruct((HGRN_HEADS, T // c, HGRN_DIM, HGRN_DIM), F32),
                   jax.ShapeDtypeStruct((T, HGRN_HEADS * c), BF16)],
        scratch=[pltpu.VMEM((HGRN_PAIR, HGRN_DIM, HGRN_DIM), F32)], args=(z, lb, onw),
        semantics=("parallel", "arbitrary"), exchange=exchange)


def _hgrn_bwd(z, lb, onw, o, states, scores, dcat, name, exchange=None):
    T = z.shape[0]
    tb = min(HGRN_ROWS, T)
    nb, c, nc = T // tb, HGRN_CHUNK, min(HGRN_ROWS, T) // HGRN_CHUNK

    def body(z_ref, lb_ref, onw_ref, o_ref, st_ref, drec_ref, a_ref,
             dqr_ref, dfr_ref, dir_ref, dgr_ref, dlb_ref, donw_ref, dstate):
        @pl.when(pl.program_id(1) == 0)
        def _():
            dstate[...] = jnp.zeros_like(dstate)
            dlb_ref[...] = jnp.zeros_like(dlb_ref)

        @pl.when((pl.program_id(0) == 0) & (pl.program_id(1) == 0))
        def _():
            donw_ref[...] = jnp.zeros_like(donw_ref)

        consts = _hgrn_consts()
        rev_mat, eye, same_block, upper = consts[1:5]
        below = consts[6]
        lbv = lb_ref[...]
        onwv = onw_ref[...]
        last = lax.broadcasted_iota(jnp.int32, (c, 1), 0) == c - 1

        def chunk(i, carry):
            ci = nc - 1 - i
            sl = pl.ds(pl.multiple_of(ci * c, c), c)
            hs = range(HGRN_PAIR)
            lss = [slice(HGRN_DIM * hh, HGRN_DIM * (hh + 1)) for hh in hs]
            qr = [_hgrn_z(z_ref, sl, Z_Q, hh) for hh in hs]
            gates = [_hgrn_gates(qr[hh], _hgrn_z(z_ref, sl, Z_F, hh), lbv[:, lss[hh]]) for hh in hs]
            q, sq, sf, f, k, g = ([t[j] for t in gates] for j in range(6))
            v = [_bf(_hgrn_z(z_ref, sl, Z_I, hh)) for hh in hs]
            b = [_fold3(_dot(consts[0], _split3(g[hh]))) for hh in hs]
            _, saved = _hgrn_intra(q, k, g, b, consts, scores=False)
            a = [a_ref[sl, c * hh:c * (hh + 1)] for hh in hs]
            st = [st_ref[hh, ci] for hh in hs]
            dst = [dstate[hh] for hh in hs]

            gr = [_hgrn_z(z_ref, sl, Z_G, hh) for hh in hs]
            sg = [_sigmoid(gr[hh]) for hh in hs]
            norm = [_rms(o_ref[sl, ls]) for ls in lss]
            oh, r = [t[0] for t in norm], [t[1] for t in norm]
            drec = [drec_ref[sl, ls].astype(F32) for ls in lss]
            don = [drec[hh] * (gr[hh] * sg[hh]) for hh in hs]
            do = [_bf(_rms_bwd(don[hh] * onwv, oh[hh], r[hh])) for hh in hs]
            donw = jnp.sum(don[0] * oh[0], axis=0, keepdims=True)
            for hh in hs:
                dgr_ref[sl, lss[hh]] = (drec[hh] * oh[hh] * onwv
                                        * (sg[hh] * (1.0 + gr[hh] * (1.0 - sg[hh])))).astype(dgr_ref.dtype)
                if hh:
                    donw = donw + jnp.sum(don[hh] * oh[hh], axis=0, keepdims=True)
            donw_ref[...] += donw

            eb = [jnp.exp(b[hh]) for hh in hs]
            bl = [b[hh][c - 1:c, :] for hh in hs]
            ebl = [jnp.exp(bl[hh]) for hh in hs]
            ekb = [jnp.exp(bl[hh] - b[hh]) for hh in hs]
            qe = [q[hh] * eb[hh] for hh in hs]
            ke = [k[hh] * ekb[hh] for hh in hs]
            da = [_dot(do[hh], v[hh], NT) for hh in hs]
            dat = [_dot(v[hh], do[hh], NT) for hh in hs]
            dqe = [_dot(do[hh], _bf(st[hh])) for hh in hs]
            dke = [_dot(v[hh], _bf(dst[hh])) for hh in hs]
            dv_a = [_dot(a[hh], do[hh], TN) for hh in hs]
            dv_s = [_dot(_bf(ke[hh]), _bf(dst[hh]), NT) for hh in hs]
            dst_in = [_dot(do[hh], _bf(qe[hh]), TN) for hh in hs]
            dad = [jnp.sum(jnp.where(eye, da[hh], 0.0), axis=1, keepdims=True) for hh in hs]
            dq = [dqe[hh] * eb[hh] + dad[hh] * k[hh] for hh in hs]
            dk = [dke[hh] * ekb[hh] + dad[hh] * q[hh] for hh in hs]
            db_last = [jnp.sum(dke[hh] * ke[hh], axis=0, keepdims=True)
                       + jnp.sum(dst[hh] * st[hh], axis=0, keepdims=True) * ebl[hh] for hh in hs]
            for hh in hs:
                dstate[hh] = dst[hh] * ebl[hh] + dst_in[hh]
                dir_ref[sl, lss[hh]] = (dv_a[hh] + dv_s[hh]).astype(dir_ref.dtype)
            for lvl, m in enumerate(HGRN_LEVELS):
                if m == 1:
                    pair = [jnp.sum(jnp.where(below, da[hh], 0.0), axis=1, keepdims=True) for hh in hs]
                    xq = [pair[hh] * pltpu.roll(saved[hh][lvl][2], 1, 0) for hh in hs]
                    xk = [pltpu.roll(pair[hh] * saved[hh][lvl][1], c - 1, 0) for hh in hs]
                else:
                    xq = [_dot_hilo(jnp.where(same_block[lvl], da[hh], 0.0), saved[hh][lvl][2]) for hh in hs]
                    xk = [_dot_hilo(jnp.where(same_block[lvl], dat[hh], 0.0), saved[hh][lvl][1]) for hh in hs]
                for hh in hs:
                    e = saved[hh][lvl][0]
                    dq[hh] = dq[hh] + jnp.where(upper[lvl], xq[hh] * e, 0.0)
                    dk[hh] = dk[hh] + jnp.where(upper[lvl], 0.0, xk[hh] * e)
            db = [q[hh] * dq[hh] - k[hh] * dk[hh] + jnp.where(last, db_last[hh], 0.0) for hh in hs]
            dg = [_fold3(_dot(rev_mat, _split3(db[hh]))) for hh in hs]

            for hh in hs:
                ls = lss[hh]
                dqr_ref[sl, ls] = (dq[hh] * (HGRN_DIM ** -0.5)
                                   * (sq[hh] * (1.0 + qr[hh] * (1.0 - sq[hh])))).astype(dqr_ref.dtype)
                dfk = dg[hh] / f[hh] - dk[hh]
                dfr_ref[sl, ls] = ((1.0 - lbv[:, ls]) * sf[hh] * (1.0 - sf[hh]) * dfk).astype(dfr_ref.dtype)
                dlb_ref[:, ls] += jnp.sum((1.0 - sf[hh]) * dfk, axis=0, keepdims=True)
            return carry

        lax.fori_loop(0, nc, chunk, 0)

    in_specs = _hgrn_specs(tb, nb, True)
    rblk = pl.BlockSpec((tb, HGRN_PAIR * HGRN_DIM), lambda h, t: (nb - 1 - t, h))
    in_specs = in_specs + [
        rblk,
        pl.BlockSpec((HGRN_PAIR, nc, HGRN_DIM, HGRN_DIM), lambda h, t: (h, nb - 1 - t, 0, 0)),
        pl.BlockSpec((tb, HGRN_PAIR * HGRN_DIM), lambda h, t: (nb - 1 - t, 4 // HGRN_PAIR + h)),
        pl.BlockSpec((tb, HGRN_PAIR * c), lambda h, t: (nb - 1 - t, h)),
    ]
    return _hosted_call(
        body, name=name, grid=(HGRN_HEADS // HGRN_PAIR, nb), in_specs=in_specs,
        out_specs=[rblk, rblk, rblk, rblk, pl.BlockSpec((1, HGRN_PAIR * HGRN_DIM), lambda h, t: (0, h)),
                   pl.BlockSpec((1, HGRN_DIM), lambda h, t: (0, 0))],
        out_shape=[jax.ShapeDtypeStruct((T, 512), BF16)] * 4
        + [jax.ShapeDtypeStruct((1, 512), F32), jax.ShapeDtypeStruct((1, HGRN_DIM), F32)],
        scratch=[pltpu.VMEM((HGRN_PAIR, HGRN_DIM, HGRN_DIM), F32)], args=(z, lb, onw, o, states, dcat, scores),
        semantics=("arbitrary", "arbitrary"), exchange=exchange)


def _lower_bound(logits, name):
    def body(l_ref, lb_ref):
        l0, l1 = l_ref[0:1, :], l_ref[1:2, :]
        m = jnp.maximum(l0, l1)
        e0, e1 = jnp.exp(l0 - m), jnp.exp(l1 - m)
        lb_ref[...] = e0 / (e0 + e1)

    return pl.pallas_call(
        body, name=name, out_shape=jax.ShapeDtypeStruct((1, logits.shape[1]), F32),
    )(logits)


def _lower_bound_bwd(lb, dlb, name):
    def body(lb_ref, dlb_ref, dl_ref):
        p = lb_ref[...]
        d0 = dlb_ref[...] * p * (1.0 - p)
        dl_ref[0:1, :] = d0
        dl_ref[1:2, :] = -d0

    return pl.pallas_call(
        body, name=name, out_shape=jax.ShapeDtypeStruct((2, lb.shape[1]), F32),
    )(lb, dlb)


CA_ROWS = 512


def _ca_fwd(q, k, v, name):
    T, W = q.shape
    M = k.shape[0]
    tq = min(CA_ROWS, T)
    scale = CA_HEAD_DIM ** -0.5

    def body(q_ref, k_ref, v_ref, o_ref):
        for h in range(CA_HEADS):
            hs = slice(CA_HEAD_DIM * h, CA_HEAD_DIM * (h + 1))
            s = _dot(q_ref[:, hs], k_ref[:, hs], NT) * scale
            p = jnp.exp(s - jnp.max(s, axis=-1, keepdims=True))
            p = p / jnp.sum(p, axis=-1, keepdims=True)
            o_ref[:, hs] = _dot(_bf(p), v_ref[:, hs]).astype(o_ref.dtype)

    full = pl.BlockSpec((M, W), lambda i: (0, 0))
    return pl.pallas_call(
        body, name=name, grid=(T // tq,), in_specs=[_row_spec(tq, W), full, full], out_specs=_row_spec(tq, W),
        out_shape=jax.ShapeDtypeStruct((T, W), BF16), compiler_params=_params("parallel"),
    )(q, k, v)


def _ca_bwd(q, k, v, do, name):
    T, W = q.shape
    M = k.shape[0]
    tq = min(CA_ROWS, T)
    scale = CA_HEAD_DIM ** -0.5

    def body(q_ref, k_ref, v_ref, do_ref, dq_ref, dk_ref, dv_ref):
        @pl.when(pl.program_id(0) == 0)
        def _():
            dk_ref[...] = jnp.zeros_like(dk_ref)
            dv_ref[...] = jnp.zeros_like(dv_ref)

        for h in range(CA_HEADS):
            hs = slice(CA_HEAD_DIM * h, CA_HEAD_DIM * (h + 1))
            qh, kh, vh, doh = q_ref[:, hs], k_ref[:, hs], v_ref[:, hs], do_ref[:, hs]
            s = _dot(qh, kh, NT) * scale
            p = jnp.exp(s - jnp.max(s, axis=-1, keepdims=True))
            p = p / jnp.sum(p, axis=-1, keepdims=True)
            dp = _dot(doh, vh, NT)
            ds = _bf(p * (dp - jnp.sum(p * dp, axis=-1, keepdims=True)) * scale)
            dq_ref[:, hs] = _dot(ds, kh).astype(dq_ref.dtype)
            dk_ref[:, hs] += _dot(ds, qh, TN)
            dv_ref[:, hs] += _dot(_bf(p), doh, TN)

    full = pl.BlockSpec((M, W), lambda i: (0, 0))
    return pl.pallas_call(
        body, name=name, grid=(T // tq,), in_specs=[_row_spec(tq, W), full, full, _row_spec(tq, W)],
        out_specs=[_row_spec(tq, W), full, full],
        out_shape=[jax.ShapeDtypeStruct((T, W), BF16), jax.ShapeDtypeStruct((M, W), F32), jax.ShapeDtypeStruct((M, W), F32)],
        compiler_params=_params("arbitrary"),
    )(q, k, v, do)


FFN_ROWS = 256
FFN_COLS = 1408
GELU_C0 = 0.7978845608028654
GELU_C1 = 0.044715


def _gelu(x):
    t = jnp.tanh(GELU_C0 * (x + GELU_C1 * x * x * x))
    return 0.5 * x * (1.0 + t), t


def _gelu_grad(x, t):
    return 0.5 * (1.0 + t) + 0.5 * x * (1.0 - t * t) * GELU_C0 * (1.0 + 3.0 * GELU_C1 * x * x)


def _shift_down(cur, halo, first, tb):
    row = lax.broadcasted_iota(jnp.int32, (tb, 1), 0)
    h6 = jnp.where(first, 0.0, halo[6:7])
    h7 = jnp.where(first, 0.0, halo[7:8])
    u1 = jnp.where(row == 0, h7, pltpu.roll(cur, 1, 0))
    u2 = jnp.where(row == 0, h6, jnp.where(row == 1, h7, pltpu.roll(cur, 2, 0)))
    return u1, u2


def _conv(u_ref, halo_ref, w_ref, b_ref, half, first, tb):
    cur = u_ref[half]
    u1, u2 = _shift_down(cur, halo_ref[half], first, tb)
    w = w_ref[...]
    return w[0:1] * u2 + w[1:2] * u1 + w[2:3] * cur + b_ref[...], cur, u1, u2


def _ffn_specs(tb, tc, rows_first):
    nj = D_FF // tc
    rc = (lambda a, b: (a, b)) if rows_first else (lambda a, b: (b, a))
    def at(f):
        return lambda a, b: f(*rc(a, b))
    blk = pl.BlockSpec((2, tb, tc), at(lambda t, j: (0, t, j)))
    halo = pl.BlockSpec((2, 8, tc), at(lambda t, j: (0, jnp.maximum(t * (tb // 8) - 1, 0), j)))
    wg = pl.BlockSpec((3, tc), at(lambda t, j: (0, j)))
    wv = pl.BlockSpec((3, tc), at(lambda t, j: (0, j + nj)))
    bg = pl.BlockSpec((1, tc), at(lambda t, j: (0, j)))
    bv = pl.BlockSpec((1, tc), at(lambda t, j: (0, j + nj)))
    flat = pl.BlockSpec((tb, tc), at(lambda t, j: (t, j)))
    return blk, halo, wg, wv, bg, bv, flat


def _glu_fwd(u, cw, cb, name):
    T = u.shape[1]
    tb, tc = min(FFN_ROWS, T), FFN_COLS

    def body(u_ref, halo_ref, wg_ref, wv_ref, bg_ref, bv_ref, a_ref):
        first = pl.program_id(0) == 0
        cg = _conv(u_ref, halo_ref, wg_ref, bg_ref, 0, first, tb)[0]
        cv = _conv(u_ref, halo_ref, wv_ref, bv_ref, 1, first, tb)[0]
        a_ref[...] = (_gelu(cg)[0] * cv).astype(a_ref.dtype)

    blk, halo, wg, wv, bg, bv, flat = _ffn_specs(tb, tc, True)
    return pl.pallas_call(
        body, name=name, grid=(T // tb, D_FF // tc), in_specs=[blk, halo, wg, wv, bg, bv], out_specs=flat,
        out_shape=jax.ShapeDtypeStruct((T, D_FF), BF16), compiler_params=_params("parallel", "parallel"),
    )(u, u, cw, cw, cb, cb)


def _glu_bwd(u, cw, cb, da, name, exchange=None):
    T = u.shape[1]
    tb, tc = min(FFN_ROWS, T), FFN_COLS

    def body(u_ref, halo_ref, wg_ref, wv_ref, bg_ref, bv_ref, da_ref, dc_ref, db_ref, dw_ref):
        first = pl.program_id(1) == 0

        @pl.when(first)
        def _():
            db_ref[...] = jnp.zeros_like(db_ref)
            dw_ref[...] = jnp.zeros_like(dw_ref)

        cg, ug, ug1, ug2 = _conv(u_ref, halo_ref, wg_ref, bg_ref, 0, first, tb)
        cv, uv, uv1, uv2 = _conv(u_ref, halo_ref, wv_ref, bv_ref, 1, first, tb)
        da = da_ref[...]
        gl, t = _gelu(cg)
        dcg = da * cv * _gelu_grad(cg, t)
        dcv = da * gl
        dc_ref[0] = dcg
        dc_ref[1] = dcv
        for half, dc, taps in ((0, dcg, (ug2, ug1, ug)), (1, dcv, (uv2, uv1, uv))):
            db_ref[half] += jnp.sum(dc, axis=0, keepdims=True)
            for tap in range(3):
                dw_ref[half, tap:tap + 1, :] += jnp.sum(dc * taps[tap], axis=0, keepdims=True)

    blk, halo, wg, wv, bg, bv, flat = _ffn_specs(tb, tc, False)
    return _hosted_call(
        body, name=name, grid=(D_FF // tc, T // tb), in_specs=[blk, halo, wg, wv, bg, bv, flat],
        out_specs=[blk, pl.BlockSpec((2, 1, tc), lambda j, t: (0, 0, j)), pl.BlockSpec((2, 3, tc), lambda j, t: (0, 0, j))],
        out_shape=[jax.ShapeDtypeStruct((2, T, D_FF), F32), jax.ShapeDtypeStruct((2, 1, D_FF), F32),
                   jax.ShapeDtypeStruct((2, 3, D_FF), F32)],
        scratch=[], args=(u, u, cw, cw, cb, cb, da), semantics=("parallel", "arbitrary"), exchange=exchange)


def _conv_bwd(dc, cw, name):
    T = dc.shape[1]
    tb, tc = min(FFN_ROWS, T), FFN_COLS
    nt, nj = T // tb, D_FF // tc

    def body(dc_ref, halo_ref, wg_ref, wv_ref, du_ref):
        last = pl.program_id(0) == nt - 1
        row = lax.broadcasted_iota(jnp.int32, (tb, 1), 0)
        for half, w_ref in ((0, wg_ref), (1, wv_ref)):
            cur = dc_ref[half]
            halo = halo_ref[half]
            h0 = jnp.where(last, 0.0, halo[0:1])
            h1 = jnp.where(last, 0.0, halo[1:2])
            d1 = jnp.where(row == tb - 1, h0, pltpu.roll(cur, tb - 1, 0))
            d2 = jnp.where(row == tb - 1, h1, jnp.where(row == tb - 2, h0, pltpu.roll(cur, tb - 2, 0)))
            w = w_ref[...]
            du_ref[half] = (w[2:3] * cur + w[1:2] * d1 + w[0:1] * d2).astype(du_ref.dtype)

    blk = pl.BlockSpec((2, tb, tc), lambda t, j: (0, t, j))
    halo = pl.BlockSpec((2, 8, tc), lambda t, j: (0, jnp.minimum((t + 1) * (tb // 8), T // 8 - 1), j))
    wg = pl.BlockSpec((3, tc), lambda t, j: (0, j))
    wv = pl.BlockSpec((3, tc), lambda t, j: (0, j + nj))
    return pl.pallas_call(
        body, name=name, grid=(nt, nj), in_specs=[blk, halo, wg, wv], out_specs=blk,
        out_shape=jax.ShapeDtypeStruct((2, T, D_FF), BF16), compiler_params=_params("parallel", "parallel"),
    )(dc, dc, cw, cw)


def _mesh_pos():
    return lax.axis_index("x"), lax.axis_index("y"), lax.axis_index("c")


def _peer(pos, k):
    return (pos[0] ^ ((k >> 2) & 1), pos[1] ^ ((k >> 1) & 1), pos[2] ^ (k & 1))


def _index(pos):
    return 4 * pos[0] + 2 * pos[1] + pos[2]


class _Exchange:
    def __init__(self, kind, buf, relay=False):
        assert kind in ("gather", "scatter") and not (relay and kind == "scatter")
        self.kind, self.buf, self.relay = kind, buf, relay
        self.out_shape = jax.ShapeDtypeStruct(((N_DEV,) + buf.shape) if kind == "gather" else buf.shape, buf.dtype)
        self.spec = pl.BlockSpec(memory_space=pl.ANY)
        self.scratch = [pltpu.SemaphoreType.DMA((N_DEV - 1,)), pltpu.SemaphoreType.DMA((N_DEV - 1,)),
                        pltpu.SemaphoreType.DMA]

    def _src(self, x_ref, dest):
        return x_ref if self.kind == "gather" else x_ref.at[dest]

    def _copies(self, x_ref, out_ref, send_sems, recv_sems, local_sem):
        pos = _mesh_pos()
        me = _index(pos)
        local = pltpu.make_async_copy(self._src(x_ref, me), out_ref.at[me], local_sem)
        sends, recvs = [], []
        for k in range(1, N_DEV):
            peer = _peer(pos, k)
            sends.append(pltpu.make_async_remote_copy(
                src_ref=self._src(x_ref, _index(peer)), dst_ref=out_ref.at[me], send_sem=send_sems.at[k - 1],
                recv_sem=recv_sems.at[k - 1], device_id=peer, device_id_type=pl.DeviceIdType.MESH))
            recvs.append(pltpu.make_async_remote_copy(
                src_ref=self._src(x_ref, me), dst_ref=out_ref.at[_index(peer)], send_sem=send_sems.at[k - 1],
                recv_sem=recv_sems.at[k - 1], device_id=peer, device_id_type=pl.DeviceIdType.MESH))
        return local, sends, recvs

    def _relay_copies(self, x_ref, out_ref, send_sems, recv_sems, local_sem):
        x, y, c = _mesh_pos()
        me, sibling = (x, y, c), (x, y, 1 - c)
        chips = [(1 - x, y), (x, 1 - y), (1 - x, 1 - y)]

        def copy(k, block, to, own=False):
            return pltpu.make_async_remote_copy(
                src_ref=x_ref if own else out_ref.at[_index(block)], dst_ref=out_ref.at[_index(block)],
                send_sem=send_sems.at[k], recv_sem=recv_sems.at[k], device_id=to, device_id_type=pl.DeviceIdType.MESH)

        local = pltpu.make_async_copy(x_ref, out_ref.at[_index(me)], local_sem)
        first = [copy(0, me, sibling, own=True)] + [copy(1 + j, me, (*chip, c), own=True) for j, chip in enumerate(chips)]
        landed = [copy(1 + j, (*chip, c), me) for j, chip in enumerate(chips)]
        passed = [copy(4 + j, (*chip, c), sibling) for j, chip in enumerate(chips)]
        from_sibling = [copy(0, sibling, me)] + [copy(4 + j, (*chip, 1 - c), me) for j, chip in enumerate(chips)]
        return local, first, landed, passed, from_sibling

    def start(self, *refs):
        if self.relay:
            local, first = self._relay_copies(*refs)[:2]
            local.start()
            for cp in first:
                cp.start()
            return
        local, sends, _ = self._copies(*refs)
        local.start()
        for cp in sends:
            cp.start()

    def finish(self, *refs):
        if self.relay:
            local, first, landed, passed, from_sibling = self._relay_copies(*refs)
            for got, forward in zip(landed, passed):
                got.wait_recv()
                forward.start()
            for cp in from_sibling:
                cp.wait_recv()
            for cp in first + passed:
                cp.wait_send()
            local.wait()
            return
        local, sends, recvs = self._copies(*refs)
        for cp in recvs:
            cp.wait_recv()
        for cp in sends:
            cp.wait_send()
        local.wait()


def _hosted_call(body, *, name, grid, in_specs, out_specs, out_shape, scratch, args, semantics, exchange=None):
    if exchange is None:
        return pl.pallas_call(
            body, name=name, grid=grid, in_specs=in_specs, out_specs=out_specs, out_shape=out_shape,
            scratch_shapes=scratch, compiler_params=_params(*semantics))(*args)
    n_in, n_out, n_scr = len(in_specs), len(out_specs), len(scratch)

    def hosted(*refs):
        ins, x_ref = refs[:n_in], refs[n_in]
        outs, land_ref = refs[n_in + 1:n_in + 1 + n_out], refs[n_in + 1 + n_out]
        rest = refs[n_in + n_out + 2:]
        sems = rest[n_scr:]
        ids = [pl.program_id(a) for a in range(len(grid))]
        first, last = ids[0] == 0, ids[0] == grid[0] - 1
        for a in range(1, len(grid)):
            first, last = first & (ids[a] == 0), last & (ids[a] == grid[a] - 1)

        @pl.when(first)
        def _():
            exchange.start(x_ref, land_ref, *sems)

        body(*ins, *outs, *rest[:n_scr])

        @pl.when(last)
        def _():
            exchange.finish(x_ref, land_ref, *sems)

    return pl.pallas_call(
        hosted, name=name, grid=grid, in_specs=list(in_specs) + [exchange.spec],
        out_specs=list(out_specs) + [exchange.spec], out_shape=list(out_shape) + [exchange.out_shape],
        scratch_shapes=list(scratch) + exchange.scratch, compiler_params=_params(*(["arbitrary"] * len(grid))),
    )(*args, exchange.buf)


def _exchange_alone(exchange, name):
    def body(x_ref, out_ref, send_sems, recv_sems, local_sem):
        exchange.start(x_ref, out_ref, send_sems, recv_sems, local_sem)
        exchange.finish(x_ref, out_ref, send_sems, recv_sems, local_sem)

    return pl.pallas_call(
        body, name=name, out_shape=exchange.out_shape, in_specs=[exchange.spec], out_specs=exchange.spec,
        scratch_shapes=exchange.scratch)(exchange.buf)


def _adamw(w, g, m, v):
    m = ADAM_B1 * m + (1.0 - ADAM_B1) * g
    v = ADAM_B2 * v + (1.0 - ADAM_B2) * (g * g)
    m_hat = m / (1.0 - ADAM_B1 ** ADAM_STEP)
    v_hat = v / (1.0 - ADAM_B2 ** ADAM_STEP)
    delta = -ADAM_LR * (m_hat / (jnp.sqrt(v_hat) + ADAM_EPS) + ADAM_WD * w)
    return delta, m, v


def _sum_rows(parts, r0, rows, name, wmv=None):
    C = parts.shape[2]
    tr = max(t for t in range(16, ROWS + 1, 16) if rows % t == 0 and r0 % t == 0)

    def total(p_ref):
        g = p_ref[0].astype(F32)
        for i in range(1, N_DEV):
            g = g + p_ref[i].astype(F32)
        return g

    p_spec = pl.BlockSpec((N_DEV, tr, C), lambda i: (0, r0 // tr + i, 0))
    if wmv is None:
        def body(p_ref, g_ref):
            g_ref[...] = total(p_ref)

        return pl.pallas_call(
            body, name=name, grid=(rows // tr,), in_specs=[p_spec], out_specs=_row_spec(tr, C),
            out_shape=jax.ShapeDtypeStruct((rows, C), F32), compiler_params=_params("parallel"))(parts)

    def body(p_ref, w_ref, m_ref, v_ref, g_ref, d_ref, mo_ref, vo_ref):
        g = total(p_ref)
        g_ref[0] = g
        d_ref[0], mo_ref[0], vo_ref[0] = _adamw(w_ref[0], g, m_ref[0], v_ref[0])

    blk = pl.BlockSpec((1, tr, C), lambda i: (0, i, 0))
    return pl.pallas_call(
        body, name=name, grid=(rows // tr,), in_specs=[p_spec, blk, blk, blk], out_specs=[blk] * 4,
        out_shape=[jax.ShapeDtypeStruct((1, rows, C), F32)] * 4, compiler_params=_params("parallel"))(parts, *wmv)


def _sum_parts(parts, name):
    _, R, C = parts.shape

    def body(p_ref, g_ref):
        g = p_ref[0]
        for i in range(1, N_DEV):
            g = g + p_ref[i]
        g_ref[...] = g

    return pl.pallas_call(body, name=name, out_shape=jax.ShapeDtypeStruct((R, C), F32))(parts)


def _adamw_call(w, g, m, v, name):
    _, R, C = w.shape
    tr = min(ROWS, R)

    def body(w_ref, g_ref, m_ref, v_ref, d_ref, mo_ref, vo_ref):
        d_ref[...], mo_ref[...], vo_ref[...] = _adamw(w_ref[...], g_ref[...], m_ref[...], v_ref[...])

    blk = pl.BlockSpec((1, tr, C), lambda i: (0, i, 0))
    return pl.pallas_call(
        body, name=name, grid=(R // tr,), in_specs=[blk] * 4, out_specs=[blk] * 3,
        out_shape=[jax.ShapeDtypeStruct(w.shape, F32)] * 3, compiler_params=_params("parallel"))(w, g, m, v)


NORMS = ("mix_pre_norm", "mix_post_norm", "ca_pre_norm", "mem_norm", "ca_post_norm", "ffn_pre_norm", "ffn_post_norm")
SMALL = ("mix_pre_norm", "attn_sinks", "hgrn_lb_logits", "hgrn_out_norm", "mix_post_norm", "ca_pre_norm", "mem_norm",
         "ca_post_norm", "ffn_pre_norm", "ffn_conv_w", "ffn_conv_b", "ffn_post_norm")
SMALL_ROWS = 40
ROW_LOGITS, ROW_MISC, ROW_CONV_B, ROW_CONV_W = 7, 8, 9, 15
LANE_SINKS, LANE_LOSS = 128, 256
FF_PIECES = ((0, 1024), (1024, 2048), (2048, D_FF))


def _pack_small(norm_grads, dlogits, donw, dsinks, loss, d_cb, d_cw, name):
    def body(*refs):
        norm_refs = refs[:len(NORMS)]
        dl_ref, donw_ref, dsink_ref, loss_ref, cb_ref, cw_ref, out_ref = refs[len(NORMS):]
        out_ref[...] = jnp.zeros_like(out_ref)
        for i, ref in enumerate(norm_refs):
            out_ref[i:i + 1, :] = ref[...]
        out_ref[ROW_LOGITS:ROW_LOGITS + 1, 0:512] = dl_ref[0:1, :]
        out_ref[ROW_LOGITS:ROW_LOGITS + 1, 512:1024] = dl_ref[1:2, :]
        out_ref[ROW_MISC:ROW_MISC + 1, 0:HGRN_DIM] = donw_ref[...]
        out_ref[ROW_MISC:ROW_MISC + 1, LANE_SINKS:LANE_SINKS + ATTN_Q_HEADS] = dsink_ref[...]
        out_ref[ROW_MISC:ROW_MISC + 1, LANE_LOSS:LANE_LOSS + LANE] = loss_ref[...]
        for h in range(2):
            for j, (c0, c1) in enumerate(FF_PIECES):
                r = ROW_CONV_B + 3 * h + j
                out_ref[r:r + 1, 0:c1 - c0] = cb_ref[h, :, c0:c1]
                for t in range(3):
                    r = ROW_CONV_W + 3 * (3 * h + t) + j
                    out_ref[r:r + 1, 0:c1 - c0] = cw_ref[h, t:t + 1, c0:c1]

    return pl.pallas_call(
        body, name=name, out_shape=jax.ShapeDtypeStruct((SMALL_ROWS, 1024), F32),
    )(*norm_grads, dlogits, donw, dsinks, loss, d_cb, d_cw)


def _adamw_small(total, g_conv_w, w, m, v, name):
    n = len(SMALL)

    def body(*refs):
        t_ref, gcw_ref = refs[:2]
        w_refs, m_refs, v_refs = (dict(zip(SMALL, refs[2 + n * i:2 + n * (i + 1)])) for i in range(3))
        outs = refs[2 + 3 * n:]
        loss_ref = outs[0]
        g_refs, d_refs, mo_refs, vo_refs = (dict(zip(SMALL, outs[1 + n * i:1 + n * (i + 1)])) for i in range(4))
        loss_ref[...] = t_ref[ROW_MISC:ROW_MISC + 1, LANE_LOSS:LANE_LOSS + 1]

        def step(nm, idx, g):
            g_refs[nm][idx] = g
            d_refs[nm][idx], mo_refs[nm][idx], vo_refs[nm][idx] = _adamw(w_refs[nm][idx], g, m_refs[nm][idx], v_refs[nm][idx])

        everything = (slice(None), slice(None))
        for i, nm in enumerate(NORMS):
            step(nm, everything, t_ref[i:i + 1, :])
        step("hgrn_lb_logits", (slice(0, 1), slice(None)), t_ref[ROW_LOGITS:ROW_LOGITS + 1, 0:512])
        step("hgrn_lb_logits", (slice(1, 2), slice(None)), t_ref[ROW_LOGITS:ROW_LOGITS + 1, 512:1024])
        step("hgrn_out_norm", everything, t_ref[ROW_MISC:ROW_MISC + 1, 0:HGRN_DIM])
        step("attn_sinks", everything, t_ref[ROW_MISC:ROW_MISC + 1, LANE_SINKS:LANE_SINKS + ATTN_Q_HEADS])
        for h in range(2):
            for j, (c0, c1) in enumerate(FF_PIECES):
                r = ROW_CONV_B + 3 * h + j
                step("ffn_conv_b", (slice(None), slice(D_FF * h + c0, D_FF * h + c1)), t_ref[r:r + 1, 0:c1 - c0])
        step("ffn_conv_w", (slice(None), slice(None), slice(None)), gcw_ref[...])

    shapes = [jax.ShapeDtypeStruct(w[nm].shape, F32) for nm in SMALL]
    out = pl.pallas_call(
        body, name=name, out_shape=[jax.ShapeDtypeStruct((1, 1), F32)] + shapes * 4,
    )(total, g_conv_w, *[w[nm] for nm in SMALL], *[m[nm] for nm in SMALL], *[v[nm] for nm in SMALL])
    trees = [dict(zip(SMALL, out[1 + n * i:1 + n * (i + 1)])) for i in range(4)]
    return out[0], trees


BIG = ("w_in", "w_out", "ca_wq", "ca_wk", "ca_wv", "ca_wo", "ffn_w_up", "ffn_w_down")
BIG_FULL = {"w_in": (1024, 2816), "w_out": (1024, 1024), "ca_wq": (1024, 1024), "ca_wk": (1024, 1024),
            "ca_wv": (1024, 1024), "ca_wo": (1024, 1024), "ffn_w_up": (1024, 5632), "ffn_w_down": (2816, 1024)}
G_IN, G_MID, G_UP, G_DOWN = ("w_in",), ("w_out", "ca_wq", "ca_wk", "ca_wv", "ca_wo"), ("ffn_w_up",), ("ffn_w_down",)
GROUPS = (G_IN, G_MID, G_UP, G_DOWN)
COL_SHARDED = ("w_in", "ffn_w_up")
PACK_COLS = 1024


def _big_rows(name):
    r, c = BIG_FULL[name]
    return r * c // N_DEV // PACK_COLS


def _pack_shards(w, names):
    rows = [w[n][0].T if n in COL_SHARDED else w[n][0] for n in names]
    return (rows[0] if len(rows) == 1 else jnp.concatenate(rows, axis=0)).astype(BF16)


def _unpack_gathered(gathered, names):
    out, r0 = {}, 0
    for n in names:
        rows = _big_rows(n)
        out[n] = gathered[:, r0:r0 + rows].reshape(N_DEV * rows, PACK_COLS)
        r0 += rows
    return out


def _pack_full_grads(grads, names):
    parts = [grads[n].reshape(N_DEV, _big_rows(n), PACK_COLS) for n in names]
    return parts[0] if len(parts) == 1 else jnp.concatenate(parts, axis=1)


def kernel(x, mem, mix_pre_norm, w_in, attn_sinks, hgrn_lb_logits, hgrn_out_norm, w_out, mix_post_norm, ca_pre_norm, mem_norm, ca_wq, ca_wk, ca_wv, ca_wo, ca_post_norm, ffn_pre_norm, ffn_w_up, ffn_conv_w, ffn_conv_b, ffn_w_down, ffn_post_norm, loss_target, m_mix_pre_norm, m_w_in, m_attn_sinks, m_hgrn_lb_logits, m_hgrn_out_norm, m_w_out, m_mix_post_norm, m_ca_pre_norm, m_mem_norm, m_ca_wq, m_ca_wk, m_ca_wv, m_ca_wo, m_ca_post_norm, m_ffn_pre_norm, m_ffn_w_up, m_ffn_conv_w, m_ffn_conv_b, m_ffn_w_down, m_ffn_post_norm, v_mix_pre_norm, v_w_in, v_attn_sinks, v_hgrn_lb_logits, v_hgrn_out_norm, v_w_out, v_mix_post_norm, v_ca_pre_norm, v_mem_norm, v_ca_wq, v_ca_wk, v_ca_wv, v_ca_wo, v_ca_post_norm, v_ffn_pre_norm, v_ffn_w_up, v_ffn_conv_w, v_ffn_conv_b, v_ffn_w_down, v_ffn_post_norm):
    names = ["mix_pre_norm", "w_in", "attn_sinks", "hgrn_lb_logits", "hgrn_out_norm", "w_out", "mix_post_norm",
             "ca_pre_norm", "mem_norm", "ca_wq", "ca_wk", "ca_wv", "ca_wo", "ca_post_norm", "ffn_pre_norm",
             "ffn_w_up", "ffn_conv_w", "ffn_conv_b", "ffn_w_down", "ffn_post_norm"]
    w_all = dict(zip(names, [mix_pre_norm, w_in, attn_sinks, hgrn_lb_logits, hgrn_out_norm, w_out, mix_post_norm,
                             ca_pre_norm, mem_norm, ca_wq, ca_wk, ca_wv, ca_wo, ca_post_norm, ffn_pre_norm,
                             ffn_w_up, ffn_conv_w, ffn_conv_b, ffn_w_down, ffn_post_norm]))
    m_all = dict(zip(names, [m_mix_pre_norm, m_w_in, m_attn_sinks, m_hgrn_lb_logits, m_hgrn_out_norm, m_w_out,
                             m_mix_post_norm, m_ca_pre_norm, m_mem_norm, m_ca_wq, m_ca_wk, m_ca_wv, m_ca_wo,
                             m_ca_post_norm, m_ffn_pre_norm, m_ffn_w_up, m_ffn_conv_w, m_ffn_conv_b, m_ffn_w_down,
                             m_ffn_post_norm]))
    v_all = dict(zip(names, [v_mix_pre_norm, v_w_in, v_attn_sinks, v_hgrn_lb_logits, v_hgrn_out_norm, v_w_out,
                             v_mix_post_norm, v_ca_pre_norm, v_mem_norm, v_ca_wq, v_ca_wk, v_ca_wv, v_ca_wo,
                             v_ca_post_norm, v_ffn_pre_norm, v_ffn_w_up, v_ffn_conv_w, v_ffn_conv_b, v_ffn_w_down,
                             v_ffn_post_norm]))
    dev = _index(_mesh_pos())

    w_packs = {grp: _pack_shards(w_all, grp) for grp in GROUPS}
    shard_w = D_FF * 2 // N_DEV
    conv_w_rows = _exchange_alone(_Exchange("gather", ffn_conv_w[0]), "gather_conv_w")
    conv_w_full = conv_w_rows.transpose(1, 0, 2).reshape(3, 2 * D_FF)

    received, small_pack, grad_x = _local_step(
        x[0], mem[0], loss_target[0], w_packs, conv_w_full,
        {n: w_all[n] for n in NORMS}, attn_sinks, hgrn_lb_logits, hgrn_out_norm, ffn_conv_b)

    total = _sum_parts(_exchange_alone(_Exchange("gather", small_pack), "gather_small"), "sum_small")
    cw = total[ROW_CONV_W:ROW_CONV_W + 18].reshape(2, 3, 3 * PACK_COLS)[:, :, :D_FF]
    cw = cw.transpose(1, 0, 2).reshape(3, 2 * D_FF)
    g_conv_w = lax.dynamic_slice_in_dim(cw, dev * shard_w, shard_w, axis=1)[None]
    loss, (out_g, out_d, out_m, out_v) = _adamw_small(total, g_conv_w, w_all, m_all, v_all, "adamw_small")

    for grp in GROUPS:
        r0 = 0
        for n in grp:
            rows = _big_rows(n)
            if n in COL_SHARDED:
                g = _sum_rows(received[grp], r0, rows, "sum_" + n).T[None]
                d, mo, vo = _adamw_call(w_all[n], g, m_all[n], v_all[n], "adamw_" + n)
            else:
                g, d, mo, vo = _sum_rows(received[grp], r0, rows, "adamw_" + n, wmv=(w_all[n], m_all[n], v_all[n]))
            out_g[n], out_d[n], out_m[n], out_v[n] = g, d, mo, vo
            r0 += rows

    return (loss[0, 0], grad_x[None], *[out_g[n] for n in names], *[out_d[n] for n in names],
            *[out_m[n] for n in names], *[out_v[n] for n in names])


def _local_step(x, mem, target, w_packs, conv_w, norms, sinks, lb_logits, out_norm, conv_b):
    g1, g2, g3 = norms["mix_pre_norm"], norms["mix_post_norm"], norms["ca_pre_norm"]
    g4, g5, g6, g7 = norms["mem_norm"], norms["ca_post_norm"], norms["ffn_pre_norm"], norms["ffn_post_norm"]

    h1, gathered = _norm_fwd(x, g1, "mix_norm", exchange=_Exchange("gather", w_packs[G_IN], relay=True))
    w_in_t = _unpack_gathered(gathered, G_IN)["w_in"]
    up_shard = w_packs[G_UP]
    up_rows = up_shard.shape[0]
    up_cuts = (0, up_rows // 2, 3 * up_rows // 4, up_rows)
    up_parts = [up_shard[a:b] for a, b in zip(up_cuts[:-1], up_cuts[1:])]
    z, up_0 = _mm(h1, w_in_t, mode="nt", out_dtype=BF16, name="in_proj", tn=1408,
                  exchange=_Exchange("gather", up_parts[0]))
    attn, lse, gathered = _swa_fwd(z, sinks, "swa_fwd", exchange=_Exchange("gather", w_packs[G_DOWN]))
    w_down = _unpack_gathered(gathered, G_DOWN)["ffn_w_down"]
    lb = _lower_bound(lb_logits, "lower_bound")
    rec, o_rec, states, scores, gathered = _hgrn_fwd(
        z, lb, out_norm, "hgrn_fwd", exchange=_Exchange("gather", w_packs[G_MID]))
    w_out, wq, wk, wv, wo = (_unpack_gathered(gathered, G_MID)[n] for n in G_MID)
    cat = jnp.concatenate([attn, rec], axis=1)
    mix = _mm(cat, w_out, mode="nn", out_dtype=BF16, name="out_proj")
    x1, h2, up_1 = _post_pre(x, mix, g2, g3, "mix_post", exchange=_Exchange("gather", up_parts[1]))
    mem_n = _norm_fwd(mem, g4, "mem_norm")
    q = _mm(h2, wq, mode="nn", out_dtype=BF16, name="ca_q")
    k = _mm(mem_n, wk, mode="nn", out_dtype=BF16, name="ca_k")
    v = _mm(mem_n, wv, mode="nn", out_dtype=BF16, name="ca_v")
    oc = _ca_fwd(q, k, v, "ca_fwd")
    c = _mm(oc, wo, mode="nn", out_dtype=BF16, name="ca_o")
    x2, h3, up_2 = _post_pre(x1, c, g5, g6, "ca_post", exchange=_Exchange("gather", up_parts[2]))
    w_up_t = jnp.concatenate([up_0, up_1, up_2], axis=1).reshape(-1, PACK_COLS)
    u = _mm(h3, w_up_t, mode="nt", out_dtype=F32, name="ffn_up", tn=1408, split_out=True)
    a = _glu_fwd(u, conv_w, conv_b, "glu_fwd")
    y = _mm(a, w_down, mode="nn", out_dtype=BF16, name="ffn_down", tk=2816)
    loss, dx3, dy, dg7 = _final(x2, y, g7, target, "loss_head")

    da = _mm(dy, w_down, mode="nt", out_dtype=F32, name="ffn_down_dx", tn=1408)
    d_w_down = _mm(a, dy, mode="tn", out_dtype=BF16, name="ffn_down_dw", tm=1408, tk=1024)
    dc, d_cb, d_cw, got_down = _glu_bwd(
        u, conv_w, conv_b, da, "glu_bwd",
        exchange=_Exchange("scatter", _pack_full_grads({"ffn_w_down": d_w_down}, G_DOWN)))
    du = _conv_bwd(dc, conv_w, "conv_bwd")
    d_w_up_t = _mm(du, h3, mode="tn", out_dtype=BF16, name="ffn_up_dw", tm=1408, tk=1024, split_a=True)
    dx2, dcv, dg6, dg5, got_up = _mm(
        du, w_up_t, mode="nn", out_dtype=BF16, name="ffn_up_dx", tm=1024, tk=1408, split_a=True,
        exchange=_Exchange("scatter", _pack_full_grads({"ffn_w_up": d_w_up_t}, G_UP)),
        epilogue=_norm_bwd2(dx3, x2, c, g6, g5))
    doc = _mm(dcv, wo, mode="nt", out_dtype=BF16, name="ca_o_dx")
    d_wo = _mm(oc, dcv, mode="tn", out_dtype=BF16, name="ca_o_dw", tm=1024, tk=1024)
    dq, dk, dv = _ca_bwd(q, k, v, doc, "ca_bwd")
    d_wq = _mm(h2, dq, mode="tn", out_dtype=BF16, name="ca_q_dw", tm=1024, tk=1024)
    dx1, dmix, dg3, dg2 = _mm(dq, wq, mode="nt", out_dtype=BF16, name="ca_q_dx",
                              epilogue=_norm_bwd2(dx2, x1, mix, g3, g2))
    d_wk = _mm(mem_n, dk, mode="tn", out_dtype=BF16, name="ca_k_dw", tm=1024)
    d_wv = _mm(mem_n, dv, mode="tn", out_dtype=BF16, name="ca_v_dw", tm=1024)
    dmem_k = _mm(dk, wk, mode="nt", out_dtype=F32, name="ca_k_dx")
    dmem_v = _mm(dv, wv, mode="nt", out_dtype=F32, name="ca_v_dx")
    dg4 = _gain_bwd(mem, dmem_k, dmem_v, "mem_norm_bwd")
    dcat = _mm(dmix, w_out, mode="nt", out_dtype=BF16, name="out_proj_dx")
    d_w_out = _mm(cat, dmix, mode="tn", out_dtype=BF16, name="out_proj_dw", tm=1024, tk=1024)
    mid = {"w_out": d_w_out, "ca_wq": d_wq, "ca_wk": d_wk, "ca_wv": d_wv, "ca_wo": d_wo}
    dqr, dfr, dir_, dgr, dlb, donw, got_mid = _hgrn_bwd(
        z, lb, out_norm, o_rec, states, scores, dcat, "hgrn_bwd",
        exchange=_Exchange("scatter", _pack_full_grads(mid, G_MID)))
    dq_a, dka, dkb, dva, dvb, dsinks = _swa_bwd(z, sinks, dcat, lse, "swa_bwd")
    dz = _assemble_dz(dq_a, dka, dkb, dva, dvb, dqr, dfr, dir_, dgr, "assemble_dz")
    d_w_in_t = _mm(dz, h1, mode="tn", out_dtype=BF16, name="in_proj_dw", tm=1408, tk=1024)
    dx, dg1, got_in = _mm(dz, w_in_t, mode="nn", out_dtype=BF16, name="in_proj_dx", tm=512, tk=2816,
                          exchange=_Exchange("scatter", _pack_full_grads({"w_in": d_w_in_t}, G_IN)),
                          epilogue=_norm_bwd1(dx1, x, g1))

    small_pack = _pack_small(
        (dg1, dg2, dg3, dg4, dg5, dg6, dg7), _lower_bound_bwd(lb, dlb, "lower_bound_bwd"), donw, dsinks, loss,
        d_cb, d_cw, "pack_small")
    return {G_IN: got_in, G_MID: got_mid, G_UP: got_up, G_DOWN: got_down}, small_pack, dx
```

```python
import jax
import jax.numpy as jnp
from jax import lax
from jax.experimental import pallas as pl
from jax.experimental.pallas import tpu as pltpu

F32 = jnp.float32
BF16 = jnp.bfloat16
EPS = 1e-6
N_DEV = 8
MESH_AXES = ("x", "y", "c")

ATTN_HEAD_DIM = 64
ATTN_Q_HEADS = 8
ATTN_KV_HEADS = 2
ATTN_BLOCK = 128
HGRN_HEADS = 4
HGRN_DIM = 128
HGRN_CHUNK = 64
HGRN_PAIR = 4
Z_Q, Z_F, Z_I, Z_G = 768, 1280, 1792, 2304
HGRN_LEVELS = (32, 16, 8, 4, 2, 1)
CA_HEADS = 4
CA_HEAD_DIM = 256
D_FF = 2816

ADAM_LR = 0.001
ADAM_B1 = 0.9
ADAM_B2 = 0.999
ADAM_EPS = 1e-08
ADAM_WD = 0.01
ADAM_STEP = 10

VMEM_LIMIT = 58 << 20
EPILOGUE_ROWS = 256
LANE = 128

NT = (((1,), (1,)), ((), ()))
TN = (((0,), (0,)), ((), ()))


def _params(*sem):
    return pltpu.CompilerParams(dimension_semantics=sem, vmem_limit_bytes=VMEM_LIMIT)


def _tile(n, cap):
    if n <= cap:
        return n
    best = 0
    for t in range(LANE, cap + 1, LANE):
        if n % t == 0:
            best = t
    assert best, (n, cap)
    return best


def _dot(a, b, dims=None):
    if dims is None:
        return jnp.dot(a, b, preferred_element_type=F32)
    return lax.dot_general(a, b, dims, preferred_element_type=F32)


def _bf(x):
    return x.astype(BF16)


def _sigmoid(x):
    return 1.0 / (1.0 + jnp.exp(-x))


def _rms(x):
    r = lax.rsqrt(jnp.mean(x * x, axis=-1, keepdims=True) + EPS)
    return x * r, r


def _rms_bwd(dxh, xh, r):
    return r * (dxh - xh * jnp.mean(dxh * xh, axis=-1, keepdims=True))


def _mm(a, b, *, mode, out_dtype, name, tm=1024, tn=1024, tk=1024, split_a=False, split_b=False, split_out=False,
        exchange=None, epilogue=None):
    def dims(arr, split):
        if split:
            return arr.shape[1], 2 * arr.shape[2]
        return arr.shape

    ar, ac = dims(a, split_a)
    br, bc = dims(b, split_b)
    if mode == "nn":
        M, K, N = ar, ac, bc
        assert br == K
    elif mode == "nt":
        M, K, N = ar, ac, br
        assert bc == K
    else:
        K, M, N = ar, ac, bc
        assert br == K
    a_cols_half = ac // 2 if split_a else None
    b_cols_half = bc // 2 if split_b else None
    tm = _tile(M, tm)
    tn = _tile((N // 2) if (split_out or (split_b and mode != "nt")) else N, tn)
    tk = _tile((K // 2) if ((split_a and mode != "tn") or (split_b and mode == "nt")) else K, tk)
    if split_a and mode == "tn":
        tm = _tile(M // 2, tm)
    gm, gn, gk = M // tm, N // tn, K // tk
    a_bytes, b_bytes = a.size * a.dtype.itemsize, b.size * b.dtype.itemsize
    rows_outer = gk > 1 or a_bytes + gm * b_bytes <= gn * a_bytes + b_bytes
    grid = (gm, gn, gk) if rows_outer else (gn, gm, gk)

    def spec(split, half, blk, rc):
        def imap(p, q, k):
            r, c = rc(*((p, q) if rows_outer else (q, p)), k)
            if not split:
                return (r, c)
            per_half = half // blk[1]
            return (c // per_half, r, c % per_half)

        return pl.BlockSpec(((None,) + blk) if split else blk, imap)

    if mode == "nn":
        a_spec = spec(split_a, a_cols_half, (tm, tk), lambda i, j, k: (i, k))
        b_spec = spec(split_b, b_cols_half, (tk, tn), lambda i, j, k: (k, j))
        dn = None
    elif mode == "nt":
        a_spec = spec(split_a, a_cols_half, (tm, tk), lambda i, j, k: (i, k))
        b_spec = spec(split_b, b_cols_half, (tn, tk), lambda i, j, k: (j, k))
        dn = NT
    else:
        a_spec = spec(split_a, a_cols_half, (tk, tm), lambda i, j, k: (k, i))
        b_spec = spec(split_b, b_cols_half, (tk, tn), lambda i, j, k: (k, j))
        dn = TN
    o_spec = spec(split_out, N // 2 if split_out else None, (tm, tn), lambda i, j, k: (i, j))
    out_shape = (2, M, N // 2) if split_out else (M, N)

    in_specs, out_specs, args = [a_spec, b_spec], [o_spec], (a, b)
    out_shapes = [jax.ShapeDtypeStruct(out_shape, out_dtype)]
    semantics = ("parallel", "parallel", "arbitrary")

    def store(result, extra, outs):
        outs[0][...] = result[...].astype(outs[0].dtype)

    if epilogue is not None:
        assert gn == 1 and not split_out
        n_vec = epilogue.n_out_vecs
        row = pl.BlockSpec((tm, N), lambda p, q, k: ((p if rows_outer else q), 0))
        vec = pl.BlockSpec((1, N), lambda p, q, k: (0, 0))
        in_specs += [row] * len(epilogue.rows) + [vec] * len(epilogue.vecs)
        args += tuple(epilogue.rows) + tuple(epilogue.vecs)
        out_specs = [row] * len(epilogue.out_rows) + [vec] * n_vec
        out_shapes = ([jax.ShapeDtypeStruct((M, N), dt) for dt in epilogue.out_rows]
                      + [jax.ShapeDtypeStruct((1, N), F32)] * n_vec)
        semantics = ("arbitrary",) * 3

        def store(result, extra, outs):
            n_rows, n_out_rows, sub = len(epilogue.rows), len(epilogue.out_rows), min(EPILOGUE_ROWS, tm)
            for r in range(0, tm, sub):
                rows = pl.ds(r, sub)
                epilogue.fn(result[r:r + sub], *[ref.at[rows] for ref in extra[:n_rows]], *extra[n_rows:],
                            *[ref.at[rows] for ref in outs[:n_out_rows]], *outs[n_out_rows:])

    n_extra = len(in_specs) - 2
    n_out = len(out_specs)

    def body(a_ref, b_ref, *refs):
        extra, outs, scratch_refs = refs[:n_extra], refs[n_extra:n_extra + n_out], refs[n_extra + n_out:]
        k = pl.program_id(2)
        if epilogue is not None:
            @pl.when((pl.program_id(0) == 0) & (pl.program_id(1) == 0) & (k == 0))
            def _():
                for ref in outs[n_out - epilogue.n_out_vecs:]:
                    ref[...] = jnp.zeros_like(ref)

        if gk == 1:
            store(_dot(_bf(a_ref[...]), _bf(b_ref[...]), dn), extra, outs)
            return
        acc_ref = scratch_refs[0]

        @pl.when(k == 0)
        def _():
            acc_ref[...] = jnp.zeros_like(acc_ref)

        acc_ref[...] += _dot(_bf(a_ref[...]), _bf(b_ref[...]), dn)

        @pl.when(k == gk - 1)
        def _():
            store(acc_ref, extra, outs)

    out = _hosted_call(
        body, name=name, grid=grid, in_specs=in_specs, out_specs=out_specs, out_shape=out_shapes,
        scratch=[] if gk == 1 else [pltpu.VMEM((tm, tn), F32)], args=args, semantics=semantics, exchange=exchange)
    return out[0] if (exchange is None and epilogue is None) else out


ROWS = 512


def _row_spec(tr, cols):
    return pl.BlockSpec((tr, cols), lambda i: (i, 0))


def _vec_spec(cols):
    return pl.BlockSpec((1, cols), lambda i: (0, 0))


def _norm_fwd(x, g, name, exchange=None):
    T, Dm = x.shape
    tr = min(ROWS, T)

    def body(x_ref, g_ref, h_ref):
        xh, _ = _rms(x_ref[...])
        h_ref[...] = (xh * g_ref[...]).astype(h_ref.dtype)

    out = _hosted_call(
        body, name=name, grid=(T // tr,), in_specs=[_row_spec(tr, Dm), _vec_spec(Dm)], out_specs=[_row_spec(tr, Dm)],
        out_shape=[jax.ShapeDtypeStruct((T, Dm), BF16)], scratch=[], args=(x, g), semantics=("parallel",),
        exchange=exchange)
    return out[0] if exchange is None else out


def _post_pre(x, g_post, g_pre):
    def fn(m, x_ref, gp_ref, gn_ref, xo_ref, h_ref, m_ref):
        mh, _ = _rms(m)
        xn = x_ref[...] + mh * gp_ref[...]
        xo_ref[...] = xn
        xh, _ = _rms(xn)
        h_ref[...] = (xh * gn_ref[...]).astype(h_ref.dtype)
        m_ref[...] = m.astype(m_ref.dtype)

    return _RowEpilogue(fn, [x], [g_post, g_pre], [F32, BF16, BF16], 0)


def _final(x2, target, g_post):
    def fn(y, x_ref, t_ref, g_ref, dx_ref, dy_ref, loss_ref, dg_ref):
        g = g_ref[...]
        yh, r = _rms(y)
        d = x_ref[...] + yh * g - t_ref[...]
        loss_ref[...] += 0.5 * jnp.sum(jnp.mean(d * d, axis=-1, keepdims=True))
        dx = d * (1.0 / d.shape[-1])
        dx_ref[...] = dx
        dy_ref[...] = _rms_bwd(dx * g, yh, r).astype(dy_ref.dtype)
        dg_ref[...] += jnp.sum(dx * yh, axis=0, keepdims=True)

    return _RowEpilogue(fn, [x2, target], [g_post], [F32, BF16], 2)


class _RowEpilogue:
    def __init__(self, fn, rows, vecs, out_rows, n_out_vecs):
        self.fn, self.rows, self.vecs, self.out_rows, self.n_out_vecs = fn, rows, vecs, out_rows, n_out_vecs


def _norm_bwd2(dx_cur, x_prev, m_prev, g_pre, g_post):
    def fn(dh, dx_ref, x_ref, m_ref, gn_ref, gp_ref, dxo_ref, dm_ref, dgn_ref, dgp_ref):
        xh, r = _rms(x_ref[...])
        dx = dx_ref[...] + _rms_bwd(dh * gn_ref[...], xh, r)
        dxo_ref[...] = dx
        dgn_ref[...] += jnp.sum(dh * xh, axis=0, keepdims=True)
        mh, rm = _rms(m_ref[...].astype(F32))
        dm_ref[...] = _rms_bwd(dx * gp_ref[...], mh, rm).astype(dm_ref.dtype)
        dgp_ref[...] += jnp.sum(dx * mh, axis=0, keepdims=True)

    return _RowEpilogue(fn, [dx_cur, x_prev, m_prev], [g_pre, g_post], [F32, BF16], 2)


def _norm_bwd1(dx_cur, x_prev, g_pre):
    def fn(dh, dx_ref, x_ref, gn_ref, dxo_ref, dgn_ref):
        xh, r = _rms(x_ref[...])
        dxo_ref[...] = dx_ref[...] + _rms_bwd(dh * gn_ref[...], xh, r)
        dgn_ref[...] += jnp.sum(dh * xh, axis=0, keepdims=True)

    return _RowEpilogue(fn, [dx_cur, x_prev], [g_pre], [F32], 1)


def _gain_bwd(x, dh_a, dh_b, name):
    T, Dm = x.shape

    def body(x_ref, a_ref, b_ref, dg_ref):
        xh, _ = _rms(x_ref[...])
        dg_ref[...] = jnp.sum((a_ref[...] + b_ref[...]) * xh, axis=0, keepdims=True)

    return pl.pallas_call(
        body, name=name, grid=(1,), in_specs=[_row_spec(T, Dm)] * 3, out_specs=_vec_spec(Dm),
        out_shape=jax.ShapeDtypeStruct((1, Dm), F32), compiler_params=_params("arbitrary"),
    )(x, dh_a, dh_b)


ATTN_GROUP = ATTN_Q_HEADS // ATTN_KV_HEADS
ASSEMBLE_ROWS = 1024


def _swa_mask(n):
    rows = ATTN_GROUP * ATTN_BLOCK
    row = lax.broadcasted_iota(jnp.int32, (rows, 2 * ATTN_BLOCK), 0) & (ATTN_BLOCK - 1)
    col = lax.broadcasted_iota(jnp.int32, (rows, 2 * ATTN_BLOCK), 1)
    diff = row + ATTN_BLOCK - col
    return (diff >= 0) & (diff < ATTN_BLOCK) & ((col >= ATTN_BLOCK) | (n > 0))


def _swa_rows(ref, hk, dtype):
    hd = ATTN_HEAD_DIM
    return jnp.concatenate(
        [ref[:, hd * (hk * ATTN_GROUP + g):hd * (hk * ATTN_GROUP + g + 1)].astype(dtype) for g in range(ATTN_GROUP)],
        axis=0)


def _swa_per_row(vals):
    seg = lax.broadcasted_iota(jnp.int32, (ATTN_GROUP * ATTN_BLOCK, 1), 0) // ATTN_BLOCK
    col = jnp.zeros((ATTN_GROUP * ATTN_BLOCK, 1), F32)
    for g, val in enumerate(vals):
        col = jnp.where(seg == g, val, col)
    return col


def _swa_specs():
    blk = ATTN_BLOCK
    prev = lambda n: jnp.maximum(n - 1, 0)
    return [
        pl.BlockSpec(memory_space=pltpu.SMEM),
        pl.BlockSpec((blk, 512), lambda n: (n, 0)),
        pl.BlockSpec((blk, 128), lambda n: (prev(n), 4)),
        pl.BlockSpec((blk, 128), lambda n: (n, 4)),
        pl.BlockSpec((blk, 128), lambda n: (prev(n), 5)),
        pl.BlockSpec((blk, 128), lambda n: (n, 5)),
    ]


def _swa_fwd(z, sinks, name, exchange=None):
    T = z.shape[0]
    blk, hd = ATTN_BLOCK, ATTN_HEAD_DIM
    scale = hd ** -0.5

    def body(sink_ref, q_ref, kp_ref, kc_ref, vp_ref, vc_ref, o_ref, lse_ref):
        allowed = _swa_mask(pl.program_id(0))
        hks = range(ATTN_KV_HEADS)
        kss = [slice(hd * hk, hd * hk + hd) for hk in hks]
        k = [_bf(jnp.concatenate([kp_ref[:, ks], kc_ref[:, ks]], axis=0)) for ks in kss]
        v = [_bf(jnp.concatenate([vp_ref[:, ks], vc_ref[:, ks]], axis=0)) for ks in kss]
        s = [jnp.where(allowed, _dot(_swa_rows(q_ref, hk, BF16), k[hk], NT) * scale, -1e30) for hk in hks]
        sink = [_swa_per_row([sink_ref[0, hk * ATTN_GROUP + g] for g in range(ATTN_GROUP)]) for hk in hks]
        m = [jnp.maximum(jnp.max(s[hk], axis=-1, keepdims=True), sink[hk]) for hk in hks]
        p = [jnp.exp(s[hk] - m[hk]) for hk in hks]
        l = [jnp.sum(p[hk], axis=-1, keepdims=True) + jnp.exp(sink[hk] - m[hk]) for hk in hks]
        o = [_dot(_bf(p[hk] / l[hk]), v[hk]).astype(o_ref.dtype) for hk in hks]
        for hk in hks:
            lse = m[hk] + jnp.log(l[hk])
            for g in range(ATTN_GROUP):
                h = hk * ATTN_GROUP + g
                o_ref[:, hd * h:hd * (h + 1)] = o[hk][blk * g:blk * (g + 1)]
                lse_ref[:, h:h + 1] = lse[blk * g:blk * (g + 1)]

    return _hosted_call(
        body, name=name, grid=(T // blk,), in_specs=_swa_specs(),
        out_specs=[pl.BlockSpec((blk, 512), lambda n: (n, 0)), pl.BlockSpec((blk, ATTN_Q_HEADS), lambda n: (n, 0))],
        out_shape=[jax.ShapeDtypeStruct((T, 512), BF16), jax.ShapeDtypeStruct((T, ATTN_Q_HEADS), F32)],
        scratch=[], args=(sinks, z, z, z, z, z), semantics=("parallel",), exchange=exchange)


def _swa_bwd(z, sinks, dcat, lse, name):
    T = z.shape[0]
    blk, hd = ATTN_BLOCK, ATTN_HEAD_DIM
    scale = hd ** -0.5
    group = ATTN_Q_HEADS // ATTN_KV_HEADS

    def body(sink_ref, q_ref, kp_ref, kc_ref, vp_ref, vc_ref, do_ref, lse_ref,
             dq_ref, dka_ref, dkb_ref, dva_ref, dvb_ref, dsink_ref):
        @pl.when(pl.program_id(0) == 0)
        def _():
            dsink_ref[...] = jnp.zeros_like(dsink_ref)

        allowed = _swa_mask(pl.program_id(0))
        lane = lax.broadcasted_iota(jnp.int32, (1, ATTN_Q_HEADS), 1)
        dsink = jnp.zeros((1, ATTN_Q_HEADS), F32)
        hks = range(ATTN_KV_HEADS)
        kss = [slice(hd * hk, hd * hk + hd) for hk in hks]
        k = [_bf(jnp.concatenate([kp_ref[:, ks], kc_ref[:, ks]], axis=0)) for ks in kss]
        v = [_bf(jnp.concatenate([vp_ref[:, ks], vc_ref[:, ks]], axis=0)) for ks in kss]
        qs = [_swa_rows(q_ref, hk, BF16) for hk in hks]
        dos = [_swa_rows(do_ref, hk, BF16) for hk in hks]
        lse = [jnp.concatenate([lse_ref[:, hk * group + g:hk * group + g + 1] for g in range(group)], axis=0)
               for hk in hks]
        s = [_dot(qs[hk], k[hk], NT) * scale for hk in hks]
        dp = [_dot(dos[hk], v[hk], NT) for hk in hks]
        p = [jnp.where(allowed, jnp.exp(jnp.where(allowed, s[hk], -1e30) - lse[hk]), 0.0) for hk in hks]
        delta = [jnp.sum(p[hk] * dp[hk], axis=-1, keepdims=True) for hk in hks]
        ds = [_bf(p[hk] * (dp[hk] - delta[hk]) * scale) for hk in hks]
        dq = [_dot(ds[hk], k[hk]).astype(dq_ref.dtype) for hk in hks]
        dk = [_dot(ds[hk], qs[hk], TN) for hk in hks]
        dv = [_dot(_bf(p[hk]), dos[hk], TN) for hk in hks]
        for hk in hks:
            sink = _swa_per_row([sink_ref[0, hk * group + g] for g in range(group)])
            sink_part = jnp.exp(sink - lse[hk]) * delta[hk]
            for g in range(group):
                h = hk * group + g
                dq_ref[:, hd * h:hd * (h + 1)] = dq[hk][blk * g:blk * (g + 1)]
                dsink = dsink + jnp.where(lane == h, -jnp.sum(sink_part[blk * g:blk * (g + 1)]), 0.0)
            dkb_ref[:, kss[hk]] = dk[hk][:blk]
            dka_ref[:, kss[hk]] = dk[hk][blk:]
            dvb_ref[:, kss[hk]] = dv[hk][:blk]
            dva_ref[:, kss[hk]] = dv[hk][blk:]
        dsink_ref[...] += dsink

    kv_out = pl.BlockSpec((blk, 128), lambda n: (n, 0))
    return pl.pallas_call(
        body, name=name, grid=(T // blk,),
        in_specs=_swa_specs() + [pl.BlockSpec((blk, 512), lambda n: (n, 0)),
                                 pl.BlockSpec((blk, ATTN_Q_HEADS), lambda n: (n, 0))],
        out_specs=[pl.BlockSpec((blk, 512), lambda n: (n, 0)), kv_out, kv_out, kv_out, kv_out,
                   pl.BlockSpec((1, ATTN_Q_HEADS), lambda n: (0, 0))],
        out_shape=[jax.ShapeDtypeStruct((T, 512), BF16)] + [jax.ShapeDtypeStruct((T, 128), F32)] * 4
        + [jax.ShapeDtypeStruct((1, ATTN_Q_HEADS), F32)],
        compiler_params=_params("arbitrary"),
    )(sinks, z, z, z, z, z, dcat, lse)


def _assemble_dz(dq_a, dka, dkb, dva, dvb, dqr, dfr, dir_, dgr, name):
    T = dq_a.shape[0]
    blk = ATTN_BLOCK
    rows = min(ASSEMBLE_ROWS, T)
    nb, per = T // rows, rows // blk

    def body(dq_ref, dka_ref, dkb_ref, dkn_ref, dva_ref, dvb_ref, dvn_ref, dqr_ref, dfr_ref, dir_ref, dgr_ref, o_ref):
        has_next = pl.program_id(0) < nb - 1

        def with_next(a_ref, b_ref, n_ref):
            after = jnp.where(has_next, n_ref[...], 0.0)
            shifted = after if per == 1 else jnp.concatenate([b_ref[blk:, :], after], axis=0)
            return (a_ref[...] + shifted).astype(o_ref.dtype)

        o_ref[:, 0:512] = dq_ref[...]
        o_ref[:, 512:640] = with_next(dka_ref, dkb_ref, dkn_ref)
        o_ref[:, 640:768] = with_next(dva_ref, dvb_ref, dvn_ref)
        o_ref[:, 768:1280] = dqr_ref[...]
        o_ref[:, 1280:1792] = dfr_ref[...]
        o_ref[:, 1792:2304] = dir_ref[...]
        o_ref[:, 2304:2816] = dgr_ref[...]

    cur = lambda w: pl.BlockSpec((rows, w), lambda n: (n, 0))
    nxt = pl.BlockSpec((blk, 128), lambda n: (jnp.minimum((n + 1) * per, T // blk - 1), 0))
    return pl.pallas_call(
        body, name=name, grid=(nb,),
        in_specs=[cur(512), cur(128), cur(128), nxt, cur(128), cur(128), nxt, cur(512), cur(512), cur(512), cur(512)],
        out_specs=pl.BlockSpec((rows, 2816), lambda n: (n, 0)),
        out_shape=jax.ShapeDtypeStruct((T, 2816), BF16), compiler_params=_params("parallel"),
    )(dq_a, dka, dkb, dkb, dva, dvb, dvb, dqr, dfr, dir_, dgr)


HGRN_ROWS = 512


def _hgrn_consts():
    c = HGRN_CHUNK
    r = lax.broadcasted_iota(jnp.int32, (c, c), 0)
    s = lax.broadcasted_iota(jnp.int32, (c, c), 1)
    rcol = lax.broadcasted_iota(jnp.int32, (c, 1), 0)
    same_block, upper = [], []
    for m in HGRN_LEVELS:
        same_block.append((r & ~(2 * m - 1)) == (s & ~(2 * m - 1)))
        upper.append((rcol & (2 * m - 1)) >= m)
    cum_mat = jnp.where(s <= r, 1.0, 0.0).astype(BF16)
    rev_mat = jnp.where(s >= r, 1.0, 0.0).astype(BF16)
    return cum_mat, rev_mat, r == s, same_block, upper, rcol & 3, s == r - 1


def _hgrn_level_decay(g, b, m, pos4):
    c = HGRN_CHUNK
    if m == 1:
        return jnp.exp(jnp.where((pos4 & 1) == 1, g, 0.0))
    if m == 2:
        after, before = pltpu.roll(g, c - 1, 0), pltpu.roll(g, 1, 0)
        return jnp.exp(jnp.where(pos4 == 0, after, jnp.where(pos4 == 1, 0.0, jnp.where(pos4 == 2, g, g + before))))
    b3 = b.reshape(c // (2 * m), 2 * m, HGRN_DIM)
    bref = jnp.broadcast_to(b3[:, m - 1:m, :], b3.shape).reshape(c, HGRN_DIM)
    return jnp.exp(-jnp.abs(b - bref))


def _split3(x):
    hi = _bf(x)
    r1 = x - hi.astype(F32)
    mid = _bf(r1)
    lo = _bf(r1 - mid.astype(F32))
    return jnp.concatenate([hi, mid, lo], axis=1)


def _dot_hilo(a, b):
    r, c = a.shape[0], b.shape[1]
    a_hi, b_hi = _bf(a), _bf(b)
    a2 = jnp.concatenate([a_hi, _bf(a - a_hi.astype(F32))], axis=0)
    b2 = jnp.concatenate([b_hi, _bf(b - b_hi.astype(F32))], axis=1)
    y = _dot(a2, b2)
    return y[:r, :c] + y[:r, c:] + y[r:, :c]


def _fold3(y):
    w = y.shape[1] // 3
    return y[:, :w] + y[:, w:2 * w] + y[:, 2 * w:]


def _hgrn_gates(qr, fr, lb):
    sq = _sigmoid(qr)
    q = qr * sq * (HGRN_DIM ** -0.5)
    sf = _sigmoid(fr)
    f = lb + (1.0 - lb) * sf
    k = (1.0 - lb) * _sigmoid(-fr)
    return q, sq, sf, f, k, jnp.log(f)


def _hgrn_intra(q, k, g, b, consts, scores=True):
    _, _, eye, same_block, upper, pos4, below = consts
    heads = range(len(q))
    a = None
    if scores:
        a = [jnp.where(eye, jnp.sum(q[hh] * k[hh], axis=1, keepdims=True), 0.0) for hh in heads]
    saved = [[] for _ in heads]
    for i, m in enumerate(HGRN_LEVELS):
        up = upper[i]
        e = [_hgrn_level_decay(g[hh], b[hh], m, pos4) for hh in heads]
        qt = [jnp.where(up, q[hh] * e[hh], 0.0) for hh in heads]
        kt = [jnp.where(up, 0.0, k[hh] * e[hh]) for hh in heads]
        for hh in heads:
            saved[hh].append((e[hh], qt[hh], kt[hh]))
        if not scores:
            continue
        if m == 1:
            for hh in heads:
                pair = jnp.sum(qt[hh] * pltpu.roll(kt[hh], 1, 0), axis=1, keepdims=True)
                a[hh] = a[hh] + jnp.where(below, pair, 0.0)
            continue
        p = [_dot(_bf(qt[hh]), _bf(kt[hh]), NT) for hh in heads]
        for hh in heads:
            a[hh] = a[hh] + jnp.where(same_block[i], p[hh], 0.0)
    return a, saved


def _hgrn_specs(tb, nb, rev):
    tmap = (lambda t: nb - 1 - t) if rev else (lambda t: t)
    assert HGRN_PAIR == HGRN_HEADS
    return [pl.BlockSpec((tb, 2816), lambda h, t: (tmap(t), 0)),
            pl.BlockSpec((1, HGRN_PAIR * HGRN_DIM), lambda h, t: (0, h)),
            pl.BlockSpec((1, HGRN_DIM), lambda h, t: (0, 0))]


def _hgrn_z(z_ref, sl, base, head):
    return z_ref[sl, base + HGRN_DIM * head:base + HGRN_DIM * (head + 1)].astype(F32)


def _hgrn_fwd(z, lb, onw, name, exchange=None):
    T = z.shape[0]
    tb = min(HGRN_ROWS, T)
    nb, c, nc = T // tb, HGRN_CHUNK, min(HGRN_ROWS, T) // HGRN_CHUNK

    def body(z_ref, lb_ref, onw_ref, rec_ref, o_ref, st_ref, a_ref, state):
        @pl.when(pl.program_id(1) == 0)
        def _():
            state[...] = jnp.zeros_like(state)

        consts = _hgrn_consts()
        lbv = lb_ref[...]
        onwv = onw_ref[...]

        def chunk(ci, carry):
            sl = pl.ds(pl.multiple_of(ci * c, c), c)
            heads = range(HGRN_PAIR)
            lss = [slice(HGRN_DIM * hh, HGRN_DIM * (hh + 1)) for hh in heads]
            gates = [_hgrn_gates(_hgrn_z(z_ref, sl, Z_Q, hh), _hgrn_z(z_ref, sl, Z_F, hh), lbv[:, lss[hh]])
                     for hh in heads]
            q, k, g = [t[0] for t in gates], [t[4] for t in gates], [t[5] for t in gates]
            v = [_bf(_hgrn_z(z_ref, sl, Z_I, hh)) for hh in heads]
            b = [_fold3(_dot(consts[0], _split3(g[hh]))) for hh in heads]
            a, _ = _hgrn_intra(q, k, g, b, consts)
            st = [state[hh] for hh in heads]
            for hh in heads:
                st_ref[hh, ci] = st[hh]
            bl = [b[hh][c - 1:c, :] for hh in heads]
            o_state = [_dot(_bf(q[hh] * jnp.exp(b[hh])), _bf(st[hh]), NT) for hh in heads]
            kv = [_dot(v[hh], _bf(k[hh] * jnp.exp(bl[hh] - b[hh])), TN) for hh in heads]
            a = [_bf(a[hh]) for hh in heads]
            o = [_dot(a[hh], v[hh]) + o_state[hh] for hh in heads]
            for hh in heads:
                a_ref[sl, c * hh:c * (hh + 1)] = a[hh]
                state[hh] = st[hh] * jnp.exp(bl[hh]) + kv[hh]
                o_ref[sl, lss[hh]] = o[hh]
                oh, _ = _rms(o[hh])
                gr = _hgrn_z(z_ref, sl, Z_G, hh)
                rec_ref[sl, lss[hh]] = (oh * onwv * (gr * _sigmoid(gr))).astype(rec_ref.dtype)
            return carry

        lax.fori_loop(0, nc, chunk, 0)

    in_specs = _hgrn_specs(tb, nb, False)
    out_blk = pl.BlockSpec((tb, HGRN_PAIR * HGRN_DIM), lambda h, t: (t, h))
    return _hosted_call(
        body, name=name, grid=(HGRN_HEADS // HGRN_PAIR, nb), in_specs=in_specs,
        out_specs=[out_blk, out_blk, pl.BlockSpec((HGRN_PAIR, nc, HGRN_DIM, HGRN_DIM), lambda h, t: (h, t, 0, 0)),
                   pl.BlockSpec((tb, HGRN_PAIR * c), lambda h, t: (t, h))],
        out_shape=[jax.ShapeDtypeStruct((T, 512), BF16), jax.ShapeDtypeStruct((T, 512), F32),
                   jax.ShapeDtypeStruct((HGRN_HEADS, T // c, HGRN_DIM, HGRN_DIM), F32),
                   jax.ShapeDtypeStruct((T, HGRN_HEADS * c), BF16)],
        scratch=[pltpu.VMEM((HGRN_PAIR, HGRN_DIM, HGRN_DIM), F32)], args=(z, lb, onw),
        semantics=("parallel", "arbitrary"), exchange=exchange)


def _hgrn_bwd(z, lb, onw, o, states, scores, dcat, name, exchange=None):
    T = z.shape[0]
    tb = min(HGRN_ROWS, T)
    nb, c, nc = T // tb, HGRN_CHUNK, min(HGRN_ROWS, T) // HGRN_CHUNK

    def body(z_ref, lb_ref, onw_ref, o_ref, st_ref, drec_ref, a_ref,
             dqr_ref, dfr_ref, dir_ref, dgr_ref, dlb_ref, donw_ref, dstate):
        @pl.when(pl.program_id(1) == 0)
        def _():
            dstate[...] = jnp.zeros_like(dstate)
            dlb_ref[...] = jnp.zeros_like(dlb_ref)

        @pl.when((pl.program_id(0) == 0) & (pl.program_id(1) == 0))
        def _():
            donw_ref[...] = jnp.zeros_like(donw_ref)

        consts = _hgrn_consts()
        rev_mat, eye, same_block, upper = consts[1:5]
        below = consts[6]
        lbv = lb_ref[...]
        onwv = onw_ref[...]
        last = lax.broadcasted_iota(jnp.int32, (c, 1), 0) == c - 1

        def chunk(i, carry):
            ci = nc - 1 - i
            sl = pl.ds(pl.multiple_of(ci * c, c), c)
            hs = range(HGRN_PAIR)
            lss = [slice(HGRN_DIM * hh, HGRN_DIM * (hh + 1)) for hh in hs]
            qr = [_hgrn_z(z_ref, sl, Z_Q, hh) for hh in hs]
            gates = [_hgrn_gates(qr[hh], _hgrn_z(z_ref, sl, Z_F, hh), lbv[:, lss[hh]]) for hh in hs]
            q, sq, sf, f, k, g = ([t[j] for t in gates] for j in range(6))
            v = [_bf(_hgrn_z(z_ref, sl, Z_I, hh)) for hh in hs]
            b = [_fold3(_dot(consts[0], _split3(g[hh]))) for hh in hs]
            _, saved = _hgrn_intra(q, k, g, b, consts, scores=False)
            a = [a_ref[sl, c * hh:c * (hh + 1)] for hh in hs]
            st = [st_ref[hh, ci] for hh in hs]
            dst = [dstate[hh] for hh in hs]

            gr = [_hgrn_z(z_ref, sl, Z_G, hh) for hh in hs]
            sg = [_sigmoid(gr[hh]) for hh in hs]
            norm = [_rms(o_ref[sl, ls]) for ls in lss]
            oh, r = [t[0] for t in norm], [t[1] for t in norm]
            drec = [drec_ref[sl, ls].astype(F32) for ls in lss]
            don = [drec[hh] * (gr[hh] * sg[hh]) for hh in hs]
            do = [_bf(_rms_bwd(don[hh] * onwv, oh[hh], r[hh])) for hh in hs]
            donw = jnp.sum(don[0] * oh[0], axis=0, keepdims=True)
            for hh in hs:
                dgr_ref[sl, lss[hh]] = (drec[hh] * oh[hh] * onwv
                                        * (sg[hh] * (1.0 + gr[hh] * (1.0 - sg[hh])))).astype(dgr_ref.dtype)
                if hh:
                    donw = donw + jnp.sum(don[hh] * oh[hh], axis=0, keepdims=True)
            donw_ref[...] += donw

            eb = [jnp.exp(b[hh]) for hh in hs]
            bl = [b[hh][c - 1:c, :] for hh in hs]
            ebl = [jnp.exp(bl[hh]) for hh in hs]
            ekb = [jnp.exp(bl[hh] - b[hh]) for hh in hs]
            qe = [q[hh] * eb[hh] for hh in hs]
            ke = [k[hh] * ekb[hh] for hh in hs]
            da = [_dot(do[hh], v[hh], NT) for hh in hs]
            dat = [_dot(v[hh], do[hh], NT) for hh in hs]
            dqe = [_dot(do[hh], _bf(st[hh])) for hh in hs]
            dke = [_dot(v[hh], _bf(dst[hh])) for hh in hs]
            dv_a = [_dot(a[hh], do[hh], TN) for hh in hs]
            dv_s = [_dot(_bf(ke[hh]), _bf(dst[hh]), NT) for hh in hs]
            dst_in = [_dot(do[hh], _bf(qe[hh]), TN) for hh in hs]
            dad = [jnp.sum(jnp.where(eye, da[hh], 0.0), axis=1, keepdims=True) for hh in hs]
            dq = [dqe[hh] * eb[hh] + dad[hh] * k[hh] for hh in hs]
            dk = [dke[hh] * ekb[hh] + dad[hh] * q[hh] for hh in hs]
            db_last = [jnp.sum(dke[hh] * ke[hh], axis=0, keepdims=True)
                       + jnp.sum(dst[hh] * st[hh], axis=0, keepdims=True) * ebl[hh] for hh in hs]
            for hh in hs:
                dstate[hh] = dst[hh] * ebl[hh] + dst_in[hh]
                dir_ref[sl, lss[hh]] = (dv_a[hh] + dv_s[hh]).astype(dir_ref.dtype)
            for lvl, m in enumerate(HGRN_LEVELS):
                if m == 1:
                    pair = [jnp.sum(jnp.where(below, da[hh], 0.0), axis=1, keepdims=True) for hh in hs]
                    xq = [pair[hh] * pltpu.roll(saved[hh][lvl][2], 1, 0) for hh in hs]
                    xk = [pltpu.roll(pair[hh] * saved[hh][lvl][1], c - 1, 0) for hh in hs]
                else:
                    xq = [_dot_hilo(jnp.where(same_block[lvl], da[hh], 0.0), saved[hh][lvl][2]) for hh in hs]
                    xk = [_dot_hilo(jnp.where(same_block[lvl], dat[hh], 0.0), saved[hh][lvl][1]) for hh in hs]
                for hh in hs:
                    e = saved[hh][lvl][0]
                    dq[hh] = dq[hh] + jnp.where(upper[lvl], xq[hh] * e, 0.0)
                    dk[hh] = dk[hh] + jnp.where(upper[lvl], 0.0, xk[hh] * e)
            db = [q[hh] * dq[hh] - k[hh] * dk[hh] + jnp.where(last, db_last[hh], 0.0) for hh in hs]
            dg = [_fold3(_dot(rev_mat, _split3(db[hh]))) for hh in hs]

            for hh in hs:
                ls = lss[hh]
                dqr_ref[sl, ls] = (dq[hh] * (HGRN_DIM ** -0.5)
                                   * (sq[hh] * (1.0 + qr[hh] * (1.0 - sq[hh])))).astype(dqr_ref.dtype)
                dfk = dg[hh] / f[hh] - dk[hh]
                dfr_ref[sl, ls] = ((1.0 - lbv[:, ls]) * sf[hh] * (1.0 - sf[hh]) * dfk).astype(dfr_ref.dtype)
                dlb_ref[:, ls] += jnp.sum((1.0 - sf[hh]) * dfk, axis=0, keepdims=True)
            return carry

        lax.fori_loop(0, nc, chunk, 0)

    in_specs = _hgrn_specs(tb, nb, True)
    rblk = pl.BlockSpec((tb, HGRN_PAIR * HGRN_DIM), lambda h, t: (nb - 1 - t, h))
    in_specs = in_specs + [
        rblk,
        pl.BlockSpec((HGRN_PAIR, nc, HGRN_DIM, HGRN_DIM), lambda h, t: (h, nb - 1 - t, 0, 0)),
        pl.BlockSpec((tb, HGRN_PAIR * HGRN_DIM), lambda h, t: (nb - 1 - t, 4 // HGRN_PAIR + h)),
        pl.BlockSpec((tb, HGRN_PAIR * c), lambda h, t: (nb - 1 - t, h)),
    ]
    return _hosted_call(
        body, name=name, grid=(HGRN_HEADS // HGRN_PAIR, nb), in_specs=in_specs,
        out_specs=[rblk, rblk, rblk, rblk, pl.BlockSpec((1, HGRN_PAIR * HGRN_DIM), lambda h, t: (0, h)),
                   pl.BlockSpec((1, HGRN_DIM), lambda h, t: (0, 0))],
        out_shape=[jax.ShapeDtypeStruct((T, 512), BF16)] * 4
        + [jax.ShapeDtypeStruct((1, 512), F32), jax.ShapeDtypeStruct((1, HGRN_DIM), F32)],
        scratch=[pltpu.VMEM((HGRN_PAIR, HGRN_DIM, HGRN_DIM), F32)], args=(z, lb, onw, o, states, dcat, scores),
        semantics=("arbitrary", "arbitrary"), exchange=exchange)


def _lower_bound(logits, name):
    def body(l_ref, lb_ref):
        l0, l1 = l_ref[0:1, :], l_ref[1:2, :]
        m = jnp.maximum(l0, l1)
        e0, e1 = jnp.exp(l0 - m), jnp.exp(l1 - m)
        lb_ref[...] = e0 / (e0 + e1)

    return pl.pallas_call(
        body, name=name, out_shape=jax.ShapeDtypeStruct((1, logits.shape[1]), F32),
    )(logits)


def _lower_bound_bwd(lb, dlb, name):
    def body(lb_ref, dlb_ref, dl_ref):
        p = lb_ref[...]
        d0 = dlb_ref[...] * p * (1.0 - p)
        dl_ref[0:1, :] = d0
        dl_ref[1:2, :] = -d0

    return pl.pallas_call(
        body, name=name, out_shape=jax.ShapeDtypeStruct((2, lb.shape[1]), F32),
    )(lb, dlb)


CA_ROWS = 512


def _ca_fwd(q, k, v, name):
    T, W = q.shape
    M = k.shape[0]
    tq = min(CA_ROWS, T)
    scale = CA_HEAD_DIM ** -0.5

    def body(q_ref, k_ref, v_ref, o_ref):
        for h in range(CA_HEADS):
            hs = slice(CA_HEAD_DIM * h, CA_HEAD_DIM * (h + 1))
            s = _dot(q_ref[:, hs], k_ref[:, hs], NT) * scale
            p = jnp.exp(s - jnp.max(s, axis=-1, keepdims=True))
            p = p / jnp.sum(p, axis=-1, keepdims=True)
            o_ref[:, hs] = _dot(_bf(p), v_ref[:, hs]).astype(o_ref.dtype)

    full = pl.BlockSpec((M, W), lambda i: (0, 0))
    return pl.pallas_call(
        body, name=name, grid=(T // tq,), in_specs=[_row_spec(tq, W), full, full], out_specs=_row_spec(tq, W),
        out_shape=jax.ShapeDtypeStruct((T, W), BF16), compiler_params=_params("parallel"),
    )(q, k, v)


def _ca_bwd(q, k, v, do, name):
    T, W = q.shape
    M = k.shape[0]
    tq = min(CA_ROWS, T)
    scale = CA_HEAD_DIM ** -0.5

    def body(q_ref, k_ref, v_ref, do_ref, dq_ref, dk_ref, dv_ref):
        @pl.when(pl.program_id(0) == 0)
        def _():
            dk_ref[...] = jnp.zeros_like(dk_ref)
            dv_ref[...] = jnp.zeros_like(dv_ref)

        for h in range(CA_HEADS):
            hs = slice(CA_HEAD_DIM * h, CA_HEAD_DIM * (h + 1))
            qh, kh, vh, doh = q_ref[:, hs], k_ref[:, hs], v_ref[:, hs], do_ref[:, hs]
            s = _dot(qh, kh, NT) * scale
            p = jnp.exp(s - jnp.max(s, axis=-1, keepdims=True))
            p = p / jnp.sum(p, axis=-1, keepdims=True)
            dp = _dot(doh, vh, NT)
            ds = _bf(p * (dp - jnp.sum(p * dp, axis=-1, keepdims=True)) * scale)
            dq_ref[:, hs] = _dot(ds, kh).astype(dq_ref.dtype)
            dk_ref[:, hs] += _dot(ds, qh, TN)
            dv_ref[:, hs] += _dot(_bf(p), doh, TN)

    full = pl.BlockSpec((M, W), lambda i: (0, 0))
    return pl.pallas_call(
        body, name=name, grid=(T // tq,), in_specs=[_row_spec(tq, W), full, full, _row_spec(tq, W)],
        out_specs=[_row_spec(tq, W), full, full],
        out_shape=[jax.ShapeDtypeStruct((T, W), BF16), jax.ShapeDtypeStruct((M, W), F32), jax.ShapeDtypeStruct((M, W), F32)],
        compiler_params=_params("arbitrary"),
    )(q, k, v, do)


FFN_ROWS = 256
FFN_COLS = 1408
GELU_C0 = 0.7978845608028654
GELU_C1 = 0.044715


def _gelu(x):
    t = jnp.tanh(GELU_C0 * (x + GELU_C1 * x * x * x))
    return 0.5 * x * (1.0 + t), t


def _gelu_grad(x, t):
    return 0.5 * (1.0 + t) + 0.5 * x * (1.0 - t * t) * GELU_C0 * (1.0 + 3.0 * GELU_C1 * x * x)


def _shift_down(cur, halo, first, tb):
    row = lax.broadcasted_iota(jnp.int32, (tb, 1), 0)
    h6 = jnp.where(first, 0.0, halo[6:7])
    h7 = jnp.where(first, 0.0, halo[7:8])
    u1 = jnp.where(row == 0, h7, pltpu.roll(cur, 1, 0))
    u2 = jnp.where(row == 0, h6, jnp.where(row == 1, h7, pltpu.roll(cur, 2, 0)))
    return u1, u2


def _conv(u_ref, halo_ref, w_ref, b_ref, half, first, tb):
    cur = u_ref[half]
    u1, u2 = _shift_down(cur, halo_ref[half], first, tb)
    w = w_ref[...]
    return w[0:1] * u2 + w[1:2] * u1 + w[2:3] * cur + b_ref[...], cur, u1, u2


def _ffn_specs(tb, tc, rows_first):
    nj = D_FF // tc
    rc = (lambda a, b: (a, b)) if rows_first else (lambda a, b: (b, a))
    def at(f):
        return lambda a, b: f(*rc(a, b))
    blk = pl.BlockSpec((2, tb, tc), at(lambda t, j: (0, t, j)))
    halo = pl.BlockSpec((2, 8, tc), at(lambda t, j: (0, jnp.maximum(t * (tb // 8) - 1, 0), j)))
    wg = pl.BlockSpec((3, tc), at(lambda t, j: (0, j)))
    wv = pl.BlockSpec((3, tc), at(lambda t, j: (0, j + nj)))
    bg = pl.BlockSpec((1, tc), at(lambda t, j: (0, j)))
    bv = pl.BlockSpec((1, tc), at(lambda t, j: (0, j + nj)))
    flat = pl.BlockSpec((tb, tc), at(lambda t, j: (t, j)))
    return blk, halo, wg, wv, bg, bv, flat


def _glu_fwd(u, cw, cb, name):
    T = u.shape[1]
    tb, tc = min(FFN_ROWS, T), FFN_COLS

    def body(u_ref, halo_ref, wg_ref, wv_ref, bg_ref, bv_ref, a_ref):
        first = pl.program_id(0) == 0
        cg = _conv(u_ref, halo_ref, wg_ref, bg_ref, 0, first, tb)[0]
        cv = _conv(u_ref, halo_ref, wv_ref, bv_ref, 1, first, tb)[0]
        a_ref[...] = (_gelu(cg)[0] * cv).astype(a_ref.dtype)

    blk, halo, wg, wv, bg, bv, flat = _ffn_specs(tb, tc, True)
    return pl.pallas_call(
        body, name=name, grid=(T // tb, D_FF // tc), in_specs=[blk, halo, wg, wv, bg, bv], out_specs=flat,
        out_shape=jax.ShapeDtypeStruct((T, D_FF), BF16), compiler_params=_params("parallel", "parallel"),
    )(u, u, cw, cw, cb, cb)


def _glu_bwd(u, cw, cb, da, name, exchange=None):
    T = u.shape[1]
    tb, tc = min(FFN_ROWS, T), FFN_COLS

    def body(u_ref, halo_ref, wg_ref, wv_ref, bg_ref, bv_ref, da_ref, dc_ref, db_ref, dw_ref):
        first = pl.program_id(1) == 0

        @pl.when(first)
        def _():
            db_ref[...] = jnp.zeros_like(db_ref)
            dw_ref[...] = jnp.zeros_like(dw_ref)

        cg, ug, ug1, ug2 = _conv(u_ref, halo_ref, wg_ref, bg_ref, 0, first, tb)
        cv, uv, uv1, uv2 = _conv(u_ref, halo_ref, wv_ref, bv_ref, 1, first, tb)
        da = da_ref[...]
        gl, t = _gelu(cg)
        dcg = da * cv * _gelu_grad(cg, t)
        dcv = da * gl
        dc_ref[0] = dcg
        dc_ref[1] = dcv
        for half, dc, taps in ((0, dcg, (ug2, ug1, ug)), (1, dcv, (uv2, uv1, uv))):
            db_ref[half] += jnp.sum(dc, axis=0, keepdims=True)
            for tap in range(3):
                dw_ref[half, tap:tap + 1, :] += jnp.sum(dc * taps[tap], axis=0, keepdims=True)

    blk, halo, wg, wv, bg, bv, flat = _ffn_specs(tb, tc, False)
    return _hosted_call(
        body, name=name, grid=(D_FF // tc, T // tb), in_specs=[blk, halo, wg, wv, bg, bv, flat],
        out_specs=[blk, pl.BlockSpec((2, 1, tc), lambda j, t: (0, 0, j)), pl.BlockSpec((2, 3, tc), lambda j, t: (0, 0, j))],
        out_shape=[jax.ShapeDtypeStruct((2, T, D_FF), F32), jax.ShapeDtypeStruct((2, 1, D_FF), F32),
                   jax.ShapeDtypeStruct((2, 3, D_FF), F32)],
        scratch=[], args=(u, u, cw, cw, cb, cb, da), semantics=("parallel", "arbitrary"), exchange=exchange)


def _conv_bwd(dc, cw, name):
    T = dc.shape[1]
    tb, tc = min(FFN_ROWS, T), FFN_COLS
    nt, nj = T // tb, D_FF // tc

    def body(dc_ref, halo_ref, wg_ref, wv_ref, du_ref):
        last = pl.program_id(0) == nt - 1
        row = lax.broadcasted_iota(jnp.int32, (tb, 1), 0)
        for half, w_ref in ((0, wg_ref), (1, wv_ref)):
            cur = dc_ref[half]
            halo = halo_ref[half]
            h0 = jnp.where(last, 0.0, halo[0:1])
            h1 = jnp.where(last, 0.0, halo[1:2])
            d1 = jnp.where(row == tb - 1, h0, pltpu.roll(cur, tb - 1, 0))
            d2 = jnp.where(row == tb - 1, h1, jnp.where(row == tb - 2, h0, pltpu.roll(cur, tb - 2, 0)))
            w = w_ref[...]
            du_ref[half] = (w[2:3] * cur + w[1:2] * d1 + w[0:1] * d2).astype(du_ref.dtype)

    blk = pl.BlockSpec((2, tb, tc), lambda t, j: (0, t, j))
    halo = pl.BlockSpec((2, 8, tc), lambda t, j: (0, jnp.minimum((t + 1) * (tb // 8), T // 8 - 1), j))
    wg = pl.BlockSpec((3, tc), lambda t, j: (0, j))
    wv = pl.BlockSpec((3, tc), lambda t, j: (0, j + nj))
    return pl.pallas_call(
        body, name=name, grid=(nt, nj), in_specs=[blk, halo, wg, wv], out_specs=blk,
        out_shape=jax.ShapeDtypeStruct((2, T, D_FF), BF16), compiler_params=_params("parallel", "parallel"),
    )(dc, dc, cw, cw)


def _mesh_pos():
    return lax.axis_index("x"), lax.axis_index("y"), lax.axis_index("c")


def _peer(pos, k):
    return (pos[0] ^ ((k >> 2) & 1), pos[1] ^ ((k >> 1) & 1), pos[2] ^ (k & 1))


def _index(pos):
    return 4 * pos[0] + 2 * pos[1] + pos[2]


class _Exchange:
    def __init__(self, kind, buf, relay=False):
        assert kind in ("gather", "scatter") and not (relay and kind == "scatter")
        self.kind, self.buf, self.relay = kind, buf, relay
        self.out_shape = jax.ShapeDtypeStruct(((N_DEV,) + buf.shape) if kind == "gather" else buf.shape, buf.dtype)
        self.spec = pl.BlockSpec(memory_space=pl.ANY)
        self.scratch = [pltpu.SemaphoreType.DMA((N_DEV - 1,)), pltpu.SemaphoreType.DMA((N_DEV - 1,)),
                        pltpu.SemaphoreType.DMA]

    def _src(self, x_ref, dest):
        return x_ref if self.kind == "gather" else x_ref.at[dest]

    def _copies(self, x_ref, out_ref, send_sems, recv_sems, local_sem):
        pos = _mesh_pos()
        me = _index(pos)
        local = pltpu.make_async_copy(self._src(x_ref, me), out_ref.at[me], local_sem)
        sends, recvs = [], []
        for k in range(1, N_DEV):
            peer = _peer(pos, k)
            sends.append(pltpu.make_async_remote_copy(
                src_ref=self._src(x_ref, _index(peer)), dst_ref=out_ref.at[me], send_sem=send_sems.at[k - 1],
                recv_sem=recv_sems.at[k - 1], device_id=peer, device_id_type=pl.DeviceIdType.MESH))
            recvs.append(pltpu.make_async_remote_copy(
                src_ref=self._src(x_ref, me), dst_ref=out_ref.at[_index(peer)], send_sem=send_sems.at[k - 1],
                recv_sem=recv_sems.at[k - 1], device_id=peer, device_id_type=pl.DeviceIdType.MESH))
        return local, sends, recvs

    def _relay_copies(self, x_ref, out_ref, send_sems, recv_sems, local_sem):
        x, y, c = _mesh_pos()
        me, sibling = (x, y, c), (x, y, 1 - c)
        chips = [(1 - x, y), (x, 1 - y), (1 - x, 1 - y)]

        def copy(k, block, to, own=False):
            return pltpu.make_async_remote_copy(
                src_ref=x_ref if own else out_ref.at[_index(block)], dst_ref=out_ref.at[_index(block)],
                send_sem=send_sems.at[k], recv_sem=recv_sems.at[k], device_id=to, device_id_type=pl.DeviceIdType.MESH)

        local = pltpu.make_async_copy(x_ref, out_ref.at[_index(me)], local_sem)
        first = [copy(0, me, sibling, own=True)] + [copy(1 + j, me, (*chip, c), own=True) for j, chip in enumerate(chips)]
        landed = [copy(1 + j, (*chip, c), me) for j, chip in enumerate(chips)]
        passed = [copy(4 + j, (*chip, c), sibling) for j, chip in enumerate(chips)]
        from_sibling = [copy(0, sibling, me)] + [copy(4 + j, (*chip, 1 - c), me) for j, chip in enumerate(chips)]
        return local, first, landed, passed, from_sibling

    def start(self, *refs):
        if self.relay:
            local, first = self._relay_copies(*refs)[:2]
            local.start()
            for cp in first:
                cp.start()
            return
        local, sends, _ = self._copies(*refs)
        local.start()
        for cp in sends:
            cp.start()

    def finish(self, *refs):
        if self.relay:
            local, first, landed, passed, from_sibling = self._relay_copies(*refs)
            for got, forward in zip(landed, passed):
                got.wait_recv()
                forward.start()
            for cp in from_sibling:
                cp.wait_recv()
            for cp in first + passed:
                cp.wait_send()
            local.wait()
            return
        local, sends, recvs = self._copies(*refs)
        for cp in recvs:
            cp.wait_recv()
        for cp in sends:
            cp.wait_send()
        local.wait()


def _hosted_call(body, *, name, grid, in_specs, out_specs, out_shape, scratch, args, semantics, exchange=None):
    if exchange is None:
        return pl.pallas_call(
            body, name=name, grid=grid, in_specs=in_specs, out_specs=out_specs, out_shape=out_shape,
            scratch_shapes=scratch, compiler_params=_params(*semantics))(*args)
    n_in, n_out, n_scr = len(in_specs), len(out_specs), len(scratch)

    def hosted(*refs):
        ins, x_ref = refs[:n_in], refs[n_in]
        outs, land_ref = refs[n_in + 1:n_in + 1 + n_out], refs[n_in + 1 + n_out]
        rest = refs[n_in + n_out + 2:]
        sems = rest[n_scr:]
        ids = [pl.program_id(a) for a in range(len(grid))]
        first, last = ids[0] == 0, ids[0] == grid[0] - 1
        for a in range(1, len(grid)):
            first, last = first & (ids[a] == 0), last & (ids[a] == grid[a] - 1)

        @pl.when(first)
        def _():
            exchange.start(x_ref, land_ref, *sems)

        body(*ins, *outs, *rest[:n_scr])

        @pl.when(last)
        def _():
            exchange.finish(x_ref, land_ref, *sems)

    return pl.pallas_call(
        hosted, name=name, grid=grid, in_specs=list(in_specs) + [exchange.spec],
        out_specs=list(out_specs) + [exchange.spec], out_shape=list(out_shape) + [exchange.out_shape],
        scratch_shapes=list(scratch) + exchange.scratch, compiler_params=_params(*(["arbitrary"] * len(grid))),
    )(*args, exchange.buf)


def _exchange_alone(exchange, name):
    def body(x_ref, out_ref, send_sems, recv_sems, local_sem):
        exchange.start(x_ref, out_ref, send_sems, recv_sems, local_sem)
        exchange.finish(x_ref, out_ref, send_sems, recv_sems, local_sem)

    return pl.pallas_call(
        body, name=name, out_shape=exchange.out_shape, in_specs=[exchange.spec], out_specs=exchange.spec,
        scratch_shapes=exchange.scratch)(exchange.buf)


def _adamw(w, g, m, v):
    m = ADAM_B1 * m + (1.0 - ADAM_B1) * g
    v = ADAM_B2 * v + (1.0 - ADAM_B2) * (g * g)
    m_hat = m / (1.0 - ADAM_B1 ** ADAM_STEP)
    v_hat = v / (1.0 - ADAM_B2 ** ADAM_STEP)
    delta = -ADAM_LR * (m_hat / (jnp.sqrt(v_hat) + ADAM_EPS) + ADAM_WD * w)
    return delta, m, v


def _sum_rows(parts, r0, rows, name, wmv=None):
    C = parts.shape[2]
    tr = max(t for t in range(16, ROWS + 1, 16) if rows % t == 0 and r0 % t == 0)

    def total(p_ref):
        g = p_ref[0].astype(F32)
        for i in range(1, N_DEV):
            g = g + p_ref[i].astype(F32)
        return g

    p_spec = pl.BlockSpec((N_DEV, tr, C), lambda i: (0, r0 // tr + i, 0))
    if wmv is None:
        def body(p_ref, g_ref):
            g_ref[...] = total(p_ref)

        return pl.pallas_call(
            body, name=name, grid=(rows // tr,), in_specs=[p_spec], out_specs=_row_spec(tr, C),
            out_shape=jax.ShapeDtypeStruct((rows, C), F32), compiler_params=_params("parallel"))(parts)

    def body(p_ref, w_ref, m_ref, v_ref, g_ref, d_ref, mo_ref, vo_ref):
        g = total(p_ref)
        g_ref[0] = g
        d_ref[0], mo_ref[0], vo_ref[0] = _adamw(w_ref[0], g, m_ref[0], v_ref[0])

    blk = pl.BlockSpec((1, tr, C), lambda i: (0, i, 0))
    return pl.pallas_call(
        body, name=name, grid=(rows // tr,), in_specs=[p_spec, blk, blk, blk], out_specs=[blk] * 4,
        out_shape=[jax.ShapeDtypeStruct((1, rows, C), F32)] * 4, compiler_params=_params("parallel"))(parts, *wmv)


def _sum_parts(parts, name):
    _, R, C = parts.shape

    def body(p_ref, g_ref):
        g = p_ref[0]
        for i in range(1, N_DEV):
            g = g + p_ref[i]
        g_ref[...] = g

    return pl.pallas_call(body, name=name, out_shape=jax.ShapeDtypeStruct((R, C), F32))(parts)


def _adamw_call(w, g, m, v, name):
    _, R, C = w.shape
    tr = min(ROWS, R)

    def body(w_ref, g_ref, m_ref, v_ref, d_ref, mo_ref, vo_ref):
        d_ref[...], mo_ref[...], vo_ref[...] = _adamw(w_ref[...], g_ref[...], m_ref[...], v_ref[...])

    blk = pl.BlockSpec((1, tr, C), lambda i: (0, i, 0))
    return pl.pallas_call(
        body, name=name, grid=(R // tr,), in_specs=[blk] * 4, out_specs=[blk] * 3,
        out_shape=[jax.ShapeDtypeStruct(w.shape, F32)] * 3, compiler_params=_params("parallel"))(w, g, m, v)


NORMS = ("mix_pre_norm", "mix_post_norm", "ca_pre_norm", "mem_norm", "ca_post_norm", "ffn_pre_norm", "ffn_post_norm")
SMALL = ("mix_pre_norm", "attn_sinks", "hgrn_lb_logits", "hgrn_out_norm", "mix_post_norm", "ca_pre_norm", "mem_norm",
         "ca_post_norm", "ffn_pre_norm", "ffn_conv_w", "ffn_conv_b", "ffn_post_norm")
SMALL_ROWS = 40
ROW_LOGITS, ROW_MISC, ROW_CONV_B, ROW_CONV_W = 7, 8, 9, 15
LANE_SINKS, LANE_LOSS = 128, 256
FF_PIECES = ((0, 1024), (1024, 2048), (2048, D_FF))


def _pack_small(norm_grads, dlogits, donw, dsinks, loss, d_cb, d_cw, name):
    def body(*refs):
        norm_refs = refs[:len(NORMS)]
        dl_ref, donw_ref, dsink_ref, loss_ref, cb_ref, cw_ref, out_ref = refs[len(NORMS):]
        out_ref[...] = jnp.zeros_like(out_ref)
        for i, ref in enumerate(norm_refs):
            out_ref[i:i + 1, :] = ref[...]
        out_ref[ROW_LOGITS:ROW_LOGITS + 1, 0:512] = dl_ref[0:1, :]
        out_ref[ROW_LOGITS:ROW_LOGITS + 1, 512:1024] = dl_ref[1:2, :]
        out_ref[ROW_MISC:ROW_MISC + 1, 0:HGRN_DIM] = donw_ref[...]
        out_ref[ROW_MISC:ROW_MISC + 1, LANE_SINKS:LANE_SINKS + ATTN_Q_HEADS] = dsink_ref[...]
        out_ref[ROW_MISC:ROW_MISC + 1, LANE_LOSS:LANE_LOSS + LANE] = loss_ref[...]
        for h in range(2):
            for j, (c0, c1) in enumerate(FF_PIECES):
                r = ROW_CONV_B + 3 * h + j
                out_ref[r:r + 1, 0:c1 - c0] = cb_ref[h, :, c0:c1]
                for t in range(3):
                    r = ROW_CONV_W + 3 * (3 * h + t) + j
                    out_ref[r:r + 1, 0:c1 - c0] = cw_ref[h, t:t + 1, c0:c1]

    return pl.pallas_call(
        body, name=name, out_shape=jax.ShapeDtypeStruct((SMALL_ROWS, 1024), F32),
    )(*norm_grads, dlogits, donw, dsinks, loss, d_cb, d_cw)


def _adamw_small(total, g_conv_w, w, m, v, name):
    n = len(SMALL)

    def body(*refs):
        t_ref, gcw_ref = refs[:2]
        w_refs, m_refs, v_refs = (dict(zip(SMALL, refs[2 + n * i:2 + n * (i + 1)])) for i in range(3))
        outs = refs[2 + 3 * n:]
        loss_ref = outs[0]
        g_refs, d_refs, mo_refs, vo_refs = (dict(zip(SMALL, outs[1 + n * i:1 + n * (i + 1)])) for i in range(4))
        loss_ref[...] = t_ref[ROW_MISC:ROW_MISC + 1, LANE_LOSS:LANE_LOSS + 1]

        def step(nm, idx, g):
            g_refs[nm][idx] = g
            d_refs[nm][idx], mo_refs[nm][idx], vo_refs[nm][idx] = _adamw(w_refs[nm][idx], g, m_refs[nm][idx], v_refs[nm][idx])

        everything = (slice(None), slice(None))
        for i, nm in enumerate(NORMS):
            step(nm, everything, t_ref[i:i + 1, :])
        step("hgrn_lb_logits", (slice(0, 1), slice(None)), t_ref[ROW_LOGITS:ROW_LOGITS + 1, 0:512])
        step("hgrn_lb_logits", (slice(1, 2), slice(None)), t_ref[ROW_LOGITS:ROW_LOGITS + 1, 512:1024])
        step("hgrn_out_norm", everything, t_ref[ROW_MISC:ROW_MISC + 1, 0:HGRN_DIM])
        step("attn_sinks", everything, t_ref[ROW_MISC:ROW_MISC + 1, LANE_SINKS:LANE_SINKS + ATTN_Q_HEADS])
        for h in range(2):
            for j, (c0, c1) in enumerate(FF_PIECES):
                r = ROW_CONV_B + 3 * h + j
                step("ffn_conv_b", (slice(None), slice(D_FF * h + c0, D_FF * h + c1)), t_ref[r:r + 1, 0:c1 - c0])
        step("ffn_conv_w", (slice(None), slice(None), slice(None)), gcw_ref[...])

    shapes = [jax.ShapeDtypeStruct(w[nm].shape, F32) for nm in SMALL]
    out = pl.pallas_call(
        body, name=name, out_shape=[jax.ShapeDtypeStruct((1, 1), F32)] + shapes * 4,
    )(total, g_conv_w, *[w[nm] for nm in SMALL], *[m[nm] for nm in SMALL], *[v[nm] for nm in SMALL])
    trees = [dict(zip(SMALL, out[1 + n * i:1 + n * (i + 1)])) for i in range(4)]
    return out[0], trees


BIG = ("w_in", "w_out", "ca_wq", "ca_wk", "ca_wv", "ca_wo", "ffn_w_up", "ffn_w_down")
BIG_FULL = {"w_in": (1024, 2816), "w_out": (1024, 1024), "ca_wq": (1024, 1024), "ca_wk": (1024, 1024),
            "ca_wv": (1024, 1024), "ca_wo": (1024, 1024), "ffn_w_up": (1024, 5632), "ffn_w_down": (2816, 1024)}
G_IN, G_MID, G_UP, G_DOWN = ("w_in",), ("w_out", "ca_wq", "ca_wk", "ca_wv", "ca_wo"), ("ffn_w_up",), ("ffn_w_down",)
GROUPS = (G_IN, G_MID, G_UP, G_DOWN)
COL_SHARDED = ("w_in", "ffn_w_up")
PACK_COLS = 1024


def _big_rows(name):
    r, c = BIG_FULL[name]
    return r * c // N_DEV // PACK_COLS


def _pack_shards(w, names):
    rows = [w[n][0].T if n in COL_SHARDED else w[n][0] for n in names]
    return (rows[0] if len(rows) == 1 else jnp.concatenate(rows, axis=0)).astype(BF16)


def _unpack_gathered(gathered, names):
    out, r0 = {}, 0
    for n in names:
        rows = _big_rows(n)
        out[n] = gathered[:, r0:r0 + rows].reshape(N_DEV * rows, PACK_COLS)
        r0 += rows
    return out


def _pack_full_grads(grads, names):
    parts = [grads[n].reshape(N_DEV, _big_rows(n), PACK_COLS) for n in names]
    return parts[0] if len(parts) == 1 else jnp.concatenate(parts, axis=1)


def kernel(x, mem, mix_pre_norm, w_in, attn_sinks, hgrn_lb_logits, hgrn_out_norm, w_out, mix_post_norm, ca_pre_norm, mem_norm, ca_wq, ca_wk, ca_wv, ca_wo, ca_post_norm, ffn_pre_norm, ffn_w_up, ffn_conv_w, ffn_conv_b, ffn_w_down, ffn_post_norm, loss_target, m_mix_pre_norm, m_w_in, m_attn_sinks, m_hgrn_lb_logits, m_hgrn_out_norm, m_w_out, m_mix_post_norm, m_ca_pre_norm, m_mem_norm, m_ca_wq, m_ca_wk, m_ca_wv, m_ca_wo, m_ca_post_norm, m_ffn_pre_norm, m_ffn_w_up, m_ffn_conv_w, m_ffn_conv_b, m_ffn_w_down, m_ffn_post_norm, v_mix_pre_norm, v_w_in, v_attn_sinks, v_hgrn_lb_logits, v_hgrn_out_norm, v_w_out, v_mix_post_norm, v_ca_pre_norm, v_mem_norm, v_ca_wq, v_ca_wk, v_ca_wv, v_ca_wo, v_ca_post_norm, v_ffn_pre_norm, v_ffn_w_up, v_ffn_conv_w, v_ffn_conv_b, v_ffn_w_down, v_ffn_post_norm):
    names = ["mix_pre_norm", "w_in", "attn_sinks", "hgrn_lb_logits", "hgrn_out_norm", "w_out", "mix_post_norm",
             "ca_pre_norm", "mem_norm", "ca_wq", "ca_wk", "ca_wv", "ca_wo", "ca_post_norm", "ffn_pre_norm",
             "ffn_w_up", "ffn_conv_w", "ffn_conv_b", "ffn_w_down", "ffn_post_norm"]
    w_all = dict(zip(names, [mix_pre_norm, w_in, attn_sinks, hgrn_lb_logits, hgrn_out_norm, w_out, mix_post_norm,
                             ca_pre_norm, mem_norm, ca_wq, ca_wk, ca_wv, ca_wo, ca_post_norm, ffn_pre_norm,
                             ffn_w_up, ffn_conv_w, ffn_conv_b, ffn_w_down, ffn_post_norm]))
    m_all = dict(zip(names, [m_mix_pre_norm, m_w_in, m_attn_sinks, m_hgrn_lb_logits, m_hgrn_out_norm, m_w_out,
                             m_mix_post_norm, m_ca_pre_norm, m_mem_norm, m_ca_wq, m_ca_wk, m_ca_wv, m_ca_wo,
                             m_ca_post_norm, m_ffn_pre_norm, m_ffn_w_up, m_ffn_conv_w, m_ffn_conv_b, m_ffn_w_down,
                             m_ffn_post_norm]))
    v_all = dict(zip(names, [v_mix_pre_norm, v_w_in, v_attn_sinks, v_hgrn_lb_logits, v_hgrn_out_norm, v_w_out,
                             v_mix_post_norm, v_ca_pre_norm, v_mem_norm, v_ca_wq, v_ca_wk, v_ca_wv, v_ca_wo,
                             v_ca_post_norm, v_ffn_pre_norm, v_ffn_w_up, v_ffn_conv_w, v_ffn_conv_b, v_ffn_w_down,
                             v_ffn_post_norm]))
    dev = _index(_mesh_pos())

    w_packs = {grp: _pack_shards(w_all, grp) for grp in GROUPS}
    shard_w = D_FF * 2 // N_DEV
    conv_w_rows = _exchange_alone(_Exchange("gather", ffn_conv_w[0]), "gather_conv_w")
    conv_w_full = conv_w_rows.transpose(1, 0, 2).reshape(3, 2 * D_FF)

    received, small_pack, grad_x = _local_step(
        x[0], mem[0], loss_target[0], w_packs, conv_w_full,
        {n: w_all[n] for n in NORMS}, attn_sinks, hgrn_lb_logits, hgrn_out_norm, ffn_conv_b)

    total = _sum_parts(_exchange_alone(_Exchange("gather", small_pack), "gather_small"), "sum_small")
    cw = total[ROW_CONV_W:ROW_CONV_W + 18].reshape(2, 3, 3 * PACK_COLS)[:, :, :D_FF]
    cw = cw.transpose(1, 0, 2).reshape(3, 2 * D_FF)
    g_conv_w = lax.dynamic_slice_in_dim(cw, dev * shard_w, shard_w, axis=1)[None]
    loss, (out_g, out_d, out_m, out_v) = _adamw_small(total, g_conv_w, w_all, m_all, v_all, "adamw_small")

    for grp in GROUPS:
        r0 = 0
        for n in grp:
            rows = _big_rows(n)
            if n in COL_SHARDED:
                g = _sum_rows(received[grp], r0, rows, "sum_" + n).T[None]
                d, mo, vo = _adamw_call(w_all[n], g, m_all[n], v_all[n], "adamw_" + n)
            else:
                g, d, mo, vo = _sum_rows(received[grp], r0, rows, "adamw_" + n, wmv=(w_all[n], m_all[n], v_all[n]))
            out_g[n], out_d[n], out_m[n], out_v[n] = g, d, mo, vo
            r0 += rows

    return (loss[0, 0], grad_x[None], *[out_g[n] for n in names], *[out_d[n] for n in names],
            *[out_m[n] for n in names], *[out_v[n] for n in names])


def _local_step(x, mem, target, w_packs, conv_w, norms, sinks, lb_logits, out_norm, conv_b):
    g1, g2, g3 = norms["mix_pre_norm"], norms["mix_post_norm"], norms["ca_pre_norm"]
    g4, g5, g6, g7 = norms["mem_norm"], norms["ca_post_norm"], norms["ffn_pre_norm"], norms["ffn_post_norm"]

    h1, gathered = _norm_fwd(x, g1, "mix_norm", exchange=_Exchange("gather", w_packs[G_IN], relay=True))
    w_in_t = _unpack_gathered(gathered, G_IN)["w_in"]
    up_shard = w_packs[G_UP]
    up_rows = up_shard.shape[0]
    up_cuts = (0, up_rows // 2, 3 * up_rows // 4, up_rows)
    up_parts = [up_shard[a:b] for a, b in zip(up_cuts[:-1], up_cuts[1:])]
    z, up_0 = _mm(h1, w_in_t, mode="nt", out_dtype=BF16, name="in_proj", tn=1408,
                  exchange=_Exchange("gather", up_parts[0]))
    attn, lse, gathered = _swa_fwd(z, sinks, "swa_fwd", exchange=_Exchange("gather", w_packs[G_DOWN]))
    w_down = _unpack_gathered(gathered, G_DOWN)["ffn_w_down"]
    lb = _lower_bound(lb_logits, "lower_bound")
    rec, o_rec, states, scores, gathered = _hgrn_fwd(
        z, lb, out_norm, "hgrn_fwd", exchange=_Exchange("gather", w_packs[G_MID]))
    w_out, wq, wk, wv, wo = (_unpack_gathered(gathered, G_MID)[n] for n in G_MID)
    cat = jnp.concatenate([attn, rec], axis=1)
    x1, h2, mix, up_1 = _mm(cat, w_out, mode="nn", out_dtype=BF16, name="out_proj",
                            exchange=_Exchange("gather", up_parts[1]), epilogue=_post_pre(x, g2, g3))
    mem_n = _norm_fwd(mem, g4, "mem_norm")
    q = _mm(h2, wq, mode="nn", out_dtype=BF16, name="ca_q")
    k = _mm(mem_n, wk, mode="nn", out_dtype=BF16, name="ca_k")
    v = _mm(mem_n, wv, mode="nn", out_dtype=BF16, name="ca_v")
    oc = _ca_fwd(q, k, v, "ca_fwd")
    x2, h3, c, up_2 = _mm(oc, wo, mode="nn", out_dtype=BF16, name="ca_o",
                          exchange=_Exchange("gather", up_parts[2]), epilogue=_post_pre(x1, g5, g6))
    w_up_t = jnp.concatenate([up_0, up_1, up_2], axis=1).reshape(-1, PACK_COLS)
    u = _mm(h3, w_up_t, mode="nt", out_dtype=F32, name="ffn_up", tn=1408, split_out=True)
    a = _glu_fwd(u, conv_w, conv_b, "glu_fwd")
    dx3, dy, loss_row, dg7 = _mm(a, w_down, mode="nn", out_dtype=BF16, name="ffn_down", tm=512, tk=2816,
                                 epilogue=_final(x2, target, g7))
    loss = loss_row[:, :LANE]

    da = _mm(dy, w_down, mode="nt", out_dtype=F32, name="ffn_down_dx", tn=1408)
    d_w_down = _mm(a, dy, mode="tn", out_dtype=BF16, name="ffn_down_dw", tm=1408, tk=1024)
    dc, d_cb, d_cw, got_down = _glu_bwd(
        u, conv_w, conv_b, da, "glu_bwd",
        exchange=_Exchange("scatter", _pack_full_grads({"ffn_w_down": d_w_down}, G_DOWN)))
    du = _conv_bwd(dc, conv_w, "conv_bwd")
    d_w_up_t = _mm(du, h3, mode="tn", out_dtype=BF16, name="ffn_up_dw", tm=1408, tk=1024, split_a=True)
    dx2, dcv, dg6, dg5, got_up = _mm(
        du, w_up_t, mode="nn", out_dtype=BF16, name="ffn_up_dx", tm=1024, tk=1408, split_a=True,
        exchange=_Exchange("scatter", _pack_full_grads({"ffn_w_up": d_w_up_t}, G_UP)),
        epilogue=_norm_bwd2(dx3, x2, c, g6, g5))
    doc = _mm(dcv, wo, mode="nt", out_dtype=BF16, name="ca_o_dx")
    d_wo = _mm(oc, dcv, mode="tn", out_dtype=BF16, name="ca_o_dw", tm=1024, tk=1024)
    dq, dk, dv = _ca_bwd(q, k, v, doc, "ca_bwd")
    d_wq = _mm(h2, dq, mode="tn", out_dtype=BF16, name="ca_q_dw", tm=1024, tk=1024)
    dx1, dmix, dg3, dg2 = _mm(dq, wq, mode="nt", out_dtype=BF16, name="ca_q_dx",
                              epilogue=_norm_bwd2(dx2, x1, mix, g3, g2))
    d_wk = _mm(mem_n, dk, mode="tn", out_dtype=BF16, name="ca_k_dw", tm=1024)
    d_wv = _mm(mem_n, dv, mode="tn", out_dtype=BF16, name="ca_v_dw", tm=1024)
    dmem_k = _mm(dk, wk, mode="nt", out_dtype=F32, name="ca_k_dx")
    dmem_v = _mm(dv, wv, mode="nt", out_dtype=F32, name="ca_v_dx")
    dg4 = _gain_bwd(mem, dmem_k, dmem_v, "mem_norm_bwd")
    dcat = _mm(dmix, w_out, mode="nt", out_dtype=BF16, name="out_proj_dx")
    d_w_out = _mm(cat, dmix, mode="tn", out_dtype=BF16, name="out_proj_dw", tm=1024, tk=1024)
    mid = {"w_out": d_w_out, "ca_wq": d_wq, "ca_wk": d_wk, "ca_wv": d_wv, "ca_wo": d_wo}
    dqr, dfr, dir_, dgr, dlb, donw, got_mid = _hgrn_bwd(
        z, lb, out_norm, o_rec, states, scores, dcat, "hgrn_bwd",
        exchange=_Exchange("scatter", _pack_full_grads(mid, G_MID)))
    dq_a, dka, dkb, dva, dvb, dsinks = _swa_bwd(z, sinks, dcat, lse, "swa_bwd")
    dz = _assemble_dz(dq_a, dka, dkb, dva, dvb, dqr, dfr, dir_, dgr, "assemble_dz")
    d_w_in_t = _mm(dz, h1, mode="tn", out_dtype=BF16, name="in_proj_dw", tm=1408, tk=1024)
    dx, dg1, got_in = _mm(dz, w_in_t, mode="nn", out_dtype=BF16, name="in_proj_dx", tm=512, tk=2816,
                          exchange=_Exchange("scatter", _pack_full_grads({"w_in": d_w_in_t}, G_IN)),
                          epilogue=_norm_bwd1(dx1, x, g1))

    small_pack = _pack_small(
        (dg1, dg2, dg3, dg4, dg5, dg6, dg7), _lower_bound_bwd(lb, dlb, "lower_bound_bwd"), donw, dsinks, loss,
        d_cb, d_cw, "pack_small")
    return {G_IN: got_in, G_MID: got_mid, G_UP: got_up, G_DOWN: got_down}, small_pack, dx
```

```python
import jax
import jax.numpy as jnp
from jax import lax
from jax.experimental import pallas as pl
from jax.experimental.pallas import tpu as pltpu

F32 = jnp.float32
BF16 = jnp.bfloat16
EPS = 1e-6
N_DEV = 8
MESH_AXES = ("x", "y", "c")

ATTN_HEAD_DIM = 64
ATTN_Q_HEADS = 8
ATTN_KV_HEADS = 2
ATTN_BLOCK = 128
HGRN_HEADS = 4
HGRN_DIM = 128
HGRN_CHUNK = 64
HGRN_PAIR = 4
Z_Q, Z_F, Z_I, Z_G = 768, 1280, 1792, 2304
HGRN_LEVELS = (32, 16, 8, 4, 2, 1)
CA_HEADS = 4
CA_HEAD_DIM = 256
D_FF = 2816

ADAM_LR = 0.001
ADAM_B1 = 0.9
ADAM_B2 = 0.999
ADAM_EPS = 1e-08
ADAM_WD = 0.01
ADAM_STEP = 10

VMEM_LIMIT = 58 << 20
EPILOGUE_ROWS = 256
LANE = 128

NT = (((1,), (1,)), ((), ()))
TN = (((0,), (0,)), ((), ()))


def _params(*sem):
    return pltpu.CompilerParams(dimension_semantics=sem, vmem_limit_bytes=VMEM_LIMIT)


def _tile(n, cap):
    if n <= cap:
        return n
    best = 0
    for t in range(LANE, cap + 1, LANE):
        if n % t == 0:
            best = t
    assert best, (n, cap)
    return best


def _dot(a, b, dims=None):
    if dims is None:
        return jnp.dot(a, b, preferred_element_type=F32)
    return lax.dot_general(a, b, dims, preferred_element_type=F32)


def _bf(x):
    return x.astype(BF16)


def _sigmoid(x):
    return 1.0 / (1.0 + jnp.exp(-x))


def _rms(x):
    r = lax.rsqrt(jnp.mean(x * x, axis=-1, keepdims=True) + EPS)
    return x * r, r


def _rms_bwd(dxh, xh, r):
    return r * (dxh - xh * jnp.mean(dxh * xh, axis=-1, keepdims=True))


def _mm(a, b, *, mode, out_dtype, name, tm=1024, tn=1024, tk=1024, split_a=False, split_b=False, split_out=False,
        exchange=None, epilogue=None):
    def dims(arr, split):
        if split:
            return arr.shape[1], 2 * arr.shape[2]
        return arr.shape

    ar, ac = dims(a, split_a)
    br, bc = dims(b, split_b)
    if mode == "nn":
        M, K, N = ar, ac, bc
        assert br == K
    elif mode == "nt":
        M, K, N = ar, ac, br
        assert bc == K
    else:
        K, M, N = ar, ac, bc
        assert br == K
    a_cols_half = ac // 2 if split_a else None
    b_cols_half = bc // 2 if split_b else None
    tm = _tile(M, tm)
    tn = _tile((N // 2) if (split_out or (split_b and mode != "nt")) else N, tn)
    tk = _tile((K // 2) if ((split_a and mode != "tn") or (split_b and mode == "nt")) else K, tk)
    if split_a and mode == "tn":
        tm = _tile(M // 2, tm)
    gm, gn, gk = M // tm, N // tn, K // tk
    a_bytes, b_bytes = a.size * a.dtype.itemsize, b.size * b.dtype.itemsize
    rows_outer = gk > 1 or a_bytes + gm * b_bytes <= gn * a_bytes + b_bytes
    grid = (gm, gn, gk) if rows_outer else (gn, gm, gk)

    def spec(split, half, blk, rc):
        def imap(p, q, k):
            r, c = rc(*((p, q) if rows_outer else (q, p)), k)
            if not split:
                return (r, c)
            per_half = half // blk[1]
            return (c // per_half, r, c % per_half)

        return pl.BlockSpec(((None,) + blk) if split else blk, imap)

    if mode == "nn":
        a_spec = spec(split_a, a_cols_half, (tm, tk), lambda i, j, k: (i, k))
        b_spec = spec(split_b, b_cols_half, (tk, tn), lambda i, j, k: (k, j))
        dn = None
    elif mode == "nt":
        a_spec = spec(split_a, a_cols_half, (tm, tk), lambda i, j, k: (i, k))
        b_spec = spec(split_b, b_cols_half, (tn, tk), lambda i, j, k: (j, k))
        dn = NT
    else:
        a_spec = spec(split_a, a_cols_half, (tk, tm), lambda i, j, k: (k, i))
        b_spec = spec(split_b, b_cols_half, (tk, tn), lambda i, j, k: (k, j))
        dn = TN
    o_spec = spec(split_out, N // 2 if split_out else None, (tm, tn), lambda i, j, k: (i, j))
    out_shape = (2, M, N // 2) if split_out else (M, N)

    in_specs, out_specs, args = [a_spec, b_spec], [o_spec], (a, b)
    out_shapes = [jax.ShapeDtypeStruct(out_shape, out_dtype)]
    semantics = ("parallel", "parallel", "arbitrary")

    def store(result, extra, outs):
        outs[0][...] = result[...].astype(outs[0].dtype)

    if epilogue is not None:
        assert gn == 1 and not split_out
        n_vec = epilogue.n_out_vecs
        row = pl.BlockSpec((tm, N), lambda p, q, k: ((p if rows_outer else q), 0))
        vec = pl.BlockSpec((1, N), lambda p, q, k: (0, 0))
        in_specs += [row] * len(epilogue.rows) + [vec] * len(epilogue.vecs)
        args += tuple(epilogue.rows) + tuple(epilogue.vecs)
        out_specs = [row] * len(epilogue.out_rows) + [vec] * n_vec
        out_shapes = ([jax.ShapeDtypeStruct((M, N), dt) for dt in epilogue.out_rows]
                      + [jax.ShapeDtypeStruct((1, N), F32)] * n_vec)
        semantics = ("arbitrary",) * 3

        def store(result, extra, outs):
            n_rows, n_out_rows, sub = len(epilogue.rows), len(epilogue.out_rows), min(EPILOGUE_ROWS, tm)
            for r in range(0, tm, sub):
                rows = pl.ds(r, sub)
                epilogue.fn(result[r:r + sub], *[ref.at[rows] for ref in extra[:n_rows]], *extra[n_rows:],
                            *[ref.at[rows] for ref in outs[:n_out_rows]], *outs[n_out_rows:])

    n_extra = len(in_specs) - 2
    n_out = len(out_specs)

    def body(a_ref, b_ref, *refs):
        extra, outs, scratch_refs = refs[:n_extra], refs[n_extra:n_extra + n_out], refs[n_extra + n_out:]
        k = pl.program_id(2)
        if epilogue is not None:
            @pl.when((pl.program_id(0) == 0) & (pl.program_id(1) == 0) & (k == 0))
            def _():
                for ref in outs[n_out - epilogue.n_out_vecs:]:
                    ref[...] = jnp.zeros_like(ref)

        if gk == 1:
            store(_dot(_bf(a_ref[...]), _bf(b_ref[...]), dn), extra, outs)
            return
        acc_ref = scratch_refs[0]

        @pl.when(k == 0)
        def _():
            acc_ref[...] = jnp.zeros_like(acc_ref)

        acc_ref[...] += _dot(_bf(a_ref[...]), _bf(b_ref[...]), dn)

        @pl.when(k == gk - 1)
        def _():
            store(acc_ref, extra, outs)

    out = _hosted_call(
        body, name=name, grid=grid, in_specs=in_specs, out_specs=out_specs, out_shape=out_shapes,
        scratch=[] if gk == 1 else [pltpu.VMEM((tm, tn), F32)], args=args, semantics=semantics, exchange=exchange)
    return out[0] if (exchange is None and epilogue is None) else out


ROWS = 512


def _row_spec(tr, cols):
    return pl.BlockSpec((tr, cols), lambda i: (i, 0))


def _vec_spec(cols):
    return pl.BlockSpec((1, cols), lambda i: (0, 0))


def _norm_fwd(x, g, name, exchange=None):
    T, Dm = x.shape
    tr = min(ROWS, T)

    def body(x_ref, g_ref, h_ref):
        xh, _ = _rms(x_ref[...])
        h_ref[...] = (xh * g_ref[...]).astype(h_ref.dtype)

    out = _hosted_call(
        body, name=name, grid=(T // tr,), in_specs=[_row_spec(tr, Dm), _vec_spec(Dm)], out_specs=[_row_spec(tr, Dm)],
        out_shape=[jax.ShapeDtypeStruct((T, Dm), BF16)], scratch=[], args=(x, g), semantics=("parallel",),
        exchange=exchange)
    return out[0] if exchange is None else out


def _post_pre(x, g_post, g_pre):
    def fn(m, x_ref, gp_ref, gn_ref, xo_ref, h_ref, m_ref):
        mh, _ = _rms(m)
        xn = x_ref[...] + mh * gp_ref[...]
        xo_ref[...] = xn
        xh, _ = _rms(xn)
        h_ref[...] = (xh * gn_ref[...]).astype(h_ref.dtype)
        m_ref[...] = m.astype(m_ref.dtype)

    return _RowEpilogue(fn, [x], [g_post, g_pre], [F32, BF16, BF16], 0)


def _final(x2, target, g_post):
    def fn(y, x_ref, t_ref, g_ref, dx_ref, dy_ref, loss_ref, dg_ref):
        g = g_ref[...]
        yh, r = _rms(y)
        d = x_ref[...] + yh * g - t_ref[...]
        loss_ref[...] += 0.5 * jnp.sum(jnp.mean(d * d, axis=-1, keepdims=True))
        dx = d * (1.0 / d.shape[-1])
        dx_ref[...] = dx
        dy_ref[...] = _rms_bwd(dx * g, yh, r).astype(dy_ref.dtype)
        dg_ref[...] += jnp.sum(dx * yh, axis=0, keepdims=True)

    return _RowEpilogue(fn, [x2, target], [g_post], [F32, BF16], 2)


class _RowEpilogue:
    def __init__(self, fn, rows, vecs, out_rows, n_out_vecs):
        self.fn, self.rows, self.vecs, self.out_rows, self.n_out_vecs = fn, rows, vecs, out_rows, n_out_vecs


def _norm_bwd2(dx_cur, x_prev, m_prev, g_pre, g_post):
    def fn(dh, dx_ref, x_ref, m_ref, gn_ref, gp_ref, dxo_ref, dm_ref, dgn_ref, dgp_ref):
        xh, r = _rms(x_ref[...])
        dx = dx_ref[...] + _rms_bwd(dh * gn_ref[...], xh, r)
        dxo_ref[...] = dx
        dgn_ref[...] += jnp.sum(dh * xh, axis=0, keepdims=True)
        mh, rm = _rms(m_ref[...].astype(F32))
        dm_ref[...] = _rms_bwd(dx * gp_ref[...], mh, rm).astype(dm_ref.dtype)
        dgp_ref[...] += jnp.sum(dx * mh, axis=0, keepdims=True)

    return _RowEpilogue(fn, [dx_cur, x_prev, m_prev], [g_pre, g_post], [F32, BF16], 2)


def _norm_bwd1(dx_cur, x_prev, g_pre):
    def fn(dh, dx_ref, x_ref, gn_ref, dxo_ref, dgn_ref):
        xh, r = _rms(x_ref[...])
        dxo_ref[...] = dx_ref[...] + _rms_bwd(dh * gn_ref[...], xh, r)
        dgn_ref[...] += jnp.sum(dh * xh, axis=0, keepdims=True)

    return _RowEpilogue(fn, [dx_cur, x_prev], [g_pre], [F32], 1)


def _gain_bwd(x, dh_a, dh_b, name):
    T, Dm = x.shape

    def body(x_ref, a_ref, b_ref, dg_ref):
        xh, _ = _rms(x_ref[...])
        dg_ref[...] = jnp.sum((a_ref[...] + b_ref[...]) * xh, axis=0, keepdims=True)

    return pl.pallas_call(
        body, name=name, grid=(1,), in_specs=[_row_spec(T, Dm)] * 3, out_specs=_vec_spec(Dm),
        out_shape=jax.ShapeDtypeStruct((1, Dm), F32), compiler_params=_params("arbitrary"),
    )(x, dh_a, dh_b)


ATTN_GROUP = ATTN_Q_HEADS // ATTN_KV_HEADS
ASSEMBLE_ROWS = 1024


def _swa_mask(n):
    rows = ATTN_GROUP * ATTN_BLOCK
    row = lax.broadcasted_iota(jnp.int32, (rows, 2 * ATTN_BLOCK), 0) & (ATTN_BLOCK - 1)
    col = lax.broadcasted_iota(jnp.int32, (rows, 2 * ATTN_BLOCK), 1)
    diff = row + ATTN_BLOCK - col
    return (diff >= 0) & (diff < ATTN_BLOCK) & ((col >= ATTN_BLOCK) | (n > 0))


def _swa_rows(ref, hk, dtype):
    hd = ATTN_HEAD_DIM
    return jnp.concatenate(
        [ref[:, hd * (hk * ATTN_GROUP + g):hd * (hk * ATTN_GROUP + g + 1)].astype(dtype) for g in range(ATTN_GROUP)],
        axis=0)


def _swa_per_row(vals):
    seg = lax.broadcasted_iota(jnp.int32, (ATTN_GROUP * ATTN_BLOCK, 1), 0) // ATTN_BLOCK
    col = jnp.zeros((ATTN_GROUP * ATTN_BLOCK, 1), F32)
    for g, val in enumerate(vals):
        col = jnp.where(seg == g, val, col)
    return col


def _swa_specs():
    blk = ATTN_BLOCK
    prev = lambda n: jnp.maximum(n - 1, 0)
    return [
        pl.BlockSpec(memory_space=pltpu.SMEM),
        pl.BlockSpec((blk, 512), lambda n: (n, 0)),
        pl.BlockSpec((blk, 128), lambda n: (prev(n), 4)),
        pl.BlockSpec((blk, 128), lambda n: (n, 4)),
        pl.BlockSpec((blk, 128), lambda n: (prev(n), 5)),
        pl.BlockSpec((blk, 128), lambda n: (n, 5)),
    ]


def _swa_fwd(z, sinks, name, exchange=None):
    T = z.shape[0]
    blk, hd = ATTN_BLOCK, ATTN_HEAD_DIM
    scale = hd ** -0.5

    def body(sink_ref, q_ref, kp_ref, kc_ref, vp_ref, vc_ref, o_ref, lse_ref):
        allowed = _swa_mask(pl.program_id(0))
        hks = range(ATTN_KV_HEADS)
        kss = [slice(hd * hk, hd * hk + hd) for hk in hks]
        k = [_bf(jnp.concatenate([kp_ref[:, ks], kc_ref[:, ks]], axis=0)) for ks in kss]
        v = [_bf(jnp.concatenate([vp_ref[:, ks], vc_ref[:, ks]], axis=0)) for ks in kss]
        s = [jnp.where(allowed, _dot(_swa_rows(q_ref, hk, BF16), k[hk], NT) * scale, -1e30) for hk in hks]
        sink = [_swa_per_row([sink_ref[0, hk * ATTN_GROUP + g] for g in range(ATTN_GROUP)]) for hk in hks]
        m = [jnp.maximum(jnp.max(s[hk], axis=-1, keepdims=True), sink[hk]) for hk in hks]
        p = [jnp.exp(s[hk] - m[hk]) for hk in hks]
        l = [jnp.sum(p[hk], axis=-1, keepdims=True) + jnp.exp(sink[hk] - m[hk]) for hk in hks]
        o = [_dot(_bf(p[hk] / l[hk]), v[hk]).astype(o_ref.dtype) for hk in hks]
        for hk in hks:
            lse = m[hk] + jnp.log(l[hk])
            for g in range(ATTN_GROUP):
                h = hk * ATTN_GROUP + g
                o_ref[:, hd * h:hd * (h + 1)] = o[hk][blk * g:blk * (g + 1)]
                lse_ref[:, h:h + 1] = lse[blk * g:blk * (g + 1)]

    return _hosted_call(
        body, name=name, grid=(T // blk,), in_specs=_swa_specs(),
        out_specs=[pl.BlockSpec((blk, 512), lambda n: (n, 0)), pl.BlockSpec((blk, ATTN_Q_HEADS), lambda n: (n, 0))],
        out_shape=[jax.ShapeDtypeStruct((T, 512), BF16), jax.ShapeDtypeStruct((T, ATTN_Q_HEADS), F32)],
        scratch=[], args=(sinks, z, z, z, z, z), semantics=("parallel",), exchange=exchange)


def _swa_bwd(z, sinks, dcat, lse, name):
    T = z.shape[0]
    blk, hd = ATTN_BLOCK, ATTN_HEAD_DIM
    scale = hd ** -0.5
    group = ATTN_Q_HEADS // ATTN_KV_HEADS

    def body(sink_ref, q_ref, kp_ref, kc_ref, vp_ref, vc_ref, do_ref, lse_ref,
             dq_ref, dka_ref, dkb_ref, dva_ref, dvb_ref, dsink_ref):
        @pl.when(pl.program_id(0) == 0)
        def _():
            dsink_ref[...] = jnp.zeros_like(dsink_ref)

        allowed = _swa_mask(pl.program_id(0))
        lane = lax.broadcasted_iota(jnp.int32, (1, ATTN_Q_HEADS), 1)
        dsink = jnp.zeros((1, ATTN_Q_HEADS), F32)
        hks = range(ATTN_KV_HEADS)
        kss = [slice(hd * hk, hd * hk + hd) for hk in hks]
        k = [_bf(jnp.concatenate([kp_ref[:, ks], kc_ref[:, ks]], axis=0)) for ks in kss]
        v = [_bf(jnp.concatenate([vp_ref[:, ks], vc_ref[:, ks]], axis=0)) for ks in kss]
        qs = [_swa_rows(q_ref, hk, BF16) for hk in hks]
        dos = [_swa_rows(do_ref, hk, BF16) for hk in hks]
        lse = [jnp.concatenate([lse_ref[:, hk * group + g:hk * group + g + 1] for g in range(group)], axis=0)
               for hk in hks]
        s = [_dot(qs[hk], k[hk], NT) * scale for hk in hks]
        dp = [_dot(dos[hk], v[hk], NT) for hk in hks]
        p = [jnp.where(allowed, jnp.exp(jnp.where(allowed, s[hk], -1e30) - lse[hk]), 0.0) for hk in hks]
        delta = [jnp.sum(p[hk] * dp[hk], axis=-1, keepdims=True) for hk in hks]
        ds = [_bf(p[hk] * (dp[hk] - delta[hk]) * scale) for hk in hks]
        dq = [_dot(ds[hk], k[hk]).astype(dq_ref.dtype) for hk in hks]
        dk = [_dot(ds[hk], qs[hk], TN) for hk in hks]
        dv = [_dot(_bf(p[hk]), dos[hk], TN) for hk in hks]
        for hk in hks:
            sink = _swa_per_row([sink_ref[0, hk * group + g] for g in range(group)])
            sink_part = jnp.exp(sink - lse[hk]) * delta[hk]
            for g in range(group):
                h = hk * group + g
                dq_ref[:, hd * h:hd * (h + 1)] = dq[hk][blk * g:blk * (g + 1)]
                dsink = dsink + jnp.where(lane == h, -jnp.sum(sink_part[blk * g:blk * (g + 1)]), 0.0)
            dkb_ref[:, kss[hk]] = dk[hk][:blk]
            dka_ref[:, kss[hk]] = dk[hk][blk:]
            dvb_ref[:, kss[hk]] = dv[hk][:blk]
            dva_ref[:, kss[hk]] = dv[hk][blk:]
        dsink_ref[...] += dsink

    kv_out = pl.BlockSpec((blk, 128), lambda n: (n, 0))
    return pl.pallas_call(
        body, name=name, grid=(T // blk,),
        in_specs=_swa_specs() + [pl.BlockSpec((blk, 512), lambda n: (n, 0)),
                                 pl.BlockSpec((blk, ATTN_Q_HEADS), lambda n: (n, 0))],
        out_specs=[pl.BlockSpec((blk, 512), lambda n: (n, 0)), kv_out, kv_out, kv_out, kv_out,
                   pl.BlockSpec((1, ATTN_Q_HEADS), lambda n: (0, 0))],
        out_shape=[jax.ShapeDtypeStruct((T, 512), BF16)] + [jax.ShapeDtypeStruct((T, 128), F32)] * 4
        + [jax.ShapeDtypeStruct((1, ATTN_Q_HEADS), F32)],
        compiler_params=_params("arbitrary"),
    )(sinks, z, z, z, z, z, dcat, lse)


def _assemble_dz(dq_a, dka, dkb, dva, dvb, dqr, dfr, dir_, dgr, name):
    T = dq_a.shape[0]
    blk = ATTN_BLOCK
    rows = min(ASSEMBLE_ROWS, T)
    nb, per = T // rows, rows // blk

    def body(dq_ref, dka_ref, dkb_ref, dkn_ref, dva_ref, dvb_ref, dvn_ref, dqr_ref, dfr_ref, dir_ref, dgr_ref, o_ref):
        has_next = pl.program_id(0) < nb - 1

        def with_next(a_ref, b_ref, n_ref):
            after = jnp.where(has_next, n_ref[...], 0.0)
            shifted = after if per == 1 else jnp.concatenate([b_ref[blk:, :], after], axis=0)
            return (a_ref[...] + shifted).astype(o_ref.dtype)

        o_ref[:, 0:512] = dq_ref[...]
        o_ref[:, 512:640] = with_next(dka_ref, dkb_ref, dkn_ref)
        o_ref[:, 640:768] = with_next(dva_ref, dvb_ref, dvn_ref)
        o_ref[:, 768:1280] = dqr_ref[...]
        o_ref[:, 1280:1792] = dfr_ref[...]
        o_ref[:, 1792:2304] = dir_ref[...]
        o_ref[:, 2304:2816] = dgr_ref[...]

    cur = lambda w: pl.BlockSpec((rows, w), lambda n: (n, 0))
    nxt = pl.BlockSpec((blk, 128), lambda n: (jnp.minimum((n + 1) * per, T // blk - 1), 0))
    return pl.pallas_call(
        body, name=name, grid=(nb,),
        in_specs=[cur(512), cur(128), cur(128), nxt, cur(128), cur(128), nxt, cur(512), cur(512), cur(512), cur(512)],
        out_specs=pl.BlockSpec((rows, 2816), lambda n: (n, 0)),
        out_shape=jax.ShapeDtypeStruct((T, 2816), BF16), compiler_params=_params("parallel"),
    )(dq_a, dka, dkb, dkb, dva, dvb, dvb, dqr, dfr, dir_, dgr)


HGRN_ROWS = 512


def _hgrn_consts():
    c = HGRN_CHUNK
    r = lax.broadcasted_iota(jnp.int32, (c, c), 0)
    s = lax.broadcasted_iota(jnp.int32, (c, c), 1)
    rcol = lax.broadcasted_iota(jnp.int32, (c, 1), 0)
    same_block, upper = [], []
    for m in HGRN_LEVELS:
        same_block.append((r & ~(2 * m - 1)) == (s & ~(2 * m - 1)))
        upper.append((rcol & (2 * m - 1)) >= m)
    cum_mat = jnp.where(s <= r, 1.0, 0.0).astype(BF16)
    rev_mat = jnp.where(s >= r, 1.0, 0.0).astype(BF16)
    return cum_mat, rev_mat, r == s, same_block, upper, rcol & 3, s == r - 1


def _hgrn_level_decay(g, b, m, pos4):
    c = HGRN_CHUNK
    if m == 1:
        return jnp.exp(jnp.where((pos4 & 1) == 1, g, 0.0))
    if m == 2:
        after, before = pltpu.roll(g, c - 1, 0), pltpu.roll(g, 1, 0)
        return jnp.exp(jnp.where(pos4 == 0, after, jnp.where(pos4 == 1, 0.0, jnp.where(pos4 == 2, g, g + before))))
    b3 = b.reshape(c // (2 * m), 2 * m, HGRN_DIM)
    bref = jnp.broadcast_to(b3[:, m - 1:m, :], b3.shape).reshape(c, HGRN_DIM)
    return jnp.exp(-jnp.abs(b - bref))


def _split3(x):
    hi = _bf(x)
    r1 = x - hi.astype(F32)
    mid = _bf(r1)
    lo = _bf(r1 - mid.astype(F32))
    return jnp.concatenate([hi, mid, lo], axis=1)


def _dot_hilo(a, b):
    r, c = a.shape[0], b.shape[1]
    a_hi, b_hi = _bf(a), _bf(b)
    a2 = jnp.concatenate([a_hi, _bf(a - a_hi.astype(F32))], axis=0)
    b2 = jnp.concatenate([b_hi, _bf(b - b_hi.astype(F32))], axis=1)
    y = _dot(a2, b2)
    return y[:r, :c] + y[:r, c:] + y[r:, :c]


def _fold3(y):
    w = y.shape[1] // 3
    return y[:, :w] + y[:, w:2 * w] + y[:, 2 * w:]


def _hgrn_gates(qr, fr, lb):
    sq = _sigmoid(qr)
    q = qr * sq * (HGRN_DIM ** -0.5)
    sf = _sigmoid(fr)
    f = lb + (1.0 - lb) * sf
    k = (1.0 - lb) * _sigmoid(-fr)
    return q, sq, sf, f, k, jnp.log(f)


def _hgrn_intra(q, k, g, b, consts, scores=True):
    _, _, eye, same_block, upper, pos4, below = consts
    heads = range(len(q))
    a = None
    if scores:
        a = [jnp.where(eye, jnp.sum(q[hh] * k[hh], axis=1, keepdims=True), 0.0) for hh in heads]
    saved = [[] for _ in heads]
    for i, m in enumerate(HGRN_LEVELS):
        up = upper[i]
        e = [_hgrn_level_decay(g[hh], b[hh], m, pos4) for hh in heads]
        qt = [jnp.where(up, q[hh] * e[hh], 0.0) for hh in heads]
        kt = [jnp.where(up, 0.0, k[hh] * e[hh]) for hh in heads]
        for hh in heads:
            saved[hh].append((e[hh], qt[hh], kt[hh]))
        if not scores:
            continue
        if m == 1:
            for hh in heads:
                pair = jnp.sum(qt[hh] * pltpu.roll(kt[hh], 1, 0), axis=1, keepdims=True)
                a[hh] = a[hh] + jnp.where(below, pair, 0.0)
            continue
        p = [_dot(_bf(qt[hh]), _bf(kt[hh]), NT) for hh in heads]
        for hh in heads:
            a[hh] = a[hh] + jnp.where(same_block[i], p[hh], 0.0)
    return a, saved


def _hgrn_specs(tb, nb, rev):
    tmap = (lambda t: nb - 1 - t) if rev else (lambda t: t)
    assert HGRN_PAIR == HGRN_HEADS
    return [pl.BlockSpec((tb, 2816), lambda h, t: (tmap(t), 0)),
            pl.BlockSpec((1, HGRN_PAIR * HGRN_DIM), lambda h, t: (0, h)),
            pl.BlockSpec((1, HGRN_DIM), lambda h, t: (0, 0))]


def _hgrn_z(z_ref, sl, base, head):
    return z_ref[sl, base + HGRN_DIM * head:base + HGRN_DIM * (head + 1)].astype(F32)


def _hgrn_fwd(z, lb, onw, name, exchange=None):
    T = z.shape[0]
    tb = min(HGRN_ROWS, T)
    nb, c, nc = T // tb, HGRN_CHUNK, min(HGRN_ROWS, T) // HGRN_CHUNK

    def body(z_ref, lb_ref, onw_ref, rec_ref, o_ref, st_ref, a_ref, state):
        @pl.when(pl.program_id(1) == 0)
        def _():
            state[...] = jnp.zeros_like(state)

        consts = _hgrn_consts()
        lbv = lb_ref[...]
        onwv = onw_ref[...]

        def chunk(ci, carry):
            sl = pl.ds(pl.multiple_of(ci * c, c), c)
            heads = range(HGRN_PAIR)
            lss = [slice(HGRN_DIM * hh, HGRN_DIM * (hh + 1)) for hh in heads]
            gates = [_hgrn_gates(_hgrn_z(z_ref, sl, Z_Q, hh), _hgrn_z(z_ref, sl, Z_F, hh), lbv[:, lss[hh]])
                     for hh in heads]
            q, k, g = [t[0] for t in gates], [t[4] for t in gates], [t[5] for t in gates]
            v = [_bf(_hgrn_z(z_ref, sl, Z_I, hh)) for hh in heads]
            b = [_fold3(_dot(consts[0], _split3(g[hh]))) for hh in heads]
            a, _ = _hgrn_intra(q, k, g, b, consts)
            st = [state[hh] for hh in heads]
            for hh in heads:
                st_ref[hh, ci] = st[hh]
            bl = [b[hh][c - 1:c, :] for hh in heads]
            o_state = [_dot(_bf(q[hh] * jnp.exp(b[hh])), _bf(st[hh]), NT) for hh in heads]
            kv = [_dot(v[hh], _bf(k[hh] * jnp.exp(bl[hh] - b[hh])), TN) for hh in heads]
            a = [_bf(a[hh]) for hh in heads]
            o = [_dot(a[hh], v[hh]) + o_state[hh] for hh in heads]
            for hh in heads:
                a_ref[sl, c * hh:c * (hh + 1)] = a[hh]
                state[hh] = st[hh] * jnp.exp(bl[hh]) + kv[hh]
                o_ref[sl, lss[hh]] = o[hh]
                oh, _ = _rms(o[hh])
                gr = _hgrn_z(z_ref, sl, Z_G, hh)
                rec_ref[sl, lss[hh]] = (oh * onwv * (gr * _sigmoid(gr))).astype(rec_ref.dtype)
            return carry

        lax.fori_loop(0, nc, chunk, 0)

    in_specs = _hgrn_specs(tb, nb, False)
    out_blk = pl.BlockSpec((tb, HGRN_PAIR * HGRN_DIM), lambda h, t: (t, h))
    return _hosted_call(
        body, name=name, grid=(HGRN_HEADS // HGRN_PAIR, nb), in_specs=in_specs,
        out_specs=[out_blk, out_blk, pl.BlockSpec((HGRN_PAIR, nc, HGRN_DIM, HGRN_DIM), lambda h, t: (h, t, 0, 0)),
                   pl.BlockSpec((tb, HGRN_PAIR * c), lambda h, t: (t, h))],
        out_shape=[jax.ShapeDtypeStruct((T, 512), BF16), jax.ShapeDtypeStruct((T, 512), F32),
                   jax.ShapeDtypeStruct((HGRN_HEADS, T // c, HGRN_DIM, HGRN_DIM), F32),
                   jax.ShapeDtypeStruct((T, HGRN_HEADS * c), BF16)],
        scratch=[pltpu.VMEM((HGRN_PAIR, HGRN_DIM, HGRN_DIM), F32)], args=(z, lb, onw),
        semantics=("parallel", "arbitrary"), exchange=exchange)


def _hgrn_bwd(z, lb, onw, o, states, scores, dcat, name, exchange=None):
    T = z.shape[0]
    tb = min(HGRN_ROWS, T)
    nb, c, nc = T // tb, HGRN_CHUNK, min(HGRN_ROWS, T) // HGRN_CHUNK

    def body(z_ref, lb_ref, onw_ref, o_ref, st_ref, drec_ref, a_ref,
             dqr_ref, dfr_ref, dir_ref, dgr_ref, dlb_ref, donw_ref, dstate):
        @pl.when(pl.program_id(1) == 0)
        def _():
            dstate[...] = jnp.zeros_like(dstate)
            dlb_ref[...] = jnp.zeros_like(dlb_ref)

        @pl.when((pl.program_id(0) == 0) & (pl.program_id(1) == 0))
        def _():
            donw_ref[...] = jnp.zeros_like(donw_ref)

        consts = _hgrn_consts()
        rev_mat, eye, same_block, upper = consts[1:5]
        below = consts[6]
        lbv = lb_ref[...]
        onwv = onw_ref[...]
        last = lax.broadcasted_iota(jnp.int32, (c, 1), 0) == c - 1

        def chunk(i, carry):
            ci = nc - 1 - i
            sl = pl.ds(pl.multiple_of(ci * c, c), c)
            hs = range(HGRN_PAIR)
            lss = [slice(HGRN_DIM * hh, HGRN_DIM * (hh + 1)) for hh in hs]
            qr = [_hgrn_z(z_ref, sl, Z_Q, hh) for hh in hs]
            gates = [_hgrn_gates(qr[hh], _hgrn_z(z_ref, sl, Z_F, hh), lbv[:, lss[hh]]) for hh in hs]
            q, sq, sf, f, k, g = ([t[j] for t in gates] for j in range(6))
            v = [_bf(_hgrn_z(z_ref, sl, Z_I, hh)) for hh in hs]
            b = [_fold3(_dot(consts[0], _split3(g[hh]))) for hh in hs]
            _, saved = _hgrn_intra(q, k, g, b, consts, scores=False)
            a = [a_ref[sl, c * hh:c * (hh + 1)] for hh in hs]
            st = [st_ref[hh, ci] for hh in hs]
            dst = [dstate[hh] for hh in hs]

            gr = [_hgrn_z(z_ref, sl, Z_G, hh) for hh in hs]
            sg = [_sigmoid(gr[hh]) for hh in hs]
            norm = [_rms(o_ref[sl, ls]) for ls in lss]
            oh, r = [t[0] for t in norm], [t[1] for t in norm]
            drec = [drec_ref[sl, ls].astype(F32) for ls in lss]
            don = [drec[hh] * (gr[hh] * sg[hh]) for hh in hs]
            do = [_bf(_rms_bwd(don[hh] * onwv, oh[hh], r[hh])) for hh in hs]
            donw = jnp.sum(don[0] * oh[0], axis=0, keepdims=True)
            for hh in hs:
                dgr_ref[sl, lss[hh]] = (drec[hh] * oh[hh] * onwv
                                        * (sg[hh] * (1.0 + gr[hh] * (1.0 - sg[hh])))).astype(dgr_ref.dtype)
                if hh:
                    donw = donw + jnp.sum(don[hh] * oh[hh], axis=0, keepdims=True)
            donw_ref[...] += donw

            eb = [jnp.exp(b[hh]) for hh in hs]
            bl = [b[hh][c - 1:c, :] for hh in hs]
            ebl = [jnp.exp(bl[hh]) for hh in hs]
            ekb = [jnp.exp(bl[hh] - b[hh]) for hh in hs]
            qe = [q[hh] * eb[hh] for hh in hs]
            ke = [k[hh] * ekb[hh] for hh in hs]
            da = [_dot(do[hh], v[hh], NT) for hh in hs]
            dat = [_dot(v[hh], do[hh], NT) for hh in hs]
            dqe = [_dot(do[hh], _bf(st[hh])) for hh in hs]
            dke = [_dot(v[hh], _bf(dst[hh])) for hh in hs]
            dv_a = [_dot(a[hh], do[hh], TN) for hh in hs]
            dv_s = [_dot(_bf(ke[hh]), _bf(dst[hh]), NT) for hh in hs]
            dst_in = [_dot(do[hh], _bf(qe[hh]), TN) for hh in hs]
            dad = [jnp.sum(jnp.where(eye, da[hh], 0.0), axis=1, keepdims=True) for hh in hs]
            dq = [dqe[hh] * eb[hh] + dad[hh] * k[hh] for hh in hs]
            dk = [dke[hh] * ekb[hh] + dad[hh] * q[hh] for hh in hs]
            db_last = [jnp.sum(dke[hh] * ke[hh], axis=0, keepdims=True)
                       + jnp.sum(dst[hh] * st[hh], axis=0, keepdims=True) * ebl[hh] for hh in hs]
            for hh in hs:
                dstate[hh] = dst[hh] * ebl[hh] + dst_in[hh]
                dir_ref[sl, lss[hh]] = (dv_a[hh] + dv_s[hh]).astype(dir_ref.dtype)
            for lvl, m in enumerate(HGRN_LEVELS):
                if m == 1:
                    pair = [jnp.sum(jnp.where(below, da[hh], 0.0), axis=1, keepdims=True) for hh in hs]
                    xq = [pair[hh] * pltpu.roll(saved[hh][lvl][2], 1, 0) for hh in hs]
                    xk = [pltpu.roll(pair[hh] * saved[hh][lvl][1], c - 1, 0) for hh in hs]
                else:
                    xq = [_dot_hilo(jnp.where(same_block[lvl], da[hh], 0.0), saved[hh][lvl][2]) for hh in hs]
                    xk = [_dot_hilo(jnp.where(same_block[lvl], dat[hh], 0.0), saved[hh][lvl][1]) for hh in hs]
                for hh in hs:
                    e = saved[hh][lvl][0]
                    dq[hh] = dq[hh] + jnp.where(upper[lvl], xq[hh] * e, 0.0)
                    dk[hh] = dk[hh] + jnp.where(upper[lvl], 0.0, xk[hh] * e)
            db = [q[hh] * dq[hh] - k[hh] * dk[hh] + jnp.where(last, db_last[hh], 0.0) for hh in hs]
            dg = [_fold3(_dot(rev_mat, _split3(db[hh]))) for hh in hs]

            for hh in hs:
                ls = lss[hh]
                dqr_ref[sl, ls] = (dq[hh] * (HGRN_DIM ** -0.5)
                                   * (sq[hh] * (1.0 + qr[hh] * (1.0 - sq[hh])))).astype(dqr_ref.dtype)
                dfk = dg[hh] / f[hh] - dk[hh]
                dfr_ref[sl, ls] = ((1.0 - lbv[:, ls]) * sf[hh] * (1.0 - sf[hh]) * dfk).astype(dfr_ref.dtype)
                dlb_ref[:, ls] += jnp.sum((1.0 - sf[hh]) * dfk, axis=0, keepdims=True)
            return carry

        lax.fori_loop(0, nc, chunk, 0)

    in_specs = _hgrn_specs(tb, nb, True)
    rblk = pl.BlockSpec((tb, HGRN_PAIR * HGRN_DIM), lambda h, t: (nb - 1 - t, h))
    in_specs = in_specs + [
        rblk,
        pl.BlockSpec((HGRN_PAIR, nc, HGRN_DIM, HGRN_DIM), lambda h, t: (h, nb - 1 - t, 0, 0)),
        pl.BlockSpec((tb, HGRN_PAIR * HGRN_DIM), lambda h, t: (nb - 1 - t, 4 // HGRN_PAIR + h)),
        pl.BlockSpec((tb, HGRN_PAIR * c), lambda h, t: (nb - 1 - t, h)),
    ]
    return _hosted_call(
        body, name=name, grid=(HGRN_HEADS // HGRN_PAIR, nb), in_specs=in_specs,
        out_specs=[rblk, rblk, rblk, rblk, pl.BlockSpec((1, HGRN_PAIR * HGRN_DIM), lambda h, t: (0, h)),
                   pl.BlockSpec((1, HGRN_DIM), lambda h, t: (0, 0))],
        out_shape=[jax.ShapeDtypeStruct((T, 512), BF16)] * 4
        + [jax.ShapeDtypeStruct((1, 512), F32), jax.ShapeDtypeStruct((1, HGRN_DIM), F32)],
        scratch=[pltpu.VMEM((HGRN_PAIR, HGRN_DIM, HGRN_DIM), F32)], args=(z, lb, onw, o, states, dcat, scores),
        semantics=("arbitrary", "arbitrary"), exchange=exchange)


def _lower_bound(logits, name):
    def body(l_ref, lb_ref):
        l0, l1 = l_ref[0:1, :], l_ref[1:2, :]
        m = jnp.maximum(l0, l1)
        e0, e1 = jnp.exp(l0 - m), jnp.exp(l1 - m)
        lb_ref[...] = e0 / (e0 + e1)

    return pl.pallas_call(
        body, name=name, out_shape=jax.ShapeDtypeStruct((1, logits.shape[1]), F32),
    )(logits)


def _lower_bound_bwd(lb, dlb, name):
    def body(lb_ref, dlb_ref, dl_ref):
        p = lb_ref[...]
        d0 = dlb_ref[...] * p * (1.0 - p)
        dl_ref[0:1, :] = d0
        dl_ref[1:2, :] = -d0

    return pl.pallas_call(
        body, name=name, out_shape=jax.ShapeDtypeStruct((2, lb.shape[1]), F32),
    )(lb, dlb)


CA_ROWS = 512


def _ca_fwd(q, k, v, name):
    T, W = q.shape
    M = k.shape[0]
    tq = min(CA_ROWS, T)
    scale = CA_HEAD_DIM ** -0.5

    def body(q_ref, k_ref, v_ref, o_ref):
        for h in range(CA_HEADS):
            hs = slice(CA_HEAD_DIM * h, CA_HEAD_DIM * (h + 1))
            s = _dot(q_ref[:, hs], k_ref[:, hs], NT) * scale
            p = jnp.exp(s - jnp.max(s, axis=-1, keepdims=True))
            p = p / jnp.sum(p, axis=-1, keepdims=True)
            o_ref[:, hs] = _dot(_bf(p), v_ref[:, hs]).astype(o_ref.dtype)

    full = pl.BlockSpec((M, W), lambda i: (0, 0))
    return pl.pallas_call(
        body, name=name, grid=(T // tq,), in_specs=[_row_spec(tq, W), full, full], out_specs=_row_spec(tq, W),
        out_shape=jax.ShapeDtypeStruct((T, W), BF16), compiler_params=_params("parallel"),
    )(q, k, v)


def _ca_bwd(q, k, v, do, name):
    T, W = q.shape
    M = k.shape[0]
    tq = min(CA_ROWS, T)
    scale = CA_HEAD_DIM ** -0.5

    def body(q_ref, k_ref, v_ref, do_ref, dq_ref, dk_ref, dv_ref):
        @pl.when(pl.program_id(0) == 0)
        def _():
            dk_ref[...] = jnp.zeros_like(dk_ref)
            dv_ref[...] = jnp.zeros_like(dv_ref)

        for h in range(CA_HEADS):
            hs = slice(CA_HEAD_DIM * h, CA_HEAD_DIM * (h + 1))
            qh, kh, vh, doh = q_ref[:, hs], k_ref[:, hs], v_ref[:, hs], do_ref[:, hs]
            s = _dot(qh, kh, NT) * scale
            p = jnp.exp(s - jnp.max(s, axis=-1, keepdims=True))
            p = p / jnp.sum(p, axis=-1, keepdims=True)
            dp = _dot(doh, vh, NT)
            ds = _bf(p * (dp - jnp.sum(p * dp, axis=-1, keepdims=True)) * scale)
            dq_ref[:, hs] = _dot(ds, kh).astype(dq_ref.dtype)
            dk_ref[:, hs] += _dot(ds, qh, TN)
            dv_ref[:, hs] += _dot(_bf(p), doh, TN)

    full = pl.BlockSpec((M, W), lambda i: (0, 0))
    return pl.pallas_call(
        body, name=name, grid=(T // tq,), in_specs=[_row_spec(tq, W), full, full, _row_spec(tq, W)],
        out_specs=[_row_spec(tq, W), full, full],
        out_shape=[jax.ShapeDtypeStruct((T, W), BF16), jax.ShapeDtypeStruct((M, W), F32), jax.ShapeDtypeStruct((M, W), F32)],
        compiler_params=_params("arbitrary"),
    )(q, k, v, do)


FFN_ROWS = 256
FFN_COLS = 1408
GELU_C0 = 0.7978845608028654
GELU_C1 = 0.044715


def _gelu(x):
    x2 = x * x
    t = jnp.tanh((GELU_C0 * x) * (1.0 + GELU_C1 * x2))
    half_x = 0.5 * x
    return half_x + half_x * t, (t, x2, half_x)


def _gelu_grad(parts):
    t, x2, half_x = parts
    return 0.5 + 0.5 * t + (half_x * (1.0 - t * t)) * (GELU_C0 + (3.0 * GELU_C0 * GELU_C1) * x2)


def _shift_down(cur, halo, first, tb):
    row = lax.broadcasted_iota(jnp.int32, (tb, 1), 0)
    h6 = jnp.where(first, 0.0, halo[6:7])
    h7 = jnp.where(first, 0.0, halo[7:8])
    u1 = jnp.where(row == 0, h7, pltpu.roll(cur, 1, 0))
    u2 = jnp.where(row == 0, h6, jnp.where(row == 1, h7, pltpu.roll(cur, 2, 0)))
    return u1, u2


def _conv(u_ref, halo_ref, w_ref, b_ref, half, first, tb):
    cur = u_ref[half]
    u1, u2 = _shift_down(cur, halo_ref[half], first, tb)
    w = w_ref[...]
    return w[0:1] * u2 + w[1:2] * u1 + w[2:3] * cur + b_ref[...], cur, u1, u2


def _ffn_specs(tb, tc, rows_first):
    nj = D_FF // tc
    rc = (lambda a, b: (a, b)) if rows_first else (lambda a, b: (b, a))
    def at(f):
        return lambda a, b: f(*rc(a, b))
    blk = pl.BlockSpec((2, tb, tc), at(lambda t, j: (0, t, j)))
    halo = pl.BlockSpec((2, 8, tc), at(lambda t, j: (0, jnp.maximum(t * (tb // 8) - 1, 0), j)))
    wg = pl.BlockSpec((3, tc), at(lambda t, j: (0, j)))
    wv = pl.BlockSpec((3, tc), at(lambda t, j: (0, j + nj)))
    bg = pl.BlockSpec((1, tc), at(lambda t, j: (0, j)))
    bv = pl.BlockSpec((1, tc), at(lambda t, j: (0, j + nj)))
    flat = pl.BlockSpec((tb, tc), at(lambda t, j: (t, j)))
    return blk, halo, wg, wv, bg, bv, flat


def _glu_fwd(u, cw, cb, name):
    T = u.shape[1]
    tb, tc = min(FFN_ROWS, T), FFN_COLS

    def body(u_ref, halo_ref, wg_ref, wv_ref, bg_ref, bv_ref, a_ref):
        first = pl.program_id(0) == 0
        cg = _conv(u_ref, halo_ref, wg_ref, bg_ref, 0, first, tb)[0]
        cv = _conv(u_ref, halo_ref, wv_ref, bv_ref, 1, first, tb)[0]
        a_ref[...] = (_gelu(cg)[0] * cv).astype(a_ref.dtype)

    blk, halo, wg, wv, bg, bv, flat = _ffn_specs(tb, tc, True)
    return pl.pallas_call(
        body, name=name, grid=(T // tb, D_FF // tc), in_specs=[blk, halo, wg, wv, bg, bv], out_specs=flat,
        out_shape=jax.ShapeDtypeStruct((T, D_FF), BF16), compiler_params=_params("parallel", "parallel"),
    )(u, u, cw, cw, cb, cb)


def _glu_bwd(u, cw, cb, da, name, exchange=None):
    T = u.shape[1]
    tb, tc = min(FFN_ROWS, T), FFN_COLS

    def body(u_ref, halo_ref, wg_ref, wv_ref, bg_ref, bv_ref, da_ref, dc_ref, db_ref, dw_ref):
        first = pl.program_id(1) == 0

        @pl.when(first)
        def _():
            db_ref[...] = jnp.zeros_like(db_ref)
            dw_ref[...] = jnp.zeros_like(dw_ref)

        cg, ug, ug1, ug2 = _conv(u_ref, halo_ref, wg_ref, bg_ref, 0, first, tb)
        cv, uv, uv1, uv2 = _conv(u_ref, halo_ref, wv_ref, bv_ref, 1, first, tb)
        da = da_ref[...]
        gl, parts = _gelu(cg)
        dcg = da * cv * _gelu_grad(parts)
        dcv = da * gl
        dc_ref[0] = dcg
        dc_ref[1] = dcv
        for half, dc, taps in ((0, dcg, (ug2, ug1, ug)), (1, dcv, (uv2, uv1, uv))):
            db_ref[half] += jnp.sum(dc, axis=0, keepdims=True)
            for tap in range(3):
                dw_ref[half, tap:tap + 1, :] += jnp.sum(dc * taps[tap], axis=0, keepdims=True)

    blk, halo, wg, wv, bg, bv, flat = _ffn_specs(tb, tc, False)
    return _hosted_call(
        body, name=name, grid=(D_FF // tc, T // tb), in_specs=[blk, halo, wg, wv, bg, bv, flat],
        out_specs=[blk, pl.BlockSpec((2, 1, tc), lambda j, t: (0, 0, j)), pl.BlockSpec((2, 3, tc), lambda j, t: (0, 0, j))],
        out_shape=[jax.ShapeDtypeStruct((2, T, D_FF), F32), jax.ShapeDtypeStruct((2, 1, D_FF), F32),
                   jax.ShapeDtypeStruct((2, 3, D_FF), F32)],
        scratch=[], args=(u, u, cw, cw, cb, cb, da), semantics=("parallel", "arbitrary"), exchange=exchange)


def _conv_bwd(dc, cw, name):
    T = dc.shape[1]
    tb, tc = min(FFN_ROWS, T), FFN_COLS
    nt, nj = T // tb, D_FF // tc

    def body(dc_ref, halo_ref, wg_ref, wv_ref, du_ref):
        last = pl.program_id(0) == nt - 1
        row = lax.broadcasted_iota(jnp.int32, (tb, 1), 0)
        for half, w_ref in ((0, wg_ref), (1, wv_ref)):
            cur = dc_ref[half]
            halo = halo_ref[half]
            h0 = jnp.where(last, 0.0, halo[0:1])
            h1 = jnp.where(last, 0.0, halo[1:2])
            d1 = jnp.where(row == tb - 1, h0, pltpu.roll(cur, tb - 1, 0))
            d2 = jnp.where(row == tb - 1, h1, jnp.where(row == tb - 2, h0, pltpu.roll(cur, tb - 2, 0)))
            w = w_ref[...]
            du_ref[half] = (w[2:3] * cur + w[1:2] * d1 + w[0:1] * d2).astype(du_ref.dtype)

    blk = pl.BlockSpec((2, tb, tc), lambda t, j: (0, t, j))
    halo = pl.BlockSpec((2, 8, tc), lambda t, j: (0, jnp.minimum((t + 1) * (tb // 8), T // 8 - 1), j))
    wg = pl.BlockSpec((3, tc), lambda t, j: (0, j))
    wv = pl.BlockSpec((3, tc), lambda t, j: (0, j + nj))
    return pl.pallas_call(
        body, name=name, grid=(nt, nj), in_specs=[blk, halo, wg, wv], out_specs=blk,
        out_shape=jax.ShapeDtypeStruct((2, T, D_FF), BF16), compiler_params=_params("parallel", "parallel"),
    )(dc, dc, cw, cw)


def _mesh_pos():
    return lax.axis_index("x"), lax.axis_index("y"), lax.axis_index("c")


def _peer(pos, k):
    return (pos[0] ^ ((k >> 2) & 1), pos[1] ^ ((k >> 1) & 1), pos[2] ^ (k & 1))


def _index(pos):
    return 4 * pos[0] + 2 * pos[1] + pos[2]


class _Exchange:
    def __init__(self, kind, buf, relay=False):
        assert kind in ("gather", "scatter") and not (relay and kind == "scatter")
        self.kind, self.buf, self.relay = kind, buf, relay
        self.out_shape = jax.ShapeDtypeStruct(((N_DEV,) + buf.shape) if kind == "gather" else buf.shape, buf.dtype)
        self.spec = pl.BlockSpec(memory_space=pl.ANY)
        self.scratch = [pltpu.SemaphoreType.DMA((N_DEV - 1,)), pltpu.SemaphoreType.DMA((N_DEV - 1,)),
                        pltpu.SemaphoreType.DMA]

    def _src(self, x_ref, dest):
        return x_ref if self.kind == "gather" else x_ref.at[dest]

    def _copies(self, x_ref, out_ref, send_sems, recv_sems, local_sem):
        pos = _mesh_pos()
        me = _index(pos)
        local = pltpu.make_async_copy(self._src(x_ref, me), out_ref.at[me], local_sem)
        sends, recvs = [], []
        for k in range(1, N_DEV):
            peer = _peer(pos, k)
            sends.append(pltpu.make_async_remote_copy(
                src_ref=self._src(x_ref, _index(peer)), dst_ref=out_ref.at[me], send_sem=send_sems.at[k - 1],
                recv_sem=recv_sems.at[k - 1], device_id=peer, device_id_type=pl.DeviceIdType.MESH))
            recvs.append(pltpu.make_async_remote_copy(
                src_ref=self._src(x_ref, me), dst_ref=out_ref.at[_index(peer)], send_sem=send_sems.at[k - 1],
                recv_sem=recv_sems.at[k - 1], device_id=peer, device_id_type=pl.DeviceIdType.MESH))
        return local, sends, recvs

    def _relay_copies(self, x_ref, out_ref, send_sems, recv_sems, local_sem):
        x, y, c = _mesh_pos()
        me, sibling = (x, y, c), (x, y, 1 - c)
        chips = [(1 - x, y), (x, 1 - y), (1 - x, 1 - y)]

        def copy(k, block, to, own=False):
            return pltpu.make_async_remote_copy(
                src_ref=x_ref if own else out_ref.at[_index(block)], dst_ref=out_ref.at[_index(block)],
                send_sem=send_sems.at[k], recv_sem=recv_sems.at[k], device_id=to, device_id_type=pl.DeviceIdType.MESH)

        local = pltpu.make_async_copy(x_ref, out_ref.at[_index(me)], local_sem)
        first = [copy(0, me, sibling, own=True)] + [copy(1 + j, me, (*chip, c), own=True) for j, chip in enumerate(chips)]
        landed = [copy(1 + j, (*chip, c), me) for j, chip in enumerate(chips)]
        passed = [copy(4 + j, (*chip, c), sibling) for j, chip in enumerate(chips)]
        from_sibling = [copy(0, sibling, me)] + [copy(4 + j, (*chip, 1 - c), me) for j, chip in enumerate(chips)]
        return local, first, landed, passed, from_sibling

    def start(self, *refs):
        if self.relay:
            local, first = self._relay_copies(*refs)[:2]
            local.start()
            for cp in first:
                cp.start()
            return
        local, sends, _ = self._copies(*refs)
        local.start()
        for cp in sends:
            cp.start()

    def finish(self, *refs):
        if self.relay:
            local, first, landed, passed, from_sibling = self._relay_copies(*refs)
            for got, forward in zip(landed, passed):
                got.wait_recv()
                forward.start()
            for cp in from_sibling:
                cp.wait_recv()
            for cp in first + passed:
                cp.wait_send()
            local.wait()
            return
        local, sends, recvs = self._copies(*refs)
        for cp in recvs:
            cp.wait_recv()
        for cp in sends:
            cp.wait_send()
        local.wait()


def _hosted_call(body, *, name, grid, in_specs, out_specs, out_shape, scratch, args, semantics, exchange=None):
    if exchange is None:
        return pl.pallas_call(
            body, name=name, grid=grid, in_specs=in_specs, out_specs=out_specs, out_shape=out_shape,
            scratch_shapes=scratch, compiler_params=_params(*semantics))(*args)
    n_in, n_out, n_scr = len(in_specs), len(out_specs), len(scratch)

    def hosted(*refs):
        ins, x_ref = refs[:n_in], refs[n_in]
        outs, land_ref = refs[n_in + 1:n_in + 1 + n_out], refs[n_in + 1 + n_out]
        rest = refs[n_in + n_out + 2:]
        sems = rest[n_scr:]
        ids = [pl.program_id(a) for a in range(len(grid))]
        first, last = ids[0] == 0, ids[0] == grid[0] - 1
        for a in range(1, len(grid)):
            first, last = first & (ids[a] == 0), last & (ids[a] == grid[a] - 1)

        @pl.when(first)
        def _():
            exchange.start(x_ref, land_ref, *sems)

        body(*ins, *outs, *rest[:n_scr])

        @pl.when(last)
        def _():
            exchange.finish(x_ref, land_ref, *sems)

    return pl.pallas_call(
        hosted, name=name, grid=grid, in_specs=list(in_specs) + [exchange.spec],
        out_specs=list(out_specs) + [exchange.spec], out_shape=list(out_shape) + [exchange.out_shape],
        scratch_shapes=list(scratch) + exchange.scratch, compiler_params=_params(*(["arbitrary"] * len(grid))),
    )(*args, exchange.buf)


def _exchange_alone(exchange, name):
    def body(x_ref, out_ref, send_sems, recv_sems, local_sem):
        exchange.start(x_ref, out_ref, send_sems, recv_sems, local_sem)
        exchange.finish(x_ref, out_ref, send_sems, recv_sems, local_sem)

    return pl.pallas_call(
        body, name=name, out_shape=exchange.out_shape, in_specs=[exchange.spec], out_specs=exchange.spec,
        scratch_shapes=exchange.scratch)(exchange.buf)


def _adamw(w, g, m, v):
    m = ADAM_B1 * m + (1.0 - ADAM_B1) * g
    v = ADAM_B2 * v + (1.0 - ADAM_B2) * (g * g)
    m_hat = m / (1.0 - ADAM_B1 ** ADAM_STEP)
    v_hat = v / (1.0 - ADAM_B2 ** ADAM_STEP)
    delta = -ADAM_LR * (m_hat / (jnp.sqrt(v_hat) + ADAM_EPS) + ADAM_WD * w)
    return delta, m, v


def _sum_rows(parts, r0, rows, name, wmv=None):
    C = parts.shape[2]
    tr = max(t for t in range(16, ROWS + 1, 16) if rows % t == 0 and r0 % t == 0)

    def total(p_ref):
        g = p_ref[0].astype(F32)
        for i in range(1, N_DEV):
            g = g + p_ref[i].astype(F32)
        return g

    p_spec = pl.BlockSpec((N_DEV, tr, C), lambda i: (0, r0 // tr + i, 0))
    if wmv is None:
        def body(p_ref, g_ref):
            g_ref[...] = total(p_ref)

        return pl.pallas_call(
            body, name=name, grid=(rows // tr,), in_specs=[p_spec], out_specs=_row_spec(tr, C),
            out_shape=jax.ShapeDtypeStruct((rows, C), F32), compiler_params=_params("parallel"))(parts)

    def body(p_ref, w_ref, m_ref, v_ref, g_ref, d_ref, mo_ref, vo_ref):
        g = total(p_ref)
        g_ref[0] = g
        d_ref[0], mo_ref[0], vo_ref[0] = _adamw(w_ref[0], g, m_ref[0], v_ref[0])

    blk = pl.BlockSpec((1, tr, C), lambda i: (0, i, 0))
    return pl.pallas_call(
        body, name=name, grid=(rows // tr,), in_specs=[p_spec, blk, blk, blk], out_specs=[blk] * 4,
        out_shape=[jax.ShapeDtypeStruct((1, rows, C), F32)] * 4, compiler_params=_params("parallel"))(parts, *wmv)


def _sum_parts(parts, name):
    _, R, C = parts.shape

    def body(p_ref, g_ref):
        g = p_ref[0]
        for i in range(1, N_DEV):
            g = g + p_ref[i]
        g_ref[...] = g

    return pl.pallas_call(body, name=name, out_shape=jax.ShapeDtypeStruct((R, C), F32))(parts)


def _adamw_call(w, g, m, v, name):
    _, R, C = w.shape
    tr = min(ROWS, R)

    def body(w_ref, g_ref, m_ref, v_ref, d_ref, mo_ref, vo_ref):
        d_ref[...], mo_ref[...], vo_ref[...] = _adamw(w_ref[...], g_ref[...], m_ref[...], v_ref[...])

    blk = pl.BlockSpec((1, tr, C), lambda i: (0, i, 0))
    return pl.pallas_call(
        body, name=name, grid=(R // tr,), in_specs=[blk] * 4, out_specs=[blk] * 3,
        out_shape=[jax.ShapeDtypeStruct(w.shape, F32)] * 3, compiler_params=_params("parallel"))(w, g, m, v)


NORMS = ("mix_pre_norm", "mix_post_norm", "ca_pre_norm", "mem_norm", "ca_post_norm", "ffn_pre_norm", "ffn_post_norm")
SMALL = ("mix_pre_norm", "attn_sinks", "hgrn_lb_logits", "hgrn_out_norm", "mix_post_norm", "ca_pre_norm", "mem_norm",
         "ca_post_norm", "ffn_pre_norm", "ffn_conv_w", "ffn_conv_b", "ffn_post_norm")
SMALL_ROWS = 40
ROW_LOGITS, ROW_MISC, ROW_CONV_B, ROW_CONV_W = 7, 8, 9, 15
LANE_SINKS, LANE_LOSS = 128, 256
FF_PIECES = ((0, 1024), (1024, 2048), (2048, D_FF))


def _pack_small(norm_grads, dlogits, donw, dsinks, loss, d_cb, d_cw, name):
    def body(*refs):
        norm_refs = refs[:len(NORMS)]
        dl_ref, donw_ref, dsink_ref, loss_ref, cb_ref, cw_ref, out_ref = refs[len(NORMS):]
        out_ref[...] = jnp.zeros_like(out_ref)
        for i, ref in enumerate(norm_refs):
            out_ref[i:i + 1, :] = ref[...]
        out_ref[ROW_LOGITS:ROW_LOGITS + 1, 0:512] = dl_ref[0:1, :]
        out_ref[ROW_LOGITS:ROW_LOGITS + 1, 512:1024] = dl_ref[1:2, :]
        out_ref[ROW_MISC:ROW_MISC + 1, 0:HGRN_DIM] = donw_ref[...]
        out_ref[ROW_MISC:ROW_MISC + 1, LANE_SINKS:LANE_SINKS + ATTN_Q_HEADS] = dsink_ref[...]
        out_ref[ROW_MISC:ROW_MISC + 1, LANE_LOSS:LANE_LOSS + LANE] = loss_ref[...]
        for h in range(2):
            for j, (c0, c1) in enumerate(FF_PIECES):
                r = ROW_CONV_B + 3 * h + j
                out_ref[r:r + 1, 0:c1 - c0] = cb_ref[h, :, c0:c1]
                for t in range(3):
                    r = ROW_CONV_W + 3 * (3 * h + t) + j
                    out_ref[r:r + 1, 0:c1 - c0] = cw_ref[h, t:t + 1, c0:c1]

    return pl.pallas_call(
        body, name=name, out_shape=jax.ShapeDtypeStruct((SMALL_ROWS, 1024), F32),
    )(*norm_grads, dlogits, donw, dsinks, loss, d_cb, d_cw)


def _adamw_small(total, g_conv_w, w, m, v, name):
    n = len(SMALL)

    def body(*refs):
        t_ref, gcw_ref = refs[:2]
        w_refs, m_refs, v_refs = (dict(zip(SMALL, refs[2 + n * i:2 + n * (i + 1)])) for i in range(3))
        outs = refs[2 + 3 * n:]
        loss_ref = outs[0]
        g_refs, d_refs, mo_refs, vo_refs = (dict(zip(SMALL, outs[1 + n * i:1 + n * (i + 1)])) for i in range(4))
        loss_ref[...] = t_ref[ROW_MISC:ROW_MISC + 1, LANE_LOSS:LANE_LOSS + 1]

        def step(nm, idx, g):
            g_refs[nm][idx] = g
            d_refs[nm][idx], mo_refs[nm][idx], vo_refs[nm][idx] = _adamw(w_refs[nm][idx], g, m_refs[nm][idx], v_refs[nm][idx])

        everything = (slice(None), slice(None))
        for i, nm in enumerate(NORMS):
            step(nm, everything, t_ref[i:i + 1, :])
        step("hgrn_lb_logits", (slice(0, 1), slice(None)), t_ref[ROW_LOGITS:ROW_LOGITS + 1, 0:512])
        step("hgrn_lb_logits", (slice(1, 2), slice(None)), t_ref[ROW_LOGITS:ROW_LOGITS + 1, 512:1024])
        step("hgrn_out_norm", everything, t_ref[ROW_MISC:ROW_MISC + 1, 0:HGRN_DIM])
        step("attn_sinks", everything, t_ref[ROW_MISC:ROW_MISC + 1, LANE_SINKS:LANE_SINKS + ATTN_Q_HEADS])
        for h in range(2):
            for j, (c0, c1) in enumerate(FF_PIECES):
                r = ROW_CONV_B + 3 * h + j
                step("ffn_conv_b", (slice(None), slice(D_FF * h + c0, D_FF * h + c1)), t_ref[r:r + 1, 0:c1 - c0])
        step("ffn_conv_w", (slice(None), slice(None), slice(None)), gcw_ref[...])

    shapes = [jax.ShapeDtypeStruct(w[nm].shape, F32) for nm in SMALL]
    out = pl.pallas_call(
        body, name=name, out_shape=[jax.ShapeDtypeStruct((1, 1), F32)] + shapes * 4,
    )(total, g_conv_w, *[w[nm] for nm in SMALL], *[m[nm] for nm in SMALL], *[v[nm] for nm in SMALL])
    trees = [dict(zip(SMALL, out[1 + n * i:1 + n * (i + 1)])) for i in range(4)]
    return out[0], trees


BIG = ("w_in", "w_out", "ca_wq", "ca_wk", "ca_wv", "ca_wo", "ffn_w_up", "ffn_w_down")
BIG_FULL = {"w_in": (1024, 2816), "w_out": (1024, 1024), "ca_wq": (1024, 1024), "ca_wk": (1024, 1024),
            "ca_wv": (1024, 1024), "ca_wo": (1024, 1024), "ffn_w_up": (1024, 5632), "ffn_w_down": (2816, 1024)}
G_IN, G_MID, G_UP, G_DOWN = ("w_in",), ("w_out", "ca_wq", "ca_wk", "ca_wv", "ca_wo"), ("ffn_w_up",), ("ffn_w_down",)
GROUPS = (G_IN, G_MID, G_UP, G_DOWN)
COL_SHARDED = ("w_in", "ffn_w_up")
PACK_COLS = 1024


def _big_rows(name):
    r, c = BIG_FULL[name]
    return r * c // N_DEV // PACK_COLS


def _pack_shards(w, names):
    rows = [w[n][0].T if n in COL_SHARDED else w[n][0] for n in names]
    return (rows[0] if len(rows) == 1 else jnp.concatenate(rows, axis=0)).astype(BF16)


def _unpack_gathered(gathered, names):
    out, r0 = {}, 0
    for n in names:
        rows = _big_rows(n)
        out[n] = gathered[:, r0:r0 + rows].reshape(N_DEV * rows, PACK_COLS)
        r0 += rows
    return out


def _pack_full_grads(grads, names):
    parts = [grads[n].reshape(N_DEV, _big_rows(n), PACK_COLS) for n in names]
    return parts[0] if len(parts) == 1 else jnp.concatenate(parts, axis=1)


def kernel(x, mem, mix_pre_norm, w_in, attn_sinks, hgrn_lb_logits, hgrn_out_norm, w_out, mix_post_norm, ca_pre_norm, mem_norm, ca_wq, ca_wk, ca_wv, ca_wo, ca_post_norm, ffn_pre_norm, ffn_w_up, ffn_conv_w, ffn_conv_b, ffn_w_down, ffn_post_norm, loss_target, m_mix_pre_norm, m_w_in, m_attn_sinks, m_hgrn_lb_logits, m_hgrn_out_norm, m_w_out, m_mix_post_norm, m_ca_pre_norm, m_mem_norm, m_ca_wq, m_ca_wk, m_ca_wv, m_ca_wo, m_ca_post_norm, m_ffn_pre_norm, m_ffn_w_up, m_ffn_conv_w, m_ffn_conv_b, m_ffn_w_down, m_ffn_post_norm, v_mix_pre_norm, v_w_in, v_attn_sinks, v_hgrn_lb_logits, v_hgrn_out_norm, v_w_out, v_mix_post_norm, v_ca_pre_norm, v_mem_norm, v_ca_wq, v_ca_wk, v_ca_wv, v_ca_wo, v_ca_post_norm, v_ffn_pre_norm, v_ffn_w_up, v_ffn_conv_w, v_ffn_conv_b, v_ffn_w_down, v_ffn_post_norm):
    names = ["mix_pre_norm", "w_in", "attn_sinks", "hgrn_lb_logits", "hgrn_out_norm", "w_out", "mix_post_norm",
             "ca_pre_norm", "mem_norm", "ca_wq", "ca_wk", "ca_wv", "ca_wo", "ca_post_norm", "ffn_pre_norm",
             "ffn_w_up", "ffn_conv_w", "ffn_conv_b", "ffn_w_down", "ffn_post_norm"]
    w_all = dict(zip(names, [mix_pre_norm, w_in, attn_sinks, hgrn_lb_logits, hgrn_out_norm, w_out, mix_post_norm,
                             ca_pre_norm, mem_norm, ca_wq, ca_wk, ca_wv, ca_wo, ca_post_norm, ffn_pre_norm,
                             ffn_w_up, ffn_conv_w, ffn_conv_b, ffn_w_down, ffn_post_norm]))
    m_all = dict(zip(names, [m_mix_pre_norm, m_w_in, m_attn_sinks, m_hgrn_lb_logits, m_hgrn_out_norm, m_w_out,
                             m_mix_post_norm, m_ca_pre_norm, m_mem_norm, m_ca_wq, m_ca_wk, m_ca_wv, m_ca_wo,
                             m_ca_post_norm, m_ffn_pre_norm, m_ffn_w_up, m_ffn_conv_w, m_ffn_conv_b, m_ffn_w_down,
                             m_ffn_post_norm]))
    v_all = dict(zip(names, [v_mix_pre_norm, v_w_in, v_attn_sinks, v_hgrn_lb_logits, v_hgrn_out_norm, v_w_out,
                             v_mix_post_norm, v_ca_pre_norm, v_mem_norm, v_ca_wq, v_ca_wk, v_ca_wv, v_ca_wo,
                             v_ca_post_norm, v_ffn_pre_norm, v_ffn_w_up, v_ffn_conv_w, v_ffn_conv_b, v_ffn_w_down,
                             v_ffn_post_norm]))
    dev = _index(_mesh_pos())

    w_packs = {grp: _pack_shards(w_all, grp) for grp in GROUPS}
    shard_w = D_FF * 2 // N_DEV
    conv_w_rows = _exchange_alone(_Exchange("gather", ffn_conv_w[0]), "gather_conv_w")
    conv_w_full = conv_w_rows.transpose(1, 0, 2).reshape(3, 2 * D_FF)

    received, small_pack, grad_x = _local_step(
        x[0], mem[0], loss_target[0], w_packs, conv_w_full,
        {n: w_all[n] for n in NORMS}, attn_sinks, hgrn_lb_logits, hgrn_out_norm, ffn_conv_b)

    total = _sum_parts(_exchange_alone(_Exchange("gather", small_pack), "gather_small"), "sum_small")
    cw = total[ROW_CONV_W:ROW_CONV_W + 18].reshape(2, 3, 3 * PACK_COLS)[:, :, :D_FF]
    cw = cw.transpose(1, 0, 2).reshape(3, 2 * D_FF)
    g_conv_w = lax.dynamic_slice_in_dim(cw, dev * shard_w, shard_w, axis=1)[None]
    loss, (out_g, out_d, out_m, out_v) = _adamw_small(total, g_conv_w, w_all, m_all, v_all, "adamw_small")

    for grp in GROUPS:
        r0 = 0
        for n in grp:
            rows = _big_rows(n)
            if n in COL_SHARDED:
                g = _sum_rows(received[grp], r0, rows, "sum_" + n).T[None]
                d, mo, vo = _adamw_call(w_all[n], g, m_all[n], v_all[n], "adamw_" + n)
            else:
                g, d, mo, vo = _sum_rows(received[grp], r0, rows, "adamw_" + n, wmv=(w_all[n], m_all[n], v_all[n]))
            out_g[n], out_d[n], out_m[n], out_v[n] = g, d, mo, vo
            r0 += rows

    return (loss[0, 0], grad_x[None], *[out_g[n] for n in names], *[out_d[n] for n in names],
            *[out_m[n] for n in names], *[out_v[n] for n in names])


def _local_step(x, mem, target, w_packs, conv_w, norms, sinks, lb_logits, out_norm, conv_b):
    g1, g2, g3 = norms["mix_pre_norm"], norms["mix_post_norm"], norms["ca_pre_norm"]
    g4, g5, g6, g7 = norms["mem_norm"], norms["ca_post_norm"], norms["ffn_pre_norm"], norms["ffn_post_norm"]

    h1, gathered = _norm_fwd(x, g1, "mix_norm", exchange=_Exchange("gather", w_packs[G_IN], relay=True))
    w_in_t = _unpack_gathered(gathered, G_IN)["w_in"]
    up_shard = w_packs[G_UP]
    up_rows = up_shard.shape[0]
    up_cuts = (0, up_rows // 2, 3 * up_rows // 4, up_rows)
    up_parts = [up_shard[a:b] for a, b in zip(up_cuts[:-1], up_cuts[1:])]
    z, up_0 = _mm(h1, w_in_t, mode="nt", out_dtype=BF16, name="in_proj", tn=1408,
                  exchange=_Exchange("gather", up_parts[0]))
    attn, lse, gathered = _swa_fwd(z, sinks, "swa_fwd", exchange=_Exchange("gather", w_packs[G_DOWN]))
    w_down = _unpack_gathered(gathered, G_DOWN)["ffn_w_down"]
    lb = _lower_bound(lb_logits, "lower_bound")
    rec, o_rec, states, scores, gathered = _hgrn_fwd(
        z, lb, out_norm, "hgrn_fwd", exchange=_Exchange("gather", w_packs[G_MID]))
    w_out, wq, wk, wv, wo = (_unpack_gathered(gathered, G_MID)[n] for n in G_MID)
    cat = jnp.concatenate([attn, rec], axis=1)
    x1, h2, mix, up_1 = _mm(cat, w_out, mode="nn", out_dtype=BF16, name="out_proj",
                            exchange=_Exchange("gather", up_parts[1]), epilogue=_post_pre(x, g2, g3))
    mem_n = _norm_fwd(mem, g4, "mem_norm")
    q = _mm(h2, wq, mode="nn", out_dtype=BF16, name="ca_q")
    k = _mm(mem_n, wk, mode="nn", out_dtype=BF16, name="ca_k")
    v = _mm(mem_n, wv, mode="nn", out_dtype=BF16, name="ca_v")
    oc = _ca_fwd(q, k, v, "ca_fwd")
    x2, h3, c, up_2 = _mm(oc, wo, mode="nn", out_dtype=BF16, name="ca_o",
                          exchange=_Exchange("gather", up_parts[2]), epilogue=_post_pre(x1, g5, g6))
    w_up_t = jnp.concatenate([up_0, up_1, up_2], axis=1).reshape(-1, PACK_COLS)
    u = _mm(h3, w_up_t, mode="nt", out_dtype=F32, name="ffn_up", tn=1408, split_out=True)
    a = _glu_fwd(u, conv_w, conv_b, "glu_fwd")
    dx3, dy, loss_row, dg7 = _mm(a, w_down, mode="nn", out_dtype=BF16, name="ffn_down", tm=512, tk=2816,
                                 epilogue=_final(x2, target, g7))
    loss = loss_row[:, :LANE]

    da = _mm(dy, w_down, mode="nt", out_dtype=F32, name="ffn_down_dx", tn=1408)
    d_w_down = _mm(a, dy, mode="tn", out_dtype=BF16, name="ffn_down_dw", tm=1408, tk=1024)
    dc, d_cb, d_cw, got_down = _glu_bwd(
        u, conv_w, conv_b, da, "glu_bwd",
        exchange=_Exchange("scatter", _pack_full_grads({"ffn_w_down": d_w_down}, G_DOWN)))
    du = _conv_bwd(dc, conv_w, "conv_bwd")
    d_w_up_t = _mm(du, h3, mode="tn", out_dtype=BF16, name="ffn_up_dw", tm=1408, tk=1024, split_a=True)
    dx2, dcv, dg6, dg5, got_up = _mm(
        du, w_up_t, mode="nn", out_dtype=BF16, name="ffn_up_dx", tm=1024, tk=1408, split_a=True,
        exchange=_Exchange("scatter", _pack_full_grads({"ffn_w_up": d_w_up_t}, G_UP)),
        epilogue=_norm_bwd2(dx3, x2, c, g6, g5))
    doc = _mm(dcv, wo, mode="nt", out_dtype=BF16, name="ca_o_dx")
    d_wo = _mm(oc, dcv, mode="tn", out_dtype=BF16, name="ca_o_dw", tm=1024, tk=1024)
    dq, dk, dv = _ca_bwd(q, k, v, doc, "ca_bwd")
    d_wq = _mm(h2, dq, mode="tn", out_dtype=BF16, name="ca_q_dw", tm=1024, tk=1024)
    dx1, dmix, dg3, dg2 = _mm(dq, wq, mode="nt", out_dtype=BF16, name="ca_q_dx",
                              epilogue=_norm_bwd2(dx2, x1, mix, g3, g2))
    d_wk = _mm(mem_n, dk, mode="tn", out_dtype=BF16, name="ca_k_dw", tm=1024)
    d_wv = _mm(mem_n, dv, mode="tn", out_dtype=BF16, name="ca_v_dw", tm=1024)
    dmem_k = _mm(dk, wk, mode="nt", out_dtype=F32, name="ca_k_dx")
    dmem_v = _mm(dv, wv, mode="nt", out_dtype=F32, name="ca_v_dx")
    dg4 = _gain_bwd(mem, dmem_k, dmem_v, "mem_norm_bwd")
    dcat = _mm(dmix, w_out, mode="nt", out_dtype=BF16, name="out_proj_dx")
    d_w_out = _mm(cat, dmix, mode="tn", out_dtype=BF16, name="out_proj_dw", tm=1024, tk=1024)
    mid = {"w_out": d_w_out, "ca_wq": d_wq, "ca_wk": d_wk, "ca_wv": d_wv, "ca_wo": d_wo}
    dqr, dfr, dir_, dgr, dlb, donw, got_mid = _hgrn_bwd(
        z, lb, out_norm, o_rec, states, scores, dcat, "hgrn_bwd",
        exchange=_Exchange("scatter", _pack_full_grads(mid, G_MID)))
    dq_a, dka, dkb, dva, dvb, dsinks = _swa_bwd(z, sinks, dcat, lse, "swa_bwd")
    dz = _assemble_dz(dq_a, dka, dkb, dva, dvb, dqr, dfr, dir_, dgr, "assemble_dz")
    d_w_in_t = _mm(dz, h1, mode="tn", out_dtype=BF16, name="in_proj_dw", tm=1408, tk=1024)
    dx, dg1, got_in = _mm(dz, w_in_t, mode="nn", out_dtype=BF16, name="in_proj_dx", tm=512, tk=2816,
                          exchange=_Exchange("scatter", _pack_full_grads({"w_in": d_w_in_t}, G_IN)),
                          epilogue=_norm_bwd1(dx1, x, g1))

    small_pack = _pack_small(
        (dg1, dg2, dg3, dg4, dg5, dg6, dg7), _lower_bound_bwd(lb, dlb, "lower_bound_bwd"), donw, dsinks, loss,
        d_cb, d_cw, "pack_small")
    return {G_IN: got_in, G_MID: got_mid, G_UP: got_up, G_DOWN: got_down}, small_pack, dx
```

```python
import jax
import jax.numpy as jnp
from jax import lax
from jax.experimental import pallas as pl
from jax.experimental.pallas import tpu as pltpu

F32 = jnp.float32
BF16 = jnp.bfloat16
EPS = 1e-6
N_DEV = 8
MESH_AXES = ("x", "y", "c")

ATTN_HEAD_DIM = 64
ATTN_Q_HEADS = 8
ATTN_KV_HEADS = 2
ATTN_BLOCK = 128
HGRN_HEADS = 4
HGRN_DIM = 128
HGRN_CHUNK = 64
HGRN_PAIR = 4
Z_Q, Z_F, Z_I, Z_G = 768, 1280, 1792, 2304
HGRN_LEVELS = (32, 16, 8, 4, 2, 1)
CA_HEADS = 4
CA_HEAD_DIM = 256
D_FF = 2816

ADAM_LR = 0.001
ADAM_B1 = 0.9
ADAM_B2 = 0.999
ADAM_EPS = 1e-08
ADAM_WD = 0.01
ADAM_STEP = 10

VMEM_LIMIT = 58 << 20
EPILOGUE_ROWS = 256
LANE = 128

NT = (((1,), (1,)), ((), ()))
TN = (((0,), (0,)), ((), ()))


def _params(*sem):
    return pltpu.CompilerParams(dimension_semantics=sem, vmem_limit_bytes=VMEM_LIMIT)


def _tile(n, cap):
    if n <= cap:
        return n
    best = 0
    for t in range(LANE, cap + 1, LANE):
        if n % t == 0:
            best = t
    assert best, (n, cap)
    return best


def _dot(a, b, dims=None):
    if dims is None:
        return jnp.dot(a, b, preferred_element_type=F32)
    return lax.dot_general(a, b, dims, preferred_element_type=F32)


def _bf(x):
    return x.astype(BF16)


def _sigmoid(x):
    return 1.0 / (1.0 + jnp.exp(-x))


def _rms(x):
    r = lax.rsqrt(jnp.mean(x * x, axis=-1, keepdims=True) + EPS)
    return x * r, r


def _rms_bwd(dxh, xh, r):
    return r * (dxh - xh * jnp.mean(dxh * xh, axis=-1, keepdims=True))


def _mm(a, b, *, mode, out_dtype, name, tm=1024, tn=1024, tk=1024, split_a=False, split_b=False, split_out=False,
        exchange=None, epilogue=None):
    def dims(arr, split):
        if split:
            return arr.shape[1], 2 * arr.shape[2]
        return arr.shape

    ar, ac = dims(a, split_a)
    br, bc = dims(b, split_b)
    if mode == "nn":
        M, K, N = ar, ac, bc
        assert br == K
    elif mode == "nt":
        M, K, N = ar, ac, br
        assert bc == K
    else:
        K, M, N = ar, ac, bc
        assert br == K
    a_cols_half = ac // 2 if split_a else None
    b_cols_half = bc // 2 if split_b else None
    tm = _tile(M, tm)
    tn = _tile((N // 2) if (split_out or (split_b and mode != "nt")) else N, tn)
    tk = _tile((K // 2) if ((split_a and mode != "tn") or (split_b and mode == "nt")) else K, tk)
    if split_a and mode == "tn":
        tm = _tile(M // 2, tm)
    gm, gn, gk = M // tm, N // tn, K // tk
    a_bytes, b_bytes = a.size * a.dtype.itemsize, b.size * b.dtype.itemsize
    rows_outer = gk > 1 or a_bytes + gm * b_bytes <= gn * a_bytes + b_bytes
    grid = (gm, gn, gk) if rows_outer else (gn, gm, gk)

    def spec(split, half, blk, rc):
        def imap(p, q, k):
            r, c = rc(*((p, q) if rows_outer else (q, p)), k)
            if not split:
                return (r, c)
            per_half = half // blk[1]
            return (c // per_half, r, c % per_half)

        return pl.BlockSpec(((None,) + blk) if split else blk, imap)

    if mode == "nn":
        a_spec = spec(split_a, a_cols_half, (tm, tk), lambda i, j, k: (i, k))
        b_spec = spec(split_b, b_cols_half, (tk, tn), lambda i, j, k: (k, j))
        dn = None
    elif mode == "nt":
        a_spec = spec(split_a, a_cols_half, (tm, tk), lambda i, j, k: (i, k))
        b_spec = spec(split_b, b_cols_half, (tn, tk), lambda i, j, k: (j, k))
        dn = NT
    else:
        a_spec = spec(split_a, a_cols_half, (tk, tm), lambda i, j, k: (k, i))
        b_spec = spec(split_b, b_cols_half, (tk, tn), lambda i, j, k: (k, j))
        dn = TN
    o_spec = spec(split_out, N // 2 if split_out else None, (tm, tn), lambda i, j, k: (i, j))
    out_shape = (2, M, N // 2) if split_out else (M, N)

    in_specs, out_specs, args = [a_spec, b_spec], [o_spec], (a, b)
    out_shapes = [jax.ShapeDtypeStruct(out_shape, out_dtype)]
    semantics = ("parallel", "parallel", "arbitrary")

    def store(result, extra, outs):
        outs[0][...] = result[...].astype(outs[0].dtype)

    if epilogue is not None:
        assert gn == 1 and not split_out
        n_vec = epilogue.n_out_vecs
        row = pl.BlockSpec((tm, N), lambda p, q, k: ((p if rows_outer else q), 0))
        vec = pl.BlockSpec((1, N), lambda p, q, k: (0, 0))
        in_specs += [row] * len(epilogue.rows) + [vec] * len(epilogue.vecs)
        args += tuple(epilogue.rows) + tuple(epilogue.vecs)
        out_specs = [row] * len(epilogue.out_rows) + [vec] * n_vec
        out_shapes = ([jax.ShapeDtypeStruct((M, N), dt) for dt in epilogue.out_rows]
                      + [jax.ShapeDtypeStruct((1, N), F32)] * n_vec)
        semantics = ("arbitrary",) * 3

        def store(result, extra, outs):
            n_rows, n_out_rows, sub = len(epilogue.rows), len(epilogue.out_rows), min(EPILOGUE_ROWS, tm)
            for r in range(0, tm, sub):
                rows = pl.ds(r, sub)
                epilogue.fn(result[r:r + sub], *[ref.at[rows] for ref in extra[:n_rows]], *extra[n_rows:],
                            *[ref.at[rows] for ref in outs[:n_out_rows]], *outs[n_out_rows:])

    n_extra = len(in_specs) - 2
    n_out = len(out_specs)

    def body(a_ref, b_ref, *refs):
        extra, outs, scratch_refs = refs[:n_extra], refs[n_extra:n_extra + n_out], refs[n_extra + n_out:]
        k = pl.program_id(2)
        if epilogue is not None:
            @pl.when((pl.program_id(0) == 0) & (pl.program_id(1) == 0) & (k == 0))
            def _():
                for ref in outs[n_out - epilogue.n_out_vecs:]:
                    ref[...] = jnp.zeros_like(ref)

        if gk == 1:
            store(_dot(_bf(a_ref[...]), _bf(b_ref[...]), dn), extra, outs)
            return
        acc_ref = scratch_refs[0]

        @pl.when(k == 0)
        def _():
            acc_ref[...] = jnp.zeros_like(acc_ref)

        acc_ref[...] += _dot(_bf(a_ref[...]), _bf(b_ref[...]), dn)

        @pl.when(k == gk - 1)
        def _():
            store(acc_ref, extra, outs)

    out = _hosted_call(
        body, name=name, grid=grid, in_specs=in_specs, out_specs=out_specs, out_shape=out_shapes,
        scratch=[] if gk == 1 else [pltpu.VMEM((tm, tn), F32)], args=args, semantics=semantics, exchange=exchange)
    return out[0] if (exchange is None and epilogue is None) else out


ROWS = 512


def _row_spec(tr, cols):
    return pl.BlockSpec((tr, cols), lambda i: (i, 0))


def _vec_spec(cols):
    return pl.BlockSpec((1, cols), lambda i: (0, 0))


def _norm_fwd(x, g, name, exchange=None):
    T, Dm = x.shape
    tr = min(ROWS, T)

    def body(x_ref, g_ref, h_ref):
        xh, _ = _rms(x_ref[...])
        h_ref[...] = (xh * g_ref[...]).astype(h_ref.dtype)

    out = _hosted_call(
        body, name=name, grid=(T // tr,), in_specs=[_row_spec(tr, Dm), _vec_spec(Dm)], out_specs=[_row_spec(tr, Dm)],
        out_shape=[jax.ShapeDtypeStruct((T, Dm), BF16)], scratch=[], args=(x, g), semantics=("parallel",),
        exchange=exchange)
    return out[0] if exchange is None else out


def _post_pre(x, g_post, g_pre):
    def fn(m, x_ref, gp_ref, gn_ref, xo_ref, h_ref, m_ref):
        mh, _ = _rms(m)
        xn = x_ref[...] + mh * gp_ref[...]
        xo_ref[...] = xn
        xh, _ = _rms(xn)
        h_ref[...] = (xh * gn_ref[...]).astype(h_ref.dtype)
        m_ref[...] = m.astype(m_ref.dtype)

    return _RowEpilogue(fn, [x], [g_post, g_pre], [F32, BF16, BF16], 0)


def _final(x2, target, g_post):
    def fn(y, x_ref, t_ref, g_ref, dx_ref, dy_ref, loss_ref, dg_ref):
        g = g_ref[...]
        yh, r = _rms(y)
        d = x_ref[...] + yh * g - t_ref[...]
        loss_ref[...] += 0.5 * jnp.sum(jnp.mean(d * d, axis=-1, keepdims=True))
        dx = d * (1.0 / d.shape[-1])
        dx_ref[...] = dx
        dy_ref[...] = _rms_bwd(dx * g, yh, r).astype(dy_ref.dtype)
        dg_ref[...] += jnp.sum(dx * yh, axis=0, keepdims=True)

    return _RowEpilogue(fn, [x2, target], [g_post], [F32, BF16], 2)


class _RowEpilogue:
    def __init__(self, fn, rows, vecs, out_rows, n_out_vecs):
        self.fn, self.rows, self.vecs, self.out_rows, self.n_out_vecs = fn, rows, vecs, out_rows, n_out_vecs


def _norm_bwd2(dx_cur, x_prev, m_prev, g_pre, g_post):
    def fn(dh, dx_ref, x_ref, m_ref, gn_ref, gp_ref, dxo_ref, dm_ref, dgn_ref, dgp_ref):
        xh, r = _rms(x_ref[...])
        dx = dx_ref[...] + _rms_bwd(dh * gn_ref[...], xh, r)
        dxo_ref[...] = dx
        dgn_ref[...] += jnp.sum(dh * xh, axis=0, keepdims=True)
        mh, rm = _rms(m_ref[...].astype(F32))
        dm_ref[...] = _rms_bwd(dx * gp_ref[...], mh, rm).astype(dm_ref.dtype)
        dgp_ref[...] += jnp.sum(dx * mh, axis=0, keepdims=True)

    return _RowEpilogue(fn, [dx_cur, x_prev, m_prev], [g_pre, g_post], [F32, BF16], 2)


def _norm_bwd1(dx_cur, x_prev, g_pre):
    def fn(dh, dx_ref, x_ref, gn_ref, dxo_ref, dgn_ref):
        xh, r = _rms(x_ref[...])
        dxo_ref[...] = dx_ref[...] + _rms_bwd(dh * gn_ref[...], xh, r)
        dgn_ref[...] += jnp.sum(dh * xh, axis=0, keepdims=True)

    return _RowEpilogue(fn, [dx_cur, x_prev], [g_pre], [F32], 1)


def _gain_bwd(x, dh_a, dh_b, name):
    T, Dm = x.shape

    def body(x_ref, a_ref, b_ref, dg_ref):
        xh, _ = _rms(x_ref[...])
        dg_ref[...] = jnp.sum((a_ref[...] + b_ref[...]) * xh, axis=0, keepdims=True)

    return pl.pallas_call(
        body, name=name, grid=(1,), in_specs=[_row_spec(T, Dm)] * 3, out_specs=_vec_spec(Dm),
        out_shape=jax.ShapeDtypeStruct((1, Dm), F32), compiler_params=_params("arbitrary"),
    )(x, dh_a, dh_b)


ATTN_GROUP = ATTN_Q_HEADS // ATTN_KV_HEADS
ASSEMBLE_ROWS = 1024


def _swa_mask(n):
    rows = ATTN_GROUP * ATTN_BLOCK
    row = lax.broadcasted_iota(jnp.int32, (rows, 2 * ATTN_BLOCK), 0) & (ATTN_BLOCK - 1)
    col = lax.broadcasted_iota(jnp.int32, (rows, 2 * ATTN_BLOCK), 1)
    diff = row + ATTN_BLOCK - col
    return (diff >= 0) & (diff < ATTN_BLOCK) & ((col >= ATTN_BLOCK) | (n > 0))


def _swa_rows(ref, hk, dtype):
    hd = ATTN_HEAD_DIM
    return jnp.concatenate(
        [ref[:, hd * (hk * ATTN_GROUP + g):hd * (hk * ATTN_GROUP + g + 1)].astype(dtype) for g in range(ATTN_GROUP)],
        axis=0)


def _swa_per_row(vals):
    seg = lax.broadcasted_iota(jnp.int32, (ATTN_GROUP * ATTN_BLOCK, 1), 0) // ATTN_BLOCK
    col = jnp.zeros((ATTN_GROUP * ATTN_BLOCK, 1), F32)
    for g, val in enumerate(vals):
        col = jnp.where(seg == g, val, col)
    return col


def _swa_specs():
    blk = ATTN_BLOCK
    prev = lambda n: jnp.maximum(n - 1, 0)
    return [
        pl.BlockSpec(memory_space=pltpu.SMEM),
        pl.BlockSpec((blk, 512), lambda n: (n, 0)),
        pl.BlockSpec((blk, 128), lambda n: (prev(n), 4)),
        pl.BlockSpec((blk, 128), lambda n: (n, 4)),
        pl.BlockSpec((blk, 128), lambda n: (prev(n), 5)),
        pl.BlockSpec((blk, 128), lambda n: (n, 5)),
    ]


def _swa_fwd(z, sinks, name, exchange=None):
    T = z.shape[0]
    blk, hd = ATTN_BLOCK, ATTN_HEAD_DIM
    scale = hd ** -0.5

    def body(sink_ref, q_ref, kp_ref, kc_ref, vp_ref, vc_ref, o_ref, lse_ref):
        allowed = _swa_mask(pl.program_id(0))
        hks = range(ATTN_KV_HEADS)
        kss = [slice(hd * hk, hd * hk + hd) for hk in hks]
        k = [_bf(jnp.concatenate([kp_ref[:, ks], kc_ref[:, ks]], axis=0)) for ks in kss]
        v = [_bf(jnp.concatenate([vp_ref[:, ks], vc_ref[:, ks]], axis=0)) for ks in kss]
        s = [jnp.where(allowed, _dot(_swa_rows(q_ref, hk, BF16), k[hk], NT) * scale, -1e30) for hk in hks]
        sink = [_swa_per_row([sink_ref[0, hk * ATTN_GROUP + g] for g in range(ATTN_GROUP)]) for hk in hks]
        m = [jnp.maximum(jnp.max(s[hk], axis=-1, keepdims=True), sink[hk]) for hk in hks]
        p = [jnp.exp(s[hk] - m[hk]) for hk in hks]
        l = [jnp.sum(p[hk], axis=-1, keepdims=True) + jnp.exp(sink[hk] - m[hk]) for hk in hks]
        o = [_dot(_bf(p[hk] / l[hk]), v[hk]).astype(o_ref.dtype) for hk in hks]
        for hk in hks:
            lse = m[hk] + jnp.log(l[hk])
            for g in range(ATTN_GROUP):
                h = hk * ATTN_GROUP + g
                o_ref[:, hd * h:hd * (h + 1)] = o[hk][blk * g:blk * (g + 1)]
                lse_ref[:, h:h + 1] = lse[blk * g:blk * (g + 1)]

    return _hosted_call(
        body, name=name, grid=(T // blk,), in_specs=_swa_specs(),
        out_specs=[pl.BlockSpec((blk, 512), lambda n: (n, 0)), pl.BlockSpec((blk, ATTN_Q_HEADS), lambda n: (n, 0))],
        out_shape=[jax.ShapeDtypeStruct((T, 512), BF16), jax.ShapeDtypeStruct((T, ATTN_Q_HEADS), F32)],
        scratch=[], args=(sinks, z, z, z, z, z), semantics=("parallel",), exchange=exchange)


def _swa_bwd(z, sinks, dcat, lse, name):
    T = z.shape[0]
    blk, hd = ATTN_BLOCK, ATTN_HEAD_DIM
    scale = hd ** -0.5
    group = ATTN_Q_HEADS // ATTN_KV_HEADS

    def body(sink_ref, q_ref, kp_ref, kc_ref, vp_ref, vc_ref, do_ref, lse_ref,
             dq_ref, dka_ref, dkb_ref, dva_ref, dvb_ref, dsink_ref):
        @pl.when(pl.program_id(0) == 0)
        def _():
            dsink_ref[...] = jnp.zeros_like(dsink_ref)

        allowed = _swa_mask(pl.program_id(0))
        lane = lax.broadcasted_iota(jnp.int32, (1, ATTN_Q_HEADS), 1)
        dsink = jnp.zeros((1, ATTN_Q_HEADS), F32)
        hks = range(ATTN_KV_HEADS)
        kss = [slice(hd * hk, hd * hk + hd) for hk in hks]
        k = [_bf(jnp.concatenate([kp_ref[:, ks], kc_ref[:, ks]], axis=0)) for ks in kss]
        v = [_bf(jnp.concatenate([vp_ref[:, ks], vc_ref[:, ks]], axis=0)) for ks in kss]
        qs = [_swa_rows(q_ref, hk, BF16) for hk in hks]
        dos = [_swa_rows(do_ref, hk, BF16) for hk in hks]
        lse = [jnp.concatenate([lse_ref[:, hk * group + g:hk * group + g + 1] for g in range(group)], axis=0)
               for hk in hks]
        s = [_dot(qs[hk], k[hk], NT) * scale for hk in hks]
        dp = [_dot(dos[hk], v[hk], NT) for hk in hks]
        p = [jnp.where(allowed, jnp.exp(jnp.where(allowed, s[hk], -1e30) - lse[hk]), 0.0) for hk in hks]
        delta = [jnp.sum(p[hk] * dp[hk], axis=-1, keepdims=True) for hk in hks]
        ds = [_bf(p[hk] * (dp[hk] - delta[hk]) * scale) for hk in hks]
        dq = [_dot(ds[hk], k[hk]).astype(dq_ref.dtype) for hk in hks]
        dk = [_dot(ds[hk], qs[hk], TN) for hk in hks]
        dv = [_dot(_bf(p[hk]), dos[hk], TN) for hk in hks]
        for hk in hks:
            sink = _swa_per_row([sink_ref[0, hk * group + g] for g in range(group)])
            sink_part = jnp.exp(sink - lse[hk]) * delta[hk]
            for g in range(group):
                h = hk * group + g
                dq_ref[:, hd * h:hd * (h + 1)] = dq[hk][blk * g:blk * (g + 1)]
                dsink = dsink + jnp.where(lane == h, -jnp.sum(sink_part[blk * g:blk * (g + 1)]), 0.0)
            dkb_ref[:, kss[hk]] = dk[hk][:blk]
            dka_ref[:, kss[hk]] = dk[hk][blk:]
            dvb_ref[:, kss[hk]] = dv[hk][:blk]
            dva_ref[:, kss[hk]] = dv[hk][blk:]
        dsink_ref[...] += dsink

    kv_out = pl.BlockSpec((blk, 128), lambda n: (n, 0))
    return pl.pallas_call(
        body, name=name, grid=(T // blk,),
        in_specs=_swa_specs() + [pl.BlockSpec((blk, 512), lambda n: (n, 0)),
                                 pl.BlockSpec((blk, ATTN_Q_HEADS), lambda n: (n, 0))],
        out_specs=[pl.BlockSpec((blk, 512), lambda n: (n, 0)), kv_out, kv_out, kv_out, kv_out,
                   pl.BlockSpec((1, ATTN_Q_HEADS), lambda n: (0, 0))],
        out_shape=[jax.ShapeDtypeStruct((T, 512), BF16)] + [jax.ShapeDtypeStruct((T, 128), F32)] * 4
        + [jax.ShapeDtypeStruct((1, ATTN_Q_HEADS), F32)],
        compiler_params=_params("arbitrary"),
    )(sinks, z, z, z, z, z, dcat, lse)


def _assemble_dz(dq_a, dka, dkb, dva, dvb, dqr, dfr, dir_, dgr, name):
    T = dq_a.shape[0]
    blk = ATTN_BLOCK
    rows = min(ASSEMBLE_ROWS, T)
    nb, per = T // rows, rows // blk

    def body(dq_ref, dka_ref, dkb_ref, dkn_ref, dva_ref, dvb_ref, dvn_ref, dqr_ref, dfr_ref, dir_ref, dgr_ref, o_ref):
        has_next = pl.program_id(0) < nb - 1

        def with_next(a_ref, b_ref, n_ref):
            after = jnp.where(has_next, n_ref[...], 0.0)
            shifted = after if per == 1 else jnp.concatenate([b_ref[blk:, :], after], axis=0)
            return (a_ref[...] + shifted).astype(o_ref.dtype)

        o_ref[:, 0:512] = dq_ref[...]
        o_ref[:, 512:640] = with_next(dka_ref, dkb_ref, dkn_ref)
        o_ref[:, 640:768] = with_next(dva_ref, dvb_ref, dvn_ref)
        o_ref[:, 768:1280] = dqr_ref[...]
        o_ref[:, 1280:1792] = dfr_ref[...]
        o_ref[:, 1792:2304] = dir_ref[...]
        o_ref[:, 2304:2816] = dgr_ref[...]

    cur = lambda w: pl.BlockSpec((rows, w), lambda n: (n, 0))
    nxt = pl.BlockSpec((blk, 128), lambda n: (jnp.minimum((n + 1) * per, T // blk - 1), 0))
    return pl.pallas_call(
        body, name=name, grid=(nb,),
        in_specs=[cur(512), cur(128), cur(128), nxt, cur(128), cur(128), nxt, cur(512), cur(512), cur(512), cur(512)],
        out_specs=pl.BlockSpec((rows, 2816), lambda n: (n, 0)),
        out_shape=jax.ShapeDtypeStruct((T, 2816), BF16), compiler_params=_params("parallel"),
    )(dq_a, dka, dkb, dkb, dva, dvb, dvb, dqr, dfr, dir_, dgr)


HGRN_ROWS = 512


def _hgrn_consts():
    c = HGRN_CHUNK
    r = lax.broadcasted_iota(jnp.int32, (c, c), 0)
    s = lax.broadcasted_iota(jnp.int32, (c, c), 1)
    rcol = lax.broadcasted_iota(jnp.int32, (c, 1), 0)
    same_block, upper = [], []
    for m in HGRN_LEVELS:
        same_block.append((r & ~(2 * m - 1)) == (s & ~(2 * m - 1)))
        upper.append((rcol & (2 * m - 1)) >= m)
    cum_mat = jnp.where(s <= r, 1.0, 0.0).astype(BF16)
    rev_mat = jnp.where(s >= r, 1.0, 0.0).astype(BF16)
    return cum_mat, rev_mat, r == s, same_block, upper, rcol & 3, s == r - 1


def _hgrn_level_decay(g, b, m, pos4):
    c = HGRN_CHUNK
    if m == 1:
        return jnp.exp(jnp.where((pos4 & 1) == 1, g, 0.0))
    if m == 2:
        after, before = pltpu.roll(g, c - 1, 0), pltpu.roll(g, 1, 0)
        return jnp.exp(jnp.where(pos4 == 0, after, jnp.where(pos4 == 1, 0.0, jnp.where(pos4 == 2, g, g + before))))
    b3 = b.reshape(c // (2 * m), 2 * m, HGRN_DIM)
    bref = jnp.broadcast_to(b3[:, m - 1:m, :], b3.shape).reshape(c, HGRN_DIM)
    return jnp.exp(-jnp.abs(b - bref))


def _split3(x):
    hi = _bf(x)
    r1 = x - hi.astype(F32)
    mid = _bf(r1)
    lo = _bf(r1 - mid.astype(F32))
    return jnp.concatenate([hi, mid, lo], axis=1)


def _dot_hilo(a, b):
    r, c = a.shape[0], b.shape[1]
    a_hi, b_hi = _bf(a), _bf(b)
    a2 = jnp.concatenate([a_hi, _bf(a - a_hi.astype(F32))], axis=0)
    b2 = jnp.concatenate([b_hi, _bf(b - b_hi.astype(F32))], axis=1)
    y = _dot(a2, b2)
    return y[:r, :c] + y[:r, c:] + y[r:, :c]


def _fold3(y):
    w = y.shape[1] // 3
    return y[:, :w] + y[:, w:2 * w] + y[:, 2 * w:]


def _hgrn_gates(qr, fr, lb):
    sq = _sigmoid(qr)
    q = qr * sq * (HGRN_DIM ** -0.5)
    sf = _sigmoid(fr)
    f = lb + (1.0 - lb) * sf
    k = (1.0 - lb) * _sigmoid(-fr)
    return q, sq, sf, f, k, jnp.log(f)


def _hgrn_intra(q, k, g, b, consts, scores=True):
    _, _, eye, same_block, upper, pos4, below = consts
    heads = range(len(q))
    a = None
    if scores:
        a = [jnp.where(eye, jnp.sum(q[hh] * k[hh], axis=1, keepdims=True), 0.0) for hh in heads]
    saved = [[] for _ in heads]
    for i, m in enumerate(HGRN_LEVELS):
        up = upper[i]
        e = [_hgrn_level_decay(g[hh], b[hh], m, pos4) for hh in heads]
        qt = [jnp.where(up, q[hh] * e[hh], 0.0) for hh in heads]
        kt = [jnp.where(up, 0.0, k[hh] * e[hh]) for hh in heads]
        for hh in heads:
            saved[hh].append((e[hh], qt[hh], kt[hh]))
        if not scores:
            continue
        if m == 1:
            for hh in heads:
                pair = jnp.sum(qt[hh] * pltpu.roll(kt[hh], 1, 0), axis=1, keepdims=True)
                a[hh] = a[hh] + jnp.where(below, pair, 0.0)
            continue
        p = [_dot(_bf(qt[hh]), _bf(kt[hh]), NT) for hh in heads]
        for hh in heads:
            a[hh] = a[hh] + jnp.where(same_block[i], p[hh], 0.0)
    return a, saved


def _hgrn_specs(tb, nb, rev):
    tmap = (lambda t: nb - 1 - t) if rev else (lambda t: t)
    assert HGRN_PAIR == HGRN_HEADS
    return [pl.BlockSpec((tb, 2816), lambda h, t: (tmap(t), 0)),
            pl.BlockSpec((1, HGRN_PAIR * HGRN_DIM), lambda h, t: (0, h)),
            pl.BlockSpec((1, HGRN_DIM), lambda h, t: (0, 0))]


def _hgrn_z(z_ref, sl, base, head):
    return z_ref[sl, base + HGRN_DIM * head:base + HGRN_DIM * (head + 1)].astype(F32)


def _hgrn_fwd(z, lb, onw, name, exchange=None):
    T = z.shape[0]
    tb = min(HGRN_ROWS, T)
    nb, c, nc = T // tb, HGRN_CHUNK, min(HGRN_ROWS, T) // HGRN_CHUNK

    def body(z_ref, lb_ref, onw_ref, rec_ref, o_ref, st_ref, a_ref, state):
        @pl.when(pl.program_id(1) == 0)
        def _():
            state[...] = jnp.zeros_like(state)

        consts = _hgrn_consts()
        lbv = lb_ref[...]
        onwv = onw_ref[...]

        def chunk(ci, carry):
            sl = pl.ds(pl.multiple_of(ci * c, c), c)
            heads = range(HGRN_PAIR)
            lss = [slice(HGRN_DIM * hh, HGRN_DIM * (hh + 1)) for hh in heads]
            gates = [_hgrn_gates(_hgrn_z(z_ref, sl, Z_Q, hh), _hgrn_z(z_ref, sl, Z_F, hh), lbv[:, lss[hh]])
                     for hh in heads]
            q, k, g = [t[0] for t in gates], [t[4] for t in gates], [t[5] for t in gates]
            v = [_bf(_hgrn_z(z_ref, sl, Z_I, hh)) for hh in heads]
            b = [_fold3(_dot(consts[0], _split3(g[hh]))) for hh in heads]
            a, _ = _hgrn_intra(q, k, g, b, consts)
            st = [state[hh] for hh in heads]
            for hh in heads:
                st_ref[hh, ci] = st[hh]
            bl = [b[hh][c - 1:c, :] for hh in heads]
            o_state = [_dot(_bf(q[hh] * jnp.exp(b[hh])), _bf(st[hh]), NT) for hh in heads]
            kv = [_dot(v[hh], _bf(k[hh] * jnp.exp(bl[hh] - b[hh])), TN) for hh in heads]
            a = [_bf(a[hh]) for hh in heads]
            o = [_dot(a[hh], v[hh]) + o_state[hh] for hh in heads]
            for hh in heads:
                a_ref[sl, c * hh:c * (hh + 1)] = a[hh]
                state[hh] = st[hh] * jnp.exp(bl[hh]) + kv[hh]
                o_ref[sl, lss[hh]] = o[hh]
                oh, _ = _rms(o[hh])
                gr = _hgrn_z(z_ref, sl, Z_G, hh)
                rec_ref[sl, lss[hh]] = (oh * onwv * (gr * _sigmoid(gr))).astype(rec_ref.dtype)
            return carry

        lax.fori_loop(0, nc, chunk, 0)

    in_specs = _hgrn_specs(tb, nb, False)
    out_blk = pl.BlockSpec((tb, HGRN_PAIR * HGRN_DIM), lambda h, t: (t, h))
    return _hosted_call(
        body, name=name, grid=(HGRN_HEADS // HGRN_PAIR, nb), in_specs=in_specs,
        out_specs=[out_blk, out_blk, pl.BlockSpec((HGRN_PAIR, nc, HGRN_DIM, HGRN_DIM), lambda h, t: (h, t, 0, 0)),
                   pl.BlockSpec((tb, HGRN_PAIR * c), lambda h, t: (t, h))],
        out_shape=[jax.ShapeDtypeStruct((T, 512), BF16), jax.ShapeDtypeStruct((T, 512), F32),
                   jax.ShapeDtypeStruct((HGRN_HEADS, T // c, HGRN_DIM, HGRN_DIM), F32),
                   jax.ShapeDtypeStruct((T, HGRN_HEADS * c), BF16)],
        scratch=[pltpu.VMEM((HGRN_PAIR, HGRN_DIM, HGRN_DIM), F32)], args=(z, lb, onw),
        semantics=("parallel", "arbitrary"), exchange=exchange)


def _hgrn_bwd(z, lb, onw, o, states, scores, dcat, name, exchange=None):
    T = z.shape[0]
    tb = min(HGRN_ROWS, T)
    nb, c, nc = T // tb, HGRN_CHUNK, min(HGRN_ROWS, T) // HGRN_CHUNK

    def body(z_ref, lb_ref, onw_ref, o_ref, st_ref, drec_ref, a_ref,
             dqr_ref, dfr_ref, dir_ref, dgr_ref, dlb_ref, donw_ref, dstate):
        @pl.when(pl.program_id(1) == 0)
        def _():
            dstate[...] = jnp.zeros_like(dstate)
            dlb_ref[...] = jnp.zeros_like(dlb_ref)

        @pl.when((pl.program_id(0) == 0) & (pl.program_id(1) == 0))
        def _():
            donw_ref[...] = jnp.zeros_like(donw_ref)

        consts = _hgrn_consts()
        rev_mat, eye, same_block, upper = consts[1:5]
        below = consts[6]
        lbv = lb_ref[...]
        onwv = onw_ref[...]
        last = lax.broadcasted_iota(jnp.int32, (c, 1), 0) == c - 1

        def chunk(i, carry):
            ci = nc - 1 - i
            sl = pl.ds(pl.multiple_of(ci * c, c), c)
            hs = range(HGRN_PAIR)
            lss = [slice(HGRN_DIM * hh, HGRN_DIM * (hh + 1)) for hh in hs]
            qr = [_hgrn_z(z_ref, sl, Z_Q, hh) for hh in hs]
            gates = [_hgrn_gates(qr[hh], _hgrn_z(z_ref, sl, Z_F, hh), lbv[:, lss[hh]]) for hh in hs]
            q, sq, sf, f, k, g = ([t[j] for t in gates] for j in range(6))
            v = [_bf(_hgrn_z(z_ref, sl, Z_I, hh)) for hh in hs]
            b = [_fold3(_dot(consts[0], _split3(g[hh]))) for hh in hs]
            _, saved = _hgrn_intra(q, k, g, b, consts, scores=False)
            a = [a_ref[sl, c * hh:c * (hh + 1)] for hh in hs]
            st = [st_ref[hh, ci] for hh in hs]
            dst = [dstate[hh] for hh in hs]

            gr = [_hgrn_z(z_ref, sl, Z_G, hh) for hh in hs]
            sg = [_sigmoid(gr[hh]) for hh in hs]
            norm = [_rms(o_ref[sl, ls]) for ls in lss]
            oh, r = [t[0] for t in norm], [t[1] for t in norm]
            drec = [drec_ref[sl, ls].astype(F32) for ls in lss]
            don = [drec[hh] * (gr[hh] * sg[hh]) for hh in hs]
            do = [_bf(_rms_bwd(don[hh] * onwv, oh[hh], r[hh])) for hh in hs]
            donw = jnp.sum(don[0] * oh[0], axis=0, keepdims=True)
            for hh in hs:
                dgr_ref[sl, lss[hh]] = (drec[hh] * oh[hh] * onwv
                                        * (sg[hh] * (1.0 + gr[hh] * (1.0 - sg[hh])))).astype(dgr_ref.dtype)
                if hh:
                    donw = donw + jnp.sum(don[hh] * oh[hh], axis=0, keepdims=True)
            donw_ref[...] += donw

            eb = [jnp.exp(b[hh]) for hh in hs]
            bl = [b[hh][c - 1:c, :] for hh in hs]
            ebl = [jnp.exp(bl[hh]) for hh in hs]
            ekb = [jnp.exp(bl[hh] - b[hh]) for hh in hs]
            qe = [q[hh] * eb[hh] for hh in hs]
            ke = [k[hh] * ekb[hh] for hh in hs]
            da = [_dot(do[hh], v[hh], NT) for hh in hs]
            dat = [_dot(v[hh], do[hh], NT) for hh in hs]
            dqe = [_dot(do[hh], _bf(st[hh])) for hh in hs]
            dke = [_dot(v[hh], _bf(dst[hh])) for hh in hs]
            dv_a = [_dot(a[hh], do[hh], TN) for hh in hs]
            dv_s = [_dot(_bf(ke[hh]), _bf(dst[hh]), NT) for hh in hs]
            dst_in = [_dot(do[hh], _bf(qe[hh]), TN) for hh in hs]
            dad = [jnp.sum(jnp.where(eye, da[hh], 0.0), axis=1, keepdims=True) for hh in hs]
            dq = [dqe[hh] * eb[hh] + dad[hh] * k[hh] for hh in hs]
            dk = [dke[hh] * ekb[hh] + dad[hh] * q[hh] for hh in hs]
            db_last = [jnp.sum(dke[hh] * ke[hh], axis=0, keepdims=True)
                       + jnp.sum(dst[hh] * st[hh], axis=0, keepdims=True) * ebl[hh] for hh in hs]
            for hh in hs:
                dstate[hh] = dst[hh] * ebl[hh] + dst_in[hh]
                dir_ref[sl, lss[hh]] = (dv_a[hh] + dv_s[hh]).astype(dir_ref.dtype)
            for lvl, m in enumerate(HGRN_LEVELS):
                if m == 1:
                    pair = [jnp.sum(jnp.where(below, da[hh], 0.0), axis=1, keepdims=True) for hh in hs]
                    xq = [pair[hh] * pltpu.roll(saved[hh][lvl][2], 1, 0) for hh in hs]
                    xk = [pltpu.roll(pair[hh] * saved[hh][lvl][1], c - 1, 0) for hh in hs]
                else:
                    xq = [_dot_hilo(jnp.where(same_block[lvl], da[hh], 0.0), saved[hh][lvl][2]) for hh in hs]
                    xk = [_dot_hilo(jnp.where(same_block[lvl], dat[hh], 0.0), saved[hh][lvl][1]) for hh in hs]
                for hh in hs:
                    e = saved[hh][lvl][0]
                    dq[hh] = dq[hh] + jnp.where(upper[lvl], xq[hh] * e, 0.0)
                    dk[hh] = dk[hh] + jnp.where(upper[lvl], 0.0, xk[hh] * e)
            db = [q[hh] * dq[hh] - k[hh] * dk[hh] + jnp.where(last, db_last[hh], 0.0) for hh in hs]
            dg = [_fold3(_dot(rev_mat, _split3(db[hh]))) for hh in hs]

            for hh in hs:
                ls = lss[hh]
                dqr_ref[sl, ls] = (dq[hh] * (HGRN_DIM ** -0.5)
                                   * (sq[hh] * (1.0 + qr[hh] * (1.0 - sq[hh])))).astype(dqr_ref.dtype)
                dfk = dg[hh] / f[hh] - dk[hh]
                dfr_ref[sl, ls] = ((1.0 - lbv[:, ls]) * sf[hh] * (1.0 - sf[hh]) * dfk).astype(dfr_ref.dtype)
                dlb_ref[:, ls] += jnp.sum((1.0 - sf[hh]) * dfk, axis=0, keepdims=True)
            return carry

        lax.fori_loop(0, nc, chunk, 0)

    in_specs = _hgrn_specs(tb, nb, True)
    rblk = pl.BlockSpec((tb, HGRN_PAIR * HGRN_DIM), lambda h, t: (nb - 1 - t, h))
    in_specs = in_specs + [
        rblk,
        pl.BlockSpec((HGRN_PAIR, nc, HGRN_DIM, HGRN_DIM), lambda h, t: (h, nb - 1 - t, 0, 0)),
        pl.BlockSpec((tb, HGRN_PAIR * HGRN_DIM), lambda h, t: (nb - 1 - t, 4 // HGRN_PAIR + h)),
        pl.BlockSpec((tb, HGRN_PAIR * c), lambda h, t: (nb - 1 - t, h)),
    ]
    return _hosted_call(
        body, name=name, grid=(HGRN_HEADS // HGRN_PAIR, nb), in_specs=in_specs,
        out_specs=[rblk, rblk, rblk, rblk, pl.BlockSpec((1, HGRN_PAIR * HGRN_DIM), lambda h, t: (0, h)),
                   pl.BlockSpec((1, HGRN_DIM), lambda h, t: (0, 0))],
        out_shape=[jax.ShapeDtypeStruct((T, 512), BF16)] * 4
        + [jax.ShapeDtypeStruct((1, 512), F32), jax.ShapeDtypeStruct((1, HGRN_DIM), F32)],
        scratch=[pltpu.VMEM((HGRN_PAIR, HGRN_DIM, HGRN_DIM), F32)], args=(z, lb, onw, o, states, dcat, scores),
        semantics=("arbitrary", "arbitrary"), exchange=exchange)


def _lower_bound(logits, name):
    def body(l_ref, lb_ref):
        l0, l1 = l_ref[0:1, :], l_ref[1:2, :]
        m = jnp.maximum(l0, l1)
        e0, e1 = jnp.exp(l0 - m), jnp.exp(l1 - m)
        lb_ref[...] = e0 / (e0 + e1)

    return pl.pallas_call(
        body, name=name, out_shape=jax.ShapeDtypeStruct((1, logits.shape[1]), F32),
    )(logits)


def _lower_bound_bwd(lb, dlb, name):
    def body(lb_ref, dlb_ref, dl_ref):
        p = lb_ref[...]
        d0 = dlb_ref[...] * p * (1.0 - p)
        dl_ref[0:1, :] = d0
        dl_ref[1:2, :] = -d0

    return pl.pallas_call(
        body, name=name, out_shape=jax.ShapeDtypeStruct((2, lb.shape[1]), F32),
    )(lb, dlb)


CA_ROWS = 512


def _ca_fwd(q, k, v, name):
    T, W = q.shape
    M = k.shape[0]
    tq = min(CA_ROWS, T)
    scale = CA_HEAD_DIM ** -0.5

    def body(q_ref, k_ref, v_ref, o_ref):
        hss = [slice(CA_HEAD_DIM * h, CA_HEAD_DIM * (h + 1)) for h in range(CA_HEADS)]
        s = [_dot(q_ref[:, hs], k_ref[:, hs], NT) * scale for hs in hss]
        p = [jnp.exp(sh - jnp.max(sh, axis=-1, keepdims=True)) for sh in s]
        p = [ph / jnp.sum(ph, axis=-1, keepdims=True) for ph in p]
        o = [_dot(_bf(ph), v_ref[:, hs]) for ph, hs in zip(p, hss)]
        for oh, hs in zip(o, hss):
            o_ref[:, hs] = oh.astype(o_ref.dtype)

    full = pl.BlockSpec((M, W), lambda i: (0, 0))
    return pl.pallas_call(
        body, name=name, grid=(T // tq,), in_specs=[_row_spec(tq, W), full, full], out_specs=_row_spec(tq, W),
        out_shape=jax.ShapeDtypeStruct((T, W), BF16), compiler_params=_params("parallel"),
    )(q, k, v)


def _ca_bwd(q, k, v, do, name):
    T, W = q.shape
    M = k.shape[0]
    tq = min(CA_ROWS, T)
    scale = CA_HEAD_DIM ** -0.5

    def body(q_ref, k_ref, v_ref, do_ref, dq_ref, dk_ref, dv_ref):
        @pl.when(pl.program_id(0) == 0)
        def _():
            dk_ref[...] = jnp.zeros_like(dk_ref)
            dv_ref[...] = jnp.zeros_like(dv_ref)

        heads = range(CA_HEADS)
        hss = [slice(CA_HEAD_DIM * h, CA_HEAD_DIM * (h + 1)) for h in heads]
        qh, kh = [q_ref[:, hs] for hs in hss], [k_ref[:, hs] for hs in hss]
        vh, doh = [v_ref[:, hs] for hs in hss], [do_ref[:, hs] for hs in hss]
        s = [_dot(qh[h], kh[h], NT) * scale for h in heads]
        dp = [_dot(doh[h], vh[h], NT) for h in heads]
        p = [jnp.exp(s[h] - jnp.max(s[h], axis=-1, keepdims=True)) for h in heads]
        p = [p[h] / jnp.sum(p[h], axis=-1, keepdims=True) for h in heads]
        ds = [_bf(p[h] * (dp[h] - jnp.sum(p[h] * dp[h], axis=-1, keepdims=True)) * scale) for h in heads]
        dq = [_dot(ds[h], kh[h]) for h in heads]
        dk = [_dot(ds[h], qh[h], TN) for h in heads]
        dv = [_dot(_bf(p[h]), doh[h], TN) for h in heads]
        for h in heads:
            dq_ref[:, hss[h]] = dq[h].astype(dq_ref.dtype)
            dk_ref[:, hss[h]] += dk[h]
            dv_ref[:, hss[h]] += dv[h]

    full = pl.BlockSpec((M, W), lambda i: (0, 0))
    return pl.pallas_call(
        body, name=name, grid=(T // tq,), in_specs=[_row_spec(tq, W), full, full, _row_spec(tq, W)],
        out_specs=[_row_spec(tq, W), full, full],
        out_shape=[jax.ShapeDtypeStruct((T, W), BF16), jax.ShapeDtypeStruct((M, W), F32), jax.ShapeDtypeStruct((M, W), F32)],
        compiler_params=_params("arbitrary"),
    )(q, k, v, do)


FFN_ROWS = 256
FFN_COLS = 1408
GELU_C0 = 0.7978845608028654
GELU_C1 = 0.044715


def _gelu(x):
    t = jnp.tanh(GELU_C0 * (x + GELU_C1 * x * x * x))
    return 0.5 * x * (1.0 + t), t


def _gelu_grad(x, t):
    return 0.5 * (1.0 + t) + 0.5 * x * (1.0 - t * t) * GELU_C0 * (1.0 + 3.0 * GELU_C1 * x * x)


def _shift_down(cur, halo, first, tb):
    row = lax.broadcasted_iota(jnp.int32, (tb, 1), 0)
    h6 = jnp.where(first, 0.0, halo[6:7])
    h7 = jnp.where(first, 0.0, halo[7:8])
    u1 = jnp.where(row == 0, h7, pltpu.roll(cur, 1, 0))
    u2 = jnp.where(row == 0, h6, jnp.where(row == 1, h7, pltpu.roll(cur, 2, 0)))
    return u1, u2


def _conv(u_ref, halo_ref, w_ref, b_ref, half, first, tb):
    cur = u_ref[half]
    u1, u2 = _shift_down(cur, halo_ref[half], first, tb)
    w = w_ref[...]
    return w[0:1] * u2 + w[1:2] * u1 + w[2:3] * cur + b_ref[...], cur, u1, u2


def _ffn_specs(tb, tc, rows_first):
    nj = D_FF // tc
    rc = (lambda a, b: (a, b)) if rows_first else (lambda a, b: (b, a))
    def at(f):
        return lambda a, b: f(*rc(a, b))
    blk = pl.BlockSpec((2, tb, tc), at(lambda t, j: (0, t, j)))
    halo = pl.BlockSpec((2, 8, tc), at(lambda t, j: (0, jnp.maximum(t * (tb // 8) - 1, 0), j)))
    wg = pl.BlockSpec((3, tc), at(lambda t, j: (0, j)))
    wv = pl.BlockSpec((3, tc), at(lambda t, j: (0, j + nj)))
    bg = pl.BlockSpec((1, tc), at(lambda t, j: (0, j)))
    bv = pl.BlockSpec((1, tc), at(lambda t, j: (0, j + nj)))
    flat = pl.BlockSpec((tb, tc), at(lambda t, j: (t, j)))
    return blk, halo, wg, wv, bg, bv, flat


def _glu_fwd(u, cw, cb, name):
    T = u.shape[1]
    tb, tc = min(FFN_ROWS, T), FFN_COLS

    def body(u_ref, halo_ref, wg_ref, wv_ref, bg_ref, bv_ref, a_ref):
        first = pl.program_id(0) == 0
        cg = _conv(u_ref, halo_ref, wg_ref, bg_ref, 0, first, tb)[0]
        cv = _conv(u_ref, halo_ref, wv_ref, bv_ref, 1, first, tb)[0]
        a_ref[...] = (_gelu(cg)[0] * cv).astype(a_ref.dtype)

    blk, halo, wg, wv, bg, bv, flat = _ffn_specs(tb, tc, True)
    return pl.pallas_call(
        body, name=name, grid=(T // tb, D_FF // tc), in_specs=[blk, halo, wg, wv, bg, bv], out_specs=flat,
        out_shape=jax.ShapeDtypeStruct((T, D_FF), BF16), compiler_params=_params("parallel", "parallel"),
    )(u, u, cw, cw, cb, cb)


def _glu_bwd(u, cw, cb, da, name, exchange=None):
    T = u.shape[1]
    tb, tc = min(FFN_ROWS, T), FFN_COLS

    def body(u_ref, halo_ref, wg_ref, wv_ref, bg_ref, bv_ref, da_ref, dc_ref, db_ref, dw_ref):
        first = pl.program_id(1) == 0

        @pl.when(first)
        def _():
            db_ref[...] = jnp.zeros_like(db_ref)
            dw_ref[...] = jnp.zeros_like(dw_ref)

        cg, ug, ug1, ug2 = _conv(u_ref, halo_ref, wg_ref, bg_ref, 0, first, tb)
        cv, uv, uv1, uv2 = _conv(u_ref, halo_ref, wv_ref, bv_ref, 1, first, tb)
        da = da_ref[...]
        gl, t = _gelu(cg)
        dcg = da * cv * _gelu_grad(cg, t)
        dcv = da * gl
        dc_ref[0] = dcg
        dc_ref[1] = dcv
        for half, dc, taps in ((0, dcg, (ug2, ug1, ug)), (1, dcv, (uv2, uv1, uv))):
            db_ref[half] += jnp.sum(dc, axis=0, keepdims=True)
            for tap in range(3):
                dw_ref[half, tap:tap + 1, :] += jnp.sum(dc * taps[tap], axis=0, keepdims=True)

    blk, halo, wg, wv, bg, bv, flat = _ffn_specs(tb, tc, False)
    return _hosted_call(
        body, name=name, grid=(D_FF // tc, T // tb), in_specs=[blk, halo, wg, wv, bg, bv, flat],
        out_specs=[blk, pl.BlockSpec((2, 1, tc), lambda j, t: (0, 0, j)), pl.BlockSpec((2, 3, tc), lambda j, t: (0, 0, j))],
        out_shape=[jax.ShapeDtypeStruct((2, T, D_FF), F32), jax.ShapeDtypeStruct((2, 1, D_FF), F32),
                   jax.ShapeDtypeStruct((2, 3, D_FF), F32)],
        scratch=[], args=(u, u, cw, cw, cb, cb, da), semantics=("parallel", "arbitrary"), exchange=exchange)


def _conv_bwd(dc, cw, name):
    T = dc.shape[1]
    tb, tc = min(FFN_ROWS, T), FFN_COLS
    nt, nj = T // tb, D_FF // tc

    def body(dc_ref, halo_ref, wg_ref, wv_ref, du_ref):
        last = pl.program_id(0) == nt - 1
        row = lax.broadcasted_iota(jnp.int32, (tb, 1), 0)
        for half, w_ref in ((0, wg_ref), (1, wv_ref)):
            cur = dc_ref[half]
            halo = halo_ref[half]
            h0 = jnp.where(last, 0.0, halo[0:1])
            h1 = jnp.where(last, 0.0, halo[1:2])
            d1 = jnp.where(row == tb - 1, h0, pltpu.roll(cur, tb - 1, 0))
            d2 = jnp.where(row == tb - 1, h1, jnp.where(row == tb - 2, h0, pltpu.roll(cur, tb - 2, 0)))
            w = w_ref[...]
            du_ref[half] = (w[2:3] * cur + w[1:2] * d1 + w[0:1] * d2).astype(du_ref.dtype)

    blk = pl.BlockSpec((2, tb, tc), lambda t, j: (0, t, j))
    halo = pl.BlockSpec((2, 8, tc), lambda t, j: (0, jnp.minimum((t + 1) * (tb // 8), T // 8 - 1), j))
    wg = pl.BlockSpec((3, tc), lambda t, j: (0, j))
    wv = pl.BlockSpec((3, tc), lambda t, j: (0, j + nj))
    return pl.pallas_call(
        body, name=name, grid=(nt, nj), in_specs=[blk, halo, wg, wv], out_specs=blk,
        out_shape=jax.ShapeDtypeStruct((2, T, D_FF), BF16), compiler_params=_params("parallel", "parallel"),
    )(dc, dc, cw, cw)


def _mesh_pos():
    return lax.axis_index("x"), lax.axis_index("y"), lax.axis_index("c")


def _peer(pos, k):
    return (pos[0] ^ ((k >> 2) & 1), pos[1] ^ ((k >> 1) & 1), pos[2] ^ (k & 1))


def _index(pos):
    return 4 * pos[0] + 2 * pos[1] + pos[2]


class _Exchange:
    def __init__(self, kind, buf, relay=False):
        assert kind in ("gather", "scatter") and not (relay and kind == "scatter")
        self.kind, self.buf, self.relay = kind, buf, relay
        self.out_shape = jax.ShapeDtypeStruct(((N_DEV,) + buf.shape) if kind == "gather" else buf.shape, buf.dtype)
        self.spec = pl.BlockSpec(memory_space=pl.ANY)
        self.scratch = [pltpu.SemaphoreType.DMA((N_DEV - 1,)), pltpu.SemaphoreType.DMA((N_DEV - 1,)),
                        pltpu.SemaphoreType.DMA]

    def _src(self, x_ref, dest):
        return x_ref if self.kind == "gather" else x_ref.at[dest]

    def _copies(self, x_ref, out_ref, send_sems, recv_sems, local_sem):
        pos = _mesh_pos()
        me = _index(pos)
        local = pltpu.make_async_copy(self._src(x_ref, me), out_ref.at[me], local_sem)
        sends, recvs = [], []
        for k in range(1, N_DEV):
            peer = _peer(pos, k)
            sends.append(pltpu.make_async_remote_copy(
                src_ref=self._src(x_ref, _index(peer)), dst_ref=out_ref.at[me], send_sem=send_sems.at[k - 1],
                recv_sem=recv_sems.at[k - 1], device_id=peer, device_id_type=pl.DeviceIdType.MESH))
            recvs.append(pltpu.make_async_remote_copy(
                src_ref=self._src(x_ref, me), dst_ref=out_ref.at[_index(peer)], send_sem=send_sems.at[k - 1],
                recv_sem=recv_sems.at[k - 1], device_id=peer, device_id_type=pl.DeviceIdType.MESH))
        return local, sends, recvs

    def _relay_copies(self, x_ref, out_ref, send_sems, recv_sems, local_sem):
        x, y, c = _mesh_pos()
        me, sibling = (x, y, c), (x, y, 1 - c)
        chips = [(1 - x, y), (x, 1 - y), (1 - x, 1 - y)]

        def copy(k, block, to, own=False):
            return pltpu.make_async_remote_copy(
                src_ref=x_ref if own else out_ref.at[_index(block)], dst_ref=out_ref.at[_index(block)],
                send_sem=send_sems.at[k], recv_sem=recv_sems.at[k], device_id=to, device_id_type=pl.DeviceIdType.MESH)

        local = pltpu.make_async_copy(x_ref, out_ref.at[_index(me)], local_sem)
        first = [copy(0, me, sibling, own=True)] + [copy(1 + j, me, (*chip, c), own=True) for j, chip in enumerate(chips)]
        landed = [copy(1 + j, (*chip, c), me) for j, chip in enumerate(chips)]
        passed = [copy(4 + j, (*chip, c), sibling) for j, chip in enumerate(chips)]
        from_sibling = [copy(0, sibling, me)] + [copy(4 + j, (*chip, 1 - c), me) for j, chip in enumerate(chips)]
        return local, first, landed, passed, from_sibling

    def start(self, *refs):
        if self.relay:
            local, first = self._relay_copies(*refs)[:2]
            local.start()
            for cp in first:
                cp.start()
            return
        local, sends, _ = self._copies(*refs)
        local.start()
        for cp in sends:
            cp.start()

    def finish(self, *refs):
        if self.relay:
            local, first, landed, passed, from_sibling = self._relay_copies(*refs)
            for got, forward in zip(landed, passed):
                got.wait_recv()
                forward.start()
            for cp in from_sibling:
                cp.wait_recv()
            for cp in first + passed:
                cp.wait_send()
            local.wait()
            return
        local, sends, recvs = self._copies(*refs)
        for cp in recvs:
            cp.wait_recv()
        for cp in sends:
            cp.wait_send()
        local.wait()


def _hosted_call(body, *, name, grid, in_specs, out_specs, out_shape, scratch, args, semantics, exchange=None):
    if exchange is None:
        return pl.pallas_call(
            body, name=name, grid=grid, in_specs=in_specs, out_specs=out_specs, out_shape=out_shape,
            scratch_shapes=scratch, compiler_params=_params(*semantics))(*args)
    n_in, n_out, n_scr = len(in_specs), len(out_specs), len(scratch)

    def hosted(*refs):
        ins, x_ref = refs[:n_in], refs[n_in]
        outs, land_ref = refs[n_in + 1:n_in + 1 + n_out], refs[n_in + 1 + n_out]
        rest = refs[n_in + n_out + 2:]
        sems = rest[n_scr:]
        ids = [pl.program_id(a) for a in range(len(grid))]
        first, last = ids[0] == 0, ids[0] == grid[0] - 1
        for a in range(1, len(grid)):
            first, last = first & (ids[a] == 0), last & (ids[a] == grid[a] - 1)

        @pl.when(first)
        def _():
            exchange.start(x_ref, land_ref, *sems)

        body(*ins, *outs, *rest[:n_scr])

        @pl.when(last)
        def _():
            exchange.finish(x_ref, land_ref, *sems)

    return pl.pallas_call(
        hosted, name=name, grid=grid, in_specs=list(in_specs) + [exchange.spec],
        out_specs=list(out_specs) + [exchange.spec], out_shape=list(out_shape) + [exchange.out_shape],
        scratch_shapes=list(scratch) + exchange.scratch, compiler_params=_params(*(["arbitrary"] * len(grid))),
    )(*args, exchange.buf)


def _exchange_alone(exchange, name):
    def body(x_ref, out_ref, send_sems, recv_sems, local_sem):
        exchange.start(x_ref, out_ref, send_sems, recv_sems, local_sem)
        exchange.finish(x_ref, out_ref, send_sems, recv_sems, local_sem)

    return pl.pallas_call(
        body, name=name, out_shape=exchange.out_shape, in_specs=[exchange.spec], out_specs=exchange.spec,
        scratch_shapes=exchange.scratch)(exchange.buf)


def _adamw(w, g, m, v):
    m = ADAM_B1 * m + (1.0 - ADAM_B1) * g
    v = ADAM_B2 * v + (1.0 - ADAM_B2) * (g * g)
    m_hat = m / (1.0 - ADAM_B1 ** ADAM_STEP)
    v_hat = v / (1.0 - ADAM_B2 ** ADAM_STEP)
    delta = -ADAM_LR * (m_hat / (jnp.sqrt(v_hat) + ADAM_EPS) + ADAM_WD * w)
    return delta, m, v


def _sum_rows(parts, r0, rows, name, wmv=None):
    C = parts.shape[2]
    tr = max(t for t in range(16, ROWS + 1, 16) if rows % t == 0 and r0 % t == 0)

    def total(p_ref):
        g = p_ref[0].astype(F32)
        for i in range(1, N_DEV):
            g = g + p_ref[i].astype(F32)
        return g

    p_spec = pl.BlockSpec((N_DEV, tr, C), lambda i: (0, r0 // tr + i, 0))
    if wmv is None:
        def body(p_ref, g_ref):
            g_ref[...] = total(p_ref)

        return pl.pallas_call(
            body, name=name, grid=(rows // tr,), in_specs=[p_spec], out_specs=_row_spec(tr, C),
            out_shape=jax.ShapeDtypeStruct((rows, C), F32), compiler_params=_params("parallel"))(parts)

    def body(p_ref, w_ref, m_ref, v_ref, g_ref, d_ref, mo_ref, vo_ref):
        g = total(p_ref)
        g_ref[0] = g
        d_ref[0], mo_ref[0], vo_ref[0] = _adamw(w_ref[0], g, m_ref[0], v_ref[0])

    blk = pl.BlockSpec((1, tr, C), lambda i: (0, i, 0))
    return pl.pallas_call(
        body, name=name, grid=(rows // tr,), in_specs=[p_spec, blk, blk, blk], out_specs=[blk] * 4,
        out_shape=[jax.ShapeDtypeStruct((1, rows, C), F32)] * 4, compiler_params=_params("parallel"))(parts, *wmv)


def _sum_parts(parts, name):
    _, R, C = parts.shape

    def body(p_ref, g_ref):
        g = p_ref[0]
        for i in range(1, N_DEV):
            g = g + p_ref[i]
        g_ref[...] = g

    return pl.pallas_call(body, name=name, out_shape=jax.ShapeDtypeStruct((R, C), F32))(parts)


def _adamw_call(w, g, m, v, name):
    _, R, C = w.shape
    tr = min(ROWS, R)

    def body(w_ref, g_ref, m_ref, v_ref, d_ref, mo_ref, vo_ref):
        d_ref[...], mo_ref[...], vo_ref[...] = _adamw(w_ref[...], g_ref[...], m_ref[...], v_ref[...])

    blk = pl.BlockSpec((1, tr, C), lambda i: (0, i, 0))
    return pl.pallas_call(
        body, name=name, grid=(R // tr,), in_specs=[blk] * 4, out_specs=[blk] * 3,
        out_shape=[jax.ShapeDtypeStruct(w.shape, F32)] * 3, compiler_params=_params("parallel"))(w, g, m, v)


NORMS = ("mix_pre_norm", "mix_post_norm", "ca_pre_norm", "mem_norm", "ca_post_norm", "ffn_pre_norm", "ffn_post_norm")
SMALL = ("mix_pre_norm", "attn_sinks", "hgrn_lb_logits", "hgrn_out_norm", "mix_post_norm", "ca_pre_norm", "mem_norm",
         "ca_post_norm", "ffn_pre_norm", "ffn_conv_w", "ffn_conv_b", "ffn_post_norm")
SMALL_ROWS = 40
ROW_LOGITS, ROW_MISC, ROW_CONV_B, ROW_CONV_W = 7, 8, 9, 15
LANE_SINKS, LANE_LOSS = 128, 256
FF_PIECES = ((0, 1024), (1024, 2048), (2048, D_FF))


def _pack_small(norm_grads, dlogits, donw, dsinks, loss, d_cb, d_cw, name):
    def body(*refs):
        norm_refs = refs[:len(NORMS)]
        dl_ref, donw_ref, dsink_ref, loss_ref, cb_ref, cw_ref, out_ref = refs[len(NORMS):]
        out_ref[...] = jnp.zeros_like(out_ref)
        for i, ref in enumerate(norm_refs):
            out_ref[i:i + 1, :] = ref[...]
        out_ref[ROW_LOGITS:ROW_LOGITS + 1, 0:512] = dl_ref[0:1, :]
        out_ref[ROW_LOGITS:ROW_LOGITS + 1, 512:1024] = dl_ref[1:2, :]
        out_ref[ROW_MISC:ROW_MISC + 1, 0:HGRN_DIM] = donw_ref[...]
        out_ref[ROW_MISC:ROW_MISC + 1, LANE_SINKS:LANE_SINKS + ATTN_Q_HEADS] = dsink_ref[...]
        out_ref[ROW_MISC:ROW_MISC + 1, LANE_LOSS:LANE_LOSS + LANE] = loss_ref[...]
        for h in range(2):
            for j, (c0, c1) in enumerate(FF_PIECES):
                r = ROW_CONV_B + 3 * h + j
                out_ref[r:r + 1, 0:c1 - c0] = cb_ref[h, :, c0:c1]
                for t in range(3):
                    r = ROW_CONV_W + 3 * (3 * h + t) + j
                    out_ref[r:r + 1, 0:c1 - c0] = cw_ref[h, t:t + 1, c0:c1]

    return pl.pallas_call(
        body, name=name, out_shape=jax.ShapeDtypeStruct((SMALL_ROWS, 1024), F32),
    )(*norm_grads, dlogits, donw, dsinks, loss, d_cb, d_cw)


def _adamw_small(total, g_conv_w, w, m, v, name):
    n = len(SMALL)

    def body(*refs):
        t_ref, gcw_ref = refs[:2]
        w_refs, m_refs, v_refs = (dict(zip(SMALL, refs[2 + n * i:2 + n * (i + 1)])) for i in range(3))
        outs = refs[2 + 3 * n:]
        loss_ref = outs[0]
        g_refs, d_refs, mo_refs, vo_refs = (dict(zip(SMALL, outs[1 + n * i:1 + n * (i + 1)])) for i in range(4))
        loss_ref[...] = t_ref[ROW_MISC:ROW_MISC + 1, LANE_LOSS:LANE_LOSS + 1]

        def step(nm, idx, g):
            g_refs[nm][idx] = g
            d_refs[nm][idx], mo_refs[nm][idx], vo_refs[nm][idx] = _adamw(w_refs[nm][idx], g, m_refs[nm][idx], v_refs[nm][idx])

        everything = (slice(None), slice(None))
        for i, nm in enumerate(NORMS):
            step(nm, everything, t_ref[i:i + 1, :])
        step("hgrn_lb_logits", (slice(0, 1), slice(None)), t_ref[ROW_LOGITS:ROW_LOGITS + 1, 0:512])
        step("hgrn_lb_logits", (slice(1, 2), slice(None)), t_ref[ROW_LOGITS:ROW_LOGITS + 1, 512:1024])
        step("hgrn_out_norm", everything, t_ref[ROW_MISC:ROW_MISC + 1, 0:HGRN_DIM])
        step("attn_sinks", everything, t_ref[ROW_MISC:ROW_MISC + 1, LANE_SINKS:LANE_SINKS + ATTN_Q_HEADS])
        for h in range(2):
            for j, (c0, c1) in enumerate(FF_PIECES):
                r = ROW_CONV_B + 3 * h + j
                step("ffn_conv_b", (slice(None), slice(D_FF * h + c0, D_FF * h + c1)), t_ref[r:r + 1, 0:c1 - c0])
        step("ffn_conv_w", (slice(None), slice(None), slice(None)), gcw_ref[...])

    shapes = [jax.ShapeDtypeStruct(w[nm].shape, F32) for nm in SMALL]
    out = pl.pallas_call(
        body, name=name, out_shape=[jax.ShapeDtypeStruct((1, 1), F32)] + shapes * 4,
    )(total, g_conv_w, *[w[nm] for nm in SMALL], *[m[nm] for nm in SMALL], *[v[nm] for nm in SMALL])
    trees = [dict(zip(SMALL, out[1 + n * i:1 + n * (i + 1)])) for i in range(4)]
    return out[0], trees


BIG = ("w_in", "w_out", "ca_wq", "ca_wk", "ca_wv", "ca_wo", "ffn_w_up", "ffn_w_down")
BIG_FULL = {"w_in": (1024, 2816), "w_out": (1024, 1024), "ca_wq": (1024, 1024), "ca_wk": (1024, 1024),
            "ca_wv": (1024, 1024), "ca_wo": (1024, 1024), "ffn_w_up": (1024, 5632), "ffn_w_down": (2816, 1024)}
G_IN, G_MID, G_UP, G_DOWN = ("w_in",), ("w_out", "ca_wq", "ca_wk", "ca_wv", "ca_wo"), ("ffn_w_up",), ("ffn_w_down",)
GROUPS = (G_IN, G_MID, G_UP, G_DOWN)
COL_SHARDED = ("w_in", "ffn_w_up")
PACK_COLS = 1024


def _big_rows(name):
    r, c = BIG_FULL[name]
    return r * c // N_DEV // PACK_COLS


def _pack_shards(w, names):
    rows = [w[n][0].T if n in COL_SHARDED else w[n][0] for n in names]
    return (rows[0] if len(rows) == 1 else jnp.concatenate(rows, axis=0)).astype(BF16)


def _unpack_gathered(gathered, names):
    out, r0 = {}, 0
    for n in names:
        rows = _big_rows(n)
        out[n] = gathered[:, r0:r0 + rows].reshape(N_DEV * rows, PACK_COLS)
        r0 += rows
    return out


def _pack_full_grads(grads, names):
    parts = [grads[n].reshape(N_DEV, _big_rows(n), PACK_COLS) for n in names]
    return parts[0] if len(parts) == 1 else jnp.concatenate(parts, axis=1)


def kernel(x, mem, mix_pre_norm, w_in, attn_sinks, hgrn_lb_logits, hgrn_out_norm, w_out, mix_post_norm, ca_pre_norm, mem_norm, ca_wq, ca_wk, ca_wv, ca_wo, ca_post_norm, ffn_pre_norm, ffn_w_up, ffn_conv_w, ffn_conv_b, ffn_w_down, ffn_post_norm, loss_target, m_mix_pre_norm, m_w_in, m_attn_sinks, m_hgrn_lb_logits, m_hgrn_out_norm, m_w_out, m_mix_post_norm, m_ca_pre_norm, m_mem_norm, m_ca_wq, m_ca_wk, m_ca_wv, m_ca_wo, m_ca_post_norm, m_ffn_pre_norm, m_ffn_w_up, m_ffn_conv_w, m_ffn_conv_b, m_ffn_w_down, m_ffn_post_norm, v_mix_pre_norm, v_w_in, v_attn_sinks, v_hgrn_lb_logits, v_hgrn_out_norm, v_w_out, v_mix_post_norm, v_ca_pre_norm, v_mem_norm, v_ca_wq, v_ca_wk, v_ca_wv, v_ca_wo, v_ca_post_norm, v_ffn_pre_norm, v_ffn_w_up, v_ffn_conv_w, v_ffn_conv_b, v_ffn_w_down, v_ffn_post_norm):
    names = ["mix_pre_norm", "w_in", "attn_sinks", "hgrn_lb_logits", "hgrn_out_norm", "w_out", "mix_post_norm",
             "ca_pre_norm", "mem_norm", "ca_wq", "ca_wk", "ca_wv", "ca_wo", "ca_post_norm", "ffn_pre_norm",
             "ffn_w_up", "ffn_conv_w", "ffn_conv_b", "ffn_w_down", "ffn_post_norm"]
    w_all = dict(zip(names, [mix_pre_norm, w_in, attn_sinks, hgrn_lb_logits, hgrn_out_norm, w_out, mix_post_norm,
                             ca_pre_norm, mem_norm, ca_wq, ca_wk, ca_wv, ca_wo, ca_post_norm, ffn_pre_norm,
                             ffn_w_up, ffn_conv_w, ffn_conv_b, ffn_w_down, ffn_post_norm]))
    m_all = dict(zip(names, [m_mix_pre_norm, m_w_in, m_attn_sinks, m_hgrn_lb_logits, m_hgrn_out_norm, m_w_out,
                             m_mix_post_norm, m_ca_pre_norm, m_mem_norm, m_ca_wq, m_ca_wk, m_ca_wv, m_ca_wo,
                             m_ca_post_norm, m_ffn_pre_norm, m_ffn_w_up, m_ffn_conv_w, m_ffn_conv_b, m_ffn_w_down,
                             m_ffn_post_norm]))
    v_all = dict(zip(names, [v_mix_pre_norm, v_w_in, v_attn_sinks, v_hgrn_lb_logits, v_hgrn_out_norm, v_w_out,
                             v_mix_post_norm, v_ca_pre_norm, v_mem_norm, v_ca_wq, v_ca_wk, v_ca_wv, v_ca_wo,
                             v_ca_post_norm, v_ffn_pre_norm, v_ffn_w_up, v_ffn_conv_w, v_ffn_conv_b, v_ffn_w_down,
                             v_ffn_post_norm]))
    dev = _index(_mesh_pos())

    w_packs = {grp: _pack_shards(w_all, grp) for grp in GROUPS}
    shard_w = D_FF * 2 // N_DEV
    conv_w_rows = _exchange_alone(_Exchange("gather", ffn_conv_w[0]), "gather_conv_w")
    conv_w_full = conv_w_rows.transpose(1, 0, 2).reshape(3, 2 * D_FF)

    received, small_pack, grad_x = _local_step(
        x[0], mem[0], loss_target[0], w_packs, conv_w_full,
        {n: w_all[n] for n in NORMS}, attn_sinks, hgrn_lb_logits, hgrn_out_norm, ffn_conv_b)

    total = _sum_parts(_exchange_alone(_Exchange("gather", small_pack), "gather_small"), "sum_small")
    cw = total[ROW_CONV_W:ROW_CONV_W + 18].reshape(2, 3, 3 * PACK_COLS)[:, :, :D_FF]
    cw = cw.transpose(1, 0, 2).reshape(3, 2 * D_FF)
    g_conv_w = lax.dynamic_slice_in_dim(cw, dev * shard_w, shard_w, axis=1)[None]
    loss, (out_g, out_d, out_m, out_v) = _adamw_small(total, g_conv_w, w_all, m_all, v_all, "adamw_small")

    for grp in GROUPS:
        r0 = 0
        for n in grp:
            rows = _big_rows(n)
            if n in COL_SHARDED:
                g = _sum_rows(received[grp], r0, rows, "sum_" + n).T[None]
                d, mo, vo = _adamw_call(w_all[n], g, m_all[n], v_all[n], "adamw_" + n)
            else:
                g, d, mo, vo = _sum_rows(received[grp], r0, rows, "adamw_" + n, wmv=(w_all[n], m_all[n], v_all[n]))
            out_g[n], out_d[n], out_m[n], out_v[n] = g, d, mo, vo
            r0 += rows

    return (loss[0, 0], grad_x[None], *[out_g[n] for n in names], *[out_d[n] for n in names],
            *[out_m[n] for n in names], *[out_v[n] for n in names])


def _local_step(x, mem, target, w_packs, conv_w, norms, sinks, lb_logits, out_norm, conv_b):
    g1, g2, g3 = norms["mix_pre_norm"], norms["mix_post_norm"], norms["ca_pre_norm"]
    g4, g5, g6, g7 = norms["mem_norm"], norms["ca_post_norm"], norms["ffn_pre_norm"], norms["ffn_post_norm"]

    h1, gathered = _norm_fwd(x, g1, "mix_norm", exchange=_Exchange("gather", w_packs[G_IN], relay=True))
    w_in_t = _unpack_gathered(gathered, G_IN)["w_in"]
    up_shard = w_packs[G_UP]
    up_rows = up_shard.shape[0]
    up_cuts = (0, up_rows // 2, 3 * up_rows // 4, up_rows)
    up_parts = [up_shard[a:b] for a, b in zip(up_cuts[:-1], up_cuts[1:])]
    z, up_0 = _mm(h1, w_in_t, mode="nt", out_dtype=BF16, name="in_proj", tn=1408,
                  exchange=_Exchange("gather", up_parts[0]))
    attn, lse, gathered = _swa_fwd(z, sinks, "swa_fwd", exchange=_Exchange("gather", w_packs[G_DOWN]))
    w_down = _unpack_gathered(gathered, G_DOWN)["ffn_w_down"]
    lb = _lower_bound(lb_logits, "lower_bound")
    rec, o_rec, states, scores, gathered = _hgrn_fwd(
        z, lb, out_norm, "hgrn_fwd", exchange=_Exchange("gather", w_packs[G_MID]))
    w_out, wq, wk, wv, wo = (_unpack_gathered(gathered, G_MID)[n] for n in G_MID)
    cat = jnp.concatenate([attn, rec], axis=1)
    x1, h2, mix, up_1 = _mm(cat, w_out, mode="nn", out_dtype=BF16, name="out_proj",
                            exchange=_Exchange("gather", up_parts[1]), epilogue=_post_pre(x, g2, g3))
    mem_n = _norm_fwd(mem, g4, "mem_norm")
    q = _mm(h2, wq, mode="nn", out_dtype=BF16, name="ca_q")
    k = _mm(mem_n, wk, mode="nn", out_dtype=BF16, name="ca_k")
    v = _mm(mem_n, wv, mode="nn", out_dtype=BF16, name="ca_v")
    oc = _ca_fwd(q, k, v, "ca_fwd")
    x2, h3, c, up_2 = _mm(oc, wo, mode="nn", out_dtype=BF16, name="ca_o",
                          exchange=_Exchange("gather", up_parts[2]), epilogue=_post_pre(x1, g5, g6))
    w_up_t = jnp.concatenate([up_0, up_1, up_2], axis=1).reshape(-1, PACK_COLS)
    u = _mm(h3, w_up_t, mode="nt", out_dtype=F32, name="ffn_up", tn=1408, split_out=True)
    a = _glu_fwd(u, conv_w, conv_b, "glu_fwd")
    dx3, dy, loss_row, dg7 = _mm(a, w_down, mode="nn", out_dtype=BF16, name="ffn_down", tm=512, tk=2816,
                                 epilogue=_final(x2, target, g7))
    loss = loss_row[:, :LANE]

    da = _mm(dy, w_down, mode="nt", out_dtype=F32, name="ffn_down_dx", tn=1408)
    d_w_down = _mm(a, dy, mode="tn", out_dtype=BF16, name="ffn_down_dw", tm=1408, tk=1024)
    dc, d_cb, d_cw, got_down = _glu_bwd(
        u, conv_w, conv_b, da, "glu_bwd",
        exchange=_Exchange("scatter", _pack_full_grads({"ffn_w_down": d_w_down}, G_DOWN)))
    du = _conv_bwd(dc, conv_w, "conv_bwd")
    d_w_up_t = _mm(du, h3, mode="tn", out_dtype=BF16, name="ffn_up_dw", tm=1408, tk=1024, split_a=True)
    dx2, dcv, dg6, dg5, got_up = _mm(
        du, w_up_t, mode="nn", out_dtype=BF16, name="ffn_up_dx", tm=1024, tk=1408, split_a=True,
        exchange=_Exchange("scatter", _pack_full_grads({"ffn_w_up": d_w_up_t}, G_UP)),
        epilogue=_norm_bwd2(dx3, x2, c, g6, g5))
    doc = _mm(dcv, wo, mode="nt", out_dtype=BF16, name="ca_o_dx")
    d_wo = _mm(oc, dcv, mode="tn", out_dtype=BF16, name="ca_o_dw", tm=1024, tk=1024)
    dq, dk, dv = _ca_bwd(q, k, v, doc, "ca_bwd")
    d_wq = _mm(h2, dq, mode="tn", out_dtype=BF16, name="ca_q_dw", tm=1024, tk=1024)
    dx1, dmix, dg3, dg2 = _mm(dq, wq, mode="nt", out_dtype=BF16, name="ca_q_dx",
                              epilogue=_norm_bwd2(dx2, x1, mix, g3, g2))
    d_wk = _mm(mem_n, dk, mode="tn", out_dtype=BF16, name="ca_k_dw", tm=1024)
    d_wv = _mm(mem_n, dv, mode="tn", out_dtype=BF16, name="ca_v_dw", tm=1024)
    dmem_k = _mm(dk, wk, mode="nt", out_dtype=F32, name="ca_k_dx")
    dmem_v = _mm(dv, wv, mode="nt", out_dtype=F32, name="ca_v_dx")
    dg4 = _gain_bwd(mem, dmem_k, dmem_v, "mem_norm_bwd")
    dcat = _mm(dmix, w_out, mode="nt", out_dtype=BF16, name="out_proj_dx")
    d_w_out = _mm(cat, dmix, mode="tn", out_dtype=BF16, name="out_proj_dw", tm=1024, tk=1024)
    mid = {"w_out": d_w_out, "ca_wq": d_wq, "ca_wk": d_wk, "ca_wv": d_wv, "ca_wo": d_wo}
    dqr, dfr, dir_, dgr, dlb, donw, got_mid = _hgrn_bwd(
        z, lb, out_norm, o_rec, states, scores, dcat, "hgrn_bwd",
        exchange=_Exchange("scatter", _pack_full_grads(mid, G_MID)))
    dq_a, dka, dkb, dva, dvb, dsinks = _swa_bwd(z, sinks, dcat, lse, "swa_bwd")
    dz = _assemble_dz(dq_a, dka, dkb, dva, dvb, dqr, dfr, dir_, dgr, "assemble_dz")
    d_w_in_t = _mm(dz, h1, mode="tn", out_dtype=BF16, name="in_proj_dw", tm=1408, tk=1024)
    dx, dg1, got_in = _mm(dz, w_in_t, mode="nn", out_dtype=BF16, name="in_proj_dx", tm=512, tk=2816,
                          exchange=_Exchange("scatter", _pack_full_grads({"w_in": d_w_in_t}, G_IN)),
                          epilogue=_norm_bwd1(dx1, x, g1))

    small_pack = _pack_small(
        (dg1, dg2, dg3, dg4, dg5, dg6, dg7), _lower_bound_bwd(lb, dlb, "lower_bound_bwd"), donw, dsinks, loss,
        d_cb, d_cw, "pack_small")
    return {G_IN: got_in, G_MID: got_mid, G_UP: got_up, G_DOWN: got_down}, small_pack, dx
```

```python
import jax
import jax.numpy as jnp
from jax import lax
from jax.experimental import pallas as pl
from jax.experimental.pallas import tpu as pltpu

F32 = jnp.float32
BF16 = jnp.bfloat16
EPS = 1e-6
N_DEV = 8
MESH_AXES = ("x", "y", "c")

ATTN_HEAD_DIM = 64
ATTN_Q_HEADS = 8
ATTN_KV_HEADS = 2
ATTN_BLOCK = 128
HGRN_HEADS = 4
HGRN_DIM = 128
HGRN_CHUNK = 64
HGRN_PAIR = 4
Z_Q, Z_F, Z_I, Z_G = 768, 1280, 1792, 2304
HGRN_LEVELS = (32, 16, 8, 4, 2, 1)
CA_HEADS = 4
CA_HEAD_DIM = 256
D_FF = 2816

ADAM_LR = 0.001
ADAM_B1 = 0.9
ADAM_B2 = 0.999
ADAM_EPS = 1e-08
ADAM_WD = 0.01
ADAM_STEP = 10

VMEM_LIMIT = 58 << 20
EPILOGUE_ROWS = 256
LANE = 128

NT = (((1,), (1,)), ((), ()))
TN = (((0,), (0,)), ((), ()))


def _params(*sem):
    return pltpu.CompilerParams(dimension_semantics=sem, vmem_limit_bytes=VMEM_LIMIT)


def _tile(n, cap):
    if n <= cap:
        return n
    best = 0
    for t in range(LANE, cap + 1, LANE):
        if n % t == 0:
            best = t
    assert best, (n, cap)
    return best


def _dot(a, b, dims=None):
    if dims is None:
        return jnp.dot(a, b, preferred_element_type=F32)
    return lax.dot_general(a, b, dims, preferred_element_type=F32)


def _bf(x):
    return x.astype(BF16)


def _sigmoid(x):
    return 1.0 / (1.0 + jnp.exp(-x))


def _rms(x):
    r = lax.rsqrt(jnp.mean(x * x, axis=-1, keepdims=True) + EPS)
    return x * r, r


def _rms_bwd(dxh, xh, r):
    return r * (dxh - xh * jnp.mean(dxh * xh, axis=-1, keepdims=True))


def _mm(a, b, *, mode, out_dtype, name, tm=1024, tn=1024, tk=1024, split_a=False, split_b=False, split_out=False,
        exchange=None, epilogue=None):
    def dims(arr, split):
        if split:
            return arr.shape[1], 2 * arr.shape[2]
        return arr.shape

    ar, ac = dims(a, split_a)
    br, bc = dims(b, split_b)
    if mode == "nn":
        M, K, N = ar, ac, bc
        assert br == K
    elif mode == "nt":
        M, K, N = ar, ac, br
        assert bc == K
    else:
        K, M, N = ar, ac, bc
        assert br == K
    a_cols_half = ac // 2 if split_a else None
    b_cols_half = bc // 2 if split_b else None
    tm = _tile(M, tm)
    tn = _tile((N // 2) if (split_out or (split_b and mode != "nt")) else N, tn)
    tk = _tile((K // 2) if ((split_a and mode != "tn") or (split_b and mode == "nt")) else K, tk)
    if split_a and mode == "tn":
        tm = _tile(M // 2, tm)
    gm, gn, gk = M // tm, N // tn, K // tk
    a_bytes, b_bytes = a.size * a.dtype.itemsize, b.size * b.dtype.itemsize
    rows_outer = gk > 1 or a_bytes + gm * b_bytes <= gn * a_bytes + b_bytes
    grid = (gm, gn, gk) if rows_outer else (gn, gm, gk)

    def spec(split, half, blk, rc):
        def imap(p, q, k):
            r, c = rc(*((p, q) if rows_outer else (q, p)), k)
            if not split:
                return (r, c)
            per_half = half // blk[1]
            return (c // per_half, r, c % per_half)

        return pl.BlockSpec(((None,) + blk) if split else blk, imap)

    if mode == "nn":
        a_spec = spec(split_a, a_cols_half, (tm, tk), lambda i, j, k: (i, k))
        b_spec = spec(split_b, b_cols_half, (tk, tn), lambda i, j, k: (k, j))
        dn = None
    elif mode == "nt":
        a_spec = spec(split_a, a_cols_half, (tm, tk), lambda i, j, k: (i, k))
        b_spec = spec(split_b, b_cols_half, (tn, tk), lambda i, j, k: (j, k))
        dn = NT
    else:
        a_spec = spec(split_a, a_cols_half, (tk, tm), lambda i, j, k: (k, i))
        b_spec = spec(split_b, b_cols_half, (tk, tn), lambda i, j, k: (k, j))
        dn = TN
    o_spec = spec(split_out, N // 2 if split_out else None, (tm, tn), lambda i, j, k: (i, j))
    out_shape = (2, M, N // 2) if split_out else (M, N)

    in_specs, out_specs, args = [a_spec, b_spec], [o_spec], (a, b)
    out_shapes = [jax.ShapeDtypeStruct(out_shape, out_dtype)]
    semantics = ("parallel", "parallel", "arbitrary")

    def store(result, extra, outs):
        outs[0][...] = result[...].astype(outs[0].dtype)

    if epilogue is not None:
        assert gn == 1 and not split_out
        n_vec = epilogue.n_out_vecs
        row = pl.BlockSpec((tm, N), lambda p, q, k: ((p if rows_outer else q), 0))
        vec = pl.BlockSpec((1, N), lambda p, q, k: (0, 0))
        in_specs += [row] * len(epilogue.rows) + [vec] * len(epilogue.vecs)
        args += tuple(epilogue.rows) + tuple(epilogue.vecs)
        out_specs = [row] * len(epilogue.out_rows) + [vec] * n_vec
        out_shapes = ([jax.ShapeDtypeStruct((M, N), dt) for dt in epilogue.out_rows]
                      + [jax.ShapeDtypeStruct((1, N), F32)] * n_vec)
        semantics = ("arbitrary",) * 3

        def store(result, extra, outs):
            n_rows, n_out_rows, sub = len(epilogue.rows), len(epilogue.out_rows), min(EPILOGUE_ROWS, tm)
            for r in range(0, tm, sub):
                rows = pl.ds(r, sub)
                epilogue.fn(result[r:r + sub], *[ref.at[rows] for ref in extra[:n_rows]], *extra[n_rows:],
                            *[ref.at[rows] for ref in outs[:n_out_rows]], *outs[n_out_rows:])

    n_extra = len(in_specs) - 2
    n_out = len(out_specs)

    def body(a_ref, b_ref, *refs):
        extra, outs, scratch_refs = refs[:n_extra], refs[n_extra:n_extra + n_out], refs[n_extra + n_out:]
        k = pl.program_id(2)
        if epilogue is not None:
            @pl.when((pl.program_id(0) == 0) & (pl.program_id(1) == 0) & (k == 0))
            def _():
                for ref in outs[n_out - epilogue.n_out_vecs:]:
                    ref[...] = jnp.zeros_like(ref)

        if gk == 1:
            store(_dot(_bf(a_ref[...]), _bf(b_ref[...]), dn), extra, outs)
            return
        acc_ref = scratch_refs[0]

        @pl.when(k == 0)
        def _():
            acc_ref[...] = jnp.zeros_like(acc_ref)

        acc_ref[...] += _dot(_bf(a_ref[...]), _bf(b_ref[...]), dn)

        @pl.when(k == gk - 1)
        def _():
            store(acc_ref, extra, outs)

    out = _hosted_call(
        body, name=name, grid=grid, in_specs=in_specs, out_specs=out_specs, out_shape=out_shapes,
        scratch=[] if gk == 1 else [pltpu.VMEM((tm, tn), F32)], args=args, semantics=semantics, exchange=exchange)
    return out[0] if (exchange is None and epilogue is None) else out


ROWS = 512


def _row_spec(tr, cols):
    return pl.BlockSpec((tr, cols), lambda i: (i, 0))


def _vec_spec(cols):
    return pl.BlockSpec((1, cols), lambda i: (0, 0))


def _norm_fwd(x, g, name, exchange=None):
    T, Dm = x.shape
    tr = min(ROWS, T)

    def body(x_ref, g_ref, h_ref):
        xh, _ = _rms(x_ref[...])
        h_ref[...] = (xh * g_ref[...]).astype(h_ref.dtype)

    out = _hosted_call(
        body, name=name, grid=(T // tr,), in_specs=[_row_spec(tr, Dm), _vec_spec(Dm)], out_specs=[_row_spec(tr, Dm)],
        out_shape=[jax.ShapeDtypeStruct((T, Dm), BF16)], scratch=[], args=(x, g), semantics=("parallel",),
        exchange=exchange)
    return out[0] if exchange is None else out


def _post_pre(x, g_post, g_pre):
    def fn(m, x_ref, gp_ref, gn_ref, xo_ref, h_ref, m_ref):
        mh, _ = _rms(m)
        xn = x_ref[...] + mh * gp_ref[...]
        xo_ref[...] = xn
        xh, _ = _rms(xn)
        h_ref[...] = (xh * gn_ref[...]).astype(h_ref.dtype)
        m_ref[...] = m.astype(m_ref.dtype)

    return _RowEpilogue(fn, [x], [g_post, g_pre], [F32, BF16, BF16], 0)


def _final(x2, target, g_post):
    def fn(y, x_ref, t_ref, g_ref, dx_ref, dy_ref, loss_ref, dg_ref):
        g = g_ref[...]
        yh, r = _rms(y)
        d = x_ref[...] + yh * g - t_ref[...]
        loss_ref[...] += 0.5 * jnp.sum(jnp.mean(d * d, axis=-1, keepdims=True))
        dx = d * (1.0 / d.shape[-1])
        dx_ref[...] = dx
        dy_ref[...] = _rms_bwd(dx * g, yh, r).astype(dy_ref.dtype)
        dg_ref[...] += jnp.sum(dx * yh, axis=0, keepdims=True)

    return _RowEpilogue(fn, [x2, target], [g_post], [F32, BF16], 2)


class _RowEpilogue:
    def __init__(self, fn, rows, vecs, out_rows, n_out_vecs):
        self.fn, self.rows, self.vecs, self.out_rows, self.n_out_vecs = fn, rows, vecs, out_rows, n_out_vecs


def _norm_bwd2(dx_cur, x_prev, m_prev, g_pre, g_post):
    def fn(dh, dx_ref, x_ref, m_ref, gn_ref, gp_ref, dxo_ref, dm_ref, dgn_ref, dgp_ref):
        xh, r = _rms(x_ref[...])
        dx = dx_ref[...] + _rms_bwd(dh * gn_ref[...], xh, r)
        dxo_ref[...] = dx
        dgn_ref[...] += jnp.sum(dh * xh, axis=0, keepdims=True)
        mh, rm = _rms(m_ref[...].astype(F32))
        dm_ref[...] = _rms_bwd(dx * gp_ref[...], mh, rm).astype(dm_ref.dtype)
        dgp_ref[...] += jnp.sum(dx * mh, axis=0, keepdims=True)

    return _RowEpilogue(fn, [dx_cur, x_prev, m_prev], [g_pre, g_post], [F32, BF16], 2)


def _norm_bwd1(dx_cur, x_prev, g_pre):
    def fn(dh, dx_ref, x_ref, gn_ref, dxo_ref, dgn_ref):
        xh, r = _rms(x_ref[...])
        dxo_ref[...] = dx_ref[...] + _rms_bwd(dh * gn_ref[...], xh, r)
        dgn_ref[...] += jnp.sum(dh * xh, axis=0, keepdims=True)

    return _RowEpilogue(fn, [dx_cur, x_prev], [g_pre], [F32], 1)


def _gain_bwd(x, dh_a, dh_b, name):
    T, Dm = x.shape

    def body(x_ref, a_ref, b_ref, dg_ref):
        xh, _ = _rms(x_ref[...])
        dg_ref[...] = jnp.sum((a_ref[...] + b_ref[...]) * xh, axis=0, keepdims=True)

    return pl.pallas_call(
        body, name=name, grid=(1,), in_specs=[_row_spec(T, Dm)] * 3, out_specs=_vec_spec(Dm),
        out_shape=jax.ShapeDtypeStruct((1, Dm), F32), compiler_params=_params("arbitrary"),
    )(x, dh_a, dh_b)


ATTN_GROUP = ATTN_Q_HEADS // ATTN_KV_HEADS
ASSEMBLE_ROWS = 1024


def _swa_mask(n):
    rows = ATTN_GROUP * ATTN_BLOCK
    row = lax.broadcasted_iota(jnp.int32, (rows, 2 * ATTN_BLOCK), 0) & (ATTN_BLOCK - 1)
    col = lax.broadcasted_iota(jnp.int32, (rows, 2 * ATTN_BLOCK), 1)
    diff = row + ATTN_BLOCK - col
    return (diff >= 0) & (diff < ATTN_BLOCK) & ((col >= ATTN_BLOCK) | (n > 0))


def _swa_rows(ref, hk, dtype):
    hd = ATTN_HEAD_DIM
    return jnp.concatenate(
        [ref[:, hd * (hk * ATTN_GROUP + g):hd * (hk * ATTN_GROUP + g + 1)].astype(dtype) for g in range(ATTN_GROUP)],
        axis=0)


def _swa_per_row(vals):
    seg = lax.broadcasted_iota(jnp.int32, (ATTN_GROUP * ATTN_BLOCK, 1), 0) // ATTN_BLOCK
    col = jnp.zeros((ATTN_GROUP * ATTN_BLOCK, 1), F32)
    for g, val in enumerate(vals):
        col = jnp.where(seg == g, val, col)
    return col


def _swa_specs():
    blk = ATTN_BLOCK
    prev = lambda n: jnp.maximum(n - 1, 0)
    return [
        pl.BlockSpec(memory_space=pltpu.SMEM),
        pl.BlockSpec((blk, 512), lambda n: (n, 0)),
        pl.BlockSpec((blk, 128), lambda n: (prev(n), 4)),
        pl.BlockSpec((blk, 128), lambda n: (n, 4)),
        pl.BlockSpec((blk, 128), lambda n: (prev(n), 5)),
        pl.BlockSpec((blk, 128), lambda n: (n, 5)),
    ]


def _swa_fwd(z, sinks, name, exchange=None):
    T = z.shape[0]
    blk, hd = ATTN_BLOCK, ATTN_HEAD_DIM
    scale = hd ** -0.5

    def body(sink_ref, q_ref, kp_ref, kc_ref, vp_ref, vc_ref, o_ref, lse_ref):
        allowed = _swa_mask(pl.program_id(0))
        hks = range(ATTN_KV_HEADS)
        kss = [slice(hd * hk, hd * hk + hd) for hk in hks]
        k = [_bf(jnp.concatenate([kp_ref[:, ks], kc_ref[:, ks]], axis=0)) for ks in kss]
        v = [_bf(jnp.concatenate([vp_ref[:, ks], vc_ref[:, ks]], axis=0)) for ks in kss]
        s = [jnp.where(allowed, _dot(_swa_rows(q_ref, hk, BF16), k[hk], NT) * scale, -1e30) for hk in hks]
        sink = [_swa_per_row([sink_ref[0, hk * ATTN_GROUP + g] for g in range(ATTN_GROUP)]) for hk in hks]
        m = [jnp.maximum(jnp.max(s[hk], axis=-1, keepdims=True), sink[hk]) for hk in hks]
        p = [jnp.exp(s[hk] - m[hk]) for hk in hks]
        l = [jnp.sum(p[hk], axis=-1, keepdims=True) + jnp.exp(sink[hk] - m[hk]) for hk in hks]
        o = [_dot(_bf(p[hk] / l[hk]), v[hk]).astype(o_ref.dtype) for hk in hks]
        for hk in hks:
            lse = m[hk] + jnp.log(l[hk])
            for g in range(ATTN_GROUP):
                h = hk * ATTN_GROUP + g
                o_ref[:, hd * h:hd * (h + 1)] = o[hk][blk * g:blk * (g + 1)]
                lse_ref[:, h:h + 1] = lse[blk * g:blk * (g + 1)]

    return _hosted_call(
        body, name=name, grid=(T // blk,), in_specs=_swa_specs(),
        out_specs=[pl.BlockSpec((blk, 512), lambda n: (n, 0)), pl.BlockSpec((blk, ATTN_Q_HEADS), lambda n: (n, 0))],
        out_shape=[jax.ShapeDtypeStruct((T, 512), BF16), jax.ShapeDtypeStruct((T, ATTN_Q_HEADS), F32)],
        scratch=[], args=(sinks, z, z, z, z, z), semantics=("parallel",), exchange=exchange)


def _swa_bwd(z, sinks, dcat, lse, name):
    T = z.shape[0]
    blk, hd = ATTN_BLOCK, ATTN_HEAD_DIM
    scale = hd ** -0.5
    group = ATTN_Q_HEADS // ATTN_KV_HEADS

    def body(sink_ref, q_ref, kp_ref, kc_ref, vp_ref, vc_ref, do_ref, lse_ref,
             dq_ref, dka_ref, dkb_ref, dva_ref, dvb_ref, dsink_ref):
        @pl.when(pl.program_id(0) == 0)
        def _():
            dsink_ref[...] = jnp.zeros_like(dsink_ref)

        allowed = _swa_mask(pl.program_id(0))
        lane = lax.broadcasted_iota(jnp.int32, (1, ATTN_Q_HEADS), 1)
        dsink = jnp.zeros((1, ATTN_Q_HEADS), F32)
        hks = range(ATTN_KV_HEADS)
        kss = [slice(hd * hk, hd * hk + hd) for hk in hks]
        k = [_bf(jnp.concatenate([kp_ref[:, ks], kc_ref[:, ks]], axis=0)) for ks in kss]
        v = [_bf(jnp.concatenate([vp_ref[:, ks], vc_ref[:, ks]], axis=0)) for ks in kss]
        qs = [_swa_rows(q_ref, hk, BF16) for hk in hks]
        dos = [_swa_rows(do_ref, hk, BF16) for hk in hks]
        lse = [jnp.concatenate([lse_ref[:, hk * group + g:hk * group + g + 1] for g in range(group)], axis=0)
               for hk in hks]
        s = [_dot(qs[hk], k[hk], NT) * scale for hk in hks]
        dp = [_dot(dos[hk], v[hk], NT) for hk in hks]
        p = [jnp.where(allowed, jnp.exp(jnp.where(allowed, s[hk], -1e30) - lse[hk]), 0.0) for hk in hks]
        delta = [jnp.sum(p[hk] * dp[hk], axis=-1, keepdims=True) for hk in hks]
        ds = [_bf(p[hk] * (dp[hk] - delta[hk]) * scale) for hk in hks]
        dq = [_dot(ds[hk], k[hk]).astype(dq_ref.dtype) for hk in hks]
        dk = [_dot(ds[hk], qs[hk], TN) for hk in hks]
        dv = [_dot(_bf(p[hk]), dos[hk], TN) for hk in hks]
        for hk in hks:
            sink = _swa_per_row([sink_ref[0, hk * group + g] for g in range(group)])
            sink_part = jnp.exp(sink - lse[hk]) * delta[hk]
            for g in range(group):
                h = hk * group + g
                dq_ref[:, hd * h:hd * (h + 1)] = dq[hk][blk * g:blk * (g + 1)]
                dsink = dsink + jnp.where(lane == h, -jnp.sum(sink_part[blk * g:blk * (g + 1)]), 0.0)
            dkb_ref[:, kss[hk]] = dk[hk][:blk]
            dka_ref[:, kss[hk]] = dk[hk][blk:]
            dvb_ref[:, kss[hk]] = dv[hk][:blk]
            dva_ref[:, kss[hk]] = dv[hk][blk:]
        dsink_ref[...] += dsink

    kv_out = pl.BlockSpec((blk, 128), lambda n: (n, 0))
    return pl.pallas_call(
        body, name=name, grid=(T // blk,),
        in_specs=_swa_specs() + [pl.BlockSpec((blk, 512), lambda n: (n, 0)),
                                 pl.BlockSpec((blk, ATTN_Q_HEADS), lambda n: (n, 0))],
        out_specs=[pl.BlockSpec((blk, 512), lambda n: (n, 0)), kv_out, kv_out, kv_out, kv_out,
                   pl.BlockSpec((1, ATTN_Q_HEADS), lambda n: (0, 0))],
        out_shape=[jax.ShapeDtypeStruct((T, 512), BF16)] + [jax.ShapeDtypeStruct((T, 128), F32)] * 4
        + [jax.ShapeDtypeStruct((1, ATTN_Q_HEADS), F32)],
        compiler_params=_params("arbitrary"),
    )(sinks, z, z, z, z, z, dcat, lse)


def _assemble_dz(dq_a, dka, dkb, dva, dvb, dqr, dfr, dir_, dgr, name):
    T = dq_a.shape[0]
    blk = ATTN_BLOCK
    rows = min(ASSEMBLE_ROWS, T)
    nb, per = T // rows, rows // blk

    def body(dq_ref, dka_ref, dkb_ref, dkn_ref, dva_ref, dvb_ref, dvn_ref, dqr_ref, dfr_ref, dir_ref, dgr_ref, o_ref):
        has_next = pl.program_id(0) < nb - 1

        def with_next(a_ref, b_ref, n_ref):
            after = jnp.where(has_next, n_ref[...], 0.0)
            shifted = after if per == 1 else jnp.concatenate([b_ref[blk:, :], after], axis=0)
            return (a_ref[...] + shifted).astype(o_ref.dtype)

        o_ref[:, 0:512] = dq_ref[...]
        o_ref[:, 512:640] = with_next(dka_ref, dkb_ref, dkn_ref)
        o_ref[:, 640:768] = with_next(dva_ref, dvb_ref, dvn_ref)
        o_ref[:, 768:1280] = dqr_ref[...]
        o_ref[:, 1280:1792] = dfr_ref[...]
        o_ref[:, 1792:2304] = dir_ref[...]
        o_ref[:, 2304:2816] = dgr_ref[...]

    cur = lambda w: pl.BlockSpec((rows, w), lambda n: (n, 0))
    nxt = pl.BlockSpec((blk, 128), lambda n: (jnp.minimum((n + 1) * per, T // blk - 1), 0))
    return pl.pallas_call(
        body, name=name, grid=(nb,),
        in_specs=[cur(512), cur(128), cur(128), nxt, cur(128), cur(128), nxt, cur(512), cur(512), cur(512), cur(512)],
        out_specs=pl.BlockSpec((rows, 2816), lambda n: (n, 0)),
        out_shape=jax.ShapeDtypeStruct((T, 2816), BF16), compiler_params=_params("parallel"),
    )(dq_a, dka, dkb, dkb, dva, dvb, dvb, dqr, dfr, dir_, dgr)


HGRN_ROWS = 512


def _hgrn_consts():
    c = HGRN_CHUNK
    r = lax.broadcasted_iota(jnp.int32, (c, c), 0)
    s = lax.broadcasted_iota(jnp.int32, (c, c), 1)
    rcol = lax.broadcasted_iota(jnp.int32, (c, 1), 0)
    same_block, upper = [], []
    for m in HGRN_LEVELS:
        same_block.append((r & ~(2 * m - 1)) == (s & ~(2 * m - 1)))
        upper.append((rcol & (2 * m - 1)) >= m)
    cum_mat = jnp.where(s <= r, 1.0, 0.0).astype(BF16)
    rev_mat = jnp.where(s >= r, 1.0, 0.0).astype(BF16)
    return cum_mat, rev_mat, r == s, same_block, upper, rcol & 3, s == r - 1


def _hgrn_level_decay(g, b, m, pos4):
    c = HGRN_CHUNK
    if m == 1:
        return jnp.exp(jnp.where((pos4 & 1) == 1, g, 0.0))
    if m == 2:
        after, before = pltpu.roll(g, c - 1, 0), pltpu.roll(g, 1, 0)
        return jnp.exp(jnp.where(pos4 == 0, after, jnp.where(pos4 == 1, 0.0, jnp.where(pos4 == 2, g, g + before))))
    b3 = b.reshape(c // (2 * m), 2 * m, HGRN_DIM)
    bref = jnp.broadcast_to(b3[:, m - 1:m, :], b3.shape).reshape(c, HGRN_DIM)
    return jnp.exp(-jnp.abs(b - bref))


def _split3(x):
    hi = _bf(x)
    r1 = x - hi.astype(F32)
    mid = _bf(r1)
    lo = _bf(r1 - mid.astype(F32))
    return jnp.concatenate([hi, mid, lo], axis=1)


def _dot_hilo(a, b):
    r, c = a.shape[0], b.shape[1]
    a_hi, b_hi = _bf(a), _bf(b)
    a2 = jnp.concatenate([a_hi, _bf(a - a_hi.astype(F32))], axis=0)
    b2 = jnp.concatenate([b_hi, _bf(b - b_hi.astype(F32))], axis=1)
    y = _dot(a2, b2)
    return y[:r, :c] + y[:r, c:] + y[r:, :c]


def _fold3(y):
    w = y.shape[1] // 3
    return y[:, :w] + y[:, w:2 * w] + y[:, 2 * w:]


def _hgrn_gates(qr, fr, lb):
    sq = _sigmoid(qr)
    q = qr * sq * (HGRN_DIM ** -0.5)
    sf = _sigmoid(fr)
    f = lb + (1.0 - lb) * sf
    k = (1.0 - lb) * _sigmoid(-fr)
    return q, sq, sf, f, k, jnp.log(f)


def _hgrn_intra(q, k, g, b, consts, scores=True):
    _, _, eye, same_block, upper, pos4, below = consts
    heads = range(len(q))
    a = None
    if scores:
        a = [jnp.where(eye, jnp.sum(q[hh] * k[hh], axis=1, keepdims=True), 0.0) for hh in heads]
    saved = [[] for _ in heads]
    for i, m in enumerate(HGRN_LEVELS):
        up = upper[i]
        e = [_hgrn_level_decay(g[hh], b[hh], m, pos4) for hh in heads]
        qt = [jnp.where(up, q[hh] * e[hh], 0.0) for hh in heads]
        kt = [jnp.where(up, 0.0, k[hh] * e[hh]) for hh in heads]
        for hh in heads:
            saved[hh].append((e[hh], qt[hh], kt[hh]))
        if not scores:
            continue
        if m == 1:
            for hh in heads:
                pair = jnp.sum(qt[hh] * pltpu.roll(kt[hh], 1, 0), axis=1, keepdims=True)
                a[hh] = a[hh] + jnp.where(below, pair, 0.0)
            continue
        p = [_dot(_bf(qt[hh]), _bf(kt[hh]), NT) for hh in heads]
        for hh in heads:
            a[hh] = a[hh] + jnp.where(same_block[i], p[hh], 0.0)
    return a, saved


def _hgrn_specs(tb, nb, rev):
    tmap = (lambda t: nb - 1 - t) if rev else (lambda t: t)
    assert HGRN_PAIR == HGRN_HEADS
    return [pl.BlockSpec((tb, 2816), lambda h, t: (tmap(t), 0)),
            pl.BlockSpec((1, HGRN_PAIR * HGRN_DIM), lambda h, t: (0, h)),
            pl.BlockSpec((1, HGRN_DIM), lambda h, t: (0, 0))]


def _hgrn_z(z_ref, sl, base, head):
    return z_ref[sl, base + HGRN_DIM * head:base + HGRN_DIM * (head + 1)].astype(F32)


def _hgrn_fwd(z, lb, onw, name, exchange=None):
    T = z.shape[0]
    tb = min(HGRN_ROWS, T)
    nb, c, nc = T // tb, HGRN_CHUNK, min(HGRN_ROWS, T) // HGRN_CHUNK

    def body(z_ref, lb_ref, onw_ref, rec_ref, o_ref, st_ref, a_ref, state):
        @pl.when(pl.program_id(1) == 0)
        def _():
            state[...] = jnp.zeros_like(state)

        consts = _hgrn_consts()
        lbv = lb_ref[...]
        onwv = onw_ref[...]

        def chunk(ci, carry):
            sl = pl.ds(pl.multiple_of(ci * c, c), c)
            heads = range(HGRN_PAIR)
            lss = [slice(HGRN_DIM * hh, HGRN_DIM * (hh + 1)) for hh in heads]
            gates = [_hgrn_gates(_hgrn_z(z_ref, sl, Z_Q, hh), _hgrn_z(z_ref, sl, Z_F, hh), lbv[:, lss[hh]])
                     for hh in heads]
            q, k, g = [t[0] for t in gates], [t[4] for t in gates], [t[5] for t in gates]
            v = [_bf(_hgrn_z(z_ref, sl, Z_I, hh)) for hh in heads]
            b = [_fold3(_dot(consts[0], _split3(g[hh]))) for hh in heads]
            a, _ = _hgrn_intra(q, k, g, b, consts)
            st = [state[hh] for hh in heads]
            for hh in heads:
                st_ref[hh, ci] = st[hh]
            bl = [b[hh][c - 1:c, :] for hh in heads]
            o_state = [_dot(_bf(q[hh] * jnp.exp(b[hh])), _bf(st[hh]), NT) for hh in heads]
            kv = [_dot(v[hh], _bf(k[hh] * jnp.exp(bl[hh] - b[hh])), TN) for hh in heads]
            a = [_bf(a[hh]) for hh in heads]
            o = [_dot(a[hh], v[hh]) + o_state[hh] for hh in heads]
            for hh in heads:
                a_ref[sl, c * hh:c * (hh + 1)] = a[hh]
                state[hh] = st[hh] * jnp.exp(bl[hh]) + kv[hh]
                o_ref[sl, lss[hh]] = o[hh]
                oh, _ = _rms(o[hh])
                gr = _hgrn_z(z_ref, sl, Z_G, hh)
                rec_ref[sl, lss[hh]] = (oh * onwv * (gr * _sigmoid(gr))).astype(rec_ref.dtype)
            return carry

        lax.fori_loop(0, nc, chunk, 0)

    in_specs = _hgrn_specs(tb, nb, False)
    out_blk = pl.BlockSpec((tb, HGRN_PAIR * HGRN_DIM), lambda h, t: (t, h))
    return _hosted_call(
        body, name=name, grid=(HGRN_HEADS // HGRN_PAIR, nb), in_specs=in_specs,
        out_specs=[out_blk, out_blk, pl.BlockSpec((HGRN_PAIR, nc, HGRN_DIM, HGRN_DIM), lambda h, t: (h, t, 0, 0)),
                   pl.BlockSpec((tb, HGRN_PAIR * c), lambda h, t: (t, h))],
        out_shape=[jax.ShapeDtypeStruct((T, 512), BF16), jax.ShapeDtypeStruct((T, 512), F32),
                   jax.ShapeDtypeStruct((HGRN_HEADS, T // c, HGRN_DIM, HGRN_DIM), F32),
                   jax.ShapeDtypeStruct((T, HGRN_HEADS * c), BF16)],
        scratch=[pltpu.VMEM((HGRN_PAIR, HGRN_DIM, HGRN_DIM), F32)], args=(z, lb, onw),
        semantics=("parallel", "arbitrary"), exchange=exchange)


def _hgrn_bwd(z, lb, onw, o, states, scores, dcat, name, exchange=None):
    T = z.shape[0]
    tb = min(HGRN_ROWS, T)
    nb, c, nc = T // tb, HGRN_CHUNK, min(HGRN_ROWS, T) // HGRN_CHUNK

    def body(z_ref, lb_ref, onw_ref, o_ref, st_ref, drec_ref, a_ref,
             dqr_ref, dfr_ref, dir_ref, dgr_ref, dlb_ref, donw_ref, dstate):
        @pl.when(pl.program_id(1) == 0)
        def _():
            dstate[...] = jnp.zeros_like(dstate)
            dlb_ref[...] = jnp.zeros_like(dlb_ref)

        @pl.when((pl.program_id(0) == 0) & (pl.program_id(1) == 0))
        def _():
            donw_ref[...] = jnp.zeros_like(donw_ref)

        consts = _hgrn_consts()
        rev_mat, eye, same_block, upper = consts[1:5]
        below = consts[6]
        lbv = lb_ref[...]
        onwv = onw_ref[...]
        last = lax.broadcasted_iota(jnp.int32, (c, 1), 0) == c - 1

        def chunk(i, carry):
            ci = nc - 1 - i
            sl = pl.ds(pl.multiple_of(ci * c, c), c)
            hs = range(HGRN_PAIR)
            lss = [slice(HGRN_DIM * hh, HGRN_DIM * (hh + 1)) for hh in hs]
            qr = [_hgrn_z(z_ref, sl, Z_Q, hh) for hh in hs]
            gates = [_hgrn_gates(qr[hh], _hgrn_z(z_ref, sl, Z_F, hh), lbv[:, lss[hh]]) for hh in hs]
            q, sq, sf, f, k, g = ([t[j] for t in gates] for j in range(6))
            v = [_bf(_hgrn_z(z_ref, sl, Z_I, hh)) for hh in hs]
            b = [_fold3(_dot(consts[0], _split3(g[hh]))) for hh in hs]
            _, saved = _hgrn_intra(q, k, g, b, consts, scores=False)
            a = [a_ref[sl, c * hh:c * (hh + 1)] for hh in hs]
            st = [st_ref[hh, ci] for hh in hs]
            dst = [dstate[hh] for hh in hs]

            gr = [_hgrn_z(z_ref, sl, Z_G, hh) for hh in hs]
            sg = [_sigmoid(gr[hh]) for hh in hs]
            norm = [_rms(o_ref[sl, ls]) for ls in lss]
            oh, r = [t[0] for t in norm], [t[1] for t in norm]
            drec = [drec_ref[sl, ls].astype(F32) for ls in lss]
            don = [drec[hh] * (gr[hh] * sg[hh]) for hh in hs]
            do = [_bf(_rms_bwd(don[hh] * onwv, oh[hh], r[hh])) for hh in hs]
            donw = jnp.sum(don[0] * oh[0], axis=0, keepdims=True)
            for hh in hs:
                dgr_ref[sl, lss[hh]] = (drec[hh] * oh[hh] * onwv
                                        * (sg[hh] * (1.0 + gr[hh] * (1.0 - sg[hh])))).astype(dgr_ref.dtype)
                if hh:
                    donw = donw + jnp.sum(don[hh] * oh[hh], axis=0, keepdims=True)
            donw_ref[...] += donw

            eb = [jnp.exp(b[hh]) for hh in hs]
            bl = [b[hh][c - 1:c, :] for hh in hs]
            ebl = [jnp.exp(bl[hh]) for hh in hs]
            ekb = [jnp.exp(bl[hh] - b[hh]) for hh in hs]
            qe = [q[hh] * eb[hh] for hh in hs]
            ke = [k[hh] * ekb[hh] for hh in hs]
            da = [_dot(do[hh], v[hh], NT) for hh in hs]
            dat = [_dot(v[hh], do[hh], NT) for hh in hs]
            dqe = [_dot(do[hh], _bf(st[hh])) for hh in hs]
            dke = [_dot(v[hh], _bf(dst[hh])) for hh in hs]
            dv_a = [_dot(a[hh], do[hh], TN) for hh in hs]
            dv_s = [_dot(_bf(ke[hh]), _bf(dst[hh]), NT) for hh in hs]
            dst_in = [_dot(do[hh], _bf(qe[hh]), TN) for hh in hs]
            dad = [jnp.sum(jnp.where(eye, da[hh], 0.0), axis=1, keepdims=True) for hh in hs]
            dq = [dqe[hh] * eb[hh] + dad[hh] * k[hh] for hh in hs]
            dk = [dke[hh] * ekb[hh] + dad[hh] * q[hh] for hh in hs]
            db_last = [jnp.sum(dke[hh] * ke[hh], axis=0, keepdims=True)
                       + jnp.sum(dst[hh] * st[hh], axis=0, keepdims=True) * ebl[hh] for hh in hs]
            for hh in hs:
                dstate[hh] = dst[hh] * ebl[hh] + dst_in[hh]
                dir_ref[sl, lss[hh]] = (dv_a[hh] + dv_s[hh]).astype(dir_ref.dtype)
            for lvl, m in enumerate(HGRN_LEVELS):
                if m == 1:
                    pair = [jnp.sum(jnp.where(below, da[hh], 0.0), axis=1, keepdims=True) for hh in hs]
                    xq = [pair[hh] * pltpu.roll(saved[hh][lvl][2], 1, 0) for hh in hs]
                    xk = [pltpu.roll(pair[hh] * saved[hh][lvl][1], c - 1, 0) for hh in hs]
                else:
                    xq = [_dot_hilo(jnp.where(same_block[lvl], da[hh], 0.0), saved[hh][lvl][2]) for hh in hs]
                    xk = [_dot_hilo(jnp.where(same_block[lvl], dat[hh], 0.0), saved[hh][lvl][1]) for hh in hs]
                for hh in hs:
                    e = saved[hh][lvl][0]
                    dq[hh] = dq[hh] + jnp.where(upper[lvl], xq[hh] * e, 0.0)
                    dk[hh] = dk[hh] + jnp.where(upper[lvl], 0.0, xk[hh] * e)
            db = [q[hh] * dq[hh] - k[hh] * dk[hh] + jnp.where(last, db_last[hh], 0.0) for hh in hs]
            dg = [_fold3(_dot(rev_mat, _split3(db[hh]))) for hh in hs]

            for hh in hs:
                ls = lss[hh]
                dqr_ref[sl, ls] = (dq[hh] * (HGRN_DIM ** -0.5)
                                   * (sq[hh] * (1.0 + qr[hh] * (1.0 - sq[hh])))).astype(dqr_ref.dtype)
                dfk = dg[hh] / f[hh] - dk[hh]
                dfr_ref[sl, ls] = ((1.0 - lbv[:, ls]) * sf[hh] * (1.0 - sf[hh]) * dfk).astype(dfr_ref.dtype)
                dlb_ref[:, ls] += jnp.sum((1.0 - sf[hh]) * dfk, axis=0, keepdims=True)
            return carry

        lax.fori_loop(0, nc, chunk, 0)

    in_specs = _hgrn_specs(tb, nb, True)
    rblk = pl.BlockSpec((tb, HGRN_PAIR * HGRN_DIM), lambda h, t: (nb - 1 - t, h))
    in_specs = in_specs + [
        rblk,
        pl.BlockSpec((HGRN_PAIR, nc, HGRN_DIM, HGRN_DIM), lambda h, t: (h, nb - 1 - t, 0, 0)),
        pl.BlockSpec((tb, HGRN_PAIR * HGRN_DIM), lambda h, t: (nb - 1 - t, 4 // HGRN_PAIR + h)),
        pl.BlockSpec((tb, HGRN_PAIR * c), lambda h, t: (nb - 1 - t, h)),
    ]
    return _hosted_call(
        body, name=name, grid=(HGRN_HEADS // HGRN_PAIR, nb), in_specs=in_specs,
        out_specs=[rblk, rblk, rblk, rblk, pl.BlockSpec((1, HGRN_PAIR * HGRN_DIM), lambda h, t: (0, h)),
                   pl.BlockSpec((1, HGRN_DIM), lambda h, t: (0, 0))],
        out_shape=[jax.ShapeDtypeStruct((T, 512), BF16)] * 4
        + [jax.ShapeDtypeStruct((1, 512), F32), jax.ShapeDtypeStruct((1, HGRN_DIM), F32)],
        scratch=[pltpu.VMEM((HGRN_PAIR, HGRN_DIM, HGRN_DIM), F32)], args=(z, lb, onw, o, states, dcat, scores),
        semantics=("arbitrary", "arbitrary"), exchange=exchange)


def _lower_bound(logits, name):
    def body(l_ref, lb_ref):
        l0, l1 = l_ref[0:1, :], l_ref[1:2, :]
        m = jnp.maximum(l0, l1)
        e0, e1 = jnp.exp(l0 - m), jnp.exp(l1 - m)
        lb_ref[...] = e0 / (e0 + e1)

    return pl.pallas_call(
        body, name=name, out_shape=jax.ShapeDtypeStruct((1, logits.shape[1]), F32),
    )(logits)


def _lower_bound_bwd(lb, dlb, name):
    def body(lb_ref, dlb_ref, dl_ref):
        p = lb_ref[...]
        d0 = dlb_ref[...] * p * (1.0 - p)
        dl_ref[0:1, :] = d0
        dl_ref[1:2, :] = -d0

    return pl.pallas_call(
        body, name=name, out_shape=jax.ShapeDtypeStruct((2, lb.shape[1]), F32),
    )(lb, dlb)


CA_ROWS = 512


def _ca_fwd(q, k, v, name):
    T, W = q.shape
    M = k.shape[0]
    tq = min(CA_ROWS, T)
    scale = CA_HEAD_DIM ** -0.5

    def body(q_ref, k_ref, v_ref, o_ref):
        hss = [slice(CA_HEAD_DIM * h, CA_HEAD_DIM * (h + 1)) for h in range(CA_HEADS)]
        s = [_dot(q_ref[:, hs], k_ref[:, hs], NT) * scale for hs in hss]
        p = [jnp.exp(sh - jnp.max(sh, axis=-1, keepdims=True)) for sh in s]
        p = [ph / jnp.sum(ph, axis=-1, keepdims=True) for ph in p]
        o = [_dot(_bf(ph), v_ref[:, hs]) for ph, hs in zip(p, hss)]
        for oh, hs in zip(o, hss):
            o_ref[:, hs] = oh.astype(o_ref.dtype)

    full = pl.BlockSpec((M, W), lambda i: (0, 0))
    return pl.pallas_call(
        body, name=name, grid=(T // tq,), in_specs=[_row_spec(tq, W), full, full], out_specs=_row_spec(tq, W),
        out_shape=jax.ShapeDtypeStruct((T, W), BF16), compiler_params=_params("parallel"),
    )(q, k, v)


def _ca_bwd(q, k, v, do, name):
    T, W = q.shape
    M = k.shape[0]
    tq = min(CA_ROWS, T)
    scale = CA_HEAD_DIM ** -0.5

    def body(q_ref, k_ref, v_ref, do_ref, dq_ref, dk_ref, dv_ref):
        @pl.when(pl.program_id(0) == 0)
        def _():
            dk_ref[...] = jnp.zeros_like(dk_ref)
            dv_ref[...] = jnp.zeros_like(dv_ref)

        heads = range(CA_HEADS)
        hss = [slice(CA_HEAD_DIM * h, CA_HEAD_DIM * (h + 1)) for h in heads]
        qh, kh = [q_ref[:, hs] for hs in hss], [k_ref[:, hs] for hs in hss]
        vh, doh = [v_ref[:, hs] for hs in hss], [do_ref[:, hs] for hs in hss]
        s = [_dot(qh[h], kh[h], NT) * scale for h in heads]
        dp = [_dot(doh[h], vh[h], NT) for h in heads]
        p = [jnp.exp(s[h] - jnp.max(s[h], axis=-1, keepdims=True)) for h in heads]
        p = [p[h] / jnp.sum(p[h], axis=-1, keepdims=True) for h in heads]
        ds = [_bf(p[h] * (dp[h] - jnp.sum(p[h] * dp[h], axis=-1, keepdims=True)) * scale) for h in heads]
        dq = [_dot(ds[h], kh[h]) for h in heads]
        dk = [_dot(ds[h], qh[h], TN) for h in heads]
        dv = [_dot(_bf(p[h]), doh[h], TN) for h in heads]
        for h in heads:
            dq_ref[:, hss[h]] = dq[h].astype(dq_ref.dtype)
            dk_ref[:, hss[h]] += dk[h]
            dv_ref[:, hss[h]] += dv[h]

    full = pl.BlockSpec((M, W), lambda i: (0, 0))
    return pl.pallas_call(
        body, name=name, grid=(T // tq,), in_specs=[_row_spec(tq, W), full, full, _row_spec(tq, W)],
        out_specs=[_row_spec(tq, W), full, full],
        out_shape=[jax.ShapeDtypeStruct((T, W), BF16), jax.ShapeDtypeStruct((M, W), F32), jax.ShapeDtypeStruct((M, W), F32)],
        compiler_params=_params("arbitrary"),
    )(q, k, v, do)


FFN_ROWS = 256
FFN_COLS = 1408
GELU_C0 = 0.7978845608028654
GELU_C1 = 0.044715


def _gelu(x):
    t = jnp.tanh(GELU_C0 * (x + GELU_C1 * x * x * x))
    return 0.5 * x * (1.0 + t), t


def _gelu_grad(x, t):
    return 0.5 * (1.0 + t) + 0.5 * x * (1.0 - t * t) * GELU_C0 * (1.0 + 3.0 * GELU_C1 * x * x)


def _shift_down(cur, halo, first, tb):
    row = lax.broadcasted_iota(jnp.int32, (tb, 1), 0)
    h6 = jnp.where(first, 0.0, halo[6:7])
    h7 = jnp.where(first, 0.0, halo[7:8])
    u1 = jnp.where(row == 0, h7, pltpu.roll(cur, 1, 0))
    u2 = jnp.where(row == 0, h6, jnp.where(row == 1, h7, pltpu.roll(cur, 2, 0)))
    return u1, u2


def _conv(u_ref, halo_ref, w_ref, b_ref, half, first, tb):
    cur = u_ref[half]
    u1, u2 = _shift_down(cur, halo_ref[half], first, tb)
    w = w_ref[...]
    return w[0:1] * u2 + w[1:2] * u1 + w[2:3] * cur + b_ref[...], cur, u1, u2


def _ffn_specs(tb, tc, rows_first):
    nj = D_FF // tc
    rc = (lambda a, b: (a, b)) if rows_first else (lambda a, b: (b, a))
    def at(f):
        return lambda a, b: f(*rc(a, b))
    blk = pl.BlockSpec((2, tb, tc), at(lambda t, j: (0, t, j)))
    halo = pl.BlockSpec((2, 8, tc), at(lambda t, j: (0, jnp.maximum(t * (tb // 8) - 1, 0), j)))
    wg = pl.BlockSpec((3, tc), at(lambda t, j: (0, j)))
    wv = pl.BlockSpec((3, tc), at(lambda t, j: (0, j + nj)))
    bg = pl.BlockSpec((1, tc), at(lambda t, j: (0, j)))
    bv = pl.BlockSpec((1, tc), at(lambda t, j: (0, j + nj)))
    flat = pl.BlockSpec((tb, tc), at(lambda t, j: (t, j)))
    return blk, halo, wg, wv, bg, bv, flat


def _glu_fwd(u, cw, cb, name):
    T = u.shape[1]
    tb, tc = min(FFN_ROWS, T), FFN_COLS

    def body(u_ref, halo_ref, wg_ref, wv_ref, bg_ref, bv_ref, a_ref):
        first = pl.program_id(0) == 0
        cg = _conv(u_ref, halo_ref, wg_ref, bg_ref, 0, first, tb)[0]
        cv = _conv(u_ref, halo_ref, wv_ref, bv_ref, 1, first, tb)[0]
        a_ref[...] = (_gelu(cg)[0] * cv).astype(a_ref.dtype)

    blk, halo, wg, wv, bg, bv, flat = _ffn_specs(tb, tc, True)
    return pl.pallas_call(
        body, name=name, grid=(T // tb, D_FF // tc), in_specs=[blk, halo, wg, wv, bg, bv], out_specs=flat,
        out_shape=jax.ShapeDtypeStruct((T, D_FF), BF16), compiler_params=_params("parallel", "parallel"),
    )(u, u, cw, cw, cb, cb)


def _glu_bwd(u, cw, cb, da, name, exchange=None):
    T = u.shape[1]
    tb, tc = min(FFN_ROWS, T), FFN_COLS

    def body(u_ref, halo_ref, wg_ref, wv_ref, bg_ref, bv_ref, da_ref, dc_ref, db_ref, dw_ref):
        first = pl.program_id(1) == 0

        @pl.when(first)
        def _():
            db_ref[...] = jnp.zeros_like(db_ref)
            dw_ref[...] = jnp.zeros_like(dw_ref)

        cg, ug, ug1, ug2 = _conv(u_ref, halo_ref, wg_ref, bg_ref, 0, first, tb)
        cv, uv, uv1, uv2 = _conv(u_ref, halo_ref, wv_ref, bv_ref, 1, first, tb)
        da = da_ref[...]
        gl, t = _gelu(cg)
        dcg = da * cv * _gelu_grad(cg, t)
        dcv = da * gl
        dc_ref[0] = dcg
        dc_ref[1] = dcv
        for half, dc, taps in ((0, dcg, (ug2, ug1, ug)), (1, dcv, (uv2, uv1, uv))):
            db_ref[half] += jnp.sum(dc, axis=0, keepdims=True)
            for tap in range(3):
                dw_ref[half, tap:tap + 1, :] += jnp.sum(dc * taps[tap], axis=0, keepdims=True)

    blk, halo, wg, wv, bg, bv, flat = _ffn_specs(tb, tc, False)
    return _hosted_call(
        body, name=name, grid=(D_FF // tc, T // tb), in_specs=[blk, halo, wg, wv, bg, bv, flat],
        out_specs=[blk, pl.BlockSpec((2, 1, tc), lambda j, t: (0, 0, j)), pl.BlockSpec((2, 3, tc), lambda j, t: (0, 0, j))],
        out_shape=[jax.ShapeDtypeStruct((2, T, D_FF), F32), jax.ShapeDtypeStruct((2, 1, D_FF), F32),
                   jax.ShapeDtypeStruct((2, 3, D_FF), F32)],
        scratch=[], args=(u, u, cw, cw, cb, cb, da), semantics=("parallel", "arbitrary"), exchange=exchange)


def _conv_bwd(dc, cw, name):
    T = dc.shape[1]
    tb, tc = min(FFN_ROWS, T), FFN_COLS
    nt, nj = T // tb, D_FF // tc

    def body(dc_ref, halo_ref, wg_ref, wv_ref, du_ref):
        last = pl.program_id(0) == nt - 1
        row = lax.broadcasted_iota(jnp.int32, (tb, 1), 0)
        for half, w_ref in ((0, wg_ref), (1, wv_ref)):
            cur = dc_ref[half]
            halo = halo_ref[half]
            h0 = jnp.where(last, 0.0, halo[0:1])
            h1 = jnp.where(last, 0.0, halo[1:2])
            d1 = jnp.where(row == tb - 1, h0, pltpu.roll(cur, tb - 1, 0))
            d2 = jnp.where(row == tb - 1, h1, jnp.where(row == tb - 2, h0, pltpu.roll(cur, tb - 2, 0)))
            w = w_ref[...]
            du_ref[half] = (w[2:3] * cur + w[1:2] * d1 + w[0:1] * d2).astype(du_ref.dtype)

    blk = pl.BlockSpec((2, tb, tc), lambda t, j: (0, t, j))
    halo = pl.BlockSpec((2, 8, tc), lambda t, j: (0, jnp.minimum((t + 1) * (tb // 8), T // 8 - 1), j))
    wg = pl.BlockSpec((3, tc), lambda t, j: (0, j))
    wv = pl.BlockSpec((3, tc), lambda t, j: (0, j + nj))
    return pl.pallas_call(
        body, name=name, grid=(nt, nj), in_specs=[blk, halo, wg, wv], out_specs=blk,
        out_shape=jax.ShapeDtypeStruct((2, T, D_FF), BF16), compiler_params=_params("parallel", "parallel"),
    )(dc, dc, cw, cw)


def _mesh_pos():
    return lax.axis_index("x"), lax.axis_index("y"), lax.axis_index("c")


def _peer(pos, k):
    return (pos[0] ^ ((k >> 2) & 1), pos[1] ^ ((k >> 1) & 1), pos[2] ^ (k & 1))


def _index(pos):
    return 4 * pos[0] + 2 * pos[1] + pos[2]


class _Exchange:
    def __init__(self, kind, buf, relay=False):
        assert kind in ("gather", "scatter") and not (relay and kind == "scatter")
        self.kind, self.buf, self.relay = kind, buf, relay
        self.out_shape = jax.ShapeDtypeStruct(((N_DEV,) + buf.shape) if kind == "gather" else buf.shape, buf.dtype)
        self.spec = pl.BlockSpec(memory_space=pl.ANY)
        self.scratch = [pltpu.SemaphoreType.DMA((N_DEV - 1,)), pltpu.SemaphoreType.DMA((N_DEV - 1,)),
                        pltpu.SemaphoreType.DMA]

    def _src(self, x_ref, dest):
        return x_ref if self.kind == "gather" else x_ref.at[dest]

    def _copies(self, x_ref, out_ref, send_sems, recv_sems, local_sem):
        pos = _mesh_pos()
        me = _index(pos)
        local = pltpu.make_async_copy(self._src(x_ref, me), out_ref.at[me], local_sem)
        sends, recvs = [], []
        for k in range(1, N_DEV):
            peer = _peer(pos, k)
            sends.append(pltpu.make_async_remote_copy(
                src_ref=self._src(x_ref, _index(peer)), dst_ref=out_ref.at[me], send_sem=send_sems.at[k - 1],
                recv_sem=recv_sems.at[k - 1], device_id=peer, device_id_type=pl.DeviceIdType.MESH))
            recvs.append(pltpu.make_async_remote_copy(
                src_ref=self._src(x_ref, me), dst_ref=out_ref.at[_index(peer)], send_sem=send_sems.at[k - 1],
                recv_sem=recv_sems.at[k - 1], device_id=peer, device_id_type=pl.DeviceIdType.MESH))
        return local, sends, recvs

    def _relay_copies(self, x_ref, out_ref, send_sems, recv_sems, local_sem):
        x, y, c = _mesh_pos()
        me, sibling = (x, y, c), (x, y, 1 - c)
        chips = [(1 - x, y), (x, 1 - y), (1 - x, 1 - y)]

        def copy(k, block, to, own=False):
            return pltpu.make_async_remote_copy(
                src_ref=x_ref if own else out_ref.at[_index(block)], dst_ref=out_ref.at[_index(block)],
                send_sem=send_sems.at[k], recv_sem=recv_sems.at[k], device_id=to, device_id_type=pl.DeviceIdType.MESH)

        local = pltpu.make_async_copy(x_ref, out_ref.at[_index(me)], local_sem)
        first = [copy(0, me, sibling, own=True)] + [copy(1 + j, me, (*chip, c), own=True) for j, chip in enumerate(chips)]
        landed = [copy(1 + j, (*chip, c), me) for j, chip in enumerate(chips)]
        passed = [copy(4 + j, (*chip, c), sibling) for j, chip in enumerate(chips)]
        from_sibling = [copy(0, sibling, me)] + [copy(4 + j, (*chip, 1 - c), me) for j, chip in enumerate(chips)]
        return local, first, landed, passed, from_sibling

    def start(self, *refs):
        if self.relay:
            local, first = self._relay_copies(*refs)[:2]
            local.start()
            for cp in first:
                cp.start()
            return
        local, sends, _ = self._copies(*refs)
        local.start()
        for cp in sends:
            cp.start()

    def finish(self, *refs):
        if self.relay:
            local, first, landed, passed, from_sibling = self._relay_copies(*refs)
            for got, forward in zip(landed, passed):
                got.wait_recv()
                forward.start()
            for cp in from_sibling:
                cp.wait_recv()
            for cp in first + passed:
                cp.wait_send()
            local.wait()
            return
        local, sends, recvs = self._copies(*refs)
        for cp in recvs:
            cp.wait_recv()
        for cp in sends:
            cp.wait_send()
        local.wait()


def _hosted_call(body, *, name, grid, in_specs, out_specs, out_shape, scratch, args, semantics, exchange=None):
    if exchange is None:
        return pl.pallas_call(
            body, name=name, grid=grid, in_specs=in_specs, out_specs=out_specs, out_shape=out_shape,
            scratch_shapes=scratch, compiler_params=_params(*semantics))(*args)
    n_in, n_out, n_scr = len(in_specs), len(out_specs), len(scratch)

    def hosted(*refs):
        ins, x_ref = refs[:n_in], refs[n_in]
        outs, land_ref = refs[n_in + 1:n_in + 1 + n_out], refs[n_in + 1 + n_out]
        rest = refs[n_in + n_out + 2:]
        sems = rest[n_scr:]
        ids = [pl.program_id(a) for a in range(len(grid))]
        first, last = ids[0] == 0, ids[0] == grid[0] - 1
        for a in range(1, len(grid)):
            first, last = first & (ids[a] == 0), last & (ids[a] == grid[a] - 1)

        @pl.when(first)
        def _():
            exchange.start(x_ref, land_ref, *sems)

        body(*ins, *outs, *rest[:n_scr])

        @pl.when(last)
        def _():
            exchange.finish(x_ref, land_ref, *sems)

    return pl.pallas_call(
        hosted, name=name, grid=grid, in_specs=list(in_specs) + [exchange.spec],
        out_specs=list(out_specs) + [exchange.spec], out_shape=list(out_shape) + [exchange.out_shape],
        scratch_shapes=list(scratch) + exchange.scratch, compiler_params=_params(*(["arbitrary"] * len(grid))),
    )(*args, exchange.buf)


def _exchange_alone(exchange, name):
    def body(x_ref, out_ref, send_sems, recv_sems, local_sem):
        exchange.start(x_ref, out_ref, send_sems, recv_sems, local_sem)
        exchange.finish(x_ref, out_ref, send_sems, recv_sems, local_sem)

    return pl.pallas_call(
        body, name=name, out_shape=exchange.out_shape, in_specs=[exchange.spec], out_specs=exchange.spec,
        scratch_shapes=exchange.scratch)(exchange.buf)


def _adamw(w, g, m, v):
    m = ADAM_B1 * m + (1.0 - ADAM_B1) * g
    v = ADAM_B2 * v + (1.0 - ADAM_B2) * (g * g)
    m_hat = m / (1.0 - ADAM_B1 ** ADAM_STEP)
    v_hat = v / (1.0 - ADAM_B2 ** ADAM_STEP)
    delta = -ADAM_LR * (m_hat / (jnp.sqrt(v_hat) + ADAM_EPS) + ADAM_WD * w)
    return delta, m, v


def _sum_rows(parts, r0, rows, name, wmv=None):
    C = parts.shape[2]
    tr = max(t for t in range(16, ROWS + 1, 16) if rows % t == 0 and r0 % t == 0)

    def total(p_ref):
        g = p_ref[0].astype(F32)
        for i in range(1, N_DEV):
            g = g + p_ref[i].astype(F32)
        return g

    p_spec = pl.BlockSpec((N_DEV, tr, C), lambda i: (0, r0 // tr + i, 0))
    if wmv is None:
        def body(p_ref, g_ref):
            g_ref[...] = total(p_ref)

        return pl.pallas_call(
            body, name=name, grid=(rows // tr,), in_specs=[p_spec], out_specs=_row_spec(tr, C),
            out_shape=jax.ShapeDtypeStruct((rows, C), F32), compiler_params=_params("parallel"))(parts)

    def body(p_ref, w_ref, m_ref, v_ref, g_ref, d_ref, mo_ref, vo_ref):
        g = total(p_ref)
        g_ref[0] = g
        d_ref[0], mo_ref[0], vo_ref[0] = _adamw(w_ref[0], g, m_ref[0], v_ref[0])

    blk = pl.BlockSpec((1, tr, C), lambda i: (0, i, 0))
    return pl.pallas_call(
        body, name=name, grid=(rows // tr,), in_specs=[p_spec, blk, blk, blk], out_specs=[blk] * 4,
        out_shape=[jax.ShapeDtypeStruct((1, rows, C), F32)] * 4, compiler_params=_params("parallel"))(parts, *wmv)


def _sum_parts(parts, name):
    _, R, C = parts.shape

    def body(p_ref, g_ref):
        g = p_ref[0]
        for i in range(1, N_DEV):
            g = g + p_ref[i]
        g_ref[...] = g

    return pl.pallas_call(body, name=name, out_shape=jax.ShapeDtypeStruct((R, C), F32))(parts)


def _adamw_call(w, g, m, v, name):
    _, R, C = w.shape
    tr = min(ROWS, R)

    def body(w_ref, g_ref, m_ref, v_ref, d_ref, mo_ref, vo_ref):
        d_ref[...], mo_ref[...], vo_ref[...] = _adamw(w_ref[...], g_ref[...], m_ref[...], v_ref[...])

    blk = pl.BlockSpec((1, tr, C), lambda i: (0, i, 0))
    return pl.pallas_call(
        body, name=name, grid=(R // tr,), in_specs=[blk] * 4, out_specs=[blk] * 3,
        out_shape=[jax.ShapeDtypeStruct(w.shape, F32)] * 3, compiler_params=_params("parallel"))(w, g, m, v)


NORMS = ("mix_pre_norm", "mix_post_norm", "ca_pre_norm", "mem_norm", "ca_post_norm", "ffn_pre_norm", "ffn_post_norm")
SMALL = ("mix_pre_norm", "attn_sinks", "hgrn_lb_logits", "hgrn_out_norm", "mix_post_norm", "ca_pre_norm", "mem_norm",
         "ca_post_norm", "ffn_pre_norm", "ffn_conv_w", "ffn_conv_b", "ffn_post_norm")
SMALL_ROWS = 40
ROW_LOGITS, ROW_MISC, ROW_CONV_B, ROW_CONV_W = 7, 8, 9, 15
LANE_SINKS, LANE_LOSS = 128, 256
FF_PIECES = ((0, 1024), (1024, 2048), (2048, D_FF))


def _pack_small(norm_grads, dlogits, donw, dsinks, loss, d_cb, d_cw, name):
    def body(*refs):
        norm_refs = refs[:len(NORMS)]
        dl_ref, donw_ref, dsink_ref, loss_ref, cb_ref, cw_ref, out_ref = refs[len(NORMS):]
        out_ref[...] = jnp.zeros_like(out_ref)
        for i, ref in enumerate(norm_refs):
            out_ref[i:i + 1, :] = ref[...]
        out_ref[ROW_LOGITS:ROW_LOGITS + 1, 0:512] = dl_ref[0:1, :]
        out_ref[ROW_LOGITS:ROW_LOGITS + 1, 512:1024] = dl_ref[1:2, :]
        out_ref[ROW_MISC:ROW_MISC + 1, 0:HGRN_DIM] = donw_ref[...]
        out_ref[ROW_MISC:ROW_MISC + 1, LANE_SINKS:LANE_SINKS + ATTN_Q_HEADS] = dsink_ref[...]
        out_ref[ROW_MISC:ROW_MISC + 1, LANE_LOSS:LANE_LOSS + LANE] = loss_ref[...]
        for h in range(2):
            for j, (c0, c1) in enumerate(FF_PIECES):
                r = ROW_CONV_B + 3 * h + j
                out_ref[r:r + 1, 0:c1 - c0] = cb_ref[h, :, c0:c1]
                for t in range(3):
                    r = ROW_CONV_W + 3 * (3 * h + t) + j
                    out_ref[r:r + 1, 0:c1 - c0] = cw_ref[h, t:t + 1, c0:c1]

    return pl.pallas_call(
        body, name=name, out_shape=jax.ShapeDtypeStruct((SMALL_ROWS, 1024), F32),
    )(*norm_grads, dlogits, donw, dsinks, loss, d_cb, d_cw)


def _adamw_small(total, g_conv_w, w, m, v, name):
    n = len(SMALL)

    def body(*refs):
        t_ref, gcw_ref = refs[:2]
        w_refs, m_refs, v_refs = (dict(zip(SMALL, refs[2 + n * i:2 + n * (i + 1)])) for i in range(3))
        outs = refs[2 + 3 * n:]
        loss_ref = outs[0]
        g_refs, d_refs, mo_refs, vo_refs = (dict(zip(SMALL, outs[1 + n * i:1 + n * (i + 1)])) for i in range(4))
        loss_ref[...] = t_ref[ROW_MISC:ROW_MISC + 1, LANE_LOSS:LANE_LOSS + 1]

        def step(nm, idx, g):
            g_refs[nm][idx] = g
            d_refs[nm][idx], mo_refs[nm][idx], vo_refs[nm][idx] = _adamw(w_refs[nm][idx], g, m_refs[nm][idx], v_refs[nm][idx])

        everything = (slice(None), slice(None))
        for i, nm in enumerate(NORMS):
            step(nm, everything, t_ref[i:i + 1, :])
        step("hgrn_lb_logits", (slice(0, 1), slice(None)), t_ref[ROW_LOGITS:ROW_LOGITS + 1, 0:512])
        step("hgrn_lb_logits", (slice(1, 2), slice(None)), t_ref[ROW_LOGITS:ROW_LOGITS + 1, 512:1024])
        step("hgrn_out_norm", everything, t_ref[ROW_MISC:ROW_MISC + 1, 0:HGRN_DIM])
        step("attn_sinks", everything, t_ref[ROW_MISC:ROW_MISC + 1, LANE_SINKS:LANE_SINKS + ATTN_Q_HEADS])
        for h in range(2):
            for j, (c0, c1) in enumerate(FF_PIECES):
                r = ROW_CONV_B + 3 * h + j
                step("ffn_conv_b", (slice(None), slice(D_FF * h + c0, D_FF * h + c1)), t_ref[r:r + 1, 0:c1 - c0])
        step("ffn_conv_w", (slice(None), slice(None), slice(None)), gcw_ref[...])

    shapes = [jax.ShapeDtypeStruct(w[nm].shape, F32) for nm in SMALL]
    out = pl.pallas_call(
        body, name=name, out_shape=[jax.ShapeDtypeStruct((1, 1), F32)] + shapes * 4,
    )(total, g_conv_w, *[w[nm] for nm in SMALL], *[m[nm] for nm in SMALL], *[v[nm] for nm in SMALL])
    trees = [dict(zip(SMALL, out[1 + n * i:1 + n * (i + 1)])) for i in range(4)]
    return out[0], trees


BIG = ("w_in", "w_out", "ca_wq", "ca_wk", "ca_wv", "ca_wo", "ffn_w_up", "ffn_w_down")
BIG_FULL = {"w_in": (1024, 2816), "w_out": (1024, 1024), "ca_wq": (1024, 1024), "ca_wk": (1024, 1024),
            "ca_wv": (1024, 1024), "ca_wo": (1024, 1024), "ffn_w_up": (1024, 5632), "ffn_w_down": (2816, 1024)}
G_IN, G_MID, G_UP, G_DOWN = ("w_in",), ("w_out", "ca_wq", "ca_wk", "ca_wv", "ca_wo"), ("ffn_w_up",), ("ffn_w_down",)
GROUPS = (G_IN, G_MID, G_UP, G_DOWN)
COL_SHARDED = ("w_in", "ffn_w_up")
PACK_COLS = 1024


def _big_rows(name):
    r, c = BIG_FULL[name]
    return r * c // N_DEV // PACK_COLS


def _pack_shards(w, names):
    rows = [w[n][0].T if n in COL_SHARDED else w[n][0] for n in names]
    return (rows[0] if len(rows) == 1 else jnp.concatenate(rows, axis=0)).astype(BF16)


def _unpack_gathered(gathered, names):
    out, r0 = {}, 0
    for n in names:
        rows = _big_rows(n)
        out[n] = gathered[:, r0:r0 + rows].reshape(N_DEV * rows, PACK_COLS)
        r0 += rows
    return out


def _pack_full_grads(grads, names):
    parts = [grads[n].reshape(N_DEV, _big_rows(n), PACK_COLS) for n in names]
    return parts[0] if len(parts) == 1 else jnp.concatenate(parts, axis=1)


def kernel(x, mem, mix_pre_norm, w_in, attn_sinks, hgrn_lb_logits, hgrn_out_norm, w_out, mix_post_norm, ca_pre_norm, mem_norm, ca_wq, ca_wk, ca_wv, ca_wo, ca_post_norm, ffn_pre_norm, ffn_w_up, ffn_conv_w, ffn_conv_b, ffn_w_down, ffn_post_norm, loss_target, m_mix_pre_norm, m_w_in, m_attn_sinks, m_hgrn_lb_logits, m_hgrn_out_norm, m_w_out, m_mix_post_norm, m_ca_pre_norm, m_mem_norm, m_ca_wq, m_ca_wk, m_ca_wv, m_ca_wo, m_ca_post_norm, m_ffn_pre_norm, m_ffn_w_up, m_ffn_conv_w, m_ffn_conv_b, m_ffn_w_down, m_ffn_post_norm, v_mix_pre_norm, v_w_in, v_attn_sinks, v_hgrn_lb_logits, v_hgrn_out_norm, v_w_out, v_mix_post_norm, v_ca_pre_norm, v_mem_norm, v_ca_wq, v_ca_wk, v_ca_wv, v_ca_wo, v_ca_post_norm, v_ffn_pre_norm, v_ffn_w_up, v_ffn_conv_w, v_ffn_conv_b, v_ffn_w_down, v_ffn_post_norm):
    names = ["mix_pre_norm", "w_in", "attn_sinks", "hgrn_lb_logits", "hgrn_out_norm", "w_out", "mix_post_norm",
             "ca_pre_norm", "mem_norm", "ca_wq", "ca_wk", "ca_wv", "ca_wo", "ca_post_norm", "ffn_pre_norm",
             "ffn_w_up", "ffn_conv_w", "ffn_conv_b", "ffn_w_down", "ffn_post_norm"]
    w_all = dict(zip(names, [mix_pre_norm, w_in, attn_sinks, hgrn_lb_logits, hgrn_out_norm, w_out, mix_post_norm,
                             ca_pre_norm, mem_norm, ca_wq, ca_wk, ca_wv, ca_wo, ca_post_norm, ffn_pre_norm,
                             ffn_w_up, ffn_conv_w, ffn_conv_b, ffn_w_down, ffn_post_norm]))
    m_all = dict(zip(names, [m_mix_pre_norm, m_w_in, m_attn_sinks, m_hgrn_lb_logits, m_hgrn_out_norm, m_w_out,
                             m_mix_post_norm, m_ca_pre_norm, m_mem_norm, m_ca_wq, m_ca_wk, m_ca_wv, m_ca_wo,
                             m_ca_post_norm, m_ffn_pre_norm, m_ffn_w_up, m_ffn_conv_w, m_ffn_conv_b, m_ffn_w_down,
                             m_ffn_post_norm]))
    v_all = dict(zip(names, [v_mix_pre_norm, v_w_in, v_attn_sinks, v_hgrn_lb_logits, v_hgrn_out_norm, v_w_out,
                             v_mix_post_norm, v_ca_pre_norm, v_mem_norm, v_ca_wq, v_ca_wk, v_ca_wv, v_ca_wo,
                             v_ca_post_norm, v_ffn_pre_norm, v_ffn_w_up, v_ffn_conv_w, v_ffn_conv_b, v_ffn_w_down,
                             v_ffn_post_norm]))
    dev = _index(_mesh_pos())

    w_packs = {grp: _pack_shards(w_all, grp) for grp in GROUPS}
    shard_w = D_FF * 2 // N_DEV
    conv_w_rows = _exchange_alone(_Exchange("gather", ffn_conv_w[0]), "gather_conv_w")
    conv_w_full = conv_w_rows.transpose(1, 0, 2).reshape(3, 2 * D_FF)

    received, small_pack, grad_x = _local_step(
        x[0], mem[0], loss_target[0], w_packs, conv_w_full,
        {n: w_all[n] for n in NORMS}, attn_sinks, hgrn_lb_logits, hgrn_out_norm, ffn_conv_b)

    total = _sum_parts(_exchange_alone(_Exchange("gather", small_pack), "gather_small"), "sum_small")
    cw = total[ROW_CONV_W:ROW_CONV_W + 18].reshape(2, 3, 3 * PACK_COLS)[:, :, :D_FF]
    cw = cw.transpose(1, 0, 2).reshape(3, 2 * D_FF)
    g_conv_w = lax.dynamic_slice_in_dim(cw, dev * shard_w, shard_w, axis=1)[None]
    loss, (out_g, out_d, out_m, out_v) = _adamw_small(total, g_conv_w, w_all, m_all, v_all, "adamw_small")

    for grp in GROUPS:
        r0 = 0
        for n in grp:
            rows = _big_rows(n)
            if n in COL_SHARDED:
                g = _sum_rows(received[grp], r0, rows, "sum_" + n).T[None]
                d, mo, vo = _adamw_call(w_all[n], g, m_all[n], v_all[n], "adamw_" + n)
            else:
                g, d, mo, vo = _sum_rows(received[grp], r0, rows, "adamw_" + n, wmv=(w_all[n], m_all[n], v_all[n]))
            out_g[n], out_d[n], out_m[n], out_v[n] = g, d, mo, vo
            r0 += rows

    return (loss[0, 0], grad_x[None], *[out_g[n] for n in names], *[out_d[n] for n in names],
            *[out_m[n] for n in names], *[out_v[n] for n in names])


def _local_step(x, mem, target, w_packs, conv_w, norms, sinks, lb_logits, out_norm, conv_b):
    g1, g2, g3 = norms["mix_pre_norm"], norms["mix_post_norm"], norms["ca_pre_norm"]
    g4, g5, g6, g7 = norms["mem_norm"], norms["ca_post_norm"], norms["ffn_pre_norm"], norms["ffn_post_norm"]

    h1, gathered = _norm_fwd(x, g1, "mix_norm", exchange=_Exchange("gather", w_packs[G_IN], relay=True))
    w_in_t = _unpack_gathered(gathered, G_IN)["w_in"]
    up_shard = w_packs[G_UP]
    up_rows = up_shard.shape[0]
    up_cuts = (0, up_rows // 2, 3 * up_rows // 4, up_rows)
    up_parts = [up_shard[a:b] for a, b in zip(up_cuts[:-1], up_cuts[1:])]
    z, up_0 = _mm(h1, w_in_t, mode="nt", out_dtype=BF16, name="in_proj", tn=2816,
                  exchange=_Exchange("gather", up_parts[0]))
    attn, lse, gathered = _swa_fwd(z, sinks, "swa_fwd", exchange=_Exchange("gather", w_packs[G_DOWN]))
    w_down = _unpack_gathered(gathered, G_DOWN)["ffn_w_down"]
    lb = _lower_bound(lb_logits, "lower_bound")
    rec, o_rec, states, scores, gathered = _hgrn_fwd(
        z, lb, out_norm, "hgrn_fwd", exchange=_Exchange("gather", w_packs[G_MID]))
    w_out, wq, wk, wv, wo = (_unpack_gathered(gathered, G_MID)[n] for n in G_MID)
    cat = jnp.concatenate([attn, rec], axis=1)
    x1, h2, mix, up_1 = _mm(cat, w_out, mode="nn", out_dtype=BF16, name="out_proj",
                            exchange=_Exchange("gather", up_parts[1]), epilogue=_post_pre(x, g2, g3))
    mem_n = _norm_fwd(mem, g4, "mem_norm")
    q = _mm(h2, wq, mode="nn", out_dtype=BF16, name="ca_q")
    k = _mm(mem_n, wk, mode="nn", out_dtype=BF16, name="ca_k")
    v = _mm(mem_n, wv, mode="nn", out_dtype=BF16, name="ca_v")
    oc = _ca_fwd(q, k, v, "ca_fwd")
    x2, h3, c, up_2 = _mm(oc, wo, mode="nn", out_dtype=BF16, name="ca_o",
                          exchange=_Exchange("gather", up_parts[2]), epilogue=_post_pre(x1, g5, g6))
    w_up_t = jnp.concatenate([up_0, up_1, up_2], axis=1).reshape(-1, PACK_COLS)
    u = _mm(h3, w_up_t, mode="nt", out_dtype=F32, name="ffn_up", tn=2816, split_out=True)
    a = _glu_fwd(u, conv_w, conv_b, "glu_fwd")
    dx3, dy, loss_row, dg7 = _mm(a, w_down, mode="nn", out_dtype=BF16, name="ffn_down", tm=512, tk=2816,
                                 epilogue=_final(x2, target, g7))
    loss = loss_row[:, :LANE]

    da = _mm(dy, w_down, mode="nt", out_dtype=F32, name="ffn_down_dx", tn=2816)
    d_w_down = _mm(a, dy, mode="tn", out_dtype=BF16, name="ffn_down_dw", tm=2816, tk=1024)
    dc, d_cb, d_cw, got_down = _glu_bwd(
        u, conv_w, conv_b, da, "glu_bwd",
        exchange=_Exchange("scatter", _pack_full_grads({"ffn_w_down": d_w_down}, G_DOWN)))
    du = _conv_bwd(dc, conv_w, "conv_bwd")
    d_w_up_t = _mm(du, h3, mode="tn", out_dtype=BF16, name="ffn_up_dw", tm=2816, tk=1024, split_a=True)
    dx2, dcv, dg6, dg5, got_up = _mm(
        du, w_up_t, mode="nn", out_dtype=BF16, name="ffn_up_dx", tm=1024, tk=1408, split_a=True,
        exchange=_Exchange("scatter", _pack_full_grads({"ffn_w_up": d_w_up_t}, G_UP)),
        epilogue=_norm_bwd2(dx3, x2, c, g6, g5))
    doc = _mm(dcv, wo, mode="nt", out_dtype=BF16, name="ca_o_dx")
    d_wo = _mm(oc, dcv, mode="tn", out_dtype=BF16, name="ca_o_dw", tm=1024, tk=1024)
    dq, dk, dv = _ca_bwd(q, k, v, doc, "ca_bwd")
    d_wq = _mm(h2, dq, mode="tn", out_dtype=BF16, name="ca_q_dw", tm=1024, tk=1024)
    dx1, dmix, dg3, dg2 = _mm(dq, wq, mode="nt", out_dtype=BF16, name="ca_q_dx",
                              epilogue=_norm_bwd2(dx2, x1, mix, g3, g2))
    d_wk = _mm(mem_n, dk, mode="tn", out_dtype=BF16, name="ca_k_dw", tm=1024)
    d_wv = _mm(mem_n, dv, mode="tn", out_dtype=BF16, name="ca_v_dw", tm=1024)
    dmem_k = _mm(dk, wk, mode="nt", out_dtype=F32, name="ca_k_dx")
    dmem_v = _mm(dv, wv, mode="nt", out_dtype=F32, name="ca_v_dx")
    dg4 = _gain_bwd(mem, dmem_k, dmem_v, "mem_norm_bwd")
    dcat = _mm(dmix, w_out, mode="nt", out_dtype=BF16, name="out_proj_dx")
    d_w_out = _mm(cat, dmix, mode="tn", out_dtype=BF16, name="out_proj_dw", tm=1024, tk=1024)
    mid = {"w_out": d_w_out, "ca_wq": d_wq, "ca_wk": d_wk, "ca_wv": d_wv, "ca_wo": d_wo}
    dqr, dfr, dir_, dgr, dlb, donw, got_mid = _hgrn_bwd(
        z, lb, out_norm, o_rec, states, scores, dcat, "hgrn_bwd",
        exchange=_Exchange("scatter", _pack_full_grads(mid, G_MID)))
    dq_a, dka, dkb, dva, dvb, dsinks = _swa_bwd(z, sinks, dcat, lse, "swa_bwd")
    dz = _assemble_dz(dq_a, dka, dkb, dva, dvb, dqr, dfr, dir_, dgr, "assemble_dz")
    d_w_in_t = _mm(dz, h1, mode="tn", out_dtype=BF16, name="in_proj_dw", tm=2816, tk=1024)
    dx, dg1, got_in = _mm(dz, w_in_t, mode="nn", out_dtype=BF16, name="in_proj_dx", tm=512, tk=2816,
                          exchange=_Exchange("scatter", _pack_full_grads({"w_in": d_w_in_t}, G_IN)),
                          epilogue=_norm_bwd1(dx1, x, g1))

    small_pack = _pack_small(
        (dg1, dg2, dg3, dg4, dg5, dg6, dg7), _lower_bound_bwd(lb, dlb, "lower_bound_bwd"), donw, dsinks, loss,
        d_cb, d_cw, "pack_small")
    return {G_IN: got_in, G_MID: got_mid, G_UP: got_up, G_DOWN: got_down}, small_pack, dx
```

```python
import jax
import jax.numpy as jnp
from jax import lax
from jax.experimental import pallas as pl
from jax.experimental.pallas import tpu as pltpu

F32 = jnp.float32
BF16 = jnp.bfloat16
EPS = 1e-6
N_DEV = 8
MESH_AXES = ("x", "y", "c")

ATTN_HEAD_DIM = 64
ATTN_Q_HEADS = 8
ATTN_KV_HEADS = 2
ATTN_BLOCK = 128
HGRN_HEADS = 4
HGRN_DIM = 128
HGRN_CHUNK = 64
HGRN_PAIR = 4
Z_Q, Z_F, Z_I, Z_G = 768, 1280, 1792, 2304
HGRN_LEVELS = (32, 16, 8, 4, 2, 1)
CA_HEADS = 4
CA_HEAD_DIM = 256
D_FF = 2816

ADAM_LR = 0.001
ADAM_B1 = 0.9
ADAM_B2 = 0.999
ADAM_EPS = 1e-08
ADAM_WD = 0.01
ADAM_STEP = 10

VMEM_LIMIT = 58 << 20
EPILOGUE_ROWS = 256
LANE = 128

NT = (((1,), (1,)), ((), ()))
TN = (((0,), (0,)), ((), ()))


def _params(*sem):
    return pltpu.CompilerParams(dimension_semantics=sem, vmem_limit_bytes=VMEM_LIMIT)


def _tile(n, cap):
    if n <= cap:
        return n
    best = 0
    for t in range(LANE, cap + 1, LANE):
        if n % t == 0:
            best = t
    assert best, (n, cap)
    return best


def _dot(a, b, dims=None):
    if dims is None:
        return jnp.dot(a, b, preferred_element_type=F32)
    return lax.dot_general(a, b, dims, preferred_element_type=F32)


def _bf(x):
    return x.astype(BF16)


def _sigmoid(x):
    return 1.0 / (1.0 + jnp.exp(-x))


def _rms(x):
    r = lax.rsqrt(jnp.mean(x * x, axis=-1, keepdims=True) + EPS)
    return x * r, r


def _rms_bwd(dxh, xh, r):
    return r * (dxh - xh * jnp.mean(dxh * xh, axis=-1, keepdims=True))


def _mm(a, b, *, mode, out_dtype, name, tm=1024, tn=1024, tk=1024, split_a=False, split_b=False, split_out=False,
        exchange=None, epilogue=None):
    def dims(arr, split):
        if split:
            return arr.shape[1], 2 * arr.shape[2]
        return arr.shape

    ar, ac = dims(a, split_a)
    br, bc = dims(b, split_b)
    if mode == "nn":
        M, K, N = ar, ac, bc
        assert br == K
    elif mode == "nt":
        M, K, N = ar, ac, br
        assert bc == K
    else:
        K, M, N = ar, ac, bc
        assert br == K
    a_cols_half = ac // 2 if split_a else None
    b_cols_half = bc // 2 if split_b else None
    tm = _tile(M, tm)
    tn = _tile((N // 2) if (split_out or (split_b and mode != "nt")) else N, tn)
    tk = _tile((K // 2) if ((split_a and mode != "tn") or (split_b and mode == "nt")) else K, tk)
    if split_a and mode == "tn":
        tm = _tile(M // 2, tm)
    gm, gn, gk = M // tm, N // tn, K // tk
    a_bytes, b_bytes = a.size * a.dtype.itemsize, b.size * b.dtype.itemsize
    rows_outer = gk > 1 or a_bytes + gm * b_bytes <= gn * a_bytes + b_bytes
    grid = (gm, gn, gk) if rows_outer else (gn, gm, gk)

    def spec(split, half, blk, rc):
        def imap(p, q, k):
            r, c = rc(*((p, q) if rows_outer else (q, p)), k)
            if not split:
                return (r, c)
            per_half = half // blk[1]
            return (c // per_half, r, c % per_half)

        return pl.BlockSpec(((None,) + blk) if split else blk, imap)

    if mode == "nn":
        a_spec = spec(split_a, a_cols_half, (tm, tk), lambda i, j, k: (i, k))
        b_spec = spec(split_b, b_cols_half, (tk, tn), lambda i, j, k: (k, j))
        dn = None
    elif mode == "nt":
        a_spec = spec(split_a, a_cols_half, (tm, tk), lambda i, j, k: (i, k))
        b_spec = spec(split_b, b_cols_half, (tn, tk), lambda i, j, k: (j, k))
        dn = NT
    else:
        a_spec = spec(split_a, a_cols_half, (tk, tm), lambda i, j, k: (k, i))
        b_spec = spec(split_b, b_cols_half, (tk, tn), lambda i, j, k: (k, j))
        dn = TN
    o_spec = spec(split_out, N // 2 if split_out else None, (tm, tn), lambda i, j, k: (i, j))
    out_shape = (2, M, N // 2) if split_out else (M, N)

    in_specs, out_specs, args = [a_spec, b_spec], [o_spec], (a, b)
    out_shapes = [jax.ShapeDtypeStruct(out_shape, out_dtype)]
    semantics = ("parallel", "parallel", "arbitrary")

    def store(result, extra, outs):
        outs[0][...] = result[...].astype(outs[0].dtype)

    if epilogue is not None:
        assert gn == 1 and not split_out
        n_vec = epilogue.n_out_vecs
        row = pl.BlockSpec((tm, N), lambda p, q, k: ((p if rows_outer else q), 0))
        vec = pl.BlockSpec((1, N), lambda p, q, k: (0, 0))
        in_specs += [row] * len(epilogue.rows) + [vec] * len(epilogue.vecs)
        args += tuple(epilogue.rows) + tuple(epilogue.vecs)
        out_specs = [row] * len(epilogue.out_rows) + [vec] * n_vec
        out_shapes = ([jax.ShapeDtypeStruct((M, N), dt) for dt in epilogue.out_rows]
                      + [jax.ShapeDtypeStruct((1, N), F32)] * n_vec)
        semantics = ("arbitrary",) * 3

        def store(result, extra, outs):
            n_rows, n_out_rows, sub = len(epilogue.rows), len(epilogue.out_rows), min(EPILOGUE_ROWS, tm)
            for r in range(0, tm, sub):
                rows = pl.ds(r, sub)
                epilogue.fn(result[r:r + sub], *[ref.at[rows] for ref in extra[:n_rows]], *extra[n_rows:],
                            *[ref.at[rows] for ref in outs[:n_out_rows]], *outs[n_out_rows:])

    n_extra = len(in_specs) - 2
    n_out = len(out_specs)

    def body(a_ref, b_ref, *refs):
        extra, outs, scratch_refs = refs[:n_extra], refs[n_extra:n_extra + n_out], refs[n_extra + n_out:]
        k = pl.program_id(2)
        if epilogue is not None:
            @pl.when((pl.program_id(0) == 0) & (pl.program_id(1) == 0) & (k == 0))
            def _():
                for ref in outs[n_out - epilogue.n_out_vecs:]:
                    ref[...] = jnp.zeros_like(ref)

        if gk == 1:
            store(_dot(_bf(a_ref[...]), _bf(b_ref[...]), dn), extra, outs)
            return
        acc_ref = scratch_refs[0]

        @pl.when(k == 0)
        def _():
            acc_ref[...] = jnp.zeros_like(acc_ref)

        acc_ref[...] += _dot(_bf(a_ref[...]), _bf(b_ref[...]), dn)

        @pl.when(k == gk - 1)
        def _():
            store(acc_ref, extra, outs)

    out = _hosted_call(
        body, name=name, grid=grid, in_specs=in_specs, out_specs=out_specs, out_shape=out_shapes,
        scratch=[] if gk == 1 else [pltpu.VMEM((tm, tn), F32)], args=args, semantics=semantics, exchange=exchange)
    return out[0] if (exchange is None and epilogue is None) else out


ROWS = 512


def _row_spec(tr, cols):
    return pl.BlockSpec((tr, cols), lambda i: (i, 0))


def _vec_spec(cols):
    return pl.BlockSpec((1, cols), lambda i: (0, 0))


def _norm_fwd(x, g, name, exchange=None):
    T, Dm = x.shape
    tr = min(ROWS, T)

    def body(x_ref, g_ref, h_ref):
        xh, _ = _rms(x_ref[...])
        h_ref[...] = (xh * g_ref[...]).astype(h_ref.dtype)

    out = _hosted_call(
        body, name=name, grid=(T // tr,), in_specs=[_row_spec(tr, Dm), _vec_spec(Dm)], out_specs=[_row_spec(tr, Dm)],
        out_shape=[jax.ShapeDtypeStruct((T, Dm), BF16)], scratch=[], args=(x, g), semantics=("parallel",),
        exchange=exchange)
    return out[0] if exchange is None else out


def _post_pre(x, g_post, g_pre):
    def fn(m, x_ref, gp_ref, gn_ref, xo_ref, h_ref, m_ref):
        mh, _ = _rms(m)
        xn = x_ref[...] + mh * gp_ref[...]
        xo_ref[...] = xn
        xh, _ = _rms(xn)
        h_ref[...] = (xh * gn_ref[...]).astype(h_ref.dtype)
        m_ref[...] = m.astype(m_ref.dtype)

    return _RowEpilogue(fn, [x], [g_post, g_pre], [F32, BF16, BF16], 0)


def _final(x2, target, g_post):
    def fn(y, x_ref, t_ref, g_ref, dx_ref, dy_ref, loss_ref, dg_ref):
        g = g_ref[...]
        yh, r = _rms(y)
        d = x_ref[...] + yh * g - t_ref[...]
        loss_ref[...] += 0.5 * jnp.sum(jnp.mean(d * d, axis=-1, keepdims=True))
        dx = d * (1.0 / d.shape[-1])
        dx_ref[...] = dx
        dy_ref[...] = _rms_bwd(dx * g, yh, r).astype(dy_ref.dtype)
        dg_ref[...] += jnp.sum(dx * yh, axis=0, keepdims=True)

    return _RowEpilogue(fn, [x2, target], [g_post], [F32, BF16], 2)


class _RowEpilogue:
    def __init__(self, fn, rows, vecs, out_rows, n_out_vecs):
        self.fn, self.rows, self.vecs, self.out_rows, self.n_out_vecs = fn, rows, vecs, out_rows, n_out_vecs


def _norm_bwd2(dx_cur, x_prev, m_prev, g_pre, g_post):
    def fn(dh, dx_ref, x_ref, m_ref, gn_ref, gp_ref, dxo_ref, dm_ref, dgn_ref, dgp_ref):
        xh, r = _rms(x_ref[...])
        dx = dx_ref[...] + _rms_bwd(dh * gn_ref[...], xh, r)
        dxo_ref[...] = dx
        dgn_ref[...] += jnp.sum(dh * xh, axis=0, keepdims=True)
        mh, rm = _rms(m_ref[...].astype(F32))
        dm_ref[...] = _rms_bwd(dx * gp_ref[...], mh, rm).astype(dm_ref.dtype)
        dgp_ref[...] += jnp.sum(dx * mh, axis=0, keepdims=True)

    return _RowEpilogue(fn, [dx_cur, x_prev, m_prev], [g_pre, g_post], [F32, BF16], 2)


def _norm_bwd1(dx_cur, x_prev, g_pre):
    def fn(dh, dx_ref, x_ref, gn_ref, dxo_ref, dgn_ref):
        xh, r = _rms(x_ref[...])
        dxo_ref[...] = dx_ref[...] + _rms_bwd(dh * gn_ref[...], xh, r)
        dgn_ref[...] += jnp.sum(dh * xh, axis=0, keepdims=True)

    return _RowEpilogue(fn, [dx_cur, x_prev], [g_pre], [F32], 1)


def _gain_bwd(x, dh_a, dh_b, name):
    T, Dm = x.shape

    def body(x_ref, a_ref, b_ref, dg_ref):
        xh, _ = _rms(x_ref[...])
        dg_ref[...] = jnp.sum((a_ref[...] + b_ref[...]) * xh, axis=0, keepdims=True)

    return pl.pallas_call(
        body, name=name, grid=(1,), in_specs=[_row_spec(T, Dm)] * 3, out_specs=_vec_spec(Dm),
        out_shape=jax.ShapeDtypeStruct((1, Dm), F32), compiler_params=_params("arbitrary"),
    )(x, dh_a, dh_b)


ATTN_GROUP = ATTN_Q_HEADS // ATTN_KV_HEADS
ASSEMBLE_ROWS = 1024


def _swa_mask(n):
    rows = ATTN_GROUP * ATTN_BLOCK
    row = lax.broadcasted_iota(jnp.int32, (rows, 2 * ATTN_BLOCK), 0) & (ATTN_BLOCK - 1)
    col = lax.broadcasted_iota(jnp.int32, (rows, 2 * ATTN_BLOCK), 1)
    diff = row + ATTN_BLOCK - col
    return (diff >= 0) & (diff < ATTN_BLOCK) & ((col >= ATTN_BLOCK) | (n > 0))


def _swa_rows(ref, hk, dtype):
    hd = ATTN_HEAD_DIM
    return jnp.concatenate(
        [ref[:, hd * (hk * ATTN_GROUP + g):hd * (hk * ATTN_GROUP + g + 1)].astype(dtype) for g in range(ATTN_GROUP)],
        axis=0)


def _swa_per_row(vals):
    seg = lax.broadcasted_iota(jnp.int32, (ATTN_GROUP * ATTN_BLOCK, 1), 0) // ATTN_BLOCK
    col = jnp.zeros((ATTN_GROUP * ATTN_BLOCK, 1), F32)
    for g, val in enumerate(vals):
        col = jnp.where(seg == g, val, col)
    return col


def _swa_specs():
    blk = ATTN_BLOCK
    prev = lambda n: jnp.maximum(n - 1, 0)
    return [
        pl.BlockSpec(memory_space=pltpu.SMEM),
        pl.BlockSpec((blk, 512), lambda n: (n, 0)),
        pl.BlockSpec((blk, 128), lambda n: (prev(n), 4)),
        pl.BlockSpec((blk, 128), lambda n: (n, 4)),
        pl.BlockSpec((blk, 128), lambda n: (prev(n), 5)),
        pl.BlockSpec((blk, 128), lambda n: (n, 5)),
    ]


def _swa_fwd(z, sinks, name, exchange=None):
    T = z.shape[0]
    blk, hd = ATTN_BLOCK, ATTN_HEAD_DIM
    scale = hd ** -0.5

    def body(sink_ref, q_ref, kp_ref, kc_ref, vp_ref, vc_ref, o_ref, lse_ref):
        allowed = _swa_mask(pl.program_id(0))
        hks = range(ATTN_KV_HEADS)
        kss = [slice(hd * hk, hd * hk + hd) for hk in hks]
        k = [_bf(jnp.concatenate([kp_ref[:, ks], kc_ref[:, ks]], axis=0)) for ks in kss]
        v = [_bf(jnp.concatenate([vp_ref[:, ks], vc_ref[:, ks]], axis=0)) for ks in kss]
        s = [jnp.where(allowed, _dot(_swa_rows(q_ref, hk, BF16), k[hk], NT) * scale, -1e30) for hk in hks]
        sink = [_swa_per_row([sink_ref[0, hk * ATTN_GROUP + g] for g in range(ATTN_GROUP)]) for hk in hks]
        m = [jnp.maximum(jnp.max(s[hk], axis=-1, keepdims=True), sink[hk]) for hk in hks]
        p = [jnp.exp(s[hk] - m[hk]) for hk in hks]
        l = [jnp.sum(p[hk], axis=-1, keepdims=True) + jnp.exp(sink[hk] - m[hk]) for hk in hks]
        o = [_dot(_bf(p[hk] / l[hk]), v[hk]).astype(o_ref.dtype) for hk in hks]
        for hk in hks:
            lse = m[hk] + jnp.log(l[hk])
            for g in range(ATTN_GROUP):
                h = hk * ATTN_GROUP + g
                o_ref[:, hd * h:hd * (h + 1)] = o[hk][blk * g:blk * (g + 1)]
                lse_ref[:, h:h + 1] = lse[blk * g:blk * (g + 1)]

    return _hosted_call(
        body, name=name, grid=(T // blk,), in_specs=_swa_specs(),
        out_specs=[pl.BlockSpec((blk, 512), lambda n: (n, 0)), pl.BlockSpec((blk, ATTN_Q_HEADS), lambda n: (n, 0))],
        out_shape=[jax.ShapeDtypeStruct((T, 512), BF16), jax.ShapeDtypeStruct((T, ATTN_Q_HEADS), F32)],
        scratch=[], args=(sinks, z, z, z, z, z), semantics=("parallel",), exchange=exchange)


def _swa_bwd(z, sinks, dcat, lse, name):
    T = z.shape[0]
    blk, hd = ATTN_BLOCK, ATTN_HEAD_DIM
    scale = hd ** -0.5
    group = ATTN_Q_HEADS // ATTN_KV_HEADS

    def body(sink_ref, q_ref, kp_ref, kc_ref, vp_ref, vc_ref, do_ref, lse_ref,
             dq_ref, dka_ref, dkb_ref, dva_ref, dvb_ref, dsink_ref):
        @pl.when(pl.program_id(0) == 0)
        def _():
            dsink_ref[...] = jnp.zeros_like(dsink_ref)

        allowed = _swa_mask(pl.program_id(0))
        lane = lax.broadcasted_iota(jnp.int32, (1, ATTN_Q_HEADS), 1)
        dsink = jnp.zeros((1, ATTN_Q_HEADS), F32)
        hks = range(ATTN_KV_HEADS)
        kss = [slice(hd * hk, hd * hk + hd) for hk in hks]
        k = [_bf(jnp.concatenate([kp_ref[:, ks], kc_ref[:, ks]], axis=0)) for ks in kss]
        v = [_bf(jnp.concatenate([vp_ref[:, ks], vc_ref[:, ks]], axis=0)) for ks in kss]
        qs = [_swa_rows(q_ref, hk, BF16) for hk in hks]
        dos = [_swa_rows(do_ref, hk, BF16) for hk in hks]
        lse = [jnp.concatenate([lse_ref[:, hk * group + g:hk * group + g + 1] for g in range(group)], axis=0)
               for hk in hks]
        s = [_dot(qs[hk], k[hk], NT) * scale for hk in hks]
        dp = [_dot(dos[hk], v[hk], NT) for hk in hks]
        p = [jnp.where(allowed, jnp.exp(jnp.where(allowed, s[hk], -1e30) - lse[hk]), 0.0) for hk in hks]
        delta = [jnp.sum(p[hk] * dp[hk], axis=-1, keepdims=True) for hk in hks]
        ds = [_bf(p[hk] * (dp[hk] - delta[hk]) * scale) for hk in hks]
        dq = [_dot(ds[hk], k[hk]).astype(dq_ref.dtype) for hk in hks]
        dk = [_dot(ds[hk], qs[hk], TN) for hk in hks]
        dv = [_dot(_bf(p[hk]), dos[hk], TN) for hk in hks]
        for hk in hks:
            sink = _swa_per_row([sink_ref[0, hk * group + g] for g in range(group)])
            sink_part = jnp.exp(sink - lse[hk]) * delta[hk]
            for g in range(group):
                h = hk * group + g
                dq_ref[:, hd * h:hd * (h + 1)] = dq[hk][blk * g:blk * (g + 1)]
                dsink = dsink + jnp.where(lane == h, -jnp.sum(sink_part[blk * g:blk * (g + 1)]), 0.0)
            dkb_ref[:, kss[hk]] = dk[hk][:blk]
            dka_ref[:, kss[hk]] = dk[hk][blk:]
            dvb_ref[:, kss[hk]] = dv[hk][:blk]
            dva_ref[:, kss[hk]] = dv[hk][blk:]
        dsink_ref[...] += dsink

    kv_out = pl.BlockSpec((blk, 128), lambda n: (n, 0))
    return pl.pallas_call(
        body, name=name, grid=(T // blk,),
        in_specs=_swa_specs() + [pl.BlockSpec((blk, 512), lambda n: (n, 0)),
                                 pl.BlockSpec((blk, ATTN_Q_HEADS), lambda n: (n, 0))],
        out_specs=[pl.BlockSpec((blk, 512), lambda n: (n, 0)), kv_out, kv_out, kv_out, kv_out,
                   pl.BlockSpec((1, ATTN_Q_HEADS), lambda n: (0, 0))],
        out_shape=[jax.ShapeDtypeStruct((T, 512), BF16)] + [jax.ShapeDtypeStruct((T, 128), F32)] * 4
        + [jax.ShapeDtypeStruct((1, ATTN_Q_HEADS), F32)],
        compiler_params=_params("arbitrary"),
    )(sinks, z, z, z, z, z, dcat, lse)


def _assemble_dz(dq_a, dka, dkb, dva, dvb, dqr, dfr, dir_, dgr, name):
    T = dq_a.shape[0]
    blk = ATTN_BLOCK
    rows = min(ASSEMBLE_ROWS, T)
    nb, per = T // rows, rows // blk

    def body(dq_ref, dka_ref, dkb_ref, dkn_ref, dva_ref, dvb_ref, dvn_ref, dqr_ref, dfr_ref, dir_ref, dgr_ref, o_ref):
        has_next = pl.program_id(0) < nb - 1

        def with_next(a_ref, b_ref, n_ref):
            after = jnp.where(has_next, n_ref[...], 0.0)
            shifted = after if per == 1 else jnp.concatenate([b_ref[blk:, :], after], axis=0)
            return (a_ref[...] + shifted).astype(o_ref.dtype)

        o_ref[:, 0:512] = dq_ref[...]
        o_ref[:, 512:640] = with_next(dka_ref, dkb_ref, dkn_ref)
        o_ref[:, 640:768] = with_next(dva_ref, dvb_ref, dvn_ref)
        o_ref[:, 768:1280] = dqr_ref[...]
        o_ref[:, 1280:1792] = dfr_ref[...]
        o_ref[:, 1792:2304] = dir_ref[...]
        o_ref[:, 2304:2816] = dgr_ref[...]

    cur = lambda w: pl.BlockSpec((rows, w), lambda n: (n, 0))
    nxt = pl.BlockSpec((blk, 128), lambda n: (jnp.minimum((n + 1) * per, T // blk - 1), 0))
    return pl.pallas_call(
        body, name=name, grid=(nb,),
        in_specs=[cur(512), cur(128), cur(128), nxt, cur(128), cur(128), nxt, cur(512), cur(512), cur(512), cur(512)],
        out_specs=pl.BlockSpec((rows, 2816), lambda n: (n, 0)),
        out_shape=jax.ShapeDtypeStruct((T, 2816), BF16), compiler_params=_params("parallel"),
    )(dq_a, dka, dkb, dkb, dva, dvb, dvb, dqr, dfr, dir_, dgr)


HGRN_ROWS = 512


def _hgrn_consts():
    c = HGRN_CHUNK
    r = lax.broadcasted_iota(jnp.int32, (c, c), 0)
    s = lax.broadcasted_iota(jnp.int32, (c, c), 1)
    rcol = lax.broadcasted_iota(jnp.int32, (c, 1), 0)
    same_block, upper = [], []
    for m in HGRN_LEVELS:
        same_block.append((r & ~(2 * m - 1)) == (s & ~(2 * m - 1)))
        upper.append((rcol & (2 * m - 1)) >= m)
    cum_mat = jnp.where(s <= r, 1.0, 0.0).astype(BF16)
    rev_mat = jnp.where(s >= r, 1.0, 0.0).astype(BF16)
    return cum_mat, rev_mat, r == s, same_block, upper, rcol & 3, s == r - 1


def _hgrn_level_decay(g, b, m, pos4):
    c = HGRN_CHUNK
    if m == 1:
        return jnp.exp(jnp.where((pos4 & 1) == 1, g, 0.0))
    if m == 2:
        after, before = pltpu.roll(g, c - 1, 0), pltpu.roll(g, 1, 0)
        return jnp.exp(jnp.where(pos4 == 0, after, jnp.where(pos4 == 1, 0.0, jnp.where(pos4 == 2, g, g + before))))
    b3 = b.reshape(c // (2 * m), 2 * m, HGRN_DIM)
    bref = jnp.broadcast_to(b3[:, m - 1:m, :], b3.shape).reshape(c, HGRN_DIM)
    return jnp.exp(-jnp.abs(b - bref))


def _split3(x):
    hi = _bf(x)
    r1 = x - hi.astype(F32)
    mid = _bf(r1)
    lo = _bf(r1 - mid.astype(F32))
    return jnp.concatenate([hi, mid, lo], axis=1)


def _dot_hilo(a, b):
    r, c = a.shape[0], b.shape[1]
    a_hi, b_hi = _bf(a), _bf(b)
    a2 = jnp.concatenate([a_hi, _bf(a - a_hi.astype(F32))], axis=0)
    b2 = jnp.concatenate([b_hi, _bf(b - b_hi.astype(F32))], axis=1)
    y = _dot(a2, b2)
    return y[:r, :c] + y[:r, c:] + y[r:, :c]


def _fold3(y):
    w = y.shape[1] // 3
    return y[:, :w] + y[:, w:2 * w] + y[:, 2 * w:]


def _hgrn_gates(qr, fr, lb):
    sq = _sigmoid(qr)
    q = qr * sq * (HGRN_DIM ** -0.5)
    sf = _sigmoid(fr)
    f = lb + (1.0 - lb) * sf
    k = (1.0 - lb) * _sigmoid(-fr)
    return q, sq, sf, f, k, jnp.log(f)


def _hgrn_intra(q, k, g, b, consts, scores=True):
    _, _, eye, same_block, upper, pos4, below = consts
    heads = range(len(q))
    a = None
    if scores:
        a = [jnp.where(eye, jnp.sum(q[hh] * k[hh], axis=1, keepdims=True), 0.0) for hh in heads]
    saved = [[] for _ in heads]
    for i, m in enumerate(HGRN_LEVELS):
        up = upper[i]
        e = [_hgrn_level_decay(g[hh], b[hh], m, pos4) for hh in heads]
        qt = [jnp.where(up, q[hh] * e[hh], 0.0) for hh in heads]
        kt = [jnp.where(up, 0.0, k[hh] * e[hh]) for hh in heads]
        for hh in heads:
            saved[hh].append((e[hh], qt[hh], kt[hh]))
        if not scores:
            continue
        if m == 1:
            for hh in heads:
                pair = jnp.sum(qt[hh] * pltpu.roll(kt[hh], 1, 0), axis=1, keepdims=True)
                a[hh] = a[hh] + jnp.where(below, pair, 0.0)
            continue
        p = [_dot(_bf(qt[hh]), _bf(kt[hh]), NT) for hh in heads]
        for hh in heads:
            a[hh] = a[hh] + jnp.where(same_block[i], p[hh], 0.0)
    return a, saved


def _hgrn_specs(tb, nb, rev):
    tmap = (lambda t: nb - 1 - t) if rev else (lambda t: t)
    assert HGRN_PAIR == HGRN_HEADS
    return [pl.BlockSpec((tb, 2816), lambda h, t: (tmap(t), 0)),
            pl.BlockSpec((1, HGRN_PAIR * HGRN_DIM), lambda h, t: (0, h)),
            pl.BlockSpec((1, HGRN_DIM), lambda h, t: (0, 0))]


def _hgrn_z(z_ref, sl, base, head):
    return z_ref[sl, base + HGRN_DIM * head:base + HGRN_DIM * (head + 1)].astype(F32)


def _hgrn_fwd(z, lb, onw, name, exchange=None):
    T = z.shape[0]
    tb = min(HGRN_ROWS, T)
    nb, c, nc = T // tb, HGRN_CHUNK, min(HGRN_ROWS, T) // HGRN_CHUNK

    def body(z_ref, lb_ref, onw_ref, rec_ref, o_ref, st_ref, a_ref, state):
        @pl.when(pl.program_id(1) == 0)
        def _():
            state[...] = jnp.zeros_like(state)

        consts = _hgrn_consts()
        lbv = lb_ref[...]
        onwv = onw_ref[...]

        def chunk(ci, carry):
            sl = pl.ds(pl.multiple_of(ci * c, c), c)
            heads = range(HGRN_PAIR)
            lss = [slice(HGRN_DIM * hh, HGRN_DIM * (hh + 1)) for hh in heads]
            gates = [_hgrn_gates(_hgrn_z(z_ref, sl, Z_Q, hh), _hgrn_z(z_ref, sl, Z_F, hh), lbv[:, lss[hh]])
                     for hh in heads]
            q, k, g = [t[0] for t in gates], [t[4] for t in gates], [t[5] for t in gates]
            v = [_bf(_hgrn_z(z_ref, sl, Z_I, hh)) for hh in heads]
            b = [_fold3(_dot(consts[0], _split3(g[hh]))) for hh in heads]
            a, _ = _hgrn_intra(q, k, g, b, consts)
            st = [state[hh] for hh in heads]
            for hh in heads:
                st_ref[hh, ci] = st[hh]
            bl = [b[hh][c - 1:c, :] for hh in heads]
            o_state = [_dot(_bf(q[hh] * jnp.exp(b[hh])), _bf(st[hh]), NT) for hh in heads]
            kv = [_dot(v[hh], _bf(k[hh] * jnp.exp(bl[hh] - b[hh])), TN) for hh in heads]
            a = [_bf(a[hh]) for hh in heads]
            o = [_dot(a[hh], v[hh]) + o_state[hh] for hh in heads]
            for hh in heads:
                a_ref[sl, c * hh:c * (hh + 1)] = a[hh]
                state[hh] = st[hh] * jnp.exp(bl[hh]) + kv[hh]
                o_ref[sl, lss[hh]] = o[hh]
                oh, _ = _rms(o[hh])
                gr = _hgrn_z(z_ref, sl, Z_G, hh)
                rec_ref[sl, lss[hh]] = (oh * onwv * (gr * _sigmoid(gr))).astype(rec_ref.dtype)
            return carry

        lax.fori_loop(0, nc, chunk, 0)

    in_specs = _hgrn_specs(tb, nb, False)
    out_blk = pl.BlockSpec((tb, HGRN_PAIR * HGRN_DIM), lambda h, t: (t, h))
    return _hosted_call(
        body, name=name, grid=(HGRN_HEADS // HGRN_PAIR, nb), in_specs=in_specs,
        out_specs=[out_blk, out_blk, pl.BlockSpec((HGRN_PAIR, nc, HGRN_DIM, HGRN_DIM), lambda h, t: (h, t, 0, 0)),
                   pl.BlockSpec((tb, HGRN_PAIR * c), lambda h, t: (t, h))],
        out_shape=[jax.ShapeDtypeStruct((T, 512), BF16), jax.ShapeDtypeStruct((T, 512), F32),
                   jax.ShapeDtypeStruct((HGRN_HEADS, T // c, HGRN_DIM, HGRN_DIM), F32),
                   jax.ShapeDtypeStruct((T, HGRN_HEADS * c), BF16)],
        scratch=[pltpu.VMEM((HGRN_PAIR, HGRN_DIM, HGRN_DIM), F32)], args=(z, lb, onw),
        semantics=("parallel", "arbitrary"), exchange=exchange)


def _hgrn_bwd(z, lb, onw, o, states, scores, dcat, name, exchange=None):
    T = z.shape[0]
    tb = min(HGRN_ROWS, T)
    nb, c, nc = T // tb, HGRN_CHUNK, min(HGRN_ROWS, T) // HGRN_CHUNK

    def body(z_ref, lb_ref, onw_ref, o_ref, st_ref, drec_ref, a_ref,
             dqr_ref, dfr_ref, dir_ref, dgr_ref, dlb_ref, donw_ref, dstate):
        @pl.when(pl.program_id(1) == 0)
        def _():
            dstate[...] = jnp.zeros_like(dstate)
            dlb_ref[...] = jnp.zeros_like(dlb_ref)

        @pl.when((pl.program_id(0) == 0) & (pl.program_id(1) == 0))
        def _():
            donw_ref[...] = jnp.zeros_like(donw_ref)

        consts = _hgrn_consts()
        rev_mat, eye, same_block, upper = consts[1:5]
        below = consts[6]
        lbv = lb_ref[...]
        onwv = onw_ref[...]
        last = lax.broadcasted_iota(jnp.int32, (c, 1), 0) == c - 1

        def chunk(i, carry):
            ci = nc - 1 - i
            sl = pl.ds(pl.multiple_of(ci * c, c), c)
            hs = range(HGRN_PAIR)
            lss = [slice(HGRN_DIM * hh, HGRN_DIM * (hh + 1)) for hh in hs]
            qr = [_hgrn_z(z_ref, sl, Z_Q, hh) for hh in hs]
            gates = [_hgrn_gates(qr[hh], _hgrn_z(z_ref, sl, Z_F, hh), lbv[:, lss[hh]]) for hh in hs]
            q, sq, sf, f, k, g = ([t[j] for t in gates] for j in range(6))
            v = [_bf(_hgrn_z(z_ref, sl, Z_I, hh)) for hh in hs]
            b = [_fold3(_dot(consts[0], _split3(g[hh]))) for hh in hs]
            _, saved = _hgrn_intra(q, k, g, b, consts, scores=False)
            a = [a_ref[sl, c * hh:c * (hh + 1)] for hh in hs]
            st = [st_ref[hh, ci] for hh in hs]
            dst = [dstate[hh] for hh in hs]

            gr = [_hgrn_z(z_ref, sl, Z_G, hh) for hh in hs]
            sg = [_sigmoid(gr[hh]) for hh in hs]
            norm = [_rms(o_ref[sl, ls]) for ls in lss]
            oh, r = [t[0] for t in norm], [t[1] for t in norm]
            drec = [drec_ref[sl, ls].astype(F32) for ls in lss]
            don = [drec[hh] * (gr[hh] * sg[hh]) for hh in hs]
            do = [_bf(_rms_bwd(don[hh] * onwv, oh[hh], r[hh])) for hh in hs]
            donw = jnp.sum(don[0] * oh[0], axis=0, keepdims=True)
            for hh in hs:
                dgr_ref[sl, lss[hh]] = (drec[hh] * oh[hh] * onwv
                                        * (sg[hh] * (1.0 + gr[hh] * (1.0 - sg[hh])))).astype(dgr_ref.dtype)
                if hh:
                    donw = donw + jnp.sum(don[hh] * oh[hh], axis=0, keepdims=True)
            donw_ref[...] += donw

            eb = [jnp.exp(b[hh]) for hh in hs]
            bl = [b[hh][c - 1:c, :] for hh in hs]
            ebl = [jnp.exp(bl[hh]) for hh in hs]
            ekb = [jnp.exp(bl[hh] - b[hh]) for hh in hs]
            qe = [q[hh] * eb[hh] for hh in hs]
            ke = [k[hh] * ekb[hh] for hh in hs]
            da = [_dot(do[hh], v[hh], NT) for hh in hs]
            dat = [_dot(v[hh], do[hh], NT) for hh in hs]
            dqe = [_dot(do[hh], _bf(st[hh])) for hh in hs]
            dke = [_dot(v[hh], _bf(dst[hh])) for hh in hs]
            dv_a = [_dot(a[hh], do[hh], TN) for hh in hs]
            dv_s = [_dot(_bf(ke[hh]), _bf(dst[hh]), NT) for hh in hs]
            dst_in = [_dot(do[hh], _bf(qe[hh]), TN) for hh in hs]
            dad = [jnp.sum(jnp.where(eye, da[hh], 0.0), axis=1, keepdims=True) for hh in hs]
            dq = [dqe[hh] * eb[hh] + dad[hh] * k[hh] for hh in hs]
            dk = [dke[hh] * ekb[hh] + dad[hh] * q[hh] for hh in hs]
            db_last = [jnp.sum(dke[hh] * ke[hh], axis=0, keepdims=True)
                       + jnp.sum(dst[hh] * st[hh], axis=0, keepdims=True) * ebl[hh] for hh in hs]
            for hh in hs:
                dstate[hh] = dst[hh] * ebl[hh] + dst_in[hh]
                dir_ref[sl, lss[hh]] = (dv_a[hh] + dv_s[hh]).astype(dir_ref.dtype)
            for lvl, m in enumerate(HGRN_LEVELS):
                if m == 1:
                    pair = [jnp.sum(jnp.where(below, da[hh], 0.0), axis=1, keepdims=True) for hh in hs]
                    xq = [pair[hh] * pltpu.roll(saved[hh][lvl][2], 1, 0) for hh in hs]
                    xk = [pltpu.roll(pair[hh] * saved[hh][lvl][1], c - 1, 0) for hh in hs]
                else:
                    xq = [_dot_hilo(jnp.where(same_block[lvl], da[hh], 0.0), saved[hh][lvl][2]) for hh in hs]
                    xk = [_dot_hilo(jnp.where(same_block[lvl], dat[hh], 0.0), saved[hh][lvl][1]) for hh in hs]
                for hh in hs:
                    e = saved[hh][lvl][0]
                    dq[hh] = dq[hh] + jnp.where(upper[lvl], xq[hh] * e, 0.0)
                    dk[hh] = dk[hh] + jnp.where(upper[lvl], 0.0, xk[hh] * e)
            db = [q[hh] * dq[hh] - k[hh] * dk[hh] + jnp.where(last, db_last[hh], 0.0) for hh in hs]
            dg = [_fold3(_dot(rev_mat, _split3(db[hh]))) for hh in hs]

            for hh in hs:
                ls = lss[hh]
                dqr_ref[sl, ls] = (dq[hh] * (HGRN_DIM ** -0.5)
                                   * (sq[hh] * (1.0 + qr[hh] * (1.0 - sq[hh])))).astype(dqr_ref.dtype)
                dfk = dg[hh] / f[hh] - dk[hh]
                dfr_ref[sl, ls] = ((1.0 - lbv[:, ls]) * sf[hh] * (1.0 - sf[hh]) * dfk).astype(dfr_ref.dtype)
                dlb_ref[:, ls] += jnp.sum((1.0 - sf[hh]) * dfk, axis=0, keepdims=True)
            return carry

        lax.fori_loop(0, nc, chunk, 0)

    in_specs = _hgrn_specs(tb, nb, True)
    rblk = pl.BlockSpec((tb, HGRN_PAIR * HGRN_DIM), lambda h, t: (nb - 1 - t, h))
    in_specs = in_specs + [
        rblk,
        pl.BlockSpec((HGRN_PAIR, nc, HGRN_DIM, HGRN_DIM), lambda h, t: (h, nb - 1 - t, 0, 0)),
        pl.BlockSpec((tb, HGRN_PAIR * HGRN_DIM), lambda h, t: (nb - 1 - t, 4 // HGRN_PAIR + h)),
        pl.BlockSpec((tb, HGRN_PAIR * c), lambda h, t: (nb - 1 - t, h)),
    ]
    return _hosted_call(
        body, name=name, grid=(HGRN_HEADS // HGRN_PAIR, nb), in_specs=in_specs,
        out_specs=[rblk, rblk, rblk, rblk, pl.BlockSpec((1, HGRN_PAIR * HGRN_DIM), lambda h, t: (0, h)),
                   pl.BlockSpec((1, HGRN_DIM), lambda h, t: (0, 0))],
        out_shape=[jax.ShapeDtypeStruct((T, 512), BF16)] * 4
        + [jax.ShapeDtypeStruct((1, 512), F32), jax.ShapeDtypeStruct((1, HGRN_DIM), F32)],
        scratch=[pltpu.VMEM((HGRN_PAIR, HGRN_DIM, HGRN_DIM), F32)], args=(z, lb, onw, o, states, dcat, scores),
        semantics=("arbitrary", "arbitrary"), exchange=exchange)


def _lower_bound(logits, name):
    def body(l_ref, lb_ref):
        l0, l1 = l_ref[0:1, :], l_ref[1:2, :]
        m = jnp.maximum(l0, l1)
        e0, e1 = jnp.exp(l0 - m), jnp.exp(l1 - m)
        lb_ref[...] = e0 / (e0 + e1)

    return pl.pallas_call(
        body, name=name, out_shape=jax.ShapeDtypeStruct((1, logits.shape[1]), F32),
    )(logits)


def _lower_bound_bwd(lb, dlb, name):
    def body(lb_ref, dlb_ref, dl_ref):
        p = lb_ref[...]
        d0 = dlb_ref[...] * p * (1.0 - p)
        dl_ref[0:1, :] = d0
        dl_ref[1:2, :] = -d0

    return pl.pallas_call(
        body, name=name, out_shape=jax.ShapeDtypeStruct((2, lb.shape[1]), F32),
    )(lb, dlb)


CA_ROWS = 1024


def _ca_fwd(q, k, v, name):
    T, W = q.shape
    M = k.shape[0]
    tq = min(CA_ROWS, T)
    scale = CA_HEAD_DIM ** -0.5

    def body(q_ref, k_ref, v_ref, o_ref):
        hss = [slice(CA_HEAD_DIM * h, CA_HEAD_DIM * (h + 1)) for h in range(CA_HEADS)]
        s = [_dot(q_ref[:, hs], k_ref[:, hs], NT) * scale for hs in hss]
        p = [jnp.exp(sh - jnp.max(sh, axis=-1, keepdims=True)) for sh in s]
        p = [ph / jnp.sum(ph, axis=-1, keepdims=True) for ph in p]
        o = [_dot(_bf(ph), v_ref[:, hs]) for ph, hs in zip(p, hss)]
        for oh, hs in zip(o, hss):
            o_ref[:, hs] = oh.astype(o_ref.dtype)

    full = pl.BlockSpec((M, W), lambda i: (0, 0))
    return pl.pallas_call(
        body, name=name, grid=(T // tq,), in_specs=[_row_spec(tq, W), full, full], out_specs=_row_spec(tq, W),
        out_shape=jax.ShapeDtypeStruct((T, W), BF16), compiler_params=_params("parallel"),
    )(q, k, v)


def _ca_bwd(q, k, v, do, name):
    T, W = q.shape
    M = k.shape[0]
    tq = min(CA_ROWS, T)
    scale = CA_HEAD_DIM ** -0.5

    def body(q_ref, k_ref, v_ref, do_ref, dq_ref, dk_ref, dv_ref):
        @pl.when(pl.program_id(0) == 0)
        def _():
            dk_ref[...] = jnp.zeros_like(dk_ref)
            dv_ref[...] = jnp.zeros_like(dv_ref)

        heads = range(CA_HEADS)
        hss = [slice(CA_HEAD_DIM * h, CA_HEAD_DIM * (h + 1)) for h in heads]
        qh, kh = [q_ref[:, hs] for hs in hss], [k_ref[:, hs] for hs in hss]
        vh, doh = [v_ref[:, hs] for hs in hss], [do_ref[:, hs] for hs in hss]
        s = [_dot(qh[h], kh[h], NT) * scale for h in heads]
        dp = [_dot(doh[h], vh[h], NT) for h in heads]
        p = [jnp.exp(s[h] - jnp.max(s[h], axis=-1, keepdims=True)) for h in heads]
        p = [p[h] / jnp.sum(p[h], axis=-1, keepdims=True) for h in heads]
        ds = [_bf(p[h] * (dp[h] - jnp.sum(p[h] * dp[h], axis=-1, keepdims=True)) * scale) for h in heads]
        dq = [_dot(ds[h], kh[h]) for h in heads]
        dk = [_dot(ds[h], qh[h], TN) for h in heads]
        dv = [_dot(_bf(p[h]), doh[h], TN) for h in heads]
        for h in heads:
            dq_ref[:, hss[h]] = dq[h].astype(dq_ref.dtype)
            dk_ref[:, hss[h]] += dk[h]
            dv_ref[:, hss[h]] += dv[h]

    full = pl.BlockSpec((M, W), lambda i: (0, 0))
    return pl.pallas_call(
        body, name=name, grid=(T // tq,), in_specs=[_row_spec(tq, W), full, full, _row_spec(tq, W)],
        out_specs=[_row_spec(tq, W), full, full],
        out_shape=[jax.ShapeDtypeStruct((T, W), BF16), jax.ShapeDtypeStruct((M, W), F32), jax.ShapeDtypeStruct((M, W), F32)],
        compiler_params=_params("arbitrary"),
    )(q, k, v, do)


FFN_ROWS = 256
FFN_COLS = 1408
GELU_C0 = 0.7978845608028654
GELU_C1 = 0.044715


def _gelu(x):
    t = jnp.tanh(GELU_C0 * (x + GELU_C1 * x * x * x))
    return 0.5 * x * (1.0 + t), t


def _gelu_grad(x, t):
    return 0.5 * (1.0 + t) + 0.5 * x * (1.0 - t * t) * GELU_C0 * (1.0 + 3.0 * GELU_C1 * x * x)


def _shift_down(cur, halo, first, tb):
    row = lax.broadcasted_iota(jnp.int32, (tb, 1), 0)
    h6 = jnp.where(first, 0.0, halo[6:7])
    h7 = jnp.where(first, 0.0, halo[7:8])
    u1 = jnp.where(row == 0, h7, pltpu.roll(cur, 1, 0))
    u2 = jnp.where(row == 0, h6, jnp.where(row == 1, h7, pltpu.roll(cur, 2, 0)))
    return u1, u2


def _conv(u_ref, halo_ref, w_ref, b_ref, half, first, tb):
    cur = u_ref[half]
    u1, u2 = _shift_down(cur, halo_ref[half], first, tb)
    w = w_ref[...]
    return w[0:1] * u2 + w[1:2] * u1 + w[2:3] * cur + b_ref[...], cur, u1, u2


def _ffn_specs(tb, tc, rows_first):
    nj = D_FF // tc
    rc = (lambda a, b: (a, b)) if rows_first else (lambda a, b: (b, a))
    def at(f):
        return lambda a, b: f(*rc(a, b))
    blk = pl.BlockSpec((2, tb, tc), at(lambda t, j: (0, t, j)))
    halo = pl.BlockSpec((2, 8, tc), at(lambda t, j: (0, jnp.maximum(t * (tb // 8) - 1, 0), j)))
    wg = pl.BlockSpec((3, tc), at(lambda t, j: (0, j)))
    wv = pl.BlockSpec((3, tc), at(lambda t, j: (0, j + nj)))
    bg = pl.BlockSpec((1, tc), at(lambda t, j: (0, j)))
    bv = pl.BlockSpec((1, tc), at(lambda t, j: (0, j + nj)))
    flat = pl.BlockSpec((tb, tc), at(lambda t, j: (t, j)))
    return blk, halo, wg, wv, bg, bv, flat


def _glu_fwd(u, cw, cb, name):
    T = u.shape[1]
    tb, tc = min(FFN_ROWS, T), FFN_COLS

    def body(u_ref, halo_ref, wg_ref, wv_ref, bg_ref, bv_ref, a_ref):
        first = pl.program_id(0) == 0
        cg = _conv(u_ref, halo_ref, wg_ref, bg_ref, 0, first, tb)[0]
        cv = _conv(u_ref, halo_ref, wv_ref, bv_ref, 1, first, tb)[0]
        a_ref[...] = (_gelu(cg)[0] * cv).astype(a_ref.dtype)

    blk, halo, wg, wv, bg, bv, flat = _ffn_specs(tb, tc, True)
    return pl.pallas_call(
        body, name=name, grid=(T // tb, D_FF // tc), in_specs=[blk, halo, wg, wv, bg, bv], out_specs=flat,
        out_shape=jax.ShapeDtypeStruct((T, D_FF), BF16), compiler_params=_params("parallel", "parallel"),
    )(u, u, cw, cw, cb, cb)


def _glu_bwd(u, cw, cb, da, name, exchange=None):
    T = u.shape[1]
    tb, tc = min(FFN_ROWS, T), FFN_COLS

    def body(u_ref, halo_ref, wg_ref, wv_ref, bg_ref, bv_ref, da_ref, dc_ref, db_ref, dw_ref):
        first = pl.program_id(1) == 0

        @pl.when(first)
        def _():
            db_ref[...] = jnp.zeros_like(db_ref)
            dw_ref[...] = jnp.zeros_like(dw_ref)

        cg, ug, ug1, ug2 = _conv(u_ref, halo_ref, wg_ref, bg_ref, 0, first, tb)
        cv, uv, uv1, uv2 = _conv(u_ref, halo_ref, wv_ref, bv_ref, 1, first, tb)
        da = da_ref[...]
        gl, t = _gelu(cg)
        dcg = da * cv * _gelu_grad(cg, t)
        dcv = da * gl
        dc_ref[0] = dcg
        dc_ref[1] = dcv
        for half, dc, taps in ((0, dcg, (ug2, ug1, ug)), (1, dcv, (uv2, uv1, uv))):
            db_ref[half] += jnp.sum(dc, axis=0, keepdims=True)
            for tap in range(3):
                dw_ref[half, tap:tap + 1, :] += jnp.sum(dc * taps[tap], axis=0, keepdims=True)

    blk, halo, wg, wv, bg, bv, flat = _ffn_specs(tb, tc, False)
    return _hosted_call(
        body, name=name, grid=(D_FF // tc, T // tb), in_specs=[blk, halo, wg, wv, bg, bv, flat],
        out_specs=[blk, pl.BlockSpec((2, 1, tc), lambda j, t: (0, 0, j)), pl.BlockSpec((2, 3, tc), lambda j, t: (0, 0, j))],
        out_shape=[jax.ShapeDtypeStruct((2, T, D_FF), F32), jax.ShapeDtypeStruct((2, 1, D_FF), F32),
                   jax.ShapeDtypeStruct((2, 3, D_FF), F32)],
        scratch=[], args=(u, u, cw, cw, cb, cb, da), semantics=("parallel", "arbitrary"), exchange=exchange)


def _conv_bwd(dc, cw, name):
    T = dc.shape[1]
    tb, tc = min(FFN_ROWS, T), FFN_COLS
    nt, nj = T // tb, D_FF // tc

    def body(dc_ref, halo_ref, wg_ref, wv_ref, du_ref):
        last = pl.program_id(0) == nt - 1
        row = lax.broadcasted_iota(jnp.int32, (tb, 1), 0)
        for half, w_ref in ((0, wg_ref), (1, wv_ref)):
            cur = dc_ref[half]
            halo = halo_ref[half]
            h0 = jnp.where(last, 0.0, halo[0:1])
            h1 = jnp.where(last, 0.0, halo[1:2])
            d1 = jnp.where(row == tb - 1, h0, pltpu.roll(cur, tb - 1, 0))
            d2 = jnp.where(row == tb - 1, h1, jnp.where(row == tb - 2, h0, pltpu.roll(cur, tb - 2, 0)))
            w = w_ref[...]
            du_ref[half] = (w[2:3] * cur + w[1:2] * d1 + w[0:1] * d2).astype(du_ref.dtype)

    blk = pl.BlockSpec((2, tb, tc), lambda t, j: (0, t, j))
    halo = pl.BlockSpec((2, 8, tc), lambda t, j: (0, jnp.minimum((t + 1) * (tb // 8), T // 8 - 1), j))
    wg = pl.BlockSpec((3, tc), lambda t, j: (0, j))
    wv = pl.BlockSpec((3, tc), lambda t, j: (0, j + nj))
    return pl.pallas_call(
        body, name=name, grid=(nt, nj), in_specs=[blk, halo, wg, wv], out_specs=blk,
        out_shape=jax.ShapeDtypeStruct((2, T, D_FF), BF16), compiler_params=_params("parallel", "parallel"),
    )(dc, dc, cw, cw)


def _mesh_pos():
    return lax.axis_index("x"), lax.axis_index("y"), lax.axis_index("c")


def _peer(pos, k):
    return (pos[0] ^ ((k >> 2) & 1), pos[1] ^ ((k >> 1) & 1), pos[2] ^ (k & 1))


def _index(pos):
    return 4 * pos[0] + 2 * pos[1] + pos[2]


class _Exchange:
    def __init__(self, kind, buf, relay=False):
        assert kind in ("gather", "scatter") and not (relay and kind == "scatter")
        self.kind, self.buf, self.relay = kind, buf, relay
        self.out_shape = jax.ShapeDtypeStruct(((N_DEV,) + buf.shape) if kind == "gather" else buf.shape, buf.dtype)
        self.spec = pl.BlockSpec(memory_space=pl.ANY)
        self.scratch = [pltpu.SemaphoreType.DMA((N_DEV - 1,)), pltpu.SemaphoreType.DMA((N_DEV - 1,)),
                        pltpu.SemaphoreType.DMA]

    def _src(self, x_ref, dest):
        return x_ref if self.kind == "gather" else x_ref.at[dest]

    def _copies(self, x_ref, out_ref, send_sems, recv_sems, local_sem):
        pos = _mesh_pos()
        me = _index(pos)
        local = pltpu.make_async_copy(self._src(x_ref, me), out_ref.at[me], local_sem)
        sends, recvs = [], []
        for k in range(1, N_DEV):
            peer = _peer(pos, k)
            sends.append(pltpu.make_async_remote_copy(
                src_ref=self._src(x_ref, _index(peer)), dst_ref=out_ref.at[me], send_sem=send_sems.at[k - 1],
                recv_sem=recv_sems.at[k - 1], device_id=peer, device_id_type=pl.DeviceIdType.MESH))
            recvs.append(pltpu.make_async_remote_copy(
                src_ref=self._src(x_ref, me), dst_ref=out_ref.at[_index(peer)], send_sem=send_sems.at[k - 1],
                recv_sem=recv_sems.at[k - 1], device_id=peer, device_id_type=pl.DeviceIdType.MESH))
        return local, sends, recvs

    def _relay_copies(self, x_ref, out_ref, send_sems, recv_sems, local_sem):
        x, y, c = _mesh_pos()
        me, sibling = (x, y, c), (x, y, 1 - c)
        chips = [(1 - x, y), (x, 1 - y), (1 - x, 1 - y)]

        def copy(k, block, to, own=False):
            return pltpu.make_async_remote_copy(
                src_ref=x_ref if own else out_ref.at[_index(block)], dst_ref=out_ref.at[_index(block)],
                send_sem=send_sems.at[k], recv_sem=recv_sems.at[k], device_id=to, device_id_type=pl.DeviceIdType.MESH)

        local = pltpu.make_async_copy(x_ref, out_ref.at[_index(me)], local_sem)
        first = [copy(0, me, sibling, own=True)] + [copy(1 + j, me, (*chip, c), own=True) for j, chip in enumerate(chips)]
        landed = [copy(1 + j, (*chip, c), me) for j, chip in enumerate(chips)]
        passed = [copy(4 + j, (*chip, c), sibling) for j, chip in enumerate(chips)]
        from_sibling = [copy(0, sibling, me)] + [copy(4 + j, (*chip, 1 - c), me) for j, chip in enumerate(chips)]
        return local, first, landed, passed, from_sibling

    def start(self, *refs):
        if self.relay:
            local, first = self._relay_copies(*refs)[:2]
            local.start()
            for cp in first:
                cp.start()
            return
        local, sends, _ = self._copies(*refs)
        local.start()
        for cp in sends:
            cp.start()

    def finish(self, *refs):
        if self.relay:
            local, first, landed, passed, from_sibling = self._relay_copies(*refs)
            for got, forward in zip(landed, passed):
                got.wait_recv()
                forward.start()
            for cp in from_sibling:
                cp.wait_recv()
            for cp in first + passed:
                cp.wait_send()
            local.wait()
            return
        local, sends, recvs = self._copies(*refs)
        for cp in recvs:
            cp.wait_recv()
        for cp in sends:
            cp.wait_send()
        local.wait()


def _hosted_call(body, *, name, grid, in_specs, out_specs, out_shape, scratch, args, semantics, exchange=None):
    if exchange is None:
        return pl.pallas_call(
            body, name=name, grid=grid, in_specs=in_specs, out_specs=out_specs, out_shape=out_shape,
            scratch_shapes=scratch, compiler_params=_params(*semantics))(*args)
    n_in, n_out, n_scr = len(in_specs), len(out_specs), len(scratch)

    def hosted(*refs):
        ins, x_ref = refs[:n_in], refs[n_in]
        outs, land_ref = refs[n_in + 1:n_in + 1 + n_out], refs[n_in + 1 + n_out]
        rest = refs[n_in + n_out + 2:]
        sems = rest[n_scr:]
        ids = [pl.program_id(a) for a in range(len(grid))]
        first, last = ids[0] == 0, ids[0] == grid[0] - 1
        for a in range(1, len(grid)):
            first, last = first & (ids[a] == 0), last & (ids[a] == grid[a] - 1)

        @pl.when(first)
        def _():
            exchange.start(x_ref, land_ref, *sems)

        body(*ins, *outs, *rest[:n_scr])

        @pl.when(last)
        def _():
            exchange.finish(x_ref, land_ref, *sems)

    return pl.pallas_call(
        hosted, name=name, grid=grid, in_specs=list(in_specs) + [exchange.spec],
        out_specs=list(out_specs) + [exchange.spec], out_shape=list(out_shape) + [exchange.out_shape],
        scratch_shapes=list(scratch) + exchange.scratch, compiler_params=_params(*(["arbitrary"] * len(grid))),
    )(*args, exchange.buf)


def _exchange_alone(exchange, name):
    def body(x_ref, out_ref, send_sems, recv_sems, local_sem):
        exchange.start(x_ref, out_ref, send_sems, recv_sems, local_sem)
        exchange.finish(x_ref, out_ref, send_sems, recv_sems, local_sem)

    return pl.pallas_call(
        body, name=name, out_shape=exchange.out_shape, in_specs=[exchange.spec], out_specs=exchange.spec,
        scratch_shapes=exchange.scratch)(exchange.buf)


def _adamw(w, g, m, v):
    m = ADAM_B1 * m + (1.0 - ADAM_B1) * g
    v = ADAM_B2 * v + (1.0 - ADAM_B2) * (g * g)
    m_hat = m / (1.0 - ADAM_B1 ** ADAM_STEP)
    v_hat = v / (1.0 - ADAM_B2 ** ADAM_STEP)
    delta = -ADAM_LR * (m_hat / (jnp.sqrt(v_hat) + ADAM_EPS) + ADAM_WD * w)
    return delta, m, v


def _sum_rows(parts, r0, rows, name, wmv=None):
    C = parts.shape[2]
    tr = max(t for t in range(16, ROWS + 1, 16) if rows % t == 0 and r0 % t == 0)

    def total(p_ref):
        g = p_ref[0].astype(F32)
        for i in range(1, N_DEV):
            g = g + p_ref[i].astype(F32)
        return g

    p_spec = pl.BlockSpec((N_DEV, tr, C), lambda i: (0, r0 // tr + i, 0))
    if wmv is None:
        def body(p_ref, g_ref):
            g_ref[...] = total(p_ref)

        return pl.pallas_call(
            body, name=name, grid=(rows // tr,), in_specs=[p_spec], out_specs=_row_spec(tr, C),
            out_shape=jax.ShapeDtypeStruct((rows, C), F32), compiler_params=_params("parallel"))(parts)

    def body(p_ref, w_ref, m_ref, v_ref, g_ref, d_ref, mo_ref, vo_ref):
        g = total(p_ref)
        g_ref[0] = g
        d_ref[0], mo_ref[0], vo_ref[0] = _adamw(w_ref[0], g, m_ref[0], v_ref[0])

    blk = pl.BlockSpec((1, tr, C), lambda i: (0, i, 0))
    return pl.pallas_call(
        body, name=name, grid=(rows // tr,), in_specs=[p_spec, blk, blk, blk], out_specs=[blk] * 4,
        out_shape=[jax.ShapeDtypeStruct((1, rows, C), F32)] * 4, compiler_params=_params("parallel"))(parts, *wmv)


def _sum_parts(parts, name):
    _, R, C = parts.shape

    def body(p_ref, g_ref):
        g = p_ref[0]
        for i in range(1, N_DEV):
            g = g + p_ref[i]
        g_ref[...] = g

    return pl.pallas_call(body, name=name, out_shape=jax.ShapeDtypeStruct((R, C), F32))(parts)


def _adamw_call(w, g, m, v, name):
    _, R, C = w.shape
    tr = min(ROWS, R)

    def body(w_ref, g_ref, m_ref, v_ref, d_ref, mo_ref, vo_ref):
        d_ref[...], mo_ref[...], vo_ref[...] = _adamw(w_ref[...], g_ref[...], m_ref[...], v_ref[...])

    blk = pl.BlockSpec((1, tr, C), lambda i: (0, i, 0))
    return pl.pallas_call(
        body, name=name, grid=(R // tr,), in_specs=[blk] * 4, out_specs=[blk] * 3,
        out_shape=[jax.ShapeDtypeStruct(w.shape, F32)] * 3, compiler_params=_params("parallel"))(w, g, m, v)


NORMS = ("mix_pre_norm", "mix_post_norm", "ca_pre_norm", "mem_norm", "ca_post_norm", "ffn_pre_norm", "ffn_post_norm")
SMALL = ("mix_pre_norm", "attn_sinks", "hgrn_lb_logits", "hgrn_out_norm", "mix_post_norm", "ca_pre_norm", "mem_norm",
         "ca_post_norm", "ffn_pre_norm", "ffn_conv_w", "ffn_conv_b", "ffn_post_norm")
SMALL_ROWS = 40
ROW_LOGITS, ROW_MISC, ROW_CONV_B, ROW_CONV_W = 7, 8, 9, 15
LANE_SINKS, LANE_LOSS = 128, 256
FF_PIECES = ((0, 1024), (1024, 2048), (2048, D_FF))


def _pack_small(norm_grads, dlogits, donw, dsinks, loss, d_cb, d_cw, name):
    def body(*refs):
        norm_refs = refs[:len(NORMS)]
        dl_ref, donw_ref, dsink_ref, loss_ref, cb_ref, cw_ref, out_ref = refs[len(NORMS):]
        out_ref[...] = jnp.zeros_like(out_ref)
        for i, ref in enumerate(norm_refs):
            out_ref[i:i + 1, :] = ref[...]
        out_ref[ROW_LOGITS:ROW_LOGITS + 1, 0:512] = dl_ref[0:1, :]
        out_ref[ROW_LOGITS:ROW_LOGITS + 1, 512:1024] = dl_ref[1:2, :]
        out_ref[ROW_MISC:ROW_MISC + 1, 0:HGRN_DIM] = donw_ref[...]
        out_ref[ROW_MISC:ROW_MISC + 1, LANE_SINKS:LANE_SINKS + ATTN_Q_HEADS] = dsink_ref[...]
        out_ref[ROW_MISC:ROW_MISC + 1, LANE_LOSS:LANE_LOSS + LANE] = loss_ref[...]
        for h in range(2):
            for j, (c0, c1) in enumerate(FF_PIECES):
                r = ROW_CONV_B + 3 * h + j
                out_ref[r:r + 1, 0:c1 - c0] = cb_ref[h, :, c0:c1]
                for t in range(3):
                    r = ROW_CONV_W + 3 * (3 * h + t) + j
                    out_ref[r:r + 1, 0:c1 - c0] = cw_ref[h, t:t + 1, c0:c1]

    return pl.pallas_call(
        body, name=name, out_shape=jax.ShapeDtypeStruct((SMALL_ROWS, 1024), F32),
    )(*norm_grads, dlogits, donw, dsinks, loss, d_cb, d_cw)


def _adamw_small(total, g_conv_w, w, m, v, name):
    n = len(SMALL)

    def body(*refs):
        t_ref, gcw_ref = refs[:2]
        w_refs, m_refs, v_refs = (dict(zip(SMALL, refs[2 + n * i:2 + n * (i + 1)])) for i in range(3))
        outs = refs[2 + 3 * n:]
        loss_ref = outs[0]
        g_refs, d_refs, mo_refs, vo_refs = (dict(zip(SMALL, outs[1 + n * i:1 + n * (i + 1)])) for i in range(4))
        loss_ref[...] = t_ref[ROW_MISC:ROW_MISC + 1, LANE_LOSS:LANE_LOSS + 1]

        def step(nm, idx, g):
            g_refs[nm][idx] = g
            d_refs[nm][idx], mo_refs[nm][idx], vo_refs[nm][idx] = _adamw(w_refs[nm][idx], g, m_refs[nm][idx], v_refs[nm][idx])

        everything = (slice(None), slice(None))
        for i, nm in enumerate(NORMS):
            step(nm, everything, t_ref[i:i + 1, :])
        step("hgrn_lb_logits", (slice(0, 1), slice(None)), t_ref[ROW_LOGITS:ROW_LOGITS + 1, 0:512])
        step("hgrn_lb_logits", (slice(1, 2), slice(None)), t_ref[ROW_LOGITS:ROW_LOGITS + 1, 512:1024])
        step("hgrn_out_norm", everything, t_ref[ROW_MISC:ROW_MISC + 1, 0:HGRN_DIM])
        step("attn_sinks", everything, t_ref[ROW_MISC:ROW_MISC + 1, LANE_SINKS:LANE_SINKS + ATTN_Q_HEADS])
        for h in range(2):
            for j, (c0, c1) in enumerate(FF_PIECES):
                r = ROW_CONV_B + 3 * h + j
                step("ffn_conv_b", (slice(None), slice(D_FF * h + c0, D_FF * h + c1)), t_ref[r:r + 1, 0:c1 - c0])
        step("ffn_conv_w", (slice(None), slice(None), slice(None)), gcw_ref[...])

    shapes = [jax.ShapeDtypeStruct(w[nm].shape, F32) for nm in SMALL]
    out = pl.pallas_call(
        body, name=name, out_shape=[jax.ShapeDtypeStruct((1, 1), F32)] + shapes * 4,
    )(total, g_conv_w, *[w[nm] for nm in SMALL], *[m[nm] for nm in SMALL], *[v[nm] for nm in SMALL])
    trees = [dict(zip(SMALL, out[1 + n * i:1 + n * (i + 1)])) for i in range(4)]
    return out[0], trees


BIG = ("w_in", "w_out", "ca_wq", "ca_wk", "ca_wv", "ca_wo", "ffn_w_up", "ffn_w_down")
BIG_FULL = {"w_in": (1024, 2816), "w_out": (1024, 1024), "ca_wq": (1024, 1024), "ca_wk": (1024, 1024),
            "ca_wv": (1024, 1024), "ca_wo": (1024, 1024), "ffn_w_up": (1024, 5632), "ffn_w_down": (2816, 1024)}
G_IN, G_MID, G_UP, G_DOWN = ("w_in",), ("w_out", "ca_wq", "ca_wk", "ca_wv", "ca_wo"), ("ffn_w_up",), ("ffn_w_down",)
GROUPS = (G_IN, G_MID, G_UP, G_DOWN)
COL_SHARDED = ("w_in", "ffn_w_up")
PACK_COLS = 1024


def _big_rows(name):
    r, c = BIG_FULL[name]
    return r * c // N_DEV // PACK_COLS


def _pack_shards(w, names):
    rows = [w[n][0].T if n in COL_SHARDED else w[n][0] for n in names]
    return (rows[0] if len(rows) == 1 else jnp.concatenate(rows, axis=0)).astype(BF16)


def _unpack_gathered(gathered, names):
    out, r0 = {}, 0
    for n in names:
        rows = _big_rows(n)
        out[n] = gathered[:, r0:r0 + rows].reshape(N_DEV * rows, PACK_COLS)
        r0 += rows
    return out


def _pack_full_grads(grads, names):
    parts = [grads[n].reshape(N_DEV, _big_rows(n), PACK_COLS) for n in names]
    return parts[0] if len(parts) == 1 else jnp.concatenate(parts, axis=1)


def kernel(x, mem, mix_pre_norm, w_in, attn_sinks, hgrn_lb_logits, hgrn_out_norm, w_out, mix_post_norm, ca_pre_norm, mem_norm, ca_wq, ca_wk, ca_wv, ca_wo, ca_post_norm, ffn_pre_norm, ffn_w_up, ffn_conv_w, ffn_conv_b, ffn_w_down, ffn_post_norm, loss_target, m_mix_pre_norm, m_w_in, m_attn_sinks, m_hgrn_lb_logits, m_hgrn_out_norm, m_w_out, m_mix_post_norm, m_ca_pre_norm, m_mem_norm, m_ca_wq, m_ca_wk, m_ca_wv, m_ca_wo, m_ca_post_norm, m_ffn_pre_norm, m_ffn_w_up, m_ffn_conv_w, m_ffn_conv_b, m_ffn_w_down, m_ffn_post_norm, v_mix_pre_norm, v_w_in, v_attn_sinks, v_hgrn_lb_logits, v_hgrn_out_norm, v_w_out, v_mix_post_norm, v_ca_pre_norm, v_mem_norm, v_ca_wq, v_ca_wk, v_ca_wv, v_ca_wo, v_ca_post_norm, v_ffn_pre_norm, v_ffn_w_up, v_ffn_conv_w, v_ffn_conv_b, v_ffn_w_down, v_ffn_post_norm):
    names = ["mix_pre_norm", "w_in", "attn_sinks", "hgrn_lb_logits", "hgrn_out_norm", "w_out", "mix_post_norm",
             "ca_pre_norm", "mem_norm", "ca_wq", "ca_wk", "ca_wv", "ca_wo", "ca_post_norm", "ffn_pre_norm",
             "ffn_w_up", "ffn_conv_w", "ffn_conv_b", "ffn_w_down", "ffn_post_norm"]
    w_all = dict(zip(names, [mix_pre_norm, w_in, attn_sinks, hgrn_lb_logits, hgrn_out_norm, w_out, mix_post_norm,
                             ca_pre_norm, mem_norm, ca_wq, ca_wk, ca_wv, ca_wo, ca_post_norm, ffn_pre_norm,
                             ffn_w_up, ffn_conv_w, ffn_conv_b, ffn_w_down, ffn_post_norm]))
    m_all = dict(zip(names, [m_mix_pre_norm, m_w_in, m_attn_sinks, m_hgrn_lb_logits, m_hgrn_out_norm, m_w_out,
                             m_mix_post_norm, m_ca_pre_norm, m_mem_norm, m_ca_wq, m_ca_wk, m_ca_wv, m_ca_wo,
                             m_ca_post_norm, m_ffn_pre_norm, m_ffn_w_up, m_ffn_conv_w, m_ffn_conv_b, m_ffn_w_down,
                             m_ffn_post_norm]))
    v_all = dict(zip(names, [v_mix_pre_norm, v_w_in, v_attn_sinks, v_hgrn_lb_logits, v_hgrn_out_norm, v_w_out,
                             v_mix_post_norm, v_ca_pre_norm, v_mem_norm, v_ca_wq, v_ca_wk, v_ca_wv, v_ca_wo,
                             v_ca_post_norm, v_ffn_pre_norm, v_ffn_w_up, v_ffn_conv_w, v_ffn_conv_b, v_ffn_w_down,
                             v_ffn_post_norm]))
    dev = _index(_mesh_pos())

    w_packs = {grp: _pack_shards(w_all, grp) for grp in GROUPS}
    shard_w = D_FF * 2 // N_DEV
    conv_w_rows = _exchange_alone(_Exchange("gather", ffn_conv_w[0]), "gather_conv_w")
    conv_w_full = conv_w_rows.transpose(1, 0, 2).reshape(3, 2 * D_FF)

    received, small_pack, grad_x = _local_step(
        x[0], mem[0], loss_target[0], w_packs, conv_w_full,
        {n: w_all[n] for n in NORMS}, attn_sinks, hgrn_lb_logits, hgrn_out_norm, ffn_conv_b)

    total = _sum_parts(_exchange_alone(_Exchange("gather", small_pack), "gather_small"), "sum_small")
    cw = total[ROW_CONV_W:ROW_CONV_W + 18].reshape(2, 3, 3 * PACK_COLS)[:, :, :D_FF]
    cw = cw.transpose(1, 0, 2).reshape(3, 2 * D_FF)
    g_conv_w = lax.dynamic_slice_in_dim(cw, dev * shard_w, shard_w, axis=1)[None]
    loss, (out_g, out_d, out_m, out_v) = _adamw_small(total, g_conv_w, w_all, m_all, v_all, "adamw_small")

    for grp in GROUPS:
        r0 = 0
        for n in grp:
            rows = _big_rows(n)
            if n in COL_SHARDED:
                g = _sum_rows(received[grp], r0, rows, "sum_" + n).T[None]
                d, mo, vo = _adamw_call(w_all[n], g, m_all[n], v_all[n], "adamw_" + n)
            else:
                g, d, mo, vo = _sum_rows(received[grp], r0, rows, "adamw_" + n, wmv=(w_all[n], m_all[n], v_all[n]))
            out_g[n], out_d[n], out_m[n], out_v[n] = g, d, mo, vo
            r0 += rows

    return (loss[0, 0], grad_x[None], *[out_g[n] for n in names], *[out_d[n] for n in names],
            *[out_m[n] for n in names], *[out_v[n] for n in names])


def _local_step(x, mem, target, w_packs, conv_w, norms, sinks, lb_logits, out_norm, conv_b):
    g1, g2, g3 = norms["mix_pre_norm"], norms["mix_post_norm"], norms["ca_pre_norm"]
    g4, g5, g6, g7 = norms["mem_norm"], norms["ca_post_norm"], norms["ffn_pre_norm"], norms["ffn_post_norm"]

    h1, gathered = _norm_fwd(x, g1, "mix_norm", exchange=_Exchange("gather", w_packs[G_IN], relay=True))
    w_in_t = _unpack_gathered(gathered, G_IN)["w_in"]
    up_shard = w_packs[G_UP]
    up_rows = up_shard.shape[0]
    up_cuts = (0, up_rows // 2, 3 * up_rows // 4, up_rows)
    up_parts = [up_shard[a:b] for a, b in zip(up_cuts[:-1], up_cuts[1:])]
    z, up_0 = _mm(h1, w_in_t, mode="nt", out_dtype=BF16, name="in_proj", tn=2816,
                  exchange=_Exchange("gather", up_parts[0]))
    attn, lse, gathered = _swa_fwd(z, sinks, "swa_fwd", exchange=_Exchange("gather", w_packs[G_DOWN]))
    w_down = _unpack_gathered(gathered, G_DOWN)["ffn_w_down"]
    lb = _lower_bound(lb_logits, "lower_bound")
    rec, o_rec, states, scores, gathered = _hgrn_fwd(
        z, lb, out_norm, "hgrn_fwd", exchange=_Exchange("gather", w_packs[G_MID]))
    w_out, wq, wk, wv, wo = (_unpack_gathered(gathered, G_MID)[n] for n in G_MID)
    cat = jnp.concatenate([attn, rec], axis=1)
    x1, h2, mix, up_1 = _mm(cat, w_out, mode="nn", out_dtype=BF16, name="out_proj",
                            exchange=_Exchange("gather", up_parts[1]), epilogue=_post_pre(x, g2, g3))
    mem_n = _norm_fwd(mem, g4, "mem_norm")
    q = _mm(h2, wq, mode="nn", out_dtype=BF16, name="ca_q", tm=2048)
    k = _mm(mem_n, wk, mode="nn", out_dtype=BF16, name="ca_k")
    v = _mm(mem_n, wv, mode="nn", out_dtype=BF16, name="ca_v")
    oc = _ca_fwd(q, k, v, "ca_fwd")
    x2, h3, c, up_2 = _mm(oc, wo, mode="nn", out_dtype=BF16, name="ca_o",
                          exchange=_Exchange("gather", up_parts[2]), epilogue=_post_pre(x1, g5, g6))
    w_up_t = jnp.concatenate([up_0, up_1, up_2], axis=1).reshape(-1, PACK_COLS)
    u = _mm(h3, w_up_t, mode="nt", out_dtype=F32, name="ffn_up", tn=2816, split_out=True)
    a = _glu_fwd(u, conv_w, conv_b, "glu_fwd")
    dx3, dy, loss_row, dg7 = _mm(a, w_down, mode="nn", out_dtype=BF16, name="ffn_down", tm=512, tk=2816,
                                 epilogue=_final(x2, target, g7))
    loss = loss_row[:, :LANE]

    da = _mm(dy, w_down, mode="nt", out_dtype=F32, name="ffn_down_dx", tn=2816)
    d_w_down = _mm(a, dy, mode="tn", out_dtype=BF16, name="ffn_down_dw", tm=2816, tk=1024)
    dc, d_cb, d_cw, got_down = _glu_bwd(
        u, conv_w, conv_b, da, "glu_bwd",
        exchange=_Exchange("scatter", _pack_full_grads({"ffn_w_down": d_w_down}, G_DOWN)))
    du = _conv_bwd(dc, conv_w, "conv_bwd")
    d_w_up_t = _mm(du, h3, mode="tn", out_dtype=BF16, name="ffn_up_dw", tm=2816, tk=1024, split_a=True)
    dx2, dcv, dg6, dg5, got_up = _mm(
        du, w_up_t, mode="nn", out_dtype=BF16, name="ffn_up_dx", tm=1024, tk=1408, split_a=True,
        exchange=_Exchange("scatter", _pack_full_grads({"ffn_w_up": d_w_up_t}, G_UP)),
        epilogue=_norm_bwd2(dx3, x2, c, g6, g5))
    doc = _mm(dcv, wo, mode="nt", out_dtype=BF16, name="ca_o_dx", tm=2048)
    d_wo = _mm(oc, dcv, mode="tn", out_dtype=BF16, name="ca_o_dw", tm=1024, tk=2048)
    dq, dk, dv = _ca_bwd(q, k, v, doc, "ca_bwd")
    d_wq = _mm(h2, dq, mode="tn", out_dtype=BF16, name="ca_q_dw", tm=1024, tk=2048)
    dx1, dmix, dg3, dg2 = _mm(dq, wq, mode="nt", out_dtype=BF16, name="ca_q_dx",
                              epilogue=_norm_bwd2(dx2, x1, mix, g3, g2))
    d_wk = _mm(mem_n, dk, mode="tn", out_dtype=BF16, name="ca_k_dw", tm=1024)
    d_wv = _mm(mem_n, dv, mode="tn", out_dtype=BF16, name="ca_v_dw", tm=1024)
    dmem_k = _mm(dk, wk, mode="nt", out_dtype=F32, name="ca_k_dx")
    dmem_v = _mm(dv, wv, mode="nt", out_dtype=F32, name="ca_v_dx")
    dg4 = _gain_bwd(mem, dmem_k, dmem_v, "mem_norm_bwd")
    dcat = _mm(dmix, w_out, mode="nt", out_dtype=BF16, name="out_proj_dx", tm=2048)
    d_w_out = _mm(cat, dmix, mode="tn", out_dtype=BF16, name="out_proj_dw", tm=1024, tk=2048)
    mid = {"w_out": d_w_out, "ca_wq": d_wq, "ca_wk": d_wk, "ca_wv": d_wv, "ca_wo": d_wo}
    dqr, dfr, dir_, dgr, dlb, donw, got_mid = _hgrn_bwd(
        z, lb, out_norm, o_rec, states, scores, dcat, "hgrn_bwd",
        exchange=_Exchange("scatter", _pack_full_grads(mid, G_MID)))
    dq_a, dka, dkb, dva, dvb, dsinks = _swa_bwd(z, sinks, dcat, lse, "swa_bwd")
    dz = _assemble_dz(dq_a, dka, dkb, dva, dvb, dqr, dfr, dir_, dgr, "assemble_dz")
    d_w_in_t = _mm(dz, h1, mode="tn", out_dtype=BF16, name="in_proj_dw", tm=2816, tk=1024)
    dx, dg1, got_in = _mm(dz, w_in_t, mode="nn", out_dtype=BF16, name="in_proj_dx", tm=512, tk=2816,
                          exchange=_Exchange("scatter", _pack_full_grads({"w_in": d_w_in_t}, G_IN)),
                          epilogue=_norm_bwd1(dx1, x, g1))

    small_pack = _pack_small(
        (dg1, dg2, dg3, dg4, dg5, dg6, dg7), _lower_bound_bwd(lb, dlb, "lower_bound_bwd"), donw, dsinks, loss,
        d_cb, d_cw, "pack_small")
    return {G_IN: got_in, G_MID: got_mid, G_UP: got_up, G_DOWN: got_down}, small_pack, dx
```

```python
import jax
import jax.numpy as jnp
from jax import lax
from jax.experimental import pallas as pl
from jax.experimental.pallas import tpu as pltpu

F32 = jnp.float32
BF16 = jnp.bfloat16
EPS = 1e-6
N_DEV = 8
MESH_AXES = ("x", "y", "c")

ATTN_HEAD_DIM = 64
ATTN_Q_HEADS = 8
ATTN_KV_HEADS = 2
ATTN_BLOCK = 128
HGRN_HEADS = 4
HGRN_DIM = 128
HGRN_CHUNK = 64
HGRN_PAIR = 4
Z_Q, Z_F, Z_I, Z_G = 768, 1280, 1792, 2304
HGRN_LEVELS = (32, 16, 8, 4, 2, 1)
CA_HEADS = 4
CA_HEAD_DIM = 256
D_FF = 2816

ADAM_LR = 0.001
ADAM_B1 = 0.9
ADAM_B2 = 0.999
ADAM_EPS = 1e-08
ADAM_WD = 0.01
ADAM_STEP = 10

VMEM_LIMIT = 58 << 20
EPILOGUE_ROWS = 256
LANE = 128

NT = (((1,), (1,)), ((), ()))
TN = (((0,), (0,)), ((), ()))


def _params(*sem):
    return pltpu.CompilerParams(dimension_semantics=sem, vmem_limit_bytes=VMEM_LIMIT)


def _tile(n, cap):
    if n <= cap:
        return n
    best = 0
    for t in range(LANE, cap + 1, LANE):
        if n % t == 0:
            best = t
    assert best, (n, cap)
    return best


def _dot(a, b, dims=None):
    if dims is None:
        return jnp.dot(a, b, preferred_element_type=F32)
    return lax.dot_general(a, b, dims, preferred_element_type=F32)


def _bf(x):
    return x.astype(BF16)


def _sigmoid(x):
    return 1.0 / (1.0 + jnp.exp(-x))


def _rms(x):
    r = lax.rsqrt(jnp.mean(x * x, axis=-1, keepdims=True) + EPS)
    return x * r, r


def _rms_bwd(dxh, xh, r):
    return r * (dxh - xh * jnp.mean(dxh * xh, axis=-1, keepdims=True))


def _mm(a, b, *, mode, out_dtype, name, tm=1024, tn=1024, tk=1024, split_a=False, split_b=False, split_out=False,
        exchange=None, epilogue=None):
    def dims(arr, split):
        if split:
            return arr.shape[1], 2 * arr.shape[2]
        return arr.shape

    ar, ac = dims(a, split_a)
    br, bc = dims(b, split_b)
    if mode == "nn":
        M, K, N = ar, ac, bc
        assert br == K
    elif mode == "nt":
        M, K, N = ar, ac, br
        assert bc == K
    else:
        K, M, N = ar, ac, bc
        assert br == K
    a_cols_half = ac // 2 if split_a else None
    b_cols_half = bc // 2 if split_b else None
    tm = _tile(M, tm)
    tn = _tile((N // 2) if (split_out or (split_b and mode != "nt")) else N, tn)
    tk = _tile((K // 2) if ((split_a and mode != "tn") or (split_b and mode == "nt")) else K, tk)
    if split_a and mode == "tn":
        tm = _tile(M // 2, tm)
    gm, gn, gk = M // tm, N // tn, K // tk
    a_bytes, b_bytes = a.size * a.dtype.itemsize, b.size * b.dtype.itemsize
    rows_outer = gk > 1 or a_bytes + gm * b_bytes <= gn * a_bytes + b_bytes
    grid = (gm, gn, gk) if rows_outer else (gn, gm, gk)

    def spec(split, half, blk, rc):
        def imap(p, q, k):
            r, c = rc(*((p, q) if rows_outer else (q, p)), k)
            if not split:
                return (r, c)
            per_half = half // blk[1]
            return (c // per_half, r, c % per_half)

        return pl.BlockSpec(((None,) + blk) if split else blk, imap)

    if mode == "nn":
        a_spec = spec(split_a, a_cols_half, (tm, tk), lambda i, j, k: (i, k))
        b_spec = spec(split_b, b_cols_half, (tk, tn), lambda i, j, k: (k, j))
        dn = None
    elif mode == "nt":
        a_spec = spec(split_a, a_cols_half, (tm, tk), lambda i, j, k: (i, k))
        b_spec = spec(split_b, b_cols_half, (tn, tk), lambda i, j, k: (j, k))
        dn = NT
    else:
        a_spec = spec(split_a, a_cols_half, (tk, tm), lambda i, j, k: (k, i))
        b_spec = spec(split_b, b_cols_half, (tk, tn), lambda i, j, k: (k, j))
        dn = TN
    o_spec = spec(split_out, N // 2 if split_out else None, (tm, tn), lambda i, j, k: (i, j))
    out_shape = (2, M, N // 2) if split_out else (M, N)

    in_specs, out_specs, args = [a_spec, b_spec], [o_spec], (a, b)
    out_shapes = [jax.ShapeDtypeStruct(out_shape, out_dtype)]
    semantics = ("parallel", "parallel", "arbitrary")

    def store(result, extra, outs):
        outs[0][...] = result[...].astype(outs[0].dtype)

    if epilogue is not None:
        assert gn == 1 and not split_out
        n_vec = epilogue.n_out_vecs
        row = pl.BlockSpec((tm, N), lambda p, q, k: ((p if rows_outer else q), 0))
        vec = pl.BlockSpec((1, N), lambda p, q, k: (0, 0))
        in_specs += [row] * len(epilogue.rows) + [vec] * len(epilogue.vecs)
        args += tuple(epilogue.rows) + tuple(epilogue.vecs)
        out_specs = [row] * len(epilogue.out_rows) + [vec] * n_vec
        out_shapes = ([jax.ShapeDtypeStruct((M, N), dt) for dt in epilogue.out_rows]
                      + [jax.ShapeDtypeStruct((1, N), F32)] * n_vec)
        semantics = ("arbitrary",) * 3

        def store(result, extra, outs):
            n_rows, n_out_rows, sub = len(epilogue.rows), len(epilogue.out_rows), min(EPILOGUE_ROWS, tm)
            for r in range(0, tm, sub):
                rows = pl.ds(r, sub)
                epilogue.fn(result[r:r + sub], *[ref.at[rows] for ref in extra[:n_rows]], *extra[n_rows:],
                            *[ref.at[rows] for ref in outs[:n_out_rows]], *outs[n_out_rows:])

    n_extra = len(in_specs) - 2
    n_out = len(out_specs)

    def body(a_ref, b_ref, *refs):
        extra, outs, scratch_refs = refs[:n_extra], refs[n_extra:n_extra + n_out], refs[n_extra + n_out:]
        k = pl.program_id(2)
        if epilogue is not None:
            @pl.when((pl.program_id(0) == 0) & (pl.program_id(1) == 0) & (k == 0))
            def _():
                for ref in outs[n_out - epilogue.n_out_vecs:]:
                    ref[...] = jnp.zeros_like(ref)

        if gk == 1:
            store(_dot(_bf(a_ref[...]), _bf(b_ref[...]), dn), extra, outs)
            return
        acc_ref = scratch_refs[0]

        @pl.when(k == 0)
        def _():
            acc_ref[...] = jnp.zeros_like(acc_ref)

        acc_ref[...] += _dot(_bf(a_ref[...]), _bf(b_ref[...]), dn)

        @pl.when(k == gk - 1)
        def _():
            store(acc_ref, extra, outs)

    out = _hosted_call(
        body, name=name, grid=grid, in_specs=in_specs, out_specs=out_specs, out_shape=out_shapes,
        scratch=[] if gk == 1 else [pltpu.VMEM((tm, tn), F32)], args=args, semantics=semantics, exchange=exchange)
    return out[0] if (exchange is None and epilogue is None) else out


ROWS = 512


def _row_spec(tr, cols):
    return pl.BlockSpec((tr, cols), lambda i: (i, 0))


def _vec_spec(cols):
    return pl.BlockSpec((1, cols), lambda i: (0, 0))


def _norm_fwd(x, g, name, exchange=None):
    T, Dm = x.shape
    tr = min(ROWS, T)

    def body(x_ref, g_ref, h_ref):
        xh, _ = _rms(x_ref[...])
        h_ref[...] = (xh * g_ref[...]).astype(h_ref.dtype)

    out = _hosted_call(
        body, name=name, grid=(T // tr,), in_specs=[_row_spec(tr, Dm), _vec_spec(Dm)], out_specs=[_row_spec(tr, Dm)],
        out_shape=[jax.ShapeDtypeStruct((T, Dm), BF16)], scratch=[], args=(x, g), semantics=("parallel",),
        exchange=exchange)
    return out[0] if exchange is None else out


def _post_pre(x, g_post, g_pre):
    def fn(m, x_ref, gp_ref, gn_ref, xo_ref, h_ref, m_ref):
        mh, _ = _rms(m)
        xn = x_ref[...] + mh * gp_ref[...]
        xo_ref[...] = xn
        xh, _ = _rms(xn)
        h_ref[...] = (xh * gn_ref[...]).astype(h_ref.dtype)
        m_ref[...] = m.astype(m_ref.dtype)

    return _RowEpilogue(fn, [x], [g_post, g_pre], [F32, BF16, BF16], 0)


def _final(x2, target, g_post):
    def fn(y, x_ref, t_ref, g_ref, dx_ref, dy_ref, loss_ref, dg_ref):
        g = g_ref[...]
        yh, r = _rms(y)
        d = x_ref[...] + yh * g - t_ref[...]
        loss_ref[...] += 0.5 * jnp.sum(jnp.mean(d * d, axis=-1, keepdims=True))
        dx = d * (1.0 / d.shape[-1])
        dx_ref[...] = dx
        dy_ref[...] = _rms_bwd(dx * g, yh, r).astype(dy_ref.dtype)
        dg_ref[...] += jnp.sum(dx * yh, axis=0, keepdims=True)

    return _RowEpilogue(fn, [x2, target], [g_post], [F32, BF16], 2)


class _RowEpilogue:
    def __init__(self, fn, rows, vecs, out_rows, n_out_vecs):
        self.fn, self.rows, self.vecs, self.out_rows, self.n_out_vecs = fn, rows, vecs, out_rows, n_out_vecs


def _norm_bwd2(dx_cur, x_prev, m_prev, g_pre, g_post):
    def fn(dh, dx_ref, x_ref, m_ref, gn_ref, gp_ref, dxo_ref, dm_ref, dgn_ref, dgp_ref):
        xh, r = _rms(x_ref[...])
        dx = dx_ref[...] + _rms_bwd(dh * gn_ref[...], xh, r)
        dxo_ref[...] = dx
        dgn_ref[...] += jnp.sum(dh * xh, axis=0, keepdims=True)
        mh, rm = _rms(m_ref[...].astype(F32))
        dm_ref[...] = _rms_bwd(dx * gp_ref[...], mh, rm).astype(dm_ref.dtype)
        dgp_ref[...] += jnp.sum(dx * mh, axis=0, keepdims=True)

    return _RowEpilogue(fn, [dx_cur, x_prev, m_prev], [g_pre, g_post], [F32, BF16], 2)


def _norm_bwd1(dx_cur, x_prev, g_pre):
    def fn(dh, dx_ref, x_ref, gn_ref, dxo_ref, dgn_ref):
        xh, r = _rms(x_ref[...])
        dxo_ref[...] = dx_ref[...] + _rms_bwd(dh * gn_ref[...], xh, r)
        dgn_ref[...] += jnp.sum(dh * xh, axis=0, keepdims=True)

    return _RowEpilogue(fn, [dx_cur, x_prev], [g_pre], [F32], 1)


def _gain_bwd(x, dh_a, dh_b, name):
    T, Dm = x.shape

    def body(x_ref, a_ref, b_ref, dg_ref):
        xh, _ = _rms(x_ref[...])
        dg_ref[...] = jnp.sum((a_ref[...] + b_ref[...]) * xh, axis=0, keepdims=True)

    return pl.pallas_call(
        body, name=name, grid=(1,), in_specs=[_row_spec(T, Dm)] * 3, out_specs=_vec_spec(Dm),
        out_shape=jax.ShapeDtypeStruct((1, Dm), F32), compiler_params=_params("arbitrary"),
    )(x, dh_a, dh_b)


ATTN_GROUP = ATTN_Q_HEADS // ATTN_KV_HEADS
ASSEMBLE_ROWS = 1024


def _swa_mask(n):
    rows = ATTN_GROUP * ATTN_BLOCK
    row = lax.broadcasted_iota(jnp.int32, (rows, 2 * ATTN_BLOCK), 0) & (ATTN_BLOCK - 1)
    col = lax.broadcasted_iota(jnp.int32, (rows, 2 * ATTN_BLOCK), 1)
    diff = row + ATTN_BLOCK - col
    return (diff >= 0) & (diff < ATTN_BLOCK) & ((col >= ATTN_BLOCK) | (n > 0))


def _swa_rows(ref, hk, dtype):
    hd = ATTN_HEAD_DIM
    return jnp.concatenate(
        [ref[:, hd * (hk * ATTN_GROUP + g):hd * (hk * ATTN_GROUP + g + 1)].astype(dtype) for g in range(ATTN_GROUP)],
        axis=0)


def _swa_per_row(vals):
    seg = lax.broadcasted_iota(jnp.int32, (ATTN_GROUP * ATTN_BLOCK, 1), 0) // ATTN_BLOCK
    col = jnp.zeros((ATTN_GROUP * ATTN_BLOCK, 1), F32)
    for g, val in enumerate(vals):
        col = jnp.where(seg == g, val, col)
    return col


def _swa_specs():
    blk = ATTN_BLOCK
    prev = lambda n: jnp.maximum(n - 1, 0)
    return [
        pl.BlockSpec(memory_space=pltpu.SMEM),
        pl.BlockSpec((blk, 512), lambda n: (n, 0)),
        pl.BlockSpec((blk, 128), lambda n: (prev(n), 4)),
        pl.BlockSpec((blk, 128), lambda n: (n, 4)),
        pl.BlockSpec((blk, 128), lambda n: (prev(n), 5)),
        pl.BlockSpec((blk, 128), lambda n: (n, 5)),
    ]


def _swa_fwd(z, sinks, name, exchange=None):
    T = z.shape[0]
    blk, hd = ATTN_BLOCK, ATTN_HEAD_DIM
    scale = hd ** -0.5

    def body(sink_ref, q_ref, kp_ref, kc_ref, vp_ref, vc_ref, o_ref, lse_ref):
        allowed = _swa_mask(pl.program_id(0))
        hks = range(ATTN_KV_HEADS)
        kss = [slice(hd * hk, hd * hk + hd) for hk in hks]
        k = [_bf(jnp.concatenate([kp_ref[:, ks], kc_ref[:, ks]], axis=0)) for ks in kss]
        v = [_bf(jnp.concatenate([vp_ref[:, ks], vc_ref[:, ks]], axis=0)) for ks in kss]
        s = [jnp.where(allowed, _dot(_swa_rows(q_ref, hk, BF16), k[hk], NT) * scale, -1e30) for hk in hks]
        sink = [_swa_per_row([sink_ref[0, hk * ATTN_GROUP + g] for g in range(ATTN_GROUP)]) for hk in hks]
        m = [jnp.maximum(jnp.max(s[hk], axis=-1, keepdims=True), sink[hk]) for hk in hks]
        p = [jnp.exp(s[hk] - m[hk]) for hk in hks]
        l = [jnp.sum(p[hk], axis=-1, keepdims=True) + jnp.exp(sink[hk] - m[hk]) for hk in hks]
        o = [_dot(_bf(p[hk] / l[hk]), v[hk]).astype(o_ref.dtype) for hk in hks]
        for hk in hks:
            lse = m[hk] + jnp.log(l[hk])
            for g in range(ATTN_GROUP):
                h = hk * ATTN_GROUP + g
                o_ref[:, hd * h:hd * (h + 1)] = o[hk][blk * g:blk * (g + 1)]
                lse_ref[:, h:h + 1] = lse[blk * g:blk * (g + 1)]

    return _hosted_call(
        body, name=name, grid=(T // blk,), in_specs=_swa_specs(),
        out_specs=[pl.BlockSpec((blk, 512), lambda n: (n, 0)), pl.BlockSpec((blk, ATTN_Q_HEADS), lambda n: (n, 0))],
        out_shape=[jax.ShapeDtypeStruct((T, 512), BF16), jax.ShapeDtypeStruct((T, ATTN_Q_HEADS), F32)],
        scratch=[], args=(sinks, z, z, z, z, z), semantics=("parallel",), exchange=exchange)


def _swa_bwd(z, sinks, dcat, lse, name):
    T = z.shape[0]
    blk, hd = ATTN_BLOCK, ATTN_HEAD_DIM
    scale = hd ** -0.5
    group = ATTN_Q_HEADS // ATTN_KV_HEADS

    def body(sink_ref, q_ref, kp_ref, kc_ref, vp_ref, vc_ref, do_ref, lse_ref,
             dq_ref, dka_ref, dkb_ref, dva_ref, dvb_ref, dsink_ref):
        @pl.when(pl.program_id(0) == 0)
        def _():
            dsink_ref[...] = jnp.zeros_like(dsink_ref)

        allowed = _swa_mask(pl.program_id(0))
        lane = lax.broadcasted_iota(jnp.int32, (1, ATTN_Q_HEADS), 1)
        dsink = jnp.zeros((1, ATTN_Q_HEADS), F32)
        hks = range(ATTN_KV_HEADS)
        kss = [slice(hd * hk, hd * hk + hd) for hk in hks]
        k = [_bf(jnp.concatenate([kp_ref[:, ks], kc_ref[:, ks]], axis=0)) for ks in kss]
        v = [_bf(jnp.concatenate([vp_ref[:, ks], vc_ref[:, ks]], axis=0)) for ks in kss]
        qs = [_swa_rows(q_ref, hk, BF16) for hk in hks]
        dos = [_swa_rows(do_ref, hk, BF16) for hk in hks]
        lse = [jnp.concatenate([lse_ref[:, hk * group + g:hk * group + g + 1] for g in range(group)], axis=0)
               for hk in hks]
        s = [_dot(qs[hk], k[hk], NT) * scale for hk in hks]
        dp = [_dot(dos[hk], v[hk], NT) for hk in hks]
        p = [jnp.where(allowed, jnp.exp(jnp.where(allowed, s[hk], -1e30) - lse[hk]), 0.0) for hk in hks]
        delta = [jnp.sum(p[hk] * dp[hk], axis=-1, keepdims=True) for hk in hks]
        ds = [_bf(p[hk] * (dp[hk] - delta[hk]) * scale) for hk in hks]
        dq = [_dot(ds[hk], k[hk]).astype(dq_ref.dtype) for hk in hks]
        dk = [_dot(ds[hk], qs[hk], TN) for hk in hks]
        dv = [_dot(_bf(p[hk]), dos[hk], TN) for hk in hks]
        for hk in hks:
            sink = _swa_per_row([sink_ref[0, hk * group + g] for g in range(group)])
            sink_part = jnp.exp(sink - lse[hk]) * delta[hk]
            for g in range(group):
                h = hk * group + g
                dq_ref[:, hd * h:hd * (h + 1)] = dq[hk][blk * g:blk * (g + 1)]
                dsink = dsink + jnp.where(lane == h, -jnp.sum(sink_part[blk * g:blk * (g + 1)]), 0.0)
            dkb_ref[:, kss[hk]] = dk[hk][:blk]
            dka_ref[:, kss[hk]] = dk[hk][blk:]
            dvb_ref[:, kss[hk]] = dv[hk][:blk]
            dva_ref[:, kss[hk]] = dv[hk][blk:]
        dsink_ref[...] += dsink

    kv_out = pl.BlockSpec((blk, 128), lambda n: (n, 0))
    return pl.pallas_call(
        body, name=name, grid=(T // blk,),
        in_specs=_swa_specs() + [pl.BlockSpec((blk, 512), lambda n: (n, 0)),
                                 pl.BlockSpec((blk, ATTN_Q_HEADS), lambda n: (n, 0))],
        out_specs=[pl.BlockSpec((blk, 512), lambda n: (n, 0)), kv_out, kv_out, kv_out, kv_out,
                   pl.BlockSpec((1, ATTN_Q_HEADS), lambda n: (0, 0))],
        out_shape=[jax.ShapeDtypeStruct((T, 512), BF16)] + [jax.ShapeDtypeStruct((T, 128), F32)] * 4
        + [jax.ShapeDtypeStruct((1, ATTN_Q_HEADS), F32)],
        compiler_params=_params("arbitrary"),
    )(sinks, z, z, z, z, z, dcat, lse)


def _assemble_dz(dq_a, dka, dkb, dva, dvb, dqr, dfr, dir_, dgr, name):
    T = dq_a.shape[0]
    blk = ATTN_BLOCK
    rows = min(ASSEMBLE_ROWS, T)
    nb, per = T // rows, rows // blk

    def body(dq_ref, dka_ref, dkb_ref, dkn_ref, dva_ref, dvb_ref, dvn_ref, dqr_ref, dfr_ref, dir_ref, dgr_ref, o_ref):
        has_next = pl.program_id(0) < nb - 1

        def with_next(a_ref, b_ref, n_ref):
            after = jnp.where(has_next, n_ref[...], 0.0)
            shifted = after if per == 1 else jnp.concatenate([b_ref[blk:, :], after], axis=0)
            return (a_ref[...] + shifted).astype(o_ref.dtype)

        o_ref[:, 0:512] = dq_ref[...]
        o_ref[:, 512:640] = with_next(dka_ref, dkb_ref, dkn_ref)
        o_ref[:, 640:768] = with_next(dva_ref, dvb_ref, dvn_ref)
        o_ref[:, 768:1280] = dqr_ref[...]
        o_ref[:, 1280:1792] = dfr_ref[...]
        o_ref[:, 1792:2304] = dir_ref[...]
        o_ref[:, 2304:2816] = dgr_ref[...]

    cur = lambda w: pl.BlockSpec((rows, w), lambda n: (n, 0))
    nxt = pl.BlockSpec((blk, 128), lambda n: (jnp.minimum((n + 1) * per, T // blk - 1), 0))
    return pl.pallas_call(
        body, name=name, grid=(nb,),
        in_specs=[cur(512), cur(128), cur(128), nxt, cur(128), cur(128), nxt, cur(512), cur(512), cur(512), cur(512)],
        out_specs=pl.BlockSpec((rows, 2816), lambda n: (n, 0)),
        out_shape=jax.ShapeDtypeStruct((T, 2816), BF16), compiler_params=_params("parallel"),
    )(dq_a, dka, dkb, dkb, dva, dvb, dvb, dqr, dfr, dir_, dgr)


HGRN_ROWS = 512


def _hgrn_consts():
    c = HGRN_CHUNK
    r = lax.broadcasted_iota(jnp.int32, (c, c), 0)
    s = lax.broadcasted_iota(jnp.int32, (c, c), 1)
    rcol = lax.broadcasted_iota(jnp.int32, (c, 1), 0)
    same_block, upper = [], []
    for m in HGRN_LEVELS:
        same_block.append((r & ~(2 * m - 1)) == (s & ~(2 * m - 1)))
        upper.append((rcol & (2 * m - 1)) >= m)
    cum_mat = jnp.where(s <= r, 1.0, 0.0).astype(BF16)
    rev_mat = jnp.where(s >= r, 1.0, 0.0).astype(BF16)
    return cum_mat, rev_mat, r == s, same_block, upper, rcol & 3, s == r - 1


def _hgrn_level_decay(g, b, m, pos4):
    c = HGRN_CHUNK
    if m == 1:
        return jnp.exp(jnp.where((pos4 & 1) == 1, g, 0.0))
    if m == 2:
        after, before = pltpu.roll(g, c - 1, 0), pltpu.roll(g, 1, 0)
        return jnp.exp(jnp.where(pos4 == 0, after, jnp.where(pos4 == 1, 0.0, jnp.where(pos4 == 2, g, g + before))))
    b3 = b.reshape(c // (2 * m), 2 * m, HGRN_DIM)
    bref = jnp.broadcast_to(b3[:, m - 1:m, :], b3.shape).reshape(c, HGRN_DIM)
    return jnp.exp(-jnp.abs(b - bref))


def _split3(x):
    hi = _bf(x)
    r1 = x - hi.astype(F32)
    mid = _bf(r1)
    lo = _bf(r1 - mid.astype(F32))
    return jnp.concatenate([hi, mid, lo], axis=1)


def _dot_hilo(a, b):
    r, c = a.shape[0], b.shape[1]
    a_hi, b_hi = _bf(a), _bf(b)
    a2 = jnp.concatenate([a_hi, _bf(a - a_hi.astype(F32))], axis=0)
    b2 = jnp.concatenate([b_hi, _bf(b - b_hi.astype(F32))], axis=1)
    y = _dot(a2, b2)
    return y[:r, :c] + y[:r, c:] + y[r:, :c]


def _fold3(y):
    w = y.shape[1] // 3
    return y[:, :w] + y[:, w:2 * w] + y[:, 2 * w:]


def _hgrn_gates(qr, fr, lb):
    sq = _sigmoid(qr)
    q = qr * sq * (HGRN_DIM ** -0.5)
    sf = _sigmoid(fr)
    f = lb + (1.0 - lb) * sf
    k = (1.0 - lb) * _sigmoid(-fr)
    return q, sq, sf, f, k, jnp.log(f)


def _hgrn_intra(q, k, g, b, consts, scores=True):
    _, _, eye, same_block, upper, pos4, below = consts
    heads = range(len(q))
    a = None
    if scores:
        a = [jnp.where(eye, jnp.sum(q[hh] * k[hh], axis=1, keepdims=True), 0.0) for hh in heads]
    saved = [[] for _ in heads]
    for i, m in enumerate(HGRN_LEVELS):
        up = upper[i]
        e = [_hgrn_level_decay(g[hh], b[hh], m, pos4) for hh in heads]
        qt = [jnp.where(up, q[hh] * e[hh], 0.0) for hh in heads]
        kt = [jnp.where(up, 0.0, k[hh] * e[hh]) for hh in heads]
        for hh in heads:
            saved[hh].append((e[hh], qt[hh], kt[hh]))
        if not scores:
            continue
        if m == 1:
            for hh in heads:
                pair = jnp.sum(qt[hh] * pltpu.roll(kt[hh], 1, 0), axis=1, keepdims=True)
                a[hh] = a[hh] + jnp.where(below, pair, 0.0)
            continue
        p = [_dot(_bf(qt[hh]), _bf(kt[hh]), NT) for hh in heads]
        for hh in heads:
            a[hh] = a[hh] + jnp.where(same_block[i], p[hh], 0.0)
    return a, saved


def _hgrn_specs(tb, nb, rev):
    tmap = (lambda t: nb - 1 - t) if rev else (lambda t: t)
    assert HGRN_PAIR == HGRN_HEADS
    return [pl.BlockSpec((tb, 2816), lambda h, t: (tmap(t), 0)),
            pl.BlockSpec((1, HGRN_PAIR * HGRN_DIM), lambda h, t: (0, h)),
            pl.BlockSpec((1, HGRN_DIM), lambda h, t: (0, 0))]


def _hgrn_z(z_ref, sl, base, head):
    return z_ref[sl, base + HGRN_DIM * head:base + HGRN_DIM * (head + 1)].astype(F32)


def _hgrn_fwd(z, lb, onw, name, exchange=None):
    T = z.shape[0]
    tb = min(HGRN_ROWS, T)
    nb, c, nc = T // tb, HGRN_CHUNK, min(HGRN_ROWS, T) // HGRN_CHUNK

    def body(z_ref, lb_ref, onw_ref, rec_ref, o_ref, st_ref, a_ref, state):
        @pl.when(pl.program_id(1) == 0)
        def _():
            state[...] = jnp.zeros_like(state)

        consts = _hgrn_consts()
        lbv = lb_ref[...]
        onwv = onw_ref[...]

        def chunk(ci, carry):
            sl = pl.ds(pl.multiple_of(ci * c, c), c)
            heads = range(HGRN_PAIR)
            lss = [slice(HGRN_DIM * hh, HGRN_DIM * (hh + 1)) for hh in heads]
            gates = [_hgrn_gates(_hgrn_z(z_ref, sl, Z_Q, hh), _hgrn_z(z_ref, sl, Z_F, hh), lbv[:, lss[hh]])
                     for hh in heads]
            q, k, g = [t[0] for t in gates], [t[4] for t in gates], [t[5] for t in gates]
            v = [_bf(_hgrn_z(z_ref, sl, Z_I, hh)) for hh in heads]
            b = [_fold3(_dot(consts[0], _split3(g[hh]))) for hh in heads]
            a, _ = _hgrn_intra(q, k, g, b, consts)
            st = [state[hh] for hh in heads]
            for hh in heads:
                st_ref[hh, ci] = st[hh]
            bl = [b[hh][c - 1:c, :] for hh in heads]
            o_state = [_dot(_bf(q[hh] * jnp.exp(b[hh])), _bf(st[hh]), NT) for hh in heads]
            kv = [_dot(v[hh], _bf(k[hh] * jnp.exp(bl[hh] - b[hh])), TN) for hh in heads]
            a = [_bf(a[hh]) for hh in heads]
            o = [_dot(a[hh], v[hh]) + o_state[hh] for hh in heads]
            for hh in heads:
                a_ref[sl, c * hh:c * (hh + 1)] = a[hh]
                state[hh] = st[hh] * jnp.exp(bl[hh]) + kv[hh]
                o_ref[sl, lss[hh]] = o[hh]
                oh, _ = _rms(o[hh])
                gr = _hgrn_z(z_ref, sl, Z_G, hh)
                rec_ref[sl, lss[hh]] = (oh * onwv * (gr * _sigmoid(gr))).astype(rec_ref.dtype)
            return carry

        lax.fori_loop(0, nc, chunk, 0)

    in_specs = _hgrn_specs(tb, nb, False)
    out_blk = pl.BlockSpec((tb, HGRN_PAIR * HGRN_DIM), lambda h, t: (t, h))
    return _hosted_call(
        body, name=name, grid=(HGRN_HEADS // HGRN_PAIR, nb), in_specs=in_specs,
        out_specs=[out_blk, out_blk, pl.BlockSpec((HGRN_PAIR, nc, HGRN_DIM, HGRN_DIM), lambda h, t: (h, t, 0, 0)),
                   pl.BlockSpec((tb, HGRN_PAIR * c), lambda h, t: (t, h))],
        out_shape=[jax.ShapeDtypeStruct((T, 512), BF16), jax.ShapeDtypeStruct((T, 512), F32),
                   jax.ShapeDtypeStruct((HGRN_HEADS, T // c, HGRN_DIM, HGRN_DIM), F32),
                   jax.ShapeDtypeStruct((T, HGRN_HEADS * c), BF16)],
        scratch=[pltpu.VMEM((HGRN_PAIR, HGRN_DIM, HGRN_DIM), F32)], args=(z, lb, onw),
        semantics=("parallel", "arbitrary"), exchange=exchange)


def _hgrn_bwd(z, lb, onw, o, states, scores, dcat, name, exchange=None):
    T = z.shape[0]
    tb = min(HGRN_ROWS, T)
    nb, c, nc = T // tb, HGRN_CHUNK, min(HGRN_ROWS, T) // HGRN_CHUNK

    def body(z_ref, lb_ref, onw_ref, o_ref, st_ref, drec_ref, a_ref,
             dqr_ref, dfr_ref, dir_ref, dgr_ref, dlb_ref, donw_ref, dstate):
        @pl.when(pl.program_id(1) == 0)
        def _():
            dstate[...] = jnp.zeros_like(dstate)
            dlb_ref[...] = jnp.zeros_like(dlb_ref)

        @pl.when((pl.program_id(0) == 0) & (pl.program_id(1) == 0))
        def _():
            donw_ref[...] = jnp.zeros_like(donw_ref)

        consts = _hgrn_consts()
        rev_mat, eye, same_block, upper = consts[1:5]
        below = consts[6]
        lbv = lb_ref[...]
        onwv = onw_ref[...]
        last = lax.broadcasted_iota(jnp.int32, (c, 1), 0) == c - 1

        def chunk(i, carry):
            ci = nc - 1 - i
            sl = pl.ds(pl.multiple_of(ci * c, c), c)
            hs = range(HGRN_PAIR)
            lss = [slice(HGRN_DIM * hh, HGRN_DIM * (hh + 1)) for hh in hs]
            qr = [_hgrn_z(z_ref, sl, Z_Q, hh) for hh in hs]
            gates = [_hgrn_gates(qr[hh], _hgrn_z(z_ref, sl, Z_F, hh), lbv[:, lss[hh]]) for hh in hs]
            q, sq, sf, f, k, g = ([t[j] for t in gates] for j in range(6))
            v = [_bf(_hgrn_z(z_ref, sl, Z_I, hh)) for hh in hs]
            b = [_fold3(_dot(consts[0], _split3(g[hh]))) for hh in hs]
            _, saved = _hgrn_intra(q, k, g, b, consts, scores=False)
            a = [a_ref[sl, c * hh:c * (hh + 1)] for hh in hs]
            st = [st_ref[hh, ci] for hh in hs]
            dst = [dstate[hh] for hh in hs]

            gr = [_hgrn_z(z_ref, sl, Z_G, hh) for hh in hs]
            sg = [_sigmoid(gr[hh]) for hh in hs]
            norm = [_rms(o_ref[sl, ls]) for ls in lss]
            oh, r = [t[0] for t in norm], [t[1] for t in norm]
            drec = [drec_ref[sl, ls].astype(F32) for ls in lss]
            don = [drec[hh] * (gr[hh] * sg[hh]) for hh in hs]
            do = [_bf(_rms_bwd(don[hh] * onwv, oh[hh], r[hh])) for hh in hs]
            donw = jnp.sum(don[0] * oh[0], axis=0, keepdims=True)
            for hh in hs:
                dgr_ref[sl, lss[hh]] = (drec[hh] * oh[hh] * onwv
                                        * (sg[hh] * (1.0 + gr[hh] * (1.0 - sg[hh])))).astype(dgr_ref.dtype)
                if hh:
                    donw = donw + jnp.sum(don[hh] * oh[hh], axis=0, keepdims=True)
            donw_ref[...] += donw

            eb = [jnp.exp(b[hh]) for hh in hs]
            bl = [b[hh][c - 1:c, :] for hh in hs]
            ebl = [jnp.exp(bl[hh]) for hh in hs]
            ekb = [jnp.exp(bl[hh] - b[hh]) for hh in hs]
            qe = [q[hh] * eb[hh] for hh in hs]
            ke = [k[hh] * ekb[hh] for hh in hs]
            da = [_dot(do[hh], v[hh], NT) for hh in hs]
            dat = [_dot(v[hh], do[hh], NT) for hh in hs]
            dqe = [_dot(do[hh], _bf(st[hh])) for hh in hs]
            dke = [_dot(v[hh], _bf(dst[hh])) for hh in hs]
            dv_a = [_dot(a[hh], do[hh], TN) for hh in hs]
            dv_s = [_dot(_bf(ke[hh]), _bf(dst[hh]), NT) for hh in hs]
            dst_in = [_dot(do[hh], _bf(qe[hh]), TN) for hh in hs]
            dad = [jnp.sum(jnp.where(eye, da[hh], 0.0), axis=1, keepdims=True) for hh in hs]
            dq = [dqe[hh] * eb[hh] + dad[hh] * k[hh] for hh in hs]
            dk = [dke[hh] * ekb[hh] + dad[hh] * q[hh] for hh in hs]
            db_last = [jnp.sum(dke[hh] * ke[hh], axis=0, keepdims=True)
                       + jnp.sum(dst[hh] * st[hh], axis=0, keepdims=True) * ebl[hh] for hh in hs]
            for hh in hs:
                dstate[hh] = dst[hh] * ebl[hh] + dst_in[hh]
                dir_ref[sl, lss[hh]] = (dv_a[hh] + dv_s[hh]).astype(dir_ref.dtype)
            for lvl, m in enumerate(HGRN_LEVELS):
                if m == 1:
                    pair = [jnp.sum(jnp.where(below, da[hh], 0.0), axis=1, keepdims=True) for hh in hs]
                    xq = [pair[hh] * pltpu.roll(saved[hh][lvl][2], 1, 0) for hh in hs]
                    xk = [pltpu.roll(pair[hh] * saved[hh][lvl][1], c - 1, 0) for hh in hs]
                else:
                    xq = [_dot_hilo(jnp.where(same_block[lvl], da[hh], 0.0), saved[hh][lvl][2]) for hh in hs]
                    xk = [_dot_hilo(jnp.where(same_block[lvl], dat[hh], 0.0), saved[hh][lvl][1]) for hh in hs]
                for hh in hs:
                    e = saved[hh][lvl][0]
                    dq[hh] = dq[hh] + jnp.where(upper[lvl], xq[hh] * e, 0.0)
                    dk[hh] = dk[hh] + jnp.where(upper[lvl], 0.0, xk[hh] * e)
            db = [q[hh] * dq[hh] - k[hh] * dk[hh] + jnp.where(last, db_last[hh], 0.0) for hh in hs]
            dg = [_fold3(_dot(rev_mat, _split3(db[hh]))) for hh in hs]

            for hh in hs:
                ls = lss[hh]
                dqr_ref[sl, ls] = (dq[hh] * (HGRN_DIM ** -0.5)
                                   * (sq[hh] * (1.0 + qr[hh] * (1.0 - sq[hh])))).astype(dqr_ref.dtype)
                dfk = dg[hh] / f[hh] - dk[hh]
                dfr_ref[sl, ls] = ((1.0 - lbv[:, ls]) * sf[hh] * (1.0 - sf[hh]) * dfk).astype(dfr_ref.dtype)
                dlb_ref[:, ls] += jnp.sum((1.0 - sf[hh]) * dfk, axis=0, keepdims=True)
            return carry

        lax.fori_loop(0, nc, chunk, 0)

    in_specs = _hgrn_specs(tb, nb, True)
    rblk = pl.BlockSpec((tb, HGRN_PAIR * HGRN_DIM), lambda h, t: (nb - 1 - t, h))
    in_specs = in_specs + [
        rblk,
        pl.BlockSpec((HGRN_PAIR, nc, HGRN_DIM, HGRN_DIM), lambda h, t: (h, nb - 1 - t, 0, 0)),
        pl.BlockSpec((tb, HGRN_PAIR * HGRN_DIM), lambda h, t: (nb - 1 - t, 4 // HGRN_PAIR + h)),
        pl.BlockSpec((tb, HGRN_PAIR * c), lambda h, t: (nb - 1 - t, h)),
    ]
    return _hosted_call(
        body, name=name, grid=(HGRN_HEADS // HGRN_PAIR, nb), in_specs=in_specs,
        out_specs=[rblk, rblk, rblk, rblk, pl.BlockSpec((1, HGRN_PAIR * HGRN_DIM), lambda h, t: (0, h)),
                   pl.BlockSpec((1, HGRN_DIM), lambda h, t: (0, 0))],
        out_shape=[jax.ShapeDtypeStruct((T, 512), BF16)] * 4
        + [jax.ShapeDtypeStruct((1, 512), F32), jax.ShapeDtypeStruct((1, HGRN_DIM), F32)],
        scratch=[pltpu.VMEM((HGRN_PAIR, HGRN_DIM, HGRN_DIM), F32)], args=(z, lb, onw, o, states, dcat, scores),
        semantics=("arbitrary", "arbitrary"), exchange=exchange)


def _lower_bound(logits, name):
    def body(l_ref, lb_ref):
        l0, l1 = l_ref[0:1, :], l_ref[1:2, :]
        m = jnp.maximum(l0, l1)
        e0, e1 = jnp.exp(l0 - m), jnp.exp(l1 - m)
        lb_ref[...] = e0 / (e0 + e1)

    return pl.pallas_call(
        body, name=name, out_shape=jax.ShapeDtypeStruct((1, logits.shape[1]), F32),
    )(logits)


def _lower_bound_bwd(lb, dlb, name):
    def body(lb_ref, dlb_ref, dl_ref):
        p = lb_ref[...]
        d0 = dlb_ref[...] * p * (1.0 - p)
        dl_ref[0:1, :] = d0
        dl_ref[1:2, :] = -d0

    return pl.pallas_call(
        body, name=name, out_shape=jax.ShapeDtypeStruct((2, lb.shape[1]), F32),
    )(lb, dlb)


CA_ROWS = 1024


def _ca_fwd(q, k, v, name):
    T, W = q.shape
    M = k.shape[0]
    tq = min(CA_ROWS, T)
    scale = CA_HEAD_DIM ** -0.5

    def body(q_ref, k_ref, v_ref, o_ref):
        hss = [slice(CA_HEAD_DIM * h, CA_HEAD_DIM * (h + 1)) for h in range(CA_HEADS)]
        s = [_dot(q_ref[:, hs], k_ref[:, hs], NT) * scale for hs in hss]
        p = [jnp.exp(sh - jnp.max(sh, axis=-1, keepdims=True)) for sh in s]
        p = [ph / jnp.sum(ph, axis=-1, keepdims=True) for ph in p]
        o = [_dot(_bf(ph), v_ref[:, hs]) for ph, hs in zip(p, hss)]
        for oh, hs in zip(o, hss):
            o_ref[:, hs] = oh.astype(o_ref.dtype)

    full = pl.BlockSpec((M, W), lambda i: (0, 0))
    return pl.pallas_call(
        body, name=name, grid=(T // tq,), in_specs=[_row_spec(tq, W), full, full], out_specs=_row_spec(tq, W),
        out_shape=jax.ShapeDtypeStruct((T, W), BF16), compiler_params=_params("parallel"),
    )(q, k, v)


def _ca_bwd(q, k, v, do, name):
    T, W = q.shape
    M = k.shape[0]
    tq = min(CA_ROWS, T)
    scale = CA_HEAD_DIM ** -0.5

    def body(q_ref, k_ref, v_ref, do_ref, dq_ref, dk_ref, dv_ref):
        @pl.when(pl.program_id(0) == 0)
        def _():
            dk_ref[...] = jnp.zeros_like(dk_ref)
            dv_ref[...] = jnp.zeros_like(dv_ref)

        heads = range(CA_HEADS)
        hss = [slice(CA_HEAD_DIM * h, CA_HEAD_DIM * (h + 1)) for h in heads]
        qh, kh = [q_ref[:, hs] for hs in hss], [k_ref[:, hs] for hs in hss]
        vh, doh = [v_ref[:, hs] for hs in hss], [do_ref[:, hs] for hs in hss]
        s = [_dot(qh[h], kh[h], NT) * scale for h in heads]
        dp = [_dot(doh[h], vh[h], NT) for h in heads]
        p = [jnp.exp(s[h] - jnp.max(s[h], axis=-1, keepdims=True)) for h in heads]
        p = [p[h] / jnp.sum(p[h], axis=-1, keepdims=True) for h in heads]
        ds = [_bf(p[h] * (dp[h] - jnp.sum(p[h] * dp[h], axis=-1, keepdims=True)) * scale) for h in heads]
        dq = [_dot(ds[h], kh[h]) for h in heads]
        dk = [_dot(ds[h], qh[h], TN) for h in heads]
        dv = [_dot(_bf(p[h]), doh[h], TN) for h in heads]
        for h in heads:
            dq_ref[:, hss[h]] = dq[h].astype(dq_ref.dtype)
            dk_ref[:, hss[h]] += dk[h]
            dv_ref[:, hss[h]] += dv[h]

    full = pl.BlockSpec((M, W), lambda i: (0, 0))
    return pl.pallas_call(
        body, name=name, grid=(T // tq,), in_specs=[_row_spec(tq, W), full, full, _row_spec(tq, W)],
        out_specs=[_row_spec(tq, W), full, full],
        out_shape=[jax.ShapeDtypeStruct((T, W), BF16), jax.ShapeDtypeStruct((M, W), F32), jax.ShapeDtypeStruct((M, W), F32)],
        compiler_params=_params("arbitrary"),
    )(q, k, v, do)


FFN_ROWS = 256
FFN_COLS = 1408
GELU_C0 = 0.7978845608028654
GELU_C1 = 0.044715


def _gelu(x):
    t = jnp.tanh(GELU_C0 * (x + GELU_C1 * x * x * x))
    return 0.5 * x * (1.0 + t), t


def _gelu_grad(x, t):
    return 0.5 * (1.0 + t) + 0.5 * x * (1.0 - t * t) * GELU_C0 * (1.0 + 3.0 * GELU_C1 * x * x)


def _shift_down(cur, halo, first, tb):
    row = lax.broadcasted_iota(jnp.int32, (tb, 1), 0)
    h6 = jnp.where(first, 0.0, halo[6:7])
    h7 = jnp.where(first, 0.0, halo[7:8])
    u1 = jnp.where(row == 0, h7, pltpu.roll(cur, 1, 0))
    u2 = jnp.where(row == 0, h6, jnp.where(row == 1, h7, pltpu.roll(cur, 2, 0)))
    return u1, u2


def _conv(u_ref, halo_ref, w_ref, b_ref, half, first, tb):
    cur = u_ref[half]
    u1, u2 = _shift_down(cur, halo_ref[half], first, tb)
    w = w_ref[...]
    return w[0:1] * u2 + w[1:2] * u1 + w[2:3] * cur + b_ref[...], cur, u1, u2


def _ffn_specs(tb, tc, rows_first):
    nj = D_FF // tc
    rc = (lambda a, b: (a, b)) if rows_first else (lambda a, b: (b, a))
    def at(f):
        return lambda a, b: f(*rc(a, b))
    blk = pl.BlockSpec((2, tb, tc), at(lambda t, j: (0, t, j)))
    halo = pl.BlockSpec((2, 8, tc), at(lambda t, j: (0, jnp.maximum(t * (tb // 8) - 1, 0), j)))
    wg = pl.BlockSpec((3, tc), at(lambda t, j: (0, j)))
    wv = pl.BlockSpec((3, tc), at(lambda t, j: (0, j + nj)))
    bg = pl.BlockSpec((1, tc), at(lambda t, j: (0, j)))
    bv = pl.BlockSpec((1, tc), at(lambda t, j: (0, j + nj)))
    flat = pl.BlockSpec((tb, tc), at(lambda t, j: (t, j)))
    return blk, halo, wg, wv, bg, bv, flat


def _glu_fwd(u, cw, cb, name):
    T = u.shape[1]
    tb, tc = min(FFN_ROWS, T), FFN_COLS

    def body(u_ref, halo_ref, wg_ref, wv_ref, bg_ref, bv_ref, a_ref):
        first = pl.program_id(0) == 0
        cg = _conv(u_ref, halo_ref, wg_ref, bg_ref, 0, first, tb)[0]
        cv = _conv(u_ref, halo_ref, wv_ref, bv_ref, 1, first, tb)[0]
        a_ref[...] = (_gelu(cg)[0] * cv).astype(a_ref.dtype)

    blk, halo, wg, wv, bg, bv, flat = _ffn_specs(tb, tc, True)
    return pl.pallas_call(
        body, name=name, grid=(T // tb, D_FF // tc), in_specs=[blk, halo, wg, wv, bg, bv], out_specs=flat,
        out_shape=jax.ShapeDtypeStruct((T, D_FF), BF16), compiler_params=_params("parallel", "parallel"),
    )(u, u, cw, cw, cb, cb)


def _glu_bwd(u, cw, cb, da, name, exchange=None):
    T = u.shape[1]
    tb, tc = min(FFN_ROWS, T), FFN_COLS

    def body(u_ref, halo_ref, wg_ref, wv_ref, bg_ref, bv_ref, da_ref, dc_ref, db_ref, dw_ref):
        first = pl.program_id(1) == 0

        @pl.when(first)
        def _():
            db_ref[...] = jnp.zeros_like(db_ref)
            dw_ref[...] = jnp.zeros_like(dw_ref)

        cg, ug, ug1, ug2 = _conv(u_ref, halo_ref, wg_ref, bg_ref, 0, first, tb)
        cv, uv, uv1, uv2 = _conv(u_ref, halo_ref, wv_ref, bv_ref, 1, first, tb)
        da = da_ref[...]
        gl, t = _gelu(cg)
        dcg = da * cv * _gelu_grad(cg, t)
        dcv = da * gl
        dc_ref[0] = dcg
        dc_ref[1] = dcv
        for half, dc, taps in ((0, dcg, (ug2, ug1, ug)), (1, dcv, (uv2, uv1, uv))):
            db_ref[half] += jnp.sum(dc, axis=0, keepdims=True)
            for tap in range(3):
                dw_ref[half, tap:tap + 1, :] += jnp.sum(dc * taps[tap], axis=0, keepdims=True)

    blk, halo, wg, wv, bg, bv, flat = _ffn_specs(tb, tc, False)
    return _hosted_call(
        body, name=name, grid=(D_FF // tc, T // tb), in_specs=[blk, halo, wg, wv, bg, bv, flat],
        out_specs=[blk, pl.BlockSpec((2, 1, tc), lambda j, t: (0, 0, j)), pl.BlockSpec((2, 3, tc), lambda j, t: (0, 0, j))],
        out_shape=[jax.ShapeDtypeStruct((2, T, D_FF), F32), jax.ShapeDtypeStruct((2, 1, D_FF), F32),
                   jax.ShapeDtypeStruct((2, 3, D_FF), F32)],
        scratch=[], args=(u, u, cw, cw, cb, cb, da), semantics=("parallel", "arbitrary"), exchange=exchange)


def _conv_bwd(dc, cw, name):
    T = dc.shape[1]
    tb, tc = min(FFN_ROWS, T), FFN_COLS
    nt, nj = T // tb, D_FF // tc

    def body(dc_ref, halo_ref, wg_ref, wv_ref, du_ref):
        last = pl.program_id(0) == nt - 1
        row = lax.broadcasted_iota(jnp.int32, (tb, 1), 0)
        for half, w_ref in ((0, wg_ref), (1, wv_ref)):
            cur = dc_ref[half]
            halo = halo_ref[half]
            h0 = jnp.where(last, 0.0, halo[0:1])
            h1 = jnp.where(last, 0.0, halo[1:2])
            d1 = jnp.where(row == tb - 1, h0, pltpu.roll(cur, tb - 1, 0))
            d2 = jnp.where(row == tb - 1, h1, jnp.where(row == tb - 2, h0, pltpu.roll(cur, tb - 2, 0)))
            w = w_ref[...]
            du_ref[half] = (w[2:3] * cur + w[1:2] * d1 + w[0:1] * d2).astype(du_ref.dtype)

    blk = pl.BlockSpec((2, tb, tc), lambda t, j: (0, t, j))
    halo = pl.BlockSpec((2, 8, tc), lambda t, j: (0, jnp.minimum((t + 1) * (tb // 8), T // 8 - 1), j))
    wg = pl.BlockSpec((3, tc), lambda t, j: (0, j))
    wv = pl.BlockSpec((3, tc), lambda t, j: (0, j + nj))
    return pl.pallas_call(
        body, name=name, grid=(nt, nj), in_specs=[blk, halo, wg, wv], out_specs=blk,
        out_shape=jax.ShapeDtypeStruct((2, T, D_FF), BF16), compiler_params=_params("parallel", "parallel"),
    )(dc, dc, cw, cw)


def _mesh_pos():
    return lax.axis_index("x"), lax.axis_index("y"), lax.axis_index("c")


def _peer(pos, k):
    return (pos[0] ^ ((k >> 2) & 1), pos[1] ^ ((k >> 1) & 1), pos[2] ^ (k & 1))


def _index(pos):
    return 4 * pos[0] + 2 * pos[1] + pos[2]


class _Exchange:
    def __init__(self, kind, buf, relay=False):
        assert kind in ("gather", "scatter") and not (relay and kind == "scatter")
        self.kind, self.buf, self.relay = kind, buf, relay
        self.out_shape = jax.ShapeDtypeStruct(((N_DEV,) + buf.shape) if kind == "gather" else buf.shape, buf.dtype)
        self.spec = pl.BlockSpec(memory_space=pl.ANY)
        self.scratch = [pltpu.SemaphoreType.DMA((N_DEV - 1,)), pltpu.SemaphoreType.DMA((N_DEV - 1,)),
                        pltpu.SemaphoreType.DMA]

    def _src(self, x_ref, dest):
        return x_ref if self.kind == "gather" else x_ref.at[dest]

    def _copies(self, x_ref, out_ref, send_sems, recv_sems, local_sem):
        pos = _mesh_pos()
        me = _index(pos)
        local = pltpu.make_async_copy(self._src(x_ref, me), out_ref.at[me], local_sem)
        sends, recvs = [], []
        for k in range(1, N_DEV):
            peer = _peer(pos, k)
            sends.append(pltpu.make_async_remote_copy(
                src_ref=self._src(x_ref, _index(peer)), dst_ref=out_ref.at[me], send_sem=send_sems.at[k - 1],
                recv_sem=recv_sems.at[k - 1], device_id=peer, device_id_type=pl.DeviceIdType.MESH))
            recvs.append(pltpu.make_async_remote_copy(
                src_ref=self._src(x_ref, me), dst_ref=out_ref.at[_index(peer)], send_sem=send_sems.at[k - 1],
                recv_sem=recv_sems.at[k - 1], device_id=peer, device_id_type=pl.DeviceIdType.MESH))
        return local, sends, recvs

    def _relay_copies(self, x_ref, out_ref, send_sems, recv_sems, local_sem):
        x, y, c = _mesh_pos()
        me, sibling = (x, y, c), (x, y, 1 - c)
        chips = [(1 - x, y), (x, 1 - y), (1 - x, 1 - y)]

        def copy(k, block, to, own=False):
            return pltpu.make_async_remote_copy(
                src_ref=x_ref if own else out_ref.at[_index(block)], dst_ref=out_ref.at[_index(block)],
                send_sem=send_sems.at[k], recv_sem=recv_sems.at[k], device_id=to, device_id_type=pl.DeviceIdType.MESH)

        local = pltpu.make_async_copy(x_ref, out_ref.at[_index(me)], local_sem)
        first = [copy(0, me, sibling, own=True)] + [copy(1 + j, me, (*chip, c), own=True) for j, chip in enumerate(chips)]
        landed = [copy(1 + j, (*chip, c), me) for j, chip in enumerate(chips)]
        passed = [copy(4 + j, (*chip, c), sibling) for j, chip in enumerate(chips)]
        from_sibling = [copy(0, sibling, me)] + [copy(4 + j, (*chip, 1 - c), me) for j, chip in enumerate(chips)]
        return local, first, landed, passed, from_sibling

    def start(self, *refs):
        if self.relay:
            local, first = self._relay_copies(*refs)[:2]
            local.start()
            for cp in first:
                cp.start()
            return
        local, sends, _ = self._copies(*refs)
        local.start()
        for cp in sends:
            cp.start()

    def finish(self, *refs):
        if self.relay:
            local, first, landed, passed, from_sibling = self._relay_copies(*refs)
            for got, forward in zip(landed, passed):
                got.wait_recv()
                forward.start()
            for cp in from_sibling:
                cp.wait_recv()
            for cp in first + passed:
                cp.wait_send()
            local.wait()
            return
        local, sends, recvs = self._copies(*refs)
        for cp in recvs:
            cp.wait_recv()
        for cp in sends:
            cp.wait_send()
        local.wait()


def _hosted_call(body, *, name, grid, in_specs, out_specs, out_shape, scratch, args, semantics, exchange=None):
    if exchange is None:
        return pl.pallas_call(
            body, name=name, grid=grid, in_specs=in_specs, out_specs=out_specs, out_shape=out_shape,
            scratch_shapes=scratch, compiler_params=_params(*semantics))(*args)
    n_in, n_out, n_scr = len(in_specs), len(out_specs), len(scratch)

    def hosted(*refs):
        ins, x_ref = refs[:n_in], refs[n_in]
        outs, land_ref = refs[n_in + 1:n_in + 1 + n_out], refs[n_in + 1 + n_out]
        rest = refs[n_in + n_out + 2:]
        sems = rest[n_scr:]
        ids = [pl.program_id(a) for a in range(len(grid))]
        first, last = ids[0] == 0, ids[0] == grid[0] - 1
        for a in range(1, len(grid)):
            first, last = first & (ids[a] == 0), last & (ids[a] == grid[a] - 1)

        @pl.when(first)
        def _():
            exchange.start(x_ref, land_ref, *sems)

        body(*ins, *outs, *rest[:n_scr])

        @pl.when(last)
        def _():
            exchange.finish(x_ref, land_ref, *sems)

    return pl.pallas_call(
        hosted, name=name, grid=grid, in_specs=list(in_specs) + [exchange.spec],
        out_specs=list(out_specs) + [exchange.spec], out_shape=list(out_shape) + [exchange.out_shape],
        scratch_shapes=list(scratch) + exchange.scratch, compiler_params=_params(*(["arbitrary"] * len(grid))),
    )(*args, exchange.buf)


def _exchange_alone(exchange, name):
    def body(x_ref, out_ref, send_sems, recv_sems, local_sem):
        exchange.start(x_ref, out_ref, send_sems, recv_sems, local_sem)
        exchange.finish(x_ref, out_ref, send_sems, recv_sems, local_sem)

    return pl.pallas_call(
        body, name=name, out_shape=exchange.out_shape, in_specs=[exchange.spec], out_specs=exchange.spec,
        scratch_shapes=exchange.scratch)(exchange.buf)


def _adamw(w, g, m, v):
    m = ADAM_B1 * m + (1.0 - ADAM_B1) * g
    v = ADAM_B2 * v + (1.0 - ADAM_B2) * (g * g)
    m_hat = m / (1.0 - ADAM_B1 ** ADAM_STEP)
    v_hat = v / (1.0 - ADAM_B2 ** ADAM_STEP)
    delta = -ADAM_LR * (m_hat / (jnp.sqrt(v_hat) + ADAM_EPS) + ADAM_WD * w)
    return delta, m, v


def _sum_rows(parts, r0, rows, name, wmv=None):
    C = parts.shape[2]
    tr = max(t for t in range(16, ROWS + 1, 16) if rows % t == 0 and r0 % t == 0)

    def total(p_ref):
        g = p_ref[0].astype(F32)
        for i in range(1, N_DEV):
            g = g + p_ref[i].astype(F32)
        return g

    p_spec = pl.BlockSpec((N_DEV, tr, C), lambda i: (0, r0 // tr + i, 0))
    if wmv is None:
        def body(p_ref, g_ref):
            g_ref[...] = total(p_ref)

        return pl.pallas_call(
            body, name=name, grid=(rows // tr,), in_specs=[p_spec], out_specs=_row_spec(tr, C),
            out_shape=jax.ShapeDtypeStruct((rows, C), F32), compiler_params=_params("parallel"))(parts)

    def body(p_ref, w_ref, m_ref, v_ref, g_ref, d_ref, mo_ref, vo_ref):
        g = total(p_ref)
        g_ref[0] = g
        d_ref[0], mo_ref[0], vo_ref[0] = _adamw(w_ref[0], g, m_ref[0], v_ref[0])

    blk = pl.BlockSpec((1, tr, C), lambda i: (0, i, 0))
    return pl.pallas_call(
        body, name=name, grid=(rows // tr,), in_specs=[p_spec, blk, blk, blk], out_specs=[blk] * 4,
        out_shape=[jax.ShapeDtypeStruct((1, rows, C), F32)] * 4, compiler_params=_params("parallel"))(parts, *wmv)


def _sum_parts(parts, name):
    _, R, C = parts.shape

    def body(p_ref, g_ref):
        g = p_ref[0]
        for i in range(1, N_DEV):
            g = g + p_ref[i]
        g_ref[...] = g

    return pl.pallas_call(body, name=name, out_shape=jax.ShapeDtypeStruct((R, C), F32))(parts)


def _adamw_call(w, g, m, v, name):
    _, R, C = w.shape
    tr = min(ROWS, R)

    def body(w_ref, g_ref, m_ref, v_ref, d_ref, mo_ref, vo_ref):
        d_ref[...], mo_ref[...], vo_ref[...] = _adamw(w_ref[...], g_ref[...], m_ref[...], v_ref[...])

    blk = pl.BlockSpec((1, tr, C), lambda i: (0, i, 0))
    return pl.pallas_call(
        body, name=name, grid=(R // tr,), in_specs=[blk] * 4, out_specs=[blk] * 3,
        out_shape=[jax.ShapeDtypeStruct(w.shape, F32)] * 3, compiler_params=_params("parallel"))(w, g, m, v)


NORMS = ("mix_pre_norm", "mix_post_norm", "ca_pre_norm", "mem_norm", "ca_post_norm", "ffn_pre_norm", "ffn_post_norm")
SMALL = ("mix_pre_norm", "attn_sinks", "hgrn_lb_logits", "hgrn_out_norm", "mix_post_norm", "ca_pre_norm", "mem_norm",
         "ca_post_norm", "ffn_pre_norm", "ffn_conv_w", "ffn_conv_b", "ffn_post_norm")
SMALL_ROWS = 40
ROW_LOGITS, ROW_MISC, ROW_CONV_B, ROW_CONV_W = 7, 8, 9, 15
LANE_SINKS, LANE_LOSS = 128, 256
FF_PIECES = ((0, 1024), (1024, 2048), (2048, D_FF))


def _pack_small(norm_grads, dlogits, donw, dsinks, loss, d_cb, d_cw, name):
    def body(*refs):
        norm_refs = refs[:len(NORMS)]
        dl_ref, donw_ref, dsink_ref, loss_ref, cb_ref, cw_ref, out_ref = refs[len(NORMS):]
        out_ref[...] = jnp.zeros_like(out_ref)
        for i, ref in enumerate(norm_refs):
            out_ref[i:i + 1, :] = ref[...]
        out_ref[ROW_LOGITS:ROW_LOGITS + 1, 0:512] = dl_ref[0:1, :]
        out_ref[ROW_LOGITS:ROW_LOGITS + 1, 512:1024] = dl_ref[1:2, :]
        out_ref[ROW_MISC:ROW_MISC + 1, 0:HGRN_DIM] = donw_ref[...]
        out_ref[ROW_MISC:ROW_MISC + 1, LANE_SINKS:LANE_SINKS + ATTN_Q_HEADS] = dsink_ref[...]
        out_ref[ROW_MISC:ROW_MISC + 1, LANE_LOSS:LANE_LOSS + LANE] = loss_ref[...]
        for h in range(2):
            for j, (c0, c1) in enumerate(FF_PIECES):
                r = ROW_CONV_B + 3 * h + j
                out_ref[r:r + 1, 0:c1 - c0] = cb_ref[h, :, c0:c1]
                for t in range(3):
                    r = ROW_CONV_W + 3 * (3 * h + t) + j
                    out_ref[r:r + 1, 0:c1 - c0] = cw_ref[h, t:t + 1, c0:c1]

    return pl.pallas_call(
        body, name=name, out_shape=jax.ShapeDtypeStruct((SMALL_ROWS, 1024), F32),
    )(*norm_grads, dlogits, donw, dsinks, loss, d_cb, d_cw)


def _adamw_small(total, g_conv_w, w, m, v, name):
    n = len(SMALL)

    def body(*refs):
        t_ref, gcw_ref = refs[:2]
        w_refs, m_refs, v_refs = (dict(zip(SMALL, refs[2 + n * i:2 + n * (i + 1)])) for i in range(3))
        outs = refs[2 + 3 * n:]
        loss_ref = outs[0]
        g_refs, d_refs, mo_refs, vo_refs = (dict(zip(SMALL, outs[1 + n * i:1 + n * (i + 1)])) for i in range(4))
        loss_ref[...] = t_ref[ROW_MISC:ROW_MISC + 1, LANE_LOSS:LANE_LOSS + 1]

        def step(nm, idx, g):
            g_refs[nm][idx] = g
            d_refs[nm][idx], mo_refs[nm][idx], vo_refs[nm][idx] = _adamw(w_refs[nm][idx], g, m_refs[nm][idx], v_refs[nm][idx])

        everything = (slice(None), slice(None))
        for i, nm in enumerate(NORMS):
            step(nm, everything, t_ref[i:i + 1, :])
        step("hgrn_lb_logits", (slice(0, 1), slice(None)), t_ref[ROW_LOGITS:ROW_LOGITS + 1, 0:512])
        step("hgrn_lb_logits", (slice(1, 2), slice(None)), t_ref[ROW_LOGITS:ROW_LOGITS + 1, 512:1024])
        step("hgrn_out_norm", everything, t_ref[ROW_MISC:ROW_MISC + 1, 0:HGRN_DIM])
        step("attn_sinks", everything, t_ref[ROW_MISC:ROW_MISC + 1, LANE_SINKS:LANE_SINKS + ATTN_Q_HEADS])
        for h in range(2):
            for j, (c0, c1) in enumerate(FF_PIECES):
                r = ROW_CONV_B + 3 * h + j
                step("ffn_conv_b", (slice(None), slice(D_FF * h + c0, D_FF * h + c1)), t_ref[r:r + 1, 0:c1 - c0])
        step("ffn_conv_w", (slice(None), slice(None), slice(None)), gcw_ref[...])

    shapes = [jax.ShapeDtypeStruct(w[nm].shape, F32) for nm in SMALL]
    out = pl.pallas_call(
        body, name=name, out_shape=[jax.ShapeDtypeStruct((1, 1), F32)] + shapes * 4,
    )(total, g_conv_w, *[w[nm] for nm in SMALL], *[m[nm] for nm in SMALL], *[v[nm] for nm in SMALL])
    trees = [dict(zip(SMALL, out[1 + n * i:1 + n * (i + 1)])) for i in range(4)]
    return out[0], trees


BIG = ("w_in", "w_out", "ca_wq", "ca_wk", "ca_wv", "ca_wo", "ffn_w_up", "ffn_w_down")
BIG_FULL = {"w_in": (1024, 2816), "w_out": (1024, 1024), "ca_wq": (1024, 1024), "ca_wk": (1024, 1024),
            "ca_wv": (1024, 1024), "ca_wo": (1024, 1024), "ffn_w_up": (1024, 5632), "ffn_w_down": (2816, 1024)}
G_IN, G_MID, G_UP, G_DOWN = ("w_in",), ("w_out", "ca_wq", "ca_wk", "ca_wv", "ca_wo"), ("ffn_w_up",), ("ffn_w_down",)
GROUPS = (G_IN, G_MID, G_UP, G_DOWN)
COL_SHARDED = ("w_in", "ffn_w_up")
PACK_COLS = 1024


def _big_rows(name):
    r, c = BIG_FULL[name]
    return r * c // N_DEV // PACK_COLS


def _pack_shards(w, names):
    rows = [w[n][0].T if n in COL_SHARDED else w[n][0] for n in names]
    return (rows[0] if len(rows) == 1 else jnp.concatenate(rows, axis=0)).astype(BF16)


def _unpack_gathered(gathered, names):
    out, r0 = {}, 0
    for n in names:
        rows = _big_rows(n)
        out[n] = gathered[:, r0:r0 + rows].reshape(N_DEV * rows, PACK_COLS)
        r0 += rows
    return out


def _pack_full_grads(grads, names):
    parts = [grads[n].reshape(N_DEV, _big_rows(n), PACK_COLS) for n in names]
    return parts[0] if len(parts) == 1 else jnp.concatenate(parts, axis=1)


def kernel(x, mem, mix_pre_norm, w_in, attn_sinks, hgrn_lb_logits, hgrn_out_norm, w_out, mix_post_norm, ca_pre_norm, mem_norm, ca_wq, ca_wk, ca_wv, ca_wo, ca_post_norm, ffn_pre_norm, ffn_w_up, ffn_conv_w, ffn_conv_b, ffn_w_down, ffn_post_norm, loss_target, m_mix_pre_norm, m_w_in, m_attn_sinks, m_hgrn_lb_logits, m_hgrn_out_norm, m_w_out, m_mix_post_norm, m_ca_pre_norm, m_mem_norm, m_ca_wq, m_ca_wk, m_ca_wv, m_ca_wo, m_ca_post_norm, m_ffn_pre_norm, m_ffn_w_up, m_ffn_conv_w, m_ffn_conv_b, m_ffn_w_down, m_ffn_post_norm, v_mix_pre_norm, v_w_in, v_attn_sinks, v_hgrn_lb_logits, v_hgrn_out_norm, v_w_out, v_mix_post_norm, v_ca_pre_norm, v_mem_norm, v_ca_wq, v_ca_wk, v_ca_wv, v_ca_wo, v_ca_post_norm, v_ffn_pre_norm, v_ffn_w_up, v_ffn_conv_w, v_ffn_conv_b, v_ffn_w_down, v_ffn_post_norm):
    names = ["mix_pre_norm", "w_in", "attn_sinks", "hgrn_lb_logits", "hgrn_out_norm", "w_out", "mix_post_norm",
             "ca_pre_norm", "mem_norm", "ca_wq", "ca_wk", "ca_wv", "ca_wo", "ca_post_norm", "ffn_pre_norm",
             "ffn_w_up", "ffn_conv_w", "ffn_conv_b", "ffn_w_down", "ffn_post_norm"]
    w_all = dict(zip(names, [mix_pre_norm, w_in, attn_sinks, hgrn_lb_logits, hgrn_out_norm, w_out, mix_post_norm,
                             ca_pre_norm, mem_norm, ca_wq, ca_wk, ca_wv, ca_wo, ca_post_norm, ffn_pre_norm,
                             ffn_w_up, ffn_conv_w, ffn_conv_b, ffn_w_down, ffn_post_norm]))
    m_all = dict(zip(names, [m_mix_pre_norm, m_w_in, m_attn_sinks, m_hgrn_lb_logits, m_hgrn_out_norm, m_w_out,
                             m_mix_post_norm, m_ca_pre_norm, m_mem_norm, m_ca_wq, m_ca_wk, m_ca_wv, m_ca_wo,
                             m_ca_post_norm, m_ffn_pre_norm, m_ffn_w_up, m_ffn_conv_w, m_ffn_conv_b, m_ffn_w_down,
                             m_ffn_post_norm]))
    v_all = dict(zip(names, [v_mix_pre_norm, v_w_in, v_attn_sinks, v_hgrn_lb_logits, v_hgrn_out_norm, v_w_out,
                             v_mix_post_norm, v_ca_pre_norm, v_mem_norm, v_ca_wq, v_ca_wk, v_ca_wv, v_ca_wo,
                             v_ca_post_norm, v_ffn_pre_norm, v_ffn_w_up, v_ffn_conv_w, v_ffn_conv_b, v_ffn_w_down,
                             v_ffn_post_norm]))
    dev = _index(_mesh_pos())

    w_packs = {grp: _pack_shards(w_all, grp) for grp in GROUPS}
    shard_w = D_FF * 2 // N_DEV
    conv_w_rows = _exchange_alone(_Exchange("gather", ffn_conv_w[0]), "gather_conv_w")
    conv_w_full = conv_w_rows.transpose(1, 0, 2).reshape(3, 2 * D_FF)

    received, small_pack, grad_x = _local_step(
        x[0], mem[0], loss_target[0], w_packs, conv_w_full,
        {n: w_all[n] for n in NORMS}, attn_sinks, hgrn_lb_logits, hgrn_out_norm, ffn_conv_b)

    total = _sum_parts(_exchange_alone(_Exchange("gather", small_pack), "gather_small"), "sum_small")
    cw = total[ROW_CONV_W:ROW_CONV_W + 18].reshape(2, 3, 3 * PACK_COLS)[:, :, :D_FF]
    cw = cw.transpose(1, 0, 2).reshape(3, 2 * D_FF)
    g_conv_w = lax.dynamic_slice_in_dim(cw, dev * shard_w, shard_w, axis=1)[None]
    loss, (out_g, out_d, out_m, out_v) = _adamw_small(total, g_conv_w, w_all, m_all, v_all, "adamw_small")

    for grp in GROUPS:
        r0 = 0
        for n in grp:
            rows = _big_rows(n)
            if n in COL_SHARDED:
                g = _sum_rows(received[grp], r0, rows, "sum_" + n).T[None]
                d, mo, vo = _adamw_call(w_all[n], g, m_all[n], v_all[n], "adamw_" + n)
            else:
                g, d, mo, vo = _sum_rows(received[grp], r0, rows, "adamw_" + n, wmv=(w_all[n], m_all[n], v_all[n]))
            out_g[n], out_d[n], out_m[n], out_v[n] = g, d, mo, vo
            r0 += rows

    return (loss[0, 0], grad_x[None], *[out_g[n] for n in names], *[out_d[n] for n in names],
            *[out_m[n] for n in names], *[out_v[n] for n in names])


def _local_step(x, mem, target, w_packs, conv_w, norms, sinks, lb_logits, out_norm, conv_b):
    g1, g2, g3 = norms["mix_pre_norm"], norms["mix_post_norm"], norms["ca_pre_norm"]
    g4, g5, g6, g7 = norms["mem_norm"], norms["ca_post_norm"], norms["ffn_pre_norm"], norms["ffn_post_norm"]

    h1, gathered = _norm_fwd(x, g1, "mix_norm", exchange=_Exchange("gather", w_packs[G_IN], relay=True))
    w_in_t = _unpack_gathered(gathered, G_IN)["w_in"]
    up_shard = w_packs[G_UP]
    up_rows = up_shard.shape[0]
    up_cuts = (0, up_rows // 2, 3 * up_rows // 4, up_rows)
    up_parts = [up_shard[a:b] for a, b in zip(up_cuts[:-1], up_cuts[1:])]
    z, up_0 = _mm(h1, w_in_t, mode="nt", out_dtype=BF16, name="in_proj", tn=2816,
                  exchange=_Exchange("gather", up_parts[0]))
    attn, lse, gathered = _swa_fwd(z, sinks, "swa_fwd", exchange=_Exchange("gather", w_packs[G_DOWN]))
    w_down = _unpack_gathered(gathered, G_DOWN)["ffn_w_down"]
    lb = _lower_bound(lb_logits, "lower_bound")
    rec, o_rec, states, scores, gathered = _hgrn_fwd(
        z, lb, out_norm, "hgrn_fwd", exchange=_Exchange("gather", w_packs[G_MID]))
    w_out, wq, wk, wv, wo = (_unpack_gathered(gathered, G_MID)[n] for n in G_MID)
    cat = jnp.concatenate([attn, rec], axis=1)
    x1, h2, mix, up_1 = _mm(cat, w_out, mode="nn", out_dtype=BF16, name="out_proj",
                            exchange=_Exchange("gather", up_parts[1]), epilogue=_post_pre(x, g2, g3))
    mem_n = _norm_fwd(mem, g4, "mem_norm")
    q = _mm(h2, wq, mode="nn", out_dtype=BF16, name="ca_q")
    k = _mm(mem_n, wk, mode="nn", out_dtype=BF16, name="ca_k")
    v = _mm(mem_n, wv, mode="nn", out_dtype=BF16, name="ca_v")
    oc = _ca_fwd(q, k, v, "ca_fwd")
    x2, h3, c, up_2 = _mm(oc, wo, mode="nn", out_dtype=BF16, name="ca_o",
                          exchange=_Exchange("gather", up_parts[2]), epilogue=_post_pre(x1, g5, g6))
    w_up_t = jnp.concatenate([up_0, up_1, up_2], axis=1).reshape(-1, PACK_COLS)
    u = _mm(h3, w_up_t, mode="nt", out_dtype=F32, name="ffn_up", tn=2816, split_out=True)
    a = _glu_fwd(u, conv_w, conv_b, "glu_fwd")
    dx3, dy, loss_row, dg7 = _mm(a, w_down, mode="nn", out_dtype=BF16, name="ffn_down", tm=512, tk=2816,
                                 epilogue=_final(x2, target, g7))
    loss = loss_row[:, :LANE]

    da = _mm(dy, w_down, mode="nt", out_dtype=F32, name="ffn_down_dx", tn=2816)
    d_w_down = _mm(a, dy, mode="tn", out_dtype=BF16, name="ffn_down_dw", tm=2816, tk=1024)
    dc, d_cb, d_cw, got_down = _glu_bwd(
        u, conv_w, conv_b, da, "glu_bwd",
        exchange=_Exchange("scatter", _pack_full_grads({"ffn_w_down": d_w_down}, G_DOWN)))
    du = _conv_bwd(dc, conv_w, "conv_bwd")
    d_w_up_t = _mm(du, h3, mode="tn", out_dtype=BF16, name="ffn_up_dw", tm=2816, tk=1024, split_a=True)
    dx2, dcv, dg6, dg5, got_up = _mm(
        du, w_up_t, mode="nn", out_dtype=BF16, name="ffn_up_dx", tm=1024, tk=1408, split_a=True,
        exchange=_Exchange("scatter", _pack_full_grads({"ffn_w_up": d_w_up_t}, G_UP)),
        epilogue=_norm_bwd2(dx3, x2, c, g6, g5))
    doc = _mm(dcv, wo, mode="nt", out_dtype=BF16, name="ca_o_dx")
    d_wo = _mm(oc, dcv, mode="tn", out_dtype=BF16, name="ca_o_dw", tm=1024, tk=1024)
    dq, dk, dv = _ca_bwd(q, k, v, doc, "ca_bwd")
    d_wq = _mm(h2, dq, mode="tn", out_dtype=BF16, name="ca_q_dw", tm=1024, tk=1024)
    dx1, dmix, dg3, dg2 = _mm(dq, wq, mode="nt", out_dtype=BF16, name="ca_q_dx",
                              epilogue=_norm_bwd2(dx2, x1, mix, g3, g2))
    d_wk = _mm(mem_n, dk, mode="tn", out_dtype=BF16, name="ca_k_dw", tm=1024)
    d_wv = _mm(mem_n, dv, mode="tn", out_dtype=BF16, name="ca_v_dw", tm=1024)
    dmem_k = _mm(dk, wk, mode="nt", out_dtype=F32, name="ca_k_dx")
    dmem_v = _mm(dv, wv, mode="nt", out_dtype=F32, name="ca_v_dx")
    dg4 = _gain_bwd(mem, dmem_k, dmem_v, "mem_norm_bwd")
    dcat = _mm(dmix, w_out, mode="nt", out_dtype=BF16, name="out_proj_dx")
    d_w_out = _mm(cat, dmix, mode="tn", out_dtype=BF16, name="out_proj_dw", tm=1024, tk=1024)
    mid = {"w_out": d_w_out, "ca_wq": d_wq, "ca_wk": d_wk, "ca_wv": d_wv, "ca_wo": d_wo}
    dqr, dfr, dir_, dgr, dlb, donw, got_mid = _hgrn_bwd(
        z, lb, out_norm, o_rec, states, scores, dcat, "hgrn_bwd",
        exchange=_Exchange("scatter", _pack_full_grads(mid, G_MID)))
    dq_a, dka, dkb, dva, dvb, dsinks = _swa_bwd(z, sinks, dcat, lse, "swa_bwd")
    dz = _assemble_dz(dq_a, dka, dkb, dva, dvb, dqr, dfr, dir_, dgr, "assemble_dz")
    d_w_in_t = _mm(dz, h1, mode="tn", out_dtype=BF16, name="in_proj_dw", tm=2816, tk=1024)
    dx, dg1, got_in = _mm(dz, w_in_t, mode="nn", out_dtype=BF16, name="in_proj_dx", tm=512, tk=2816,
                          exchange=_Exchange("scatter", _pack_full_grads({"w_in": d_w_in_t}, G_IN)),
                          epilogue=_norm_bwd1(dx1, x, g1))

    small_pack = _pack_small(
        (dg1, dg2, dg3, dg4, dg5, dg6, dg7), _lower_bound_bwd(lb, dlb, "lower_bound_bwd"), donw, dsinks, loss,
        d_cb, d_cw, "pack_small")
    return {G_IN: got_in, G_MID: got_mid, G_UP: got_up, G_DOWN: got_down}, small_pack, dx
```

```python
import jax
import jax.numpy as jnp
from jax import lax
from jax.experimental import pallas as pl
from jax.experimental.pallas import tpu as pltpu

F32 = jnp.float32
BF16 = jnp.bfloat16
EPS = 1e-6
N_DEV = 8
MESH_AXES = ("x", "y", "c")

ATTN_HEAD_DIM = 64
ATTN_Q_HEADS = 8
ATTN_KV_HEADS = 2
ATTN_BLOCK = 128
HGRN_HEADS = 4
HGRN_DIM = 128
HGRN_CHUNK = 64
HGRN_PAIR = 4
Z_Q, Z_F, Z_I, Z_G = 768, 1280, 1792, 2304
HGRN_LEVELS = (32, 16, 8, 4, 2, 1)
CA_HEADS = 4
CA_HEAD_DIM = 256
D_FF = 2816

ADAM_LR = 0.001
ADAM_B1 = 0.9
ADAM_B2 = 0.999
ADAM_EPS = 1e-08
ADAM_WD = 0.01
ADAM_STEP = 10

VMEM_LIMIT = 58 << 20
EPILOGUE_ROWS = 256
LANE = 128

NT = (((1,), (1,)), ((), ()))
TN = (((0,), (0,)), ((), ()))


def _params(*sem):
    return pltpu.CompilerParams(dimension_semantics=sem, vmem_limit_bytes=VMEM_LIMIT)


def _tile(n, cap):
    if n <= cap:
        return n
    best = 0
    for t in range(LANE, cap + 1, LANE):
        if n % t == 0:
            best = t
    assert best, (n, cap)
    return best


def _dot(a, b, dims=None):
    if dims is None:
        return jnp.dot(a, b, preferred_element_type=F32)
    return lax.dot_general(a, b, dims, preferred_element_type=F32)


def _bf(x):
    return x.astype(BF16)


def _sigmoid(x):
    return 1.0 / (1.0 + jnp.exp(-x))


def _rms(x):
    r = lax.rsqrt(jnp.mean(x * x, axis=-1, keepdims=True) + EPS)
    return x * r, r


def _rms_bwd(dxh, xh, r):
    return r * (dxh - xh * jnp.mean(dxh * xh, axis=-1, keepdims=True))


def _mm(a, b, *, mode, out_dtype, name, tm=1024, tn=1024, tk=1024, split_a=False, split_b=False, split_out=False,
        exchange=None, epilogue=None):
    def dims(arr, split):
        if split:
            return arr.shape[1], 2 * arr.shape[2]
        return arr.shape

    ar, ac = dims(a, split_a)
    br, bc = dims(b, split_b)
    if mode == "nn":
        M, K, N = ar, ac, bc
        assert br == K
    elif mode == "nt":
        M, K, N = ar, ac, br
        assert bc == K
    else:
        K, M, N = ar, ac, bc
        assert br == K
    a_cols_half = ac // 2 if split_a else None
    b_cols_half = bc // 2 if split_b else None
    tm = _tile(M, tm)
    tn = _tile((N // 2) if (split_out or (split_b and mode != "nt")) else N, tn)
    tk = _tile((K // 2) if ((split_a and mode != "tn") or (split_b and mode == "nt")) else K, tk)
    if split_a and mode == "tn":
        tm = _tile(M // 2, tm)
    gm, gn, gk = M // tm, N // tn, K // tk
    a_bytes, b_bytes = a.size * a.dtype.itemsize, b.size * b.dtype.itemsize
    rows_outer = gk > 1 or a_bytes + gm * b_bytes <= gn * a_bytes + b_bytes
    grid = (gm, gn, gk) if rows_outer else (gn, gm, gk)

    def spec(split, half, blk, rc):
        def imap(p, q, k):
            r, c = rc(*((p, q) if rows_outer else (q, p)), k)
            if not split:
                return (r, c)
            per_half = half // blk[1]
            return (c // per_half, r, c % per_half)

        return pl.BlockSpec(((None,) + blk) if split else blk, imap)

    if mode == "nn":
        a_spec = spec(split_a, a_cols_half, (tm, tk), lambda i, j, k: (i, k))
        b_spec = spec(split_b, b_cols_half, (tk, tn), lambda i, j, k: (k, j))
        dn = None
    elif mode == "nt":
        a_spec = spec(split_a, a_cols_half, (tm, tk), lambda i, j, k: (i, k))
        b_spec = spec(split_b, b_cols_half, (tn, tk), lambda i, j, k: (j, k))
        dn = NT
    else:
        a_spec = spec(split_a, a_cols_half, (tk, tm), lambda i, j, k: (k, i))
        b_spec = spec(split_b, b_cols_half, (tk, tn), lambda i, j, k: (k, j))
        dn = TN
    o_spec = spec(split_out, N // 2 if split_out else None, (tm, tn), lambda i, j, k: (i, j))
    out_shape = (2, M, N // 2) if split_out else (M, N)

    in_specs, out_specs, args = [a_spec, b_spec], [o_spec], (a, b)
    out_shapes = [jax.ShapeDtypeStruct(out_shape, out_dtype)]
    semantics = ("parallel", "parallel", "arbitrary")

    def store(result, extra, outs):
        outs[0][...] = result[...].astype(outs[0].dtype)

    if epilogue is not None:
        assert gn == 1 and not split_out
        n_vec = epilogue.n_out_vecs
        row = pl.BlockSpec((tm, N), lambda p, q, k: ((p if rows_outer else q), 0))
        vec = pl.BlockSpec((1, N), lambda p, q, k: (0, 0))
        in_specs += [row] * len(epilogue.rows) + [vec] * len(epilogue.vecs)
        args += tuple(epilogue.rows) + tuple(epilogue.vecs)
        out_specs = [row] * len(epilogue.out_rows) + [vec] * n_vec
        out_shapes = ([jax.ShapeDtypeStruct((M, N), dt) for dt in epilogue.out_rows]
                      + [jax.ShapeDtypeStruct((1, N), F32)] * n_vec)
        semantics = ("arbitrary",) * 3

        def store(result, extra, outs):
            n_rows, n_out_rows, sub = len(epilogue.rows), len(epilogue.out_rows), min(EPILOGUE_ROWS, tm)
            for r in range(0, tm, sub):
                rows = pl.ds(r, sub)
                epilogue.fn(result[r:r + sub], *[ref.at[rows] for ref in extra[:n_rows]], *extra[n_rows:],
                            *[ref.at[rows] for ref in outs[:n_out_rows]], *outs[n_out_rows:])

    n_extra = len(in_specs) - 2
    n_out = len(out_specs)

    def body(a_ref, b_ref, *refs):
        extra, outs, scratch_refs = refs[:n_extra], refs[n_extra:n_extra + n_out], refs[n_extra + n_out:]
        k = pl.program_id(2)
        if epilogue is not None:
            @pl.when((pl.program_id(0) == 0) & (pl.program_id(1) == 0) & (k == 0))
            def _():
                for ref in outs[n_out - epilogue.n_out_vecs:]:
                    ref[...] = jnp.zeros_like(ref)

        if gk == 1:
            store(_dot(_bf(a_ref[...]), _bf(b_ref[...]), dn), extra, outs)
            return
        acc_ref = scratch_refs[0]

        @pl.when(k == 0)
        def _():
            acc_ref[...] = jnp.zeros_like(acc_ref)

        acc_ref[...] += _dot(_bf(a_ref[...]), _bf(b_ref[...]), dn)

        @pl.when(k == gk - 1)
        def _():
            store(acc_ref, extra, outs)

    out = _hosted_call(
        body, name=name, grid=grid, in_specs=in_specs, out_specs=out_specs, out_shape=out_shapes,
        scratch=[] if gk == 1 else [pltpu.VMEM((tm, tn), F32)], args=args, semantics=semantics, exchange=exchange)
    return out[0] if (exchange is None and epilogue is None) else out


ROWS = 512


def _row_spec(tr, cols):
    return pl.BlockSpec((tr, cols), lambda i: (i, 0))


def _vec_spec(cols):
    return pl.BlockSpec((1, cols), lambda i: (0, 0))


def _norm_fwd(x, g, name, exchange=None):
    T, Dm = x.shape
    tr = min(ROWS, T)

    def body(x_ref, g_ref, h_ref):
        xh, _ = _rms(x_ref[...])
        h_ref[...] = (xh * g_ref[...]).astype(h_ref.dtype)

    out = _hosted_call(
        body, name=name, grid=(T // tr,), in_specs=[_row_spec(tr, Dm), _vec_spec(Dm)], out_specs=[_row_spec(tr, Dm)],
        out_shape=[jax.ShapeDtypeStruct((T, Dm), BF16)], scratch=[], args=(x, g), semantics=("parallel",),
        exchange=exchange)
    return out[0] if exchange is None else out


def _post_pre(x, g_post, g_pre):
    def fn(m, x_ref, gp_ref, gn_ref, xo_ref, h_ref, m_ref):
        mh, _ = _rms(m)
        xn = x_ref[...] + mh * gp_ref[...]
        xo_ref[...] = xn
        xh, _ = _rms(xn)
        h_ref[...] = (xh * gn_ref[...]).astype(h_ref.dtype)
        m_ref[...] = m.astype(m_ref.dtype)

    return _RowEpilogue(fn, [x], [g_post, g_pre], [F32, BF16, BF16], 0)


def _final(x2, target, g_post):
    def fn(y, x_ref, t_ref, g_ref, dx_ref, dy_ref, loss_ref, dg_ref):
        g = g_ref[...]
        yh, r = _rms(y)
        d = x_ref[...] + yh * g - t_ref[...]
        loss_ref[...] += 0.5 * jnp.sum(jnp.mean(d * d, axis=-1, keepdims=True))
        dx = d * (1.0 / d.shape[-1])
        dx_ref[...] = dx
        dy_ref[...] = _rms_bwd(dx * g, yh, r).astype(dy_ref.dtype)
        dg_ref[...] += jnp.sum(dx * yh, axis=0, keepdims=True)

    return _RowEpilogue(fn, [x2, target], [g_post], [F32, BF16], 2)


class _RowEpilogue:
    def __init__(self, fn, rows, vecs, out_rows, n_out_vecs):
        self.fn, self.rows, self.vecs, self.out_rows, self.n_out_vecs = fn, rows, vecs, out_rows, n_out_vecs


def _norm_bwd2(dx_cur, x_prev, m_prev, g_pre, g_post):
    def fn(dh, dx_ref, x_ref, m_ref, gn_ref, gp_ref, dxo_ref, dm_ref, dgn_ref, dgp_ref):
        xh, r = _rms(x_ref[...])
        dx = dx_ref[...] + _rms_bwd(dh * gn_ref[...], xh, r)
        dxo_ref[...] = dx
        dgn_ref[...] += jnp.sum(dh * xh, axis=0, keepdims=True)
        mh, rm = _rms(m_ref[...].astype(F32))
        dm_ref[...] = _rms_bwd(dx * gp_ref[...], mh, rm).astype(dm_ref.dtype)
        dgp_ref[...] += jnp.sum(dx * mh, axis=0, keepdims=True)

    return _RowEpilogue(fn, [dx_cur, x_prev, m_prev], [g_pre, g_post], [F32, BF16], 2)


def _norm_bwd1(dx_cur, x_prev, g_pre):
    def fn(dh, dx_ref, x_ref, gn_ref, dxo_ref, dgn_ref):
        xh, r = _rms(x_ref[...])
        dxo_ref[...] = dx_ref[...] + _rms_bwd(dh * gn_ref[...], xh, r)
        dgn_ref[...] += jnp.sum(dh * xh, axis=0, keepdims=True)

    return _RowEpilogue(fn, [dx_cur, x_prev], [g_pre], [F32], 1)


def _gain_bwd(x, dh_a, dh_b, name):
    T, Dm = x.shape

    def body(x_ref, a_ref, b_ref, dg_ref):
        xh, _ = _rms(x_ref[...])
        dg_ref[...] = jnp.sum((a_ref[...] + b_ref[...]) * xh, axis=0, keepdims=True)

    return pl.pallas_call(
        body, name=name, grid=(1,), in_specs=[_row_spec(T, Dm)] * 3, out_specs=_vec_spec(Dm),
        out_shape=jax.ShapeDtypeStruct((1, Dm), F32), compiler_params=_params("arbitrary"),
    )(x, dh_a, dh_b)


ATTN_GROUP = ATTN_Q_HEADS // ATTN_KV_HEADS
ASSEMBLE_ROWS = 1024


def _swa_mask(n):
    rows = ATTN_GROUP * ATTN_BLOCK
    row = lax.broadcasted_iota(jnp.int32, (rows, 2 * ATTN_BLOCK), 0) & (ATTN_BLOCK - 1)
    col = lax.broadcasted_iota(jnp.int32, (rows, 2 * ATTN_BLOCK), 1)
    diff = row + ATTN_BLOCK - col
    return (diff >= 0) & (diff < ATTN_BLOCK) & ((col >= ATTN_BLOCK) | (n > 0))


def _swa_rows(ref, hk, dtype):
    hd = ATTN_HEAD_DIM
    return jnp.concatenate(
        [ref[:, hd * (hk * ATTN_GROUP + g):hd * (hk * ATTN_GROUP + g + 1)].astype(dtype) for g in range(ATTN_GROUP)],
        axis=0)


def _swa_per_row(vals):
    seg = lax.broadcasted_iota(jnp.int32, (ATTN_GROUP * ATTN_BLOCK, 1), 0) // ATTN_BLOCK
    col = jnp.zeros((ATTN_GROUP * ATTN_BLOCK, 1), F32)
    for g, val in enumerate(vals):
        col = jnp.where(seg == g, val, col)
    return col


def _swa_specs():
    blk = ATTN_BLOCK
    prev = lambda n: jnp.maximum(n - 1, 0)
    return [
        pl.BlockSpec(memory_space=pltpu.SMEM),
        pl.BlockSpec((blk, 512), lambda n: (n, 0)),
        pl.BlockSpec((blk, 128), lambda n: (prev(n), 4)),
        pl.BlockSpec((blk, 128), lambda n: (n, 4)),
        pl.BlockSpec((blk, 128), lambda n: (prev(n), 5)),
        pl.BlockSpec((blk, 128), lambda n: (n, 5)),
    ]


def _swa_fwd(z, sinks, name, exchange=None):
    T = z.shape[0]
    blk, hd = ATTN_BLOCK, ATTN_HEAD_DIM
    scale = hd ** -0.5

    def body(sink_ref, q_ref, kp_ref, kc_ref, vp_ref, vc_ref, o_ref, lse_ref):
        allowed = _swa_mask(pl.program_id(0))
        hks = range(ATTN_KV_HEADS)
        kss = [slice(hd * hk, hd * hk + hd) for hk in hks]
        k = [_bf(jnp.concatenate([kp_ref[:, ks], kc_ref[:, ks]], axis=0)) for ks in kss]
        v = [_bf(jnp.concatenate([vp_ref[:, ks], vc_ref[:, ks]], axis=0)) for ks in kss]
        s = [jnp.where(allowed, _dot(_swa_rows(q_ref, hk, BF16), k[hk], NT) * scale, -1e30) for hk in hks]
        sink = [_swa_per_row([sink_ref[0, hk * ATTN_GROUP + g] for g in range(ATTN_GROUP)]) for hk in hks]
        m = [jnp.maximum(jnp.max(s[hk], axis=-1, keepdims=True), sink[hk]) for hk in hks]
        p = [jnp.exp(s[hk] - m[hk]) for hk in hks]
        l = [jnp.sum(p[hk], axis=-1, keepdims=True) + jnp.exp(sink[hk] - m[hk]) for hk in hks]
        o = [_dot(_bf(p[hk] / l[hk]), v[hk]).astype(o_ref.dtype) for hk in hks]
        for hk in hks:
            lse = m[hk] + jnp.log(l[hk])
            for g in range(ATTN_GROUP):
                h = hk * ATTN_GROUP + g
                o_ref[:, hd * h:hd * (h + 1)] = o[hk][blk * g:blk * (g + 1)]
                lse_ref[:, h:h + 1] = lse[blk * g:blk * (g + 1)]

    return _hosted_call(
        body, name=name, grid=(T // blk,), in_specs=_swa_specs(),
        out_specs=[pl.BlockSpec((blk, 512), lambda n: (n, 0)), pl.BlockSpec((blk, ATTN_Q_HEADS), lambda n: (n, 0))],
        out_shape=[jax.ShapeDtypeStruct((T, 512), BF16), jax.ShapeDtypeStruct((T, ATTN_Q_HEADS), F32)],
        scratch=[], args=(sinks, z, z, z, z, z), semantics=("parallel",), exchange=exchange)


def _swa_bwd(z, sinks, dcat, lse, name):
    T = z.shape[0]
    blk, hd = ATTN_BLOCK, ATTN_HEAD_DIM
    scale = hd ** -0.5
    group = ATTN_Q_HEADS // ATTN_KV_HEADS

    def body(sink_ref, q_ref, kp_ref, kc_ref, vp_ref, vc_ref, do_ref, lse_ref,
             dq_ref, dka_ref, dkb_ref, dva_ref, dvb_ref, dsink_ref):
        @pl.when(pl.program_id(0) == 0)
        def _():
            dsink_ref[...] = jnp.zeros_like(dsink_ref)

        allowed = _swa_mask(pl.program_id(0))
        lane = lax.broadcasted_iota(jnp.int32, (1, ATTN_Q_HEADS), 1)
        dsink = jnp.zeros((1, ATTN_Q_HEADS), F32)
        hks = range(ATTN_KV_HEADS)
        kss = [slice(hd * hk, hd * hk + hd) for hk in hks]
        k = [_bf(jnp.concatenate([kp_ref[:, ks], kc_ref[:, ks]], axis=0)) for ks in kss]
        v = [_bf(jnp.concatenate([vp_ref[:, ks], vc_ref[:, ks]], axis=0)) for ks in kss]
        qs = [_swa_rows(q_ref, hk, BF16) for hk in hks]
        dos = [_swa_rows(do_ref, hk, BF16) for hk in hks]
        lse = [jnp.concatenate([lse_ref[:, hk * group + g:hk * group + g + 1] for g in range(group)], axis=0)
               for hk in hks]
        s = [_dot(qs[hk], k[hk], NT) * scale for hk in hks]
        dp = [_dot(dos[hk], v[hk], NT) for hk in hks]
        p = [jnp.where(allowed, jnp.exp(jnp.where(allowed, s[hk], -1e30) - lse[hk]), 0.0) for hk in hks]
        delta = [jnp.sum(p[hk] * dp[hk], axis=-1, keepdims=True) for hk in hks]
        ds = [_bf(p[hk] * (dp[hk] - delta[hk]) * scale) for hk in hks]
        dq = [_dot(ds[hk], k[hk]).astype(dq_ref.dtype) for hk in hks]
        dk = [_dot(ds[hk], qs[hk], TN) for hk in hks]
        dv = [_dot(_bf(p[hk]), dos[hk], TN) for hk in hks]
        for hk in hks:
            sink = _swa_per_row([sink_ref[0, hk * group + g] for g in range(group)])
            sink_part = jnp.exp(sink - lse[hk]) * delta[hk]
            for g in range(group):
                h = hk * group + g
                dq_ref[:, hd * h:hd * (h + 1)] = dq[hk][blk * g:blk * (g + 1)]
                dsink = dsink + jnp.where(lane == h, -jnp.sum(sink_part[blk * g:blk * (g + 1)]), 0.0)
            dkb_ref[:, kss[hk]] = dk[hk][:blk]
            dka_ref[:, kss[hk]] = dk[hk][blk:]
            dvb_ref[:, kss[hk]] = dv[hk][:blk]
            dva_ref[:, kss[hk]] = dv[hk][blk:]
        dsink_ref[...] += dsink

    kv_out = pl.BlockSpec((blk, 128), lambda n: (n, 0))
    return pl.pallas_call(
        body, name=name, grid=(T // blk,),
        in_specs=_swa_specs() + [pl.BlockSpec((blk, 512), lambda n: (n, 0)),
                                 pl.BlockSpec((blk, ATTN_Q_HEADS), lambda n: (n, 0))],
        out_specs=[pl.BlockSpec((blk, 512), lambda n: (n, 0)), kv_out, kv_out, kv_out, kv_out,
                   pl.BlockSpec((1, ATTN_Q_HEADS), lambda n: (0, 0))],
        out_shape=[jax.ShapeDtypeStruct((T, 512), BF16)] + [jax.ShapeDtypeStruct((T, 128), F32)] * 4
        + [jax.ShapeDtypeStruct((1, ATTN_Q_HEADS), F32)],
        compiler_params=_params("arbitrary"),
    )(sinks, z, z, z, z, z, dcat, lse)


def _assemble_dz(dq_a, dka, dkb, dva, dvb, dqr, dfr, dir_, dgr, name):
    T = dq_a.shape[0]
    blk = ATTN_BLOCK
    rows = min(ASSEMBLE_ROWS, T)
    nb, per = T // rows, rows // blk

    def body(dq_ref, dka_ref, dkb_ref, dkn_ref, dva_ref, dvb_ref, dvn_ref, dqr_ref, dfr_ref, dir_ref, dgr_ref, o_ref):
        has_next = pl.program_id(0) < nb - 1

        def with_next(a_ref, b_ref, n_ref):
            after = jnp.where(has_next, n_ref[...], 0.0)
            shifted = after if per == 1 else jnp.concatenate([b_ref[blk:, :], after], axis=0)
            return (a_ref[...] + shifted).astype(o_ref.dtype)

        o_ref[:, 0:512] = dq_ref[...]
        o_ref[:, 512:640] = with_next(dka_ref, dkb_ref, dkn_ref)
        o_ref[:, 640:768] = with_next(dva_ref, dvb_ref, dvn_ref)
        o_ref[:, 768:1280] = dqr_ref[...]
        o_ref[:, 1280:1792] = dfr_ref[...]
        o_ref[:, 1792:2304] = dir_ref[...]
        o_ref[:, 2304:2816] = dgr_ref[...]

    cur = lambda w: pl.BlockSpec((rows, w), lambda n: (n, 0))
    nxt = pl.BlockSpec((blk, 128), lambda n: (jnp.minimum((n + 1) * per, T // blk - 1), 0))
    return pl.pallas_call(
        body, name=name, grid=(nb,),
        in_specs=[cur(512), cur(128), cur(128), nxt, cur(128), cur(128), nxt, cur(512), cur(512), cur(512), cur(512)],
        out_specs=pl.BlockSpec((rows, 2816), lambda n: (n, 0)),
        out_shape=jax.ShapeDtypeStruct((T, 2816), BF16), compiler_params=_params("parallel"),
    )(dq_a, dka, dkb, dkb, dva, dvb, dvb, dqr, dfr, dir_, dgr)


HGRN_ROWS = 512


def _hgrn_consts():
    c = HGRN_CHUNK
    r = lax.broadcasted_iota(jnp.int32, (c, c), 0)
    s = lax.broadcasted_iota(jnp.int32, (c, c), 1)
    rcol = lax.broadcasted_iota(jnp.int32, (c, 1), 0)
    same_block, upper = [], []
    for m in HGRN_LEVELS:
        same_block.append((r & ~(2 * m - 1)) == (s & ~(2 * m - 1)))
        upper.append((rcol & (2 * m - 1)) >= m)
    cum_mat = jnp.where(s <= r, 1.0, 0.0).astype(BF16)
    rev_mat = jnp.where(s >= r, 1.0, 0.0).astype(BF16)
    return cum_mat, rev_mat, r == s, same_block, upper, rcol & 3, s == r - 1


def _hgrn_level_decay(g, b, m, pos4):
    c = HGRN_CHUNK
    if m == 1:
        return jnp.exp(jnp.where((pos4 & 1) == 1, g, 0.0))
    if m == 2:
        after, before = pltpu.roll(g, c - 1, 0), pltpu.roll(g, 1, 0)
        return jnp.exp(jnp.where(pos4 == 0, after, jnp.where(pos4 == 1, 0.0, jnp.where(pos4 == 2, g, g + before))))
    b3 = b.reshape(c // (2 * m), 2 * m, HGRN_DIM)
    bref = jnp.broadcast_to(b3[:, m - 1:m, :], b3.shape).reshape(c, HGRN_DIM)
    return jnp.exp(-jnp.abs(b - bref))


def _split3(x):
    hi = _bf(x)
    r1 = x - hi.astype(F32)
    mid = _bf(r1)
    lo = _bf(r1 - mid.astype(F32))
    return jnp.concatenate([hi, mid, lo], axis=1)


def _dot_hilo(a, b):
    r, c = a.shape[0], b.shape[1]
    a_hi, b_hi = _bf(a), _bf(b)
    a2 = jnp.concatenate([a_hi, _bf(a - a_hi.astype(F32))], axis=0)
    b2 = jnp.concatenate([b_hi, _bf(b - b_hi.astype(F32))], axis=1)
    y = _dot(a2, b2)
    return y[:r, :c] + y[:r, c:] + y[r:, :c]


def _fold3(y):
    w = y.shape[1] // 3
    return y[:, :w] + y[:, w:2 * w] + y[:, 2 * w:]


def _hgrn_gates(qr, fr, lb):
    sq = _sigmoid(qr)
    q = qr * sq * (HGRN_DIM ** -0.5)
    sf = _sigmoid(fr)
    f = lb + (1.0 - lb) * sf
    k = (1.0 - lb) * _sigmoid(-fr)
    return q, sq, sf, f, k, jnp.log(f)


def _hgrn_intra(q, k, g, b, consts, scores=True):
    _, _, eye, same_block, upper, pos4, below = consts
    heads = range(len(q))
    a = None
    if scores:
        a = [jnp.where(eye, jnp.sum(q[hh] * k[hh], axis=1, keepdims=True), 0.0) for hh in heads]
    saved = [[] for _ in heads]
    for i, m in enumerate(HGRN_LEVELS):
        up = upper[i]
        e = [_hgrn_level_decay(g[hh], b[hh], m, pos4) for hh in heads]
        qt = [jnp.where(up, q[hh] * e[hh], 0.0) for hh in heads]
        kt = [jnp.where(up, 0.0, k[hh] * e[hh]) for hh in heads]
        for hh in heads:
            saved[hh].append((e[hh], qt[hh], kt[hh]))
        if not scores:
            continue
        if m == 1:
            for hh in heads:
                pair = jnp.sum(qt[hh] * pltpu.roll(kt[hh], 1, 0), axis=1, keepdims=True)
                a[hh] = a[hh] + jnp.where(below, pair, 0.0)
            continue
        p = [_dot(_bf(qt[hh]), _bf(kt[hh]), NT) for hh in heads]
        for hh in heads:
            a[hh] = a[hh] + jnp.where(same_block[i], p[hh], 0.0)
    return a, saved


def _hgrn_specs(tb, nb, rev):
    tmap = (lambda t: nb - 1 - t) if rev else (lambda t: t)
    assert HGRN_PAIR == HGRN_HEADS
    return [pl.BlockSpec((tb, 2816), lambda h, t: (tmap(t), 0)),
            pl.BlockSpec((1, HGRN_PAIR * HGRN_DIM), lambda h, t: (0, h)),
            pl.BlockSpec((1, HGRN_DIM), lambda h, t: (0, 0))]


def _hgrn_z(z_ref, sl, base, head):
    return z_ref[sl, base + HGRN_DIM * head:base + HGRN_DIM * (head + 1)].astype(F32)


def _hgrn_fwd(z, lb, onw, name, exchange=None):
    T = z.shape[0]
    tb = min(HGRN_ROWS, T)
    nb, c, nc = T // tb, HGRN_CHUNK, min(HGRN_ROWS, T) // HGRN_CHUNK

    def body(z_ref, lb_ref, onw_ref, rec_ref, o_ref, st_ref, a_ref, state):
        @pl.when(pl.program_id(1) == 0)
        def _():
            state[...] = jnp.zeros_like(state)

        consts = _hgrn_consts()
        lbv = lb_ref[...]
        onwv = onw_ref[...]

        def chunk(ci, carry):
            sl = pl.ds(pl.multiple_of(ci * c, c), c)
            heads = range(HGRN_PAIR)
            lss = [slice(HGRN_DIM * hh, HGRN_DIM * (hh + 1)) for hh in heads]
            gates = [_hgrn_gates(_hgrn_z(z_ref, sl, Z_Q, hh), _hgrn_z(z_ref, sl, Z_F, hh), lbv[:, lss[hh]])
                     for hh in heads]
            q, k, g = [t[0] for t in gates], [t[4] for t in gates], [t[5] for t in gates]
            v = [_bf(_hgrn_z(z_ref, sl, Z_I, hh)) for hh in heads]
            b = [_fold3(_dot(consts[0], _split3(g[hh]))) for hh in heads]
            a, _ = _hgrn_intra(q, k, g, b, consts)
            st = [state[hh] for hh in heads]
            for hh in heads:
                st_ref[hh, ci] = st[hh]
            bl = [b[hh][c - 1:c, :] for hh in heads]
            o_state = [_dot(_bf(q[hh] * jnp.exp(b[hh])), _bf(st[hh]), NT) for hh in heads]
            kv = [_dot(v[hh], _bf(k[hh] * jnp.exp(bl[hh] - b[hh])), TN) for hh in heads]
            a = [_bf(a[hh]) for hh in heads]
            o = [_dot(a[hh], v[hh]) + o_state[hh] for hh in heads]
            for hh in heads:
                a_ref[sl, c * hh:c * (hh + 1)] = a[hh]
                state[hh] = st[hh] * jnp.exp(bl[hh]) + kv[hh]
                o_ref[sl, lss[hh]] = o[hh]
                oh, _ = _rms(o[hh])
                gr = _hgrn_z(z_ref, sl, Z_G, hh)
                rec_ref[sl, lss[hh]] = (oh * onwv * (gr * _sigmoid(gr))).astype(rec_ref.dtype)
            return carry

        lax.fori_loop(0, nc, chunk, 0)

    in_specs = _hgrn_specs(tb, nb, False)
    out_blk = pl.BlockSpec((tb, HGRN_PAIR * HGRN_DIM), lambda h, t: (t, h))
    return _hosted_call(
        body, name=name, grid=(HGRN_HEADS // HGRN_PAIR, nb), in_specs=in_specs,
        out_specs=[out_blk, out_blk, pl.BlockSpec((HGRN_PAIR, nc, HGRN_DIM, HGRN_DIM), lambda h, t: (h, t, 0, 0)),
                   pl.BlockSpec((tb, HGRN_PAIR * c), lambda h, t: (t, h))],
        out_shape=[jax.ShapeDtypeStruct((T, 512), BF16), jax.ShapeDtypeStruct((T, 512), F32),
                   jax.ShapeDtypeStruct((HGRN_HEADS, T // c, HGRN_DIM, HGRN_DIM), F32),
                   jax.ShapeDtypeStruct((T, HGRN_HEADS * c), BF16)],
        scratch=[pltpu.VMEM((HGRN_PAIR, HGRN_DIM, HGRN_DIM), F32)], args=(z, lb, onw),
        semantics=("parallel", "arbitrary"), exchange=exchange)


def _hgrn_bwd(z, lb, onw, o, states, scores, dcat, name, exchange=None):
    T = z.shape[0]
    tb = min(HGRN_ROWS, T)
    nb, c, nc = T // tb, HGRN_CHUNK, min(HGRN_ROWS, T) // HGRN_CHUNK

    def body(z_ref, lb_ref, onw_ref, o_ref, st_ref, drec_ref, a_ref,
             dqr_ref, dfr_ref, dir_ref, dgr_ref, dlb_ref, donw_ref, dstate):
        @pl.when(pl.program_id(1) == 0)
        def _():
            dstate[...] = jnp.zeros_like(dstate)
            dlb_ref[...] = jnp.zeros_like(dlb_ref)

        @pl.when((pl.program_id(0) == 0) & (pl.program_id(1) == 0))
        def _():
            donw_ref[...] = jnp.zeros_like(donw_ref)

        consts = _hgrn_consts()
        rev_mat, eye, same_block, upper = consts[1:5]
        below = consts[6]
        lbv = lb_ref[...]
        onwv = onw_ref[...]
        last = lax.broadcasted_iota(jnp.int32, (c, 1), 0) == c - 1

        def chunk(i, carry):
            ci = nc - 1 - i
            sl = pl.ds(pl.multiple_of(ci * c, c), c)
            hs = range(HGRN_PAIR)
            lss = [slice(HGRN_DIM * hh, HGRN_DIM * (hh + 1)) for hh in hs]
            qr = [_hgrn_z(z_ref, sl, Z_Q, hh) for hh in hs]
            gates = [_hgrn_gates(qr[hh], _hgrn_z(z_ref, sl, Z_F, hh), lbv[:, lss[hh]]) for hh in hs]
            q, sq, sf, f, k, g = ([t[j] for t in gates] for j in range(6))
            v = [_bf(_hgrn_z(z_ref, sl, Z_I, hh)) for hh in hs]
            b = [_fold3(_dot(consts[0], _split3(g[hh]))) for hh in hs]
            _, saved = _hgrn_intra(q, k, g, b, consts, scores=False)
            a = [a_ref[sl, c * hh:c * (hh + 1)] for hh in hs]
            st = [st_ref[hh, ci] for hh in hs]
            dst = [dstate[hh] for hh in hs]

            gr = [_hgrn_z(z_ref, sl, Z_G, hh) for hh in hs]
            sg = [_sigmoid(gr[hh]) for hh in hs]
            norm = [_rms(o_ref[sl, ls]) for ls in lss]
            oh, r = [t[0] for t in norm], [t[1] for t in norm]
            drec = [drec_ref[sl, ls].astype(F32) for ls in lss]
            don = [drec[hh] * (gr[hh] * sg[hh]) for hh in hs]
            do = [_bf(_rms_bwd(don[hh] * onwv, oh[hh], r[hh])) for hh in hs]
            donw = jnp.sum(don[0] * oh[0], axis=0, keepdims=True)
            for hh in hs:
                dgr_ref[sl, lss[hh]] = (drec[hh] * oh[hh] * onwv
                                        * (sg[hh] * (1.0 + gr[hh] * (1.0 - sg[hh])))).astype(dgr_ref.dtype)
                if hh:
                    donw = donw + jnp.sum(don[hh] * oh[hh], axis=0, keepdims=True)
            donw_ref[...] += donw

            eb = [jnp.exp(b[hh]) for hh in hs]
            bl = [b[hh][c - 1:c, :] for hh in hs]
            ebl = [jnp.exp(bl[hh]) for hh in hs]
            ekb = [jnp.exp(bl[hh] - b[hh]) for hh in hs]
            qe = [q[hh] * eb[hh] for hh in hs]
            ke = [k[hh] * ekb[hh] for hh in hs]
            da = [_dot(do[hh], v[hh], NT) for hh in hs]
            dat = [_dot(v[hh], do[hh], NT) for hh in hs]
            dqe = [_dot(do[hh], _bf(st[hh])) for hh in hs]
            dke = [_dot(v[hh], _bf(dst[hh])) for hh in hs]
            dv_a = [_dot(a[hh], do[hh], TN) for hh in hs]
            dv_s = [_dot(_bf(ke[hh]), _bf(dst[hh]), NT) for hh in hs]
            dst_in = [_dot(do[hh], _bf(qe[hh]), TN) for hh in hs]
            dad = [jnp.sum(jnp.where(eye, da[hh], 0.0), axis=1, keepdims=True) for hh in hs]
            dq = [dqe[hh] * eb[hh] + dad[hh] * k[hh] for hh in hs]
            dk = [dke[hh] * ekb[hh] + dad[hh] * q[hh] for hh in hs]
            db_last = [jnp.sum(dke[hh] * ke[hh], axis=0, keepdims=True)
                       + jnp.sum(dst[hh] * st[hh], axis=0, keepdims=True) * ebl[hh] for hh in hs]
            for hh in hs:
                dstate[hh] = dst[hh] * ebl[hh] + dst_in[hh]
                dir_ref[sl, lss[hh]] = (dv_a[hh] + dv_s[hh]).astype(dir_ref.dtype)
            for lvl, m in enumerate(HGRN_LEVELS):
                if m == 1:
                    pair = [jnp.sum(jnp.where(below, da[hh], 0.0), axis=1, keepdims=True) for hh in hs]
                    xq = [pair[hh] * pltpu.roll(saved[hh][lvl][2], 1, 0) for hh in hs]
                    xk = [pltpu.roll(pair[hh] * saved[hh][lvl][1], c - 1, 0) for hh in hs]
                else:
                    xq = [_dot_hilo(jnp.where(same_block[lvl], da[hh], 0.0), saved[hh][lvl][2]) for hh in hs]
                    xk = [_dot_hilo(jnp.where(same_block[lvl], dat[hh], 0.0), saved[hh][lvl][1]) for hh in hs]
                for hh in hs:
                    e = saved[hh][lvl][0]
                    dq[hh] = dq[hh] + jnp.where(upper[lvl], xq[hh] * e, 0.0)
                    dk[hh] = dk[hh] + jnp.where(upper[lvl], 0.0, xk[hh] * e)
            db = [q[hh] * dq[hh] - k[hh] * dk[hh] + jnp.where(last, db_last[hh], 0.0) for hh in hs]
            dg = [_fold3(_dot(rev_mat, _split3(db[hh]))) for hh in hs]

            for hh in hs:
                ls = lss[hh]
                dqr_ref[sl, ls] = (dq[hh] * (HGRN_DIM ** -0.5)
                                   * (sq[hh] * (1.0 + qr[hh] * (1.0 - sq[hh])))).astype(dqr_ref.dtype)
                dfk = dg[hh] / f[hh] - dk[hh]
                dfr_ref[sl, ls] = ((1.0 - lbv[:, ls]) * sf[hh] * (1.0 - sf[hh]) * dfk).astype(dfr_ref.dtype)
                dlb_ref[:, ls] += jnp.sum((1.0 - sf[hh]) * dfk, axis=0, keepdims=True)
            return carry

        lax.fori_loop(0, nc, chunk, 0)

    in_specs = _hgrn_specs(tb, nb, True)
    rblk = pl.BlockSpec((tb, HGRN_PAIR * HGRN_DIM), lambda h, t: (nb - 1 - t, h))
    in_specs = in_specs + [
        rblk,
        pl.BlockSpec((HGRN_PAIR, nc, HGRN_DIM, HGRN_DIM), lambda h, t: (h, nb - 1 - t, 0, 0)),
        pl.BlockSpec((tb, HGRN_PAIR * HGRN_DIM), lambda h, t: (nb - 1 - t, 4 // HGRN_PAIR + h)),
        pl.BlockSpec((tb, HGRN_PAIR * c), lambda h, t: (nb - 1 - t, h)),
    ]
    return _hosted_call(
        body, name=name, grid=(HGRN_HEADS // HGRN_PAIR, nb), in_specs=in_specs,
        out_specs=[rblk, rblk, rblk, rblk, pl.BlockSpec((1, HGRN_PAIR * HGRN_DIM), lambda h, t: (0, h)),
                   pl.BlockSpec((1, HGRN_DIM), lambda h, t: (0, 0))],
        out_shape=[jax.ShapeDtypeStruct((T, 512), BF16)] * 4
        + [jax.ShapeDtypeStruct((1, 512), F32), jax.ShapeDtypeStruct((1, HGRN_DIM), F32)],
        scratch=[pltpu.VMEM((HGRN_PAIR, HGRN_DIM, HGRN_DIM), F32)], args=(z, lb, onw, o, states, dcat, scores),
        semantics=("arbitrary", "arbitrary"), exchange=exchange)


def _lower_bound(logits, name):
    def body(l_ref, lb_ref):
        l0, l1 = l_ref[0:1, :], l_ref[1:2, :]
        m = jnp.maximum(l0, l1)
        e0, e1 = jnp.exp(l0 - m), jnp.exp(l1 - m)
        lb_ref[...] = e0 / (e0 + e1)

    return pl.pallas_call(
        body, name=name, out_shape=jax.ShapeDtypeStruct((1, logits.shape[1]), F32),
    )(logits)


def _lower_bound_bwd(lb, dlb, name):
    def body(lb_ref, dlb_ref, dl_ref):
        p = lb_ref[...]
        d0 = dlb_ref[...] * p * (1.0 - p)
        dl_ref[0:1, :] = d0
        dl_ref[1:2, :] = -d0

    return pl.pallas_call(
        body, name=name, out_shape=jax.ShapeDtypeStruct((2, lb.shape[1]), F32),
    )(lb, dlb)


CA_ROWS = 1024


def _ca_fwd(q, k, v, name):
    T, W = q.shape
    M = k.shape[0]
    tq = min(CA_ROWS, T)
    scale = CA_HEAD_DIM ** -0.5

    def body(q_ref, k_ref, v_ref, o_ref):
        hss = [slice(CA_HEAD_DIM * h, CA_HEAD_DIM * (h + 1)) for h in range(CA_HEADS)]
        s = [_dot(q_ref[:, hs], k_ref[:, hs], NT) * scale for hs in hss]
        p = [jnp.exp(sh - jnp.max(sh, axis=-1, keepdims=True)) for sh in s]
        p = [ph / jnp.sum(ph, axis=-1, keepdims=True) for ph in p]
        o = [_dot(_bf(ph), v_ref[:, hs]) for ph, hs in zip(p, hss)]
        for oh, hs in zip(o, hss):
            o_ref[:, hs] = oh.astype(o_ref.dtype)

    full = pl.BlockSpec((M, W), lambda i: (0, 0))
    return pl.pallas_call(
        body, name=name, grid=(T // tq,), in_specs=[_row_spec(tq, W), full, full], out_specs=_row_spec(tq, W),
        out_shape=jax.ShapeDtypeStruct((T, W), BF16), compiler_params=_params("parallel"),
    )(q, k, v)


def _ca_bwd(q, k, v, do, name):
    T, W = q.shape
    M = k.shape[0]
    tq = min(CA_ROWS, T)
    scale = CA_HEAD_DIM ** -0.5

    def body(q_ref, k_ref, v_ref, do_ref, dq_ref, dk_ref, dv_ref):
        @pl.when(pl.program_id(0) == 0)
        def _():
            dk_ref[...] = jnp.zeros_like(dk_ref)
            dv_ref[...] = jnp.zeros_like(dv_ref)

        heads = range(CA_HEADS)
        hss = [slice(CA_HEAD_DIM * h, CA_HEAD_DIM * (h + 1)) for h in heads]
        qh, kh = [q_ref[:, hs] for hs in hss], [k_ref[:, hs] for hs in hss]
        vh, doh = [v_ref[:, hs] for hs in hss], [do_ref[:, hs] for hs in hss]
        s = [_dot(qh[h], kh[h], NT) * scale for h in heads]
        dp = [_dot(doh[h], vh[h], NT) for h in heads]
        p = [jnp.exp(s[h] - jnp.max(s[h], axis=-1, keepdims=True)) for h in heads]
        p = [p[h] / jnp.sum(p[h], axis=-1, keepdims=True) for h in heads]
        ds = [_bf(p[h] * (dp[h] - jnp.sum(p[h] * dp[h], axis=-1, keepdims=True)) * scale) for h in heads]
        dq = [_dot(ds[h], kh[h]) for h in heads]
        dk = [_dot(ds[h], qh[h], TN) for h in heads]
        dv = [_dot(_bf(p[h]), doh[h], TN) for h in heads]
        for h in heads:
            dq_ref[:, hss[h]] = dq[h].astype(dq_ref.dtype)
            dk_ref[:, hss[h]] += dk[h]
            dv_ref[:, hss[h]] += dv[h]

    full = pl.BlockSpec((M, W), lambda i: (0, 0))
    return pl.pallas_call(
        body, name=name, grid=(T // tq,), in_specs=[_row_spec(tq, W), full, full, _row_spec(tq, W)],
        out_specs=[_row_spec(tq, W), full, full],
        out_shape=[jax.ShapeDtypeStruct((T, W), BF16), jax.ShapeDtypeStruct((M, W), F32), jax.ShapeDtypeStruct((M, W), F32)],
        compiler_params=_params("arbitrary"),
    )(q, k, v, do)


FFN_ROWS = 512
FFN_COLS = 1408
GELU_C0 = 0.7978845608028654
GELU_C1 = 0.044715


def _gelu(x):
    t = jnp.tanh(GELU_C0 * (x + GELU_C1 * x * x * x))
    return 0.5 * x * (1.0 + t), t


def _gelu_grad(x, t):
    return 0.5 * (1.0 + t) + 0.5 * x * (1.0 - t * t) * GELU_C0 * (1.0 + 3.0 * GELU_C1 * x * x)


def _shift_down(cur, halo, first, tb):
    row = lax.broadcasted_iota(jnp.int32, (tb, 1), 0)
    h6 = jnp.where(first, 0.0, halo[6:7])
    h7 = jnp.where(first, 0.0, halo[7:8])
    u1 = jnp.where(row == 0, h7, pltpu.roll(cur, 1, 0))
    u2 = jnp.where(row == 0, h6, jnp.where(row == 1, h7, pltpu.roll(cur, 2, 0)))
    return u1, u2


def _conv(u_ref, halo_ref, w_ref, b_ref, half, first, tb):
    cur = u_ref[half]
    u1, u2 = _shift_down(cur, halo_ref[half], first, tb)
    w = w_ref[...]
    return w[0:1] * u2 + w[1:2] * u1 + w[2:3] * cur + b_ref[...], cur, u1, u2


def _ffn_specs(tb, tc, rows_first):
    nj = D_FF // tc
    rc = (lambda a, b: (a, b)) if rows_first else (lambda a, b: (b, a))
    def at(f):
        return lambda a, b: f(*rc(a, b))
    blk = pl.BlockSpec((2, tb, tc), at(lambda t, j: (0, t, j)))
    halo = pl.BlockSpec((2, 8, tc), at(lambda t, j: (0, jnp.maximum(t * (tb // 8) - 1, 0), j)))
    wg = pl.BlockSpec((3, tc), at(lambda t, j: (0, j)))
    wv = pl.BlockSpec((3, tc), at(lambda t, j: (0, j + nj)))
    bg = pl.BlockSpec((1, tc), at(lambda t, j: (0, j)))
    bv = pl.BlockSpec((1, tc), at(lambda t, j: (0, j + nj)))
    flat = pl.BlockSpec((tb, tc), at(lambda t, j: (t, j)))
    return blk, halo, wg, wv, bg, bv, flat


def _glu_fwd(u, cw, cb, name):
    T = u.shape[1]
    tb, tc = min(FFN_ROWS, T), FFN_COLS

    def body(u_ref, halo_ref, wg_ref, wv_ref, bg_ref, bv_ref, a_ref):
        first = pl.program_id(0) == 0
        cg = _conv(u_ref, halo_ref, wg_ref, bg_ref, 0, first, tb)[0]
        cv = _conv(u_ref, halo_ref, wv_ref, bv_ref, 1, first, tb)[0]
        a_ref[...] = (_gelu(cg)[0] * cv).astype(a_ref.dtype)

    blk, halo, wg, wv, bg, bv, flat = _ffn_specs(tb, tc, True)
    return pl.pallas_call(
        body, name=name, grid=(T // tb, D_FF // tc), in_specs=[blk, halo, wg, wv, bg, bv], out_specs=flat,
        out_shape=jax.ShapeDtypeStruct((T, D_FF), BF16), compiler_params=_params("parallel", "parallel"),
    )(u, u, cw, cw, cb, cb)


def _glu_bwd(u, cw, cb, da, name, exchange=None):
    T = u.shape[1]
    tb, tc = min(FFN_ROWS, T), FFN_COLS

    def body(u_ref, halo_ref, wg_ref, wv_ref, bg_ref, bv_ref, da_ref, dc_ref, db_ref, dw_ref):
        first = pl.program_id(1) == 0

        @pl.when(first)
        def _():
            db_ref[...] = jnp.zeros_like(db_ref)
            dw_ref[...] = jnp.zeros_like(dw_ref)

        cg, ug, ug1, ug2 = _conv(u_ref, halo_ref, wg_ref, bg_ref, 0, first, tb)
        cv, uv, uv1, uv2 = _conv(u_ref, halo_ref, wv_ref, bv_ref, 1, first, tb)
        da = da_ref[...]
        gl, t = _gelu(cg)
        dcg = da * cv * _gelu_grad(cg, t)
        dcv = da * gl
        dc_ref[0] = dcg
        dc_ref[1] = dcv
        for half, dc, taps in ((0, dcg, (ug2, ug1, ug)), (1, dcv, (uv2, uv1, uv))):
            db_ref[half] += jnp.sum(dc, axis=0, keepdims=True)
            for tap in range(3):
                dw_ref[half, tap:tap + 1, :] += jnp.sum(dc * taps[tap], axis=0, keepdims=True)

    blk, halo, wg, wv, bg, bv, flat = _ffn_specs(tb, tc, False)
    return _hosted_call(
        body, name=name, grid=(D_FF // tc, T // tb), in_specs=[blk, halo, wg, wv, bg, bv, flat],
        out_specs=[blk, pl.BlockSpec((2, 1, tc), lambda j, t: (0, 0, j)), pl.BlockSpec((2, 3, tc), lambda j, t: (0, 0, j))],
        out_shape=[jax.ShapeDtypeStruct((2, T, D_FF), F32), jax.ShapeDtypeStruct((2, 1, D_FF), F32),
                   jax.ShapeDtypeStruct((2, 3, D_FF), F32)],
        scratch=[], args=(u, u, cw, cw, cb, cb, da), semantics=("parallel", "arbitrary"), exchange=exchange)


def _conv_bwd(dc, cw, name):
    T = dc.shape[1]
    tb, tc = min(FFN_ROWS, T), FFN_COLS
    nt, nj = T // tb, D_FF // tc

    def body(dc_ref, halo_ref, wg_ref, wv_ref, du_ref):
        last = pl.program_id(0) == nt - 1
        row = lax.broadcasted_iota(jnp.int32, (tb, 1), 0)
        for half, w_ref in ((0, wg_ref), (1, wv_ref)):
            cur = dc_ref[half]
            halo = halo_ref[half]
            h0 = jnp.where(last, 0.0, halo[0:1])
            h1 = jnp.where(last, 0.0, halo[1:2])
            d1 = jnp.where(row == tb - 1, h0, pltpu.roll(cur, tb - 1, 0))
            d2 = jnp.where(row == tb - 1, h1, jnp.where(row == tb - 2, h0, pltpu.roll(cur, tb - 2, 0)))
            w = w_ref[...]
            du_ref[half] = (w[2:3] * cur + w[1:2] * d1 + w[0:1] * d2).astype(du_ref.dtype)

    blk = pl.BlockSpec((2, tb, tc), lambda t, j: (0, t, j))
    halo = pl.BlockSpec((2, 8, tc), lambda t, j: (0, jnp.minimum((t + 1) * (tb // 8), T // 8 - 1), j))
    wg = pl.BlockSpec((3, tc), lambda t, j: (0, j))
    wv = pl.BlockSpec((3, tc), lambda t, j: (0, j + nj))
    return pl.pallas_call(
        body, name=name, grid=(nt, nj), in_specs=[blk, halo, wg, wv], out_specs=blk,
        out_shape=jax.ShapeDtypeStruct((2, T, D_FF), BF16), compiler_params=_params("parallel", "parallel"),
    )(dc, dc, cw, cw)


def _mesh_pos():
    return lax.axis_index("x"), lax.axis_index("y"), lax.axis_index("c")


def _peer(pos, k):
    return (pos[0] ^ ((k >> 2) & 1), pos[1] ^ ((k >> 1) & 1), pos[2] ^ (k & 1))


def _index(pos):
    return 4 * pos[0] + 2 * pos[1] + pos[2]


class _Exchange:
    def __init__(self, kind, buf, relay=False):
        assert kind in ("gather", "scatter") and not (relay and kind == "scatter")
        self.kind, self.buf, self.relay = kind, buf, relay
        self.out_shape = jax.ShapeDtypeStruct(((N_DEV,) + buf.shape) if kind == "gather" else buf.shape, buf.dtype)
        self.spec = pl.BlockSpec(memory_space=pl.ANY)
        self.scratch = [pltpu.SemaphoreType.DMA((N_DEV - 1,)), pltpu.SemaphoreType.DMA((N_DEV - 1,)),
                        pltpu.SemaphoreType.DMA]

    def _src(self, x_ref, dest):
        return x_ref if self.kind == "gather" else x_ref.at[dest]

    def _copies(self, x_ref, out_ref, send_sems, recv_sems, local_sem):
        pos = _mesh_pos()
        me = _index(pos)
        local = pltpu.make_async_copy(self._src(x_ref, me), out_ref.at[me], local_sem)
        sends, recvs = [], []
        for k in range(1, N_DEV):
            peer = _peer(pos, k)
            sends.append(pltpu.make_async_remote_copy(
                src_ref=self._src(x_ref, _index(peer)), dst_ref=out_ref.at[me], send_sem=send_sems.at[k - 1],
                recv_sem=recv_sems.at[k - 1], device_id=peer, device_id_type=pl.DeviceIdType.MESH))
            recvs.append(pltpu.make_async_remote_copy(
                src_ref=self._src(x_ref, me), dst_ref=out_ref.at[_index(peer)], send_sem=send_sems.at[k - 1],
                recv_sem=recv_sems.at[k - 1], device_id=peer, device_id_type=pl.DeviceIdType.MESH))
        return local, sends, recvs

    def _relay_copies(self, x_ref, out_ref, send_sems, recv_sems, local_sem):
        x, y, c = _mesh_pos()
        me, sibling = (x, y, c), (x, y, 1 - c)
        chips = [(1 - x, y), (x, 1 - y), (1 - x, 1 - y)]

        def copy(k, block, to, own=False):
            return pltpu.make_async_remote_copy(
                src_ref=x_ref if own else out_ref.at[_index(block)], dst_ref=out_ref.at[_index(block)],
                send_sem=send_sems.at[k], recv_sem=recv_sems.at[k], device_id=to, device_id_type=pl.DeviceIdType.MESH)

        local = pltpu.make_async_copy(x_ref, out_ref.at[_index(me)], local_sem)
        first = [copy(0, me, sibling, own=True)] + [copy(1 + j, me, (*chip, c), own=True) for j, chip in enumerate(chips)]
        landed = [copy(1 + j, (*chip, c), me) for j, chip in enumerate(chips)]
        passed = [copy(4 + j, (*chip, c), sibling) for j, chip in enumerate(chips)]
        from_sibling = [copy(0, sibling, me)] + [copy(4 + j, (*chip, 1 - c), me) for j, chip in enumerate(chips)]
        return local, first, landed, passed, from_sibling

    def start(self, *refs):
        if self.relay:
            local, first = self._relay_copies(*refs)[:2]
            local.start()
            for cp in first:
                cp.start()
            return
        local, sends, _ = self._copies(*refs)
        local.start()
        for cp in sends:
            cp.start()

    def finish(self, *refs):
        if self.relay:
            local, first, landed, passed, from_sibling = self._relay_copies(*refs)
            for got, forward in zip(landed, passed):
                got.wait_recv()
                forward.start()
            for cp in from_sibling:
                cp.wait_recv()
            for cp in first + passed:
                cp.wait_send()
            local.wait()
            return
        local, sends, recvs = self._copies(*refs)
        for cp in recvs:
            cp.wait_recv()
        for cp in sends:
            cp.wait_send()
        local.wait()


def _hosted_call(body, *, name, grid, in_specs, out_specs, out_shape, scratch, args, semantics, exchange=None):
    if exchange is None:
        return pl.pallas_call(
            body, name=name, grid=grid, in_specs=in_specs, out_specs=out_specs, out_shape=out_shape,
            scratch_shapes=scratch, compiler_params=_params(*semantics))(*args)
    n_in, n_out, n_scr = len(in_specs), len(out_specs), len(scratch)

    def hosted(*refs):
        ins, x_ref = refs[:n_in], refs[n_in]
        outs, land_ref = refs[n_in + 1:n_in + 1 + n_out], refs[n_in + 1 + n_out]
        rest = refs[n_in + n_out + 2:]
        sems = rest[n_scr:]
        ids = [pl.program_id(a) for a in range(len(grid))]
        first, last = ids[0] == 0, ids[0] == grid[0] - 1
        for a in range(1, len(grid)):
            first, last = first & (ids[a] == 0), last & (ids[a] == grid[a] - 1)

        @pl.when(first)
        def _():
            exchange.start(x_ref, land_ref, *sems)

        body(*ins, *outs, *rest[:n_scr])

        @pl.when(last)
        def _():
            exchange.finish(x_ref, land_ref, *sems)

    return pl.pallas_call(
        hosted, name=name, grid=grid, in_specs=list(in_specs) + [exchange.spec],
        out_specs=list(out_specs) + [exchange.spec], out_shape=list(out_shape) + [exchange.out_shape],
        scratch_shapes=list(scratch) + exchange.scratch, compiler_params=_params(*(["arbitrary"] * len(grid))),
    )(*args, exchange.buf)


def _exchange_alone(exchange, name):
    def body(x_ref, out_ref, send_sems, recv_sems, local_sem):
        exchange.start(x_ref, out_ref, send_sems, recv_sems, local_sem)
        exchange.finish(x_ref, out_ref, send_sems, recv_sems, local_sem)

    return pl.pallas_call(
        body, name=name, out_shape=exchange.out_shape, in_specs=[exchange.spec], out_specs=exchange.spec,
        scratch_shapes=exchange.scratch)(exchange.buf)


def _adamw(w, g, m, v):
    m = ADAM_B1 * m + (1.0 - ADAM_B1) * g
    v = ADAM_B2 * v + (1.0 - ADAM_B2) * (g * g)
    m_hat = m / (1.0 - ADAM_B1 ** ADAM_STEP)
    v_hat = v / (1.0 - ADAM_B2 ** ADAM_STEP)
    delta = -ADAM_LR * (m_hat / (jnp.sqrt(v_hat) + ADAM_EPS) + ADAM_WD * w)
    return delta, m, v


def _sum_rows(parts, r0, rows, name, wmv=None):
    C = parts.shape[2]
    tr = max(t for t in range(16, ROWS + 1, 16) if rows % t == 0 and r0 % t == 0)

    def total(p_ref):
        g = p_ref[0].astype(F32)
        for i in range(1, N_DEV):
            g = g + p_ref[i].astype(F32)
        return g

    p_spec = pl.BlockSpec((N_DEV, tr, C), lambda i: (0, r0 // tr + i, 0))
    if wmv is None:
        def body(p_ref, g_ref):
            g_ref[...] = total(p_ref)

        return pl.pallas_call(
            body, name=name, grid=(rows // tr,), in_specs=[p_spec], out_specs=_row_spec(tr, C),
            out_shape=jax.ShapeDtypeStruct((rows, C), F32), compiler_params=_params("parallel"))(parts)

    def body(p_ref, w_ref, m_ref, v_ref, g_ref, d_ref, mo_ref, vo_ref):
        g = total(p_ref)
        g_ref[0] = g
        d_ref[0], mo_ref[0], vo_ref[0] = _adamw(w_ref[0], g, m_ref[0], v_ref[0])

    blk = pl.BlockSpec((1, tr, C), lambda i: (0, i, 0))
    return pl.pallas_call(
        body, name=name, grid=(rows // tr,), in_specs=[p_spec, blk, blk, blk], out_specs=[blk] * 4,
        out_shape=[jax.ShapeDtypeStruct((1, rows, C), F32)] * 4, compiler_params=_params("parallel"))(parts, *wmv)


def _sum_parts(parts, name):
    _, R, C = parts.shape

    def body(p_ref, g_ref):
        g = p_ref[0]
        for i in range(1, N_DEV):
            g = g + p_ref[i]
        g_ref[...] = g

    return pl.pallas_call(body, name=name, out_shape=jax.ShapeDtypeStruct((R, C), F32))(parts)


def _adamw_call(w, g, m, v, name):
    _, R, C = w.shape
    tr = min(ROWS, R)

    def body(w_ref, g_ref, m_ref, v_ref, d_ref, mo_ref, vo_ref):
        d_ref[...], mo_ref[...], vo_ref[...] = _adamw(w_ref[...], g_ref[...], m_ref[...], v_ref[...])

    blk = pl.BlockSpec((1, tr, C), lambda i: (0, i, 0))
    return pl.pallas_call(
        body, name=name, grid=(R // tr,), in_specs=[blk] * 4, out_specs=[blk] * 3,
        out_shape=[jax.ShapeDtypeStruct(w.shape, F32)] * 3, compiler_params=_params("parallel"))(w, g, m, v)


NORMS = ("mix_pre_norm", "mix_post_norm", "ca_pre_norm", "mem_norm", "ca_post_norm", "ffn_pre_norm", "ffn_post_norm")
SMALL = ("mix_pre_norm", "attn_sinks", "hgrn_lb_logits", "hgrn_out_norm", "mix_post_norm", "ca_pre_norm", "mem_norm",
         "ca_post_norm", "ffn_pre_norm", "ffn_conv_w", "ffn_conv_b", "ffn_post_norm")
SMALL_ROWS = 40
ROW_LOGITS, ROW_MISC, ROW_CONV_B, ROW_CONV_W = 7, 8, 9, 15
LANE_SINKS, LANE_LOSS = 128, 256
FF_PIECES = ((0, 1024), (1024, 2048), (2048, D_FF))


def _pack_small(norm_grads, dlogits, donw, dsinks, loss, d_cb, d_cw, name):
    def body(*refs):
        norm_refs = refs[:len(NORMS)]
        dl_ref, donw_ref, dsink_ref, loss_ref, cb_ref, cw_ref, out_ref = refs[len(NORMS):]
        out_ref[...] = jnp.zeros_like(out_ref)
        for i, ref in enumerate(norm_refs):
            out_ref[i:i + 1, :] = ref[...]
        out_ref[ROW_LOGITS:ROW_LOGITS + 1, 0:512] = dl_ref[0:1, :]
        out_ref[ROW_LOGITS:ROW_LOGITS + 1, 512:1024] = dl_ref[1:2, :]
        out_ref[ROW_MISC:ROW_MISC + 1, 0:HGRN_DIM] = donw_ref[...]
        out_ref[ROW_MISC:ROW_MISC + 1, LANE_SINKS:LANE_SINKS + ATTN_Q_HEADS] = dsink_ref[...]
        out_ref[ROW_MISC:ROW_MISC + 1, LANE_LOSS:LANE_LOSS + LANE] = loss_ref[...]
        for h in range(2):
            for j, (c0, c1) in enumerate(FF_PIECES):
                r = ROW_CONV_B + 3 * h + j
                out_ref[r:r + 1, 0:c1 - c0] = cb_ref[h, :, c0:c1]
                for t in range(3):
                    r = ROW_CONV_W + 3 * (3 * h + t) + j
                    out_ref[r:r + 1, 0:c1 - c0] = cw_ref[h, t:t + 1, c0:c1]

    return pl.pallas_call(
        body, name=name, out_shape=jax.ShapeDtypeStruct((SMALL_ROWS, 1024), F32),
    )(*norm_grads, dlogits, donw, dsinks, loss, d_cb, d_cw)


def _adamw_small(total, g_conv_w, w, m, v, name):
    n = len(SMALL)

    def body(*refs):
        t_ref, gcw_ref = refs[:2]
        w_refs, m_refs, v_refs = (dict(zip(SMALL, refs[2 + n * i:2 + n * (i + 1)])) for i in range(3))
        outs = refs[2 + 3 * n:]
        loss_ref = outs[0]
        g_refs, d_refs, mo_refs, vo_refs = (dict(zip(SMALL, outs[1 + n * i:1 + n * (i + 1)])) for i in range(4))
        loss_ref[...] = t_ref[ROW_MISC:ROW_MISC + 1, LANE_LOSS:LANE_LOSS + 1]

        def step(nm, idx, g):
            g_refs[nm][idx] = g
            d_refs[nm][idx], mo_refs[nm][idx], vo_refs[nm][idx] = _adamw(w_refs[nm][idx], g, m_refs[nm][idx], v_refs[nm][idx])

        everything = (slice(None), slice(None))
        for i, nm in enumerate(NORMS):
            step(nm, everything, t_ref[i:i + 1, :])
        step("hgrn_lb_logits", (slice(0, 1), slice(None)), t_ref[ROW_LOGITS:ROW_LOGITS + 1, 0:512])
        step("hgrn_lb_logits", (slice(1, 2), slice(None)), t_ref[ROW_LOGITS:ROW_LOGITS + 1, 512:1024])
        step("hgrn_out_norm", everything, t_ref[ROW_MISC:ROW_MISC + 1, 0:HGRN_DIM])
        step("attn_sinks", everything, t_ref[ROW_MISC:ROW_MISC + 1, LANE_SINKS:LANE_SINKS + ATTN_Q_HEADS])
        for h in range(2):
            for j, (c0, c1) in enumerate(FF_PIECES):
                r = ROW_CONV_B + 3 * h + j
                step("ffn_conv_b", (slice(None), slice(D_FF * h + c0, D_FF * h + c1)), t_ref[r:r + 1, 0:c1 - c0])
        step("ffn_conv_w", (slice(None), slice(None), slice(None)), gcw_ref[...])

    shapes = [jax.ShapeDtypeStruct(w[nm].shape, F32) for nm in SMALL]
    out = pl.pallas_call(
        body, name=name, out_shape=[jax.ShapeDtypeStruct((1, 1), F32)] + shapes * 4,
    )(total, g_conv_w, *[w[nm] for nm in SMALL], *[m[nm] for nm in SMALL], *[v[nm] for nm in SMALL])
    trees = [dict(zip(SMALL, out[1 + n * i:1 + n * (i + 1)])) for i in range(4)]
    return out[0], trees


BIG = ("w_in", "w_out", "ca_wq", "ca_wk", "ca_wv", "ca_wo", "ffn_w_up", "ffn_w_down")
BIG_FULL = {"w_in": (1024, 2816), "w_out": (1024, 1024), "ca_wq": (1024, 1024), "ca_wk": (1024, 1024),
            "ca_wv": (1024, 1024), "ca_wo": (1024, 1024), "ffn_w_up": (1024, 5632), "ffn_w_down": (2816, 1024)}
G_IN, G_MID, G_UP, G_DOWN = ("w_in",), ("w_out", "ca_wq", "ca_wk", "ca_wv", "ca_wo"), ("ffn_w_up",), ("ffn_w_down",)
GROUPS = (G_IN, G_MID, G_UP, G_DOWN)
COL_SHARDED = ("w_in", "ffn_w_up")
PACK_COLS = 1024


def _big_rows(name):
    r, c = BIG_FULL[name]
    return r * c // N_DEV // PACK_COLS


def _pack_shards(w, names):
    rows = [w[n][0].T if n in COL_SHARDED else w[n][0] for n in names]
    return (rows[0] if len(rows) == 1 else jnp.concatenate(rows, axis=0)).astype(BF16)


def _unpack_gathered(gathered, names):
    out, r0 = {}, 0
    for n in names:
        rows = _big_rows(n)
        out[n] = gathered[:, r0:r0 + rows].reshape(N_DEV * rows, PACK_COLS)
        r0 += rows
    return out


def _pack_full_grads(grads, names):
    parts = [grads[n].reshape(N_DEV, _big_rows(n), PACK_COLS) for n in names]
    return parts[0] if len(parts) == 1 else jnp.concatenate(parts, axis=1)


def kernel(x, mem, mix_pre_norm, w_in, attn_sinks, hgrn_lb_logits, hgrn_out_norm, w_out, mix_post_norm, ca_pre_norm, mem_norm, ca_wq, ca_wk, ca_wv, ca_wo, ca_post_norm, ffn_pre_norm, ffn_w_up, ffn_conv_w, ffn_conv_b, ffn_w_down, ffn_post_norm, loss_target, m_mix_pre_norm, m_w_in, m_attn_sinks, m_hgrn_lb_logits, m_hgrn_out_norm, m_w_out, m_mix_post_norm, m_ca_pre_norm, m_mem_norm, m_ca_wq, m_ca_wk, m_ca_wv, m_ca_wo, m_ca_post_norm, m_ffn_pre_norm, m_ffn_w_up, m_ffn_conv_w, m_ffn_conv_b, m_ffn_w_down, m_ffn_post_norm, v_mix_pre_norm, v_w_in, v_attn_sinks, v_hgrn_lb_logits, v_hgrn_out_norm, v_w_out, v_mix_post_norm, v_ca_pre_norm, v_mem_norm, v_ca_wq, v_ca_wk, v_ca_wv, v_ca_wo, v_ca_post_norm, v_ffn_pre_norm, v_ffn_w_up, v_ffn_conv_w, v_ffn_conv_b, v_ffn_w_down, v_ffn_post_norm):
    names = ["mix_pre_norm", "w_in", "attn_sinks", "hgrn_lb_logits", "hgrn_out_norm", "w_out", "mix_post_norm",
             "ca_pre_norm", "mem_norm", "ca_wq", "ca_wk", "ca_wv", "ca_wo", "ca_post_norm", "ffn_pre_norm",
             "ffn_w_up", "ffn_conv_w", "ffn_conv_b", "ffn_w_down", "ffn_post_norm"]
    w_all = dict(zip(names, [mix_pre_norm, w_in, attn_sinks, hgrn_lb_logits, hgrn_out_norm, w_out, mix_post_norm,
                             ca_pre_norm, mem_norm, ca_wq, ca_wk, ca_wv, ca_wo, ca_post_norm, ffn_pre_norm,
                             ffn_w_up, ffn_conv_w, ffn_conv_b, ffn_w_down, ffn_post_norm]))
    m_all = dict(zip(names, [m_mix_pre_norm, m_w_in, m_attn_sinks, m_hgrn_lb_logits, m_hgrn_out_norm, m_w_out,
                             m_mix_post_norm, m_ca_pre_norm, m_mem_norm, m_ca_wq, m_ca_wk, m_ca_wv, m_ca_wo,
                             m_ca_post_norm, m_ffn_pre_norm, m_ffn_w_up, m_ffn_conv_w, m_ffn_conv_b, m_ffn_w_down,
                             m_ffn_post_norm]))
    v_all = dict(zip(names, [v_mix_pre_norm, v_w_in, v_attn_sinks, v_hgrn_lb_logits, v_hgrn_out_norm, v_w_out,
                             v_mix_post_norm, v_ca_pre_norm, v_mem_norm, v_ca_wq, v_ca_wk, v_ca_wv, v_ca_wo,
                             v_ca_post_norm, v_ffn_pre_norm, v_ffn_w_up, v_ffn_conv_w, v_ffn_conv_b, v_ffn_w_down,
                             v_ffn_post_norm]))
    dev = _index(_mesh_pos())

    w_packs = {grp: _pack_shards(w_all, grp) for grp in GROUPS}
    shard_w = D_FF * 2 // N_DEV
    conv_w_rows = _exchange_alone(_Exchange("gather", ffn_conv_w[0]), "gather_conv_w")
    conv_w_full = conv_w_rows.transpose(1, 0, 2).reshape(3, 2 * D_FF)

    received, small_pack, grad_x = _local_step(
        x[0], mem[0], loss_target[0], w_packs, conv_w_full,
        {n: w_all[n] for n in NORMS}, attn_sinks, hgrn_lb_logits, hgrn_out_norm, ffn_conv_b)

    total = _sum_parts(_exchange_alone(_Exchange("gather", small_pack), "gather_small"), "sum_small")
    cw = total[ROW_CONV_W:ROW_CONV_W + 18].reshape(2, 3, 3 * PACK_COLS)[:, :, :D_FF]
    cw = cw.transpose(1, 0, 2).reshape(3, 2 * D_FF)
    g_conv_w = lax.dynamic_slice_in_dim(cw, dev * shard_w, shard_w, axis=1)[None]
    loss, (out_g, out_d, out_m, out_v) = _adamw_small(total, g_conv_w, w_all, m_all, v_all, "adamw_small")

    for grp in GROUPS:
        r0 = 0
        for n in grp:
            rows = _big_rows(n)
            if n in COL_SHARDED:
                g = _sum_rows(received[grp], r0, rows, "sum_" + n).T[None]
                d, mo, vo = _adamw_call(w_all[n], g, m_all[n], v_all[n], "adamw_" + n)
            else:
                g, d, mo, vo = _sum_rows(received[grp], r0, rows, "adamw_" + n, wmv=(w_all[n], m_all[n], v_all[n]))
            out_g[n], out_d[n], out_m[n], out_v[n] = g, d, mo, vo
            r0 += rows

    return (loss[0, 0], grad_x[None], *[out_g[n] for n in names], *[out_d[n] for n in names],
            *[out_m[n] for n in names], *[out_v[n] for n in names])


def _local_step(x, mem, target, w_packs, conv_w, norms, sinks, lb_logits, out_norm, conv_b):
    g1, g2, g3 = norms["mix_pre_norm"], norms["mix_post_norm"], norms["ca_pre_norm"]
    g4, g5, g6, g7 = norms["mem_norm"], norms["ca_post_norm"], norms["ffn_pre_norm"], norms["ffn_post_norm"]

    h1, gathered = _norm_fwd(x, g1, "mix_norm", exchange=_Exchange("gather", w_packs[G_IN], relay=True))
    w_in_t = _unpack_gathered(gathered, G_IN)["w_in"]
    up_shard = w_packs[G_UP]
    up_rows = up_shard.shape[0]
    up_cuts = (0, up_rows // 2, 3 * up_rows // 4, up_rows)
    up_parts = [up_shard[a:b] for a, b in zip(up_cuts[:-1], up_cuts[1:])]
    z, up_0 = _mm(h1, w_in_t, mode="nt", out_dtype=BF16, name="in_proj", tn=2816,
                  exchange=_Exchange("gather", up_parts[0]))
    attn, lse, gathered = _swa_fwd(z, sinks, "swa_fwd", exchange=_Exchange("gather", w_packs[G_DOWN]))
    w_down = _unpack_gathered(gathered, G_DOWN)["ffn_w_down"]
    lb = _lower_bound(lb_logits, "lower_bound")
    rec, o_rec, states, scores, gathered = _hgrn_fwd(
        z, lb, out_norm, "hgrn_fwd", exchange=_Exchange("gather", w_packs[G_MID]))
    w_out, wq, wk, wv, wo = (_unpack_gathered(gathered, G_MID)[n] for n in G_MID)
    cat = jnp.concatenate([attn, rec], axis=1)
    x1, h2, mix, up_1 = _mm(cat, w_out, mode="nn", out_dtype=BF16, name="out_proj",
                            exchange=_Exchange("gather", up_parts[1]), epilogue=_post_pre(x, g2, g3))
    mem_n = _norm_fwd(mem, g4, "mem_norm")
    q = _mm(h2, wq, mode="nn", out_dtype=BF16, name="ca_q")
    k = _mm(mem_n, wk, mode="nn", out_dtype=BF16, name="ca_k")
    v = _mm(mem_n, wv, mode="nn", out_dtype=BF16, name="ca_v")
    oc = _ca_fwd(q, k, v, "ca_fwd")
    x2, h3, c, up_2 = _mm(oc, wo, mode="nn", out_dtype=BF16, name="ca_o",
                          exchange=_Exchange("gather", up_parts[2]), epilogue=_post_pre(x1, g5, g6))
    w_up_t = jnp.concatenate([up_0, up_1, up_2], axis=1).reshape(-1, PACK_COLS)
    u = _mm(h3, w_up_t, mode="nt", out_dtype=F32, name="ffn_up", tn=2816, split_out=True)
    a = _glu_fwd(u, conv_w, conv_b, "glu_fwd")
    dx3, dy, loss_row, dg7 = _mm(a, w_down, mode="nn", out_dtype=BF16, name="ffn_down", tm=512, tk=2816,
                                 epilogue=_final(x2, target, g7))
    loss = loss_row[:, :LANE]

    da = _mm(dy, w_down, mode="nt", out_dtype=F32, name="ffn_down_dx", tn=2816)
    d_w_down = _mm(a, dy, mode="tn", out_dtype=BF16, name="ffn_down_dw", tm=2816, tk=1024)
    dc, d_cb, d_cw, got_down = _glu_bwd(
        u, conv_w, conv_b, da, "glu_bwd",
        exchange=_Exchange("scatter", _pack_full_grads({"ffn_w_down": d_w_down}, G_DOWN)))
    du = _conv_bwd(dc, conv_w, "conv_bwd")
    d_w_up_t = _mm(du, h3, mode="tn", out_dtype=BF16, name="ffn_up_dw", tm=2816, tk=1024, split_a=True)
    dx2, dcv, dg6, dg5, got_up = _mm(
        du, w_up_t, mode="nn", out_dtype=BF16, name="ffn_up_dx", tm=1024, tk=1408, split_a=True,
        exchange=_Exchange("scatter", _pack_full_grads({"ffn_w_up": d_w_up_t}, G_UP)),
        epilogue=_norm_bwd2(dx3, x2, c, g6, g5))
    doc = _mm(dcv, wo, mode="nt", out_dtype=BF16, name="ca_o_dx")
    d_wo = _mm(oc, dcv, mode="tn", out_dtype=BF16, name="ca_o_dw", tm=1024, tk=1024)
    dq, dk, dv = _ca_bwd(q, k, v, doc, "ca_bwd")
    d_wq = _mm(h2, dq, mode="tn", out_dtype=BF16, name="ca_q_dw", tm=1024, tk=1024)
    dx1, dmix, dg3, dg2 = _mm(dq, wq, mode="nt", out_dtype=BF16, name="ca_q_dx",
                              epilogue=_norm_bwd2(dx2, x1, mix, g3, g2))
    d_wk = _mm(mem_n, dk, mode="tn", out_dtype=BF16, name="ca_k_dw", tm=1024)
    d_wv = _mm(mem_n, dv, mode="tn", out_dtype=BF16, name="ca_v_dw", tm=1024)
    dmem_k = _mm(dk, wk, mode="nt", out_dtype=F32, name="ca_k_dx")
    dmem_v = _mm(dv, wv, mode="nt", out_dtype=F32, name="ca_v_dx")
    dg4 = _gain_bwd(mem, dmem_k, dmem_v, "mem_norm_bwd")
    dcat = _mm(dmix, w_out, mode="nt", out_dtype=BF16, name="out_proj_dx")
    d_w_out = _mm(cat, dmix, mode="tn", out_dtype=BF16, name="out_proj_dw", tm=1024, tk=1024)
    mid = {"w_out": d_w_out, "ca_wq": d_wq, "ca_wk": d_wk, "ca_wv": d_wv, "ca_wo": d_wo}
    dqr, dfr, dir_, dgr, dlb, donw, got_mid = _hgrn_bwd(
        z, lb, out_norm, o_rec, states, scores, dcat, "hgrn_bwd",
        exchange=_Exchange("scatter", _pack_full_grads(mid, G_MID)))
    dq_a, dka, dkb, dva, dvb, dsinks = _swa_bwd(z, sinks, dcat, lse, "swa_bwd")
    dz = _assemble_dz(dq_a, dka, dkb, dva, dvb, dqr, dfr, dir_, dgr, "assemble_dz")
    d_w_in_t = _mm(dz, h1, mode="tn", out_dtype=BF16, name="in_proj_dw", tm=2816, tk=1024)
    dx, dg1, got_in = _mm(dz, w_in_t, mode="nn", out_dtype=BF16, name="in_proj_dx", tm=512, tk=2816,
                          exchange=_Exchange("scatter", _pack_full_grads({"w_in": d_w_in_t}, G_IN)),
                          epilogue=_norm_bwd1(dx1, x, g1))

    small_pack = _pack_small(
        (dg1, dg2, dg3, dg4, dg5, dg6, dg7), _lower_bound_bwd(lb, dlb, "lower_bound_bwd"), donw, dsinks, loss,
        d_cb, d_cw, "pack_small")
    return {G_IN: got_in, G_MID: got_mid, G_UP: got_up, G_DOWN: got_down}, small_pack, dx
```
